```python
import jax, jax.numpy as jnp
from jax import lax
import numpy as np

D_MODEL = 1024
BATCH = 8
SEQ = 2048
DEPTH = 4

GRID_W = 64
N_MEM = 256
HEAD_DIM = 64
N_BRANCH = 4
BRANCH_WIDTH = D_MODEL // N_BRANCH
IN_WIDTH = 9 * BRANCH_WIDTH
RET_HEADS = BRANCH_WIDTH // HEAD_DIM
RET_CHUNK = 128
ROPE_THETA = 10000.0
POOL_WINDOWS = (2, 4, 8, 16)
POOL_GROUPS = len(POOL_WINDOWS)
POOL_GROUP_DIM = BRANCH_WIDTH // POOL_GROUPS
NA_HEADS = BRANCH_WIDTH // HEAD_DIM
NA_WIN_ROWS = 8
NA_WIN_COLS = 16
NA_QBLOCK_COLS = 16
NA_KBLOCK_COLS = 2 * NA_QBLOCK_COLS
MEM_HEADS = BRANCH_WIDTH // HEAD_DIM
FF_HIDDEN = -(-8 * D_MODEL // (3 * 256)) * 256
NEG_INF = -1e30
EPS = 1e-6

kernel_name = "hybrid_retention_pool_natten_memory_encoder"


def rms_norm(x, g):
    xf = x.astype(jnp.float32)
    y = xf * lax.rsqrt(jnp.mean(xf * xf, axis=-1, keepdims=True) + EPS)
    return (y * g.astype(jnp.float32)).astype(x.dtype)


def split_heads(t, n_heads):
    b, s, _ = t.shape
    return t.reshape(b, s, n_heads, -1).transpose(0, 2, 1, 3)


def merge_heads(t):
    b, h, s, d = t.shape
    return t.transpose(0, 2, 1, 3).reshape(b, s, h * d)


def rotary(t, pos):
    half = t.shape[-1] // 2
    inv = ROPE_THETA ** (-jnp.arange(half, dtype=jnp.float32) / half)
    ang = pos[:, None] * inv[None, :]
    cos, sin = jnp.cos(ang), jnp.sin(ang)
    tf = t.astype(jnp.float32)
    t1, t2 = tf[..., :half], tf[..., half:]
    return jnp.concatenate([t1 * cos - t2 * sin, t1 * sin + t2 * cos], axis=-1).astype(t.dtype)


def retention_dir(q, k, v, log_gamma, include_diag):
    b, h, s, d = q.shape
    c = RET_CHUNK
    n = s // c
    dt = q.dtype
    qc, kc, vc = (t.reshape(b, h, n, c, d) for t in (q, k, v))
    idx = jnp.arange(c, dtype=jnp.float32)
    diff = idx[:, None] - idx[None, :]
    mask = (diff >= 0) if include_diag else (diff > 0)
    lg = log_gamma.astype(jnp.float32)[:, None]
    d_intra = jnp.where(mask[None], jnp.exp(jnp.where(mask, diff, 0.0)[None] * lg[:, :, None]), 0.0)
    scores = jnp.einsum('bhncd,bhnmd->bhncm', qc, kc) * d_intra[None, :, None].astype(dt)
    intra = jnp.einsum('bhncm,bhnme->bhnce', scores, vc)
    k_decay = jnp.exp((c - 1 - idx)[None, :] * lg).astype(dt)
    kv = jnp.einsum('bhncd,bhnce->nbhde', kc * k_decay[None, :, None, :, None], vc)
    chunk_decay = jnp.exp(c * lg[:, 0]).astype(dt)[None, :, None, None]

    def step(state, kv_n):
        return chunk_decay * state + kv_n, state

    _, states = lax.scan(step, jnp.zeros_like(kv[0]), kv)
    q_decay = jnp.exp((idx + 1)[None, :] * lg).astype(dt)
    cross = jnp.einsum('bhncd,nbhde->bhnce', qc * q_decay[None, :, None, :, None], states)
    return (intra + cross).reshape(b, h, s, d)


def bidirectional_retention(q, k, v, log_gamma_fwd, log_gamma_bwd):
    fwd = retention_dir(q, k, v, log_gamma_fwd, True)
    flip = lambda t: jnp.flip(t, axis=2)
    bwd = flip(retention_dir(flip(q), flip(k), flip(v), log_gamma_bwd, False))
    return fwd + bwd


def multiscale_pool(v, w_group, scale):
    b, s, cw = v.shape
    vg = v.reshape(b, s, POOL_GROUPS, POOL_GROUP_DIM)
    cs = jnp.cumsum(vg.astype(jnp.float32), axis=1)
    cs = jnp.concatenate([jnp.zeros_like(cs[:, :1]), cs], axis=1)
    t = np.arange(s)[:, None]
    half = np.array(POOL_WINDOWS)[None, :] // 2
    lo = np.clip(t - half, 0, s)
    hi = np.clip(t + half, 0, s)
    g_idx = np.arange(POOL_GROUPS)[None, :]
    win_sum = cs[:, hi, g_idx] - cs[:, lo, g_idx]
    count = jnp.asarray((hi - lo)[None, :, :, None], dtype=jnp.float32)
    pooled = (win_sum / count).astype(v.dtype) - vg
    mixed = jnp.einsum('bsgc,gce->bsge', pooled, w_group)
    return mixed.reshape(b, s, cw) * scale


def neighbourhood_attention(q, k, v, rpb):
    b, h, s, d = q.shape
    rows = s // GRID_W
    wr = min(NA_WIN_ROWS, rows)
    n_cb = GRID_W // NA_QBLOCK_COLS
    r = np.arange(rows)
    row_idx = np.clip(r - wr // 2, 0, rows - wr)[:, None] + np.arange(wr)[None, :]
    cb = np.arange(n_cb)
    kcol_idx = np.clip(cb * NA_QBLOCK_COLS - NA_WIN_COLS // 2, 0, GRID_W - NA_KBLOCK_COLS)[:, None] \
        + np.arange(NA_KBLOCK_COLS)[None, :]
    qcol = cb[:, None] * NA_QBLOCK_COLS + np.arange(NA_QBLOCK_COLS)[None, :]
    qwin = np.clip(qcol - NA_WIN_COLS // 2, 0, GRID_W - NA_WIN_COLS)
    col_mask = (kcol_idx[:, None, :] >= qwin[:, :, None]) & (kcol_idx[:, None, :] < qwin[:, :, None] + NA_WIN_COLS)
    row_off = row_idx - r[:, None]
    col_off = np.clip(kcol_idx[:, None, :] - qcol[:, :, None], -(NA_WIN_COLS - 1), NA_WIN_COLS - 1)
    bias = rpb[:, row_off[:, None, None, :, None] + NA_WIN_ROWS - 1,
               col_off[None, :, :, None, :] + NA_WIN_COLS - 1]
    bias = jnp.where(col_mask[None, None, :, :, None, :], bias.astype(jnp.float32), NEG_INF)

    qg = q.reshape(b, h, rows, n_cb, NA_QBLOCK_COLS, d)
    k_grid = k.reshape(b, h, rows, GRID_W, d)
    v_grid = v.reshape(b, h, rows, GRID_W, d)
    ri = row_idx[:, None, :, None]
    ci = kcol_idx[None, :, None, :]
    kg = k_grid[:, :, ri, ci]
    vg = v_grid[:, :, ri, ci]
    sc = jnp.einsum('bhrnqd,bhrnwkd->bhrnqwk', qg, kg).astype(jnp.float32) * (d ** -0.5) + bias[None]
    p = jax.nn.softmax(sc, axis=(-2, -1))
    o = jnp.einsum('bhrnqwk,bhrnwkd->bhrnqd', p.astype(v.dtype), vg)
    return o.reshape(b, h, s, d)


def memory_attention(q, mk, mv):
    sc = jnp.einsum('bhsd,bhmd->bhsm', q, mk).astype(jnp.float32) * (q.shape[-1] ** -0.5)
    p = jax.nn.softmax(sc, axis=-1)
    return jnp.einsum('bhsm,bhmd->bhsd', p.astype(mv.dtype), mv)


def hybrid_layer(x, mem, norm_mix_g, norm_mem_g, w_in, w_gate, ret_decay_fwd, ret_decay_bwd,
                 ret_norm_g, pool_w, pool_scale, na_q_norm_g, na_k_norm_g, na_rpb,
                 mem_q_norm_g, mem_k_norm_g, w_mem_kv, w_branch, w_out, norm_ffn_g,
                 w_ffn_in, w_ffn_out):
    b, s, dm = x.shape
    h = rms_norm(x, norm_mix_g)
    proj = h @ w_in
    rq, rk, rv, rg, pv, nq, nk, nv, mq = jnp.split(proj, 9, axis=-1)

    pos = jnp.arange(s, dtype=jnp.float32)
    rq_h = rotary(split_heads(rq, RET_HEADS), pos) * (HEAD_DIM ** -0.5)
    rk_h = rotary(split_heads(rk, RET_HEADS), pos)
    ret = bidirectional_retention(rq_h, rk_h, split_heads(rv, RET_HEADS),
                                  jax.nn.log_sigmoid(ret_decay_fwd.astype(jnp.float32)),
                                  jax.nn.log_sigmoid(ret_decay_bwd.astype(jnp.float32)))
    ret = merge_heads(rms_norm(ret, ret_norm_g.reshape(RET_HEADS, 1, HEAD_DIM))) * jax.nn.silu(rg)

    pool = multiscale_pool(pv, pool_w, pool_scale)

    na = merge_heads(neighbourhood_attention(rms_norm(split_heads(nq, NA_HEADS), na_q_norm_g),
                                             rms_norm(split_heads(nk, NA_HEADS), na_k_norm_g),
                                             split_heads(nv, NA_HEADS), na_rpb))

    mk, mv = jnp.split(rms_norm(mem, norm_mem_g) @ w_mem_kv, 2, axis=-1)
    mo = merge_heads(memory_attention(rms_norm(split_heads(mq, MEM_HEADS), mem_q_norm_g),
                                      rms_norm(split_heads(mk, MEM_HEADS), mem_k_norm_g),
                                      split_heads(mv, MEM_HEADS)))

    branches = jnp.stack([ret, pool, na, mo], axis=2)
    up = jnp.einsum('bsnc,ncd->bsnd', branches, w_branch)
    gates = jax.nn.sigmoid(h @ w_gate).reshape(b, s, N_BRANCH, dm)
    merged = jnp.einsum('bsnd,bsnd->bsd', gates, up)
    x = x + merged @ w_out

    a, g = jnp.split(rms_norm(x, norm_ffn_g) @ w_ffn_in, 2, axis=-1)
    return x + (jax.nn.silu(a) * g) @ w_ffn_out


def _fwd_setup_inputs(seed: int = 0) -> dict:
    key = jax.random.key(seed)
    ks = jax.random.split(key, 22)
    f32 = jnp.float32
    L, D, BW = DEPTH, D_MODEL, BRANCH_WIDTH

    def nrm(k, shape, scale):
        return jax.random.normal(k, shape, f32) * scale

    base_logit = jnp.log(2.0 ** (5.0 + jnp.arange(RET_HEADS, dtype=f32)) - 1.0)
    return {
        "x": nrm(ks[0], (BATCH, SEQ, D), 1.0),
        "mem": nrm(ks[1], (BATCH, N_MEM, D), 1.0),
        "norm_mix_g": 1.0 + nrm(ks[2], (L, D), 0.02),
        "norm_mem_g": 1.0 + nrm(ks[3], (L, D), 0.02),
        "w_in": nrm(ks[4], (L, D, IN_WIDTH), D ** -0.5),
        "w_gate": nrm(ks[5], (L, D, N_BRANCH * D), D ** -0.5),
        "ret_decay_fwd": base_logit[None, :] + nrm(ks[6], (L, RET_HEADS), 0.1),
        "ret_decay_bwd": base_logit[None, :] + nrm(ks[7], (L, RET_HEADS), 0.1),
        "ret_norm_g": 1.0 + nrm(ks[8], (L, BW), 0.02),
        "pool_w": nrm(ks[9], (L, POOL_GROUPS, POOL_GROUP_DIM, POOL_GROUP_DIM), POOL_GROUP_DIM ** -0.5),
        "pool_scale": 1.0 + nrm(ks[10], (L, BW), 0.02),
        "na_q_norm_g": 1.0 + nrm(ks[11], (L, HEAD_DIM), 0.02),
        "na_k_norm_g": 1.0 + nrm(ks[12], (L, HEAD_DIM), 0.02),
        "na_rpb": nrm(ks[13], (L, NA_HEADS, 2 * NA_WIN_ROWS - 1, 2 * NA_WIN_COLS - 1), 0.02),
        "mem_q_norm_g": 1.0 + nrm(ks[14], (L, HEAD_DIM), 0.02),
        "mem_k_norm_g": 1.0 + nrm(ks[15], (L, HEAD_DIM), 0.02),
        "w_mem_kv": nrm(ks[16], (L, D, 2 * BW), D ** -0.5),
        "w_branch": nrm(ks[17], (L, N_BRANCH, BW, D), BW ** -0.5),
        "w_out": nrm(ks[18], (L, D, D), D ** -0.5),
        "norm_ffn_g": 1.0 + nrm(ks[19], (L, D), 0.02),
        "w_ffn_in": nrm(ks[20], (L, D, 2 * FF_HIDDEN), D ** -0.5),
        "w_ffn_out": nrm(ks[21], (L, FF_HIDDEN, D), FF_HIDDEN ** -0.5),
    }


def _fwd_reference(x, mem, norm_mix_g, norm_mem_g, w_in, w_gate, ret_decay_fwd, ret_decay_bwd,
              ret_norm_g, pool_w, pool_scale, na_q_norm_g, na_k_norm_g, na_rpb,
              mem_q_norm_g, mem_k_norm_g, w_mem_kv, w_branch, w_out, norm_ffn_g,
              w_ffn_in, w_ffn_out):
    for l in range(DEPTH):
        x = hybrid_layer(x, mem, norm_mix_g[l], norm_mem_g[l], w_in[l], w_gate[l],
                         ret_decay_fwd[l], ret_decay_bwd[l], ret_norm_g[l], pool_w[l],
                         pool_scale[l], na_q_norm_g[l], na_k_norm_g[l], na_rpb[l],
                         mem_q_norm_g[l], mem_k_norm_g[l], w_mem_kv[l], w_branch[l],
                         w_out[l], norm_ffn_g[l], w_ffn_in[l], w_ffn_out[l])
    return x


import jax as _jax
import jax.numpy as _jnp

TWIN_FORMAT = 'train_step'
FWD_PARAMS = ['x', 'mem', 'norm_mix_g', 'norm_mem_g', 'w_in', 'w_gate', 'ret_decay_fwd', 'ret_decay_bwd', 'ret_norm_g', 'pool_w', 'pool_scale', 'na_q_norm_g', 'na_k_norm_g', 'na_rpb', 'mem_q_norm_g', 'mem_k_norm_g', 'w_mem_kv', 'w_branch', 'w_out', 'norm_ffn_g', 'w_ffn_in', 'w_ffn_out']
TWIN_WEIGHTS = ['norm_mix_g', 'norm_mem_g', 'w_in', 'w_gate', 'ret_decay_fwd', 'ret_decay_bwd', 'ret_norm_g', 'pool_w', 'pool_scale', 'na_q_norm_g', 'na_k_norm_g', 'na_rpb', 'mem_q_norm_g', 'mem_k_norm_g', 'w_mem_kv', 'w_branch', 'w_out', 'norm_ffn_g', 'w_ffn_in', 'w_ffn_out']
TWIN_DIFF_INPUT = 'x'
TWIN_INPUTS = ['x', 'mem', 'norm_mix_g', 'norm_mem_g', 'w_in', 'w_gate', 'ret_decay_fwd', 'ret_decay_bwd', 'ret_norm_g', 'pool_w', 'pool_scale', 'na_q_norm_g', 'na_k_norm_g', 'na_rpb', 'mem_q_norm_g', 'mem_k_norm_g', 'w_mem_kv', 'w_branch', 'w_out', 'norm_ffn_g', 'w_ffn_in', 'w_ffn_out', 'loss_target', 'm_norm_mix_g', 'm_norm_mem_g', 'm_w_in', 'm_w_gate', 'm_ret_decay_fwd', 'm_ret_decay_bwd', 'm_ret_norm_g', 'm_pool_w', 'm_pool_scale', 'm_na_q_norm_g', 'm_na_k_norm_g', 'm_na_rpb', 'm_mem_q_norm_g', 'm_mem_k_norm_g', 'm_w_mem_kv', 'm_w_branch', 'm_w_out', 'm_norm_ffn_g', 'm_w_ffn_in', 'm_w_ffn_out', 'v_norm_mix_g', 'v_norm_mem_g', 'v_w_in', 'v_w_gate', 'v_ret_decay_fwd', 'v_ret_decay_bwd', 'v_ret_norm_g', 'v_pool_w', 'v_pool_scale', 'v_na_q_norm_g', 'v_na_k_norm_g', 'v_na_rpb', 'v_mem_q_norm_g', 'v_mem_k_norm_g', 'v_w_mem_kv', 'v_w_branch', 'v_w_out', 'v_norm_ffn_g', 'v_w_ffn_in', 'v_w_ffn_out']
TWIN_OUTPUTS = ['loss', 'grad_x', 'grad_norm_mix_g', 'grad_norm_mem_g', 'grad_w_in', 'grad_w_gate', 'grad_ret_decay_fwd', 'grad_ret_decay_bwd', 'grad_ret_norm_g', 'grad_pool_w', 'grad_pool_scale', 'grad_na_q_norm_g', 'grad_na_k_norm_g', 'grad_na_rpb', 'grad_mem_q_norm_g', 'grad_mem_k_norm_g', 'grad_w_mem_kv', 'grad_w_branch', 'grad_w_out', 'grad_norm_ffn_g', 'grad_w_ffn_in', 'grad_w_ffn_out', 'delta_norm_mix_g', 'delta_norm_mem_g', 'delta_w_in', 'delta_w_gate', 'delta_ret_decay_fwd', 'delta_ret_decay_bwd', 'delta_ret_norm_g', 'delta_pool_w', 'delta_pool_scale', 'delta_na_q_norm_g', 'delta_na_k_norm_g', 'delta_na_rpb', 'delta_mem_q_norm_g', 'delta_mem_k_norm_g', 'delta_w_mem_kv', 'delta_w_branch', 'delta_w_out', 'delta_norm_ffn_g', 'delta_w_ffn_in', 'delta_w_ffn_out', 'new_m_norm_mix_g', 'new_m_norm_mem_g', 'new_m_w_in', 'new_m_w_gate', 'new_m_ret_decay_fwd', 'new_m_ret_decay_bwd', 'new_m_ret_norm_g', 'new_m_pool_w', 'new_m_pool_scale', 'new_m_na_q_norm_g', 'new_m_na_k_norm_g', 'new_m_na_rpb', 'new_m_mem_q_norm_g', 'new_m_mem_k_norm_g', 'new_m_w_mem_kv', 'new_m_w_branch', 'new_m_w_out', 'new_m_norm_ffn_g', 'new_m_w_ffn_in', 'new_m_w_ffn_out', 'new_v_norm_mix_g', 'new_v_norm_mem_g', 'new_v_w_in', 'new_v_w_gate', 'new_v_ret_decay_fwd', 'new_v_ret_decay_bwd', 'new_v_ret_norm_g', 'new_v_pool_w', 'new_v_pool_scale', 'new_v_na_q_norm_g', 'new_v_na_k_norm_g', 'new_v_na_rpb', 'new_v_mem_q_norm_g', 'new_v_mem_k_norm_g', 'new_v_w_mem_kv', 'new_v_w_branch', 'new_v_w_out', 'new_v_norm_ffn_g', 'new_v_w_ffn_in', 'new_v_w_ffn_out']
TWIN_LEAF_KINDS = {'loss': 'loss', 'grad_x': 'grad_x', 'grad_norm_mix_g': 'grad_w', 'grad_norm_mem_g': 'grad_w', 'grad_w_in': 'grad_w', 'grad_w_gate': 'grad_w', 'grad_ret_decay_fwd': 'grad_w', 'grad_ret_decay_bwd': 'grad_w', 'grad_ret_norm_g': 'grad_w', 'grad_pool_w': 'grad_w', 'grad_pool_scale': 'grad_w', 'grad_na_q_norm_g': 'grad_w', 'grad_na_k_norm_g': 'grad_w', 'grad_na_rpb': 'grad_w', 'grad_mem_q_norm_g': 'grad_w', 'grad_mem_k_norm_g': 'grad_w', 'grad_w_mem_kv': 'grad_w', 'grad_w_branch': 'grad_w', 'grad_w_out': 'grad_w', 'grad_norm_ffn_g': 'grad_w', 'grad_w_ffn_in': 'grad_w', 'grad_w_ffn_out': 'grad_w', 'delta_norm_mix_g': 'delta_w', 'delta_norm_mem_g': 'delta_w', 'delta_w_in': 'delta_w', 'delta_w_gate': 'delta_w', 'delta_ret_decay_fwd': 'delta_w', 'delta_ret_decay_bwd': 'delta_w', 'delta_ret_norm_g': 'delta_w', 'delta_pool_w': 'delta_w', 'delta_pool_scale': 'delta_w', 'delta_na_q_norm_g': 'delta_w', 'delta_na_k_norm_g': 'delta_w', 'delta_na_rpb': 'delta_w', 'delta_mem_q_norm_g': 'delta_w', 'delta_mem_k_norm_g': 'delta_w', 'delta_w_mem_kv': 'delta_w', 'delta_w_branch': 'delta_w', 'delta_w_out': 'delta_w', 'delta_norm_ffn_g': 'delta_w', 'delta_w_ffn_in': 'delta_w', 'delta_w_ffn_out': 'delta_w', 'new_m_norm_mix_g': 'new_m', 'new_m_norm_mem_g': 'new_m', 'new_m_w_in': 'new_m', 'new_m_w_gate': 'new_m', 'new_m_ret_decay_fwd': 'new_m', 'new_m_ret_decay_bwd': 'new_m', 'new_m_ret_norm_g': 'new_m', 'new_m_pool_w': 'new_m', 'new_m_pool_scale': 'new_m', 'new_m_na_q_norm_g': 'new_m', 'new_m_na_k_norm_g': 'new_m', 'new_m_na_rpb': 'new_m', 'new_m_mem_q_norm_g': 'new_m', 'new_m_mem_k_norm_g': 'new_m', 'new_m_w_mem_kv': 'new_m', 'new_m_w_branch': 'new_m', 'new_m_w_out': 'new_m', 'new_m_norm_ffn_g': 'new_m', 'new_m_w_ffn_in': 'new_m', 'new_m_w_ffn_out': 'new_m', 'new_v_norm_mix_g': 'new_v', 'new_v_norm_mem_g': 'new_v', 'new_v_w_in': 'new_v', 'new_v_w_gate': 'new_v', 'new_v_ret_decay_fwd': 'new_v', 'new_v_ret_decay_bwd': 'new_v', 'new_v_ret_norm_g': 'new_v', 'new_v_pool_w': 'new_v', 'new_v_pool_scale': 'new_v', 'new_v_na_q_norm_g': 'new_v', 'new_v_na_k_norm_g': 'new_v', 'new_v_na_rpb': 'new_v', 'new_v_mem_q_norm_g': 'new_v', 'new_v_mem_k_norm_g': 'new_v', 'new_v_w_mem_kv': 'new_v', 'new_v_w_branch': 'new_v', 'new_v_w_out': 'new_v', 'new_v_norm_ffn_g': 'new_v', 'new_v_w_ffn_in': 'new_v', 'new_v_w_ffn_out': 'new_v'}


def _forward(args):
    return _fwd_reference(*[args[k] for k in FWD_PARAMS])


def _output_shape():
    out = _jax.eval_shape(lambda: _forward(_fwd_setup_inputs(0)))
    return out.shape, out.dtype

N_MICROBATCH = 1
ADAM_LR = 0.001
ADAM_B1 = 0.9
ADAM_B2 = 0.999
ADAM_EPS = 1e-08
ADAM_WD = 0.01
ADAM_STEP = 10
PER_EXAMPLE_BATCH_AXIS = {'x': 0, 'mem': 0, 'loss_target': 0}
SHARED_INPUTS = []
_WEIGHT_DTYPES = {'norm_mix_g': _jnp.float32, 'norm_mem_g': _jnp.float32, 'w_in': _jnp.float32, 'w_gate': _jnp.float32, 'ret_decay_fwd': _jnp.float32, 'ret_decay_bwd': _jnp.float32, 'ret_norm_g': _jnp.float32, 'pool_w': _jnp.float32, 'pool_scale': _jnp.float32, 'na_q_norm_g': _jnp.float32, 'na_k_norm_g': _jnp.float32, 'na_rpb': _jnp.float32, 'mem_q_norm_g': _jnp.float32, 'mem_k_norm_g': _jnp.float32, 'w_mem_kv': _jnp.float32, 'w_branch': _jnp.float32, 'w_out': _jnp.float32, 'norm_ffn_g': _jnp.float32, 'w_ffn_in': _jnp.float32, 'w_ffn_out': _jnp.float32}
MOMENT_SCALE = {'norm_mix_g': 6.161818e+00, 'norm_mem_g': 8.580154e-02, 'w_in': 4.417592e-01, 'w_gate': 6.774429e-02, 'ret_decay_fwd': 1.530547e+00, 'ret_decay_bwd': 1.821360e+00, 'ret_norm_g': 6.562512e+00, 'pool_w': 2.036200e+00, 'pool_scale': 1.552856e+01, 'na_q_norm_g': 7.145341e-01, 'na_k_norm_g': 7.118145e-01, 'na_rpb': 4.493712e-02, 'mem_q_norm_g': 6.112246e-01, 'mem_k_norm_g': 6.113193e-01, 'w_mem_kv': 1.028207e-01, 'w_branch': 3.087362e-01, 'w_out': 5.681008e-01, 'norm_ffn_g': 1.231323e+01, 'w_ffn_in': 2.020605e-01, 'w_ffn_out': 3.391446e-01}


def _to_microbatches(a, axis):
    t = _jnp.moveaxis(a, axis, 0)
    t = t.reshape((N_MICROBATCH, t.shape[0] // N_MICROBATCH) + t.shape[1:])
    return _jnp.moveaxis(t, 1, axis + 1)


def setup_inputs(seed: int = 0) -> dict:
    inp = _fwd_setup_inputs(seed)
    key = _jax.random.fold_in(_jax.random.key(seed), 7919)
    shape, _ = _output_shape()
    out = dict(inp)
    out["loss_target"] = _jax.random.normal(_jax.random.fold_in(key, 0), shape, _jnp.float32)
    for i, name in enumerate(TWIN_WEIGHTS):
        w = inp[name].astype(_jnp.float32)
        if MOMENT_SCALE is None:
            s = _jnp.sqrt(_jnp.mean(_jnp.square(w)) + 1e-30)
        else:
            s = MOMENT_SCALE[name]
        km, kv = _jax.random.split(_jax.random.fold_in(key, i + 1))
        out[name] = w
        out["m_" + name] = s * _jax.random.normal(km, w.shape, _jnp.float32)
        out["v_" + name] = (s * s) * _jax.random.uniform(kv, w.shape, _jnp.float32, 0.5, 1.5)
    if N_MICROBATCH > 1:
        for name, axis in PER_EXAMPLE_BATCH_AXIS.items():
            out[name] = _to_microbatches(out[name], axis)
    return {'x': out['x'], 'mem': out['mem'], 'norm_mix_g': out['norm_mix_g'], 'norm_mem_g': out['norm_mem_g'], 'w_in': out['w_in'], 'w_gate': out['w_gate'], 'ret_decay_fwd': out['ret_decay_fwd'], 'ret_decay_bwd': out['ret_decay_bwd'], 'ret_norm_g': out['ret_norm_g'], 'pool_w': out['pool_w'], 'pool_scale': out['pool_scale'], 'na_q_norm_g': out['na_q_norm_g'], 'na_k_norm_g': out['na_k_norm_g'], 'na_rpb': out['na_rpb'], 'mem_q_norm_g': out['mem_q_norm_g'], 'mem_k_norm_g': out['mem_k_norm_g'], 'w_mem_kv': out['w_mem_kv'], 'w_branch': out['w_branch'], 'w_out': out['w_out'], 'norm_ffn_g': out['norm_ffn_g'], 'w_ffn_in': out['w_ffn_in'], 'w_ffn_out': out['w_ffn_out'], 'loss_target': out['loss_target'], 'm_norm_mix_g': out['m_norm_mix_g'], 'm_norm_mem_g': out['m_norm_mem_g'], 'm_w_in': out['m_w_in'], 'm_w_gate': out['m_w_gate'], 'm_ret_decay_fwd': out['m_ret_decay_fwd'], 'm_ret_decay_bwd': out['m_ret_decay_bwd'], 'm_ret_norm_g': out['m_ret_norm_g'], 'm_pool_w': out['m_pool_w'], 'm_pool_scale': out['m_pool_scale'], 'm_na_q_norm_g': out['m_na_q_norm_g'], 'm_na_k_norm_g': out['m_na_k_norm_g'], 'm_na_rpb': out['m_na_rpb'], 'm_mem_q_norm_g': out['m_mem_q_norm_g'], 'm_mem_k_norm_g': out['m_mem_k_norm_g'], 'm_w_mem_kv': out['m_w_mem_kv'], 'm_w_branch': out['m_w_branch'], 'm_w_out': out['m_w_out'], 'm_norm_ffn_g': out['m_norm_ffn_g'], 'm_w_ffn_in': out['m_w_ffn_in'], 'm_w_ffn_out': out['m_w_ffn_out'], 'v_norm_mix_g': out['v_norm_mix_g'], 'v_norm_mem_g': out['v_norm_mem_g'], 'v_w_in': out['v_w_in'], 'v_w_gate': out['v_w_gate'], 'v_ret_decay_fwd': out['v_ret_decay_fwd'], 'v_ret_decay_bwd': out['v_ret_decay_bwd'], 'v_ret_norm_g': out['v_ret_norm_g'], 'v_pool_w': out['v_pool_w'], 'v_pool_scale': out['v_pool_scale'], 'v_na_q_norm_g': out['v_na_q_norm_g'], 'v_na_k_norm_g': out['v_na_k_norm_g'], 'v_na_rpb': out['v_na_rpb'], 'v_mem_q_norm_g': out['v_mem_q_norm_g'], 'v_mem_k_norm_g': out['v_mem_k_norm_g'], 'v_w_mem_kv': out['v_w_mem_kv'], 'v_w_branch': out['v_w_branch'], 'v_w_out': out['v_w_out'], 'v_norm_ffn_g': out['v_norm_ffn_g'], 'v_w_ffn_in': out['v_w_ffn_in'], 'v_w_ffn_out': out['v_w_ffn_out']}


def _loss(weights, diff, rest, loss_target):
    with _jax.named_scope("forward"):
        args = {**rest, TWIN_DIFF_INPUT: diff, **{k: w.astype(_WEIGHT_DTYPES[k]) for k, w in weights.items()}}
        y = _forward(args)
    with _jax.named_scope("loss_head"):
        err = _jnp.square(y.astype(_jnp.float32) - loss_target)
        return 0.5 * _jnp.sum(_jnp.mean(err, axis=-1)) if err.ndim else 0.5 * err


def _adamw(w, g, m, v):
    m = ADAM_B1 * m + (1.0 - ADAM_B1) * g
    v = ADAM_B2 * v + (1.0 - ADAM_B2) * _jnp.square(g)
    m_hat = m / (1.0 - ADAM_B1 ** ADAM_STEP)
    v_hat = v / (1.0 - ADAM_B2 ** ADAM_STEP)
    delta = -ADAM_LR * (m_hat / (_jnp.sqrt(v_hat) + ADAM_EPS) + ADAM_WD * w)
    return delta, m, v


def reference(x, mem, norm_mix_g, norm_mem_g, w_in, w_gate, ret_decay_fwd, ret_decay_bwd, ret_norm_g, pool_w, pool_scale, na_q_norm_g, na_k_norm_g, na_rpb, mem_q_norm_g, mem_k_norm_g, w_mem_kv, w_branch, w_out, norm_ffn_g, w_ffn_in, w_ffn_out, loss_target, m_norm_mix_g, m_norm_mem_g, m_w_in, m_w_gate, m_ret_decay_fwd, m_ret_decay_bwd, m_ret_norm_g, m_pool_w, m_pool_scale, m_na_q_norm_g, m_na_k_norm_g, m_na_rpb, m_mem_q_norm_g, m_mem_k_norm_g, m_w_mem_kv, m_w_branch, m_w_out, m_norm_ffn_g, m_w_ffn_in, m_w_ffn_out, v_norm_mix_g, v_norm_mem_g, v_w_in, v_w_gate, v_ret_decay_fwd, v_ret_decay_bwd, v_ret_norm_g, v_pool_w, v_pool_scale, v_na_q_norm_g, v_na_k_norm_g, v_na_rpb, v_mem_q_norm_g, v_mem_k_norm_g, v_w_mem_kv, v_w_branch, v_w_out, v_norm_ffn_g, v_w_ffn_in, v_w_ffn_out):
    given = dict(x=x, mem=mem, norm_mix_g=norm_mix_g, norm_mem_g=norm_mem_g, w_in=w_in, w_gate=w_gate, ret_decay_fwd=ret_decay_fwd, ret_decay_bwd=ret_decay_bwd, ret_norm_g=ret_norm_g, pool_w=pool_w, pool_scale=pool_scale, na_q_norm_g=na_q_norm_g, na_k_norm_g=na_k_norm_g, na_rpb=na_rpb, mem_q_norm_g=mem_q_norm_g, mem_k_norm_g=mem_k_norm_g, w_mem_kv=w_mem_kv, w_branch=w_branch, w_out=w_out, norm_ffn_g=norm_ffn_g, w_ffn_in=w_ffn_in, w_ffn_out=w_ffn_out, loss_target=loss_target, m_norm_mix_g=m_norm_mix_g, m_norm_mem_g=m_norm_mem_g, m_w_in=m_w_in, m_w_gate=m_w_gate, m_ret_decay_fwd=m_ret_decay_fwd, m_ret_decay_bwd=m_ret_decay_bwd, m_ret_norm_g=m_ret_norm_g, m_pool_w=m_pool_w, m_pool_scale=m_pool_scale, m_na_q_norm_g=m_na_q_norm_g, m_na_k_norm_g=m_na_k_norm_g, m_na_rpb=m_na_rpb, m_mem_q_norm_g=m_mem_q_norm_g, m_mem_k_norm_g=m_mem_k_norm_g, m_w_mem_kv=m_w_mem_kv, m_w_branch=m_w_branch, m_w_out=m_w_out, m_norm_ffn_g=m_norm_ffn_g, m_w_ffn_in=m_w_ffn_in, m_w_ffn_out=m_w_ffn_out, v_norm_mix_g=v_norm_mix_g, v_norm_mem_g=v_norm_mem_g, v_w_in=v_w_in, v_w_gate=v_w_gate, v_ret_decay_fwd=v_ret_decay_fwd, v_ret_decay_bwd=v_ret_decay_bwd, v_ret_norm_g=v_ret_norm_g, v_pool_w=v_pool_w, v_pool_scale=v_pool_scale, v_na_q_norm_g=v_na_q_norm_g, v_na_k_norm_g=v_na_k_norm_g, v_na_rpb=v_na_rpb, v_mem_q_norm_g=v_mem_q_norm_g, v_mem_k_norm_g=v_mem_k_norm_g, v_w_mem_kv=v_w_mem_kv, v_w_branch=v_w_branch, v_w_out=v_w_out, v_norm_ffn_g=v_norm_ffn_g, v_w_ffn_in=v_w_ffn_in, v_w_ffn_out=v_w_ffn_out)
    weights = {n: given[n] for n in TWIN_WEIGHTS}
    shared = {n: given[n] for n in SHARED_INPUTS}
    per_example = {n: given[n] for n in ['x', 'mem']}
    grad_fn = _jax.value_and_grad(_loss, argnums=(0, 1))

    def one_microbatch(ex, loss_target):
        ex = dict(ex)
        diff = ex.pop(TWIN_DIFF_INPUT)
        return grad_fn(weights, diff, {**shared, **ex}, loss_target)

    if N_MICROBATCH == 1:
        loss, (grad_w, grad_x) = one_microbatch(per_example, given["loss_target"])
    else:
        def body(carry, xs):
            loss_sum, grad_sum = carry
            l_k, (gw_k, gx_k) = one_microbatch(xs[0], xs[1])
            with _jax.named_scope("update"):
                return (loss_sum + l_k, _jax.tree.map(_jnp.add, grad_sum, gw_k)), gx_k

        init = (_jnp.zeros((), _jnp.float32), _jax.tree.map(_jnp.zeros_like, weights))
        (loss, grad_w), grad_x = _jax.lax.scan(body, init, (per_example, given["loss_target"]))
    with _jax.named_scope("update"):
        delta_w, new_m, new_v = {}, {}, {}
        for n in TWIN_WEIGHTS:
            delta_w[n], new_m[n], new_v[n] = _adamw(weights[n], grad_w[n], given["m_" + n], given["v_" + n])
    return (loss, grad_x, *[grad_w[n] for n in TWIN_WEIGHTS], *[delta_w[n] for n in TWIN_WEIGHTS],
            *[new_m[n] for n in TWIN_WEIGHTS], *[new_v[n] for n in TWIN_WEIGHTS])
```

```python
import functools

import numpy as np
import jax
import jax.numpy as jnp
from jax import lax
from jax.experimental import pallas as pl
from jax.experimental.pallas import tpu as pltpu

F32 = jnp.float32
BF16 = jnp.bfloat16
SDS = jax.ShapeDtypeStruct
MESH = pl.DeviceIdType.MESH
ANY = pl.BlockSpec(memory_space=pl.ANY)

HEAD = 64
NH = 4
BW = NH * HEAD
CH = 128
GRID_W = 64
NA_ROWS = 8
NA_COLS = 16
POOL_HALF = (1, 2, 4, 8)
POOL_PAD = 16
ROPE_THETA = 10000.0
EPS = 1e-6
NEG_INF = -1e30
N_CHIPS = 4
N_DEV = 8
LANES = 128

ADAM_LR = 0.001
ADAM_B1 = 0.9
ADAM_B2 = 0.999
ADAM_EPS = 1e-08
ADAM_WD = 0.01
ADAM_STEP = 10

VMEM_BIG = 56 << 20
VMEM_MM = 44 << 20


def _cp(vmem=None):
    return pltpu.CompilerParams(vmem_limit_bytes=vmem) if vmem else None


def _d(a, b, ca, cb):
    return lax.dot_general(a.astype(BF16), b.astype(BF16), (((ca,), (cb,)), ((), ())),
                           preferred_element_type=F32)


@jax.custom_vjp
def _nn(a, b):
    return _d(a, b, 1, 0)


def _nn_f(a, b):
    return _d(a, b, 1, 0), (a, b)


def _nn_b(r, g):
    a, b = r
    return _d(g, b, 1, 1).astype(a.dtype), _d(a, g, 0, 0).astype(b.dtype)


_nn.defvjp(_nn_f, _nn_b)


@jax.custom_vjp
def _nt(a, b):
    return _d(a, b, 1, 1)


def _nt_f(a, b):
    return _d(a, b, 1, 1), (a, b)


def _nt_b(r, g):
    a, b = r
    return _d(g, b, 1, 0).astype(a.dtype), _d(g, a, 0, 0).astype(b.dtype)


_nt.defvjp(_nt_f, _nt_b)


@jax.custom_vjp
def _tn(a, b):
    return _d(a, b, 0, 0)


def _tn_f(a, b):
    return _d(a, b, 0, 0), (a, b)


def _tn_b(r, g):
    a, b = r
    return _d(b, g, 1, 1).astype(a.dtype), _d(a, g, 1, 0).astype(b.dtype)


_tn.defvjp(_tn_f, _tn_b)


@functools.partial(jax.custom_vjp, nondiff_argnums=(1,))
def _rollr(x, s):
    return pltpu.roll(x, s % x.shape[0], 0)


def _rollr_f(x, s):
    return _rollr(x, s), None


def _rollr_b(s, _, g):
    return (_rollr(g, -s),)


_rollr.defvjp(_rollr_f, _rollr_b)


@jax.custom_vjp
def _swap32(t):
    n = t.shape[1]
    lane = lax.broadcasted_iota(jnp.int32, (1, n), 1)
    return jnp.where((lane & (HEAD // 2)) == 0, pltpu.roll(t, n - HEAD // 2, 1), pltpu.roll(t, HEAD // 2, 1))


def _swap32_f(t):
    return _swap32(t), None


def _swap32_b(_, g):
    return (_swap32(g),)


_swap32.defvjp(_swap32_f, _swap32_b)


def _head_masks():
    lane = lax.broadcasted_iota(jnp.int32, (1, BW), 1)
    return [(lane >= HEAD * h) & (lane < HEAD * (h + 1)) for h in range(NH)]


def _head_rms(x, g):
    out = jnp.zeros_like(x)
    for mh in _head_masks():
        ms = jnp.sum(jnp.where(mh, x * x, 0.0), axis=-1, keepdims=True) * (1.0 / HEAD)
        out = out + jnp.where(mh, x * lax.rsqrt(ms + EPS), 0.0)
    return out * g


def _rms(x, g):
    return x * lax.rsqrt(jnp.mean(x * x, axis=-1, keepdims=True) + EPS) * g


def _rot(t, cos, sin):
    return t * cos + _swap32(t) * sin


def _softmax(s):
    e = jnp.exp(s - lax.stop_gradient(jnp.max(s, axis=-1, keepdims=True)))
    return e / jnp.sum(e, axis=-1, keepdims=True)


def _ret_kv_f(k, v, cos, sin, decf, decb):
    lgf, lgb = jax.nn.log_sigmoid(decf), jax.nn.log_sigmoid(decb)
    kr = _rot(k, cos, sin)
    idx = lax.broadcasted_iota(jnp.int32, (CH, 1), 0).astype(F32)
    r = lax.broadcasted_iota(jnp.int32, (BW, BW), 0) // HEAD
    c = lax.broadcasted_iota(jnp.int32, (BW, BW), 1) // HEAD
    bd = r == c
    kvf = jnp.where(bd, _tn(kr * jnp.exp((CH - 1 - idx) * lgf), v), 0.0)
    kvb = jnp.where(bd, _tn(kr * jnp.exp(idx * lgb), v), 0.0)
    return kvf, kvb


def _ret_scan_f(kvf, kvb, decf, decb):
    n = len(kvf)
    cdf = jnp.exp(CH * jax.nn.log_sigmoid(decf))
    cdb = jnp.exp(CH * jax.nn.log_sigmoid(decb))
    z = jnp.zeros((BW, BW), F32)
    sf, st = [z], z
    for a in range(n - 1):
        st = cdf * st + kvf[a]
        sf.append(st)
    sb, st = [z], z
    for a in range(n - 1, 0, -1):
        st = cdb * st + kvb[a]
        sb.append(st)
    return sf, sb[::-1]


def _ret_out_f(q, k, v, g, cos, sin, stf, stb, decf, decb, gn):
    lgf, lgb = jax.nn.log_sigmoid(decf), jax.nn.log_sigmoid(decb)
    qr = _rot(q, cos, sin) * (HEAD ** -0.5)
    kr = _rot(k, cos, sin)
    diff = (lax.broadcasted_iota(jnp.int32, (CH, CH), 0) - lax.broadcasted_iota(jnp.int32, (CH, CH), 1)).astype(F32)
    o = jnp.zeros_like(q)
    for mh in _head_masks():
        lf = jnp.sum(jnp.where(mh, lgf, 0.0), axis=1, keepdims=True) * (1.0 / HEAD)
        lb = jnp.sum(jnp.where(mh, lgb, 0.0), axis=1, keepdims=True) * (1.0 / HEAD)
        s = _nt(jnp.where(mh, qr, 0.0), kr)
        p = s * jnp.exp(jnp.where(diff >= 0, diff * lf, -diff * lb))
        o = o + jnp.where(mh, _nn(p, v), 0.0)
    idx = lax.broadcasted_iota(jnp.int32, (CH, 1), 0).astype(F32)
    o = o + _nn(qr * jnp.exp((idx + 1) * lgf), stf) + _nn(qr * jnp.exp((CH - idx) * lgb), stb)
    return _head_rms(o, gn) * (g * jax.nn.sigmoid(g))


def _pool_f(pvp, wbd, scale, t_len):
    n = pvp.shape[0]
    w2 = pvp + _rollr(pvp, 1)
    w4 = _rollr(w2, 1) + _rollr(w2, -1)
    w8 = _rollr(w4, 2) + _rollr(w4, -2)
    w16 = _rollr(w8, 4) + _rollr(w8, -4)
    grp = lax.broadcasted_iota(jnp.int32, (1, BW), 1) // HEAD
    ws = jnp.where(grp == 0, w2, jnp.where(grp == 1, w4, jnp.where(grp == 2, w8, w16)))
    half = jnp.where(grp == 0, POOL_HALF[0], jnp.where(grp == 1, POOL_HALF[1], jnp.where(grp == 2, POOL_HALF[2], POOL_HALF[3])))
    t = lax.broadcasted_iota(jnp.int32, (n, 1), 0) - POOL_PAD
    cnt = jnp.minimum(t + half, t_len) - jnp.maximum(t - half, 0)
    cnt = jnp.where((t >= 0) & (t < t_len), cnt, 1).astype(F32)
    pooled = ws / cnt - pvp
    return _nn(pooled, wbd) * scale


def _na_f(q, kw, vw, bias, gq, gk):
    qn = _head_rms(q, gq)
    kn = _head_rms(kw, gk)
    o = jnp.zeros_like(q)
    for h, mh in enumerate(_head_masks()):
        s = _nt(jnp.where(mh, qn, 0.0), kn) * (HEAD ** -0.5) + bias[h]
        o = o + jnp.where(mh, _nn(_softmax(s), vw), 0.0)
    return o


def _mem_f(q, mk, mv, gq, gk):
    qn = _head_rms(q, gq)
    kn = _head_rms(mk, gk)
    o = jnp.zeros_like(q)
    for mh in _head_masks():
        s = _nt(jnp.where(mh, qn, 0.0), kn) * (HEAD ** -0.5)
        o = o + jnp.where(mh, _nn(_softmax(s), mv), 0.0)
    return o


def _gate_f(gp, br, wb):
    out = None
    for n in range(NH):
        t = jax.nn.sigmoid(gp[n]) * _nn(br[n], wb[n])
        out = t if out is None else out + t
    return out


def _swiglu_f(a, g):
    return a * jax.nn.sigmoid(a) * g


def _tile(n, prefs):
    for p in prefs:
        if n % p == 0:
            return p
    return n


def _mm_call(name, a, b, extra, dn, grid, a_spec, b_spec, extra_specs, o_spec, out_shape, acc_shape, nred, alias=None):
    n_extra = len(extra)
    add = n_extra == 1 and alias is None

    def body(*refs):
        a_ref, b_ref = refs[0], refs[1]
        o_ref, acc = refs[2 + n_extra], refs[3 + n_extra]
        first = functools.reduce(jnp.logical_and, [pl.program_id(len(grid) - 1 - i) == 0 for i in range(nred)])
        last = functools.reduce(jnp.logical_and, [pl.program_id(len(grid) - 1 - i) == grid[len(grid) - 1 - i] - 1 for i in range(nred)])

        @pl.when(first)
        def _():
            acc[...] = jnp.zeros_like(acc)

        acc[...] += lax.dot_general(a_ref[...].astype(BF16), b_ref[...].astype(BF16), dn, preferred_element_type=F32)

        @pl.when(last)
        def _():
            r = acc[...]
            if add:
                r = r + refs[2][...]
            o_ref[...] = r.astype(o_ref.dtype)

    return pl.pallas_call(
        body, grid=grid, in_specs=[a_spec, b_spec, *extra_specs], out_specs=o_spec, out_shape=out_shape,
        scratch_shapes=[pltpu.VMEM(acc_shape, F32)], name=name, compiler_params=_cp(VMEM_MM),
        input_output_aliases=alias or {},
    )(a, b, *extra)


def _mm_nn(name, a, w, l, *, out3d=False, add=None, out_dtype=F32):
    m, k = a.shape
    _, s_n, _, ns = w.shape
    tm = _tile(m, (512, 256))
    tk = _tile(k, (1024, 1408, 512, 256))
    tn = ns
    grid = (m // tm, s_n, ns // tn, k // tk)
    a_spec = pl.BlockSpec((tm, tk), lambda i, s, j, kk: (i, kk))
    b_spec = pl.BlockSpec((None, None, tk, tn), lambda i, s, j, kk: (l, s, kk, j))
    if out3d:
        o_spec = pl.BlockSpec((None, tm, tn), lambda i, s, j, kk: (s, i, j))
        out_shape = SDS((s_n, m, ns), out_dtype)
    else:
        nj = ns // tn
        o_spec = pl.BlockSpec((tm, tn), lambda i, s, j, kk: (i, s * nj + j))
        out_shape = SDS((m, s_n * ns), out_dtype)
    extra, especs = [], []
    if add is not None:
        assert not out3d
        nj = ns // tn
        extra, especs = [add], [pl.BlockSpec((tm, tn), lambda i, s, j, kk: (i, s * nj + j))]
    return _mm_call(name, a, w, extra, (((1,), (0,)), ((), ())), grid, a_spec, b_spec, especs, o_spec, out_shape, (tm, tn), 1)


def _mm_nt(name, g, w, l):
    _, s_n, k, ns = w.shape
    g3d = g.ndim == 3
    m = g.shape[1] if g3d else g.shape[0]
    tm = _tile(m, (512, 256))
    to = _tile(k, (1024, 1408, 512, 256))
    tk = ns if (ns % LANES or ns <= 1408) else _tile(ns, (1024, 512))
    nks = ns // tk
    grid = (m // tm, k // to, s_n, nks)
    if g3d:
        a_spec = pl.BlockSpec((None, tm, tk), lambda i, j, s, r: (s, i, r))
    else:
        a_spec = pl.BlockSpec((tm, tk), lambda i, j, s, r: (i, s * nks + r))
    b_spec = pl.BlockSpec((None, None, to, tk), lambda i, j, s, r: (l, s, j, r))
    o_spec = pl.BlockSpec((tm, to), lambda i, j, s, r: (i, j))
    return _mm_call(name, g, w, [], (((1,), (1,)), ((), ())), grid, a_spec, b_spec, [], o_spec, SDS((m, k), F32), (tm, to), 2)


def _mm_tn(name, a, g, buf, l):
    t_len, k = a.shape
    _, s_n, _, ns = buf.shape
    g3d = g.ndim == 3
    tt = _tile(t_len, (1024, 512))
    to = _tile(k, (512, 256, 128))
    tn = ns
    nj = ns // tn
    grid = (s_n, k // to, nj, t_len // tt)
    a_spec = pl.BlockSpec((tt, to), lambda s, i, j, t: (t, i))
    if g3d:
        b_spec = pl.BlockSpec((None, tt, tn), lambda s, i, j, t: (s, t, j))
    else:
        b_spec = pl.BlockSpec((tt, tn), lambda s, i, j, t: (t, s * nj + j))
    o_spec = pl.BlockSpec((None, None, to, tn), lambda s, i, j, t: (l, s, i, j))
    return _mm_call(name, a, g, [buf], (((0,), (0,)), ((), ())), grid, a_spec, b_spec, [ANY], o_spec,
                    SDS(buf.shape, F32), (to, tn), 1, alias={2: 0})


def _rms_fwd(name, x, g):
    m, d = x.shape
    tm = _tile(m, (256,))

    def body(x_ref, g_ref, o_ref):
        o_ref[...] = _rms(x_ref[...], g_ref[...]).astype(BF16)

    row = pl.BlockSpec((tm, d), lambda i: (i, 0))
    return pl.pallas_call(body, grid=(m // tm,), in_specs=[row, pl.BlockSpec((1, d), lambda i: (0, 0))], out_specs=row,
                          out_shape=SDS((m, d), BF16), name=name)(x, g)


def _rms_bwd(name, x, g, dhs, add=None):
    m, d = x.shape
    tm = _tile(m, (256,))
    nh = len(dhs)

    def body(*refs):
        x_ref, g_ref = refs[0], refs[1]
        dh = refs[2][...]
        for r in refs[3:2 + nh]:
            dh = dh + r[...]
        dx_ref, dg_ref = refs[-2], refs[-1]
        _, vjp = jax.vjp(_rms, x_ref[...], g_ref[...])
        dx, dg = vjp(dh)
        if add is not None:
            dx = dx + refs[2 + nh][...]
        dx_ref[...] = dx

        @pl.when(pl.program_id(0) == 0)
        def _():
            dg_ref[...] = jnp.zeros_like(dg_ref)

        dg_ref[...] += dg

    row = pl.BlockSpec((tm, d), lambda i: (i, 0))
    vec = pl.BlockSpec((1, d), lambda i: (0, 0))
    ins = [x, g, *dhs] + ([add] if add is not None else [])
    return pl.pallas_call(body, grid=(m // tm,), in_specs=[row, vec] + [row] * (len(ins) - 2), out_specs=[row, vec],
                          out_shape=[SDS((m, d), F32), SDS((1, d), F32)], name=name)(*ins)


def _swiglu_fwd(name, ag3):
    _, t_len, w = ag3.shape
    tm = _tile(t_len, (512, 256))

    def body(a_ref, g_ref, o_ref):
        o_ref[...] = _swiglu_f(a_ref[...], g_ref[...]).astype(BF16)

    return pl.pallas_call(
        body, grid=(t_len // tm, 2),
        in_specs=[pl.BlockSpec((None, tm, w), lambda i, j: (j, i, 0)), pl.BlockSpec((None, tm, w), lambda i, j: (j + 2, i, 0))],
        out_specs=pl.BlockSpec((tm, w), lambda i, j: (i, j)), out_shape=SDS((t_len, 2 * w), BF16), name=name)(ag3, ag3)


def _swiglu_bwd(name, ag3, du):
    _, t_len, w = ag3.shape
    tm = _tile(t_len, (512, 256))

    def body(a_ref, g_ref, du_ref, o_ref):
        _, vjp = jax.vjp(_swiglu_f, a_ref[...], g_ref[...])
        da, dg = vjp(du_ref[...])
        o_ref[...] = jnp.where(pl.program_id(1) < 2, da, dg).astype(BF16)

    return pl.pallas_call(
        body, grid=(t_len // tm, 4),
        in_specs=[pl.BlockSpec((None, tm, w), lambda i, s: (s % 2, i, 0)), pl.BlockSpec((None, tm, w), lambda i, s: (s % 2 + 2, i, 0)),
                  pl.BlockSpec((tm, w), lambda i, s: (i, s % 2))],
        out_specs=pl.BlockSpec((None, tm, w), lambda i, s: (s, i, 0)), out_shape=SDS(ag3.shape, BF16), name=name)(ag3, ag3, du)


def _gate_fwd(name, gp3, br, wb, l):
    _, t_len, d = gp3.shape
    tm = _tile(t_len, (512, 256))

    def body(gp_ref, br_ref, wb_ref, o_ref):
        o_ref[...] = _gate_f([gp_ref[n] for n in range(NH)], [br_ref[n] for n in range(NH)],
                             [wb_ref[n] for n in range(NH)]).astype(BF16)

    return pl.pallas_call(
        body, grid=(t_len // tm, d // BW),
        in_specs=[pl.BlockSpec((NH, tm, BW), lambda i, s: (0, i, s)), pl.BlockSpec((NH, tm, BW), lambda i, s: (0, i, 0)),
                  pl.BlockSpec((None, None, NH, BW, BW), lambda i, s: (l, s, 0, 0, 0))],
        out_specs=pl.BlockSpec((tm, BW), lambda i, s: (i, s)), out_shape=SDS((t_len, d), BF16), name=name)(gp3, br, wb)


def _gate_bwd(name, gp3, br, wb, l, dmerged):
    _, t_len, d = gp3.shape
    tm = _tile(t_len, (512, 256))
    ns = d // BW

    def body(gp_ref, br_ref, wb_ref, dm_ref, dgp_ref, dbr_ref, dwb_ref):
        i, s = pl.program_id(0), pl.program_id(1)
        gp = [gp_ref[n] for n in range(NH)]
        brv = [br_ref[n].astype(F32) for n in range(NH)]
        wbv = [wb_ref[n].astype(F32) for n in range(NH)]
        _, vjp = jax.vjp(_gate_f, gp, brv, wbv)
        dgp, dbr, dwb = vjp(dm_ref[...])

        @pl.when(s == 0)
        def _():
            dbr_ref[...] = jnp.zeros_like(dbr_ref)

        @pl.when((i == 0) & (s == 0))
        def _():
            dwb_ref[...] = jnp.zeros_like(dwb_ref)

        for n in range(NH):
            dgp_ref[n] = dgp[n].astype(BF16)
            dbr_ref[n] += dbr[n]
            dwb_ref[s, n] += dwb[n]

    return pl.pallas_call(
        body, grid=(t_len // tm, ns),
        in_specs=[pl.BlockSpec((NH, tm, BW), lambda i, s: (0, i, s)), pl.BlockSpec((NH, tm, BW), lambda i, s: (0, i, 0)),
                  pl.BlockSpec((None, None, NH, BW, BW), lambda i, s: (l, s, 0, 0, 0)), pl.BlockSpec((tm, BW), lambda i, s: (i, s))],
        out_specs=[pl.BlockSpec((NH, tm, BW), lambda i, s: (0, i, s)), pl.BlockSpec((NH, tm, BW), lambda i, s: (0, i, 0)),
                   pl.BlockSpec((ns, NH, BW, BW), lambda i, s: (0, 0, 0, 0))],
        out_shape=[SDS(gp3.shape, BF16), SDS((NH, t_len, BW), F32), SDS((ns, NH, BW, BW), F32)], name=name,
        compiler_params=_cp(VMEM_MM))(gp3, br, wb, dmerged)


def _loss_kernel(name, y, tgt):
    m, d = y.shape
    tm = _tile(m, (256,))

    def body(y_ref, t_ref, l_ref, dy_ref):
        err = y_ref[...] - t_ref[...]
        dy_ref[...] = err * (1.0 / d)

        @pl.when(pl.program_id(0) == 0)
        def _():
            l_ref[...] = jnp.zeros_like(l_ref)

        l_ref[...] += 0.5 * jnp.sum(jnp.mean(err * err, axis=-1, keepdims=True), axis=0, keepdims=True)

    row = pl.BlockSpec((tm, d), lambda i: (i, 0))
    return pl.pallas_call(body, grid=(m // tm,), in_specs=[row, row], out_specs=[pl.BlockSpec((1, 1), lambda i: (0, 0)), row],
                          out_shape=[SDS((1, 1), F32), SDS((m, d), F32)], name=name)(y, tgt)


def _zero_first(refs):
    @pl.when(pl.program_id(0) == 0)
    def _():
        for r in refs:
            r[...] = jnp.zeros_like(r)


def _grp(rows, j):
    return pl.BlockSpec((rows, BW), lambda a, j=j: (a, j))


_VEC = pl.BlockSpec((1, BW), lambda a: (0, 0))


def _ret_fwd(tag, proj, cos, sin, decf, decb, gn):
    t_len = proj.shape[0]
    n = t_len // CH
    tab = pl.BlockSpec((CH, BW), lambda a: (a, 0))
    st = pl.BlockSpec((None, BW, BW), lambda a: (a, 0, 0))

    def k1(k_ref, v_ref, cos_ref, sin_ref, df_ref, db_ref, kvf_ref, kvb_ref):
        kvf_ref[...], kvb_ref[...] = _ret_kv_f(k_ref[...], v_ref[...], cos_ref[...], sin_ref[...], df_ref[...], db_ref[...])

    kvf, kvb = pl.pallas_call(
        k1, grid=(n,), in_specs=[_grp(CH, 1), _grp(CH, 2), tab, tab, _VEC, _VEC], out_specs=[st, st],
        out_shape=[SDS((n, BW, BW), F32)] * 2, name=f"ret_kv_{tag}")(proj, proj, cos, sin, decf, decb)

    def k2(kvf_ref, kvb_ref, df_ref, db_ref, sf_ref, sb_ref):
        sf, sb = _ret_scan_f([kvf_ref[a] for a in range(n)], [kvb_ref[a] for a in range(n)], df_ref[...], db_ref[...])
        for a in range(n):
            sf_ref[a] = sf[a]
            sb_ref[a] = sb[a]

    stf, stb = pl.pallas_call(k2, out_shape=[SDS((n, BW, BW), F32)] * 2, name=f"ret_scan_{tag}",
                              compiler_params=_cp(VMEM_BIG))(kvf, kvb, decf, decb)

    def k3(q_ref, k_ref, v_ref, g_ref, cos_ref, sin_ref, sf_ref, sb_ref, df_ref, db_ref, gn_ref, o_ref):
        o_ref[...] = _ret_out_f(q_ref[...], k_ref[...], v_ref[...], g_ref[...], cos_ref[...], sin_ref[...], sf_ref[...],
                                sb_ref[...], df_ref[...], db_ref[...], gn_ref[...]).astype(BF16)

    ret = pl.pallas_call(
        k3, grid=(n,), in_specs=[_grp(CH, 0), _grp(CH, 1), _grp(CH, 2), _grp(CH, 3), tab, tab, st, st, _VEC, _VEC, _VEC],
        out_specs=tab, out_shape=SDS((t_len, BW), BF16), name=f"ret_out_{tag}")(proj, proj, proj, proj, cos, sin, stf, stb, decf, decb, gn)
    return ret, (kvf, kvb, stf, stb)


def _ret_bwd(tag, proj, cos, sin, decf, decb, gn, saved, dret):
    kvf, kvb, stf, stb = saved
    t_len = proj.shape[0]
    n = t_len // CH
    tab = pl.BlockSpec((CH, BW), lambda a: (a, 0))
    st = pl.BlockSpec((None, BW, BW), lambda a: (a, 0, 0))

    def k3(q_ref, k_ref, v_ref, g_ref, cos_ref, sin_ref, sf_ref, sb_ref, df_ref, db_ref, gn_ref, do_ref,
           dq_ref, dk_ref, dv_ref, dg_ref, dsf_ref, dsb_ref, ddf_ref, ddb_ref, dgn_ref):
        cos, sin = cos_ref[...], sin_ref[...]
        f = lambda q, k, v, g, sf, sb, df, db, gnv: _ret_out_f(q, k, v, g, cos, sin, sf, sb, df, db, gnv)
        _, vjp = jax.vjp(f, q_ref[...], k_ref[...], v_ref[...], g_ref[...], sf_ref[...], sb_ref[...], df_ref[...], db_ref[...], gn_ref[...])
        dq, dk, dv, dg, dsf, dsb, ddf, ddb, dgn = vjp(do_ref[...])
        dq_ref[...] = dq.astype(BF16)
        dk_ref[...] = dk
        dv_ref[...] = dv
        dg_ref[...] = dg.astype(BF16)
        dsf_ref[...] = dsf
        dsb_ref[...] = dsb
        _zero_first([ddf_ref, ddb_ref, dgn_ref])
        ddf_ref[...] += ddf
        ddb_ref[...] += ddb
        dgn_ref[...] += dgn

    dq, dk3, dv3, dg, dstf, dstb, ddf3, ddb3, dgn = pl.pallas_call(
        k3, grid=(n,),
        in_specs=[_grp(CH, 0), _grp(CH, 1), _grp(CH, 2), _grp(CH, 3), tab, tab, st, st, _VEC, _VEC, _VEC, tab],
        out_specs=[tab, tab, tab, tab, st, st, _VEC, _VEC, _VEC],
        out_shape=[SDS((t_len, BW), BF16), SDS((t_len, BW), F32), SDS((t_len, BW), F32), SDS((t_len, BW), BF16),
                   SDS((n, BW, BW), F32), SDS((n, BW, BW), F32), SDS((1, BW), F32), SDS((1, BW), F32), SDS((1, BW), F32)],
        name=f"ret_out_bwd_{tag}")(proj, proj, proj, proj, cos, sin, stf, stb, decf, decb, gn, dret)

    def k2(kvf_ref, kvb_ref, df_ref, db_ref, dsf_ref, dsb_ref, dkvf_ref, dkvb_ref, ddf_ref, ddb_ref):
        _, vjp = jax.vjp(_ret_scan_f, [kvf_ref[a] for a in range(n)], [kvb_ref[a] for a in range(n)], df_ref[...], db_ref[...])
        dkvf, dkvb, ddf_ref[...], ddb_ref[...] = vjp(([dsf_ref[a] for a in range(n)], [dsb_ref[a] for a in range(n)]))
        for a in range(n):
            dkvf_ref[a] = dkvf[a]
            dkvb_ref[a] = dkvb[a]

    dkvf, dkvb, ddf2, ddb2 = pl.pallas_call(
        k2, out_shape=[SDS((n, BW, BW), F32), SDS((n, BW, BW), F32), SDS((1, BW), F32), SDS((1, BW), F32)],
        name=f"ret_scan_bwd_{tag}", compiler_params=_cp(VMEM_BIG))(kvf, kvb, decf, decb, dstf, dstb)

    def k1(k_ref, v_ref, cos_ref, sin_ref, df_ref, db_ref, dkvf_ref, dkvb_ref, dk3_ref, dv3_ref, dk_ref, dv_ref, ddf_ref, ddb_ref):
        cos, sin = cos_ref[...], sin_ref[...]
        f = lambda k, v, df, db: _ret_kv_f(k, v, cos, sin, df, db)
        _, vjp = jax.vjp(f, k_ref[...], v_ref[...], df_ref[...], db_ref[...])
        dk, dv, ddf, ddb = vjp((dkvf_ref[...], dkvb_ref[...]))
        dk_ref[...] = (dk + dk3_ref[...]).astype(BF16)
        dv_ref[...] = (dv + dv3_ref[...]).astype(BF16)
        _zero_first([ddf_ref, ddb_ref])
        ddf_ref[...] += ddf
        ddb_ref[...] += ddb

    dk, dv, ddf1, ddb1 = pl.pallas_call(
        k1, grid=(n,), in_specs=[_grp(CH, 1), _grp(CH, 2), tab, tab, _VEC, _VEC, st, st, tab, tab],
        out_specs=[tab, tab, _VEC, _VEC],
        out_shape=[SDS((t_len, BW), BF16), SDS((t_len, BW), BF16), SDS((1, BW), F32), SDS((1, BW), F32)],
        name=f"ret_kv_bwd_{tag}")(proj, proj, cos, sin, decf, decb, dkvf, dkvb, dk3, dv3)
    return dq, dk, dv, dg, ddf1 + ddf2 + ddf3, ddb1 + ddb2 + ddb3, dgn


def _pool_fwd(tag, pvp, wbd, scale, t_len):
    def body(p_ref, w_ref, s_ref, o_ref):
        o_ref[...] = _pool_f(p_ref[...], w_ref[...], s_ref[...], t_len).astype(BF16)

    return pl.pallas_call(body, out_shape=SDS(pvp.shape, BF16), name=f"pool_{tag}", compiler_params=_cp(VMEM_BIG))(pvp, wbd, scale)


def _pool_bwd(tag, pvp, wbd, scale, t_len, dpool_p):
    def body(p_ref, w_ref, s_ref, do_ref, dp_ref, dw_ref, ds_ref):
        f = lambda p, w, s: _pool_f(p, w, s, t_len)
        _, vjp = jax.vjp(f, p_ref[...], w_ref[...], s_ref[...])
        dp, dw, ds = vjp(do_ref[...])
        dp_ref[...] = dp.astype(BF16)
        dw_ref[...] = dw
        ds_ref[...] = ds

    return pl.pallas_call(body, out_shape=[SDS(pvp.shape, BF16), SDS((BW, BW), F32), SDS((1, BW), F32)],
                          name=f"pool_bwd_{tag}", compiler_params=_cp(VMEM_BIG))(pvp, wbd, scale, dpool_p)


def _na_window(r, rows):
    r0 = jnp.clip(r - NA_ROWS // 2, 0, rows - NA_ROWS)
    return r0, r - r0


def _na_fwd(tag, proj, bias_tab, gq, gk):
    t_len = proj.shape[0]
    rows = t_len // GRID_W
    win = NA_ROWS * GRID_W

    def body(q_ref, k_ref, v_ref, b_ref, gq_ref, gk_ref, o_ref):
        r0, pat = _na_window(pl.program_id(0), rows)
        start = pl.multiple_of(r0 * GRID_W, GRID_W)
        o_ref[...] = _na_f(q_ref[...], k_ref[pl.ds(start, win), :], v_ref[pl.ds(start, win), :], b_ref[pat],
                           gq_ref[...], gk_ref[...]).astype(BF16)

    full = lambda j: pl.BlockSpec((t_len, BW), lambda a, j=j: (0, j))
    return pl.pallas_call(
        body, grid=(rows,),
        in_specs=[_grp(GRID_W, 5), full(6), full(7), pl.BlockSpec(bias_tab.shape, lambda a: (0, 0, 0, 0)), _VEC, _VEC],
        out_specs=pl.BlockSpec((GRID_W, BW), lambda a: (a, 0)), out_shape=SDS((t_len, BW), BF16),
        name=f"na_{tag}", compiler_params=_cp(VMEM_MM))(proj, proj, proj, bias_tab, gq, gk)


def _na_bwd(tag, proj, bias_tab, gq, gk, dna):
    t_len = proj.shape[0]
    rows = t_len // GRID_W
    win = NA_ROWS * GRID_W

    def body(q_ref, k_ref, v_ref, b_ref, gq_ref, gk_ref, do_ref, dq_ref, dk_ref, dv_ref, db_ref, dgq_ref, dgk_ref):
        r0, pat = _na_window(pl.program_id(0), rows)
        start = pl.multiple_of(r0 * GRID_W, GRID_W)
        _zero_first([dk_ref, dv_ref, db_ref, dgq_ref, dgk_ref])
        _, vjp = jax.vjp(_na_f, q_ref[...], k_ref[pl.ds(start, win), :], v_ref[pl.ds(start, win), :], b_ref[pat], gq_ref[...], gk_ref[...])
        dq, dk, dv, db, dgq, dgk = vjp(do_ref[...])
        dq_ref[...] = dq.astype(BF16)
        dk_ref[pl.ds(start, win), :] += dk
        dv_ref[pl.ds(start, win), :] += dv
        db_ref[pat] += db
        dgq_ref[...] += dgq
        dgk_ref[...] += dgk

    full = lambda j: pl.BlockSpec((t_len, BW), lambda a, j=j: (0, j))
    whole = pl.BlockSpec((t_len, BW), lambda a: (0, 0))
    tabspec = pl.BlockSpec(bias_tab.shape, lambda a: (0, 0, 0, 0))
    row = pl.BlockSpec((GRID_W, BW), lambda a: (a, 0))
    return pl.pallas_call(
        body, grid=(rows,), in_specs=[_grp(GRID_W, 5), full(6), full(7), tabspec, _VEC, _VEC, row],
        out_specs=[row, whole, whole, tabspec, _VEC, _VEC],
        out_shape=[SDS((t_len, BW), BF16), SDS((t_len, BW), F32), SDS((t_len, BW), F32), SDS(bias_tab.shape, F32),
                   SDS((1, BW), F32), SDS((1, BW), F32)],
        name=f"na_bwd_{tag}", compiler_params=_cp(VMEM_BIG))(proj, proj, proj, bias_tab, gq, gk, dna)


def _mem_fwd(tag, proj, memkv, gq, gk):
    t_len = proj.shape[0]
    n_mem = memkv.shape[0]
    tm = _tile(t_len, (256,))

    def body(q_ref, mk_ref, mv_ref, gq_ref, gk_ref, o_ref):
        o_ref[...] = _mem_f(q_ref[...], mk_ref[...], mv_ref[...], gq_ref[...], gk_ref[...]).astype(BF16)

    mspec = lambda j: pl.BlockSpec((n_mem, BW), lambda a, j=j: (0, j))
    return pl.pallas_call(
        body, grid=(t_len // tm,), in_specs=[_grp(tm, 8), mspec(0), mspec(1), _VEC, _VEC],
        out_specs=pl.BlockSpec((tm, BW), lambda a: (a, 0)), out_shape=SDS((t_len, BW), BF16), name=f"mem_{tag}")(proj, memkv, memkv, gq, gk)


def _mem_bwd(tag, proj, memkv, gq, gk, dmo):
    t_len = proj.shape[0]
    n_mem = memkv.shape[0]
    tm = _tile(t_len, (256,))

    def body(q_ref, mk_ref, mv_ref, gq_ref, gk_ref, do_ref, dq_ref, dmk_ref, dmv_ref, dgq_ref, dgk_ref):
        _zero_first([dmk_ref, dmv_ref, dgq_ref, dgk_ref])
        _, vjp = jax.vjp(_mem_f, q_ref[...], mk_ref[...], mv_ref[...], gq_ref[...], gk_ref[...])
        dq, dmk, dmv, dgq, dgk = vjp(do_ref[...])
        dq_ref[...] = dq.astype(BF16)
        dmk_ref[...] += dmk
        dmv_ref[...] += dmv
        dgq_ref[...] += dgq
        dgk_ref[...] += dgk

    mspec = lambda j: pl.BlockSpec((n_mem, BW), lambda a, j=j: (0, j))
    mout = pl.BlockSpec((n_mem, BW), lambda a: (0, 0))
    row = pl.BlockSpec((tm, BW), lambda a: (a, 0))
    dq, dmk, dmv, dgq, dgk = pl.pallas_call(
        body, grid=(t_len // tm,), in_specs=[_grp(tm, 8), mspec(0), mspec(1), _VEC, _VEC, row],
        out_specs=[row, mout, mout, _VEC, _VEC],
        out_shape=[SDS((t_len, BW), BF16), SDS((n_mem, BW), F32), SDS((n_mem, BW), F32), SDS((1, BW), F32), SDS((1, BW), F32)],
        name=f"mem_bwd_{tag}")(proj, memkv, memkv, gq, gk, dmo)
    return dq, jnp.concatenate([dmk, dmv], axis=1), dgq, dgk


def _na_index():
    q = np.arange(GRID_W)[:, None]
    kc = np.arange(GRID_W)[None, :]
    qwin = np.clip(q - NA_COLS // 2, 0, GRID_W - NA_COLS)
    mask = (kc >= qwin) & (kc < qwin + NA_COLS)
    col = np.clip(kc - q, -(NA_COLS - 1), NA_COLS - 1) + NA_COLS - 1
    row = np.arange(NA_ROWS)[None, :] - np.arange(NA_ROWS)[:, None] + NA_ROWS - 1
    return mask, col, row


def _na_bias_table(rpb):
    mask, col, row = _na_index()
    t = rpb[:, row][:, :, :, col]
    t = jnp.where(mask[None, None, None], t, NEG_INF)
    return t.transpose(1, 0, 3, 2, 4).reshape(NA_ROWS, NH, GRID_W, NA_ROWS * GRID_W)


def _na_bias_grad(tag, dtab):
    mask, col, row = _na_index()
    nj, ni = 2 * NA_COLS - 1, 2 * NA_ROWS - 1
    onehot_col = np.zeros((GRID_W * GRID_W, LANES), np.float32)
    onehot_col[np.arange(GRID_W * GRID_W)[mask.ravel()], col.ravel()[mask.ravel()]] = 1.0
    onehot_row = np.zeros((NH * ni + 4, NA_ROWS * NH * NA_ROWS), np.float32)
    for p in range(NA_ROWS):
        for h in range(NH):
            for w in range(NA_ROWS):
                onehot_row[h * ni + row[p, w], (p * NH + h) * NA_ROWS + w] = 1.0
    x = dtab.reshape(NA_ROWS, NH, GRID_W, NA_ROWS, GRID_W).transpose(0, 1, 3, 2, 4).reshape(NA_ROWS * NH * NA_ROWS, GRID_W * GRID_W)

    def body(x_ref, c_ref, r_ref, o_ref):
        y = jnp.dot(x_ref[...], c_ref[...], precision=lax.Precision.HIGHEST, preferred_element_type=F32)
        o_ref[...] = jnp.dot(r_ref[...], y, precision=lax.Precision.HIGHEST, preferred_element_type=F32)

    out = pl.pallas_call(body, out_shape=SDS((NH * ni + 4, LANES), F32), name=f"na_bias_grad_{tag}")(
        x, jnp.asarray(onehot_col), jnp.asarray(onehot_row))
    return out[:NH * ni, :nj].reshape(NH, ni, nj)


def _place():
    x, y, c = lax.axis_index("x"), lax.axis_index("y"), lax.axis_index("c")
    chips = [(1 - x, y), (x, 1 - y), (1 - x, 1 - y)]
    return x, y, c, chips


def _remote(src, dst, ssem, rsem, dev):
    return pltpu.make_async_remote_copy(src_ref=src, dst_ref=dst, send_sem=ssem, recv_sem=rsem, device_id=dev, device_id_type=MESH)


def _all_gather_shards(ws):
    n = len(ws)

    def body(*refs):
        ins, outs = refs[:n], refs[n:2 * n]
        lsem, ssem, rsem, fssem, frsem = refs[2 * n:]
        x, y, c, chips = _place()
        me = 2 * x + y
        for i in range(n):
            h = ins[i].shape[0] // 2
            mine = pl.ds(c * h, h)
            pltpu.make_async_copy(ins[i], outs[i].at[:, me], lsem.at[i]).start()
            for r, (px, py) in enumerate(chips):
                _remote(ins[i].at[mine], outs[i].at[mine, me], ssem.at[i, r], rsem.at[i, r], (px, py, c)).start()
        for i in range(n):
            h = ins[i].shape[0] // 2
            mine = pl.ds(c * h, h)
            for r, (px, py) in enumerate(chips):
                j = 2 * px + py
                land = outs[i].at[mine, j]
                _remote(land, land, ssem.at[i, r], rsem.at[i, r], (px, py, c)).wait_recv()
                _remote(land, land, fssem.at[i, r], frsem.at[i, r], (x, y, 1 - c)).start()
        for i in range(n):
            h = ins[i].shape[0] // 2
            theirs = pl.ds((1 - c) * h, h)
            for r, (px, py) in enumerate(chips):
                land = outs[i].at[theirs, 2 * px + py]
                _remote(land, land, fssem.at[i, r], frsem.at[i, r], (x, y, 1 - c)).wait_recv()
        for i in range(n):
            h = ins[i].shape[0] // 2
            mine = pl.ds(c * h, h)
            pltpu.make_async_copy(ins[i], outs[i].at[:, me], lsem.at[i]).wait()
            for r, (px, py) in enumerate(chips):
                land = outs[i].at[mine, 2 * px + py]
                _remote(ins[i].at[mine], outs[i].at[mine, me], ssem.at[i, r], rsem.at[i, r], (px, py, c)).wait_send()
                _remote(land, land, fssem.at[i, r], frsem.at[i, r], (x, y, 1 - c)).wait_send()

    sem = lambda: pltpu.SemaphoreType.DMA((n, 3))
    return pl.pallas_call(
        body, in_specs=[ANY] * n, out_specs=[ANY] * n,
        out_shape=[SDS((w.shape[0], N_CHIPS) + w.shape[1:], w.dtype) for w in ws],
        scratch_shapes=[pltpu.SemaphoreType.DMA((n,)), sem(), sem(), sem(), sem()], name="all_gather_weights")(*ws)


def _pair_exchange(name, xs, full):
    n = len(xs)

    def body(*refs):
        ins, outs = refs[:n], refs[n:2 * n]
        lsem, ssem, rsem = refs[2 * n:]
        x, y, c, _ = _place()
        sib = (x, y, 1 - c)
        cps = []
        for i in range(n):
            if full:
                h = ins[i].shape[0]
                loc = pltpu.make_async_copy(ins[i], outs[i].at[pl.ds(c * h, h)], lsem.at[i])
                loc.start()
                cp = _remote(ins[i], outs[i].at[pl.ds(c * h, h)], ssem.at[i], rsem.at[i], sib)
            else:
                h = ins[i].shape[0] // 2
                loc = None
                cp = _remote(ins[i].at[pl.ds((1 - c) * h, h)], outs[i], ssem.at[i], rsem.at[i], sib)
            cp.start()
            cps.append((loc, cp))
        for i in range(n):
            loc, cp = cps[i]
            if full:
                h = ins[i].shape[0]
                land = outs[i].at[pl.ds((1 - c) * h, h)]
                _remote(land, land, ssem.at[i], rsem.at[i], sib).wait_recv()
                loc.wait()
            else:
                _remote(outs[i], outs[i], ssem.at[i], rsem.at[i], sib).wait_recv()
            cp.wait_send()

    def oshape(a):
        return SDS(((2 * a.shape[0],) if full else (a.shape[0] // 2,)) + a.shape[1:], a.dtype)

    return pl.pallas_call(
        body, in_specs=[ANY] * n, out_specs=[ANY] * n, out_shape=[oshape(a) for a in xs],
        scratch_shapes=[pltpu.SemaphoreType.DMA((n,)), pltpu.SemaphoreType.DMA((n,)), pltpu.SemaphoreType.DMA((n,))], name=name)(*xs)


def _chip_scatter(ps):
    n = len(ps)

    def body(*refs):
        ins, outs = refs[:n], refs[n:2 * n]
        ssem, rsem = refs[2 * n:]
        x, y, c, chips = _place()
        cps = []
        for i in range(n):
            for r, (px, py) in enumerate(chips):
                cp = _remote(ins[i].at[:, 2 * px + py], outs[i].at[r], ssem.at[i, r], rsem.at[i, r], (px, py, c))
                cp.start()
                cps.append(cp)
        for cp in cps:
            cp.wait()

    return pl.pallas_call(
        body, in_specs=[ANY] * n, out_specs=[ANY] * n,
        out_shape=[SDS((3, p.shape[0]) + p.shape[2:], p.dtype) for p in ps],
        scratch_shapes=[pltpu.SemaphoreType.DMA((n, 3)), pltpu.SemaphoreType.DMA((n, 3))], name="grad_scatter")(*ps)


def _gather_small(v):
    def body(v_ref, o_ref, lsem, ssem, rsem):
        x, y, c, _ = _place()
        me = 4 * x + 2 * y + c
        flips = [(fx, fy, fc) for fx in (0, 1) for fy in (0, 1) for fc in (0, 1)][1:]
        peer = lambda f: (1 - x if f[0] else x, 1 - y if f[1] else y, 1 - c if f[2] else c)
        loc = pltpu.make_async_copy(v_ref, o_ref.at[me], lsem)
        loc.start()
        cps = [_remote(v_ref, o_ref.at[me], ssem.at[k], rsem.at[k], peer(f)) for k, f in enumerate(flips)]
        for cp in cps:
            cp.start()
        for k, f in enumerate(flips):
            px, py, pc = peer(f)
            land = o_ref.at[4 * px + 2 * py + pc]
            _remote(land, land, ssem.at[k], rsem.at[k], peer(f)).wait_recv()
        for cp in cps:
            cp.wait_send()
        loc.wait()

    return pl.pallas_call(
        body, in_specs=[ANY], out_specs=ANY, out_shape=SDS((N_DEV,) + v.shape, v.dtype),
        scratch_shapes=[pltpu.SemaphoreType.DMA, pltpu.SemaphoreType.DMA((N_DEV - 1,)), pltpu.SemaphoreType.DMA((N_DEV - 1,))],
        name="gather_small")(v)


def _rows_tile(r):
    return _tile(r, (256, 352, 128, 64))


def _pair_sum(name, g, r1, half_start, me):
    lh, s_n, r, c = r1.shape
    tr = _rows_tile(r)

    def body(idx_ref, g_ref, r_ref, gm_ref, rm_ref, pb_ref, own_ref):
        pb_ref[...] = (g_ref[...] + r_ref[...]).astype(BF16)
        own_ref[...] = gm_ref[...] + rm_ref[...]

    blk = (None, None, tr, c)
    grid_spec = pltpu.PrefetchScalarGridSpec(
        num_scalar_prefetch=1, grid=(lh, r // tr, s_n),
        in_specs=[pl.BlockSpec(blk, lambda a, i, s, idx: (idx[0] + a, s, i, 0)), pl.BlockSpec(blk, lambda a, i, s, idx: (a, s, i, 0)),
                  pl.BlockSpec(blk, lambda a, i, s, idx: (idx[0] + a, idx[1], i, 0)), pl.BlockSpec(blk, lambda a, i, s, idx: (a, idx[1], i, 0))],
        out_specs=[pl.BlockSpec(blk, lambda a, i, s, idx: (a, s, i, 0)), pl.BlockSpec((None, tr, c), lambda a, i, s, idx: (a, i, 0))])
    idx = jnp.stack([half_start, me]).astype(jnp.int32)
    return pl.pallas_call(body, grid_spec=grid_spec, out_shape=[SDS(r1.shape, BF16), SDS((lh, r, c), F32)], name=name)(idx, g, r1, g, r1)


def _chip_sum(name, own, r2):
    lh, r, c = own.shape
    tr = _rows_tile(r)

    def body(o_ref, r_ref, f_ref):
        f_ref[...] = ((o_ref[...] + r_ref[0].astype(F32)) + r_ref[1].astype(F32)) + r_ref[2].astype(F32)

    return pl.pallas_call(
        body, grid=(lh, r // tr),
        in_specs=[pl.BlockSpec((None, tr, c), lambda a, i: (a, i, 0)), pl.BlockSpec((3, None, tr, c), lambda a, i: (0, a, i, 0))],
        out_specs=pl.BlockSpec((None, tr, c), lambda a, i: (a, i, 0)), out_shape=SDS(own.shape, F32), name=name)(own, r2)


def _adamw_math(w, g, m, v):
    m = ADAM_B1 * m + (1.0 - ADAM_B1) * g
    v = ADAM_B2 * v + (1.0 - ADAM_B2) * jnp.square(g)
    m_hat = m / (1.0 - ADAM_B1 ** ADAM_STEP)
    v_hat = v / (1.0 - ADAM_B2 ** ADAM_STEP)
    delta = -ADAM_LR * (m_hat / (jnp.sqrt(v_hat) + ADAM_EPS) + ADAM_WD * w)
    return delta, m, v


def _adamw(name, w, g, m, v):
    lh, r, c = w.shape
    tr = _rows_tile(r)

    def body(w_ref, g_ref, m_ref, v_ref, d_ref, nm_ref, nv_ref):
        d_ref[...], nm_ref[...], nv_ref[...] = _adamw_math(w_ref[...], g_ref[...], m_ref[...], v_ref[...])

    blk = pl.BlockSpec((None, tr, c), lambda a, i: (a, i, 0))
    return pl.pallas_call(body, grid=(lh, r // tr), in_specs=[blk] * 4, out_specs=[blk] * 3,
                          out_shape=[SDS(w.shape, F32)] * 3, name=name)(w, g, m, v)


def _small_update(parts, w, m, v):
    def body(p_ref, w_ref, m_ref, v_ref, g_ref, d_ref, nm_ref, nv_ref):
        g = p_ref[0]
        for d in range(1, N_DEV):
            g = g + p_ref[d]
        g_ref[...] = g
        d_ref[...], nm_ref[...], nv_ref[...] = _adamw_math(w_ref[...], g, m_ref[...], v_ref[...])

    return pl.pallas_call(body, out_shape=[SDS(w.shape, F32)] * 4, name="small_update")(parts, w, m, v)


SMALL = ["norm_mix_g", "norm_mem_g", "ret_decay_fwd", "ret_decay_bwd", "ret_norm_g", "pool_w", "pool_scale", "na_q_norm_g",
         "na_k_norm_g", "na_rpb", "mem_q_norm_g", "mem_k_norm_g", "norm_ffn_g"]
BIG = ["w_in", "w_gate", "w_mem_kv", "w_branch", "w_out", "w_ffn_in", "w_ffn_out"]
ORDER = ["norm_mix_g", "norm_mem_g", "w_in", "w_gate", "ret_decay_fwd", "ret_decay_bwd", "ret_norm_g", "pool_w", "pool_scale",
         "na_q_norm_g", "na_k_norm_g", "na_rpb", "mem_q_norm_g", "mem_k_norm_g", "w_mem_kv", "w_branch", "w_out", "norm_ffn_g",
         "w_ffn_in", "w_ffn_out"]


def _pack(arrs):
    rows = []
    for a in arrs:
        f = a.reshape(-1).astype(F32)
        pad = (-f.shape[0]) % (8 * LANES)
        rows.append(jnp.pad(f, (0, pad)).reshape(-1, LANES))
    return jnp.concatenate(rows, axis=0)


def _unpack(p, like):
    out, at = [], 0
    for a in like:
        n = int(np.prod(a.shape))
        rows = -(-n // (8 * LANES)) * 8
        out.append(p[at:at + rows].reshape(-1)[:n].reshape(a.shape))
        at += rows
    return out


def _rope_tables(t_len):
    half = HEAD // 2
    inv = ROPE_THETA ** (-jnp.arange(half, dtype=F32) / half)
    ang = jnp.arange(t_len, dtype=F32)[:, None] * inv[None, :]
    cos, sin = jnp.cos(ang), jnp.sin(ang)
    return jnp.tile(jnp.concatenate([cos, cos], axis=1), (1, NH)), jnp.tile(jnp.concatenate([-sin, sin], axis=1), (1, NH))


def _block_diag(pw):
    out = jnp.zeros((BW, BW), pw.dtype)
    for g in range(NH):
        out = out.at[g * HEAD:(g + 1) * HEAD, g * HEAD:(g + 1) * HEAD].set(pw[g])
    return out


def _fold_heads(v):
    return v.reshape(NH, HEAD).sum(axis=0)


def _fold_lanes(v):
    return v.reshape(NH, HEAD).sum(axis=1)


def kernel(x, mem, norm_mix_g, norm_mem_g, w_in, w_gate, ret_decay_fwd, ret_decay_bwd, ret_norm_g, pool_w, pool_scale, na_q_norm_g, na_k_norm_g, na_rpb, mem_q_norm_g, mem_k_norm_g, w_mem_kv, w_branch, w_out, norm_ffn_g, w_ffn_in, w_ffn_out, loss_target, m_norm_mix_g, m_norm_mem_g, m_w_in, m_w_gate, m_ret_decay_fwd, m_ret_decay_bwd, m_ret_norm_g, m_pool_w, m_pool_scale, m_na_q_norm_g, m_na_k_norm_g, m_na_rpb, m_mem_q_norm_g, m_mem_k_norm_g, m_w_mem_kv, m_w_branch, m_w_out, m_norm_ffn_g, m_w_ffn_in, m_w_ffn_out, v_norm_mix_g, v_norm_mem_g, v_w_in, v_w_gate, v_ret_decay_fwd, v_ret_decay_bwd, v_ret_norm_g, v_pool_w, v_pool_scale, v_na_q_norm_g, v_na_k_norm_g, v_na_rpb, v_mem_q_norm_g, v_mem_k_norm_g, v_w_mem_kv, v_w_branch, v_w_out, v_norm_ffn_g, v_w_ffn_in, v_w_ffn_out):
    args = dict(locals())
    W = {n: args[n] for n in ORDER}
    M = {n: args["m_" + n] for n in ORDER}
    V = {n: args["v_" + n] for n in ORDER}
    n_layers, d_model = norm_mix_g.shape
    assert n_layers % 2 == 0 and x.shape[0] == 1
    t_len = x.shape[1]
    xc, yc, cc = lax.axis_index("x"), lax.axis_index("y"), lax.axis_index("c")
    me_chip = 2 * xc + yc
    half_start = cc * (n_layers // 2)

    def as3(a):
        return a.reshape(a.shape[0], -1, a.shape[-1])

    shards = [as3(W[n]).astype(BF16) for n in BIG]
    gathered = dict(zip(BIG, _all_gather_shards(shards)))
    wi = gathered["w_in"]
    wg = gathered["w_gate"]
    wmkv = gathered["w_mem_kv"].reshape(n_layers, 1, d_model, 2 * BW)
    wb = gathered["w_branch"].reshape(n_layers, N_CHIPS, NH, BW, BW)
    wo = gathered["w_out"].reshape(n_layers, 1, d_model, d_model)
    wfi = gathered["w_ffn_in"]
    wfo = gathered["w_ffn_out"].reshape(n_layers, 1, -1, d_model)
    ff = wfo.shape[2]

    cos, sin = _rope_tables(t_len)
    lane = lambda a: a.reshape(1, -1).astype(F32)
    x2d, mem2d, tgt = x[0], mem[0], loss_target[0]

    saved = []
    for l in range(n_layers):
        tag = f"l{l}"
        s = {"x": x2d}
        s["decf"], s["decb"] = lane(jnp.repeat(ret_decay_fwd[l], HEAD)), lane(jnp.repeat(ret_decay_bwd[l], HEAD))
        s["gn"], s["pscale"] = lane(ret_norm_g[l]), lane(pool_scale[l])
        s["wbd"] = _block_diag(pool_w[l])
        s["nagq"], s["nagk"] = lane(jnp.tile(na_q_norm_g[l], NH)), lane(jnp.tile(na_k_norm_g[l], NH))
        s["mgq"], s["mgk"] = lane(jnp.tile(mem_q_norm_g[l], NH)), lane(jnp.tile(mem_k_norm_g[l], NH))
        s["btab"] = _na_bias_table(na_rpb[l])
        s["h"] = _rms_fwd(f"rms_mix_{tag}", x2d, lane(norm_mix_g[l]))
        proj3 = _mm_nn(f"mm_in_{tag}", s["h"], wi, l, out3d=True)
        s["proj"] = proj3.transpose(1, 0, 2).reshape(t_len, -1)
        s["gp3"] = _mm_nn(f"mm_gate_{tag}", s["h"], wg, l, out3d=True)
        s["hm"] = _rms_fwd(f"rms_mem_{tag}", mem2d, lane(norm_mem_g[l]))
        s["memkv"] = _mm_nn(f"mm_memkv_{tag}", s["hm"], wmkv, l)
        ret, s["ret_saved"] = _ret_fwd(tag, s["proj"], cos, sin, s["decf"], s["decb"], s["gn"])
        s["pvp"] = jnp.pad(s["proj"][:, 4 * BW:5 * BW], ((POOL_PAD, POOL_PAD), (0, 0)))
        pool = _pool_fwd(tag, s["pvp"], s["wbd"], s["pscale"], t_len)[POOL_PAD:POOL_PAD + t_len]
        na = _na_fwd(tag, s["proj"], s["btab"], s["nagq"], s["nagk"])
        mo = _mem_fwd(tag, s["proj"], s["memkv"], s["mgq"], s["mgk"])
        s["br"] = jnp.stack([ret, pool, na, mo])
        s["merged"] = _gate_fwd(f"gate_{tag}", s["gp3"], s["br"], wb, l)
        s["x2"] = _mm_nn(f"mm_out_{tag}", s["merged"], wo, l, add=x2d)
        s["h2"] = _rms_fwd(f"rms_ffn_{tag}", s["x2"], lane(norm_ffn_g[l]))
        s["ag3"] = _mm_nn(f"mm_ffn_in_{tag}", s["h2"], wfi, l, out3d=True)
        s["u"] = _swiglu_fwd(f"swiglu_{tag}", s["ag3"])
        x2d = _mm_nn(f"mm_ffn_out_{tag}", s["u"], wfo, l, add=s["x2"])
        saved.append(s)

    loss_part, dx = _loss_kernel("loss", x2d, tgt)

    def empty(shape):
        return jnp.zeros(shape, F32)

    g_wi, g_wg = empty(wi.shape), empty(wg.shape)
    g_wmkv, g_wo, g_wfi, g_wfo = empty(wmkv.shape), empty(wo.shape), empty(wfi.shape), empty(wfo.shape)
    g_wb = [None] * n_layers
    gsmall = {n: [None] * n_layers for n in SMALL}
    for l in reversed(range(n_layers)):
        tag = f"l{l}"
        s = saved[l]
        du = _mm_nt(f"mm_ffn_out_dx_{tag}", dx, wfo, l)
        g_wfo = _mm_tn(f"mm_ffn_out_dw_{tag}", s["u"], dx, g_wfo, l)
        dag3 = _swiglu_bwd(f"swiglu_bwd_{tag}", s["ag3"], du)
        dh2 = _mm_nt(f"mm_ffn_in_dx_{tag}", dag3, wfi, l)
        g_wfi = _mm_tn(f"mm_ffn_in_dw_{tag}", s["h2"], dag3, g_wfi, l)
        dx2, dg = _rms_bwd(f"rms_ffn_bwd_{tag}", s["x2"], lane(norm_ffn_g[l]), [dh2], add=dx)
        gsmall["norm_ffn_g"][l] = dg[0]
        dmerged = _mm_nt(f"mm_out_dx_{tag}", dx2, wo, l)
        g_wo = _mm_tn(f"mm_out_dw_{tag}", s["merged"], dx2, g_wo, l)
        dgp3, dbr, g_wb[l] = _gate_bwd(f"gate_bwd_{tag}", s["gp3"], s["br"], wb, l, dmerged)
        g_wg = _mm_tn(f"mm_gate_dw_{tag}", s["h"], dgp3, g_wg, l)
        dh_a = _mm_nt(f"mm_gate_dx_{tag}", dgp3, wg, l)
        drq, drk, drv, drg, ddf, ddb, dgn = _ret_bwd(tag, s["proj"], cos, sin, s["decf"], s["decb"], s["gn"], s["ret_saved"], dbr[0])
        gsmall["ret_decay_fwd"][l], gsmall["ret_decay_bwd"][l], gsmall["ret_norm_g"][l] = _fold_lanes(ddf), _fold_lanes(ddb), dgn[0]
        dpool_p = jnp.pad(dbr[1], ((POOL_PAD, POOL_PAD), (0, 0)))
        dpvp, dwbd, dps = _pool_bwd(tag, s["pvp"], s["wbd"], s["pscale"], t_len, dpool_p)
        dpv = dpvp[POOL_PAD:POOL_PAD + t_len]
        gsmall["pool_w"][l] = jnp.stack([dwbd[g * HEAD:(g + 1) * HEAD, g * HEAD:(g + 1) * HEAD] for g in range(NH)])
        gsmall["pool_scale"][l] = dps[0]
        dnq, dnk, dnv, dbtab, dgq, dgk = _na_bwd(tag, s["proj"], s["btab"], s["nagq"], s["nagk"], dbr[2])
        gsmall["na_rpb"][l] = _na_bias_grad(tag, dbtab)
        gsmall["na_q_norm_g"][l], gsmall["na_k_norm_g"][l] = _fold_heads(dgq), _fold_heads(dgk)
        dmq, dmemkv, dgq, dgk = _mem_bwd(tag, s["proj"], s["memkv"], s["mgq"], s["mgk"], dbr[3])
        gsmall["mem_q_norm_g"][l], gsmall["mem_k_norm_g"][l] = _fold_heads(dgq), _fold_heads(dgk)
        g_wmkv = _mm_tn(f"mm_memkv_dw_{tag}", s["hm"], dmemkv, g_wmkv, l)
        dhm = _mm_nt(f"mm_memkv_dx_{tag}", dmemkv, wmkv, l)
        _, dg = _rms_bwd(f"rms_mem_bwd_{tag}", mem2d, lane(norm_mem_g[l]), [dhm])
        gsmall["norm_mem_g"][l] = dg[0]
        dproj = jnp.concatenate([drq, drk, drv, drg, dpv, dnq, dnk.astype(BF16), dnv.astype(BF16), dmq], axis=1)
        dproj3 = dproj.reshape(t_len, N_CHIPS, -1).transpose(1, 0, 2)
        g_wi = _mm_tn(f"mm_in_dw_{tag}", s["h"], dproj3, g_wi, l)
        dh_b = _mm_nt(f"mm_in_dx_{tag}", dproj3, wi, l)
        dx, dg = _rms_bwd(f"rms_mix_bwd_{tag}", s["x"], lane(norm_mix_g[l]), [dh_a, dh_b], add=dx2)
        gsmall["norm_mix_g"][l] = dg[0]

    full_grads = {
        "w_in": g_wi, "w_gate": g_wg, "w_mem_kv": g_wmkv.reshape(n_layers, N_CHIPS, -1, 2 * BW),
        "w_branch": jnp.stack(g_wb).reshape(n_layers, N_CHIPS, NH * BW, BW), "w_out": g_wo.reshape(n_layers, N_CHIPS, -1, d_model),
        "w_ffn_in": g_wfi, "w_ffn_out": g_wfo.reshape(n_layers, N_CHIPS, ff // N_CHIPS, d_model),
    }
    gs = [full_grads[n] for n in BIG]
    r1 = _pair_exchange("grad_pair_exchange", gs, full=False)
    sums = [_pair_sum(f"pair_sum_{n}", g, r, half_start, me_chip) for n, g, r in zip(BIG, gs, r1)]
    r2 = _chip_scatter([p for p, _ in sums])
    halves = [_chip_sum(f"chip_sum_{n}", own, r) for n, (_, own), r in zip(BIG, sums, r2)]
    reduced = dict(zip(BIG, _pair_exchange("grad_half_exchange", halves, full=True)))

    out_g, out_d, out_m, out_v = {}, {}, {}, {}
    for n in BIG:
        shp = W[n].shape
        d, nm, nv = _adamw(f"adamw_{n}", as3(W[n]), reduced[n], as3(M[n]), as3(V[n]))
        out_g[n], out_d[n], out_m[n], out_v[n] = reduced[n].reshape(shp), d.reshape(shp), nm.reshape(shp), nv.reshape(shp)

    small_g = [jnp.stack(gsmall[n]).reshape(W[n].shape) for n in SMALL] + [loss_part]
    like = [W[n] for n in SMALL] + [loss_part]
    zero = jnp.zeros((1, 1), F32)
    parts = _gather_small(_pack(small_g))
    sg, sd, sm, sv = _small_update(parts, _pack([W[n] for n in SMALL] + [zero]), _pack([M[n] for n in SMALL] + [zero]),
                                   _pack([V[n] for n in SMALL] + [zero]))
    sg, sd, sm, sv = _unpack(sg, like), _unpack(sd, like), _unpack(sm, like), _unpack(sv, like)
    for i, n in enumerate(SMALL):
        out_g[n], out_d[n], out_m[n], out_v[n] = sg[i], sd[i], sm[i], sv[i]
    loss = sg[-1].reshape(())

    return (loss, dx.reshape(x.shape), *[out_g[n] for n in ORDER], *[out_d[n] for n in ORDER],
            *[out_m[n] for n in ORDER], *[out_v[n] for n in ORDER])
```

```python
import functools

import numpy as np
import jax
import jax.numpy as jnp
from jax import lax
from jax.experimental import pallas as pl
from jax.experimental.pallas import tpu as pltpu

F32 = jnp.float32
BF16 = jnp.bfloat16
SDS = jax.ShapeDtypeStruct
MESH = pl.DeviceIdType.MESH
ANY = pl.BlockSpec(memory_space=pl.ANY)

HEAD = 64
NH = 4
BW = NH * HEAD
CH = 128
GRID_W = 64
NA_ROWS = 8
NA_COLS = 16
POOL_HALF = (1, 2, 4, 8)
POOL_PAD = 16
ROPE_THETA = 10000.0
EPS = 1e-6
NEG_INF = -1e30
N_CHIPS = 4
N_DEV = 8
LANES = 128

ADAM_LR = 0.001
ADAM_B1 = 0.9
ADAM_B2 = 0.999
ADAM_EPS = 1e-08
ADAM_WD = 0.01
ADAM_STEP = 10

VMEM_BIG = 56 << 20
VMEM_MM = 44 << 20


def _cp(vmem=None):
    return pltpu.CompilerParams(vmem_limit_bytes=vmem) if vmem else None


def _d(a, b, ca, cb):
    return lax.dot_general(a.astype(BF16), b.astype(BF16), (((ca,), (cb,)), ((), ())),
                           preferred_element_type=F32)


@jax.custom_vjp
def _nn(a, b):
    return _d(a, b, 1, 0)


def _nn_f(a, b):
    return _d(a, b, 1, 0), (a, b)


def _nn_b(r, g):
    a, b = r
    return _d(g, b, 1, 1).astype(a.dtype), _d(a, g, 0, 0).astype(b.dtype)


_nn.defvjp(_nn_f, _nn_b)


@jax.custom_vjp
def _nt(a, b):
    return _d(a, b, 1, 1)


def _nt_f(a, b):
    return _d(a, b, 1, 1), (a, b)


def _nt_b(r, g):
    a, b = r
    return _d(g, b, 1, 0).astype(a.dtype), _d(g, a, 0, 0).astype(b.dtype)


_nt.defvjp(_nt_f, _nt_b)


@jax.custom_vjp
def _tn(a, b):
    return _d(a, b, 0, 0)


def _tn_f(a, b):
    return _d(a, b, 0, 0), (a, b)


def _tn_b(r, g):
    a, b = r
    return _d(b, g, 1, 1).astype(a.dtype), _d(a, g, 1, 0).astype(b.dtype)


_tn.defvjp(_tn_f, _tn_b)


@functools.partial(jax.custom_vjp, nondiff_argnums=(1,))
def _rollr(x, s):
    return pltpu.roll(x, s % x.shape[0], 0)


def _rollr_f(x, s):
    return _rollr(x, s), None


def _rollr_b(s, _, g):
    return (_rollr(g, -s),)


_rollr.defvjp(_rollr_f, _rollr_b)


@jax.custom_vjp
def _swap32(t):
    n = t.shape[1]
    lane = lax.broadcasted_iota(jnp.int32, (1, n), 1)
    return jnp.where((lane & (HEAD // 2)) == 0, pltpu.roll(t, n - HEAD // 2, 1), pltpu.roll(t, HEAD // 2, 1))


def _swap32_f(t):
    return _swap32(t), None


def _swap32_b(_, g):
    return (_swap32(g),)


_swap32.defvjp(_swap32_f, _swap32_b)


def _head_masks():
    lane = lax.broadcasted_iota(jnp.int32, (1, BW), 1)
    return [(lane >= HEAD * h) & (lane < HEAD * (h + 1)) for h in range(NH)]


def _head_rms(x, g):
    out = jnp.zeros_like(x)
    for mh in _head_masks():
        ms = jnp.sum(jnp.where(mh, x * x, 0.0), axis=-1, keepdims=True) * (1.0 / HEAD)
        out = out + jnp.where(mh, x * lax.rsqrt(ms + EPS), 0.0)
    return out * g


def _rms(x, g):
    return x * lax.rsqrt(jnp.mean(x * x, axis=-1, keepdims=True) + EPS) * g


def _rot(t, cos, sin):
    return t * cos + _swap32(t) * sin


def _softmax(s):
    e = jnp.exp(s - lax.stop_gradient(jnp.max(s, axis=-1, keepdims=True)))
    return e / jnp.sum(e, axis=-1, keepdims=True)


def _ret_kv_f(k, v, cos, sin, decf, decb):
    lgf, lgb = jax.nn.log_sigmoid(decf), jax.nn.log_sigmoid(decb)
    kr = _rot(k, cos, sin)
    idx = lax.broadcasted_iota(jnp.int32, (CH, 1), 0).astype(F32)
    r = lax.broadcasted_iota(jnp.int32, (BW, BW), 0) // HEAD
    c = lax.broadcasted_iota(jnp.int32, (BW, BW), 1) // HEAD
    bd = r == c
    kvf = jnp.where(bd, _tn(kr * jnp.exp((CH - 1 - idx) * lgf), v), 0.0)
    kvb = jnp.where(bd, _tn(kr * jnp.exp(idx * lgb), v), 0.0)
    return kvf, kvb


def _ret_scan_f(kvf, kvb, decf, decb):
    n = len(kvf)
    cdf = jnp.exp(CH * jax.nn.log_sigmoid(decf))
    cdb = jnp.exp(CH * jax.nn.log_sigmoid(decb))
    z = jnp.zeros((BW, BW), F32)
    sf, st = [z], z
    for a in range(n - 1):
        st = cdf * st + kvf[a]
        sf.append(st)
    sb, st = [z], z
    for a in range(n - 1, 0, -1):
        st = cdb * st + kvb[a]
        sb.append(st)
    return sf, sb[::-1]


def _ret_out_f(q, k, v, g, cos, sin, stf, stb, decf, decb, gn):
    lgf, lgb = jax.nn.log_sigmoid(decf), jax.nn.log_sigmoid(decb)
    qr = _rot(q, cos, sin) * (HEAD ** -0.5)
    kr = _rot(k, cos, sin)
    diff = (lax.broadcasted_iota(jnp.int32, (CH, CH), 0) - lax.broadcasted_iota(jnp.int32, (CH, CH), 1)).astype(F32)
    o = jnp.zeros_like(q)
    for mh in _head_masks():
        lf = jnp.sum(jnp.where(mh, lgf, 0.0), axis=1, keepdims=True) * (1.0 / HEAD)
        lb = jnp.sum(jnp.where(mh, lgb, 0.0), axis=1, keepdims=True) * (1.0 / HEAD)
        s = _nt(jnp.where(mh, qr, 0.0), kr)
        p = s * jnp.exp(jnp.where(diff >= 0, diff * lf, -diff * lb))
        o = o + jnp.where(mh, _nn(p, v), 0.0)
    idx = lax.broadcasted_iota(jnp.int32, (CH, 1), 0).astype(F32)
    o = o + _nn(qr * jnp.exp((idx + 1) * lgf), stf) + _nn(qr * jnp.exp((CH - idx) * lgb), stb)
    return _head_rms(o, gn) * (g * jax.nn.sigmoid(g))


def _pool_f(pvp, wbd, scale, t_len):
    n = pvp.shape[0]
    w2 = pvp + _rollr(pvp, 1)
    w4 = _rollr(w2, 1) + _rollr(w2, -1)
    w8 = _rollr(w4, 2) + _rollr(w4, -2)
    w16 = _rollr(w8, 4) + _rollr(w8, -4)
    grp = lax.broadcasted_iota(jnp.int32, (1, BW), 1) // HEAD
    ws = jnp.where(grp == 0, w2, jnp.where(grp == 1, w4, jnp.where(grp == 2, w8, w16)))
    half = jnp.where(grp == 0, POOL_HALF[0], jnp.where(grp == 1, POOL_HALF[1], jnp.where(grp == 2, POOL_HALF[2], POOL_HALF[3])))
    t = lax.broadcasted_iota(jnp.int32, (n, 1), 0) - POOL_PAD
    cnt = jnp.minimum(t + half, t_len) - jnp.maximum(t - half, 0)
    cnt = jnp.where((t >= 0) & (t < t_len), cnt, 1).astype(F32)
    pooled = ws / cnt - pvp
    return _nn(pooled, wbd) * scale


def _na_f(q, kw, vw, bias, gq, gk):
    qn = _head_rms(q, gq)
    kn = _head_rms(kw, gk)
    o = jnp.zeros_like(q)
    for h, mh in enumerate(_head_masks()):
        s = _nt(jnp.where(mh, qn, 0.0), kn) * (HEAD ** -0.5) + jnp.concatenate(bias[h], axis=1)
        o = o + jnp.where(mh, _nn(_softmax(s), vw), 0.0)
    return o


def _mem_f(q, mk, mv, gq, gk):
    qn = _head_rms(q, gq)
    kn = _head_rms(mk, gk)
    o = jnp.zeros_like(q)
    for mh in _head_masks():
        s = _nt(jnp.where(mh, qn, 0.0), kn) * (HEAD ** -0.5)
        o = o + jnp.where(mh, _nn(_softmax(s), mv), 0.0)
    return o


def _gate_f(gp, br, wb):
    out = None
    for n in range(NH):
        t = jax.nn.sigmoid(gp[n]) * _nn(br[n], wb[n])
        out = t if out is None else out + t
    return out


def _swiglu_f(a, g):
    return a * jax.nn.sigmoid(a) * g


def _tile(n, prefs):
    for p in prefs:
        if n % p == 0:
            return p
    return n


def _mm_call(name, a, b, extra, dn, grid, a_spec, b_spec, extra_specs, o_spec, out_shape, acc_shape, nred, alias=None):
    n_extra = len(extra)
    add = n_extra == 1 and alias is None

    def body(*refs):
        a_ref, b_ref = refs[0], refs[1]
        o_ref, acc = refs[2 + n_extra], refs[3 + n_extra]
        first = functools.reduce(jnp.logical_and, [pl.program_id(len(grid) - 1 - i) == 0 for i in range(nred)])
        last = functools.reduce(jnp.logical_and, [pl.program_id(len(grid) - 1 - i) == grid[len(grid) - 1 - i] - 1 for i in range(nred)])

        @pl.when(first)
        def _():
            acc[...] = jnp.zeros_like(acc)

        acc[...] += lax.dot_general(a_ref[...].astype(BF16), b_ref[...].astype(BF16), dn, preferred_element_type=F32)

        @pl.when(last)
        def _():
            r = acc[...]
            if add:
                r = r + refs[2][...]
            o_ref[...] = r.astype(o_ref.dtype)

    return pl.pallas_call(
        body, grid=grid, in_specs=[a_spec, b_spec, *extra_specs], out_specs=o_spec, out_shape=out_shape,
        scratch_shapes=[pltpu.VMEM(acc_shape, F32)], name=name, compiler_params=_cp(VMEM_MM),
        input_output_aliases=alias or {},
    )(a, b, *extra)


def _mm_nn(name, a, w, l, *, out3d=False, add=None, out_dtype=F32):
    m, k = a.shape
    _, s_n, _, ns = w.shape
    tm = _tile(m, (512, 256))
    tk = _tile(k, (1024, 1408, 512, 256))
    tn = ns
    grid = (m // tm, s_n, ns // tn, k // tk)
    a_spec = pl.BlockSpec((tm, tk), lambda i, s, j, kk: (i, kk))
    b_spec = pl.BlockSpec((None, None, tk, tn), lambda i, s, j, kk: (l, s, kk, j))
    if out3d:
        o_spec = pl.BlockSpec((None, tm, tn), lambda i, s, j, kk: (s, i, j))
        out_shape = SDS((s_n, m, ns), out_dtype)
    else:
        nj = ns // tn
        o_spec = pl.BlockSpec((tm, tn), lambda i, s, j, kk: (i, s * nj + j))
        out_shape = SDS((m, s_n * ns), out_dtype)
    extra, especs = [], []
    if add is not None:
        assert not out3d
        nj = ns // tn
        extra, especs = [add], [pl.BlockSpec((tm, tn), lambda i, s, j, kk: (i, s * nj + j))]
    return _mm_call(name, a, w, extra, (((1,), (0,)), ((), ())), grid, a_spec, b_spec, especs, o_spec, out_shape, (tm, tn), 1)


def _mm_nt(name, g, w, l):
    _, s_n, k, ns = w.shape
    g3d = g.ndim == 3
    m = g.shape[1] if g3d else g.shape[0]
    tm = _tile(m, (512, 256))
    to = _tile(k, (1024, 1408, 512, 256))
    tk = ns if (ns % LANES or ns <= 1408) else _tile(ns, (1024, 512))
    nks = ns // tk
    grid = (m // tm, k // to, s_n, nks)
    if g3d:
        a_spec = pl.BlockSpec((None, tm, tk), lambda i, j, s, r: (s, i, r))
    else:
        a_spec = pl.BlockSpec((tm, tk), lambda i, j, s, r: (i, s * nks + r))
    b_spec = pl.BlockSpec((None, None, to, tk), lambda i, j, s, r: (l, s, j, r))
    o_spec = pl.BlockSpec((tm, to), lambda i, j, s, r: (i, j))
    return _mm_call(name, g, w, [], (((1,), (1,)), ((), ())), grid, a_spec, b_spec, [], o_spec, SDS((m, k), F32), (tm, to), 2)


def _mm_tn(name, a, g, buf, l):
    t_len, k = a.shape
    _, s_n, _, ns = buf.shape
    g3d = g.ndim == 3
    tt = _tile(t_len, (1024, 512))
    to = _tile(k, (512, 256, 128))
    tn = ns
    nj = ns // tn
    grid = (s_n, k // to, nj, t_len // tt)
    a_spec = pl.BlockSpec((tt, to), lambda s, i, j, t: (t, i))
    if g3d:
        b_spec = pl.BlockSpec((None, tt, tn), lambda s, i, j, t: (s, t, j))
    else:
        b_spec = pl.BlockSpec((tt, tn), lambda s, i, j, t: (t, s * nj + j))
    o_spec = pl.BlockSpec((None, None, to, tn), lambda s, i, j, t: (l, s, i, j))
    return _mm_call(name, a, g, [buf], (((0,), (0,)), ((), ())), grid, a_spec, b_spec, [ANY], o_spec,
                    SDS(buf.shape, F32), (to, tn), 1, alias={2: 0})


def _rms_fwd(name, x, g):
    m, d = x.shape
    tm = _tile(m, (256,))

    def body(x_ref, g_ref, o_ref):
        o_ref[...] = _rms(x_ref[...], g_ref[...]).astype(BF16)

    row = pl.BlockSpec((tm, d), lambda i: (i, 0))
    return pl.pallas_call(body, grid=(m // tm,), in_specs=[row, pl.BlockSpec((1, d), lambda i: (0, 0))], out_specs=row,
                          out_shape=SDS((m, d), BF16), name=name)(x, g)


def _rms_bwd(name, x, g, dhs, add=None):
    m, d = x.shape
    tm = _tile(m, (256,))
    nh = len(dhs)

    def body(*refs):
        x_ref, g_ref = refs[0], refs[1]
        dh = refs[2][...]
        for r in refs[3:2 + nh]:
            dh = dh + r[...]
        dx_ref, dg_ref = refs[-2], refs[-1]
        _, vjp = jax.vjp(_rms, x_ref[...], g_ref[...])
        dx, dg = vjp(dh)
        if add is not None:
            dx = dx + refs[2 + nh][...]
        dx_ref[...] = dx

        @pl.when(pl.program_id(0) == 0)
        def _():
            dg_ref[...] = jnp.zeros_like(dg_ref)

        dg_ref[...] += dg

    row = pl.BlockSpec((tm, d), lambda i: (i, 0))
    vec = pl.BlockSpec((1, d), lambda i: (0, 0))
    ins = [x, g, *dhs] + ([add] if add is not None else [])
    return pl.pallas_call(body, grid=(m // tm,), in_specs=[row, vec] + [row] * (len(ins) - 2), out_specs=[row, vec],
                          out_shape=[SDS((m, d), F32), SDS((1, d), F32)], name=name)(*ins)


def _swiglu_fwd(name, ag3):
    _, t_len, w = ag3.shape
    tm = _tile(t_len, (512, 256))

    def body(a_ref, g_ref, o_ref):
        o_ref[...] = _swiglu_f(a_ref[...], g_ref[...]).astype(BF16)

    return pl.pallas_call(
        body, grid=(t_len // tm, 2),
        in_specs=[pl.BlockSpec((None, tm, w), lambda i, j: (j, i, 0)), pl.BlockSpec((None, tm, w), lambda i, j: (j + 2, i, 0))],
        out_specs=pl.BlockSpec((tm, w), lambda i, j: (i, j)), out_shape=SDS((t_len, 2 * w), BF16), name=name)(ag3, ag3)


def _swiglu_bwd(name, ag3, du):
    _, t_len, w = ag3.shape
    tm = _tile(t_len, (512, 256))

    def body(a_ref, g_ref, du_ref, o_ref):
        _, vjp = jax.vjp(_swiglu_f, a_ref[...], g_ref[...])
        da, dg = vjp(du_ref[...])
        o_ref[...] = jnp.where(pl.program_id(1) < 2, da, dg).astype(BF16)

    return pl.pallas_call(
        body, grid=(t_len // tm, 4),
        in_specs=[pl.BlockSpec((None, tm, w), lambda i, s: (s % 2, i, 0)), pl.BlockSpec((None, tm, w), lambda i, s: (s % 2 + 2, i, 0)),
                  pl.BlockSpec((tm, w), lambda i, s: (i, s % 2))],
        out_specs=pl.BlockSpec((None, tm, w), lambda i, s: (s, i, 0)), out_shape=SDS(ag3.shape, BF16), name=name)(ag3, ag3, du)


def _gate_fwd(name, gp3, br, wb, l):
    _, t_len, d = gp3.shape
    tm = _tile(t_len, (512, 256))

    def body(gp_ref, br_ref, wb_ref, o_ref):
        o_ref[...] = _gate_f([gp_ref[n] for n in range(NH)], [br_ref[n] for n in range(NH)],
                             [wb_ref[n] for n in range(NH)]).astype(BF16)

    return pl.pallas_call(
        body, grid=(t_len // tm, d // BW),
        in_specs=[pl.BlockSpec((NH, tm, BW), lambda i, s: (0, i, s)), pl.BlockSpec((NH, tm, BW), lambda i, s: (0, i, 0)),
                  pl.BlockSpec((None, None, NH, BW, BW), lambda i, s: (l, s, 0, 0, 0))],
        out_specs=pl.BlockSpec((tm, BW), lambda i, s: (i, s)), out_shape=SDS((t_len, d), BF16), name=name)(gp3, br, wb)


def _gate_bwd(name, gp3, br, wb, l, dmerged):
    _, t_len, d = gp3.shape
    tm = _tile(t_len, (512, 256))
    ns = d // BW

    def body(gp_ref, br_ref, wb_ref, dm_ref, dgp_ref, dbr_ref, dwb_ref):
        i, s = pl.program_id(0), pl.program_id(1)
        gp = [gp_ref[n] for n in range(NH)]
        brv = [br_ref[n].astype(F32) for n in range(NH)]
        wbv = [wb_ref[n].astype(F32) for n in range(NH)]
        _, vjp = jax.vjp(_gate_f, gp, brv, wbv)
        dgp, dbr, dwb = vjp(dm_ref[...])

        @pl.when(s == 0)
        def _():
            dbr_ref[...] = jnp.zeros_like(dbr_ref)

        @pl.when((i == 0) & (s == 0))
        def _():
            dwb_ref[...] = jnp.zeros_like(dwb_ref)

        for n in range(NH):
            dgp_ref[n] = dgp[n].astype(BF16)
            dbr_ref[n] += dbr[n]
            dwb_ref[s, n] += dwb[n]

    return pl.pallas_call(
        body, grid=(t_len // tm, ns),
        in_specs=[pl.BlockSpec((NH, tm, BW), lambda i, s: (0, i, s)), pl.BlockSpec((NH, tm, BW), lambda i, s: (0, i, 0)),
                  pl.BlockSpec((None, None, NH, BW, BW), lambda i, s: (l, s, 0, 0, 0)), pl.BlockSpec((tm, BW), lambda i, s: (i, s))],
        out_specs=[pl.BlockSpec((NH, tm, BW), lambda i, s: (0, i, s)), pl.BlockSpec((NH, tm, BW), lambda i, s: (0, i, 0)),
                   pl.BlockSpec((ns, NH, BW, BW), lambda i, s: (0, 0, 0, 0))],
        out_shape=[SDS(gp3.shape, BF16), SDS((NH, t_len, BW), F32), SDS((ns, NH, BW, BW), F32)], name=name,
        compiler_params=_cp(VMEM_MM))(gp3, br, wb, dmerged)


def _loss_kernel(name, y, tgt):
    m, d = y.shape
    tm = _tile(m, (256,))

    def body(y_ref, t_ref, l_ref, dy_ref):
        err = y_ref[...] - t_ref[...]
        dy_ref[...] = err * (1.0 / d)

        @pl.when(pl.program_id(0) == 0)
        def _():
            l_ref[...] = jnp.zeros_like(l_ref)

        l_ref[...] += 0.5 * jnp.sum(jnp.mean(err * err, axis=-1, keepdims=True), axis=0, keepdims=True)

    row = pl.BlockSpec((tm, d), lambda i: (i, 0))
    return pl.pallas_call(body, grid=(m // tm,), in_specs=[row, row], out_specs=[pl.BlockSpec((1, 1), lambda i: (0, 0)), row],
                          out_shape=[SDS((1, 1), F32), SDS((m, d), F32)], name=name)(y, tgt)


def _zero_first(refs):
    @pl.when(pl.program_id(0) == 0)
    def _():
        for r in refs:
            r[...] = jnp.zeros_like(r)


def _grp(rows, j):
    return pl.BlockSpec((rows, BW), lambda a, j=j: (a, j))


_VEC = pl.BlockSpec((1, BW), lambda a: (0, 0))


def _ret_fwd(tag, proj, cos, sin, decf, decb, gn):
    t_len = proj.shape[0]
    n = t_len // CH
    tab = pl.BlockSpec((CH, BW), lambda a: (a, 0))
    st = pl.BlockSpec((None, BW, BW), lambda a: (a, 0, 0))

    def k1(k_ref, v_ref, cos_ref, sin_ref, df_ref, db_ref, kvf_ref, kvb_ref):
        kvf_ref[...], kvb_ref[...] = _ret_kv_f(k_ref[...], v_ref[...], cos_ref[...], sin_ref[...], df_ref[...], db_ref[...])

    kvf, kvb = pl.pallas_call(
        k1, grid=(n,), in_specs=[_grp(CH, 1), _grp(CH, 2), tab, tab, _VEC, _VEC], out_specs=[st, st],
        out_shape=[SDS((n, BW, BW), F32)] * 2, name=f"ret_kv_{tag}")(proj, proj, cos, sin, decf, decb)

    def k2(kvf_ref, kvb_ref, df_ref, db_ref, sf_ref, sb_ref):
        sf, sb = _ret_scan_f([kvf_ref[a] for a in range(n)], [kvb_ref[a] for a in range(n)], df_ref[...], db_ref[...])
        for a in range(n):
            sf_ref[a] = sf[a]
            sb_ref[a] = sb[a]

    stf, stb = pl.pallas_call(k2, out_shape=[SDS((n, BW, BW), F32)] * 2, name=f"ret_scan_{tag}",
                              compiler_params=_cp(VMEM_BIG))(kvf, kvb, decf, decb)

    def k3(q_ref, k_ref, v_ref, g_ref, cos_ref, sin_ref, sf_ref, sb_ref, df_ref, db_ref, gn_ref, o_ref):
        o_ref[...] = _ret_out_f(q_ref[...], k_ref[...], v_ref[...], g_ref[...], cos_ref[...], sin_ref[...], sf_ref[...],
                                sb_ref[...], df_ref[...], db_ref[...], gn_ref[...]).astype(BF16)

    ret = pl.pallas_call(
        k3, grid=(n,), in_specs=[_grp(CH, 0), _grp(CH, 1), _grp(CH, 2), _grp(CH, 3), tab, tab, st, st, _VEC, _VEC, _VEC],
        out_specs=tab, out_shape=SDS((t_len, BW), BF16), name=f"ret_out_{tag}")(proj, proj, proj, proj, cos, sin, stf, stb, decf, decb, gn)
    return ret, (kvf, kvb, stf, stb)


def _ret_bwd(tag, proj, cos, sin, decf, decb, gn, saved, dret):
    kvf, kvb, stf, stb = saved
    t_len = proj.shape[0]
    n = t_len // CH
    tab = pl.BlockSpec((CH, BW), lambda a: (a, 0))
    st = pl.BlockSpec((None, BW, BW), lambda a: (a, 0, 0))

    def k3(q_ref, k_ref, v_ref, g_ref, cos_ref, sin_ref, sf_ref, sb_ref, df_ref, db_ref, gn_ref, do_ref,
           dq_ref, dk_ref, dv_ref, dg_ref, dsf_ref, dsb_ref, ddf_ref, ddb_ref, dgn_ref):
        cos, sin = cos_ref[...], sin_ref[...]
        f = lambda q, k, v, g, sf, sb, df, db, gnv: _ret_out_f(q, k, v, g, cos, sin, sf, sb, df, db, gnv)
        _, vjp = jax.vjp(f, q_ref[...], k_ref[...], v_ref[...], g_ref[...], sf_ref[...], sb_ref[...], df_ref[...], db_ref[...], gn_ref[...])
        dq, dk, dv, dg, dsf, dsb, ddf, ddb, dgn = vjp(do_ref[...])
        dq_ref[...] = dq.astype(BF16)
        dk_ref[...] = dk
        dv_ref[...] = dv
        dg_ref[...] = dg.astype(BF16)
        dsf_ref[...] = dsf
        dsb_ref[...] = dsb
        _zero_first([ddf_ref, ddb_ref, dgn_ref])
        ddf_ref[...] += ddf
        ddb_ref[...] += ddb
        dgn_ref[...] += dgn

    dq, dk3, dv3, dg, dstf, dstb, ddf3, ddb3, dgn = pl.pallas_call(
        k3, grid=(n,),
        in_specs=[_grp(CH, 0), _grp(CH, 1), _grp(CH, 2), _grp(CH, 3), tab, tab, st, st, _VEC, _VEC, _VEC, tab],
        out_specs=[tab, tab, tab, tab, st, st, _VEC, _VEC, _VEC],
        out_shape=[SDS((t_len, BW), BF16), SDS((t_len, BW), F32), SDS((t_len, BW), F32), SDS((t_len, BW), BF16),
                   SDS((n, BW, BW), F32), SDS((n, BW, BW), F32), SDS((1, BW), F32), SDS((1, BW), F32), SDS((1, BW), F32)],
        name=f"ret_out_bwd_{tag}")(proj, proj, proj, proj, cos, sin, stf, stb, decf, decb, gn, dret)

    def k2(kvf_ref, kvb_ref, df_ref, db_ref, dsf_ref, dsb_ref, dkvf_ref, dkvb_ref, ddf_ref, ddb_ref):
        _, vjp = jax.vjp(_ret_scan_f, [kvf_ref[a] for a in range(n)], [kvb_ref[a] for a in range(n)], df_ref[...], db_ref[...])
        dkvf, dkvb, ddf_ref[...], ddb_ref[...] = vjp(([dsf_ref[a] for a in range(n)], [dsb_ref[a] for a in range(n)]))
        for a in range(n):
            dkvf_ref[a] = dkvf[a]
            dkvb_ref[a] = dkvb[a]

    dkvf, dkvb, ddf2, ddb2 = pl.pallas_call(
        k2, out_shape=[SDS((n, BW, BW), F32), SDS((n, BW, BW), F32), SDS((1, BW), F32), SDS((1, BW), F32)],
        name=f"ret_scan_bwd_{tag}", compiler_params=_cp(VMEM_BIG))(kvf, kvb, decf, decb, dstf, dstb)

    def k1(k_ref, v_ref, cos_ref, sin_ref, df_ref, db_ref, dkvf_ref, dkvb_ref, dk3_ref, dv3_ref, dk_ref, dv_ref, ddf_ref, ddb_ref):
        cos, sin = cos_ref[...], sin_ref[...]
        f = lambda k, v, df, db: _ret_kv_f(k, v, cos, sin, df, db)
        _, vjp = jax.vjp(f, k_ref[...], v_ref[...], df_ref[...], db_ref[...])
        dk, dv, ddf, ddb = vjp((dkvf_ref[...], dkvb_ref[...]))
        dk_ref[...] = (dk + dk3_ref[...]).astype(BF16)
        dv_ref[...] = (dv + dv3_ref[...]).astype(BF16)
        _zero_first([ddf_ref, ddb_ref])
        ddf_ref[...] += ddf
        ddb_ref[...] += ddb

    dk, dv, ddf1, ddb1 = pl.pallas_call(
        k1, grid=(n,), in_specs=[_grp(CH, 1), _grp(CH, 2), tab, tab, _VEC, _VEC, st, st, tab, tab],
        out_specs=[tab, tab, _VEC, _VEC],
        out_shape=[SDS((t_len, BW), BF16), SDS((t_len, BW), BF16), SDS((1, BW), F32), SDS((1, BW), F32)],
        name=f"ret_kv_bwd_{tag}")(proj, proj, cos, sin, decf, decb, dkvf, dkvb, dk3, dv3)
    return dq, dk, dv, dg, ddf1 + ddf2 + ddf3, ddb1 + ddb2 + ddb3, dgn


def _pool_fwd(tag, pvp, wbd, scale, t_len):
    def body(p_ref, w_ref, s_ref, o_ref):
        o_ref[...] = _pool_f(p_ref[...], w_ref[...], s_ref[...], t_len).astype(BF16)

    return pl.pallas_call(body, out_shape=SDS(pvp.shape, BF16), name=f"pool_{tag}", compiler_params=_cp(VMEM_BIG))(pvp, wbd, scale)


def _pool_bwd(tag, pvp, wbd, scale, t_len, dpool_p):
    def body(p_ref, w_ref, s_ref, do_ref, dp_ref, dw_ref, ds_ref):
        f = lambda p, w, s: _pool_f(p, w, s, t_len)
        _, vjp = jax.vjp(f, p_ref[...], w_ref[...], s_ref[...])
        dp, dw, ds = vjp(do_ref[...])
        dp_ref[...] = dp.astype(BF16)
        dw_ref[...] = dw
        ds_ref[...] = ds

    return pl.pallas_call(body, out_shape=[SDS(pvp.shape, BF16), SDS((BW, BW), F32), SDS((1, BW), F32)],
                          name=f"pool_bwd_{tag}", compiler_params=_cp(VMEM_BIG))(pvp, wbd, scale, dpool_p)


def _na_window(r, rows):
    r0 = jnp.clip(r - NA_ROWS // 2, 0, rows - NA_ROWS)
    return r0, r - r0


NA_PAIRS = 2 * NA_ROWS - 2


def _na_bias_pieces(b_ref, pat):
    return [[b_ref[h * NA_PAIRS + 2 * k - pat + NA_ROWS - 1] for k in range(NA_ROWS // 2)] for h in range(NH)]


def _na_fwd(tag, proj, bias_tab, gq, gk):
    t_len = proj.shape[0]
    rows = t_len // GRID_W
    win = NA_ROWS * GRID_W

    def body(q_ref, k_ref, v_ref, b_ref, gq_ref, gk_ref, o_ref):
        r0, pat = _na_window(pl.program_id(0), rows)
        start = pl.multiple_of(r0 * GRID_W, GRID_W)
        o_ref[...] = _na_f(q_ref[...], k_ref[pl.ds(start, win), :], v_ref[pl.ds(start, win), :], _na_bias_pieces(b_ref, pat),
                           gq_ref[...], gk_ref[...]).astype(BF16)

    full = lambda j: pl.BlockSpec((t_len, BW), lambda a, j=j: (0, j))
    return pl.pallas_call(
        body, grid=(rows,),
        in_specs=[_grp(GRID_W, 5), full(6), full(7), pl.BlockSpec(bias_tab.shape, lambda a: (0, 0, 0)), _VEC, _VEC],
        out_specs=pl.BlockSpec((GRID_W, BW), lambda a: (a, 0)), out_shape=SDS((t_len, BW), BF16),
        name=f"na_{tag}", compiler_params=_cp(VMEM_MM))(proj, proj, proj, bias_tab, gq, gk)


def _na_bwd(tag, proj, bias_tab, gq, gk, dna):
    t_len = proj.shape[0]
    rows = t_len // GRID_W
    win = NA_ROWS * GRID_W

    def body(q_ref, k_ref, v_ref, b_ref, gq_ref, gk_ref, do_ref, dq_ref, dk_ref, dv_ref, db_ref, dgq_ref, dgk_ref):
        r0, pat = _na_window(pl.program_id(0), rows)
        start = pl.multiple_of(r0 * GRID_W, GRID_W)
        _zero_first([dk_ref, dv_ref, db_ref, dgq_ref, dgk_ref])
        _, vjp = jax.vjp(_na_f, q_ref[...], k_ref[pl.ds(start, win), :], v_ref[pl.ds(start, win), :], _na_bias_pieces(b_ref, pat),
                         gq_ref[...], gk_ref[...])
        dq, dk, dv, db, dgq, dgk = vjp(do_ref[...])
        dq_ref[...] = dq.astype(BF16)
        dk_ref[pl.ds(start, win), :] += dk
        dv_ref[pl.ds(start, win), :] += dv
        for h in range(NH):
            for k in range(NA_ROWS // 2):
                db_ref[h * NA_PAIRS + 2 * k - pat + NA_ROWS - 1] += db[h][k]
        dgq_ref[...] += dgq
        dgk_ref[...] += dgk

    full = lambda j: pl.BlockSpec((t_len, BW), lambda a, j=j: (0, j))
    whole = pl.BlockSpec((t_len, BW), lambda a: (0, 0))
    tabspec = pl.BlockSpec(bias_tab.shape, lambda a: (0, 0, 0))
    row = pl.BlockSpec((GRID_W, BW), lambda a: (a, 0))
    return pl.pallas_call(
        body, grid=(rows,), in_specs=[_grp(GRID_W, 5), full(6), full(7), tabspec, _VEC, _VEC, row],
        out_specs=[row, whole, whole, tabspec, _VEC, _VEC],
        out_shape=[SDS((t_len, BW), BF16), SDS((t_len, BW), F32), SDS((t_len, BW), F32), SDS(bias_tab.shape, F32),
                   SDS((1, BW), F32), SDS((1, BW), F32)],
        name=f"na_bwd_{tag}", compiler_params=_cp(VMEM_BIG))(proj, proj, proj, bias_tab, gq, gk, dna)


def _mem_fwd(tag, proj, memkv, gq, gk):
    t_len = proj.shape[0]
    n_mem = memkv.shape[0]
    tm = _tile(t_len, (256,))

    def body(q_ref, mk_ref, mv_ref, gq_ref, gk_ref, o_ref):
        o_ref[...] = _mem_f(q_ref[...], mk_ref[...], mv_ref[...], gq_ref[...], gk_ref[...]).astype(BF16)

    mspec = lambda j: pl.BlockSpec((n_mem, BW), lambda a, j=j: (0, j))
    return pl.pallas_call(
        body, grid=(t_len // tm,), in_specs=[_grp(tm, 8), mspec(0), mspec(1), _VEC, _VEC],
        out_specs=pl.BlockSpec((tm, BW), lambda a: (a, 0)), out_shape=SDS((t_len, BW), BF16), name=f"mem_{tag}")(proj, memkv, memkv, gq, gk)


def _mem_bwd(tag, proj, memkv, gq, gk, dmo):
    t_len = proj.shape[0]
    n_mem = memkv.shape[0]
    tm = _tile(t_len, (256,))

    def body(q_ref, mk_ref, mv_ref, gq_ref, gk_ref, do_ref, dq_ref, dmk_ref, dmv_ref, dgq_ref, dgk_ref):
        _zero_first([dmk_ref, dmv_ref, dgq_ref, dgk_ref])
        _, vjp = jax.vjp(_mem_f, q_ref[...], mk_ref[...], mv_ref[...], gq_ref[...], gk_ref[...])
        dq, dmk, dmv, dgq, dgk = vjp(do_ref[...])
        dq_ref[...] = dq.astype(BF16)
        dmk_ref[...] += dmk
        dmv_ref[...] += dmv
        dgq_ref[...] += dgq
        dgk_ref[...] += dgk

    mspec = lambda j: pl.BlockSpec((n_mem, BW), lambda a, j=j: (0, j))
    mout = pl.BlockSpec((n_mem, BW), lambda a: (0, 0))
    row = pl.BlockSpec((tm, BW), lambda a: (a, 0))
    dq, dmk, dmv, dgq, dgk = pl.pallas_call(
        body, grid=(t_len // tm,), in_specs=[_grp(tm, 8), mspec(0), mspec(1), _VEC, _VEC, row],
        out_specs=[row, mout, mout, _VEC, _VEC],
        out_shape=[SDS((t_len, BW), BF16), SDS((n_mem, BW), F32), SDS((n_mem, BW), F32), SDS((1, BW), F32), SDS((1, BW), F32)],
        name=f"mem_bwd_{tag}")(proj, memkv, memkv, gq, gk, dmo)
    return dq, jnp.concatenate([dmk, dmv], axis=1), dgq, dgk


NA_NJ = 2 * NA_COLS - 1


def _na_onehot():
    q = np.arange(GRID_W)[:, None]
    kc = np.arange(GRID_W)[None, :]
    qwin = np.clip(q - NA_COLS // 2, 0, GRID_W - NA_COLS)
    mask = (kc >= qwin) & (kc < qwin + NA_COLS)
    col = np.clip(kc - q, -(NA_COLS - 1), NA_COLS - 1) + NA_COLS - 1
    onehot = np.zeros((LANES, GRID_W, 2, GRID_W), np.float32)
    qq, kk = np.nonzero(mask)
    for half in range(2):
        onehot[half * NA_NJ + col[qq, kk], qq, half, kk] = 1.0
    valid = np.broadcast_to(mask[:, None, :], (GRID_W, 2, GRID_W)).astype(np.float32)
    return onehot.reshape(LANES, -1), valid.reshape(1, -1)


def _na_pair_rows(rpb):
    n_layers = rpb.shape[0]
    pair = jnp.concatenate([rpb[:, :, :-1], rpb[:, :, 1:]], axis=-1).reshape(n_layers, NH * NA_PAIRS, 2 * NA_NJ)
    return jnp.pad(pair, ((0, 0), (0, 0), (0, LANES - 2 * NA_NJ)))


def _na_bias_table(rpb):
    n_layers = rpb.shape[0]
    onehot, valid = _na_onehot()
    width = onehot.shape[1]

    def body(r_ref, oh_ref, ok_ref, o_ref):
        t = jnp.dot(r_ref[...], oh_ref[...], precision=lax.Precision.HIGHEST, preferred_element_type=F32)
        o_ref[...] = jnp.where(ok_ref[...] > 0, t, NEG_INF)

    out = pl.pallas_call(
        body, grid=(n_layers,),
        in_specs=[pl.BlockSpec((None, NH * NA_PAIRS, LANES), lambda l: (l, 0, 0)), pl.BlockSpec((LANES, width), lambda l: (0, 0)),
                  pl.BlockSpec((1, width), lambda l: (0, 0))],
        out_specs=pl.BlockSpec((None, NH * NA_PAIRS, width), lambda l: (l, 0, 0)),
        out_shape=SDS((n_layers, NH * NA_PAIRS, width), F32), name="na_bias_table")(
            _na_pair_rows(rpb), jnp.asarray(onehot), jnp.asarray(valid))
    return out.reshape(n_layers, NH * NA_PAIRS, GRID_W, 2 * GRID_W)


def _na_bias_grad(dtab):
    n_layers = dtab.shape[0]
    onehot, _ = _na_onehot()
    width = onehot.shape[1]

    def body(x_ref, oh_ref, o_ref):
        o_ref[...] = jnp.dot(x_ref[...], oh_ref[...], precision=lax.Precision.HIGHEST, preferred_element_type=F32)

    out = pl.pallas_call(
        body, grid=(n_layers,),
        in_specs=[pl.BlockSpec((None, NH * NA_PAIRS, width), lambda l: (l, 0, 0)), pl.BlockSpec((width, LANES), lambda l: (0, 0))],
        out_specs=pl.BlockSpec((None, NH * NA_PAIRS, LANES), lambda l: (l, 0, 0)),
        out_shape=SDS((n_layers, NH * NA_PAIRS, LANES), F32), name="na_bias_grad")(
            dtab.reshape(n_layers, NH * NA_PAIRS, width), jnp.asarray(onehot.T.copy()))
    out = out.reshape(n_layers, NH, NA_PAIRS, LANES)
    zero = jnp.zeros((n_layers, NH, 1, NA_NJ), F32)
    return (jnp.concatenate([out[..., :NA_NJ], zero], axis=2) + jnp.concatenate([zero, out[..., NA_NJ:2 * NA_NJ]], axis=2))


def _place():
    x, y, c = lax.axis_index("x"), lax.axis_index("y"), lax.axis_index("c")
    chips = [(1 - x, y), (x, 1 - y), (1 - x, 1 - y)]
    return x, y, c, chips


def _remote(src, dst, ssem, rsem, dev):
    return pltpu.make_async_remote_copy(src_ref=src, dst_ref=dst, send_sem=ssem, recv_sem=rsem, device_id=dev, device_id_type=MESH)


def _all_gather_shards(ws):
    n = len(ws)

    def body(*refs):
        ins, outs = refs[:n], refs[n:2 * n]
        osem_s, osem_r, ssem, rsem, fssem, frsem = refs[2 * n:]
        x, y, c, chips = _place()
        me = 2 * x + y
        sib = (x, y, 1 - c)

        def own(i):
            return _remote(ins[i], outs[i].at[:, me], osem_s.at[i], osem_r.at[i], sib)

        def first(i, r):
            h = ins[i].shape[1] // 2
            px, py = chips[r]
            return _remote(ins[i].at[:, pl.ds(c * h, h)], outs[i].at[:, me, pl.ds(c * h, h)], ssem.at[i, r], rsem.at[i, r], (px, py, c))

        def passed(i, r, half):
            h = ins[i].shape[1] // 2
            px, py = chips[r]
            land = outs[i].at[:, 2 * px + py, pl.ds(half * h, h)]
            return land, _remote(land, land, fssem.at[i, r], frsem.at[i, r], sib)

        for i in range(n):
            own(i).start()
            for r in range(3):
                first(i, r).start()
        for i in range(n):
            for r in range(3):
                land, fwd = passed(i, r, c)
                _remote(land, land, ssem.at[i, r], rsem.at[i, r], sib).wait_recv()
                fwd.start()
        for i in range(n):
            for r in range(3):
                passed(i, r, 1 - c)[1].wait_recv()
        for i in range(n):
            own(i).wait()
            for r in range(3):
                first(i, r).wait_send()
                passed(i, r, c)[1].wait_send()

    sem = lambda: pltpu.SemaphoreType.DMA((n, 3))
    return pl.pallas_call(
        body, in_specs=[ANY] * n, out_specs=[ANY] * n,
        out_shape=[SDS((w.shape[0], N_CHIPS) + w.shape[1:], w.dtype) for w in ws],
        scratch_shapes=[pltpu.SemaphoreType.DMA((n,)), pltpu.SemaphoreType.DMA((n,)), sem(), sem(), sem(), sem()],
        name="all_gather_weights")(*ws)


def _pair_exchange(gs):
    n = len(gs)

    def body(*refs):
        ins, outs = refs[:n], refs[n:2 * n]
        ssem, rsem = refs[2 * n:]
        x, y, c, _ = _place()
        cps = []
        for i in range(n):
            h = ins[i].shape[2] // 2
            cps.append(_remote(ins[i].at[:, :, pl.ds((1 - c) * h, h)], outs[i], ssem.at[i], rsem.at[i], (x, y, 1 - c)))
            cps[-1].start()
        for cp in cps:
            cp.wait()

    return pl.pallas_call(
        body, in_specs=[ANY] * n, out_specs=[ANY] * n,
        out_shape=[SDS(g.shape[:2] + (g.shape[2] // 2, g.shape[3]), g.dtype) for g in gs],
        scratch_shapes=[pltpu.SemaphoreType.DMA((n,)), pltpu.SemaphoreType.DMA((n,))], name="grad_pair_exchange")(*gs)


def _half_exchange(fs):
    n = len(fs)

    def body(*refs):
        ins, outs = refs[:n], refs[n:2 * n]
        ssem, rsem = refs[2 * n:]
        x, y, c, _ = _place()
        sib = (x, y, 1 - c)
        for i in range(n):
            h = ins[i].shape[1] // 2
            _remote(ins[i].at[:, pl.ds(c * h, h)], outs[i].at[:, pl.ds(c * h, h)], ssem.at[i], rsem.at[i], sib).start()
        for i in range(n):
            h = ins[i].shape[1] // 2
            land = outs[i].at[:, pl.ds((1 - c) * h, h)]
            _remote(land, land, ssem.at[i], rsem.at[i], sib).wait_recv()
            _remote(ins[i].at[:, pl.ds(c * h, h)], outs[i].at[:, pl.ds(c * h, h)], ssem.at[i], rsem.at[i], sib).wait_send()

    return pl.pallas_call(
        body, in_specs=[ANY] * n, out_specs=[ANY] * n, out_shape=[SDS(f.shape, f.dtype) for f in fs],
        input_output_aliases={i: i for i in range(n)},
        scratch_shapes=[pltpu.SemaphoreType.DMA((n,)), pltpu.SemaphoreType.DMA((n,))], name="grad_half_exchange")(*fs)


def _chip_scatter(ps):
    n = len(ps)

    def body(*refs):
        ins, outs = refs[:n], refs[n:2 * n]
        ssem, rsem = refs[2 * n:]
        x, y, c, chips = _place()
        cps = []
        for i in range(n):
            for r, (px, py) in enumerate(chips):
                cp = _remote(ins[i].at[:, 2 * px + py], outs[i].at[r], ssem.at[i, r], rsem.at[i, r], (px, py, c))
                cp.start()
                cps.append(cp)
        for cp in cps:
            cp.wait()

    return pl.pallas_call(
        body, in_specs=[ANY] * n, out_specs=[ANY] * n,
        out_shape=[SDS((3, p.shape[0]) + p.shape[2:], p.dtype) for p in ps],
        scratch_shapes=[pltpu.SemaphoreType.DMA((n, 3)), pltpu.SemaphoreType.DMA((n, 3))], name="grad_scatter")(*ps)


def _gather_small(v):
    def body(v_ref, o_ref, lsem, ssem, rsem):
        x, y, c, _ = _place()
        me = 4 * x + 2 * y + c
        flips = [(fx, fy, fc) for fx in (0, 1) for fy in (0, 1) for fc in (0, 1)][1:]
        peer = lambda f: (1 - x if f[0] else x, 1 - y if f[1] else y, 1 - c if f[2] else c)
        loc = pltpu.make_async_copy(v_ref, o_ref.at[me], lsem)
        loc.start()
        cps = [_remote(v_ref, o_ref.at[me], ssem.at[k], rsem.at[k], peer(f)) for k, f in enumerate(flips)]
        for cp in cps:
            cp.start()
        for k, f in enumerate(flips):
            px, py, pc = peer(f)
            land = o_ref.at[4 * px + 2 * py + pc]
            _remote(land, land, ssem.at[k], rsem.at[k], peer(f)).wait_recv()
        for cp in cps:
            cp.wait_send()
        loc.wait()

    return pl.pallas_call(
        body, in_specs=[ANY], out_specs=ANY, out_shape=SDS((N_DEV,) + v.shape, v.dtype),
        scratch_shapes=[pltpu.SemaphoreType.DMA, pltpu.SemaphoreType.DMA((N_DEV - 1,)), pltpu.SemaphoreType.DMA((N_DEV - 1,))],
        name="gather_small")(v)


def _rows_tile(r):
    return _tile(r, (256, 352, 128, 64))


def _pair_sum(name, g, r1, core, me):
    n_l, s_n, h, c = r1.shape
    tr = _rows_tile(h)
    nb = h // tr

    def body(idx_ref, g_ref, r_ref, gm_ref, rm_ref, pb_ref, own_ref):
        pb_ref[...] = (g_ref[...] + r_ref[...]).astype(BF16)
        own_ref[...] = gm_ref[...] + rm_ref[...]

    blk = (None, None, tr, c)
    grid_spec = pltpu.PrefetchScalarGridSpec(
        num_scalar_prefetch=1, grid=(n_l, nb, s_n),
        in_specs=[pl.BlockSpec(blk, lambda a, i, s, idx: (a, s, idx[0] * nb + i, 0)), pl.BlockSpec(blk, lambda a, i, s, idx: (a, s, i, 0)),
                  pl.BlockSpec(blk, lambda a, i, s, idx: (a, idx[1], idx[0] * nb + i, 0)), pl.BlockSpec(blk, lambda a, i, s, idx: (a, idx[1], i, 0))],
        out_specs=[pl.BlockSpec(blk, lambda a, i, s, idx: (a, s, i, 0)), pl.BlockSpec((None, tr, c), lambda a, i, s, idx: (a, i, 0))])
    idx = jnp.stack([core, me]).astype(jnp.int32)
    return pl.pallas_call(body, grid_spec=grid_spec, out_shape=[SDS(r1.shape, BF16), SDS((n_l, h, c), F32)], name=name)(idx, g, r1, g, r1)


def _chip_sum(name, own, r2, core):
    n_l, h, c = own.shape
    tr = _rows_tile(h)
    nb = h // tr

    def body(idx_ref, o_ref, r_ref, f_ref):
        f_ref[...] = ((o_ref[...] + r_ref[0].astype(F32)) + r_ref[1].astype(F32)) + r_ref[2].astype(F32)

    grid_spec = pltpu.PrefetchScalarGridSpec(
        num_scalar_prefetch=1, grid=(n_l, nb),
        in_specs=[pl.BlockSpec((None, tr, c), lambda a, i, idx: (a, i, 0)), pl.BlockSpec((3, None, tr, c), lambda a, i, idx: (0, a, i, 0))],
        out_specs=pl.BlockSpec((None, tr, c), lambda a, i, idx: (a, idx[0] * nb + i, 0)))
    return pl.pallas_call(body, grid_spec=grid_spec, out_shape=SDS((n_l, 2 * h, c), F32), name=name)(
        jnp.reshape(core, (1,)).astype(jnp.int32), own, r2)


def _adamw_math(w, g, m, v):
    m = ADAM_B1 * m + (1.0 - ADAM_B1) * g
    v = ADAM_B2 * v + (1.0 - ADAM_B2) * jnp.square(g)
    m_hat = m / (1.0 - ADAM_B1 ** ADAM_STEP)
    v_hat = v / (1.0 - ADAM_B2 ** ADAM_STEP)
    delta = -ADAM_LR * (m_hat / (jnp.sqrt(v_hat) + ADAM_EPS) + ADAM_WD * w)
    return delta, m, v


def _adamw(name, w, g, m, v):
    lh, r, c = w.shape
    tr = _rows_tile(r)

    def body(w_ref, g_ref, m_ref, v_ref, d_ref, nm_ref, nv_ref):
        d_ref[...], nm_ref[...], nv_ref[...] = _adamw_math(w_ref[...], g_ref[...], m_ref[...], v_ref[...])

    blk = pl.BlockSpec((None, tr, c), lambda a, i: (a, i, 0))
    return pl.pallas_call(body, grid=(lh, r // tr), in_specs=[blk] * 4, out_specs=[blk] * 3,
                          out_shape=[SDS(w.shape, F32)] * 3, name=name)(w, g, m, v)


def _small_update(parts, w, m, v):
    def body(p_ref, w_ref, m_ref, v_ref, g_ref, d_ref, nm_ref, nv_ref):
        g = p_ref[0]
        for d in range(1, N_DEV):
            g = g + p_ref[d]
        g_ref[...] = g
        d_ref[...], nm_ref[...], nv_ref[...] = _adamw_math(w_ref[...], g, m_ref[...], v_ref[...])

    return pl.pallas_call(body, out_shape=[SDS(w.shape, F32)] * 4, name="small_update")(parts, w, m, v)


SMALL = ["norm_mix_g", "norm_mem_g", "ret_decay_fwd", "ret_decay_bwd", "ret_norm_g", "pool_w", "pool_scale", "na_q_norm_g",
         "na_k_norm_g", "na_rpb", "mem_q_norm_g", "mem_k_norm_g", "norm_ffn_g"]
BIG = ["w_in", "w_gate", "w_mem_kv", "w_branch", "w_out", "w_ffn_in", "w_ffn_out"]
ORDER = ["norm_mix_g", "norm_mem_g", "w_in", "w_gate", "ret_decay_fwd", "ret_decay_bwd", "ret_norm_g", "pool_w", "pool_scale",
         "na_q_norm_g", "na_k_norm_g", "na_rpb", "mem_q_norm_g", "mem_k_norm_g", "w_mem_kv", "w_branch", "w_out", "norm_ffn_g",
         "w_ffn_in", "w_ffn_out"]


def _pack(arrs):
    rows = []
    for a in arrs:
        f = a.reshape(-1).astype(F32)
        pad = (-f.shape[0]) % (8 * LANES)
        rows.append(jnp.pad(f, (0, pad)).reshape(-1, LANES))
    return jnp.concatenate(rows, axis=0)


def _unpack(p, like):
    out, at = [], 0
    for a in like:
        n = int(np.prod(a.shape))
        rows = -(-n // (8 * LANES)) * 8
        out.append(p[at:at + rows].reshape(-1)[:n].reshape(a.shape))
        at += rows
    return out


def _rope_tables(t_len):
    half = HEAD // 2
    inv = ROPE_THETA ** (-jnp.arange(half, dtype=F32) / half)
    ang = jnp.arange(t_len, dtype=F32)[:, None] * inv[None, :]
    cos, sin = jnp.cos(ang), jnp.sin(ang)
    return jnp.tile(jnp.concatenate([cos, cos], axis=1), (1, NH)), jnp.tile(jnp.concatenate([-sin, sin], axis=1), (1, NH))


def _block_diag(pw):
    out = jnp.zeros((BW, BW), pw.dtype)
    for g in range(NH):
        out = out.at[g * HEAD:(g + 1) * HEAD, g * HEAD:(g + 1) * HEAD].set(pw[g])
    return out


def _fold_heads(v):
    return v.reshape(NH, HEAD).sum(axis=0)


def _fold_lanes(v):
    return v.reshape(NH, HEAD).sum(axis=1)


def kernel(x, mem, norm_mix_g, norm_mem_g, w_in, w_gate, ret_decay_fwd, ret_decay_bwd, ret_norm_g, pool_w, pool_scale, na_q_norm_g, na_k_norm_g, na_rpb, mem_q_norm_g, mem_k_norm_g, w_mem_kv, w_branch, w_out, norm_ffn_g, w_ffn_in, w_ffn_out, loss_target, m_norm_mix_g, m_norm_mem_g, m_w_in, m_w_gate, m_ret_decay_fwd, m_ret_decay_bwd, m_ret_norm_g, m_pool_w, m_pool_scale, m_na_q_norm_g, m_na_k_norm_g, m_na_rpb, m_mem_q_norm_g, m_mem_k_norm_g, m_w_mem_kv, m_w_branch, m_w_out, m_norm_ffn_g, m_w_ffn_in, m_w_ffn_out, v_norm_mix_g, v_norm_mem_g, v_w_in, v_w_gate, v_ret_decay_fwd, v_ret_decay_bwd, v_ret_norm_g, v_pool_w, v_pool_scale, v_na_q_norm_g, v_na_k_norm_g, v_na_rpb, v_mem_q_norm_g, v_mem_k_norm_g, v_w_mem_kv, v_w_branch, v_w_out, v_norm_ffn_g, v_w_ffn_in, v_w_ffn_out):
    args = dict(locals())
    W = {n: args[n] for n in ORDER}
    M = {n: args["m_" + n] for n in ORDER}
    V = {n: args["v_" + n] for n in ORDER}
    n_layers, d_model = norm_mix_g.shape
    assert x.shape[0] == 1
    t_len = x.shape[1]
    xc, yc, cc = lax.axis_index("x"), lax.axis_index("y"), lax.axis_index("c")
    me_chip = 2 * xc + yc

    def as3(a):
        return a.reshape(a.shape[0], -1, a.shape[-1])

    shards = [as3(W[n]).astype(BF16) for n in BIG]
    gathered = dict(zip(BIG, _all_gather_shards(shards)))
    wi = gathered["w_in"]
    wg = gathered["w_gate"]
    wmkv = gathered["w_mem_kv"].reshape(n_layers, 1, d_model, 2 * BW)
    wb = gathered["w_branch"].reshape(n_layers, N_CHIPS, NH, BW, BW)
    wo = gathered["w_out"].reshape(n_layers, 1, d_model, d_model)
    wfi = gathered["w_ffn_in"]
    wfo = gathered["w_ffn_out"].reshape(n_layers, 1, -1, d_model)
    ff = wfo.shape[2]

    cos, sin = _rope_tables(t_len)
    btabs = _na_bias_table(na_rpb)
    lane = lambda a: a.reshape(1, -1).astype(F32)
    x2d, mem2d, tgt = x[0], mem[0], loss_target[0]

    saved = []
    for l in range(n_layers):
        tag = f"l{l}"
        s = {"x": x2d}
        s["decf"], s["decb"] = lane(jnp.repeat(ret_decay_fwd[l], HEAD)), lane(jnp.repeat(ret_decay_bwd[l], HEAD))
        s["gn"], s["pscale"] = lane(ret_norm_g[l]), lane(pool_scale[l])
        s["wbd"] = _block_diag(pool_w[l])
        s["nagq"], s["nagk"] = lane(jnp.tile(na_q_norm_g[l], NH)), lane(jnp.tile(na_k_norm_g[l], NH))
        s["mgq"], s["mgk"] = lane(jnp.tile(mem_q_norm_g[l], NH)), lane(jnp.tile(mem_k_norm_g[l], NH))
        s["btab"] = btabs[l]
        s["h"] = _rms_fwd(f"rms_mix_{tag}", x2d, lane(norm_mix_g[l]))
        proj3 = _mm_nn(f"mm_in_{tag}", s["h"], wi, l, out3d=True)
        s["proj"] = proj3.transpose(1, 0, 2).reshape(t_len, -1)
        s["gp3"] = _mm_nn(f"mm_gate_{tag}", s["h"], wg, l, out3d=True)
        s["hm"] = _rms_fwd(f"rms_mem_{tag}", mem2d, lane(norm_mem_g[l]))
        s["memkv"] = _mm_nn(f"mm_memkv_{tag}", s["hm"], wmkv, l)
        ret, s["ret_saved"] = _ret_fwd(tag, s["proj"], cos, sin, s["decf"], s["decb"], s["gn"])
        s["pvp"] = jnp.pad(s["proj"][:, 4 * BW:5 * BW], ((POOL_PAD, POOL_PAD), (0, 0)))
        pool = _pool_fwd(tag, s["pvp"], s["wbd"], s["pscale"], t_len)[POOL_PAD:POOL_PAD + t_len]
        na = _na_fwd(tag, s["proj"], s["btab"], s["nagq"], s["nagk"])
        mo = _mem_fwd(tag, s["proj"], s["memkv"], s["mgq"], s["mgk"])
        s["br"] = jnp.stack([ret, pool, na, mo])
        s["merged"] = _gate_fwd(f"gate_{tag}", s["gp3"], s["br"], wb, l)
        s["x2"] = _mm_nn(f"mm_out_{tag}", s["merged"], wo, l, add=x2d)
        s["h2"] = _rms_fwd(f"rms_ffn_{tag}", s["x2"], lane(norm_ffn_g[l]))
        s["ag3"] = _mm_nn(f"mm_ffn_in_{tag}", s["h2"], wfi, l, out3d=True)
        s["u"] = _swiglu_fwd(f"swiglu_{tag}", s["ag3"])
        x2d = _mm_nn(f"mm_ffn_out_{tag}", s["u"], wfo, l, add=s["x2"])
        saved.append(s)

    loss_part, dx = _loss_kernel("loss", x2d, tgt)

    def empty(shape):
        return lax.empty(shape, F32)

    g_wi, g_wg = empty(wi.shape), empty(wg.shape)
    g_wmkv, g_wo, g_wfi, g_wfo = empty(wmkv.shape), empty(wo.shape), empty(wfi.shape), empty(wfo.shape)
    g_wb = [None] * n_layers
    dbtabs = [None] * n_layers
    gsmall = {n: [None] * n_layers for n in SMALL}
    for l in reversed(range(n_layers)):
        tag = f"l{l}"
        s = saved[l]
        du = _mm_nt(f"mm_ffn_out_dx_{tag}", dx, wfo, l)
        g_wfo = _mm_tn(f"mm_ffn_out_dw_{tag}", s["u"], dx, g_wfo, l)
        dag3 = _swiglu_bwd(f"swiglu_bwd_{tag}", s["ag3"], du)
        dh2 = _mm_nt(f"mm_ffn_in_dx_{tag}", dag3, wfi, l)
        g_wfi = _mm_tn(f"mm_ffn_in_dw_{tag}", s["h2"], dag3, g_wfi, l)
        dx2, dg = _rms_bwd(f"rms_ffn_bwd_{tag}", s["x2"], lane(norm_ffn_g[l]), [dh2], add=dx)
        gsmall["norm_ffn_g"][l] = dg[0]
        dmerged = _mm_nt(f"mm_out_dx_{tag}", dx2, wo, l)
        g_wo = _mm_tn(f"mm_out_dw_{tag}", s["merged"], dx2, g_wo, l)
        dgp3, dbr, g_wb[l] = _gate_bwd(f"gate_bwd_{tag}", s["gp3"], s["br"], wb, l, dmerged)
        g_wg = _mm_tn(f"mm_gate_dw_{tag}", s["h"], dgp3, g_wg, l)
        dh_a = _mm_nt(f"mm_gate_dx_{tag}", dgp3, wg, l)
        drq, drk, drv, drg, ddf, ddb, dgn = _ret_bwd(tag, s["proj"], cos, sin, s["decf"], s["decb"], s["gn"], s["ret_saved"], dbr[0])
        gsmall["ret_decay_fwd"][l], gsmall["ret_decay_bwd"][l], gsmall["ret_norm_g"][l] = _fold_lanes(ddf), _fold_lanes(ddb), dgn[0]
        dpool_p = jnp.pad(dbr[1], ((POOL_PAD, POOL_PAD), (0, 0)))
        dpvp, dwbd, dps = _pool_bwd(tag, s["pvp"], s["wbd"], s["pscale"], t_len, dpool_p)
        dpv = dpvp[POOL_PAD:POOL_PAD + t_len]
        gsmall["pool_w"][l] = jnp.stack([dwbd[g * HEAD:(g + 1) * HEAD, g * HEAD:(g + 1) * HEAD] for g in range(NH)])
        gsmall["pool_scale"][l] = dps[0]
        dnq, dnk, dnv, dbtabs[l], dgq, dgk = _na_bwd(tag, s["proj"], s["btab"], s["nagq"], s["nagk"], dbr[2])
        gsmall["na_q_norm_g"][l], gsmall["na_k_norm_g"][l] = _fold_heads(dgq), _fold_heads(dgk)
        dmq, dmemkv, dgq, dgk = _mem_bwd(tag, s["proj"], s["memkv"], s["mgq"], s["mgk"], dbr[3])
        gsmall["mem_q_norm_g"][l], gsmall["mem_k_norm_g"][l] = _fold_heads(dgq), _fold_heads(dgk)
        g_wmkv = _mm_tn(f"mm_memkv_dw_{tag}", s["hm"], dmemkv, g_wmkv, l)
        dhm = _mm_nt(f"mm_memkv_dx_{tag}", dmemkv, wmkv, l)
        _, dg = _rms_bwd(f"rms_mem_bwd_{tag}", mem2d, lane(norm_mem_g[l]), [dhm])
        gsmall["norm_mem_g"][l] = dg[0]
        dproj = jnp.concatenate([drq, drk, drv, drg, dpv, dnq, dnk.astype(BF16), dnv.astype(BF16), dmq], axis=1)
        dproj3 = dproj.reshape(t_len, N_CHIPS, -1).transpose(1, 0, 2)
        g_wi = _mm_tn(f"mm_in_dw_{tag}", s["h"], dproj3, g_wi, l)
        dh_b = _mm_nt(f"mm_in_dx_{tag}", dproj3, wi, l)
        dx, dg = _rms_bwd(f"rms_mix_bwd_{tag}", s["x"], lane(norm_mix_g[l]), [dh_a, dh_b], add=dx2)
        gsmall["norm_mix_g"][l] = dg[0]

    full_grads = {
        "w_in": g_wi, "w_gate": g_wg, "w_mem_kv": g_wmkv.reshape(n_layers, N_CHIPS, -1, 2 * BW),
        "w_branch": jnp.stack(g_wb).reshape(n_layers, N_CHIPS, NH * BW, BW), "w_out": g_wo.reshape(n_layers, N_CHIPS, -1, d_model),
        "w_ffn_in": g_wfi, "w_ffn_out": g_wfo.reshape(n_layers, N_CHIPS, ff // N_CHIPS, d_model),
    }
    gs = [full_grads[n] for n in BIG]
    r1 = _pair_exchange(gs)
    sums = [_pair_sum(f"pair_sum_{n}", g, r, cc, me_chip) for n, g, r in zip(BIG, gs, r1)]
    r2 = _chip_scatter([p for p, _ in sums])
    halves = [_chip_sum(f"chip_sum_{n}", own, r, cc) for n, (_, own), r in zip(BIG, sums, r2)]
    reduced = dict(zip(BIG, _half_exchange(halves)))

    out_g, out_d, out_m, out_v = {}, {}, {}, {}
    for n in BIG:
        shp = W[n].shape
        d, nm, nv = _adamw(f"adamw_{n}", as3(W[n]), reduced[n], as3(M[n]), as3(V[n]))
        out_g[n], out_d[n], out_m[n], out_v[n] = reduced[n].reshape(shp), d.reshape(shp), nm.reshape(shp), nv.reshape(shp)

    gsmall["na_rpb"] = list(_na_bias_grad(jnp.stack(dbtabs)))
    small_g = [jnp.stack(gsmall[n]).reshape(W[n].shape) for n in SMALL] + [loss_part]
    like = [W[n] for n in SMALL] + [loss_part]
    zero = jnp.zeros((1, 1), F32)
    parts = _gather_small(_pack(small_g))
    sg, sd, sm, sv = _small_update(parts, _pack([W[n] for n in SMALL] + [zero]), _pack([M[n] for n in SMALL] + [zero]),
                                   _pack([V[n] for n in SMALL] + [zero]))
    sg, sd, sm, sv = _unpack(sg, like), _unpack(sd, like), _unpack(sm, like), _unpack(sv, like)
    for i, n in enumerate(SMALL):
        out_g[n], out_d[n], out_m[n], out_v[n] = sg[i], sd[i], sm[i], sv[i]
    loss = sg[-1].reshape(())

    return (loss, dx.reshape(x.shape), *[out_g[n] for n in ORDER], *[out_d[n] for n in ORDER],
            *[out_m[n] for n in ORDER], *[out_v[n] for n in ORDER])
```

```python
import functools

import numpy as np
import jax
import jax.numpy as jnp
from jax import lax
from jax.experimental import pallas as pl
from jax.experimental.pallas import tpu as pltpu

F32 = jnp.float32
BF16 = jnp.bfloat16
SDS = jax.ShapeDtypeStruct
MESH = pl.DeviceIdType.MESH
ANY = pl.BlockSpec(memory_space=pl.ANY)

HEAD = 64
NH = 4
BW = NH * HEAD
CH = 128
GRID_W = 64
NA_ROWS = 8
NA_COLS = 16
POOL_HALF = (1, 2, 4, 8)
POOL_PAD = 16
ROPE_THETA = 10000.0
EPS = 1e-6
NEG_INF = -1e30
N_CHIPS = 4
N_DEV = 8
LANES = 128

ADAM_LR = 0.001
ADAM_B1 = 0.9
ADAM_B2 = 0.999
ADAM_EPS = 1e-08
ADAM_WD = 0.01
ADAM_STEP = 10

VMEM_BIG = 56 << 20
VMEM_MM = 44 << 20


def _cp(vmem=None):
    return pltpu.CompilerParams(vmem_limit_bytes=vmem) if vmem else None


def _d(a, b, ca, cb):
    return lax.dot_general(a.astype(BF16), b.astype(BF16), (((ca,), (cb,)), ((), ())),
                           preferred_element_type=F32)


@jax.custom_vjp
def _nn(a, b):
    return _d(a, b, 1, 0)


def _nn_f(a, b):
    return _d(a, b, 1, 0), (a, b)


def _nn_b(r, g):
    a, b = r
    return _d(g, b, 1, 1).astype(a.dtype), _d(a, g, 0, 0).astype(b.dtype)


_nn.defvjp(_nn_f, _nn_b)


@jax.custom_vjp
def _nt(a, b):
    return _d(a, b, 1, 1)


def _nt_f(a, b):
    return _d(a, b, 1, 1), (a, b)


def _nt_b(r, g):
    a, b = r
    return _d(g, b, 1, 0).astype(a.dtype), _d(g, a, 0, 0).astype(b.dtype)


_nt.defvjp(_nt_f, _nt_b)


@jax.custom_vjp
def _tn(a, b):
    return _d(a, b, 0, 0)


def _tn_f(a, b):
    return _d(a, b, 0, 0), (a, b)


def _tn_b(r, g):
    a, b = r
    return _d(b, g, 1, 1).astype(a.dtype), _d(a, g, 1, 0).astype(b.dtype)


_tn.defvjp(_tn_f, _tn_b)


@functools.partial(jax.custom_vjp, nondiff_argnums=(1,))
def _rollr(x, s):
    return pltpu.roll(x, s % x.shape[0], 0)


def _rollr_f(x, s):
    return _rollr(x, s), None


def _rollr_b(s, _, g):
    return (_rollr(g, -s),)


_rollr.defvjp(_rollr_f, _rollr_b)


@jax.custom_vjp
def _swap32(t):
    n = t.shape[1]
    lane = lax.broadcasted_iota(jnp.int32, (1, n), 1)
    return jnp.where((lane & (HEAD // 2)) == 0, pltpu.roll(t, n - HEAD // 2, 1), pltpu.roll(t, HEAD // 2, 1))


def _swap32_f(t):
    return _swap32(t), None


def _swap32_b(_, g):
    return (_swap32(g),)


_swap32.defvjp(_swap32_f, _swap32_b)


def _head_masks():
    lane = lax.broadcasted_iota(jnp.int32, (1, BW), 1)
    return [(lane >= HEAD * h) & (lane < HEAD * (h + 1)) for h in range(NH)]


def _head_rms(x, g):
    out = jnp.zeros_like(x)
    for mh in _head_masks():
        ms = jnp.sum(jnp.where(mh, x * x, 0.0), axis=-1, keepdims=True) * (1.0 / HEAD)
        out = out + jnp.where(mh, x * lax.rsqrt(ms + EPS), 0.0)
    return out * g


def _rms(x, g):
    return x * lax.rsqrt(jnp.mean(x * x, axis=-1, keepdims=True) + EPS) * g


def _rot(t, cos, sin):
    return t * cos + _swap32(t) * sin


def _softmax(s):
    e = jnp.exp(s - lax.stop_gradient(jnp.max(s, axis=-1, keepdims=True)))
    return e / jnp.sum(e, axis=-1, keepdims=True)


def _ret_kv_f(k, v, cos, sin, decf, decb):
    lgf, lgb = jax.nn.log_sigmoid(decf), jax.nn.log_sigmoid(decb)
    kr = _rot(k, cos, sin)
    idx = lax.broadcasted_iota(jnp.int32, (CH, 1), 0).astype(F32)
    r = lax.broadcasted_iota(jnp.int32, (BW, BW), 0) // HEAD
    c = lax.broadcasted_iota(jnp.int32, (BW, BW), 1) // HEAD
    bd = r == c
    kvf = jnp.where(bd, _tn(kr * jnp.exp((CH - 1 - idx) * lgf), v), 0.0)
    kvb = jnp.where(bd, _tn(kr * jnp.exp(idx * lgb), v), 0.0)
    return kvf, kvb


def _ret_scan_f(kvf, kvb, decf, decb):
    n = len(kvf)
    cdf = jnp.exp(CH * jax.nn.log_sigmoid(decf))
    cdb = jnp.exp(CH * jax.nn.log_sigmoid(decb))
    z = jnp.zeros((BW, BW), F32)
    sf, st = [z], z
    for a in range(n - 1):
        st = cdf * st + kvf[a]
        sf.append(st)
    sb, st = [z], z
    for a in range(n - 1, 0, -1):
        st = cdb * st + kvb[a]
        sb.append(st)
    return sf, sb[::-1]


def _ret_out_f(q, k, v, g, cos, sin, stf, stb, decf, decb, gn):
    lgf, lgb = jax.nn.log_sigmoid(decf), jax.nn.log_sigmoid(decb)
    qr = _rot(q, cos, sin) * (HEAD ** -0.5)
    kr = _rot(k, cos, sin)
    diff = (lax.broadcasted_iota(jnp.int32, (CH, CH), 0) - lax.broadcasted_iota(jnp.int32, (CH, CH), 1)).astype(F32)
    o = jnp.zeros_like(q)
    for mh in _head_masks():
        lf = jnp.sum(jnp.where(mh, lgf, 0.0), axis=1, keepdims=True) * (1.0 / HEAD)
        lb = jnp.sum(jnp.where(mh, lgb, 0.0), axis=1, keepdims=True) * (1.0 / HEAD)
        s = _nt(jnp.where(mh, qr, 0.0), kr)
        p = s * jnp.exp(jnp.where(diff >= 0, diff * lf, -diff * lb))
        o = o + jnp.where(mh, _nn(p, v), 0.0)
    idx = lax.broadcasted_iota(jnp.int32, (CH, 1), 0).astype(F32)
    o = o + _nn(qr * jnp.exp((idx + 1) * lgf), stf) + _nn(qr * jnp.exp((CH - idx) * lgb), stb)
    return _head_rms(o, gn) * (g * jax.nn.sigmoid(g))


def _pool_f(pvp, wbd, scale, t_len):
    n = pvp.shape[0]
    w2 = pvp + _rollr(pvp, 1)
    w4 = _rollr(w2, 1) + _rollr(w2, -1)
    w8 = _rollr(w4, 2) + _rollr(w4, -2)
    w16 = _rollr(w8, 4) + _rollr(w8, -4)
    grp = lax.broadcasted_iota(jnp.int32, (1, BW), 1) // HEAD
    ws = jnp.where(grp == 0, w2, jnp.where(grp == 1, w4, jnp.where(grp == 2, w8, w16)))
    half = jnp.where(grp == 0, POOL_HALF[0], jnp.where(grp == 1, POOL_HALF[1], jnp.where(grp == 2, POOL_HALF[2], POOL_HALF[3])))
    t = lax.broadcasted_iota(jnp.int32, (n, 1), 0) - POOL_PAD
    cnt = jnp.minimum(t + half, t_len) - jnp.maximum(t - half, 0)
    cnt = jnp.where((t >= 0) & (t < t_len), cnt, 1).astype(F32)
    pooled = ws / cnt - pvp
    return _nn(pooled, wbd) * scale


def _na_f(q, kw, vw, bias, gq, gk):
    qn = _head_rms(q, gq)
    kn = _head_rms(kw, gk)
    o = jnp.zeros_like(q)
    for h, mh in enumerate(_head_masks()):
        s = _nt(jnp.where(mh, qn, 0.0), kn) * (HEAD ** -0.5) + jnp.concatenate(bias[h], axis=1)
        o = o + jnp.where(mh, _nn(_softmax(s), vw), 0.0)
    return o


def _mem_f(q, mk, mv, gq, gk):
    qn = _head_rms(q, gq)
    kn = _head_rms(mk, gk)
    o = jnp.zeros_like(q)
    for mh in _head_masks():
        s = _nt(jnp.where(mh, qn, 0.0), kn) * (HEAD ** -0.5)
        o = o + jnp.where(mh, _nn(_softmax(s), mv), 0.0)
    return o


def _gate_f(gp, br, wb):
    out = None
    for n in range(NH):
        t = jax.nn.sigmoid(gp[n]) * _nn(br[n], wb[n])
        out = t if out is None else out + t
    return out


def _swiglu_f(a, g):
    return a * jax.nn.sigmoid(a) * g


def _tile(n, prefs):
    for p in prefs:
        if n % p == 0:
            return p
    return n


def _mm_call(name, a, b, extra, dn, grid, a_spec, b_spec, extra_specs, o_spec, out_shape, acc_shape, nred, after=None):
    add = len(extra) == 1
    if after is not None:
        extra, extra_specs = [*extra, after], [*extra_specs, ANY]
    n_extra = len(extra)

    def body(*refs):
        a_ref, b_ref = refs[0], refs[1]
        o_ref, acc = refs[2 + n_extra], refs[3 + n_extra]
        first = functools.reduce(jnp.logical_and, [pl.program_id(len(grid) - 1 - i) == 0 for i in range(nred)])
        last = functools.reduce(jnp.logical_and, [pl.program_id(len(grid) - 1 - i) == grid[len(grid) - 1 - i] - 1 for i in range(nred)])

        @pl.when(first)
        def _():
            acc[...] = jnp.zeros_like(acc)

        acc[...] += lax.dot_general(a_ref[...].astype(BF16), b_ref[...].astype(BF16), dn, preferred_element_type=F32)

        @pl.when(last)
        def _():
            r = acc[...]
            if add:
                r = r + refs[2][...]
            o_ref[...] = r.astype(o_ref.dtype)

    return pl.pallas_call(
        body, grid=grid, in_specs=[a_spec, b_spec, *extra_specs], out_specs=o_spec, out_shape=out_shape,
        scratch_shapes=[pltpu.VMEM(acc_shape, F32)], name=name, compiler_params=_cp(VMEM_MM),
    )(a, b, *extra)


def _mm_nn(name, a, w, l, *, out3d=False, add=None, out_dtype=F32):
    m, k = a.shape
    _, s_n, _, ns = w.shape
    tm = _tile(m, (512, 256))
    tk = _tile(k, (1024, 1408, 512, 256))
    tn = ns
    grid = (m // tm, s_n, ns // tn, k // tk)
    a_spec = pl.BlockSpec((tm, tk), lambda i, s, j, kk: (i, kk))
    b_spec = pl.BlockSpec((None, None, tk, tn), lambda i, s, j, kk: (l, s, kk, j))
    if out3d:
        o_spec = pl.BlockSpec((None, tm, tn), lambda i, s, j, kk: (s, i, j))
        out_shape = SDS((s_n, m, ns), out_dtype)
    else:
        nj = ns // tn
        o_spec = pl.BlockSpec((tm, tn), lambda i, s, j, kk: (i, s * nj + j))
        out_shape = SDS((m, s_n * ns), out_dtype)
    extra, especs = [], []
    if add is not None:
        assert not out3d
        nj = ns // tn
        extra, especs = [add], [pl.BlockSpec((tm, tn), lambda i, s, j, kk: (i, s * nj + j))]
    return _mm_call(name, a, w, extra, (((1,), (0,)), ((), ())), grid, a_spec, b_spec, especs, o_spec, out_shape, (tm, tn), 1)


def _mm_nt(name, g, w, l, after=None):
    _, s_n, k, ns = w.shape
    g3d = g.ndim == 3
    m = g.shape[1] if g3d else g.shape[0]
    tm = _tile(m, (512, 256))
    to = _tile(k, (1024, 1408, 512, 256))
    tk = ns if (ns % LANES or ns <= 1408) else _tile(ns, (1024, 512))
    nks = ns // tk
    grid = (m // tm, k // to, s_n, nks)
    if g3d:
        a_spec = pl.BlockSpec((None, tm, tk), lambda i, j, s, r: (s, i, r))
    else:
        a_spec = pl.BlockSpec((tm, tk), lambda i, j, s, r: (i, s * nks + r))
    b_spec = pl.BlockSpec((None, None, to, tk), lambda i, j, s, r: (l, s, j, r))
    o_spec = pl.BlockSpec((tm, to), lambda i, j, s, r: (i, j))
    return _mm_call(name, g, w, [], (((1,), (1,)), ((), ())), grid, a_spec, b_spec, [], o_spec, SDS((m, k), F32), (tm, to), 2,
                    after=after)


def _mm_tn(name, a, g, s_n):
    t_len, k = a.shape
    g3d = g.ndim == 3
    ns = g.shape[2] if g3d else g.shape[1] // s_n
    tt = _tile(t_len, (1024, 512))
    to = _tile(k, (512, 256, 128))
    tn = ns
    nj = ns // tn
    grid = (s_n, k // to, nj, t_len // tt)
    a_spec = pl.BlockSpec((tt, to), lambda s, i, j, t: (t, i))
    if g3d:
        b_spec = pl.BlockSpec((None, tt, tn), lambda s, i, j, t: (s, t, j))
    else:
        b_spec = pl.BlockSpec((tt, tn), lambda s, i, j, t: (t, s * nj + j))
    o_spec = pl.BlockSpec((None, to, tn), lambda s, i, j, t: (s, i, j))
    return _mm_call(name, a, g, [], (((0,), (0,)), ((), ())), grid, a_spec, b_spec, [], o_spec,
                    SDS((s_n, k, ns), F32), (to, tn), 1)


def _rms_fwd(name, x, g):
    m, d = x.shape
    tm = _tile(m, (256,))

    def body(x_ref, g_ref, o_ref):
        o_ref[...] = _rms(x_ref[...], g_ref[...]).astype(BF16)

    row = pl.BlockSpec((tm, d), lambda i: (i, 0))
    return pl.pallas_call(body, grid=(m // tm,), in_specs=[row, pl.BlockSpec((1, d), lambda i: (0, 0))], out_specs=row,
                          out_shape=SDS((m, d), BF16), name=name)(x, g)


def _rms_bwd(name, x, g, dhs, add=None):
    m, d = x.shape
    tm = _tile(m, (256,))
    nh = len(dhs)

    def body(*refs):
        x_ref, g_ref = refs[0], refs[1]
        dh = refs[2][...]
        for r in refs[3:2 + nh]:
            dh = dh + r[...]
        dx_ref, dg_ref = refs[-2], refs[-1]
        _, vjp = jax.vjp(_rms, x_ref[...], g_ref[...])
        dx, dg = vjp(dh)
        if add is not None:
            dx = dx + refs[2 + nh][...]
        dx_ref[...] = dx

        @pl.when(pl.program_id(0) == 0)
        def _():
            dg_ref[...] = jnp.zeros_like(dg_ref)

        dg_ref[...] += dg

    row = pl.BlockSpec((tm, d), lambda i: (i, 0))
    vec = pl.BlockSpec((1, d), lambda i: (0, 0))
    ins = [x, g, *dhs] + ([add] if add is not None else [])
    return pl.pallas_call(body, grid=(m // tm,), in_specs=[row, vec] + [row] * (len(ins) - 2), out_specs=[row, vec],
                          out_shape=[SDS((m, d), F32), SDS((1, d), F32)], name=name)(*ins)


def _swiglu_fwd(name, ag3):
    _, t_len, w = ag3.shape
    tm = _tile(t_len, (512, 256))

    def body(a_ref, g_ref, o_ref):
        o_ref[...] = _swiglu_f(a_ref[...], g_ref[...]).astype(BF16)

    return pl.pallas_call(
        body, grid=(t_len // tm, 2),
        in_specs=[pl.BlockSpec((None, tm, w), lambda i, j: (j, i, 0)), pl.BlockSpec((None, tm, w), lambda i, j: (j + 2, i, 0))],
        out_specs=pl.BlockSpec((tm, w), lambda i, j: (i, j)), out_shape=SDS((t_len, 2 * w), BF16), name=name)(ag3, ag3)


def _swiglu_bwd(name, ag3, du):
    _, t_len, w = ag3.shape
    tm = _tile(t_len, (512, 256))

    def body(a_ref, g_ref, du_ref, o_ref):
        _, vjp = jax.vjp(_swiglu_f, a_ref[...], g_ref[...])
        da, dg = vjp(du_ref[...])
        o_ref[...] = jnp.where(pl.program_id(1) < 2, da, dg).astype(BF16)

    return pl.pallas_call(
        body, grid=(t_len // tm, 4),
        in_specs=[pl.BlockSpec((None, tm, w), lambda i, s: (s % 2, i, 0)), pl.BlockSpec((None, tm, w), lambda i, s: (s % 2 + 2, i, 0)),
                  pl.BlockSpec((tm, w), lambda i, s: (i, s % 2))],
        out_specs=pl.BlockSpec((None, tm, w), lambda i, s: (s, i, 0)), out_shape=SDS(ag3.shape, BF16), name=name)(ag3, ag3, du)


def _gate_fwd(name, gp3, br, wb, l):
    _, t_len, d = gp3.shape
    tm = _tile(t_len, (512, 256))

    def body(gp_ref, br_ref, wb_ref, o_ref):
        o_ref[...] = _gate_f([gp_ref[n] for n in range(NH)], [br_ref[n] for n in range(NH)],
                             [wb_ref[n] for n in range(NH)]).astype(BF16)

    return pl.pallas_call(
        body, grid=(t_len // tm, d // BW),
        in_specs=[pl.BlockSpec((NH, tm, BW), lambda i, s: (0, i, s)), pl.BlockSpec((NH, tm, BW), lambda i, s: (0, i, 0)),
                  pl.BlockSpec((None, None, NH, BW, BW), lambda i, s: (l, s, 0, 0, 0))],
        out_specs=pl.BlockSpec((tm, BW), lambda i, s: (i, s)), out_shape=SDS((t_len, d), BF16), name=name)(gp3, br, wb)


def _gate_bwd(name, gp3, br, wb, l, dmerged):
    _, t_len, d = gp3.shape
    tm = _tile(t_len, (512, 256))
    ns = d // BW

    def body(gp_ref, br_ref, wb_ref, dm_ref, dgp_ref, dbr_ref, dwb_ref):
        i, s = pl.program_id(0), pl.program_id(1)
        gp = [gp_ref[n] for n in range(NH)]
        brv = [br_ref[n].astype(F32) for n in range(NH)]
        wbv = [wb_ref[n].astype(F32) for n in range(NH)]
        _, vjp = jax.vjp(_gate_f, gp, brv, wbv)
        dgp, dbr, dwb = vjp(dm_ref[...])

        @pl.when(s == 0)
        def _():
            dbr_ref[...] = jnp.zeros_like(dbr_ref)

        @pl.when((i == 0) & (s == 0))
        def _():
            dwb_ref[...] = jnp.zeros_like(dwb_ref)

        for n in range(NH):
            dgp_ref[n] = dgp[n].astype(BF16)
            dbr_ref[n] += dbr[n]
            dwb_ref[s, n] += dwb[n]

    return pl.pallas_call(
        body, grid=(t_len // tm, ns),
        in_specs=[pl.BlockSpec((NH, tm, BW), lambda i, s: (0, i, s)), pl.BlockSpec((NH, tm, BW), lambda i, s: (0, i, 0)),
                  pl.BlockSpec((None, None, NH, BW, BW), lambda i, s: (l, s, 0, 0, 0)), pl.BlockSpec((tm, BW), lambda i, s: (i, s))],
        out_specs=[pl.BlockSpec((NH, tm, BW), lambda i, s: (0, i, s)), pl.BlockSpec((NH, tm, BW), lambda i, s: (0, i, 0)),
                   pl.BlockSpec((ns, NH, BW, BW), lambda i, s: (0, 0, 0, 0))],
        out_shape=[SDS(gp3.shape, BF16), SDS((NH, t_len, BW), F32), SDS((ns, NH, BW, BW), F32)], name=name,
        compiler_params=_cp(VMEM_MM))(gp3, br, wb, dmerged)


def _loss_kernel(name, y, tgt):
    m, d = y.shape
    tm = _tile(m, (256,))

    def body(y_ref, t_ref, l_ref, dy_ref):
        err = y_ref[...] - t_ref[...]
        dy_ref[...] = err * (1.0 / d)

        @pl.when(pl.program_id(0) == 0)
        def _():
            l_ref[...] = jnp.zeros_like(l_ref)

        l_ref[...] += 0.5 * jnp.sum(jnp.mean(err * err, axis=-1, keepdims=True), axis=0, keepdims=True)

    row = pl.BlockSpec((tm, d), lambda i: (i, 0))
    return pl.pallas_call(body, grid=(m // tm,), in_specs=[row, row], out_specs=[pl.BlockSpec((1, 1), lambda i: (0, 0)), row],
                          out_shape=[SDS((1, 1), F32), SDS((m, d), F32)], name=name)(y, tgt)


def _zero_first(refs):
    @pl.when(pl.program_id(0) == 0)
    def _():
        for r in refs:
            r[...] = jnp.zeros_like(r)


def _grp(rows, j):
    return pl.BlockSpec((rows, BW), lambda a, j=j: (a, j))


_VEC = pl.BlockSpec((1, BW), lambda a: (0, 0))


def _ret_fwd(tag, proj, cos, sin, decf, decb, gn):
    t_len = proj.shape[0]
    n = t_len // CH
    tab = pl.BlockSpec((CH, BW), lambda a: (a, 0))
    st = pl.BlockSpec((None, BW, BW), lambda a: (a, 0, 0))

    def k1(k_ref, v_ref, cos_ref, sin_ref, df_ref, db_ref, kvf_ref, kvb_ref):
        kvf_ref[...], kvb_ref[...] = _ret_kv_f(k_ref[...], v_ref[...], cos_ref[...], sin_ref[...], df_ref[...], db_ref[...])

    kvf, kvb = pl.pallas_call(
        k1, grid=(n,), in_specs=[_grp(CH, 1), _grp(CH, 2), tab, tab, _VEC, _VEC], out_specs=[st, st],
        out_shape=[SDS((n, BW, BW), F32)] * 2, name=f"ret_kv_{tag}")(proj, proj, cos, sin, decf, decb)

    def k2(kvf_ref, kvb_ref, df_ref, db_ref, sf_ref, sb_ref):
        sf, sb = _ret_scan_f([kvf_ref[a] for a in range(n)], [kvb_ref[a] for a in range(n)], df_ref[...], db_ref[...])
        for a in range(n):
            sf_ref[a] = sf[a]
            sb_ref[a] = sb[a]

    stf, stb = pl.pallas_call(k2, out_shape=[SDS((n, BW, BW), F32)] * 2, name=f"ret_scan_{tag}",
                              compiler_params=_cp(VMEM_BIG))(kvf, kvb, decf, decb)

    def k3(q_ref, k_ref, v_ref, g_ref, cos_ref, sin_ref, sf_ref, sb_ref, df_ref, db_ref, gn_ref, o_ref):
        o_ref[...] = _ret_out_f(q_ref[...], k_ref[...], v_ref[...], g_ref[...], cos_ref[...], sin_ref[...], sf_ref[...],
                                sb_ref[...], df_ref[...], db_ref[...], gn_ref[...]).astype(BF16)

    ret = pl.pallas_call(
        k3, grid=(n,), in_specs=[_grp(CH, 0), _grp(CH, 1), _grp(CH, 2), _grp(CH, 3), tab, tab, st, st, _VEC, _VEC, _VEC],
        out_specs=tab, out_shape=SDS((t_len, BW), BF16), name=f"ret_out_{tag}")(proj, proj, proj, proj, cos, sin, stf, stb, decf, decb, gn)
    return ret, (kvf, kvb, stf, stb)


def _ret_bwd(tag, proj, cos, sin, decf, decb, gn, saved, dret):
    kvf, kvb, stf, stb = saved
    t_len = proj.shape[0]
    n = t_len // CH
    tab = pl.BlockSpec((CH, BW), lambda a: (a, 0))
    st = pl.BlockSpec((None, BW, BW), lambda a: (a, 0, 0))

    def k3(q_ref, k_ref, v_ref, g_ref, cos_ref, sin_ref, sf_ref, sb_ref, df_ref, db_ref, gn_ref, do_ref,
           dq_ref, dk_ref, dv_ref, dg_ref, dsf_ref, dsb_ref, ddf_ref, ddb_ref, dgn_ref):
        cos, sin = cos_ref[...], sin_ref[...]
        f = lambda q, k, v, g, sf, sb, df, db, gnv: _ret_out_f(q, k, v, g, cos, sin, sf, sb, df, db, gnv)
        _, vjp = jax.vjp(f, q_ref[...], k_ref[...], v_ref[...], g_ref[...], sf_ref[...], sb_ref[...], df_ref[...], db_ref[...], gn_ref[...])
        dq, dk, dv, dg, dsf, dsb, ddf, ddb, dgn = vjp(do_ref[...])
        dq_ref[...] = dq.astype(BF16)
        dk_ref[...] = dk
        dv_ref[...] = dv
        dg_ref[...] = dg.astype(BF16)
        dsf_ref[...] = dsf
        dsb_ref[...] = dsb
        _zero_first([ddf_ref, ddb_ref, dgn_ref])
        ddf_ref[...] += ddf
        ddb_ref[...] += ddb
        dgn_ref[...] += dgn

    dq, dk3, dv3, dg, dstf, dstb, ddf3, ddb3, dgn = pl.pallas_call(
        k3, grid=(n,),
        in_specs=[_grp(CH, 0), _grp(CH, 1), _grp(CH, 2), _grp(CH, 3), tab, tab, st, st, _VEC, _VEC, _VEC, tab],
        out_specs=[tab, tab, tab, tab, st, st, _VEC, _VEC, _VEC],
        out_shape=[SDS((t_len, BW), BF16), SDS((t_len, BW), F32), SDS((t_len, BW), F32), SDS((t_len, BW), BF16),
                   SDS((n, BW, BW), F32), SDS((n, BW, BW), F32), SDS((1, BW), F32), SDS((1, BW), F32), SDS((1, BW), F32)],
        name=f"ret_out_bwd_{tag}")(proj, proj, proj, proj, cos, sin, stf, stb, decf, decb, gn, dret)

    def k2(kvf_ref, kvb_ref, df_ref, db_ref, dsf_ref, dsb_ref, dkvf_ref, dkvb_ref, ddf_ref, ddb_ref):
        _, vjp = jax.vjp(_ret_scan_f, [kvf_ref[a] for a in range(n)], [kvb_ref[a] for a in range(n)], df_ref[...], db_ref[...])
        dkvf, dkvb, ddf_ref[...], ddb_ref[...] = vjp(([dsf_ref[a] for a in range(n)], [dsb_ref[a] for a in range(n)]))
        for a in range(n):
            dkvf_ref[a] = dkvf[a]
            dkvb_ref[a] = dkvb[a]

    dkvf, dkvb, ddf2, ddb2 = pl.pallas_call(
        k2, out_shape=[SDS((n, BW, BW), F32), SDS((n, BW, BW), F32), SDS((1, BW), F32), SDS((1, BW), F32)],
        name=f"ret_scan_bwd_{tag}", compiler_params=_cp(VMEM_BIG))(kvf, kvb, decf, decb, dstf, dstb)

    def k1(k_ref, v_ref, cos_ref, sin_ref, df_ref, db_ref, dkvf_ref, dkvb_ref, dk3_ref, dv3_ref, dk_ref, dv_ref, ddf_ref, ddb_ref):
        cos, sin = cos_ref[...], sin_ref[...]
        f = lambda k, v, df, db: _ret_kv_f(k, v, cos, sin, df, db)
        _, vjp = jax.vjp(f, k_ref[...], v_ref[...], df_ref[...], db_ref[...])
        dk, dv, ddf, ddb = vjp((dkvf_ref[...], dkvb_ref[...]))
        dk_ref[...] = (dk + dk3_ref[...]).astype(BF16)
        dv_ref[...] = (dv + dv3_ref[...]).astype(BF16)
        _zero_first([ddf_ref, ddb_ref])
        ddf_ref[...] += ddf
        ddb_ref[...] += ddb

    dk, dv, ddf1, ddb1 = pl.pallas_call(
        k1, grid=(n,), in_specs=[_grp(CH, 1), _grp(CH, 2), tab, tab, _VEC, _VEC, st, st, tab, tab],
        out_specs=[tab, tab, _VEC, _VEC],
        out_shape=[SDS((t_len, BW), BF16), SDS((t_len, BW), BF16), SDS((1, BW), F32), SDS((1, BW), F32)],
        name=f"ret_kv_bwd_{tag}")(proj, proj, cos, sin, decf, decb, dkvf, dkvb, dk3, dv3)
    return dq, dk, dv, dg, ddf1 + ddf2 + ddf3, ddb1 + ddb2 + ddb3, dgn


def _pool_fwd(tag, pvp, wbd, scale, t_len):
    def body(p_ref, w_ref, s_ref, o_ref):
        o_ref[...] = _pool_f(p_ref[...], w_ref[...], s_ref[...], t_len).astype(BF16)

    return pl.pallas_call(body, out_shape=SDS(pvp.shape, BF16), name=f"pool_{tag}", compiler_params=_cp(VMEM_BIG))(pvp, wbd, scale)


def _pool_bwd(tag, pvp, wbd, scale, t_len, dpool_p):
    def body(p_ref, w_ref, s_ref, do_ref, dp_ref, dw_ref, ds_ref):
        f = lambda p, w, s: _pool_f(p, w, s, t_len)
        _, vjp = jax.vjp(f, p_ref[...], w_ref[...], s_ref[...])
        dp, dw, ds = vjp(do_ref[...])
        dp_ref[...] = dp.astype(BF16)
        dw_ref[...] = dw
        ds_ref[...] = ds

    return pl.pallas_call(body, out_shape=[SDS(pvp.shape, BF16), SDS((BW, BW), F32), SDS((1, BW), F32)],
                          name=f"pool_bwd_{tag}", compiler_params=_cp(VMEM_BIG))(pvp, wbd, scale, dpool_p)


def _na_window(r, rows):
    r0 = jnp.clip(r - NA_ROWS // 2, 0, rows - NA_ROWS)
    return r0, r - r0


NA_PAIRS = 2 * NA_ROWS - 2


def _na_bias_pieces(b_ref, pat):
    return [[b_ref[h * NA_PAIRS + 2 * k - pat + NA_ROWS - 1] for k in range(NA_ROWS // 2)] for h in range(NH)]


def _na_fwd(tag, proj, bias_tab, gq, gk):
    t_len = proj.shape[0]
    rows = t_len // GRID_W
    win = NA_ROWS * GRID_W

    def body(q_ref, k_ref, v_ref, b_ref, gq_ref, gk_ref, o_ref):
        r0, pat = _na_window(pl.program_id(0), rows)
        start = pl.multiple_of(r0 * GRID_W, GRID_W)
        o_ref[...] = _na_f(q_ref[...], k_ref[pl.ds(start, win), :], v_ref[pl.ds(start, win), :], _na_bias_pieces(b_ref, pat),
                           gq_ref[...], gk_ref[...]).astype(BF16)

    full = lambda j: pl.BlockSpec((t_len, BW), lambda a, j=j: (0, j))
    return pl.pallas_call(
        body, grid=(rows,),
        in_specs=[_grp(GRID_W, 5), full(6), full(7), pl.BlockSpec(bias_tab.shape, lambda a: (0, 0, 0)), _VEC, _VEC],
        out_specs=pl.BlockSpec((GRID_W, BW), lambda a: (a, 0)), out_shape=SDS((t_len, BW), BF16),
        name=f"na_{tag}", compiler_params=_cp(VMEM_MM))(proj, proj, proj, bias_tab, gq, gk)


def _na_bwd(tag, proj, bias_tab, gq, gk, dna):
    t_len = proj.shape[0]
    rows = t_len // GRID_W
    win = NA_ROWS * GRID_W

    def body(q_ref, k_ref, v_ref, b_ref, gq_ref, gk_ref, do_ref, dq_ref, dk_ref, dv_ref, db_ref, dgq_ref, dgk_ref):
        r0, pat = _na_window(pl.program_id(0), rows)
        start = pl.multiple_of(r0 * GRID_W, GRID_W)
        _zero_first([dk_ref, dv_ref, db_ref, dgq_ref, dgk_ref])
        _, vjp = jax.vjp(_na_f, q_ref[...], k_ref[pl.ds(start, win), :], v_ref[pl.ds(start, win), :], _na_bias_pieces(b_ref, pat),
                         gq_ref[...], gk_ref[...])
        dq, dk, dv, db, dgq, dgk = vjp(do_ref[...])
        dq_ref[...] = dq.astype(BF16)
        dk_ref[pl.ds(start, win), :] += dk
        dv_ref[pl.ds(start, win), :] += dv
        for h in range(NH):
            for k in range(NA_ROWS // 2):
                db_ref[h * NA_PAIRS + 2 * k - pat + NA_ROWS - 1] += db[h][k]
        dgq_ref[...] += dgq
        dgk_ref[...] += dgk

    full = lambda j: pl.BlockSpec((t_len, BW), lambda a, j=j: (0, j))
    whole = pl.BlockSpec((t_len, BW), lambda a: (0, 0))
    tabspec = pl.BlockSpec(bias_tab.shape, lambda a: (0, 0, 0))
    row = pl.BlockSpec((GRID_W, BW), lambda a: (a, 0))
    return pl.pallas_call(
        body, grid=(rows,), in_specs=[_grp(GRID_W, 5), full(6), full(7), tabspec, _VEC, _VEC, row],
        out_specs=[row, whole, whole, tabspec, _VEC, _VEC],
        out_shape=[SDS((t_len, BW), BF16), SDS((t_len, BW), F32), SDS((t_len, BW), F32), SDS(bias_tab.shape, F32),
                   SDS((1, BW), F32), SDS((1, BW), F32)],
        name=f"na_bwd_{tag}", compiler_params=_cp(VMEM_BIG))(proj, proj, proj, bias_tab, gq, gk, dna)


def _mem_fwd(tag, proj, memkv, gq, gk):
    t_len = proj.shape[0]
    n_mem = memkv.shape[0]
    tm = _tile(t_len, (256,))

    def body(q_ref, mk_ref, mv_ref, gq_ref, gk_ref, o_ref):
        o_ref[...] = _mem_f(q_ref[...], mk_ref[...], mv_ref[...], gq_ref[...], gk_ref[...]).astype(BF16)

    mspec = lambda j: pl.BlockSpec((n_mem, BW), lambda a, j=j: (0, j))
    return pl.pallas_call(
        body, grid=(t_len // tm,), in_specs=[_grp(tm, 8), mspec(0), mspec(1), _VEC, _VEC],
        out_specs=pl.BlockSpec((tm, BW), lambda a: (a, 0)), out_shape=SDS((t_len, BW), BF16), name=f"mem_{tag}")(proj, memkv, memkv, gq, gk)


def _mem_bwd(tag, proj, memkv, gq, gk, dmo):
    t_len = proj.shape[0]
    n_mem = memkv.shape[0]
    tm = _tile(t_len, (256,))

    def body(q_ref, mk_ref, mv_ref, gq_ref, gk_ref, do_ref, dq_ref, dmk_ref, dmv_ref, dgq_ref, dgk_ref):
        _zero_first([dmk_ref, dmv_ref, dgq_ref, dgk_ref])
        _, vjp = jax.vjp(_mem_f, q_ref[...], mk_ref[...], mv_ref[...], gq_ref[...], gk_ref[...])
        dq, dmk, dmv, dgq, dgk = vjp(do_ref[...])
        dq_ref[...] = dq.astype(BF16)
        dmk_ref[...] += dmk
        dmv_ref[...] += dmv
        dgq_ref[...] += dgq
        dgk_ref[...] += dgk

    mspec = lambda j: pl.BlockSpec((n_mem, BW), lambda a, j=j: (0, j))
    mout = pl.BlockSpec((n_mem, BW), lambda a: (0, 0))
    row = pl.BlockSpec((tm, BW), lambda a: (a, 0))
    dq, dmk, dmv, dgq, dgk = pl.pallas_call(
        body, grid=(t_len // tm,), in_specs=[_grp(tm, 8), mspec(0), mspec(1), _VEC, _VEC, row],
        out_specs=[row, mout, mout, _VEC, _VEC],
        out_shape=[SDS((t_len, BW), BF16), SDS((n_mem, BW), F32), SDS((n_mem, BW), F32), SDS((1, BW), F32), SDS((1, BW), F32)],
        name=f"mem_bwd_{tag}")(proj, memkv, memkv, gq, gk, dmo)
    return dq, jnp.concatenate([dmk, dmv], axis=1), dgq, dgk


NA_NJ = 2 * NA_COLS - 1


def _na_onehot():
    q = np.arange(GRID_W)[:, None]
    kc = np.arange(GRID_W)[None, :]
    qwin = np.clip(q - NA_COLS // 2, 0, GRID_W - NA_COLS)
    mask = (kc >= qwin) & (kc < qwin + NA_COLS)
    col = np.clip(kc - q, -(NA_COLS - 1), NA_COLS - 1) + NA_COLS - 1
    onehot = np.zeros((LANES, GRID_W, 2, GRID_W), np.float32)
    qq, kk = np.nonzero(mask)
    for half in range(2):
        onehot[half * NA_NJ + col[qq, kk], qq, half, kk] = 1.0
    valid = np.broadcast_to(mask[:, None, :], (GRID_W, 2, GRID_W)).astype(np.float32)
    return onehot.reshape(LANES, -1), valid.reshape(1, -1)


def _na_pair_rows(rpb):
    n_layers = rpb.shape[0]
    pair = jnp.concatenate([rpb[:, :, :-1], rpb[:, :, 1:]], axis=-1).reshape(n_layers, NH * NA_PAIRS, 2 * NA_NJ)
    return jnp.pad(pair, ((0, 0), (0, 0), (0, LANES - 2 * NA_NJ)))


def _na_bias_table(rpb):
    n_layers = rpb.shape[0]
    onehot, valid = _na_onehot()
    width = onehot.shape[1]

    def body(r_ref, oh_ref, ok_ref, o_ref):
        t = jnp.dot(r_ref[...], oh_ref[...], precision=lax.Precision.HIGHEST, preferred_element_type=F32)
        o_ref[...] = jnp.where(ok_ref[...] > 0, t, NEG_INF)

    out = pl.pallas_call(
        body, grid=(n_layers,),
        in_specs=[pl.BlockSpec((None, NH * NA_PAIRS, LANES), lambda l: (l, 0, 0)), pl.BlockSpec((LANES, width), lambda l: (0, 0)),
                  pl.BlockSpec((1, width), lambda l: (0, 0))],
        out_specs=pl.BlockSpec((None, NH * NA_PAIRS, width), lambda l: (l, 0, 0)),
        out_shape=SDS((n_layers, NH * NA_PAIRS, width), F32), name="na_bias_table")(
            _na_pair_rows(rpb), jnp.asarray(onehot), jnp.asarray(valid))
    return out.reshape(n_layers, NH * NA_PAIRS, GRID_W, 2 * GRID_W)


def _na_bias_grad(dtab):
    n_layers = dtab.shape[0]
    onehot, _ = _na_onehot()
    width = onehot.shape[1]

    def body(x_ref, oh_ref, o_ref):
        o_ref[...] = jnp.dot(x_ref[...], oh_ref[...], precision=lax.Precision.HIGHEST, preferred_element_type=F32)

    out = pl.pallas_call(
        body, grid=(n_layers,),
        in_specs=[pl.BlockSpec((None, NH * NA_PAIRS, width), lambda l: (l, 0, 0)), pl.BlockSpec((width, LANES), lambda l: (0, 0))],
        out_specs=pl.BlockSpec((None, NH * NA_PAIRS, LANES), lambda l: (l, 0, 0)),
        out_shape=SDS((n_layers, NH * NA_PAIRS, LANES), F32), name="na_bias_grad")(
            dtab.reshape(n_layers, NH * NA_PAIRS, width), jnp.asarray(onehot.T.copy()))
    out = out.reshape(n_layers, NH, NA_PAIRS, LANES)
    zero = jnp.zeros((n_layers, NH, 1, NA_NJ), F32)
    return (jnp.concatenate([out[..., :NA_NJ], zero], axis=2) + jnp.concatenate([zero, out[..., NA_NJ:2 * NA_NJ]], axis=2))


def _place():
    x, y, c = lax.axis_index("x"), lax.axis_index("y"), lax.axis_index("c")
    chips = [(1 - x, y), (x, 1 - y), (1 - x, 1 - y)]
    return x, y, c, chips


def _remote(src, dst, ssem, rsem, dev):
    return pltpu.make_async_remote_copy(src_ref=src, dst_ref=dst, send_sem=ssem, recv_sem=rsem, device_id=dev, device_id_type=MESH)


HBM = pl.BlockSpec(memory_space=pltpu.HBM)
SEM = pl.BlockSpec(memory_space=pltpu.SEMAPHORE)
DATAFLOW = pltpu.SideEffectType.DATAFLOW_SIDE_EFFECTING


def _split_copies(name, srcs, lands, n_copies, plan):
    n_s, n_l = len(srcs), len(lands)
    hbm = lambda a: pltpu.HBM(a.shape, a.dtype)

    def start_body(*refs):
        ins = refs[:n_s + n_l]
        outs = refs[n_s + n_l:]
        sems, token = outs[:2 * n_copies], outs[-1]
        for k, (src, dst, dev) in enumerate(plan(ins[:n_s], ins[n_s:])):
            _remote(src, dst, sems[k], sems[n_copies + k], dev).start()
        token[...] = jnp.zeros_like(token)

    outs = pl.pallas_call(
        start_body, name=f"{name}_start",
        out_shape=(pltpu.SemaphoreType.DMA(()),) * (2 * n_copies) + tuple(hbm(a) for a in (*srcs, *lands)) + (SDS((8, LANES), F32),),
        in_specs=(HBM,) * (n_s + n_l), out_specs=(SEM,) * (2 * n_copies) + (HBM,) * (n_s + n_l) + (pl.BlockSpec(memory_space=pltpu.VMEM),),
        input_output_aliases={i: 2 * n_copies + i for i in range(n_s + n_l)},
        compiler_params=pltpu.CompilerParams(has_side_effects=DATAFLOW),
    )(*[pltpu.with_memory_space_constraint(a, pltpu.HBM) for a in (*srcs, *lands)])
    sems, thru, token = outs[:2 * n_copies], outs[2 * n_copies:-1], outs[-1]

    def wait(after_wait):
        def wait_body(*refs):
            ins = refs[:n_s + n_l]
            sem_refs = refs[n_s + n_l:n_s + n_l + 2 * n_copies]
            for k, (src, dst, dev) in enumerate(plan(ins[:n_s], ins[n_s:])):
                cp = _remote(src, dst, sem_refs[k], sem_refs[n_copies + k], dev)
                cp.wait_send()
                cp.wait_recv()

        res = pl.pallas_call(
            wait_body, name=f"{name}_wait", out_shape=tuple(hbm(a) for a in (*srcs, *lands)),
            in_specs=(HBM,) * (n_s + n_l) + (SEM,) * (2 * n_copies) + (ANY,), out_specs=(HBM,) * (n_s + n_l),
            input_output_aliases={i: i for i in range(n_s + n_l)},
            compiler_params=pltpu.CompilerParams(has_side_effects=DATAFLOW),
        )(*thru, *sems, after_wait)
        return list(res[n_s:])

    return token, wait


def _gather_plan(srcs, lands):
    x, y, c, chips = _place()
    me = 2 * x + y
    out = []
    for src, land in zip(srcs, lands):
        h = src.shape[0] // 2
        out.append((src, land.at[me], (x, y, 1 - c)))
        for px, py in chips:
            out.append((src.at[pl.ds(c * h, h)], land.at[me, pl.ds(c * h, h)], (px, py, c)))
    return out


def _scatter_plan(srcs, lands):
    x, y, c, chips = _place()
    out = []
    for src, land in zip(srcs, lands):
        for r, (px, py) in enumerate(chips):
            out.append((src.at[2 * px + py], land.at[r], (px, py, c)))
    return out


def _forward_halves(name, lands):
    n = len(lands)

    def body(*refs):
        ins, outs = refs[:n], refs[n:2 * n]
        ssem, rsem = refs[2 * n:]
        x, y, c, chips = _place()
        sib = (x, y, 1 - c)

        def copy(i, r, half):
            h = ins[i].shape[1] // 2
            px, py = chips[r]
            rows = pl.ds(half * h, h)
            return _remote(ins[i].at[2 * px + py, rows], outs[i].at[2 * px + py, rows], ssem.at[i, r], rsem.at[i, r], sib)

        for i in range(n):
            for r in range(3):
                copy(i, r, c).start()
        for i in range(n):
            for r in range(3):
                copy(i, r, 1 - c).wait_recv()
                copy(i, r, c).wait_send()

    return pl.pallas_call(
        body, in_specs=[ANY] * n, out_specs=[ANY] * n, out_shape=[SDS(a.shape, a.dtype) for a in lands],
        input_output_aliases={i: i for i in range(n)},
        scratch_shapes=[pltpu.SemaphoreType.DMA((n, 3)), pltpu.SemaphoreType.DMA((n, 3))], name=name)(*lands)


def _pair_exchange(name, gs):
    n = len(gs)

    def body(*refs):
        ins, outs = refs[:n], refs[n:2 * n]
        ssem, rsem = refs[2 * n:]
        x, y, c, _ = _place()
        cps = []
        for i in range(n):
            h = ins[i].shape[1] // 2
            cps.append(_remote(ins[i].at[:, pl.ds((1 - c) * h, h)], outs[i], ssem.at[i], rsem.at[i], (x, y, 1 - c)))
            cps[-1].start()
        for cp in cps:
            cp.wait()

    return pl.pallas_call(
        body, in_specs=[ANY] * n, out_specs=[ANY] * n,
        out_shape=[SDS((g.shape[0], g.shape[1] // 2, g.shape[2]), g.dtype) for g in gs],
        scratch_shapes=[pltpu.SemaphoreType.DMA((n,)), pltpu.SemaphoreType.DMA((n,))], name=name)(*gs)


def _half_exchange(name, fs):
    n = len(fs)

    def body(*refs):
        ins, outs = refs[:n], refs[n:2 * n]
        ssem, rsem = refs[2 * n:]
        x, y, c, _ = _place()

        def copy(i, half):
            h = ins[i].shape[0] // 2
            return _remote(ins[i].at[pl.ds(half * h, h)], outs[i].at[pl.ds(half * h, h)], ssem.at[i], rsem.at[i], (x, y, 1 - c))

        for i in range(n):
            copy(i, c).start()
        for i in range(n):
            copy(i, 1 - c).wait_recv()
            copy(i, c).wait_send()

    return pl.pallas_call(
        body, in_specs=[ANY] * n, out_specs=[ANY] * n, out_shape=[SDS(f.shape, f.dtype) for f in fs],
        input_output_aliases={i: i for i in range(n)},
        scratch_shapes=[pltpu.SemaphoreType.DMA((n,)), pltpu.SemaphoreType.DMA((n,))], name=name)(*fs)


def _gather_small(v):
    def body(v_ref, o_ref, lsem, ssem, rsem):
        x, y, c, _ = _place()
        me = 4 * x + 2 * y + c
        flips = [(fx, fy, fc) for fx in (0, 1) for fy in (0, 1) for fc in (0, 1)][1:]
        peer = lambda f: (1 - x if f[0] else x, 1 - y if f[1] else y, 1 - c if f[2] else c)
        loc = pltpu.make_async_copy(v_ref, o_ref.at[me], lsem)
        loc.start()
        cps = [_remote(v_ref, o_ref.at[me], ssem.at[k], rsem.at[k], peer(f)) for k, f in enumerate(flips)]
        for cp in cps:
            cp.start()
        for k, f in enumerate(flips):
            px, py, pc = peer(f)
            land = o_ref.at[4 * px + 2 * py + pc]
            _remote(land, land, ssem.at[k], rsem.at[k], peer(f)).wait_recv()
        for cp in cps:
            cp.wait_send()
        loc.wait()

    return pl.pallas_call(
        body, in_specs=[ANY], out_specs=ANY, out_shape=SDS((N_DEV,) + v.shape, v.dtype),
        scratch_shapes=[pltpu.SemaphoreType.DMA, pltpu.SemaphoreType.DMA((N_DEV - 1,)), pltpu.SemaphoreType.DMA((N_DEV - 1,))],
        name="gather_small")(v)


def _rows_tile(r):
    return _tile(r, (256, 352, 128, 64))


def _pair_sum(name, g, r1, core, me):
    n_l, s_n, h, c = r1.shape
    tr = _rows_tile(h)
    nb = h // tr

    def body(idx_ref, g_ref, r_ref, gm_ref, rm_ref, pb_ref, own_ref):
        pb_ref[...] = (g_ref[...] + r_ref[...]).astype(BF16)
        own_ref[...] = gm_ref[...] + rm_ref[...]

    blk = (None, None, tr, c)
    grid_spec = pltpu.PrefetchScalarGridSpec(
        num_scalar_prefetch=1, grid=(n_l, nb, s_n),
        in_specs=[pl.BlockSpec(blk, lambda a, i, s, idx: (a, s, idx[0] * nb + i, 0)), pl.BlockSpec(blk, lambda a, i, s, idx: (a, s, i, 0)),
                  pl.BlockSpec(blk, lambda a, i, s, idx: (a, idx[1], idx[0] * nb + i, 0)), pl.BlockSpec(blk, lambda a, i, s, idx: (a, idx[1], i, 0))],
        out_specs=[pl.BlockSpec(blk, lambda a, i, s, idx: (a, s, i, 0)), pl.BlockSpec((None, tr, c), lambda a, i, s, idx: (a, i, 0))])
    idx = jnp.stack([core, me]).astype(jnp.int32)
    return pl.pallas_call(body, grid_spec=grid_spec, out_shape=[SDS(r1.shape, BF16), SDS((n_l, h, c), F32)], name=name)(idx, g, r1, g, r1)


def _chip_sum(name, own, r2, core):
    n_l, h, c = own.shape
    tr = _rows_tile(h)
    nb = h // tr

    def body(idx_ref, o_ref, r_ref, f_ref):
        f_ref[...] = ((o_ref[...] + r_ref[0].astype(F32)) + r_ref[1].astype(F32)) + r_ref[2].astype(F32)

    grid_spec = pltpu.PrefetchScalarGridSpec(
        num_scalar_prefetch=1, grid=(n_l, nb),
        in_specs=[pl.BlockSpec((None, tr, c), lambda a, i, idx: (a, i, 0)), pl.BlockSpec((3, None, tr, c), lambda a, i, idx: (0, a, i, 0))],
        out_specs=pl.BlockSpec((None, tr, c), lambda a, i, idx: (a, idx[0] * nb + i, 0)))
    return pl.pallas_call(body, grid_spec=grid_spec, out_shape=SDS((n_l, 2 * h, c), F32), name=name)(
        jnp.reshape(core, (1,)).astype(jnp.int32), own, r2)


def _adamw_math(w, g, m, v):
    m = ADAM_B1 * m + (1.0 - ADAM_B1) * g
    v = ADAM_B2 * v + (1.0 - ADAM_B2) * jnp.square(g)
    m_hat = m / (1.0 - ADAM_B1 ** ADAM_STEP)
    v_hat = v / (1.0 - ADAM_B2 ** ADAM_STEP)
    delta = -ADAM_LR * (m_hat / (jnp.sqrt(v_hat) + ADAM_EPS) + ADAM_WD * w)
    return delta, m, v


def _adamw(name, l, w, g, m, v, bufs):
    _, r, c = w.shape
    tr = _rows_tile(r)

    def body(w_ref, g_ref, m_ref, v_ref, *rest):
        go_ref, d_ref, nm_ref, nv_ref = rest[4:]
        g = g_ref[...]
        go_ref[...] = g
        d_ref[...], nm_ref[...], nv_ref[...] = _adamw_math(w_ref[...], g, m_ref[...], v_ref[...])

    blk = pl.BlockSpec((None, tr, c), lambda i: (l, i, 0))
    return pl.pallas_call(body, grid=(r // tr,), in_specs=[blk, pl.BlockSpec((tr, c), lambda i: (i, 0)), blk, blk] + [ANY] * 4,
                          out_specs=[blk] * 4, out_shape=[SDS(w.shape, F32)] * 4, input_output_aliases={4 + k: k for k in range(4)},
                          name=name)(w, g, m, v, *bufs)


def _small_update(parts, w, m, v):
    def body(p_ref, w_ref, m_ref, v_ref, g_ref, d_ref, nm_ref, nv_ref):
        g = p_ref[0]
        for d in range(1, N_DEV):
            g = g + p_ref[d]
        g_ref[...] = g
        d_ref[...], nm_ref[...], nv_ref[...] = _adamw_math(w_ref[...], g, m_ref[...], v_ref[...])

    return pl.pallas_call(body, out_shape=[SDS(w.shape, F32)] * 4, name="small_update")(parts, w, m, v)


SMALL = ["norm_mix_g", "norm_mem_g", "ret_decay_fwd", "ret_decay_bwd", "ret_norm_g", "pool_w", "pool_scale", "na_q_norm_g",
         "na_k_norm_g", "na_rpb", "mem_q_norm_g", "mem_k_norm_g", "norm_ffn_g"]
BIG = ["w_in", "w_gate", "w_mem_kv", "w_branch", "w_out", "w_ffn_in", "w_ffn_out"]
ORDER = ["norm_mix_g", "norm_mem_g", "w_in", "w_gate", "ret_decay_fwd", "ret_decay_bwd", "ret_norm_g", "pool_w", "pool_scale",
         "na_q_norm_g", "na_k_norm_g", "na_rpb", "mem_q_norm_g", "mem_k_norm_g", "w_mem_kv", "w_branch", "w_out", "norm_ffn_g",
         "w_ffn_in", "w_ffn_out"]


def _pack(arrs):
    rows = []
    for a in arrs:
        f = a.reshape(-1).astype(F32)
        pad = (-f.shape[0]) % (8 * LANES)
        rows.append(jnp.pad(f, (0, pad)).reshape(-1, LANES))
    return jnp.concatenate(rows, axis=0)


def _unpack(p, like):
    out, at = [], 0
    for a in like:
        n = int(np.prod(a.shape))
        rows = -(-n // (8 * LANES)) * 8
        out.append(p[at:at + rows].reshape(-1)[:n].reshape(a.shape))
        at += rows
    return out


def _rope_tables(t_len):
    half = HEAD // 2
    inv = ROPE_THETA ** (-jnp.arange(half, dtype=F32) / half)
    ang = jnp.arange(t_len, dtype=F32)[:, None] * inv[None, :]
    cos, sin = jnp.cos(ang), jnp.sin(ang)
    return jnp.tile(jnp.concatenate([cos, cos], axis=1), (1, NH)), jnp.tile(jnp.concatenate([-sin, sin], axis=1), (1, NH))


def _block_diag(pw):
    out = jnp.zeros((BW, BW), pw.dtype)
    for g in range(NH):
        out = out.at[g * HEAD:(g + 1) * HEAD, g * HEAD:(g + 1) * HEAD].set(pw[g])
    return out


def _fold_heads(v):
    return v.reshape(NH, HEAD).sum(axis=0)


def _fold_lanes(v):
    return v.reshape(NH, HEAD).sum(axis=1)


def kernel(x, mem, norm_mix_g, norm_mem_g, w_in, w_gate, ret_decay_fwd, ret_decay_bwd, ret_norm_g, pool_w, pool_scale, na_q_norm_g, na_k_norm_g, na_rpb, mem_q_norm_g, mem_k_norm_g, w_mem_kv, w_branch, w_out, norm_ffn_g, w_ffn_in, w_ffn_out, loss_target, m_norm_mix_g, m_norm_mem_g, m_w_in, m_w_gate, m_ret_decay_fwd, m_ret_decay_bwd, m_ret_norm_g, m_pool_w, m_pool_scale, m_na_q_norm_g, m_na_k_norm_g, m_na_rpb, m_mem_q_norm_g, m_mem_k_norm_g, m_w_mem_kv, m_w_branch, m_w_out, m_norm_ffn_g, m_w_ffn_in, m_w_ffn_out, v_norm_mix_g, v_norm_mem_g, v_w_in, v_w_gate, v_ret_decay_fwd, v_ret_decay_bwd, v_ret_norm_g, v_pool_w, v_pool_scale, v_na_q_norm_g, v_na_k_norm_g, v_na_rpb, v_mem_q_norm_g, v_mem_k_norm_g, v_w_mem_kv, v_w_branch, v_w_out, v_norm_ffn_g, v_w_ffn_in, v_w_ffn_out):
    args = dict(locals())
    W = {n: args[n] for n in ORDER}
    M = {n: args["m_" + n] for n in ORDER}
    V = {n: args["v_" + n] for n in ORDER}
    n_layers, d_model = norm_mix_g.shape
    assert x.shape[0] == 1
    t_len = x.shape[1]
    xc, yc, cc = lax.axis_index("x"), lax.axis_index("y"), lax.axis_index("c")
    me_chip = 2 * xc + yc

    def as3(a):
        return a.reshape(a.shape[0], -1, a.shape[-1])

    shards = [as3(W[n]).astype(BF16) for n in BIG]
    gather_waits, token = [], jnp.zeros((), F32)
    for l in range(n_layers):
        srcs = [sh[l] for sh in shards]
        lands = [lax.empty((N_CHIPS,) + a.shape, BF16) for a in srcs]
        tok, wait = _split_copies(f"gather_l{l}", srcs, lands, 4 * len(BIG), _gather_plan)
        gather_waits.append(wait)
        token = token + tok[0, 0]

    cos, sin = _rope_tables(t_len)
    btabs = _na_bias_table(na_rpb)
    lane = lambda a: a.reshape(1, -1).astype(F32)
    x2d, mem2d, tgt = x[0] + token, mem[0], loss_target[0]

    saved = []
    for l in range(n_layers):
        tag = f"l{l}"
        gathered = dict(zip(BIG, _forward_halves(f"gather_pass_{tag}", gather_waits[l](x2d))))
        wi = gathered["w_in"][None]
        wg = gathered["w_gate"][None]
        wmkv = gathered["w_mem_kv"].reshape(1, 1, d_model, 2 * BW)
        wb = gathered["w_branch"].reshape(1, N_CHIPS, NH, BW, BW)
        wo = gathered["w_out"].reshape(1, 1, d_model, d_model)
        wfi = gathered["w_ffn_in"][None]
        wfo = gathered["w_ffn_out"].reshape(1, 1, -1, d_model)
        s = {"x": x2d, "w": (wi, wg, wmkv, wb, wo, wfi, wfo)}
        s["decf"], s["decb"] = lane(jnp.repeat(ret_decay_fwd[l], HEAD)), lane(jnp.repeat(ret_decay_bwd[l], HEAD))
        s["gn"], s["pscale"] = lane(ret_norm_g[l]), lane(pool_scale[l])
        s["wbd"] = _block_diag(pool_w[l])
        s["nagq"], s["nagk"] = lane(jnp.tile(na_q_norm_g[l], NH)), lane(jnp.tile(na_k_norm_g[l], NH))
        s["mgq"], s["mgk"] = lane(jnp.tile(mem_q_norm_g[l], NH)), lane(jnp.tile(mem_k_norm_g[l], NH))
        s["btab"] = btabs[l]
        s["h"] = _rms_fwd(f"rms_mix_{tag}", x2d, lane(norm_mix_g[l]))
        proj3 = _mm_nn(f"mm_in_{tag}", s["h"], wi, 0, out3d=True)
        s["proj"] = proj3.transpose(1, 0, 2).reshape(t_len, -1)
        s["gp3"] = _mm_nn(f"mm_gate_{tag}", s["h"], wg, 0, out3d=True)
        s["hm"] = _rms_fwd(f"rms_mem_{tag}", mem2d, lane(norm_mem_g[l]))
        s["memkv"] = _mm_nn(f"mm_memkv_{tag}", s["hm"], wmkv, 0)
        ret, s["ret_saved"] = _ret_fwd(tag, s["proj"], cos, sin, s["decf"], s["decb"], s["gn"])
        s["pvp"] = jnp.pad(s["proj"][:, 4 * BW:5 * BW], ((POOL_PAD, POOL_PAD), (0, 0)))
        pool = _pool_fwd(tag, s["pvp"], s["wbd"], s["pscale"], t_len)[POOL_PAD:POOL_PAD + t_len]
        na = _na_fwd(tag, s["proj"], s["btab"], s["nagq"], s["nagk"])
        mo = _mem_fwd(tag, s["proj"], s["memkv"], s["mgq"], s["mgk"])
        s["br"] = jnp.stack([ret, pool, na, mo])
        s["merged"] = _gate_fwd(f"gate_{tag}", s["gp3"], s["br"], wb, 0)
        s["x2"] = _mm_nn(f"mm_out_{tag}", s["merged"], wo, 0, add=x2d)
        s["h2"] = _rms_fwd(f"rms_ffn_{tag}", s["x2"], lane(norm_ffn_g[l]))
        s["ag3"] = _mm_nn(f"mm_ffn_in_{tag}", s["h2"], wfi, 0, out3d=True)
        s["u"] = _swiglu_fwd(f"swiglu_{tag}", s["ag3"])
        x2d = _mm_nn(f"mm_ffn_out_{tag}", s["u"], wfo, 0, add=s["x2"])
        saved.append(s)

    loss_part, dx = _loss_kernel("loss", x2d, tgt)

    dbtabs = [None] * n_layers
    gsmall = {n: [None] * n_layers for n in SMALL}
    out_bufs = {n: tuple(lax.empty(as3(W[n]).shape, F32) for _ in range(4)) for n in BIG}

    def finish_layer(l, pending, after):
        sums, wait = pending
        r2 = wait(after)
        halves = [_chip_sum(f"chip_sum_{n}_l{l}", own, r[:, None], cc)[0] for n, (_, own), r in zip(BIG, sums, r2)]
        for n, f in zip(BIG, _half_exchange(f"grad_half_exchange_l{l}", halves)):
            out_bufs[n] = _adamw(f"adamw_{n}_l{l}", l, as3(W[n]), f, as3(M[n]), as3(V[n]), out_bufs[n])

    pending, order_after = None, None
    for l in reversed(range(n_layers)):
        tag = f"l{l}"
        s = saved[l]
        wi, wg, wmkv, wb, wo, wfi, wfo = s["w"]
        du = _mm_nt(f"mm_ffn_out_dx_{tag}", dx, wfo, 0, after=order_after)
        g_wfo = _mm_tn(f"mm_ffn_out_dw_{tag}", s["u"], dx, 1)
        dag3 = _swiglu_bwd(f"swiglu_bwd_{tag}", s["ag3"], du)
        dh2 = _mm_nt(f"mm_ffn_in_dx_{tag}", dag3, wfi, 0)
        g_wfi = _mm_tn(f"mm_ffn_in_dw_{tag}", s["h2"], dag3, N_CHIPS)
        dx2, dg = _rms_bwd(f"rms_ffn_bwd_{tag}", s["x2"], lane(norm_ffn_g[l]), [dh2], add=dx)
        gsmall["norm_ffn_g"][l] = dg[0]
        dmerged = _mm_nt(f"mm_out_dx_{tag}", dx2, wo, 0)
        g_wo = _mm_tn(f"mm_out_dw_{tag}", s["merged"], dx2, 1)
        dgp3, dbr, g_wb = _gate_bwd(f"gate_bwd_{tag}", s["gp3"], s["br"], wb, 0, dmerged)
        g_wg = _mm_tn(f"mm_gate_dw_{tag}", s["h"], dgp3, N_CHIPS)
        dh_a = _mm_nt(f"mm_gate_dx_{tag}", dgp3, wg, 0)
        drq, drk, drv, drg, ddf, ddb, dgn = _ret_bwd(tag, s["proj"], cos, sin, s["decf"], s["decb"], s["gn"], s["ret_saved"], dbr[0])
        gsmall["ret_decay_fwd"][l], gsmall["ret_decay_bwd"][l], gsmall["ret_norm_g"][l] = _fold_lanes(ddf), _fold_lanes(ddb), dgn[0]
        dpool_p = jnp.pad(dbr[1], ((POOL_PAD, POOL_PAD), (0, 0)))
        dpvp, dwbd, dps = _pool_bwd(tag, s["pvp"], s["wbd"], s["pscale"], t_len, dpool_p)
        dpv = dpvp[POOL_PAD:POOL_PAD + t_len]
        gsmall["pool_w"][l] = jnp.stack([dwbd[g * HEAD:(g + 1) * HEAD, g * HEAD:(g + 1) * HEAD] for g in range(NH)])
        gsmall["pool_scale"][l] = dps[0]
        dnq, dnk, dnv, dbtabs[l], dgq, dgk = _na_bwd(tag, s["proj"], s["btab"], s["nagq"], s["nagk"], dbr[2])
        gsmall["na_q_norm_g"][l], gsmall["na_k_norm_g"][l] = _fold_heads(dgq), _fold_heads(dgk)
        dmq, dmemkv, dgq, dgk = _mem_bwd(tag, s["proj"], s["memkv"], s["mgq"], s["mgk"], dbr[3])
        gsmall["mem_q_norm_g"][l], gsmall["mem_k_norm_g"][l] = _fold_heads(dgq), _fold_heads(dgk)
        g_wmkv = _mm_tn(f"mm_memkv_dw_{tag}", s["hm"], dmemkv, 1)
        dhm = _mm_nt(f"mm_memkv_dx_{tag}", dmemkv, wmkv, 0)
        _, dg = _rms_bwd(f"rms_mem_bwd_{tag}", mem2d, lane(norm_mem_g[l]), [dhm])
        gsmall["norm_mem_g"][l] = dg[0]
        dproj = jnp.concatenate([drq, drk, drv, drg, dpv, dnq, dnk.astype(BF16), dnv.astype(BF16), dmq], axis=1)
        dproj3 = dproj.reshape(t_len, N_CHIPS, -1).transpose(1, 0, 2)
        g_wi = _mm_tn(f"mm_in_dw_{tag}", s["h"], dproj3, N_CHIPS)
        dh_b = _mm_nt(f"mm_in_dx_{tag}", dproj3, wi, 0)
        dx, dg = _rms_bwd(f"rms_mix_bwd_{tag}", s["x"], lane(norm_mix_g[l]), [dh_a, dh_b], add=dx2)
        gsmall["norm_mix_g"][l] = dg[0]

        full_grads = {
            "w_in": g_wi, "w_gate": g_wg, "w_mem_kv": g_wmkv.reshape(N_CHIPS, -1, 2 * BW), "w_branch": g_wb.reshape(N_CHIPS, NH * BW, BW),
            "w_out": g_wo.reshape(N_CHIPS, -1, d_model), "w_ffn_in": g_wfi, "w_ffn_out": g_wfo.reshape(N_CHIPS, -1, d_model),
        }
        gs = [full_grads[n] for n in BIG]
        r1 = _pair_exchange(f"grad_pair_exchange_{tag}", gs)
        sums = [_pair_sum(f"pair_sum_{n}_{tag}", g[None], r[None], cc, me_chip) for n, g, r in zip(BIG, gs, r1)]
        pbs = [p[0] for p, _ in sums]
        lands = [lax.empty((3,) + p.shape[1:], BF16) for p in pbs]
        tok, wait = _split_copies(f"scatter_{tag}", pbs, lands, 3 * len(BIG), _scatter_plan)
        if pending is not None:
            finish_layer(l + 1, pending, tok)
        pending, order_after = (sums, wait), tok
    finish_layer(0, pending, dx)

    out_g, out_d, out_m, out_v = {}, {}, {}, {}
    for n in BIG:
        shp = W[n].shape
        out_g[n], out_d[n], out_m[n], out_v[n] = [b.reshape(shp) for b in out_bufs[n]]

    gsmall["na_rpb"] = list(_na_bias_grad(jnp.stack(dbtabs)))
    small_g = [jnp.stack(gsmall[n]).reshape(W[n].shape) for n in SMALL] + [loss_part]
    like = [W[n] for n in SMALL] + [loss_part]
    zero = jnp.zeros((1, 1), F32)
    parts = _gather_small(_pack(small_g))
    sg, sd, sm, sv = _small_update(parts, _pack([W[n] for n in SMALL] + [zero]), _pack([M[n] for n in SMALL] + [zero]),
                                   _pack([V[n] for n in SMALL] + [zero]))
    sg, sd, sm, sv = _unpack(sg, like), _unpack(sd, like), _unpack(sm, like), _unpack(sv, like)
    for i, n in enumerate(SMALL):
        out_g[n], out_d[n], out_m[n], out_v[n] = sg[i], sd[i], sm[i], sv[i]
    loss = sg[-1].reshape(())

    return (loss, dx.reshape(x.shape), *[out_g[n] for n in ORDER], *[out_d[n] for n in ORDER],
            *[out_m[n] for n in ORDER], *[out_v[n] for n in ORDER])
```

```python
import functools

import numpy as np
import jax
import jax.numpy as jnp
from jax import lax
from jax.experimental import pallas as pl
from jax.experimental.pallas import tpu as pltpu

F32 = jnp.float32
BF16 = jnp.bfloat16
SDS = jax.ShapeDtypeStruct
MESH = pl.DeviceIdType.MESH
ANY = pl.BlockSpec(memory_space=pl.ANY)

HEAD = 64
NH = 4
BW = NH * HEAD
CH = 128
GRID_W = 64
NA_ROWS = 8
NA_COLS = 16
POOL_HALF = (1, 2, 4, 8)
POOL_PAD = 16
ROPE_THETA = 10000.0
EPS = 1e-6
NEG_INF = -1e30
N_CHIPS = 4
N_DEV = 8
LANES = 128

ADAM_LR = 0.001
ADAM_B1 = 0.9
ADAM_B2 = 0.999
ADAM_EPS = 1e-08
ADAM_WD = 0.01
ADAM_STEP = 10

VMEM_BIG = 56 << 20
VMEM_MM = 44 << 20


def _cp(vmem=None):
    return pltpu.CompilerParams(vmem_limit_bytes=vmem) if vmem else None


def _d(a, b, ca, cb):
    return lax.dot_general(a.astype(BF16), b.astype(BF16), (((ca,), (cb,)), ((), ())),
                           preferred_element_type=F32)


@jax.custom_vjp
def _nn(a, b):
    return _d(a, b, 1, 0)


def _nn_f(a, b):
    return _d(a, b, 1, 0), (a, b)


def _nn_b(r, g):
    a, b = r
    return _d(g, b, 1, 1).astype(a.dtype), _d(a, g, 0, 0).astype(b.dtype)


_nn.defvjp(_nn_f, _nn_b)


@jax.custom_vjp
def _nt(a, b):
    return _d(a, b, 1, 1)


def _nt_f(a, b):
    return _d(a, b, 1, 1), (a, b)


def _nt_b(r, g):
    a, b = r
    return _d(g, b, 1, 0).astype(a.dtype), _d(g, a, 0, 0).astype(b.dtype)


_nt.defvjp(_nt_f, _nt_b)


@jax.custom_vjp
def _tn(a, b):
    return _d(a, b, 0, 0)


def _tn_f(a, b):
    return _d(a, b, 0, 0), (a, b)


def _tn_b(r, g):
    a, b = r
    return _d(b, g, 1, 1).astype(a.dtype), _d(a, g, 1, 0).astype(b.dtype)


_tn.defvjp(_tn_f, _tn_b)


@functools.partial(jax.custom_vjp, nondiff_argnums=(1,))
def _rollr(x, s):
    return pltpu.roll(x, s % x.shape[0], 0)


def _rollr_f(x, s):
    return _rollr(x, s), None


def _rollr_b(s, _, g):
    return (_rollr(g, -s),)


_rollr.defvjp(_rollr_f, _rollr_b)


@jax.custom_vjp
def _swap32(t):
    n = t.shape[1]
    lane = lax.broadcasted_iota(jnp.int32, (1, n), 1)
    return jnp.where((lane & (HEAD // 2)) == 0, pltpu.roll(t, n - HEAD // 2, 1), pltpu.roll(t, HEAD // 2, 1))


def _swap32_f(t):
    return _swap32(t), None


def _swap32_b(_, g):
    return (_swap32(g),)


_swap32.defvjp(_swap32_f, _swap32_b)


def _head_masks():
    lane = lax.broadcasted_iota(jnp.int32, (1, BW), 1)
    return [(lane >= HEAD * h) & (lane < HEAD * (h + 1)) for h in range(NH)]


def _head_rms(x, g):
    out = jnp.zeros_like(x)
    for mh in _head_masks():
        ms = jnp.sum(jnp.where(mh, x * x, 0.0), axis=-1, keepdims=True) * (1.0 / HEAD)
        out = out + jnp.where(mh, x * lax.rsqrt(ms + EPS), 0.0)
    return out * g


def _rms(x, g):
    return x * lax.rsqrt(jnp.mean(x * x, axis=-1, keepdims=True) + EPS) * g


def _rot(t, cos, sin):
    return t * cos + _swap32(t) * sin


def _softmax(s):
    e = jnp.exp(s - lax.stop_gradient(jnp.max(s, axis=-1, keepdims=True)))
    return e / jnp.sum(e, axis=-1, keepdims=True)


def _ret_kv_f(k, v, cos, sin, decf, decb):
    lgf, lgb = jax.nn.log_sigmoid(decf), jax.nn.log_sigmoid(decb)
    kr = _rot(k, cos, sin)
    idx = lax.broadcasted_iota(jnp.int32, (CH, 1), 0).astype(F32)
    r = lax.broadcasted_iota(jnp.int32, (BW, BW), 0) // HEAD
    c = lax.broadcasted_iota(jnp.int32, (BW, BW), 1) // HEAD
    bd = r == c
    kvf = jnp.where(bd, _tn(kr * jnp.exp((CH - 1 - idx) * lgf), v), 0.0)
    kvb = jnp.where(bd, _tn(kr * jnp.exp(idx * lgb), v), 0.0)
    return kvf, kvb


def _ret_scan_f(kvf, kvb, decf, decb):
    n = len(kvf)
    cdf = jnp.exp(CH * jax.nn.log_sigmoid(decf))
    cdb = jnp.exp(CH * jax.nn.log_sigmoid(decb))
    z = jnp.zeros((BW, BW), F32)
    sf, st = [z], z
    for a in range(n - 1):
        st = cdf * st + kvf[a]
        sf.append(st)
    sb, st = [z], z
    for a in range(n - 1, 0, -1):
        st = cdb * st + kvb[a]
        sb.append(st)
    return sf, sb[::-1]


def _ret_out_f(q, k, v, g, cos, sin, stf, stb, decf, decb, gn):
    lgf, lgb = jax.nn.log_sigmoid(decf), jax.nn.log_sigmoid(decb)
    qr = _rot(q, cos, sin) * (HEAD ** -0.5)
    kr = _rot(k, cos, sin)
    diff = (lax.broadcasted_iota(jnp.int32, (CH, CH), 0) - lax.broadcasted_iota(jnp.int32, (CH, CH), 1)).astype(F32)
    o = jnp.zeros_like(q)
    for mh in _head_masks():
        lf = jnp.sum(jnp.where(mh, lgf, 0.0), axis=1, keepdims=True) * (1.0 / HEAD)
        lb = jnp.sum(jnp.where(mh, lgb, 0.0), axis=1, keepdims=True) * (1.0 / HEAD)
        s = _nt(jnp.where(mh, qr, 0.0), kr)
        p = s * jnp.exp(jnp.where(diff >= 0, diff * lf, -diff * lb))
        o = o + jnp.where(mh, _nn(p, v), 0.0)
    idx = lax.broadcasted_iota(jnp.int32, (CH, 1), 0).astype(F32)
    o = o + _nn(qr * jnp.exp((idx + 1) * lgf), stf) + _nn(qr * jnp.exp((CH - idx) * lgb), stb)
    return _head_rms(o, gn) * (g * jax.nn.sigmoid(g))


def _pool_f(pvp, wbd, scale, t_len):
    n = pvp.shape[0]
    w2 = pvp + _rollr(pvp, 1)
    w4 = _rollr(w2, 1) + _rollr(w2, -1)
    w8 = _rollr(w4, 2) + _rollr(w4, -2)
    w16 = _rollr(w8, 4) + _rollr(w8, -4)
    grp = lax.broadcasted_iota(jnp.int32, (1, BW), 1) // HEAD
    ws = jnp.where(grp == 0, w2, jnp.where(grp == 1, w4, jnp.where(grp == 2, w8, w16)))
    half = jnp.where(grp == 0, POOL_HALF[0], jnp.where(grp == 1, POOL_HALF[1], jnp.where(grp == 2, POOL_HALF[2], POOL_HALF[3])))
    t = lax.broadcasted_iota(jnp.int32, (n, 1), 0) - POOL_PAD
    cnt = jnp.minimum(t + half, t_len) - jnp.maximum(t - half, 0)
    cnt = jnp.where((t >= 0) & (t < t_len), cnt, 1).astype(F32)
    pooled = ws / cnt - pvp
    return _nn(pooled, wbd) * scale


def _na_f(q, kn, vw, bias, gq):
    qn = _head_rms(q, gq)
    o = jnp.zeros_like(q)
    for h, mh in enumerate(_head_masks()):
        s = _nt(jnp.where(mh, qn, 0.0), kn) * (HEAD ** -0.5) + jnp.concatenate(bias[h], axis=1)
        o = o + jnp.where(mh, _nn(_softmax(s), vw), 0.0)
    return o


def _mem_f(q, mk, mv, gq, gk):
    qn = _head_rms(q, gq)
    kn = _head_rms(mk, gk)
    o = jnp.zeros_like(q)
    for mh in _head_masks():
        s = _nt(jnp.where(mh, qn, 0.0), kn) * (HEAD ** -0.5)
        o = o + jnp.where(mh, _nn(_softmax(s), mv), 0.0)
    return o


def _gate_f(gp, br, wb):
    out = None
    for n in range(NH):
        t = jax.nn.sigmoid(gp[n]) * _nn(br[n], wb[n])
        out = t if out is None else out + t
    return out


def _swiglu_f(a, g):
    return a * jax.nn.sigmoid(a) * g


def _tile(n, prefs):
    for p in prefs:
        if n % p == 0:
            return p
    return n


def _mm_call(name, a, b, extra, dn, grid, a_spec, b_spec, extra_specs, o_spec, out_shape, acc_shape, nred, after=None):
    add = len(extra) == 1
    if after is not None:
        extra, extra_specs = [*extra, after], [*extra_specs, ANY]
    n_extra = len(extra)
    n_steps = int(np.prod([grid[len(grid) - 1 - i] for i in range(nred)]))

    def body(*refs):
        a_ref, b_ref = refs[0], refs[1]
        o_ref, acc = refs[2 + n_extra], refs[-1]
        red = [len(grid) - 1 - i for i in range(nred)]
        first = functools.reduce(jnp.logical_and, [pl.program_id(ax) == 0 for ax in red])
        last = functools.reduce(jnp.logical_and, [pl.program_id(ax) == grid[ax] - 1 for ax in red])
        d = lax.dot_general(a_ref[...].astype(BF16), b_ref[...].astype(BF16), dn, preferred_element_type=F32)

        def finish(r):
            if add:
                r = r + refs[2][...]
            o_ref[...] = r.astype(o_ref.dtype)

        if n_steps == 1:
            finish(d)
        else:
            @pl.when(first)
            def _():
                acc[...] = d

            @pl.when(jnp.logical_not(first) & jnp.logical_not(last))
            def _():
                acc[...] += d

            @pl.when(last)
            def _():
                finish(acc[...] + d)

    return pl.pallas_call(
        body, grid=grid, in_specs=[a_spec, b_spec, *extra_specs], out_specs=o_spec, out_shape=out_shape,
        scratch_shapes=[pltpu.VMEM(acc_shape, F32)] if n_steps > 1 else [], name=name, compiler_params=_cp(VMEM_MM),
    )(a, b, *extra)


def _mm_nn(name, a, w, l, *, out3d=False, add=None, out_dtype=F32):
    m, k = a.shape
    _, s_n, _, ns = w.shape
    tm = _tile(m, (512, 256))
    tk = _tile(k, (1024, 1408, 512, 256))
    tn = ns
    grid = (m // tm, s_n, ns // tn, k // tk)
    a_spec = pl.BlockSpec((tm, tk), lambda i, s, j, kk: (i, kk))
    b_spec = pl.BlockSpec((None, None, tk, tn), lambda i, s, j, kk: (l, s, kk, j))
    if out3d:
        o_spec = pl.BlockSpec((None, tm, tn), lambda i, s, j, kk: (s, i, j))
        out_shape = SDS((s_n, m, ns), out_dtype)
    else:
        nj = ns // tn
        o_spec = pl.BlockSpec((tm, tn), lambda i, s, j, kk: (i, s * nj + j))
        out_shape = SDS((m, s_n * ns), out_dtype)
    extra, especs = [], []
    if add is not None:
        assert not out3d
        nj = ns // tn
        extra, especs = [add], [pl.BlockSpec((tm, tn), lambda i, s, j, kk: (i, s * nj + j))]
    return _mm_call(name, a, w, extra, (((1,), (0,)), ((), ())), grid, a_spec, b_spec, especs, o_spec, out_shape, (tm, tn), 1)


def _mm_nt(name, g, w, l, after=None):
    _, s_n, k, ns = w.shape
    g3d = g.ndim == 3
    m = g.shape[1] if g3d else g.shape[0]
    tm = _tile(m, (512, 256))
    to = _tile(k, (1024, 1408, 512, 256))
    tk = ns if (ns % LANES or ns <= 1408) else _tile(ns, (1024, 512))
    nks = ns // tk
    grid = (m // tm, k // to, s_n, nks)
    if g3d:
        a_spec = pl.BlockSpec((None, tm, tk), lambda i, j, s, r: (s, i, r))
    else:
        a_spec = pl.BlockSpec((tm, tk), lambda i, j, s, r: (i, s * nks + r))
    b_spec = pl.BlockSpec((None, None, to, tk), lambda i, j, s, r: (l, s, j, r))
    o_spec = pl.BlockSpec((tm, to), lambda i, j, s, r: (i, j))
    return _mm_call(name, g, w, [], (((1,), (1,)), ((), ())), grid, a_spec, b_spec, [], o_spec, SDS((m, k), F32), (tm, to), 2,
                    after=after)


def _mm_tn(name, a, g, s_n):
    t_len, k = a.shape
    g3d = g.ndim == 3
    ns = g.shape[2] if g3d else g.shape[1] // s_n
    tt = _tile(t_len, (1024, 512))
    to = _tile(k, (512, 256, 128))
    tn = ns
    nj = ns // tn
    grid = (s_n, k // to, nj, t_len // tt)
    a_spec = pl.BlockSpec((tt, to), lambda s, i, j, t: (t, i))
    if g3d:
        b_spec = pl.BlockSpec((None, tt, tn), lambda s, i, j, t: (s, t, j))
    else:
        b_spec = pl.BlockSpec((tt, tn), lambda s, i, j, t: (t, s * nj + j))
    o_spec = pl.BlockSpec((None, to, tn), lambda s, i, j, t: (s, i, j))
    return _mm_call(name, a, g, [], (((0,), (0,)), ((), ())), grid, a_spec, b_spec, [], o_spec,
                    SDS((s_n, k, ns), F32), (to, tn), 1)


def _rms_fwd(name, x, g):
    m, d = x.shape
    tm = _tile(m, (256,))

    def body(x_ref, g_ref, o_ref):
        o_ref[...] = _rms(x_ref[...], g_ref[...]).astype(BF16)

    row = pl.BlockSpec((tm, d), lambda i: (i, 0))
    return pl.pallas_call(body, grid=(m // tm,), in_specs=[row, pl.BlockSpec((1, d), lambda i: (0, 0))], out_specs=row,
                          out_shape=SDS((m, d), BF16), name=name)(x, g)


def _rms_bwd(name, x, g, dhs, add=None):
    m, d = x.shape
    tm = _tile(m, (256,))
    nh = len(dhs)

    def body(*refs):
        x_ref, g_ref = refs[0], refs[1]
        dh = refs[2][...]
        for r in refs[3:2 + nh]:
            dh = dh + r[...]
        dx_ref, dg_ref = refs[-2], refs[-1]
        _, vjp = jax.vjp(_rms, x_ref[...], g_ref[...])
        dx, dg = vjp(dh)
        if add is not None:
            dx = dx + refs[2 + nh][...]
        dx_ref[...] = dx

        @pl.when(pl.program_id(0) == 0)
        def _():
            dg_ref[...] = jnp.zeros_like(dg_ref)

        dg_ref[...] += dg

    row = pl.BlockSpec((tm, d), lambda i: (i, 0))
    vec = pl.BlockSpec((1, d), lambda i: (0, 0))
    ins = [x, g, *dhs] + ([add] if add is not None else [])
    return pl.pallas_call(body, grid=(m // tm,), in_specs=[row, vec] + [row] * (len(ins) - 2), out_specs=[row, vec],
                          out_shape=[SDS((m, d), F32), SDS((1, d), F32)], name=name)(*ins)


def _swiglu_fwd(name, ag3):
    _, t_len, w = ag3.shape
    tm = _tile(t_len, (512, 256))

    def body(a_ref, g_ref, o_ref):
        o_ref[...] = _swiglu_f(a_ref[...], g_ref[...]).astype(BF16)

    return pl.pallas_call(
        body, grid=(t_len // tm, 2),
        in_specs=[pl.BlockSpec((None, tm, w), lambda i, j: (j, i, 0)), pl.BlockSpec((None, tm, w), lambda i, j: (j + 2, i, 0))],
        out_specs=pl.BlockSpec((tm, w), lambda i, j: (i, j)), out_shape=SDS((t_len, 2 * w), BF16), name=name)(ag3, ag3)


def _swiglu_bwd(name, ag3, du):
    _, t_len, w = ag3.shape
    tm = _tile(t_len, (512, 256))

    def body(a_ref, g_ref, du_ref, o_ref):
        _, vjp = jax.vjp(_swiglu_f, a_ref[...], g_ref[...])
        da, dg = vjp(du_ref[...])
        o_ref[0] = da.astype(BF16)
        o_ref[1] = dg.astype(BF16)

    out = pl.pallas_call(
        body, grid=(t_len // tm, 2),
        in_specs=[pl.BlockSpec((None, tm, w), lambda i, j: (j, i, 0)), pl.BlockSpec((None, tm, w), lambda i, j: (j + 2, i, 0)),
                  pl.BlockSpec((tm, w), lambda i, j: (i, j))],
        out_specs=pl.BlockSpec((2, None, tm, w), lambda i, j: (0, j, i, 0)), out_shape=SDS((2, 2, t_len, w), BF16), name=name)(ag3, ag3, du)
    return out.reshape(ag3.shape)


def _gate_fwd(name, gp3, br, wb, l):
    _, t_len, d = gp3.shape
    tm = _tile(t_len, (512, 256))

    def body(gp_ref, br_ref, wb_ref, o_ref):
        o_ref[...] = _gate_f([gp_ref[n] for n in range(NH)], [br_ref[n] for n in range(NH)],
                             [wb_ref[n] for n in range(NH)]).astype(BF16)

    return pl.pallas_call(
        body, grid=(t_len // tm, d // BW),
        in_specs=[pl.BlockSpec((NH, tm, BW), lambda i, s: (0, i, s)), pl.BlockSpec((NH, tm, BW), lambda i, s: (0, i, 0)),
                  pl.BlockSpec((None, None, NH, BW, BW), lambda i, s: (l, s, 0, 0, 0))],
        out_specs=pl.BlockSpec((tm, BW), lambda i, s: (i, s)), out_shape=SDS((t_len, d), BF16), name=name)(gp3, br, wb)


def _gate_bwd(name, gp3, br, wb, l, dmerged):
    _, t_len, d = gp3.shape
    tm = _tile(t_len, (512, 256))
    ns = d // BW

    def body(gp_ref, br_ref, wb_ref, dm_ref, dgp_ref, dbr_ref, dwb_ref):
        i, s = pl.program_id(0), pl.program_id(1)
        gp = [gp_ref[n] for n in range(NH)]
        brv = [br_ref[n].astype(F32) for n in range(NH)]
        wbv = [wb_ref[n].astype(F32) for n in range(NH)]
        _, vjp = jax.vjp(_gate_f, gp, brv, wbv)
        dgp, dbr, dwb = vjp(dm_ref[...])

        @pl.when(s == 0)
        def _():
            dbr_ref[...] = jnp.zeros_like(dbr_ref)

        @pl.when((i == 0) & (s == 0))
        def _():
            dwb_ref[...] = jnp.zeros_like(dwb_ref)

        for n in range(NH):
            dgp_ref[n] = dgp[n].astype(BF16)
            dbr_ref[n] += dbr[n]
            dwb_ref[s, n] += dwb[n]

    return pl.pallas_call(
        body, grid=(t_len // tm, ns),
        in_specs=[pl.BlockSpec((NH, tm, BW), lambda i, s: (0, i, s)), pl.BlockSpec((NH, tm, BW), lambda i, s: (0, i, 0)),
                  pl.BlockSpec((None, None, NH, BW, BW), lambda i, s: (l, s, 0, 0, 0)), pl.BlockSpec((tm, BW), lambda i, s: (i, s))],
        out_specs=[pl.BlockSpec((NH, tm, BW), lambda i, s: (0, i, s)), pl.BlockSpec((NH, tm, BW), lambda i, s: (0, i, 0)),
                   pl.BlockSpec((ns, NH, BW, BW), lambda i, s: (0, 0, 0, 0))],
        out_shape=[SDS(gp3.shape, BF16), SDS((NH, t_len, BW), F32), SDS((ns, NH, BW, BW), F32)], name=name,
        compiler_params=_cp(VMEM_MM))(gp3, br, wb, dmerged)


def _loss_kernel(name, y, tgt):
    m, d = y.shape
    tm = _tile(m, (256,))

    def body(y_ref, t_ref, l_ref, dy_ref):
        err = y_ref[...] - t_ref[...]
        dy_ref[...] = err * (1.0 / d)

        @pl.when(pl.program_id(0) == 0)
        def _():
            l_ref[...] = jnp.zeros_like(l_ref)

        l_ref[...] += 0.5 * jnp.sum(jnp.mean(err * err, axis=-1, keepdims=True), axis=0, keepdims=True)

    row = pl.BlockSpec((tm, d), lambda i: (i, 0))
    return pl.pallas_call(body, grid=(m // tm,), in_specs=[row, row], out_specs=[pl.BlockSpec((1, 1), lambda i: (0, 0)), row],
                          out_shape=[SDS((1, 1), F32), SDS((m, d), F32)], name=name)(y, tgt)


def _zero_first(refs):
    @pl.when(pl.program_id(0) == 0)
    def _():
        for r in refs:
            r[...] = jnp.zeros_like(r)


def _grp(rows, j):
    return pl.BlockSpec((rows, BW), lambda a, j=j: (a, j))


_VEC = pl.BlockSpec((1, BW), lambda a: (0, 0))


def _ret_fwd(tag, proj, cos, sin, decf, decb, gn):
    t_len = proj.shape[0]
    n = t_len // CH
    tab = pl.BlockSpec((CH, BW), lambda a: (a, 0))
    st = pl.BlockSpec((None, BW, BW), lambda a: (a, 0, 0))

    def k1(k_ref, v_ref, cos_ref, sin_ref, df_ref, db_ref, kvf_ref, kvb_ref):
        kvf_ref[...], kvb_ref[...] = _ret_kv_f(k_ref[...], v_ref[...], cos_ref[...], sin_ref[...], df_ref[...], db_ref[...])

    kvf, kvb = pl.pallas_call(
        k1, grid=(n,), in_specs=[_grp(CH, 1), _grp(CH, 2), tab, tab, _VEC, _VEC], out_specs=[st, st],
        out_shape=[SDS((n, BW, BW), F32)] * 2, name=f"ret_kv_{tag}")(proj, proj, cos, sin, decf, decb)

    def k2(kvf_ref, kvb_ref, df_ref, db_ref, sf_ref, sb_ref):
        sf, sb = _ret_scan_f([kvf_ref[a] for a in range(n)], [kvb_ref[a] for a in range(n)], df_ref[...], db_ref[...])
        for a in range(n):
            sf_ref[a] = sf[a]
            sb_ref[a] = sb[a]

    stf, stb = pl.pallas_call(k2, out_shape=[SDS((n, BW, BW), F32)] * 2, name=f"ret_scan_{tag}",
                              compiler_params=_cp(VMEM_BIG))(kvf, kvb, decf, decb)

    def k3(q_ref, k_ref, v_ref, g_ref, cos_ref, sin_ref, sf_ref, sb_ref, df_ref, db_ref, gn_ref, o_ref):
        o_ref[...] = _ret_out_f(q_ref[...], k_ref[...], v_ref[...], g_ref[...], cos_ref[...], sin_ref[...], sf_ref[...],
                                sb_ref[...], df_ref[...], db_ref[...], gn_ref[...]).astype(BF16)

    ret = pl.pallas_call(
        k3, grid=(n,), in_specs=[_grp(CH, 0), _grp(CH, 1), _grp(CH, 2), _grp(CH, 3), tab, tab, st, st, _VEC, _VEC, _VEC],
        out_specs=tab, out_shape=SDS((t_len, BW), BF16), name=f"ret_out_{tag}")(proj, proj, proj, proj, cos, sin, stf, stb, decf, decb, gn)
    return ret, (kvf, kvb, stf, stb)


def _ret_bwd(tag, proj, cos, sin, decf, decb, gn, saved, dret):
    kvf, kvb, stf, stb = saved
    t_len = proj.shape[0]
    n = t_len // CH
    tab = pl.BlockSpec((CH, BW), lambda a: (a, 0))
    st = pl.BlockSpec((None, BW, BW), lambda a: (a, 0, 0))

    def k3(q_ref, k_ref, v_ref, g_ref, cos_ref, sin_ref, sf_ref, sb_ref, df_ref, db_ref, gn_ref, do_ref,
           dq_ref, dk_ref, dv_ref, dg_ref, dsf_ref, dsb_ref, ddf_ref, ddb_ref, dgn_ref):
        cos, sin = cos_ref[...], sin_ref[...]
        f = lambda q, k, v, g, sf, sb, df, db, gnv: _ret_out_f(q, k, v, g, cos, sin, sf, sb, df, db, gnv)
        _, vjp = jax.vjp(f, q_ref[...], k_ref[...], v_ref[...], g_ref[...], sf_ref[...], sb_ref[...], df_ref[...], db_ref[...], gn_ref[...])
        dq, dk, dv, dg, dsf, dsb, ddf, ddb, dgn = vjp(do_ref[...])
        dq_ref[...] = dq.astype(BF16)
        dk_ref[...] = dk
        dv_ref[...] = dv
        dg_ref[...] = dg.astype(BF16)
        dsf_ref[...] = dsf
        dsb_ref[...] = dsb
        _zero_first([ddf_ref, ddb_ref, dgn_ref])
        ddf_ref[...] += ddf
        ddb_ref[...] += ddb
        dgn_ref[...] += dgn

    dq, dk3, dv3, dg, dstf, dstb, ddf3, ddb3, dgn = pl.pallas_call(
        k3, grid=(n,),
        in_specs=[_grp(CH, 0), _grp(CH, 1), _grp(CH, 2), _grp(CH, 3), tab, tab, st, st, _VEC, _VEC, _VEC, tab],
        out_specs=[tab, tab, tab, tab, st, st, _VEC, _VEC, _VEC],
        out_shape=[SDS((t_len, BW), BF16), SDS((t_len, BW), F32), SDS((t_len, BW), F32), SDS((t_len, BW), BF16),
                   SDS((n, BW, BW), F32), SDS((n, BW, BW), F32), SDS((1, BW), F32), SDS((1, BW), F32), SDS((1, BW), F32)],
        name=f"ret_out_bwd_{tag}")(proj, proj, proj, proj, cos, sin, stf, stb, decf, decb, gn, dret)

    def k2(kvf_ref, kvb_ref, df_ref, db_ref, dsf_ref, dsb_ref, dkvf_ref, dkvb_ref, ddf_ref, ddb_ref):
        _, vjp = jax.vjp(_ret_scan_f, [kvf_ref[a] for a in range(n)], [kvb_ref[a] for a in range(n)], df_ref[...], db_ref[...])
        dkvf, dkvb, ddf_ref[...], ddb_ref[...] = vjp(([dsf_ref[a] for a in range(n)], [dsb_ref[a] for a in range(n)]))
        for a in range(n):
            dkvf_ref[a] = dkvf[a]
            dkvb_ref[a] = dkvb[a]

    dkvf, dkvb, ddf2, ddb2 = pl.pallas_call(
        k2, out_shape=[SDS((n, BW, BW), F32), SDS((n, BW, BW), F32), SDS((1, BW), F32), SDS((1, BW), F32)],
        name=f"ret_scan_bwd_{tag}", compiler_params=_cp(VMEM_BIG))(kvf, kvb, decf, decb, dstf, dstb)

    def k1(k_ref, v_ref, cos_ref, sin_ref, df_ref, db_ref, dkvf_ref, dkvb_ref, dk3_ref, dv3_ref, dk_ref, dv_ref, ddf_ref, ddb_ref):
        cos, sin = cos_ref[...], sin_ref[...]
        f = lambda k, v, df, db: _ret_kv_f(k, v, cos, sin, df, db)
        _, vjp = jax.vjp(f, k_ref[...], v_ref[...], df_ref[...], db_ref[...])
        dk, dv, ddf, ddb = vjp((dkvf_ref[...], dkvb_ref[...]))
        dk_ref[...] = (dk + dk3_ref[...]).astype(BF16)
        dv_ref[...] = (dv + dv3_ref[...]).astype(BF16)
        _zero_first([ddf_ref, ddb_ref])
        ddf_ref[...] += ddf
        ddb_ref[...] += ddb

    dk, dv, ddf1, ddb1 = pl.pallas_call(
        k1, grid=(n,), in_specs=[_grp(CH, 1), _grp(CH, 2), tab, tab, _VEC, _VEC, st, st, tab, tab],
        out_specs=[tab, tab, _VEC, _VEC],
        out_shape=[SDS((t_len, BW), BF16), SDS((t_len, BW), BF16), SDS((1, BW), F32), SDS((1, BW), F32)],
        name=f"ret_kv_bwd_{tag}")(proj, proj, cos, sin, decf, decb, dkvf, dkvb, dk3, dv3)
    return dq, dk, dv, dg, ddf1 + ddf2 + ddf3, ddb1 + ddb2 + ddb3, dgn


def _pool_fwd(tag, pvp, wbd, scale, t_len):
    def body(p_ref, w_ref, s_ref, o_ref):
        o_ref[...] = _pool_f(p_ref[...], w_ref[...], s_ref[...], t_len).astype(BF16)

    return pl.pallas_call(body, out_shape=SDS(pvp.shape, BF16), name=f"pool_{tag}", compiler_params=_cp(VMEM_BIG))(pvp, wbd, scale)


def _pool_bwd(tag, pvp, wbd, scale, t_len, dpool_p):
    def body(p_ref, w_ref, s_ref, do_ref, dp_ref, dw_ref, ds_ref):
        f = lambda p, w, s: _pool_f(p, w, s, t_len)
        _, vjp = jax.vjp(f, p_ref[...], w_ref[...], s_ref[...])
        dp, dw, ds = vjp(do_ref[...])
        dp_ref[...] = dp.astype(BF16)
        dw_ref[...] = dw
        ds_ref[...] = ds

    return pl.pallas_call(body, out_shape=[SDS(pvp.shape, BF16), SDS((BW, BW), F32), SDS((1, BW), F32)],
                          name=f"pool_bwd_{tag}", compiler_params=_cp(VMEM_BIG))(pvp, wbd, scale, dpool_p)


def _na_window(r, rows):
    r0 = jnp.clip(r - NA_ROWS // 2, 0, rows - NA_ROWS)
    return r0, r - r0


NA_PAIRS = 2 * NA_ROWS - 2


def _na_bias_pieces(b_ref, pat):
    return [[b_ref[h * NA_PAIRS + 2 * k - pat + NA_ROWS - 1] for k in range(NA_ROWS // 2)] for h in range(NH)]


def _na_keys(tag, proj, gk):
    t_len = proj.shape[0]
    tm = _tile(t_len, (256,))

    def body(k_ref, v_ref, gk_ref, kn_ref, vb_ref):
        kn_ref[...] = _head_rms(k_ref[...], gk_ref[...]).astype(BF16)
        vb_ref[...] = v_ref[...].astype(BF16)

    row = pl.BlockSpec((tm, BW), lambda a: (a, 0))
    return pl.pallas_call(body, grid=(t_len // tm,), in_specs=[_grp(tm, 6), _grp(tm, 7), _VEC], out_specs=[row, row],
                          out_shape=[SDS((t_len, BW), BF16)] * 2, name=f"na_keys_{tag}")(proj, proj, gk)


def _na_keys_bwd(tag, proj, gk, dkn, dv):
    t_len = proj.shape[0]
    tm = _tile(t_len, (256,))

    def body(k_ref, gk_ref, dkn_ref, dv_ref, dk_ref, dvb_ref, dgk_ref):
        _, vjp = jax.vjp(_head_rms, k_ref[...], gk_ref[...])
        dk, dgk = vjp(dkn_ref[...])
        dk_ref[...] = dk.astype(BF16)
        dvb_ref[...] = dv_ref[...].astype(BF16)
        _zero_first([dgk_ref])
        dgk_ref[...] += dgk

    row = pl.BlockSpec((tm, BW), lambda a: (a, 0))
    return pl.pallas_call(body, grid=(t_len // tm,), in_specs=[_grp(tm, 6), _VEC, row, row], out_specs=[row, row, _VEC],
                          out_shape=[SDS((t_len, BW), BF16), SDS((t_len, BW), BF16), SDS((1, BW), F32)],
                          name=f"na_keys_bwd_{tag}")(proj, gk, dkn, dv)


def _na_fwd(tag, proj, kn, vb, bias_tab, gq):
    t_len = proj.shape[0]
    rows = t_len // GRID_W
    win = NA_ROWS * GRID_W

    def body(q_ref, k_ref, v_ref, b_ref, gq_ref, o_ref):
        r0, pat = _na_window(pl.program_id(0), rows)
        start = pl.multiple_of(r0 * GRID_W, GRID_W)
        o_ref[...] = _na_f(q_ref[...], k_ref[pl.ds(start, win), :], v_ref[pl.ds(start, win), :], _na_bias_pieces(b_ref, pat),
                           gq_ref[...]).astype(BF16)

    whole = pl.BlockSpec((t_len, BW), lambda a: (0, 0))
    return pl.pallas_call(
        body, grid=(rows,),
        in_specs=[_grp(GRID_W, 5), whole, whole, pl.BlockSpec(bias_tab.shape, lambda a: (0, 0, 0)), _VEC],
        out_specs=pl.BlockSpec((GRID_W, BW), lambda a: (a, 0)), out_shape=SDS((t_len, BW), BF16),
        name=f"na_{tag}", compiler_params=_cp(VMEM_MM))(proj, kn, vb, bias_tab, gq)


def _na_bwd(tag, proj, kn, vb, bias_tab, gq, dna):
    t_len = proj.shape[0]
    rows = t_len // GRID_W
    win = NA_ROWS * GRID_W

    def body(q_ref, k_ref, v_ref, b_ref, gq_ref, do_ref, dq_ref, dk_ref, dv_ref, db_ref, dgq_ref):
        r0, pat = _na_window(pl.program_id(0), rows)
        start = pl.multiple_of(r0 * GRID_W, GRID_W)
        _zero_first([dk_ref, dv_ref, db_ref, dgq_ref])
        _, vjp = jax.vjp(_na_f, q_ref[...], k_ref[pl.ds(start, win), :].astype(F32), v_ref[pl.ds(start, win), :].astype(F32),
                         _na_bias_pieces(b_ref, pat), gq_ref[...])
        dq, dk, dv, db, dgq = vjp(do_ref[...])
        dq_ref[...] = dq.astype(BF16)
        dk_ref[pl.ds(start, win), :] += dk
        dv_ref[pl.ds(start, win), :] += dv
        for h in range(NH):
            for k in range(NA_ROWS // 2):
                db_ref[h * NA_PAIRS + 2 * k - pat + NA_ROWS - 1] += db[h][k]
        dgq_ref[...] += dgq

    whole = pl.BlockSpec((t_len, BW), lambda a: (0, 0))
    tabspec = pl.BlockSpec(bias_tab.shape, lambda a: (0, 0, 0))
    row = pl.BlockSpec((GRID_W, BW), lambda a: (a, 0))
    return pl.pallas_call(
        body, grid=(rows,), in_specs=[_grp(GRID_W, 5), whole, whole, tabspec, _VEC, row],
        out_specs=[row, whole, whole, tabspec, _VEC],
        out_shape=[SDS((t_len, BW), BF16), SDS((t_len, BW), F32), SDS((t_len, BW), F32), SDS(bias_tab.shape, F32), SDS((1, BW), F32)],
        name=f"na_bwd_{tag}", compiler_params=_cp(VMEM_BIG))(proj, kn, vb, bias_tab, gq, dna)


def _mem_fwd(tag, proj, memkv, gq, gk):
    t_len = proj.shape[0]
    n_mem = memkv.shape[0]
    tm = _tile(t_len, (256,))

    def body(q_ref, mk_ref, mv_ref, gq_ref, gk_ref, o_ref):
        o_ref[...] = _mem_f(q_ref[...], mk_ref[...], mv_ref[...], gq_ref[...], gk_ref[...]).astype(BF16)

    mspec = lambda j: pl.BlockSpec((n_mem, BW), lambda a, j=j: (0, j))
    return pl.pallas_call(
        body, grid=(t_len // tm,), in_specs=[_grp(tm, 8), mspec(0), mspec(1), _VEC, _VEC],
        out_specs=pl.BlockSpec((tm, BW), lambda a: (a, 0)), out_shape=SDS((t_len, BW), BF16), name=f"mem_{tag}")(proj, memkv, memkv, gq, gk)


def _mem_bwd(tag, proj, memkv, gq, gk, dmo):
    t_len = proj.shape[0]
    n_mem = memkv.shape[0]
    tm = _tile(t_len, (256,))

    def body(q_ref, mk_ref, mv_ref, gq_ref, gk_ref, do_ref, dq_ref, dmk_ref, dmv_ref, dgq_ref, dgk_ref):
        _zero_first([dmk_ref, dmv_ref, dgq_ref, dgk_ref])
        _, vjp = jax.vjp(_mem_f, q_ref[...], mk_ref[...], mv_ref[...], gq_ref[...], gk_ref[...])
        dq, dmk, dmv, dgq, dgk = vjp(do_ref[...])
        dq_ref[...] = dq.astype(BF16)
        dmk_ref[...] += dmk
        dmv_ref[...] += dmv
        dgq_ref[...] += dgq
        dgk_ref[...] += dgk

    mspec = lambda j: pl.BlockSpec((n_mem, BW), lambda a, j=j: (0, j))
    mout = pl.BlockSpec((n_mem, BW), lambda a: (0, 0))
    row = pl.BlockSpec((tm, BW), lambda a: (a, 0))
    dq, dmk, dmv, dgq, dgk = pl.pallas_call(
        body, grid=(t_len // tm,), in_specs=[_grp(tm, 8), mspec(0), mspec(1), _VEC, _VEC, row],
        out_specs=[row, mout, mout, _VEC, _VEC],
        out_shape=[SDS((t_len, BW), BF16), SDS((n_mem, BW), F32), SDS((n_mem, BW), F32), SDS((1, BW), F32), SDS((1, BW), F32)],
        name=f"mem_bwd_{tag}")(proj, memkv, memkv, gq, gk, dmo)
    return dq, jnp.concatenate([dmk, dmv], axis=1), dgq, dgk


NA_NJ = 2 * NA_COLS - 1


def _na_onehot():
    q = np.arange(GRID_W)[:, None]
    kc = np.arange(GRID_W)[None, :]
    qwin = np.clip(q - NA_COLS // 2, 0, GRID_W - NA_COLS)
    mask = (kc >= qwin) & (kc < qwin + NA_COLS)
    col = np.clip(kc - q, -(NA_COLS - 1), NA_COLS - 1) + NA_COLS - 1
    onehot = np.zeros((LANES, GRID_W, 2, GRID_W), np.float32)
    qq, kk = np.nonzero(mask)
    for half in range(2):
        onehot[half * NA_NJ + col[qq, kk], qq, half, kk] = 1.0
    valid = np.broadcast_to(mask[:, None, :], (GRID_W, 2, GRID_W)).astype(np.float32)
    return onehot.reshape(LANES, -1), valid.reshape(1, -1)


def _na_pair_rows(rpb):
    n_layers = rpb.shape[0]
    pair = jnp.concatenate([rpb[:, :, :-1], rpb[:, :, 1:]], axis=-1).reshape(n_layers, NH * NA_PAIRS, 2 * NA_NJ)
    return jnp.pad(pair, ((0, 0), (0, 0), (0, LANES - 2 * NA_NJ)))


def _na_bias_table(rpb):
    n_layers = rpb.shape[0]
    onehot, valid = _na_onehot()
    width = onehot.shape[1]

    def body(r_ref, oh_ref, ok_ref, o_ref):
        t = jnp.dot(r_ref[...], oh_ref[...], precision=lax.Precision.HIGHEST, preferred_element_type=F32)
        o_ref[...] = jnp.where(ok_ref[...] > 0, t, NEG_INF)

    out = pl.pallas_call(
        body, grid=(n_layers,),
        in_specs=[pl.BlockSpec((None, NH * NA_PAIRS, LANES), lambda l: (l, 0, 0)), pl.BlockSpec((LANES, width), lambda l: (0, 0)),
                  pl.BlockSpec((1, width), lambda l: (0, 0))],
        out_specs=pl.BlockSpec((None, NH * NA_PAIRS, width), lambda l: (l, 0, 0)),
        out_shape=SDS((n_layers, NH * NA_PAIRS, width), F32), name="na_bias_table")(
            _na_pair_rows(rpb), jnp.asarray(onehot), jnp.asarray(valid))
    return out.reshape(n_layers, NH * NA_PAIRS, GRID_W, 2 * GRID_W)


def _na_bias_grad(dtab):
    n_layers = dtab.shape[0]
    onehot, _ = _na_onehot()
    width = onehot.shape[1]

    def body(x_ref, oh_ref, o_ref):
        o_ref[...] = jnp.dot(x_ref[...], oh_ref[...], precision=lax.Precision.HIGHEST, preferred_element_type=F32)

    out = pl.pallas_call(
        body, grid=(n_layers,),
        in_specs=[pl.BlockSpec((None, NH * NA_PAIRS, width), lambda l: (l, 0, 0)), pl.BlockSpec((width, LANES), lambda l: (0, 0))],
        out_specs=pl.BlockSpec((None, NH * NA_PAIRS, LANES), lambda l: (l, 0, 0)),
        out_shape=SDS((n_layers, NH * NA_PAIRS, LANES), F32), name="na_bias_grad")(
            dtab.reshape(n_layers, NH * NA_PAIRS, width), jnp.asarray(onehot.T.copy()))
    out = out.reshape(n_layers, NH, NA_PAIRS, LANES)
    zero = jnp.zeros((n_layers, NH, 1, NA_NJ), F32)
    return (jnp.concatenate([out[..., :NA_NJ], zero], axis=2) + jnp.concatenate([zero, out[..., NA_NJ:2 * NA_NJ]], axis=2))


def _place():
    x, y, c = lax.axis_index("x"), lax.axis_index("y"), lax.axis_index("c")
    chips = [(1 - x, y), (x, 1 - y), (1 - x, 1 - y)]
    return x, y, c, chips


def _remote(src, dst, ssem, rsem, dev):
    return pltpu.make_async_remote_copy(src_ref=src, dst_ref=dst, send_sem=ssem, recv_sem=rsem, device_id=dev, device_id_type=MESH)


HBM = pl.BlockSpec(memory_space=pltpu.HBM)
SEM = pl.BlockSpec(memory_space=pltpu.SEMAPHORE)
DATAFLOW = pltpu.SideEffectType.DATAFLOW_SIDE_EFFECTING


def _split_copies(name, srcs, lands, n_copies, plan):
    n_s, n_l = len(srcs), len(lands)
    hbm = lambda a: pltpu.HBM(a.shape, a.dtype)

    def start_body(*refs):
        ins = refs[:n_s + n_l]
        outs = refs[n_s + n_l:]
        sems, token = outs[:2 * n_copies], outs[-1]
        for k, (src, dst, dev) in enumerate(plan(ins[:n_s], ins[n_s:])):
            _remote(src, dst, sems[k], sems[n_copies + k], dev).start()
        token[...] = jnp.zeros_like(token)

    outs = pl.pallas_call(
        start_body, name=f"{name}_start",
        out_shape=(pltpu.SemaphoreType.DMA(()),) * (2 * n_copies) + tuple(hbm(a) for a in (*srcs, *lands)) + (SDS((8, LANES), F32),),
        in_specs=(HBM,) * (n_s + n_l), out_specs=(SEM,) * (2 * n_copies) + (HBM,) * (n_s + n_l) + (pl.BlockSpec(memory_space=pltpu.VMEM),),
        input_output_aliases={i: 2 * n_copies + i for i in range(n_s + n_l)},
        compiler_params=pltpu.CompilerParams(has_side_effects=DATAFLOW),
    )(*[pltpu.with_memory_space_constraint(a, pltpu.HBM) for a in (*srcs, *lands)])
    sems, thru, token = outs[:2 * n_copies], outs[2 * n_copies:-1], outs[-1]

    def wait(after_wait):
        def wait_body(*refs):
            ins = refs[:n_s + n_l]
            sem_refs = refs[n_s + n_l:n_s + n_l + 2 * n_copies]
            for k, (src, dst, dev) in enumerate(plan(ins[:n_s], ins[n_s:])):
                cp = _remote(src, dst, sem_refs[k], sem_refs[n_copies + k], dev)
                cp.wait_send()
                cp.wait_recv()

        res = pl.pallas_call(
            wait_body, name=f"{name}_wait", out_shape=tuple(hbm(a) for a in (*srcs, *lands)),
            in_specs=(HBM,) * (n_s + n_l) + (SEM,) * (2 * n_copies) + (ANY,), out_specs=(HBM,) * (n_s + n_l),
            input_output_aliases={i: i for i in range(n_s + n_l)},
            compiler_params=pltpu.CompilerParams(has_side_effects=DATAFLOW),
        )(*thru, *sems, after_wait)
        return list(res[n_s:])

    return token, wait


def _gather_plan(srcs, lands):
    x, y, c, chips = _place()
    me = 2 * x + y
    out = []
    for src, land in zip(srcs, lands):
        h = src.shape[0] // 2
        out.append((src, land.at[me], (x, y, 1 - c)))
        for px, py in chips:
            out.append((src.at[pl.ds(c * h, h)], land.at[me, pl.ds(c * h, h)], (px, py, c)))
    return out


def _scatter_plan(srcs, lands):
    x, y, c, chips = _place()
    out = []
    for src, land in zip(srcs, lands):
        for r, (px, py) in enumerate(chips):
            out.append((src.at[2 * px + py], land.at[r], (px, py, c)))
    return out


def _forward_halves(name, lands):
    n = len(lands)

    def body(*refs):
        ins, outs = refs[:n], refs[n:2 * n]
        ssem, rsem = refs[2 * n:]
        x, y, c, chips = _place()
        sib = (x, y, 1 - c)

        def copy(i, r, half):
            h = ins[i].shape[1] // 2
            px, py = chips[r]
            rows = pl.ds(half * h, h)
            return _remote(ins[i].at[2 * px + py, rows], outs[i].at[2 * px + py, rows], ssem.at[i, r], rsem.at[i, r], sib)

        for i in range(n):
            for r in range(3):
                copy(i, r, c).start()
        for i in range(n):
            for r in range(3):
                copy(i, r, 1 - c).wait_recv()
                copy(i, r, c).wait_send()

    return pl.pallas_call(
        body, in_specs=[ANY] * n, out_specs=[ANY] * n, out_shape=[SDS(a.shape, a.dtype) for a in lands],
        input_output_aliases={i: i for i in range(n)},
        scratch_shapes=[pltpu.SemaphoreType.DMA((n, 3)), pltpu.SemaphoreType.DMA((n, 3))], name=name)(*lands)


def _pair_exchange(name, gs):
    n = len(gs)

    def body(*refs):
        ins, outs = refs[:n], refs[n:2 * n]
        ssem, rsem = refs[2 * n:]
        x, y, c, _ = _place()
        cps = []
        for i in range(n):
            h = ins[i].shape[1] // 2
            cps.append(_remote(ins[i].at[:, pl.ds((1 - c) * h, h)], outs[i], ssem.at[i], rsem.at[i], (x, y, 1 - c)))
            cps[-1].start()
        for cp in cps:
            cp.wait()

    return pl.pallas_call(
        body, in_specs=[ANY] * n, out_specs=[ANY] * n,
        out_shape=[SDS((g.shape[0], g.shape[1] // 2, g.shape[2]), g.dtype) for g in gs],
        scratch_shapes=[pltpu.SemaphoreType.DMA((n,)), pltpu.SemaphoreType.DMA((n,))], name=name)(*gs)


def _half_exchange(name, fs):
    n = len(fs)

    def body(*refs):
        ins, outs = refs[:n], refs[n:2 * n]
        ssem, rsem = refs[2 * n:]
        x, y, c, _ = _place()

        def copy(i, half):
            h = ins[i].shape[0] // 2
            return _remote(ins[i].at[pl.ds(half * h, h)], outs[i].at[pl.ds(half * h, h)], ssem.at[i], rsem.at[i], (x, y, 1 - c))

        for i in range(n):
            copy(i, c).start()
        for i in range(n):
            copy(i, 1 - c).wait_recv()
            copy(i, c).wait_send()

    return pl.pallas_call(
        body, in_specs=[ANY] * n, out_specs=[ANY] * n, out_shape=[SDS(f.shape, f.dtype) for f in fs],
        input_output_aliases={i: i for i in range(n)},
        scratch_shapes=[pltpu.SemaphoreType.DMA((n,)), pltpu.SemaphoreType.DMA((n,))], name=name)(*fs)


def _gather_small(v):
    def body(v_ref, o_ref, lsem, ssem, rsem):
        x, y, c, _ = _place()
        me = 4 * x + 2 * y + c
        flips = [(fx, fy, fc) for fx in (0, 1) for fy in (0, 1) for fc in (0, 1)][1:]
        peer = lambda f: (1 - x if f[0] else x, 1 - y if f[1] else y, 1 - c if f[2] else c)
        loc = pltpu.make_async_copy(v_ref, o_ref.at[me], lsem)
        loc.start()
        cps = [_remote(v_ref, o_ref.at[me], ssem.at[k], rsem.at[k], peer(f)) for k, f in enumerate(flips)]
        for cp in cps:
            cp.start()
        for k, f in enumerate(flips):
            px, py, pc = peer(f)
            land = o_ref.at[4 * px + 2 * py + pc]
            _remote(land, land, ssem.at[k], rsem.at[k], peer(f)).wait_recv()
        for cp in cps:
            cp.wait_send()
        loc.wait()

    return pl.pallas_call(
        body, in_specs=[ANY], out_specs=ANY, out_shape=SDS((N_DEV,) + v.shape, v.dtype),
        scratch_shapes=[pltpu.SemaphoreType.DMA, pltpu.SemaphoreType.DMA((N_DEV - 1,)), pltpu.SemaphoreType.DMA((N_DEV - 1,))],
        name="gather_small")(v)


def _rows_tile(r):
    return _tile(r, (256, 352, 128, 64))


def _pair_sum(name, g, r1, core, me):
    n_l, s_n, h, c = r1.shape
    tr = _rows_tile(h)
    nb = h // tr

    def body(idx_ref, g_ref, r_ref, pb_ref, own_ref):
        p = g_ref[...] + r_ref[...]
        pb_ref[...] = p.astype(BF16)

        @pl.when(pl.program_id(2) == idx_ref[1])
        def _():
            own_ref[...] = p

    blk = (None, None, tr, c)
    grid_spec = pltpu.PrefetchScalarGridSpec(
        num_scalar_prefetch=1, grid=(n_l, nb, s_n),
        in_specs=[pl.BlockSpec(blk, lambda a, i, s, idx: (a, s, idx[0] * nb + i, 0)), pl.BlockSpec(blk, lambda a, i, s, idx: (a, s, i, 0))],
        out_specs=[pl.BlockSpec(blk, lambda a, i, s, idx: (a, s, i, 0)), pl.BlockSpec((None, tr, c), lambda a, i, s, idx: (a, i, 0))])
    idx = jnp.stack([core, me]).astype(jnp.int32)
    return pl.pallas_call(body, grid_spec=grid_spec, out_shape=[SDS(r1.shape, BF16), SDS((n_l, h, c), F32)], name=name)(idx, g, r1)


def _chip_sum(name, own, r2, core):
    n_l, h, c = own.shape
    tr = _rows_tile(h)
    nb = h // tr

    def body(idx_ref, o_ref, r_ref, f_ref):
        f_ref[...] = ((o_ref[...] + r_ref[0].astype(F32)) + r_ref[1].astype(F32)) + r_ref[2].astype(F32)

    grid_spec = pltpu.PrefetchScalarGridSpec(
        num_scalar_prefetch=1, grid=(n_l, nb),
        in_specs=[pl.BlockSpec((None, tr, c), lambda a, i, idx: (a, i, 0)), pl.BlockSpec((3, None, tr, c), lambda a, i, idx: (0, a, i, 0))],
        out_specs=pl.BlockSpec((None, tr, c), lambda a, i, idx: (a, idx[0] * nb + i, 0)))
    return pl.pallas_call(body, grid_spec=grid_spec, out_shape=SDS((n_l, 2 * h, c), F32), name=name)(
        jnp.reshape(core, (1,)).astype(jnp.int32), own, r2)


def _adamw_math(w, g, m, v):
    m = ADAM_B1 * m + (1.0 - ADAM_B1) * g
    v = ADAM_B2 * v + (1.0 - ADAM_B2) * jnp.square(g)
    m_hat = m / (1.0 - ADAM_B1 ** ADAM_STEP)
    v_hat = v / (1.0 - ADAM_B2 ** ADAM_STEP)
    delta = -ADAM_LR * (m_hat / (jnp.sqrt(v_hat) + ADAM_EPS) + ADAM_WD * w)
    return delta, m, v


def _adamw(name, l, w, g, m, v, bufs):
    _, r, c = w.shape
    tr = _rows_tile(r)

    def body(w_ref, g_ref, m_ref, v_ref, *rest):
        go_ref, d_ref, nm_ref, nv_ref = rest[4:]
        g = g_ref[...]
        go_ref[...] = g
        d_ref[...], nm_ref[...], nv_ref[...] = _adamw_math(w_ref[...], g, m_ref[...], v_ref[...])

    blk = pl.BlockSpec((None, tr, c), lambda i: (l, i, 0))
    return pl.pallas_call(body, grid=(r // tr,), in_specs=[blk, pl.BlockSpec((tr, c), lambda i: (i, 0)), blk, blk] + [ANY] * 4,
                          out_specs=[blk] * 4, out_shape=[SDS(w.shape, F32)] * 4, input_output_aliases={4 + k: k for k in range(4)},
                          name=name)(w, g, m, v, *bufs)


def _small_update(parts, w, m, v):
    def body(p_ref, w_ref, m_ref, v_ref, g_ref, d_ref, nm_ref, nv_ref):
        g = p_ref[0]
        for d in range(1, N_DEV):
            g = g + p_ref[d]
        g_ref[...] = g
        d_ref[...], nm_ref[...], nv_ref[...] = _adamw_math(w_ref[...], g, m_ref[...], v_ref[...])

    return pl.pallas_call(body, out_shape=[SDS(w.shape, F32)] * 4, name="small_update")(parts, w, m, v)


SMALL = ["norm_mix_g", "norm_mem_g", "ret_decay_fwd", "ret_decay_bwd", "ret_norm_g", "pool_w", "pool_scale", "na_q_norm_g",
         "na_k_norm_g", "na_rpb", "mem_q_norm_g", "mem_k_norm_g", "norm_ffn_g"]
BIG = ["w_in", "w_gate", "w_mem_kv", "w_branch", "w_out", "w_ffn_in", "w_ffn_out"]
ORDER = ["norm_mix_g", "norm_mem_g", "w_in", "w_gate", "ret_decay_fwd", "ret_decay_bwd", "ret_norm_g", "pool_w", "pool_scale",
         "na_q_norm_g", "na_k_norm_g", "na_rpb", "mem_q_norm_g", "mem_k_norm_g", "w_mem_kv", "w_branch", "w_out", "norm_ffn_g",
         "w_ffn_in", "w_ffn_out"]


def _pack(arrs):
    rows = []
    for a in arrs:
        f = a.reshape(-1).astype(F32)
        pad = (-f.shape[0]) % (8 * LANES)
        rows.append(jnp.pad(f, (0, pad)).reshape(-1, LANES))
    return jnp.concatenate(rows, axis=0)


def _unpack(p, like):
    out, at = [], 0
    for a in like:
        n = int(np.prod(a.shape))
        rows = -(-n // (8 * LANES)) * 8
        out.append(p[at:at + rows].reshape(-1)[:n].reshape(a.shape))
        at += rows
    return out


def _rope_tables(t_len):
    half = HEAD // 2
    inv = ROPE_THETA ** (-jnp.arange(half, dtype=F32) / half)
    ang = jnp.arange(t_len, dtype=F32)[:, None] * inv[None, :]
    cos, sin = jnp.cos(ang), jnp.sin(ang)
    return jnp.tile(jnp.concatenate([cos, cos], axis=1), (1, NH)), jnp.tile(jnp.concatenate([-sin, sin], axis=1), (1, NH))


def _block_diag(pw):
    out = jnp.zeros((BW, BW), pw.dtype)
    for g in range(NH):
        out = out.at[g * HEAD:(g + 1) * HEAD, g * HEAD:(g + 1) * HEAD].set(pw[g])
    return out


def _fold_heads(v):
    return v.reshape(NH, HEAD).sum(axis=0)


def _fold_lanes(v):
    return v.reshape(NH, HEAD).sum(axis=1)


def kernel(x, mem, norm_mix_g, norm_mem_g, w_in, w_gate, ret_decay_fwd, ret_decay_bwd, ret_norm_g, pool_w, pool_scale, na_q_norm_g, na_k_norm_g, na_rpb, mem_q_norm_g, mem_k_norm_g, w_mem_kv, w_branch, w_out, norm_ffn_g, w_ffn_in, w_ffn_out, loss_target, m_norm_mix_g, m_norm_mem_g, m_w_in, m_w_gate, m_ret_decay_fwd, m_ret_decay_bwd, m_ret_norm_g, m_pool_w, m_pool_scale, m_na_q_norm_g, m_na_k_norm_g, m_na_rpb, m_mem_q_norm_g, m_mem_k_norm_g, m_w_mem_kv, m_w_branch, m_w_out, m_norm_ffn_g, m_w_ffn_in, m_w_ffn_out, v_norm_mix_g, v_norm_mem_g, v_w_in, v_w_gate, v_ret_decay_fwd, v_ret_decay_bwd, v_ret_norm_g, v_pool_w, v_pool_scale, v_na_q_norm_g, v_na_k_norm_g, v_na_rpb, v_mem_q_norm_g, v_mem_k_norm_g, v_w_mem_kv, v_w_branch, v_w_out, v_norm_ffn_g, v_w_ffn_in, v_w_ffn_out):
    args = dict(locals())
    W = {n: args[n] for n in ORDER}
    M = {n: args["m_" + n] for n in ORDER}
    V = {n: args["v_" + n] for n in ORDER}
    n_layers, d_model = norm_mix_g.shape
    assert x.shape[0] == 1
    t_len = x.shape[1]
    xc, yc, cc = lax.axis_index("x"), lax.axis_index("y"), lax.axis_index("c")
    me_chip = 2 * xc + yc

    def as3(a):
        return a.reshape(a.shape[0], -1, a.shape[-1])

    shards = [as3(W[n]).astype(BF16) for n in BIG]
    gather_waits, token = [], jnp.zeros((), F32)
    for l in range(n_layers):
        srcs = [sh[l] for sh in shards]
        lands = [lax.empty((N_CHIPS,) + a.shape, BF16) for a in srcs]
        tok, wait = _split_copies(f"gather_l{l}", srcs, lands, 4 * len(BIG), _gather_plan)
        gather_waits.append(wait)
        token = token + tok[0, 0]

    cos, sin = _rope_tables(t_len)
    btabs = _na_bias_table(na_rpb)
    lane = lambda a: a.reshape(1, -1).astype(F32)
    x2d, mem2d, tgt = x[0], mem[0], loss_target[0]
    norm_mix_g = norm_mix_g + token

    saved = []
    for l in range(n_layers):
        tag = f"l{l}"
        gathered = dict(zip(BIG, _forward_halves(f"gather_pass_{tag}", gather_waits[l](x2d))))
        wi = gathered["w_in"][None]
        wg = gathered["w_gate"][None]
        wmkv = gathered["w_mem_kv"].reshape(1, 1, d_model, 2 * BW)
        wb = gathered["w_branch"].reshape(1, N_CHIPS, NH, BW, BW)
        wo = gathered["w_out"].reshape(1, 1, d_model, d_model)
        wfi = gathered["w_ffn_in"][None]
        wfo = gathered["w_ffn_out"].reshape(1, 1, -1, d_model)
        s = {"x": x2d, "w": (wi, wg, wmkv, wb, wo, wfi, wfo)}
        s["decf"], s["decb"] = lane(jnp.repeat(ret_decay_fwd[l], HEAD)), lane(jnp.repeat(ret_decay_bwd[l], HEAD))
        s["gn"], s["pscale"] = lane(ret_norm_g[l]), lane(pool_scale[l])
        s["wbd"] = _block_diag(pool_w[l])
        s["nagq"], s["nagk"] = lane(jnp.tile(na_q_norm_g[l], NH)), lane(jnp.tile(na_k_norm_g[l], NH))
        s["mgq"], s["mgk"] = lane(jnp.tile(mem_q_norm_g[l], NH)), lane(jnp.tile(mem_k_norm_g[l], NH))
        s["btab"] = btabs[l]
        s["h"] = _rms_fwd(f"rms_mix_{tag}", x2d, lane(norm_mix_g[l]))
        proj3 = _mm_nn(f"mm_in_{tag}", s["h"], wi, 0, out3d=True)
        s["proj"] = proj3.transpose(1, 0, 2).reshape(t_len, -1)
        s["gp3"] = _mm_nn(f"mm_gate_{tag}", s["h"], wg, 0, out3d=True)
        s["hm"] = _rms_fwd(f"rms_mem_{tag}", mem2d, lane(norm_mem_g[l]))
        s["memkv"] = _mm_nn(f"mm_memkv_{tag}", s["hm"], wmkv, 0)
        ret, s["ret_saved"] = _ret_fwd(tag, s["proj"], cos, sin, s["decf"], s["decb"], s["gn"])
        s["pvp"] = jnp.pad(s["proj"][:, 4 * BW:5 * BW], ((POOL_PAD, POOL_PAD), (0, 0)))
        pool = _pool_fwd(tag, s["pvp"], s["wbd"], s["pscale"], t_len)[POOL_PAD:POOL_PAD + t_len]
        s["kn"], s["vb"] = _na_keys(tag, s["proj"], s["nagk"])
        na = _na_fwd(tag, s["proj"], s["kn"], s["vb"], s["btab"], s["nagq"])
        mo = _mem_fwd(tag, s["proj"], s["memkv"], s["mgq"], s["mgk"])
        s["br"] = jnp.stack([ret, pool, na, mo])
        s["merged"] = _gate_fwd(f"gate_{tag}", s["gp3"], s["br"], wb, 0)
        s["x2"] = _mm_nn(f"mm_out_{tag}", s["merged"], wo, 0, add=x2d)
        s["h2"] = _rms_fwd(f"rms_ffn_{tag}", s["x2"], lane(norm_ffn_g[l]))
        s["ag3"] = _mm_nn(f"mm_ffn_in_{tag}", s["h2"], wfi, 0, out3d=True)
        s["u"] = _swiglu_fwd(f"swiglu_{tag}", s["ag3"])
        x2d = _mm_nn(f"mm_ffn_out_{tag}", s["u"], wfo, 0, add=s["x2"])
        saved.append(s)

    loss_part, dx = _loss_kernel("loss", x2d, tgt)

    dbtabs = [None] * n_layers
    gsmall = {n: [None] * n_layers for n in SMALL}
    out_bufs = {n: tuple(lax.empty(as3(W[n]).shape, F32) for _ in range(4)) for n in BIG}

    def finish_layer(l, pending, after):
        sums, wait = pending
        r2 = wait(after)
        halves = [_chip_sum(f"chip_sum_{n}_l{l}", own, r[:, None], cc)[0] for n, (_, own), r in zip(BIG, sums, r2)]
        for n, f in zip(BIG, _half_exchange(f"grad_half_exchange_l{l}", halves)):
            out_bufs[n] = _adamw(f"adamw_{n}_l{l}", l, as3(W[n]), f, as3(M[n]), as3(V[n]), out_bufs[n])

    pending, order_after = None, None
    for l in reversed(range(n_layers)):
        tag = f"l{l}"
        s = saved[l]
        wi, wg, wmkv, wb, wo, wfi, wfo = s["w"]
        du = _mm_nt(f"mm_ffn_out_dx_{tag}", dx, wfo, 0, after=order_after)
        g_wfo = _mm_tn(f"mm_ffn_out_dw_{tag}", s["u"], dx, 1)
        dag3 = _swiglu_bwd(f"swiglu_bwd_{tag}", s["ag3"], du)
        dh2 = _mm_nt(f"mm_ffn_in_dx_{tag}", dag3, wfi, 0)
        g_wfi = _mm_tn(f"mm_ffn_in_dw_{tag}", s["h2"], dag3, N_CHIPS)
        dx2, dg = _rms_bwd(f"rms_ffn_bwd_{tag}", s["x2"], lane(norm_ffn_g[l]), [dh2], add=dx)
        gsmall["norm_ffn_g"][l] = dg[0]
        dmerged = _mm_nt(f"mm_out_dx_{tag}", dx2, wo, 0)
        g_wo = _mm_tn(f"mm_out_dw_{tag}", s["merged"], dx2, 1)
        dgp3, dbr, g_wb = _gate_bwd(f"gate_bwd_{tag}", s["gp3"], s["br"], wb, 0, dmerged)
        g_wg = _mm_tn(f"mm_gate_dw_{tag}", s["h"], dgp3, N_CHIPS)
        dh_a = _mm_nt(f"mm_gate_dx_{tag}", dgp3, wg, 0)
        drq, drk, drv, drg, ddf, ddb, dgn = _ret_bwd(tag, s["proj"], cos, sin, s["decf"], s["decb"], s["gn"], s["ret_saved"], dbr[0])
        gsmall["ret_decay_fwd"][l], gsmall["ret_decay_bwd"][l], gsmall["ret_norm_g"][l] = _fold_lanes(ddf), _fold_lanes(ddb), dgn[0]
        dpool_p = jnp.pad(dbr[1], ((POOL_PAD, POOL_PAD), (0, 0)))
        dpvp, dwbd, dps = _pool_bwd(tag, s["pvp"], s["wbd"], s["pscale"], t_len, dpool_p)
        dpv = dpvp[POOL_PAD:POOL_PAD + t_len]
        gsmall["pool_w"][l] = jnp.stack([dwbd[g * HEAD:(g + 1) * HEAD, g * HEAD:(g + 1) * HEAD] for g in range(NH)])
        gsmall["pool_scale"][l] = dps[0]
        dnq, dkn, dnv, dbtabs[l], dgq = _na_bwd(tag, s["proj"], s["kn"], s["vb"], s["btab"], s["nagq"], dbr[2])
        dnk, dnv, dgk = _na_keys_bwd(tag, s["proj"], s["nagk"], dkn, dnv)
        gsmall["na_q_norm_g"][l], gsmall["na_k_norm_g"][l] = _fold_heads(dgq), _fold_heads(dgk)
        dmq, dmemkv, dgq, dgk = _mem_bwd(tag, s["proj"], s["memkv"], s["mgq"], s["mgk"], dbr[3])
        gsmall["mem_q_norm_g"][l], gsmall["mem_k_norm_g"][l] = _fold_heads(dgq), _fold_heads(dgk)
        g_wmkv = _mm_tn(f"mm_memkv_dw_{tag}", s["hm"], dmemkv, 1)
        dhm = _mm_nt(f"mm_memkv_dx_{tag}", dmemkv, wmkv, 0)
        _, dg = _rms_bwd(f"rms_mem_bwd_{tag}", mem2d, lane(norm_mem_g[l]), [dhm])
        gsmall["norm_mem_g"][l] = dg[0]
        dproj = jnp.concatenate([drq, drk, drv, drg, dpv, dnq, dnk, dnv, dmq], axis=1)
        dproj3 = dproj.reshape(t_len, N_CHIPS, -1).transpose(1, 0, 2)
        g_wi = _mm_tn(f"mm_in_dw_{tag}", s["h"], dproj3, N_CHIPS)
        dh_b = _mm_nt(f"mm_in_dx_{tag}", dproj3, wi, 0)
        dx, dg = _rms_bwd(f"rms_mix_bwd_{tag}", s["x"], lane(norm_mix_g[l]), [dh_a, dh_b], add=dx2)
        gsmall["norm_mix_g"][l] = dg[0]

        full_grads = {
            "w_in": g_wi, "w_gate": g_wg, "w_mem_kv": g_wmkv.reshape(N_CHIPS, -1, 2 * BW), "w_branch": g_wb.reshape(N_CHIPS, NH * BW, BW),
            "w_out": g_wo.reshape(N_CHIPS, -1, d_model), "w_ffn_in": g_wfi, "w_ffn_out": g_wfo.reshape(N_CHIPS, -1, d_model),
        }
        gs = [full_grads[n] for n in BIG]
        r1 = _pair_exchange(f"grad_pair_exchange_{tag}", gs)
        sums = [_pair_sum(f"pair_sum_{n}_{tag}", g[None], r[None], cc, me_chip) for n, g, r in zip(BIG, gs, r1)]
        pbs = [p[0] for p, _ in sums]
        lands = [lax.empty((3,) + p.shape[1:], BF16) for p in pbs]
        tok, wait = _split_copies(f"scatter_{tag}", pbs, lands, 3 * len(BIG), _scatter_plan)
        if pending is not None:
            finish_layer(l + 1, pending, tok)
        pending, order_after = (sums, wait), tok
    finish_layer(0, pending, dx)

    out_g, out_d, out_m, out_v = {}, {}, {}, {}
    for n in BIG:
        shp = W[n].shape
        out_g[n], out_d[n], out_m[n], out_v[n] = [b.reshape(shp) for b in out_bufs[n]]

    gsmall["na_rpb"] = list(_na_bias_grad(jnp.stack(dbtabs)))
    small_g = [jnp.stack(gsmall[n]).reshape(W[n].shape) for n in SMALL] + [loss_part]
    like = [W[n] for n in SMALL] + [loss_part]
    zero = jnp.zeros((1, 1), F32)
    parts = _gather_small(_pack(small_g))
    sg, sd, sm, sv = _small_update(parts, _pack([W[n] for n in SMALL] + [zero]), _pack([M[n] for n in SMALL] + [zero]),
                                   _pack([V[n] for n in SMALL] + [zero]))
    sg, sd, sm, sv = _unpack(sg, like), _unpack(sd, like), _unpack(sm, like), _unpack(sv, like)
    for i, n in enumerate(SMALL):
        out_g[n], out_d[n], out_m[n], out_v[n] = sg[i], sd[i], sm[i], sv[i]
    loss = sg[-1].reshape(())

    return (loss, dx.reshape(x.shape), *[out_g[n] for n in ORDER], *[out_d[n] for n in ORDER],
            *[out_m[n] for n in ORDER], *[out_v[n] for n in ORDER])
```

```python
import functools

import numpy as np
import jax
import jax.numpy as jnp
from jax import lax
from jax.experimental import pallas as pl
from jax.experimental.pallas import tpu as pltpu

F32 = jnp.float32
BF16 = jnp.bfloat16
SDS = jax.ShapeDtypeStruct
MESH = pl.DeviceIdType.MESH
ANY = pl.BlockSpec(memory_space=pl.ANY)

HEAD = 64
NH = 4
BW = NH * HEAD
CH = 128
GRID_W = 64
NA_ROWS = 8
NA_COLS = 16
POOL_HALF = (1, 2, 4, 8)
POOL_PAD = 16
ROPE_THETA = 10000.0
EPS = 1e-6
NEG_INF = -1e30
N_CHIPS = 4
N_DEV = 8
LANES = 128

ADAM_LR = 0.001
ADAM_B1 = 0.9
ADAM_B2 = 0.999
ADAM_EPS = 1e-08
ADAM_WD = 0.01
ADAM_STEP = 10

VMEM_BIG = 56 << 20
VMEM_MM = 44 << 20


def _cp(vmem=None):
    return pltpu.CompilerParams(vmem_limit_bytes=vmem) if vmem else None


def _d(a, b, ca, cb):
    return lax.dot_general(a.astype(BF16), b.astype(BF16), (((ca,), (cb,)), ((), ())),
                           preferred_element_type=F32)


@jax.custom_vjp
def _nn(a, b):
    return _d(a, b, 1, 0)


def _nn_f(a, b):
    return _d(a, b, 1, 0), (a, b)


def _nn_b(r, g):
    a, b = r
    return _d(g, b, 1, 1).astype(a.dtype), _d(a, g, 0, 0).astype(b.dtype)


_nn.defvjp(_nn_f, _nn_b)


@jax.custom_vjp
def _nt(a, b):
    return _d(a, b, 1, 1)


def _nt_f(a, b):
    return _d(a, b, 1, 1), (a, b)


def _nt_b(r, g):
    a, b = r
    return _d(g, b, 1, 0).astype(a.dtype), _d(g, a, 0, 0).astype(b.dtype)


_nt.defvjp(_nt_f, _nt_b)


@jax.custom_vjp
def _tn(a, b):
    return _d(a, b, 0, 0)


def _tn_f(a, b):
    return _d(a, b, 0, 0), (a, b)


def _tn_b(r, g):
    a, b = r
    return _d(b, g, 1, 1).astype(a.dtype), _d(a, g, 1, 0).astype(b.dtype)


_tn.defvjp(_tn_f, _tn_b)


@functools.partial(jax.custom_vjp, nondiff_argnums=(1,))
def _rollr(x, s):
    return pltpu.roll(x, s % x.shape[0], 0)


def _rollr_f(x, s):
    return _rollr(x, s), None


def _rollr_b(s, _, g):
    return (_rollr(g, -s),)


_rollr.defvjp(_rollr_f, _rollr_b)


@jax.custom_vjp
def _swap32(t):
    n = t.shape[1]
    lane = lax.broadcasted_iota(jnp.int32, (1, n), 1)
    return jnp.where((lane & (HEAD // 2)) == 0, pltpu.roll(t, n - HEAD // 2, 1), pltpu.roll(t, HEAD // 2, 1))


def _swap32_f(t):
    return _swap32(t), None


def _swap32_b(_, g):
    return (_swap32(g),)


_swap32.defvjp(_swap32_f, _swap32_b)


def _head_masks():
    lane = lax.broadcasted_iota(jnp.int32, (1, BW), 1)
    return [(lane >= HEAD * h) & (lane < HEAD * (h + 1)) for h in range(NH)]


def _head_rms(x, g):
    out = jnp.zeros_like(x)
    for mh in _head_masks():
        ms = jnp.sum(jnp.where(mh, x * x, 0.0), axis=-1, keepdims=True) * (1.0 / HEAD)
        out = out + jnp.where(mh, x * lax.rsqrt(ms + EPS), 0.0)
    return out * g


def _rms(x, g):
    return x * lax.rsqrt(jnp.mean(x * x, axis=-1, keepdims=True) + EPS) * g


def _rot(t, cos, sin):
    return t * cos + _swap32(t) * sin


def _softmax(s):
    e = jnp.exp(s - lax.stop_gradient(jnp.max(s, axis=-1, keepdims=True)))
    return e / jnp.sum(e, axis=-1, keepdims=True)


def _ret_kv_f(k, v, cos, sin, decf, decb):
    lgf, lgb = jax.nn.log_sigmoid(decf), jax.nn.log_sigmoid(decb)
    kr = _rot(k, cos, sin)
    idx = lax.broadcasted_iota(jnp.int32, (CH, 1), 0).astype(F32)
    r = lax.broadcasted_iota(jnp.int32, (BW, BW), 0) // HEAD
    c = lax.broadcasted_iota(jnp.int32, (BW, BW), 1) // HEAD
    bd = r == c
    kvf = jnp.where(bd, _tn(kr * jnp.exp((CH - 1 - idx) * lgf), v), 0.0)
    kvb = jnp.where(bd, _tn(kr * jnp.exp(idx * lgb), v), 0.0)
    return kvf, kvb


def _ret_scan_f(kvf, kvb, decf, decb):
    n = len(kvf)
    cdf = jnp.exp(CH * jax.nn.log_sigmoid(decf))
    cdb = jnp.exp(CH * jax.nn.log_sigmoid(decb))
    z = jnp.zeros((BW, BW), F32)
    sf, st = [z], z
    for a in range(n - 1):
        st = cdf * st + kvf[a]
        sf.append(st)
    sb, st = [z], z
    for a in range(n - 1, 0, -1):
        st = cdb * st + kvb[a]
        sb.append(st)
    return sf, sb[::-1]


def _ret_out_f(q, k, v, g, cos, sin, stf, stb, decf, decb, gn):
    lgf, lgb = jax.nn.log_sigmoid(decf), jax.nn.log_sigmoid(decb)
    qr = _rot(q, cos, sin) * (HEAD ** -0.5)
    kr = _rot(k, cos, sin)
    diff = (lax.broadcasted_iota(jnp.int32, (CH, CH), 0) - lax.broadcasted_iota(jnp.int32, (CH, CH), 1)).astype(F32)
    o = jnp.zeros_like(q)
    for mh in _head_masks():
        lf = jnp.sum(jnp.where(mh, lgf, 0.0), axis=1, keepdims=True) * (1.0 / HEAD)
        lb = jnp.sum(jnp.where(mh, lgb, 0.0), axis=1, keepdims=True) * (1.0 / HEAD)
        s = _nt(jnp.where(mh, qr, 0.0), kr)
        p = s * jnp.exp(jnp.where(diff >= 0, diff * lf, -diff * lb))
        o = o + jnp.where(mh, _nn(p, v), 0.0)
    idx = lax.broadcasted_iota(jnp.int32, (CH, 1), 0).astype(F32)
    o = o + _nn(qr * jnp.exp((idx + 1) * lgf), stf) + _nn(qr * jnp.exp((CH - idx) * lgb), stb)
    return _head_rms(o, gn) * (g * jax.nn.sigmoid(g))


def _pool_f(pvp, wbd, scale, t_len):
    n = pvp.shape[0]
    w2 = pvp + _rollr(pvp, 1)
    w4 = _rollr(w2, 1) + _rollr(w2, -1)
    w8 = _rollr(w4, 2) + _rollr(w4, -2)
    w16 = _rollr(w8, 4) + _rollr(w8, -4)
    grp = lax.broadcasted_iota(jnp.int32, (1, BW), 1) // HEAD
    ws = jnp.where(grp == 0, w2, jnp.where(grp == 1, w4, jnp.where(grp == 2, w8, w16)))
    half = jnp.where(grp == 0, POOL_HALF[0], jnp.where(grp == 1, POOL_HALF[1], jnp.where(grp == 2, POOL_HALF[2], POOL_HALF[3])))
    t = lax.broadcasted_iota(jnp.int32, (n, 1), 0) - POOL_PAD
    cnt = jnp.minimum(t + half, t_len) - jnp.maximum(t - half, 0)
    cnt = jnp.where((t >= 0) & (t < t_len), cnt, 1).astype(F32)
    pooled = ws / cnt - pvp
    return _nn(pooled, wbd) * scale


def _na_f(q, kn, vw, bias, gq):
    qn = _head_rms(q, gq)
    o = jnp.zeros_like(q)
    for h, mh in enumerate(_head_masks()):
        s = _nt(jnp.where(mh, qn, 0.0), kn) * (HEAD ** -0.5) + jnp.concatenate(bias[h], axis=1)
        o = o + jnp.where(mh, _nn(_softmax(s), vw), 0.0)
    return o


def _mem_f(q, mk, mv, gq, gk):
    qn = _head_rms(q, gq)
    kn = _head_rms(mk, gk)
    o = jnp.zeros_like(q)
    for mh in _head_masks():
        s = _nt(jnp.where(mh, qn, 0.0), kn) * (HEAD ** -0.5)
        o = o + jnp.where(mh, _nn(_softmax(s), mv), 0.0)
    return o


def _gate_f(gp, br, wb):
    out = None
    for n in range(NH):
        t = jax.nn.sigmoid(gp[n]) * _nn(br[n], wb[n])
        out = t if out is None else out + t
    return out


def _swiglu_f(a, g):
    return a * jax.nn.sigmoid(a) * g


MM_ROWS = (1024, 512, 256)
MM_COLS = (1152, 1024, 768, 512)


def _tile(n, prefs):
    for p in prefs:
        if n % p == 0:
            return p
    return n


def _mm_call(name, a, b, extra, dn, grid, a_spec, b_spec, extra_specs, o_spec, out_shape, acc_shape, nred, after=None):
    add = len(extra) == 1
    if after is not None:
        extra, extra_specs = [*extra, after], [*extra_specs, ANY]
    n_extra = len(extra)
    n_steps = int(np.prod([grid[len(grid) - 1 - i] for i in range(nred)]))

    def body(*refs):
        a_ref, b_ref = refs[0], refs[1]
        o_ref, acc = refs[2 + n_extra], refs[-1]
        red = [len(grid) - 1 - i for i in range(nred)]
        first = functools.reduce(jnp.logical_and, [pl.program_id(ax) == 0 for ax in red])
        last = functools.reduce(jnp.logical_and, [pl.program_id(ax) == grid[ax] - 1 for ax in red])
        d = lax.dot_general(a_ref[...].astype(BF16), b_ref[...].astype(BF16), dn, preferred_element_type=F32)

        def finish(r):
            if add:
                r = r + refs[2][...]
            o_ref[...] = r.astype(o_ref.dtype)

        if n_steps == 1:
            finish(d)
        else:
            @pl.when(first)
            def _():
                acc[...] = d

            @pl.when(jnp.logical_not(first) & jnp.logical_not(last))
            def _():
                acc[...] += d

            @pl.when(last)
            def _():
                finish(acc[...] + d)

    return pl.pallas_call(
        body, grid=grid, in_specs=[a_spec, b_spec, *extra_specs], out_specs=o_spec, out_shape=out_shape,
        scratch_shapes=[pltpu.VMEM(acc_shape, F32)] if n_steps > 1 else [], name=name, compiler_params=_cp(VMEM_MM),
    )(a, b, *extra)


def _mm_nn(name, a, w, l, *, out3d=False, add=None, out_dtype=F32):
    m, k = a.shape
    _, s_n, _, ns = w.shape
    tm = _tile(m, MM_ROWS)
    tk = _tile(k, (1024, 1408, 512, 256))
    tn = ns if ns <= 1408 else _tile(ns, MM_COLS)
    grid = (m // tm, s_n, ns // tn, k // tk)
    a_spec = pl.BlockSpec((tm, tk), lambda i, s, j, kk: (i, kk))
    b_spec = pl.BlockSpec((None, None, tk, tn), lambda i, s, j, kk: (l, s, kk, j))
    if out3d:
        o_spec = pl.BlockSpec((None, tm, tn), lambda i, s, j, kk: (s, i, j))
        out_shape = SDS((s_n, m, ns), out_dtype)
    else:
        nj = ns // tn
        o_spec = pl.BlockSpec((tm, tn), lambda i, s, j, kk: (i, s * nj + j))
        out_shape = SDS((m, s_n * ns), out_dtype)
    extra, especs = [], []
    if add is not None:
        assert not out3d
        nj = ns // tn
        extra, especs = [add], [pl.BlockSpec((tm, tn), lambda i, s, j, kk: (i, s * nj + j))]
    return _mm_call(name, a, w, extra, (((1,), (0,)), ((), ())), grid, a_spec, b_spec, especs, o_spec, out_shape, (tm, tn), 1)


def _mm_nt(name, g, w, l, after=None):
    _, s_n, k, ns = w.shape
    g3d = g.ndim == 3
    m = g.shape[1] if g3d else g.shape[0]
    tm = _tile(m, MM_ROWS)
    to = _tile(k, (1024, 1408, 512, 256))
    tk = ns if (ns % LANES or ns <= 1408) else _tile(ns, MM_COLS)
    nks = ns // tk
    grid = (m // tm, k // to, s_n, nks)
    if g3d:
        a_spec = pl.BlockSpec((None, tm, tk), lambda i, j, s, r: (s, i, r))
    else:
        a_spec = pl.BlockSpec((tm, tk), lambda i, j, s, r: (i, s * nks + r))
    b_spec = pl.BlockSpec((None, None, to, tk), lambda i, j, s, r: (l, s, j, r))
    o_spec = pl.BlockSpec((tm, to), lambda i, j, s, r: (i, j))
    return _mm_call(name, g, w, [], (((1,), (1,)), ((), ())), grid, a_spec, b_spec, [], o_spec, SDS((m, k), F32), (tm, to), 2,
                    after=after)


def _mm_tn(name, a, g, s_n):
    t_len, k = a.shape
    g3d = g.ndim == 3
    ns = g.shape[2] if g3d else g.shape[1] // s_n
    tt = _tile(t_len, (1024, 512))
    to = _tile(k, (1024, 768, 1408, 512, 256, 128))
    tn = ns
    nj = ns // tn
    grid = (s_n, k // to, nj, t_len // tt)
    a_spec = pl.BlockSpec((tt, to), lambda s, i, j, t: (t, i))
    if g3d:
        b_spec = pl.BlockSpec((None, tt, tn), lambda s, i, j, t: (s, t, j))
    else:
        b_spec = pl.BlockSpec((tt, tn), lambda s, i, j, t: (t, s * nj + j))
    o_spec = pl.BlockSpec((None, to, tn), lambda s, i, j, t: (s, i, j))
    return _mm_call(name, a, g, [], (((0,), (0,)), ((), ())), grid, a_spec, b_spec, [], o_spec,
                    SDS((s_n, k, ns), F32), (to, tn), 1)


def _rms_fwd(name, x, g):
    m, d = x.shape
    tm = _tile(m, (256,))

    def body(x_ref, g_ref, o_ref):
        o_ref[...] = _rms(x_ref[...], g_ref[...]).astype(BF16)

    row = pl.BlockSpec((tm, d), lambda i: (i, 0))
    return pl.pallas_call(body, grid=(m // tm,), in_specs=[row, pl.BlockSpec((1, d), lambda i: (0, 0))], out_specs=row,
                          out_shape=SDS((m, d), BF16), name=name)(x, g)


def _rms_bwd(name, x, g, dhs, add=None):
    m, d = x.shape
    tm = _tile(m, (256,))
    nh = len(dhs)

    def body(*refs):
        x_ref, g_ref = refs[0], refs[1]
        dh = refs[2][...]
        for r in refs[3:2 + nh]:
            dh = dh + r[...]
        dx_ref, dg_ref = refs[-2], refs[-1]
        _, vjp = jax.vjp(_rms, x_ref[...], g_ref[...])
        dx, dg = vjp(dh)
        if add is not None:
            dx = dx + refs[2 + nh][...]
        dx_ref[...] = dx

        @pl.when(pl.program_id(0) == 0)
        def _():
            dg_ref[...] = jnp.zeros_like(dg_ref)

        dg_ref[...] += dg

    row = pl.BlockSpec((tm, d), lambda i: (i, 0))
    vec = pl.BlockSpec((1, d), lambda i: (0, 0))
    ins = [x, g, *dhs] + ([add] if add is not None else [])
    return pl.pallas_call(body, grid=(m // tm,), in_specs=[row, vec] + [row] * (len(ins) - 2), out_specs=[row, vec],
                          out_shape=[SDS((m, d), F32), SDS((1, d), F32)], name=name)(*ins)


def _swiglu_fwd(name, ag3):
    _, t_len, w = ag3.shape
    tm = _tile(t_len, (512, 256))

    def body(a_ref, g_ref, o_ref):
        o_ref[...] = _swiglu_f(a_ref[...], g_ref[...]).astype(BF16)

    return pl.pallas_call(
        body, grid=(t_len // tm, 2),
        in_specs=[pl.BlockSpec((None, tm, w), lambda i, j: (j, i, 0)), pl.BlockSpec((None, tm, w), lambda i, j: (j + 2, i, 0))],
        out_specs=pl.BlockSpec((tm, w), lambda i, j: (i, j)), out_shape=SDS((t_len, 2 * w), BF16), name=name)(ag3, ag3)


def _swiglu_bwd(name, ag3, du):
    _, t_len, w = ag3.shape
    tm = _tile(t_len, (512, 256))

    def body(a_ref, g_ref, du_ref, o_ref):
        _, vjp = jax.vjp(_swiglu_f, a_ref[...], g_ref[...])
        da, dg = vjp(du_ref[...])
        o_ref[0] = da.astype(BF16)
        o_ref[1] = dg.astype(BF16)

    out = pl.pallas_call(
        body, grid=(t_len // tm, 2),
        in_specs=[pl.BlockSpec((None, tm, w), lambda i, j: (j, i, 0)), pl.BlockSpec((None, tm, w), lambda i, j: (j + 2, i, 0)),
                  pl.BlockSpec((tm, w), lambda i, j: (i, j))],
        out_specs=pl.BlockSpec((2, None, tm, w), lambda i, j: (0, j, i, 0)), out_shape=SDS((2, 2, t_len, w), BF16), name=name)(ag3, ag3, du)
    return out.reshape(ag3.shape)


def _gate_fwd(name, gp3, br, wb, l):
    _, t_len, d = gp3.shape
    tm = _tile(t_len, (512, 256))

    def body(gp_ref, br_ref, wb_ref, o_ref):
        o_ref[...] = _gate_f([gp_ref[n] for n in range(NH)], [br_ref[n] for n in range(NH)],
                             [wb_ref[n] for n in range(NH)]).astype(BF16)

    return pl.pallas_call(
        body, grid=(t_len // tm, d // BW),
        in_specs=[pl.BlockSpec((NH, tm, BW), lambda i, s: (0, i, s)), pl.BlockSpec((NH, tm, BW), lambda i, s: (0, i, 0)),
                  pl.BlockSpec((None, None, NH, BW, BW), lambda i, s: (l, s, 0, 0, 0))],
        out_specs=pl.BlockSpec((tm, BW), lambda i, s: (i, s)), out_shape=SDS((t_len, d), BF16), name=name)(gp3, br, wb)


def _gate_bwd(name, gp3, br, wb, l, dmerged):
    _, t_len, d = gp3.shape
    tm = _tile(t_len, (512, 256))
    ns = d // BW

    def body(gp_ref, br_ref, wb_ref, dm_ref, dgp_ref, dbr_ref, dwb_ref):
        i, s = pl.program_id(0), pl.program_id(1)
        gp = [gp_ref[n] for n in range(NH)]
        brv = [br_ref[n].astype(F32) for n in range(NH)]
        wbv = [wb_ref[n].astype(F32) for n in range(NH)]
        _, vjp = jax.vjp(_gate_f, gp, brv, wbv)
        dgp, dbr, dwb = vjp(dm_ref[...])

        @pl.when(s == 0)
        def _():
            dbr_ref[...] = jnp.zeros_like(dbr_ref)

        @pl.when((i == 0) & (s == 0))
        def _():
            dwb_ref[...] = jnp.zeros_like(dwb_ref)

        for n in range(NH):
            dgp_ref[n] = dgp[n].astype(BF16)
            dbr_ref[n] += dbr[n]
            dwb_ref[s, n] += dwb[n]

    return pl.pallas_call(
        body, grid=(t_len // tm, ns),
        in_specs=[pl.BlockSpec((NH, tm, BW), lambda i, s: (0, i, s)), pl.BlockSpec((NH, tm, BW), lambda i, s: (0, i, 0)),
                  pl.BlockSpec((None, None, NH, BW, BW), lambda i, s: (l, s, 0, 0, 0)), pl.BlockSpec((tm, BW), lambda i, s: (i, s))],
        out_specs=[pl.BlockSpec((NH, tm, BW), lambda i, s: (0, i, s)), pl.BlockSpec((NH, tm, BW), lambda i, s: (0, i, 0)),
                   pl.BlockSpec((ns, NH, BW, BW), lambda i, s: (0, 0, 0, 0))],
        out_shape=[SDS(gp3.shape, BF16), SDS((NH, t_len, BW), F32), SDS((ns, NH, BW, BW), F32)], name=name,
        compiler_params=_cp(VMEM_MM))(gp3, br, wb, dmerged)


def _loss_kernel(name, y, tgt):
    m, d = y.shape
    tm = _tile(m, (256,))

    def body(y_ref, t_ref, l_ref, dy_ref):
        err = y_ref[...] - t_ref[...]
        dy_ref[...] = err * (1.0 / d)

        @pl.when(pl.program_id(0) == 0)
        def _():
            l_ref[...] = jnp.zeros_like(l_ref)

        l_ref[...] += 0.5 * jnp.sum(jnp.mean(err * err, axis=-1, keepdims=True), axis=0, keepdims=True)

    row = pl.BlockSpec((tm, d), lambda i: (i, 0))
    return pl.pallas_call(body, grid=(m // tm,), in_specs=[row, row], out_specs=[pl.BlockSpec((1, 1), lambda i: (0, 0)), row],
                          out_shape=[SDS((1, 1), F32), SDS((m, d), F32)], name=name)(y, tgt)


def _zero_first(refs):
    @pl.when(pl.program_id(0) == 0)
    def _():
        for r in refs:
            r[...] = jnp.zeros_like(r)


def _grp(rows, j):
    return pl.BlockSpec((rows, BW), lambda a, j=j: (a, j))


_VEC = pl.BlockSpec((1, BW), lambda a: (0, 0))


def _ret_fwd(tag, proj, cos, sin, decf, decb, gn):
    t_len = proj.shape[0]
    n = t_len // CH
    tab = pl.BlockSpec((CH, BW), lambda a: (a, 0))
    st = pl.BlockSpec((None, BW, BW), lambda a: (a, 0, 0))

    def k1(k_ref, v_ref, cos_ref, sin_ref, df_ref, db_ref, kvf_ref, kvb_ref):
        kvf_ref[...], kvb_ref[...] = _ret_kv_f(k_ref[...], v_ref[...], cos_ref[...], sin_ref[...], df_ref[...], db_ref[...])

    kvf, kvb = pl.pallas_call(
        k1, grid=(n,), in_specs=[_grp(CH, 1), _grp(CH, 2), tab, tab, _VEC, _VEC], out_specs=[st, st],
        out_shape=[SDS((n, BW, BW), F32)] * 2, name=f"ret_kv_{tag}")(proj, proj, cos, sin, decf, decb)

    def k2(kvf_ref, kvb_ref, df_ref, db_ref, sf_ref, sb_ref):
        sf, sb = _ret_scan_f([kvf_ref[a] for a in range(n)], [kvb_ref[a] for a in range(n)], df_ref[...], db_ref[...])
        for a in range(n):
            sf_ref[a] = sf[a]
            sb_ref[a] = sb[a]

    stf, stb = pl.pallas_call(k2, out_shape=[SDS((n, BW, BW), F32)] * 2, name=f"ret_scan_{tag}",
                              compiler_params=_cp(VMEM_BIG))(kvf, kvb, decf, decb)

    def k3(q_ref, k_ref, v_ref, g_ref, cos_ref, sin_ref, sf_ref, sb_ref, df_ref, db_ref, gn_ref, o_ref):
        o_ref[...] = _ret_out_f(q_ref[...], k_ref[...], v_ref[...], g_ref[...], cos_ref[...], sin_ref[...], sf_ref[...],
                                sb_ref[...], df_ref[...], db_ref[...], gn_ref[...]).astype(BF16)

    ret = pl.pallas_call(
        k3, grid=(n,), in_specs=[_grp(CH, 0), _grp(CH, 1), _grp(CH, 2), _grp(CH, 3), tab, tab, st, st, _VEC, _VEC, _VEC],
        out_specs=tab, out_shape=SDS((t_len, BW), BF16), name=f"ret_out_{tag}")(proj, proj, proj, proj, cos, sin, stf, stb, decf, decb, gn)
    return ret, (kvf, kvb, stf, stb)


def _ret_bwd(tag, proj, cos, sin, decf, decb, gn, saved, dret):
    kvf, kvb, stf, stb = saved
    t_len = proj.shape[0]
    n = t_len // CH
    tab = pl.BlockSpec((CH, BW), lambda a: (a, 0))
    st = pl.BlockSpec((None, BW, BW), lambda a: (a, 0, 0))

    def k3(q_ref, k_ref, v_ref, g_ref, cos_ref, sin_ref, sf_ref, sb_ref, df_ref, db_ref, gn_ref, do_ref,
           dq_ref, dk_ref, dv_ref, dg_ref, dsf_ref, dsb_ref, ddf_ref, ddb_ref, dgn_ref):
        cos, sin = cos_ref[...], sin_ref[...]
        f = lambda q, k, v, g, sf, sb, df, db, gnv: _ret_out_f(q, k, v, g, cos, sin, sf, sb, df, db, gnv)
        _, vjp = jax.vjp(f, q_ref[...], k_ref[...], v_ref[...], g_ref[...], sf_ref[...], sb_ref[...], df_ref[...], db_ref[...], gn_ref[...])
        dq, dk, dv, dg, dsf, dsb, ddf, ddb, dgn = vjp(do_ref[...])
        dq_ref[...] = dq.astype(BF16)
        dk_ref[...] = dk
        dv_ref[...] = dv
        dg_ref[...] = dg.astype(BF16)
        dsf_ref[...] = dsf
        dsb_ref[...] = dsb
        _zero_first([ddf_ref, ddb_ref, dgn_ref])
        ddf_ref[...] += ddf
        ddb_ref[...] += ddb
        dgn_ref[...] += dgn

    dq, dk3, dv3, dg, dstf, dstb, ddf3, ddb3, dgn = pl.pallas_call(
        k3, grid=(n,),
        in_specs=[_grp(CH, 0), _grp(CH, 1), _grp(CH, 2), _grp(CH, 3), tab, tab, st, st, _VEC, _VEC, _VEC, tab],
        out_specs=[tab, tab, tab, tab, st, st, _VEC, _VEC, _VEC],
        out_shape=[SDS((t_len, BW), BF16), SDS((t_len, BW), F32), SDS((t_len, BW), F32), SDS((t_len, BW), BF16),
                   SDS((n, BW, BW), F32), SDS((n, BW, BW), F32), SDS((1, BW), F32), SDS((1, BW), F32), SDS((1, BW), F32)],
        name=f"ret_out_bwd_{tag}")(proj, proj, proj, proj, cos, sin, stf, stb, decf, decb, gn, dret)

    def k2(kvf_ref, kvb_ref, df_ref, db_ref, dsf_ref, dsb_ref, dkvf_ref, dkvb_ref, ddf_ref, ddb_ref):
        _, vjp = jax.vjp(_ret_scan_f, [kvf_ref[a] for a in range(n)], [kvb_ref[a] for a in range(n)], df_ref[...], db_ref[...])
        dkvf, dkvb, ddf_ref[...], ddb_ref[...] = vjp(([dsf_ref[a] for a in range(n)], [dsb_ref[a] for a in range(n)]))
        for a in range(n):
            dkvf_ref[a] = dkvf[a]
            dkvb_ref[a] = dkvb[a]

    dkvf, dkvb, ddf2, ddb2 = pl.pallas_call(
        k2, out_shape=[SDS((n, BW, BW), F32), SDS((n, BW, BW), F32), SDS((1, BW), F32), SDS((1, BW), F32)],
        name=f"ret_scan_bwd_{tag}", compiler_params=_cp(VMEM_BIG))(kvf, kvb, decf, decb, dstf, dstb)

    def k1(k_ref, v_ref, cos_ref, sin_ref, df_ref, db_ref, dkvf_ref, dkvb_ref, dk3_ref, dv3_ref, dk_ref, dv_ref, ddf_ref, ddb_ref):
        cos, sin = cos_ref[...], sin_ref[...]
        f = lambda k, v, df, db: _ret_kv_f(k, v, cos, sin, df, db)
        _, vjp = jax.vjp(f, k_ref[...], v_ref[...], df_ref[...], db_ref[...])
        dk, dv, ddf, ddb = vjp((dkvf_ref[...], dkvb_ref[...]))
        dk_ref[...] = (dk + dk3_ref[...]).astype(BF16)
        dv_ref[...] = (dv + dv3_ref[...]).astype(BF16)
        _zero_first([ddf_ref, ddb_ref])
        ddf_ref[...] += ddf
        ddb_ref[...] += ddb

    dk, dv, ddf1, ddb1 = pl.pallas_call(
        k1, grid=(n,), in_specs=[_grp(CH, 1), _grp(CH, 2), tab, tab, _VEC, _VEC, st, st, tab, tab],
        out_specs=[tab, tab, _VEC, _VEC],
        out_shape=[SDS((t_len, BW), BF16), SDS((t_len, BW), BF16), SDS((1, BW), F32), SDS((1, BW), F32)],
        name=f"ret_kv_bwd_{tag}")(proj, proj, cos, sin, decf, decb, dkvf, dkvb, dk3, dv3)
    return dq, dk, dv, dg, ddf1 + ddf2 + ddf3, ddb1 + ddb2 + ddb3, dgn


def _pool_fwd(tag, pvp, wbd, scale, t_len):
    def body(p_ref, w_ref, s_ref, o_ref):
        o_ref[...] = _pool_f(p_ref[...], w_ref[...], s_ref[...], t_len).astype(BF16)

    return pl.pallas_call(body, out_shape=SDS(pvp.shape, BF16), name=f"pool_{tag}", compiler_params=_cp(VMEM_BIG))(pvp, wbd, scale)


def _pool_bwd(tag, pvp, wbd, scale, t_len, dpool_p):
    def body(p_ref, w_ref, s_ref, do_ref, dp_ref, dw_ref, ds_ref):
        f = lambda p, w, s: _pool_f(p, w, s, t_len)
        _, vjp = jax.vjp(f, p_ref[...], w_ref[...], s_ref[...])
        dp, dw, ds = vjp(do_ref[...])
        dp_ref[...] = dp.astype(BF16)
        dw_ref[...] = dw
        ds_ref[...] = ds

    return pl.pallas_call(body, out_shape=[SDS(pvp.shape, BF16), SDS((BW, BW), F32), SDS((1, BW), F32)],
                          name=f"pool_bwd_{tag}", compiler_params=_cp(VMEM_BIG))(pvp, wbd, scale, dpool_p)


def _na_window(r, rows):
    r0 = jnp.clip(r - NA_ROWS // 2, 0, rows - NA_ROWS)
    return r0, r - r0


NA_PAIRS = 2 * NA_ROWS - 2


def _na_bias_pieces(b_ref, pat):
    return [[b_ref[h * NA_PAIRS + 2 * k - pat + NA_ROWS - 1] for k in range(NA_ROWS // 2)] for h in range(NH)]


def _na_keys(tag, proj, gk):
    t_len = proj.shape[0]
    tm = _tile(t_len, (256,))

    def body(k_ref, v_ref, gk_ref, kn_ref, vb_ref):
        kn_ref[...] = _head_rms(k_ref[...], gk_ref[...]).astype(BF16)
        vb_ref[...] = v_ref[...].astype(BF16)

    row = pl.BlockSpec((tm, BW), lambda a: (a, 0))
    return pl.pallas_call(body, grid=(t_len // tm,), in_specs=[_grp(tm, 6), _grp(tm, 7), _VEC], out_specs=[row, row],
                          out_shape=[SDS((t_len, BW), BF16)] * 2, name=f"na_keys_{tag}")(proj, proj, gk)


def _na_keys_bwd(tag, proj, gk, dkn, dv):
    t_len = proj.shape[0]
    tm = _tile(t_len, (256,))

    def body(k_ref, gk_ref, dkn_ref, dv_ref, dk_ref, dvb_ref, dgk_ref):
        _, vjp = jax.vjp(_head_rms, k_ref[...], gk_ref[...])
        dk, dgk = vjp(dkn_ref[...])
        dk_ref[...] = dk.astype(BF16)
        dvb_ref[...] = dv_ref[...].astype(BF16)
        _zero_first([dgk_ref])
        dgk_ref[...] += dgk

    row = pl.BlockSpec((tm, BW), lambda a: (a, 0))
    return pl.pallas_call(body, grid=(t_len // tm,), in_specs=[_grp(tm, 6), _VEC, row, row], out_specs=[row, row, _VEC],
                          out_shape=[SDS((t_len, BW), BF16), SDS((t_len, BW), BF16), SDS((1, BW), F32)],
                          name=f"na_keys_bwd_{tag}")(proj, gk, dkn, dv)


def _na_fwd(tag, proj, kn, vb, bias_tab, gq):
    t_len = proj.shape[0]
    rows = t_len // GRID_W
    win = NA_ROWS * GRID_W

    def body(q_ref, k_ref, v_ref, b_ref, gq_ref, o_ref):
        r0, pat = _na_window(pl.program_id(0), rows)
        start = pl.multiple_of(r0 * GRID_W, GRID_W)
        o_ref[...] = _na_f(q_ref[...], k_ref[pl.ds(start, win), :], v_ref[pl.ds(start, win), :], _na_bias_pieces(b_ref, pat),
                           gq_ref[...]).astype(BF16)

    whole = pl.BlockSpec((t_len, BW), lambda a: (0, 0))
    return pl.pallas_call(
        body, grid=(rows,),
        in_specs=[_grp(GRID_W, 5), whole, whole, pl.BlockSpec(bias_tab.shape, lambda a: (0, 0, 0)), _VEC],
        out_specs=pl.BlockSpec((GRID_W, BW), lambda a: (a, 0)), out_shape=SDS((t_len, BW), BF16),
        name=f"na_{tag}", compiler_params=_cp(VMEM_MM))(proj, kn, vb, bias_tab, gq)


def _na_bwd(tag, proj, kn, vb, bias_tab, gq, dna):
    t_len = proj.shape[0]
    rows = t_len // GRID_W
    win = NA_ROWS * GRID_W

    def body(q_ref, k_ref, v_ref, b_ref, gq_ref, do_ref, dq_ref, dk_ref, dv_ref, db_ref, dgq_ref):
        r0, pat = _na_window(pl.program_id(0), rows)
        start = pl.multiple_of(r0 * GRID_W, GRID_W)
        _zero_first([dk_ref, dv_ref, db_ref, dgq_ref])
        _, vjp = jax.vjp(_na_f, q_ref[...], k_ref[pl.ds(start, win), :].astype(F32), v_ref[pl.ds(start, win), :].astype(F32),
                         _na_bias_pieces(b_ref, pat), gq_ref[...])
        dq, dk, dv, db, dgq = vjp(do_ref[...])
        dq_ref[...] = dq.astype(BF16)
        dk_ref[pl.ds(start, win), :] += dk
        dv_ref[pl.ds(start, win), :] += dv
        for h in range(NH):
            for k in range(NA_ROWS // 2):
                db_ref[h * NA_PAIRS + 2 * k - pat + NA_ROWS - 1] += db[h][k]
        dgq_ref[...] += dgq

    whole = pl.BlockSpec((t_len, BW), lambda a: (0, 0))
    tabspec = pl.BlockSpec(bias_tab.shape, lambda a: (0, 0, 0))
    row = pl.BlockSpec((GRID_W, BW), lambda a: (a, 0))
    return pl.pallas_call(
        body, grid=(rows,), in_specs=[_grp(GRID_W, 5), whole, whole, tabspec, _VEC, row],
        out_specs=[row, whole, whole, tabspec, _VEC],
        out_shape=[SDS((t_len, BW), BF16), SDS((t_len, BW), F32), SDS((t_len, BW), F32), SDS(bias_tab.shape, F32), SDS((1, BW), F32)],
        name=f"na_bwd_{tag}", compiler_params=_cp(VMEM_BIG))(proj, kn, vb, bias_tab, gq, dna)


def _mem_fwd(tag, proj, memkv, gq, gk):
    t_len = proj.shape[0]
    n_mem = memkv.shape[0]
    tm = _tile(t_len, (256,))

    def body(q_ref, mk_ref, mv_ref, gq_ref, gk_ref, o_ref):
        o_ref[...] = _mem_f(q_ref[...], mk_ref[...], mv_ref[...], gq_ref[...], gk_ref[...]).astype(BF16)

    mspec = lambda j: pl.BlockSpec((n_mem, BW), lambda a, j=j: (0, j))
    return pl.pallas_call(
        body, grid=(t_len // tm,), in_specs=[_grp(tm, 8), mspec(0), mspec(1), _VEC, _VEC],
        out_specs=pl.BlockSpec((tm, BW), lambda a: (a, 0)), out_shape=SDS((t_len, BW), BF16), name=f"mem_{tag}")(proj, memkv, memkv, gq, gk)


def _mem_bwd(tag, proj, memkv, gq, gk, dmo):
    t_len = proj.shape[0]
    n_mem = memkv.shape[0]
    tm = _tile(t_len, (256,))

    def body(q_ref, mk_ref, mv_ref, gq_ref, gk_ref, do_ref, dq_ref, dmk_ref, dmv_ref, dgq_ref, dgk_ref):
        _zero_first([dmk_ref, dmv_ref, dgq_ref, dgk_ref])
        _, vjp = jax.vjp(_mem_f, q_ref[...], mk_ref[...], mv_ref[...], gq_ref[...], gk_ref[...])
        dq, dmk, dmv, dgq, dgk = vjp(do_ref[...])
        dq_ref[...] = dq.astype(BF16)
        dmk_ref[...] += dmk
        dmv_ref[...] += dmv
        dgq_ref[...] += dgq
        dgk_ref[...] += dgk

    mspec = lambda j: pl.BlockSpec((n_mem, BW), lambda a, j=j: (0, j))
    mout = pl.BlockSpec((n_mem, BW), lambda a: (0, 0))
    row = pl.BlockSpec((tm, BW), lambda a: (a, 0))
    dq, dmk, dmv, dgq, dgk = pl.pallas_call(
        body, grid=(t_len // tm,), in_specs=[_grp(tm, 8), mspec(0), mspec(1), _VEC, _VEC, row],
        out_specs=[row, mout, mout, _VEC, _VEC],
        out_shape=[SDS((t_len, BW), BF16), SDS((n_mem, BW), F32), SDS((n_mem, BW), F32), SDS((1, BW), F32), SDS((1, BW), F32)],
        name=f"mem_bwd_{tag}")(proj, memkv, memkv, gq, gk, dmo)
    return dq, jnp.concatenate([dmk, dmv], axis=1), dgq, dgk


NA_NJ = 2 * NA_COLS - 1


def _na_onehot():
    q = np.arange(GRID_W)[:, None]
    kc = np.arange(GRID_W)[None, :]
    qwin = np.clip(q - NA_COLS // 2, 0, GRID_W - NA_COLS)
    mask = (kc >= qwin) & (kc < qwin + NA_COLS)
    col = np.clip(kc - q, -(NA_COLS - 1), NA_COLS - 1) + NA_COLS - 1
    onehot = np.zeros((LANES, GRID_W, 2, GRID_W), np.float32)
    qq, kk = np.nonzero(mask)
    for half in range(2):
        onehot[half * NA_NJ + col[qq, kk], qq, half, kk] = 1.0
    valid = np.broadcast_to(mask[:, None, :], (GRID_W, 2, GRID_W)).astype(np.float32)
    return onehot.reshape(LANES, -1), valid.reshape(1, -1)


def _na_pair_rows(rpb):
    n_layers = rpb.shape[0]
    pair = jnp.concatenate([rpb[:, :, :-1], rpb[:, :, 1:]], axis=-1).reshape(n_layers, NH * NA_PAIRS, 2 * NA_NJ)
    return jnp.pad(pair, ((0, 0), (0, 0), (0, LANES - 2 * NA_NJ)))


def _na_bias_table(rpb):
    n_layers = rpb.shape[0]
    onehot, valid = _na_onehot()
    width = onehot.shape[1]

    def body(r_ref, oh_ref, ok_ref, o_ref):
        t = jnp.dot(r_ref[...], oh_ref[...], precision=lax.Precision.HIGHEST, preferred_element_type=F32)
        o_ref[...] = jnp.where(ok_ref[...] > 0, t, NEG_INF)

    out = pl.pallas_call(
        body, grid=(n_layers,),
        in_specs=[pl.BlockSpec((None, NH * NA_PAIRS, LANES), lambda l: (l, 0, 0)), pl.BlockSpec((LANES, width), lambda l: (0, 0)),
                  pl.BlockSpec((1, width), lambda l: (0, 0))],
        out_specs=pl.BlockSpec((None, NH * NA_PAIRS, width), lambda l: (l, 0, 0)),
        out_shape=SDS((n_layers, NH * NA_PAIRS, width), F32), name="na_bias_table")(
            _na_pair_rows(rpb), jnp.asarray(onehot), jnp.asarray(valid))
    return out.reshape(n_layers, NH * NA_PAIRS, GRID_W, 2 * GRID_W)


def _na_bias_grad(dtab):
    n_layers = dtab.shape[0]
    onehot, _ = _na_onehot()
    width = onehot.shape[1]

    def body(x_ref, oh_ref, o_ref):
        o_ref[...] = jnp.dot(x_ref[...], oh_ref[...], precision=lax.Precision.HIGHEST, preferred_element_type=F32)

    out = pl.pallas_call(
        body, grid=(n_layers,),
        in_specs=[pl.BlockSpec((None, NH * NA_PAIRS, width), lambda l: (l, 0, 0)), pl.BlockSpec((width, LANES), lambda l: (0, 0))],
        out_specs=pl.BlockSpec((None, NH * NA_PAIRS, LANES), lambda l: (l, 0, 0)),
        out_shape=SDS((n_layers, NH * NA_PAIRS, LANES), F32), name="na_bias_grad")(
            dtab.reshape(n_layers, NH * NA_PAIRS, width), jnp.asarray(onehot.T.copy()))
    out = out.reshape(n_layers, NH, NA_PAIRS, LANES)
    zero = jnp.zeros((n_layers, NH, 1, NA_NJ), F32)
    return (jnp.concatenate([out[..., :NA_NJ], zero], axis=2) + jnp.concatenate([zero, out[..., NA_NJ:2 * NA_NJ]], axis=2))


def _place():
    x, y, c = lax.axis_index("x"), lax.axis_index("y"), lax.axis_index("c")
    chips = [(1 - x, y), (x, 1 - y), (1 - x, 1 - y)]
    return x, y, c, chips


def _remote(src, dst, ssem, rsem, dev):
    return pltpu.make_async_remote_copy(src_ref=src, dst_ref=dst, send_sem=ssem, recv_sem=rsem, device_id=dev, device_id_type=MESH)


HBM = pl.BlockSpec(memory_space=pltpu.HBM)
SEM = pl.BlockSpec(memory_space=pltpu.SEMAPHORE)
DATAFLOW = pltpu.SideEffectType.DATAFLOW_SIDE_EFFECTING


def _split_copies(name, srcs, lands, n_copies, plan):
    n_s, n_l = len(srcs), len(lands)
    hbm = lambda a: pltpu.HBM(a.shape, a.dtype)

    def start_body(*refs):
        ins = refs[:n_s + n_l]
        outs = refs[n_s + n_l:]
        sems, token = outs[:2 * n_copies], outs[-1]
        for k, (src, dst, dev) in enumerate(plan(ins[:n_s], ins[n_s:])):
            _remote(src, dst, sems[k], sems[n_copies + k], dev).start()
        token[...] = jnp.zeros_like(token)

    outs = pl.pallas_call(
        start_body, name=f"{name}_start",
        out_shape=(pltpu.SemaphoreType.DMA(()),) * (2 * n_copies) + tuple(hbm(a) for a in (*srcs, *lands)) + (SDS((8, LANES), F32),),
        in_specs=(HBM,) * (n_s + n_l), out_specs=(SEM,) * (2 * n_copies) + (HBM,) * (n_s + n_l) + (pl.BlockSpec(memory_space=pltpu.VMEM),),
        input_output_aliases={i: 2 * n_copies + i for i in range(n_s + n_l)},
        compiler_params=pltpu.CompilerParams(has_side_effects=DATAFLOW),
    )(*[pltpu.with_memory_space_constraint(a, pltpu.HBM) for a in (*srcs, *lands)])
    sems, thru, token = outs[:2 * n_copies], outs[2 * n_copies:-1], outs[-1]

    def wait(after_wait):
        def wait_body(*refs):
            ins = refs[:n_s + n_l]
            sem_refs = refs[n_s + n_l:n_s + n_l + 2 * n_copies]
            for k, (src, dst, dev) in enumerate(plan(ins[:n_s], ins[n_s:])):
                cp = _remote(src, dst, sem_refs[k], sem_refs[n_copies + k], dev)
                cp.wait_send()
                cp.wait_recv()

        res = pl.pallas_call(
            wait_body, name=f"{name}_wait", out_shape=tuple(hbm(a) for a in (*srcs, *lands)),
            in_specs=(HBM,) * (n_s + n_l) + (SEM,) * (2 * n_copies) + (ANY,), out_specs=(HBM,) * (n_s + n_l),
            input_output_aliases={i: i for i in range(n_s + n_l)},
            compiler_params=pltpu.CompilerParams(has_side_effects=DATAFLOW),
        )(*thru, *sems, after_wait)
        return list(res[n_s:])

    return token, wait


def _gather_plan(srcs, lands):
    x, y, c, chips = _place()
    me = 2 * x + y
    out = []
    for src, land in zip(srcs, lands):
        h = src.shape[0] // 2
        out.append((src, land.at[me], (x, y, 1 - c)))
        for px, py in chips:
            out.append((src.at[pl.ds(c * h, h)], land.at[me, pl.ds(c * h, h)], (px, py, c)))
    return out


def _scatter_plan(srcs, lands):
    x, y, c, chips = _place()
    out = []
    for src, land in zip(srcs, lands):
        for r, (px, py) in enumerate(chips):
            out.append((src.at[2 * px + py], land.at[r], (px, py, c)))
    return out


def _forward_halves(name, lands):
    n = len(lands)

    def body(*refs):
        ins, outs = refs[:n], refs[n:2 * n]
        ssem, rsem = refs[2 * n:]
        x, y, c, chips = _place()
        sib = (x, y, 1 - c)

        def copy(i, r, half):
            h = ins[i].shape[1] // 2
            px, py = chips[r]
            rows = pl.ds(half * h, h)
            return _remote(ins[i].at[2 * px + py, rows], outs[i].at[2 * px + py, rows], ssem.at[i, r], rsem.at[i, r], sib)

        for i in range(n):
            for r in range(3):
                copy(i, r, c).start()
        for i in range(n):
            for r in range(3):
                copy(i, r, 1 - c).wait_recv()
                copy(i, r, c).wait_send()

    return pl.pallas_call(
        body, in_specs=[ANY] * n, out_specs=[ANY] * n, out_shape=[SDS(a.shape, a.dtype) for a in lands],
        input_output_aliases={i: i for i in range(n)},
        scratch_shapes=[pltpu.SemaphoreType.DMA((n, 3)), pltpu.SemaphoreType.DMA((n, 3))], name=name)(*lands)


def _pair_exchange(name, gs):
    n = len(gs)

    def body(*refs):
        ins, outs = refs[:n], refs[n:2 * n]
        ssem, rsem = refs[2 * n:]
        x, y, c, _ = _place()
        cps = []
        for i in range(n):
            h = ins[i].shape[1] // 2
            cps.append(_remote(ins[i].at[:, pl.ds((1 - c) * h, h)], outs[i], ssem.at[i], rsem.at[i], (x, y, 1 - c)))
            cps[-1].start()
        for cp in cps:
            cp.wait()

    return pl.pallas_call(
        body, in_specs=[ANY] * n, out_specs=[ANY] * n,
        out_shape=[SDS((g.shape[0], g.shape[1] // 2, g.shape[2]), g.dtype) for g in gs],
        scratch_shapes=[pltpu.SemaphoreType.DMA((n,)), pltpu.SemaphoreType.DMA((n,))], name=name)(*gs)


def _half_exchange(name, fs):
    n = len(fs)

    def body(*refs):
        ins, outs = refs[:n], refs[n:2 * n]
        ssem, rsem = refs[2 * n:]
        x, y, c, _ = _place()

        def copy(i, half):
            h = ins[i].shape[0] // 2
            return _remote(ins[i].at[pl.ds(half * h, h)], outs[i].at[pl.ds(half * h, h)], ssem.at[i], rsem.at[i], (x, y, 1 - c))

        for i in range(n):
            copy(i, c).start()
        for i in range(n):
            copy(i, 1 - c).wait_recv()
            copy(i, c).wait_send()

    return pl.pallas_call(
        body, in_specs=[ANY] * n, out_specs=[ANY] * n, out_shape=[SDS(f.shape, f.dtype) for f in fs],
        input_output_aliases={i: i for i in range(n)},
        scratch_shapes=[pltpu.SemaphoreType.DMA((n,)), pltpu.SemaphoreType.DMA((n,))], name=name)(*fs)


def _gather_small(v):
    def body(v_ref, o_ref, lsem, ssem, rsem):
        x, y, c, _ = _place()
        me = 4 * x + 2 * y + c
        flips = [(fx, fy, fc) for fx in (0, 1) for fy in (0, 1) for fc in (0, 1)][1:]
        peer = lambda f: (1 - x if f[0] else x, 1 - y if f[1] else y, 1 - c if f[2] else c)
        loc = pltpu.make_async_copy(v_ref, o_ref.at[me], lsem)
        loc.start()
        cps = [_remote(v_ref, o_ref.at[me], ssem.at[k], rsem.at[k], peer(f)) for k, f in enumerate(flips)]
        for cp in cps:
            cp.start()
        for k, f in enumerate(flips):
            px, py, pc = peer(f)
            land = o_ref.at[4 * px + 2 * py + pc]
            _remote(land, land, ssem.at[k], rsem.at[k], peer(f)).wait_recv()
        for cp in cps:
            cp.wait_send()
        loc.wait()

    return pl.pallas_call(
        body, in_specs=[ANY], out_specs=ANY, out_shape=SDS((N_DEV,) + v.shape, v.dtype),
        scratch_shapes=[pltpu.SemaphoreType.DMA, pltpu.SemaphoreType.DMA((N_DEV - 1,)), pltpu.SemaphoreType.DMA((N_DEV - 1,))],
        name="gather_small")(v)


def _rows_tile(r):
    return _tile(r, (256, 352, 128, 64))


def _pair_sum(name, g, r1, core, me):
    n_l, s_n, h, c = r1.shape
    tr = _rows_tile(h)
    nb = h // tr

    def body(idx_ref, g_ref, r_ref, pb_ref, own_ref):
        p = g_ref[...] + r_ref[...]
        pb_ref[...] = p.astype(BF16)

        @pl.when(pl.program_id(2) == idx_ref[1])
        def _():
            own_ref[...] = p

    blk = (None, None, tr, c)
    grid_spec = pltpu.PrefetchScalarGridSpec(
        num_scalar_prefetch=1, grid=(n_l, nb, s_n),
        in_specs=[pl.BlockSpec(blk, lambda a, i, s, idx: (a, s, idx[0] * nb + i, 0)), pl.BlockSpec(blk, lambda a, i, s, idx: (a, s, i, 0))],
        out_specs=[pl.BlockSpec(blk, lambda a, i, s, idx: (a, s, i, 0)), pl.BlockSpec((None, tr, c), lambda a, i, s, idx: (a, i, 0))])
    idx = jnp.stack([core, me]).astype(jnp.int32)
    return pl.pallas_call(body, grid_spec=grid_spec, out_shape=[SDS(r1.shape, BF16), SDS((n_l, h, c), F32)], name=name)(idx, g, r1)


def _chip_sum(name, own, r2, core):
    n_l, h, c = own.shape
    tr = _rows_tile(h)
    nb = h // tr

    def body(idx_ref, o_ref, r_ref, f_ref):
        f_ref[...] = ((o_ref[...] + r_ref[0].astype(F32)) + r_ref[1].astype(F32)) + r_ref[2].astype(F32)

    grid_spec = pltpu.PrefetchScalarGridSpec(
        num_scalar_prefetch=1, grid=(n_l, nb),
        in_specs=[pl.BlockSpec((None, tr, c), lambda a, i, idx: (a, i, 0)), pl.BlockSpec((3, None, tr, c), lambda a, i, idx: (0, a, i, 0))],
        out_specs=pl.BlockSpec((None, tr, c), lambda a, i, idx: (a, idx[0] * nb + i, 0)))
    return pl.pallas_call(body, grid_spec=grid_spec, out_shape=SDS((n_l, 2 * h, c), F32), name=name)(
        jnp.reshape(core, (1,)).astype(jnp.int32), own, r2)


def _adamw_math(w, g, m, v):
    m = ADAM_B1 * m + (1.0 - ADAM_B1) * g
    v = ADAM_B2 * v + (1.0 - ADAM_B2) * jnp.square(g)
    m_hat = m / (1.0 - ADAM_B1 ** ADAM_STEP)
    v_hat = v / (1.0 - ADAM_B2 ** ADAM_STEP)
    delta = -ADAM_LR * (m_hat / (jnp.sqrt(v_hat) + ADAM_EPS) + ADAM_WD * w)
    return delta, m, v


def _adamw(name, l, w, g, m, v, bufs):
    _, r, c = w.shape
    tr = _rows_tile(r)

    def body(w_ref, g_ref, m_ref, v_ref, *rest):
        go_ref, d_ref, nm_ref, nv_ref = rest[4:]
        g = g_ref[...]
        go_ref[...] = g
        d_ref[...], nm_ref[...], nv_ref[...] = _adamw_math(w_ref[...], g, m_ref[...], v_ref[...])

    blk = pl.BlockSpec((None, tr, c), lambda i: (l, i, 0))
    return pl.pallas_call(body, grid=(r // tr,), in_specs=[blk, pl.BlockSpec((tr, c), lambda i: (i, 0)), blk, blk] + [ANY] * 4,
                          out_specs=[blk] * 4, out_shape=[SDS(w.shape, F32)] * 4, input_output_aliases={4 + k: k for k in range(4)},
                          name=name)(w, g, m, v, *bufs)


def _small_update(parts, w, m, v):
    def body(p_ref, w_ref, m_ref, v_ref, g_ref, d_ref, nm_ref, nv_ref):
        g = p_ref[0]
        for d in range(1, N_DEV):
            g = g + p_ref[d]
        g_ref[...] = g
        d_ref[...], nm_ref[...], nv_ref[...] = _adamw_math(w_ref[...], g, m_ref[...], v_ref[...])

    return pl.pallas_call(body, out_shape=[SDS(w.shape, F32)] * 4, name="small_update")(parts, w, m, v)


SMALL = ["norm_mix_g", "norm_mem_g", "ret_decay_fwd", "ret_decay_bwd", "ret_norm_g", "pool_w", "pool_scale", "na_q_norm_g",
         "na_k_norm_g", "na_rpb", "mem_q_norm_g", "mem_k_norm_g", "norm_ffn_g"]
BIG = ["w_in", "w_gate", "w_mem_kv", "w_branch", "w_out", "w_ffn_in", "w_ffn_out"]
GROUPS = (("w_in", "w_gate", "w_mem_kv", "w_branch", "w_out"), ("w_ffn_in", "w_ffn_out"))
ORDER = ["norm_mix_g", "norm_mem_g", "w_in", "w_gate", "ret_decay_fwd", "ret_decay_bwd", "ret_norm_g", "pool_w", "pool_scale",
         "na_q_norm_g", "na_k_norm_g", "na_rpb", "mem_q_norm_g", "mem_k_norm_g", "w_mem_kv", "w_branch", "w_out", "norm_ffn_g",
         "w_ffn_in", "w_ffn_out"]


def _pack(arrs):
    rows = []
    for a in arrs:
        f = a.reshape(-1).astype(F32)
        pad = (-f.shape[0]) % (8 * LANES)
        rows.append(jnp.pad(f, (0, pad)).reshape(-1, LANES))
    return jnp.concatenate(rows, axis=0)


def _unpack(p, like):
    out, at = [], 0
    for a in like:
        n = int(np.prod(a.shape))
        rows = -(-n // (8 * LANES)) * 8
        out.append(p[at:at + rows].reshape(-1)[:n].reshape(a.shape))
        at += rows
    return out


def _rope_tables(t_len):
    half = HEAD // 2
    inv = ROPE_THETA ** (-jnp.arange(half, dtype=F32) / half)
    ang = jnp.arange(t_len, dtype=F32)[:, None] * inv[None, :]
    cos, sin = jnp.cos(ang), jnp.sin(ang)
    return jnp.tile(jnp.concatenate([cos, cos], axis=1), (1, NH)), jnp.tile(jnp.concatenate([-sin, sin], axis=1), (1, NH))


def _block_diag(pw):
    out = jnp.zeros((BW, BW), pw.dtype)
    for g in range(NH):
        out = out.at[g * HEAD:(g + 1) * HEAD, g * HEAD:(g + 1) * HEAD].set(pw[g])
    return out


def _fold_heads(v):
    return v.reshape(NH, HEAD).sum(axis=0)


def _fold_lanes(v):
    return v.reshape(NH, HEAD).sum(axis=1)


def kernel(x, mem, norm_mix_g, norm_mem_g, w_in, w_gate, ret_decay_fwd, ret_decay_bwd, ret_norm_g, pool_w, pool_scale, na_q_norm_g, na_k_norm_g, na_rpb, mem_q_norm_g, mem_k_norm_g, w_mem_kv, w_branch, w_out, norm_ffn_g, w_ffn_in, w_ffn_out, loss_target, m_norm_mix_g, m_norm_mem_g, m_w_in, m_w_gate, m_ret_decay_fwd, m_ret_decay_bwd, m_ret_norm_g, m_pool_w, m_pool_scale, m_na_q_norm_g, m_na_k_norm_g, m_na_rpb, m_mem_q_norm_g, m_mem_k_norm_g, m_w_mem_kv, m_w_branch, m_w_out, m_norm_ffn_g, m_w_ffn_in, m_w_ffn_out, v_norm_mix_g, v_norm_mem_g, v_w_in, v_w_gate, v_ret_decay_fwd, v_ret_decay_bwd, v_ret_norm_g, v_pool_w, v_pool_scale, v_na_q_norm_g, v_na_k_norm_g, v_na_rpb, v_mem_q_norm_g, v_mem_k_norm_g, v_w_mem_kv, v_w_branch, v_w_out, v_norm_ffn_g, v_w_ffn_in, v_w_ffn_out):
    args = dict(locals())
    W = {n: args[n] for n in ORDER}
    M = {n: args["m_" + n] for n in ORDER}
    V = {n: args["v_" + n] for n in ORDER}
    n_layers, d_model = norm_mix_g.shape
    assert x.shape[0] == 1
    t_len = x.shape[1]
    xc, yc, cc = lax.axis_index("x"), lax.axis_index("y"), lax.axis_index("c")
    me_chip = 2 * xc + yc

    def as3(a):
        return a.reshape(a.shape[0], -1, a.shape[-1])

    gather_waits, token = {}, jnp.zeros((), F32)
    for l in range(n_layers):
        for gi, names in enumerate(GROUPS):
            srcs = [as3(W[n])[l].astype(BF16) for n in names]
            lands = [lax.empty((N_CHIPS,) + a.shape, BF16) for a in srcs]
            tok, gather_waits[l, gi] = _split_copies(f"gather_l{l}_g{gi}", srcs, lands, 4 * len(names), _gather_plan)
            token = token + tok[0, 0]

    cos, sin = _rope_tables(t_len)
    btabs = _na_bias_table(na_rpb)
    lane = lambda a: a.reshape(1, -1).astype(F32)
    x2d, mem2d, tgt = x[0], mem[0], loss_target[0]
    norm_mix_g = norm_mix_g + token

    saved = []
    for l in range(n_layers):
        tag = f"l{l}"
        gathered = dict(zip(GROUPS[0], _forward_halves(f"gather_pass_{tag}_g0", gather_waits[l, 0](x2d))))
        wi = gathered["w_in"].transpose(1, 0, 2).reshape(1, 1, d_model, -1)
        wg = gathered["w_gate"][None]
        wmkv = gathered["w_mem_kv"].reshape(1, 1, d_model, 2 * BW)
        wb = gathered["w_branch"].reshape(1, N_CHIPS, NH, BW, BW)
        wo = gathered["w_out"].reshape(1, 1, d_model, d_model)
        s = {"x": x2d}
        s["decf"], s["decb"] = lane(jnp.repeat(ret_decay_fwd[l], HEAD)), lane(jnp.repeat(ret_decay_bwd[l], HEAD))
        s["gn"], s["pscale"] = lane(ret_norm_g[l]), lane(pool_scale[l])
        s["wbd"] = _block_diag(pool_w[l])
        s["nagq"], s["nagk"] = lane(jnp.tile(na_q_norm_g[l], NH)), lane(jnp.tile(na_k_norm_g[l], NH))
        s["mgq"], s["mgk"] = lane(jnp.tile(mem_q_norm_g[l], NH)), lane(jnp.tile(mem_k_norm_g[l], NH))
        s["btab"] = btabs[l]
        s["h"] = _rms_fwd(f"rms_mix_{tag}", x2d, lane(norm_mix_g[l]))
        s["proj"] = _mm_nn(f"mm_in_{tag}", s["h"], wi, 0)
        s["gp3"] = _mm_nn(f"mm_gate_{tag}", s["h"], wg, 0, out3d=True)
        s["hm"] = _rms_fwd(f"rms_mem_{tag}", mem2d, lane(norm_mem_g[l]))
        s["memkv"] = _mm_nn(f"mm_memkv_{tag}", s["hm"], wmkv, 0)
        ret, s["ret_saved"] = _ret_fwd(tag, s["proj"], cos, sin, s["decf"], s["decb"], s["gn"])
        s["pvp"] = jnp.pad(s["proj"][:, 4 * BW:5 * BW], ((POOL_PAD, POOL_PAD), (0, 0)))
        pool = _pool_fwd(tag, s["pvp"], s["wbd"], s["pscale"], t_len)[POOL_PAD:POOL_PAD + t_len]
        s["kn"], s["vb"] = _na_keys(tag, s["proj"], s["nagk"])
        na = _na_fwd(tag, s["proj"], s["kn"], s["vb"], s["btab"], s["nagq"])
        mo = _mem_fwd(tag, s["proj"], s["memkv"], s["mgq"], s["mgk"])
        s["br"] = jnp.stack([ret, pool, na, mo])
        s["merged"] = _gate_fwd(f"gate_{tag}", s["gp3"], s["br"], wb, 0)
        s["x2"] = _mm_nn(f"mm_out_{tag}", s["merged"], wo, 0, add=x2d)
        gathered = dict(zip(GROUPS[1], _forward_halves(f"gather_pass_{tag}_g1", gather_waits[l, 1](s["x2"]))))
        wfi = gathered["w_ffn_in"][None]
        wfo = gathered["w_ffn_out"].reshape(1, 1, -1, d_model)
        s["w"] = (wi, wg, wmkv, wb, wo, wfi, wfo)
        s["h2"] = _rms_fwd(f"rms_ffn_{tag}", s["x2"], lane(norm_ffn_g[l]))
        s["ag3"] = _mm_nn(f"mm_ffn_in_{tag}", s["h2"], wfi, 0, out3d=True)
        s["u"] = _swiglu_fwd(f"swiglu_{tag}", s["ag3"])
        x2d = _mm_nn(f"mm_ffn_out_{tag}", s["u"], wfo, 0, add=s["x2"])
        saved.append(s)

    loss_part, dx = _loss_kernel("loss", x2d, tgt)

    dbtabs = [None] * n_layers
    gsmall = {n: [None] * n_layers for n in SMALL}
    opt_view = {n: (lambda a, n=n: jnp.swapaxes(as3(a), 1, 2) if n == "w_in" else as3(a)) for n in BIG}
    out_bufs = {n: tuple(lax.empty(opt_view[n](W[n]).shape, F32) for _ in range(4)) for n in BIG}

    def start_reduce(l, gi, grads):
        names = GROUPS[gi]
        gs = [grads[n] for n in names]
        r1 = _pair_exchange(f"grad_pair_exchange_l{l}_g{gi}", gs)
        sums = [_pair_sum(f"pair_sum_{n}_l{l}", g[None], r[None], cc, me_chip) for n, g, r in zip(names, gs, r1)]
        pbs = [p[0] for p, _ in sums]
        lands = [lax.empty((3,) + p.shape[1:], BF16) for p in pbs]
        tok, wait = _split_copies(f"scatter_l{l}_g{gi}", pbs, lands, 3 * len(names), _scatter_plan)
        return (l, gi, sums, wait), tok

    def finish_reduce(pending, after):
        l, gi, sums, wait = pending
        names = GROUPS[gi]
        r2 = wait(after)
        halves = [_chip_sum(f"chip_sum_{n}_l{l}", own, r[:, None], cc)[0] for n, (_, own), r in zip(names, sums, r2)]
        for n, f in zip(names, _half_exchange(f"grad_half_exchange_l{l}_g{gi}", halves)):
            out_bufs[n] = _adamw(f"adamw_{n}_l{l}", l, opt_view[n](W[n]), f, opt_view[n](M[n]), opt_view[n](V[n]), out_bufs[n])

    pending, order_after = [], None
    for l in reversed(range(n_layers)):
        tag = f"l{l}"
        s = saved[l]
        wi, wg, wmkv, wb, wo, wfi, wfo = s["w"]
        du = _mm_nt(f"mm_ffn_out_dx_{tag}", dx, wfo, 0, after=order_after)
        g_wfo = _mm_tn(f"mm_ffn_out_dw_{tag}", s["u"], dx, 1)
        dag3 = _swiglu_bwd(f"swiglu_bwd_{tag}", s["ag3"], du)
        dh2 = _mm_nt(f"mm_ffn_in_dx_{tag}", dag3, wfi, 0)
        g_wfi = _mm_tn(f"mm_ffn_in_dw_{tag}", s["h2"], dag3, N_CHIPS)
        dx2, dg = _rms_bwd(f"rms_ffn_bwd_{tag}", s["x2"], lane(norm_ffn_g[l]), [dh2], add=dx)
        gsmall["norm_ffn_g"][l] = dg[0]
        pend_ffn, tok = start_reduce(l, 1, {"w_ffn_in": g_wfi, "w_ffn_out": g_wfo.reshape(N_CHIPS, -1, d_model)})
        dmerged = _mm_nt(f"mm_out_dx_{tag}", dx2, wo, 0, after=tok)
        g_wo = _mm_tn(f"mm_out_dw_{tag}", s["merged"], dx2, 1)
        dgp3, dbr, g_wb = _gate_bwd(f"gate_bwd_{tag}", s["gp3"], s["br"], wb, 0, dmerged)
        g_wg = _mm_tn(f"mm_gate_dw_{tag}", s["h"], dgp3, N_CHIPS)
        dh_a = _mm_nt(f"mm_gate_dx_{tag}", dgp3, wg, 0)
        drq, drk, drv, drg, ddf, ddb, dgn = _ret_bwd(tag, s["proj"], cos, sin, s["decf"], s["decb"], s["gn"], s["ret_saved"], dbr[0])
        gsmall["ret_decay_fwd"][l], gsmall["ret_decay_bwd"][l], gsmall["ret_norm_g"][l] = _fold_lanes(ddf), _fold_lanes(ddb), dgn[0]
        dpool_p = jnp.pad(dbr[1], ((POOL_PAD, POOL_PAD), (0, 0)))
        dpvp, dwbd, dps = _pool_bwd(tag, s["pvp"], s["wbd"], s["pscale"], t_len, dpool_p)
        dpv = dpvp[POOL_PAD:POOL_PAD + t_len]
        gsmall["pool_w"][l] = jnp.stack([dwbd[g * HEAD:(g + 1) * HEAD, g * HEAD:(g + 1) * HEAD] for g in range(NH)])
        gsmall["pool_scale"][l] = dps[0]
        dnq, dkn, dnv, dbtabs[l], dgq = _na_bwd(tag, s["proj"], s["kn"], s["vb"], s["btab"], s["nagq"], dbr[2])
        dnk, dnv, dgk = _na_keys_bwd(tag, s["proj"], s["nagk"], dkn, dnv)
        gsmall["na_q_norm_g"][l], gsmall["na_k_norm_g"][l] = _fold_heads(dgq), _fold_heads(dgk)
        dmq, dmemkv, dgq, dgk = _mem_bwd(tag, s["proj"], s["memkv"], s["mgq"], s["mgk"], dbr[3])
        gsmall["mem_q_norm_g"][l], gsmall["mem_k_norm_g"][l] = _fold_heads(dgq), _fold_heads(dgk)
        g_wmkv = _mm_tn(f"mm_memkv_dw_{tag}", s["hm"], dmemkv, 1)
        dhm = _mm_nt(f"mm_memkv_dx_{tag}", dmemkv, wmkv, 0)
        _, dg = _rms_bwd(f"rms_mem_bwd_{tag}", mem2d, lane(norm_mem_g[l]), [dhm])
        gsmall["norm_mem_g"][l] = dg[0]
        dproj = jnp.concatenate([drq, drk, drv, drg, dpv, dnq, dnk, dnv, dmq], axis=1)
        g_wi = _mm_tn(f"mm_in_dw_{tag}", dproj, s["h"], 1)
        dh_b = _mm_nt(f"mm_in_dx_{tag}", dproj, wi, 0)
        dx, dg = _rms_bwd(f"rms_mix_bwd_{tag}", s["x"], lane(norm_mix_g[l]), [dh_a, dh_b], add=dx2)
        gsmall["norm_mix_g"][l] = dg[0]

        pend_mix, tok = start_reduce(l, 0, {
            "w_in": g_wi.reshape(N_CHIPS, -1, d_model), "w_gate": g_wg, "w_mem_kv": g_wmkv.reshape(N_CHIPS, -1, 2 * BW),
            "w_branch": g_wb.reshape(N_CHIPS, NH * BW, BW), "w_out": g_wo.reshape(N_CHIPS, -1, d_model)})
        for p in pending:
            finish_reduce(p, tok)
        pending, order_after = [pend_ffn, pend_mix], tok
    for p in pending:
        finish_reduce(p, dx)

    out_g, out_d, out_m, out_v = {}, {}, {}, {}
    for n in BIG:
        shp = W[n].shape
        back = (lambda b: jnp.swapaxes(b, 1, 2)) if n == "w_in" else (lambda b: b)
        out_g[n], out_d[n], out_m[n], out_v[n] = [back(b).reshape(shp) for b in out_bufs[n]]

    gsmall["na_rpb"] = list(_na_bias_grad(jnp.stack(dbtabs)))
    small_g = [jnp.stack(gsmall[n]).reshape(W[n].shape) for n in SMALL] + [loss_part]
    like = [W[n] for n in SMALL] + [loss_part]
    zero = jnp.zeros((1, 1), F32)
    parts = _gather_small(_pack(small_g))
    sg, sd, sm, sv = _small_update(parts, _pack([W[n] for n in SMALL] + [zero]), _pack([M[n] for n in SMALL] + [zero]),
                                   _pack([V[n] for n in SMALL] + [zero]))
    sg, sd, sm, sv = _unpack(sg, like), _unpack(sd, like), _unpack(sm, like), _unpack(sv, like)
    for i, n in enumerate(SMALL):
        out_g[n], out_d[n], out_m[n], out_v[n] = sg[i], sd[i], sm[i], sv[i]
    loss = sg[-1].reshape(())

    return (loss, dx.reshape(x.shape), *[out_g[n] for n in ORDER], *[out_d[n] for n in ORDER],
            *[out_m[n] for n in ORDER], *[out_v[n] for n in ORDER])
```

```python
import functools

import numpy as np
import jax
import jax.numpy as jnp
from jax import lax
from jax.experimental import pallas as pl
from jax.experimental.pallas import tpu as pltpu

F32 = jnp.float32
BF16 = jnp.bfloat16
SDS = jax.ShapeDtypeStruct
MESH = pl.DeviceIdType.MESH
ANY = pl.BlockSpec(memory_space=pl.ANY)

HEAD = 64
NH = 4
BW = NH * HEAD
CH = 128
GRID_W = 64
NA_ROWS = 8
NA_COLS = 16
POOL_HALF = (1, 2, 4, 8)
POOL_PAD = 16
ROPE_THETA = 10000.0
EPS = 1e-6
NEG_INF = -1e30
N_CHIPS = 4
N_DEV = 8
LANES = 128

ADAM_LR = 0.001
ADAM_B1 = 0.9
ADAM_B2 = 0.999
ADAM_EPS = 1e-08
ADAM_WD = 0.01
ADAM_STEP = 10

VMEM_BIG = 56 << 20
VMEM_MM = 56 << 20


def _cp(vmem=None):
    return pltpu.CompilerParams(vmem_limit_bytes=vmem) if vmem else None


def _d(a, b, ca, cb):
    return lax.dot_general(a.astype(BF16), b.astype(BF16), (((ca,), (cb,)), ((), ())),
                           preferred_element_type=F32)


@jax.custom_vjp
def _nn(a, b):
    return _d(a, b, 1, 0)


def _nn_f(a, b):
    return _d(a, b, 1, 0), (a, b)


def _nn_b(r, g):
    a, b = r
    return _d(g, b, 1, 1).astype(a.dtype), _d(a, g, 0, 0).astype(b.dtype)


_nn.defvjp(_nn_f, _nn_b)


@jax.custom_vjp
def _nt(a, b):
    return _d(a, b, 1, 1)


def _nt_f(a, b):
    return _d(a, b, 1, 1), (a, b)


def _nt_b(r, g):
    a, b = r
    return _d(g, b, 1, 0).astype(a.dtype), _d(g, a, 0, 0).astype(b.dtype)


_nt.defvjp(_nt_f, _nt_b)


@jax.custom_vjp
def _tn(a, b):
    return _d(a, b, 0, 0)


def _tn_f(a, b):
    return _d(a, b, 0, 0), (a, b)


def _tn_b(r, g):
    a, b = r
    return _d(b, g, 1, 1).astype(a.dtype), _d(a, g, 1, 0).astype(b.dtype)


_tn.defvjp(_tn_f, _tn_b)


@functools.partial(jax.custom_vjp, nondiff_argnums=(1,))
def _rollr(x, s):
    return pltpu.roll(x, s % x.shape[0], 0)


def _rollr_f(x, s):
    return _rollr(x, s), None


def _rollr_b(s, _, g):
    return (_rollr(g, -s),)


_rollr.defvjp(_rollr_f, _rollr_b)


@jax.custom_vjp
def _swap32(t):
    n = t.shape[1]
    lane = lax.broadcasted_iota(jnp.int32, (1, n), 1)
    return jnp.where((lane & (HEAD // 2)) == 0, pltpu.roll(t, n - HEAD // 2, 1), pltpu.roll(t, HEAD // 2, 1))


def _swap32_f(t):
    return _swap32(t), None


def _swap32_b(_, g):
    return (_swap32(g),)


_swap32.defvjp(_swap32_f, _swap32_b)


def _head_masks():
    lane = lax.broadcasted_iota(jnp.int32, (1, BW), 1)
    return [(lane >= HEAD * h) & (lane < HEAD * (h + 1)) for h in range(NH)]


def _head_rms(x, g):
    out = jnp.zeros_like(x)
    for mh in _head_masks():
        ms = jnp.sum(jnp.where(mh, x * x, 0.0), axis=-1, keepdims=True) * (1.0 / HEAD)
        out = out + jnp.where(mh, x * lax.rsqrt(ms + EPS), 0.0)
    return out * g


def _rms(x, g):
    return x * lax.rsqrt(jnp.mean(x * x, axis=-1, keepdims=True) + EPS) * g


def _rot(t, cos, sin):
    return t * cos + _swap32(t) * sin


def _softmax(s):
    e = jnp.exp(s - lax.stop_gradient(jnp.max(s, axis=-1, keepdims=True)))
    return e / jnp.sum(e, axis=-1, keepdims=True)


def _ret_kv_f(k, v, cos, sin, decf, decb):
    lgf, lgb = jax.nn.log_sigmoid(decf), jax.nn.log_sigmoid(decb)
    kr = _rot(k, cos, sin)
    idx = lax.broadcasted_iota(jnp.int32, (CH, 1), 0).astype(F32)
    r = lax.broadcasted_iota(jnp.int32, (BW, BW), 0) // HEAD
    c = lax.broadcasted_iota(jnp.int32, (BW, BW), 1) // HEAD
    bd = r == c
    kvf = jnp.where(bd, _tn(kr * jnp.exp((CH - 1 - idx) * lgf), v), 0.0)
    kvb = jnp.where(bd, _tn(kr * jnp.exp(idx * lgb), v), 0.0)
    return kvf, kvb


def _ret_scan_f(kvf, kvb, decf, decb):
    n = len(kvf)
    cdf = jnp.exp(CH * jax.nn.log_sigmoid(decf))
    cdb = jnp.exp(CH * jax.nn.log_sigmoid(decb))
    z = jnp.zeros((BW, BW), F32)
    sf, st = [z], z
    for a in range(n - 1):
        st = cdf * st + kvf[a]
        sf.append(st)
    sb, st = [z], z
    for a in range(n - 1, 0, -1):
        st = cdb * st + kvb[a]
        sb.append(st)
    return sf, sb[::-1]


def _ret_out_f(q, k, v, g, cos, sin, stf, stb, decf, decb, gn):
    lgf, lgb = jax.nn.log_sigmoid(decf), jax.nn.log_sigmoid(decb)
    qr = _rot(q, cos, sin) * (HEAD ** -0.5)
    kr = _rot(k, cos, sin)
    diff = (lax.broadcasted_iota(jnp.int32, (CH, CH), 0) - lax.broadcasted_iota(jnp.int32, (CH, CH), 1)).astype(F32)
    o = jnp.zeros_like(q)
    for mh in _head_masks():
        lf = jnp.sum(jnp.where(mh, lgf, 0.0), axis=1, keepdims=True) * (1.0 / HEAD)
        lb = jnp.sum(jnp.where(mh, lgb, 0.0), axis=1, keepdims=True) * (1.0 / HEAD)
        s = _nt(jnp.where(mh, qr, 0.0), kr)
        p = s * jnp.exp(jnp.where(diff >= 0, diff * lf, -diff * lb))
        o = o + jnp.where(mh, _nn(p, v), 0.0)
    idx = lax.broadcasted_iota(jnp.int32, (CH, 1), 0).astype(F32)
    o = o + _nn(qr * jnp.exp((idx + 1) * lgf), stf) + _nn(qr * jnp.exp((CH - idx) * lgb), stb)
    return _head_rms(o, gn) * (g * jax.nn.sigmoid(g))


def _pool_f(pvp, wbd, scale, t_len):
    n = pvp.shape[0]
    w2 = pvp + _rollr(pvp, 1)
    w4 = _rollr(w2, 1) + _rollr(w2, -1)
    w8 = _rollr(w4, 2) + _rollr(w4, -2)
    w16 = _rollr(w8, 4) + _rollr(w8, -4)
    grp = lax.broadcasted_iota(jnp.int32, (1, BW), 1) // HEAD
    ws = jnp.where(grp == 0, w2, jnp.where(grp == 1, w4, jnp.where(grp == 2, w8, w16)))
    half = jnp.where(grp == 0, POOL_HALF[0], jnp.where(grp == 1, POOL_HALF[1], jnp.where(grp == 2, POOL_HALF[2], POOL_HALF[3])))
    t = lax.broadcasted_iota(jnp.int32, (n, 1), 0) - POOL_PAD
    cnt = jnp.minimum(t + half, t_len) - jnp.maximum(t - half, 0)
    cnt = jnp.where((t >= 0) & (t < t_len), cnt, 1).astype(F32)
    pooled = ws / cnt - pvp
    return _nn(pooled, wbd) * scale


def _na_f(q, kn, vw, bias, gq):
    qn = _head_rms(q, gq)
    o = jnp.zeros_like(q)
    for h, mh in enumerate(_head_masks()):
        s = _nt(jnp.where(mh, qn, 0.0), kn) * (HEAD ** -0.5) + jnp.concatenate(bias[h], axis=1)
        o = o + jnp.where(mh, _nn(_softmax(s), vw), 0.0)
    return o


def _mem_f(q, mk, mv, gq, gk):
    qn = _head_rms(q, gq)
    kn = _head_rms(mk, gk)
    o = jnp.zeros_like(q)
    for mh in _head_masks():
        s = _nt(jnp.where(mh, qn, 0.0), kn) * (HEAD ** -0.5)
        o = o + jnp.where(mh, _nn(_softmax(s), mv), 0.0)
    return o


def _gate_f(gp, br, wb):
    out = None
    for n in range(NH):
        t = jax.nn.sigmoid(gp[n]) * _nn(br[n], wb[n])
        out = t if out is None else out + t
    return out


def _swiglu_f(a, g):
    return a * jax.nn.sigmoid(a) * g


MM_ROWS = (2048, 1024, 512, 256)
MM_COLS = (1152, 1024, 768, 512)


def _tile(n, prefs):
    for p in prefs:
        if n % p == 0:
            return p
    return n


def _mm_call(name, a, b, extra, dn, grid, a_spec, b_spec, extra_specs, o_spec, out_shape, acc_shape, nred, after=None):
    add = len(extra) == 1
    if after is not None:
        extra, extra_specs = [*extra, after], [*extra_specs, ANY]
    n_extra = len(extra)
    n_steps = int(np.prod([grid[len(grid) - 1 - i] for i in range(nred)]))

    def body(*refs):
        a_ref, b_ref = refs[0], refs[1]
        o_ref, acc = refs[2 + n_extra], refs[-1]
        red = [len(grid) - 1 - i for i in range(nred)]
        first = functools.reduce(jnp.logical_and, [pl.program_id(ax) == 0 for ax in red])
        last = functools.reduce(jnp.logical_and, [pl.program_id(ax) == grid[ax] - 1 for ax in red])
        d = lax.dot_general(a_ref[...].astype(BF16), b_ref[...].astype(BF16), dn, preferred_element_type=F32)

        def finish(r):
            if add:
                r = r + refs[2][...]
            o_ref[...] = r.astype(o_ref.dtype)

        if n_steps == 1:
            finish(d)
        else:
            @pl.when(first)
            def _():
                acc[...] = d

            @pl.when(jnp.logical_not(first) & jnp.logical_not(last))
            def _():
                acc[...] += d

            @pl.when(last)
            def _():
                finish(acc[...] + d)

    return pl.pallas_call(
        body, grid=grid, in_specs=[a_spec, b_spec, *extra_specs], out_specs=o_spec, out_shape=out_shape,
        scratch_shapes=[pltpu.VMEM(acc_shape, F32)] if n_steps > 1 else [], name=name, compiler_params=_cp(VMEM_MM),
    )(a, b, *extra)


def _mm_nn(name, a, w, l, *, out3d=False, add=None, out_dtype=F32):
    m, k = a.shape
    _, s_n, _, ns = w.shape
    tm = _tile(m, MM_ROWS)
    tk = _tile(k, (1024, 1408, 512, 256))
    tn = ns if ns <= 1408 else _tile(ns, MM_COLS)
    grid = (m // tm, s_n, ns // tn, k // tk)
    a_spec = pl.BlockSpec((tm, tk), lambda i, s, j, kk: (i, kk))
    b_spec = pl.BlockSpec((None, None, tk, tn), lambda i, s, j, kk: (l, s, kk, j))
    if out3d:
        o_spec = pl.BlockSpec((None, tm, tn), lambda i, s, j, kk: (s, i, j))
        out_shape = SDS((s_n, m, ns), out_dtype)
    else:
        nj = ns // tn
        o_spec = pl.BlockSpec((tm, tn), lambda i, s, j, kk: (i, s * nj + j))
        out_shape = SDS((m, s_n * ns), out_dtype)
    extra, especs = [], []
    if add is not None:
        assert not out3d
        nj = ns // tn
        extra, especs = [add], [pl.BlockSpec((tm, tn), lambda i, s, j, kk: (i, s * nj + j))]
    return _mm_call(name, a, w, extra, (((1,), (0,)), ((), ())), grid, a_spec, b_spec, especs, o_spec, out_shape, (tm, tn), 1)


def _mm_nt(name, g, w, l, after=None):
    _, s_n, k, ns = w.shape
    g3d = g.ndim == 3
    m = g.shape[1] if g3d else g.shape[0]
    tm = _tile(m, MM_ROWS)
    to = _tile(k, (1024, 1408, 512, 256))
    tk = ns if (ns % LANES or ns <= 1408) else _tile(ns, MM_COLS)
    nks = ns // tk
    grid = (m // tm, k // to, s_n, nks)
    if g3d:
        a_spec = pl.BlockSpec((None, tm, tk), lambda i, j, s, r: (s, i, r))
    else:
        a_spec = pl.BlockSpec((tm, tk), lambda i, j, s, r: (i, s * nks + r))
    b_spec = pl.BlockSpec((None, None, to, tk), lambda i, j, s, r: (l, s, j, r))
    o_spec = pl.BlockSpec((tm, to), lambda i, j, s, r: (i, j))
    return _mm_call(name, g, w, [], (((1,), (1,)), ((), ())), grid, a_spec, b_spec, [], o_spec, SDS((m, k), F32), (tm, to), 2,
                    after=after)


def _mm_tn(name, a, g, s_n):
    t_len, k = a.shape
    g3d = g.ndim == 3
    ns = g.shape[2] if g3d else g.shape[1] // s_n
    tt = _tile(t_len, (1024, 512))
    to = _tile(k, (1024, 768, 1408, 512, 256, 128))
    tn = ns
    nj = ns // tn
    grid = (s_n, k // to, nj, t_len // tt)
    a_spec = pl.BlockSpec((tt, to), lambda s, i, j, t: (t, i))
    if g3d:
        b_spec = pl.BlockSpec((None, tt, tn), lambda s, i, j, t: (s, t, j))
    else:
        b_spec = pl.BlockSpec((tt, tn), lambda s, i, j, t: (t, s * nj + j))
    o_spec = pl.BlockSpec((None, to, tn), lambda s, i, j, t: (s, i, j))
    return _mm_call(name, a, g, [], (((0,), (0,)), ((), ())), grid, a_spec, b_spec, [], o_spec,
                    SDS((s_n, k, ns), F32), (to, tn), 1)


def _rms_fwd(name, x, g):
    m, d = x.shape
    tm = _tile(m, (256,))

    def body(x_ref, g_ref, o_ref):
        o_ref[...] = _rms(x_ref[...], g_ref[...]).astype(BF16)

    row = pl.BlockSpec((tm, d), lambda i: (i, 0))
    return pl.pallas_call(body, grid=(m // tm,), in_specs=[row, pl.BlockSpec((1, d), lambda i: (0, 0))], out_specs=row,
                          out_shape=SDS((m, d), BF16), name=name)(x, g)


def _rms_bwd(name, x, g, dhs, add=None):
    m, d = x.shape
    tm = _tile(m, (256,))
    nh = len(dhs)

    def body(*refs):
        x_ref, g_ref = refs[0], refs[1]
        dh = refs[2][...]
        for r in refs[3:2 + nh]:
            dh = dh + r[...]
        dx_ref, dg_ref = refs[-2], refs[-1]
        _, vjp = jax.vjp(_rms, x_ref[...], g_ref[...])
        dx, dg = vjp(dh)
        if add is not None:
            dx = dx + refs[2 + nh][...]
        dx_ref[...] = dx

        @pl.when(pl.program_id(0) == 0)
        def _():
            dg_ref[...] = jnp.zeros_like(dg_ref)

        dg_ref[...] += dg

    row = pl.BlockSpec((tm, d), lambda i: (i, 0))
    vec = pl.BlockSpec((1, d), lambda i: (0, 0))
    ins = [x, g, *dhs] + ([add] if add is not None else [])
    return pl.pallas_call(body, grid=(m // tm,), in_specs=[row, vec] + [row] * (len(ins) - 2), out_specs=[row, vec],
                          out_shape=[SDS((m, d), F32), SDS((1, d), F32)], name=name)(*ins)


def _swiglu_fwd(name, ag3):
    _, t_len, w = ag3.shape
    tm = _tile(t_len, (512, 256))

    def body(a_ref, g_ref, o_ref):
        o_ref[...] = _swiglu_f(a_ref[...], g_ref[...]).astype(BF16)

    return pl.pallas_call(
        body, grid=(t_len // tm, 2),
        in_specs=[pl.BlockSpec((None, tm, w), lambda i, j: (j, i, 0)), pl.BlockSpec((None, tm, w), lambda i, j: (j + 2, i, 0))],
        out_specs=pl.BlockSpec((tm, w), lambda i, j: (i, j)), out_shape=SDS((t_len, 2 * w), BF16), name=name)(ag3, ag3)


def _swiglu_bwd(name, ag3, du):
    _, t_len, w = ag3.shape
    tm = _tile(t_len, (512, 256))

    def body(a_ref, g_ref, du_ref, o_ref):
        _, vjp = jax.vjp(_swiglu_f, a_ref[...], g_ref[...])
        da, dg = vjp(du_ref[...])
        o_ref[0] = da.astype(BF16)
        o_ref[1] = dg.astype(BF16)

    out = pl.pallas_call(
        body, grid=(t_len // tm, 2),
        in_specs=[pl.BlockSpec((None, tm, w), lambda i, j: (j, i, 0)), pl.BlockSpec((None, tm, w), lambda i, j: (j + 2, i, 0)),
                  pl.BlockSpec((tm, w), lambda i, j: (i, j))],
        out_specs=pl.BlockSpec((2, None, tm, w), lambda i, j: (0, j, i, 0)), out_shape=SDS((2, 2, t_len, w), BF16), name=name)(ag3, ag3, du)
    return out.reshape(ag3.shape)


def _gate_fwd(name, gp3, br, wb, l):
    _, t_len, d = gp3.shape
    tm = _tile(t_len, (512, 256))

    def body(gp_ref, br_ref, wb_ref, o_ref):
        o_ref[...] = _gate_f([gp_ref[n] for n in range(NH)], [br_ref[n] for n in range(NH)],
                             [wb_ref[n] for n in range(NH)]).astype(BF16)

    return pl.pallas_call(
        body, grid=(t_len // tm, d // BW),
        in_specs=[pl.BlockSpec((NH, tm, BW), lambda i, s: (0, i, s)), pl.BlockSpec((NH, tm, BW), lambda i, s: (0, i, 0)),
                  pl.BlockSpec((None, None, NH, BW, BW), lambda i, s: (l, s, 0, 0, 0))],
        out_specs=pl.BlockSpec((tm, BW), lambda i, s: (i, s)), out_shape=SDS((t_len, d), BF16), name=name)(gp3, br, wb)


def _gate_bwd(name, gp3, br, wb, l, dmerged):
    _, t_len, d = gp3.shape
    tm = _tile(t_len, (512, 256))
    ns = d // BW

    def body(gp_ref, br_ref, wb_ref, dm_ref, dgp_ref, dbr_ref, dwb_ref):
        i, s = pl.program_id(0), pl.program_id(1)
        gp = [gp_ref[n] for n in range(NH)]
        brv = [br_ref[n].astype(F32) for n in range(NH)]
        wbv = [wb_ref[n].astype(F32) for n in range(NH)]
        _, vjp = jax.vjp(_gate_f, gp, brv, wbv)
        dgp, dbr, dwb = vjp(dm_ref[...])

        @pl.when(s == 0)
        def _():
            dbr_ref[...] = jnp.zeros_like(dbr_ref)

        @pl.when((i == 0) & (s == 0))
        def _():
            dwb_ref[...] = jnp.zeros_like(dwb_ref)

        for n in range(NH):
            dgp_ref[n] = dgp[n].astype(BF16)
            dbr_ref[n] += dbr[n]
            dwb_ref[s, n] += dwb[n]

    return pl.pallas_call(
        body, grid=(t_len // tm, ns),
        in_specs=[pl.BlockSpec((NH, tm, BW), lambda i, s: (0, i, s)), pl.BlockSpec((NH, tm, BW), lambda i, s: (0, i, 0)),
                  pl.BlockSpec((None, None, NH, BW, BW), lambda i, s: (l, s, 0, 0, 0)), pl.BlockSpec((tm, BW), lambda i, s: (i, s))],
        out_specs=[pl.BlockSpec((NH, tm, BW), lambda i, s: (0, i, s)), pl.BlockSpec((NH, tm, BW), lambda i, s: (0, i, 0)),
                   pl.BlockSpec((ns, NH, BW, BW), lambda i, s: (0, 0, 0, 0))],
        out_shape=[SDS(gp3.shape, BF16), SDS((NH, t_len, BW), F32), SDS((ns, NH, BW, BW), F32)], name=name,
        compiler_params=_cp(VMEM_MM))(gp3, br, wb, dmerged)


def _loss_kernel(name, y, tgt):
    m, d = y.shape
    tm = _tile(m, (256,))

    def body(y_ref, t_ref, l_ref, dy_ref):
        err = y_ref[...] - t_ref[...]
        dy_ref[...] = err * (1.0 / d)

        @pl.when(pl.program_id(0) == 0)
        def _():
            l_ref[...] = jnp.zeros_like(l_ref)

        l_ref[...] += 0.5 * jnp.sum(jnp.mean(err * err, axis=-1, keepdims=True), axis=0, keepdims=True)

    row = pl.BlockSpec((tm, d), lambda i: (i, 0))
    return pl.pallas_call(body, grid=(m // tm,), in_specs=[row, row], out_specs=[pl.BlockSpec((1, 1), lambda i: (0, 0)), row],
                          out_shape=[SDS((1, 1), F32), SDS((m, d), F32)], name=name)(y, tgt)


def _zero_first(refs):
    @pl.when(pl.program_id(0) == 0)
    def _():
        for r in refs:
            r[...] = jnp.zeros_like(r)


def _grp(rows, j):
    return pl.BlockSpec((rows, BW), lambda a, j=j: (a, j))


_VEC = pl.BlockSpec((1, BW), lambda a: (0, 0))


def _ret_fwd(tag, proj, cos, sin, decf, decb, gn):
    t_len = proj.shape[0]
    n = t_len // CH
    tab = pl.BlockSpec((CH, BW), lambda a: (a, 0))
    st = pl.BlockSpec((None, BW, BW), lambda a: (a, 0, 0))

    def k1(k_ref, v_ref, cos_ref, sin_ref, df_ref, db_ref, kvf_ref, kvb_ref):
        kvf_ref[...], kvb_ref[...] = _ret_kv_f(k_ref[...], v_ref[...], cos_ref[...], sin_ref[...], df_ref[...], db_ref[...])

    kvf, kvb = pl.pallas_call(
        k1, grid=(n,), in_specs=[_grp(CH, 1), _grp(CH, 2), tab, tab, _VEC, _VEC], out_specs=[st, st],
        out_shape=[SDS((n, BW, BW), F32)] * 2, name=f"ret_kv_{tag}")(proj, proj, cos, sin, decf, decb)

    def k2(kvf_ref, kvb_ref, df_ref, db_ref, sf_ref, sb_ref):
        sf, sb = _ret_scan_f([kvf_ref[a] for a in range(n)], [kvb_ref[a] for a in range(n)], df_ref[...], db_ref[...])
        for a in range(n):
            sf_ref[a] = sf[a]
            sb_ref[a] = sb[a]

    stf, stb = pl.pallas_call(k2, out_shape=[SDS((n, BW, BW), F32)] * 2, name=f"ret_scan_{tag}",
                              compiler_params=_cp(VMEM_BIG))(kvf, kvb, decf, decb)

    def k3(q_ref, k_ref, v_ref, g_ref, cos_ref, sin_ref, sf_ref, sb_ref, df_ref, db_ref, gn_ref, o_ref):
        o_ref[...] = _ret_out_f(q_ref[...], k_ref[...], v_ref[...], g_ref[...], cos_ref[...], sin_ref[...], sf_ref[...],
                                sb_ref[...], df_ref[...], db_ref[...], gn_ref[...]).astype(BF16)

    ret = pl.pallas_call(
        k3, grid=(n,), in_specs=[_grp(CH, 0), _grp(CH, 1), _grp(CH, 2), _grp(CH, 3), tab, tab, st, st, _VEC, _VEC, _VEC],
        out_specs=tab, out_shape=SDS((t_len, BW), BF16), name=f"ret_out_{tag}")(proj, proj, proj, proj, cos, sin, stf, stb, decf, decb, gn)
    return ret, (kvf, kvb, stf, stb)


def _ret_bwd(tag, proj, cos, sin, decf, decb, gn, saved, dret):
    kvf, kvb, stf, stb = saved
    t_len = proj.shape[0]
    n = t_len // CH
    tab = pl.BlockSpec((CH, BW), lambda a: (a, 0))
    st = pl.BlockSpec((None, BW, BW), lambda a: (a, 0, 0))

    def k3(q_ref, k_ref, v_ref, g_ref, cos_ref, sin_ref, sf_ref, sb_ref, df_ref, db_ref, gn_ref, do_ref,
           dq_ref, dk_ref, dv_ref, dg_ref, dsf_ref, dsb_ref, ddf_ref, ddb_ref, dgn_ref):
        cos, sin = cos_ref[...], sin_ref[...]
        f = lambda q, k, v, g, sf, sb, df, db, gnv: _ret_out_f(q, k, v, g, cos, sin, sf, sb, df, db, gnv)
        _, vjp = jax.vjp(f, q_ref[...], k_ref[...], v_ref[...], g_ref[...], sf_ref[...], sb_ref[...], df_ref[...], db_ref[...], gn_ref[...])
        dq, dk, dv, dg, dsf, dsb, ddf, ddb, dgn = vjp(do_ref[...])
        dq_ref[...] = dq.astype(BF16)
        dk_ref[...] = dk
        dv_ref[...] = dv
        dg_ref[...] = dg.astype(BF16)
        dsf_ref[...] = dsf
        dsb_ref[...] = dsb
        _zero_first([ddf_ref, ddb_ref, dgn_ref])
        ddf_ref[...] += ddf
        ddb_ref[...] += ddb
        dgn_ref[...] += dgn

    dq, dk3, dv3, dg, dstf, dstb, ddf3, ddb3, dgn = pl.pallas_call(
        k3, grid=(n,),
        in_specs=[_grp(CH, 0), _grp(CH, 1), _grp(CH, 2), _grp(CH, 3), tab, tab, st, st, _VEC, _VEC, _VEC, tab],
        out_specs=[tab, tab, tab, tab, st, st, _VEC, _VEC, _VEC],
        out_shape=[SDS((t_len, BW), BF16), SDS((t_len, BW), F32), SDS((t_len, BW), F32), SDS((t_len, BW), BF16),
                   SDS((n, BW, BW), F32), SDS((n, BW, BW), F32), SDS((1, BW), F32), SDS((1, BW), F32), SDS((1, BW), F32)],
        name=f"ret_out_bwd_{tag}")(proj, proj, proj, proj, cos, sin, stf, stb, decf, decb, gn, dret)

    def k2(kvf_ref, kvb_ref, df_ref, db_ref, dsf_ref, dsb_ref, dkvf_ref, dkvb_ref, ddf_ref, ddb_ref):
        _, vjp = jax.vjp(_ret_scan_f, [kvf_ref[a] for a in range(n)], [kvb_ref[a] for a in range(n)], df_ref[...], db_ref[...])
        dkvf, dkvb, ddf_ref[...], ddb_ref[...] = vjp(([dsf_ref[a] for a in range(n)], [dsb_ref[a] for a in range(n)]))
        for a in range(n):
            dkvf_ref[a] = dkvf[a]
            dkvb_ref[a] = dkvb[a]

    dkvf, dkvb, ddf2, ddb2 = pl.pallas_call(
        k2, out_shape=[SDS((n, BW, BW), F32), SDS((n, BW, BW), F32), SDS((1, BW), F32), SDS((1, BW), F32)],
        name=f"ret_scan_bwd_{tag}", compiler_params=_cp(VMEM_BIG))(kvf, kvb, decf, decb, dstf, dstb)

    def k1(k_ref, v_ref, cos_ref, sin_ref, df_ref, db_ref, dkvf_ref, dkvb_ref, dk3_ref, dv3_ref, dk_ref, dv_ref, ddf_ref, ddb_ref):
        cos, sin = cos_ref[...], sin_ref[...]
        f = lambda k, v, df, db: _ret_kv_f(k, v, cos, sin, df, db)
        _, vjp = jax.vjp(f, k_ref[...], v_ref[...], df_ref[...], db_ref[...])
        dk, dv, ddf, ddb = vjp((dkvf_ref[...], dkvb_ref[...]))
        dk_ref[...] = (dk + dk3_ref[...]).astype(BF16)
        dv_ref[...] = (dv + dv3_ref[...]).astype(BF16)
        _zero_first([ddf_ref, ddb_ref])
        ddf_ref[...] += ddf
        ddb_ref[...] += ddb

    dk, dv, ddf1, ddb1 = pl.pallas_call(
        k1, grid=(n,), in_specs=[_grp(CH, 1), _grp(CH, 2), tab, tab, _VEC, _VEC, st, st, tab, tab],
        out_specs=[tab, tab, _VEC, _VEC],
        out_shape=[SDS((t_len, BW), BF16), SDS((t_len, BW), BF16), SDS((1, BW), F32), SDS((1, BW), F32)],
        name=f"ret_kv_bwd_{tag}")(proj, proj, cos, sin, decf, decb, dkvf, dkvb, dk3, dv3)
    return dq, dk, dv, dg, ddf1 + ddf2 + ddf3, ddb1 + ddb2 + ddb3, dgn


def _pool_fwd(tag, pvp, wbd, scale, t_len):
    def body(p_ref, w_ref, s_ref, o_ref):
        o_ref[...] = _pool_f(p_ref[...], w_ref[...], s_ref[...], t_len).astype(BF16)

    return pl.pallas_call(body, out_shape=SDS(pvp.shape, BF16), name=f"pool_{tag}", compiler_params=_cp(VMEM_BIG))(pvp, wbd, scale)


def _pool_bwd(tag, pvp, wbd, scale, t_len, dpool_p):
    def body(p_ref, w_ref, s_ref, do_ref, dp_ref, dw_ref, ds_ref):
        f = lambda p, w, s: _pool_f(p, w, s, t_len)
        _, vjp = jax.vjp(f, p_ref[...], w_ref[...], s_ref[...])
        dp, dw, ds = vjp(do_ref[...])
        dp_ref[...] = dp.astype(BF16)
        dw_ref[...] = dw
        ds_ref[...] = ds

    return pl.pallas_call(body, out_shape=[SDS(pvp.shape, BF16), SDS((BW, BW), F32), SDS((1, BW), F32)],
                          name=f"pool_bwd_{tag}", compiler_params=_cp(VMEM_BIG))(pvp, wbd, scale, dpool_p)


def _na_window(r, rows):
    r0 = jnp.clip(r - NA_ROWS // 2, 0, rows - NA_ROWS)
    return r0, r - r0


NA_PAIRS = 2 * NA_ROWS - 2


def _na_bias_pieces(b_ref, pat):
    return [[b_ref[h * NA_PAIRS + 2 * k - pat + NA_ROWS - 1] for k in range(NA_ROWS // 2)] for h in range(NH)]


def _na_keys(tag, proj, gk):
    t_len = proj.shape[0]
    tm = _tile(t_len, (256,))

    def body(k_ref, v_ref, gk_ref, kn_ref, vb_ref):
        kn_ref[...] = _head_rms(k_ref[...], gk_ref[...]).astype(BF16)
        vb_ref[...] = v_ref[...].astype(BF16)

    row = pl.BlockSpec((tm, BW), lambda a: (a, 0))
    return pl.pallas_call(body, grid=(t_len // tm,), in_specs=[_grp(tm, 6), _grp(tm, 7), _VEC], out_specs=[row, row],
                          out_shape=[SDS((t_len, BW), BF16)] * 2, name=f"na_keys_{tag}")(proj, proj, gk)


def _na_keys_bwd(tag, proj, gk, dkn, dv):
    t_len = proj.shape[0]
    tm = _tile(t_len, (256,))

    def body(k_ref, gk_ref, dkn_ref, dv_ref, dk_ref, dvb_ref, dgk_ref):
        _, vjp = jax.vjp(_head_rms, k_ref[...], gk_ref[...])
        dk, dgk = vjp(dkn_ref[...])
        dk_ref[...] = dk.astype(BF16)
        dvb_ref[...] = dv_ref[...].astype(BF16)
        _zero_first([dgk_ref])
        dgk_ref[...] += dgk

    row = pl.BlockSpec((tm, BW), lambda a: (a, 0))
    return pl.pallas_call(body, grid=(t_len // tm,), in_specs=[_grp(tm, 6), _VEC, row, row], out_specs=[row, row, _VEC],
                          out_shape=[SDS((t_len, BW), BF16), SDS((t_len, BW), BF16), SDS((1, BW), F32)],
                          name=f"na_keys_bwd_{tag}")(proj, gk, dkn, dv)


def _na_fwd(tag, proj, kn, vb, bias_tab, gq):
    t_len = proj.shape[0]
    rows = t_len // GRID_W
    win = NA_ROWS * GRID_W

    def body(q_ref, k_ref, v_ref, b_ref, gq_ref, o_ref):
        r0, pat = _na_window(pl.program_id(0), rows)
        start = pl.multiple_of(r0 * GRID_W, GRID_W)
        o_ref[...] = _na_f(q_ref[...], k_ref[pl.ds(start, win), :], v_ref[pl.ds(start, win), :], _na_bias_pieces(b_ref, pat),
                           gq_ref[...]).astype(BF16)

    whole = pl.BlockSpec((t_len, BW), lambda a: (0, 0))
    return pl.pallas_call(
        body, grid=(rows,),
        in_specs=[_grp(GRID_W, 5), whole, whole, pl.BlockSpec(bias_tab.shape, lambda a: (0, 0, 0)), _VEC],
        out_specs=pl.BlockSpec((GRID_W, BW), lambda a: (a, 0)), out_shape=SDS((t_len, BW), BF16),
        name=f"na_{tag}", compiler_params=_cp(VMEM_MM))(proj, kn, vb, bias_tab, gq)


def _na_bwd(tag, proj, kn, vb, bias_tab, gq, dna):
    t_len = proj.shape[0]
    rows = t_len // GRID_W
    win = NA_ROWS * GRID_W

    def body(q_ref, k_ref, v_ref, b_ref, gq_ref, do_ref, dq_ref, dk_ref, dv_ref, db_ref, dgq_ref):
        r0, pat = _na_window(pl.program_id(0), rows)
        start = pl.multiple_of(r0 * GRID_W, GRID_W)
        _zero_first([dk_ref, dv_ref, db_ref, dgq_ref])
        _, vjp = jax.vjp(_na_f, q_ref[...], k_ref[pl.ds(start, win), :].astype(F32), v_ref[pl.ds(start, win), :].astype(F32),
                         _na_bias_pieces(b_ref, pat), gq_ref[...])
        dq, dk, dv, db, dgq = vjp(do_ref[...])
        dq_ref[...] = dq.astype(BF16)
        dk_ref[pl.ds(start, win), :] += dk
        dv_ref[pl.ds(start, win), :] += dv
        for h in range(NH):
            for k in range(NA_ROWS // 2):
                db_ref[h * NA_PAIRS + 2 * k - pat + NA_ROWS - 1] += db[h][k]
        dgq_ref[...] += dgq

    whole = pl.BlockSpec((t_len, BW), lambda a: (0, 0))
    tabspec = pl.BlockSpec(bias_tab.shape, lambda a: (0, 0, 0))
    row = pl.BlockSpec((GRID_W, BW), lambda a: (a, 0))
    return pl.pallas_call(
        body, grid=(rows,), in_specs=[_grp(GRID_W, 5), whole, whole, tabspec, _VEC, row],
        out_specs=[row, whole, whole, tabspec, _VEC],
        out_shape=[SDS((t_len, BW), BF16), SDS((t_len, BW), F32), SDS((t_len, BW), F32), SDS(bias_tab.shape, F32), SDS((1, BW), F32)],
        name=f"na_bwd_{tag}", compiler_params=_cp(VMEM_BIG))(proj, kn, vb, bias_tab, gq, dna)


def _mem_fwd(tag, proj, memkv, gq, gk):
    t_len = proj.shape[0]
    n_mem = memkv.shape[0]
    tm = _tile(t_len, (256,))

    def body(q_ref, mk_ref, mv_ref, gq_ref, gk_ref, o_ref):
        o_ref[...] = _mem_f(q_ref[...], mk_ref[...], mv_ref[...], gq_ref[...], gk_ref[...]).astype(BF16)

    mspec = lambda j: pl.BlockSpec((n_mem, BW), lambda a, j=j: (0, j))
    return pl.pallas_call(
        body, grid=(t_len // tm,), in_specs=[_grp(tm, 8), mspec(0), mspec(1), _VEC, _VEC],
        out_specs=pl.BlockSpec((tm, BW), lambda a: (a, 0)), out_shape=SDS((t_len, BW), BF16), name=f"mem_{tag}")(proj, memkv, memkv, gq, gk)


def _mem_bwd(tag, proj, memkv, gq, gk, dmo):
    t_len = proj.shape[0]
    n_mem = memkv.shape[0]
    tm = _tile(t_len, (256,))

    def body(q_ref, mk_ref, mv_ref, gq_ref, gk_ref, do_ref, dq_ref, dmk_ref, dmv_ref, dgq_ref, dgk_ref):
        _zero_first([dmk_ref, dmv_ref, dgq_ref, dgk_ref])
        _, vjp = jax.vjp(_mem_f, q_ref[...], mk_ref[...], mv_ref[...], gq_ref[...], gk_ref[...])
        dq, dmk, dmv, dgq, dgk = vjp(do_ref[...])
        dq_ref[...] = dq.astype(BF16)
        dmk_ref[...] += dmk
        dmv_ref[...] += dmv
        dgq_ref[...] += dgq
        dgk_ref[...] += dgk

    mspec = lambda j: pl.BlockSpec((n_mem, BW), lambda a, j=j: (0, j))
    mout = pl.BlockSpec((n_mem, BW), lambda a: (0, 0))
    row = pl.BlockSpec((tm, BW), lambda a: (a, 0))
    dq, dmk, dmv, dgq, dgk = pl.pallas_call(
        body, grid=(t_len // tm,), in_specs=[_grp(tm, 8), mspec(0), mspec(1), _VEC, _VEC, row],
        out_specs=[row, mout, mout, _VEC, _VEC],
        out_shape=[SDS((t_len, BW), BF16), SDS((n_mem, BW), F32), SDS((n_mem, BW), F32), SDS((1, BW), F32), SDS((1, BW), F32)],
        name=f"mem_bwd_{tag}")(proj, memkv, memkv, gq, gk, dmo)
    return dq, jnp.concatenate([dmk, dmv], axis=1), dgq, dgk


NA_NJ = 2 * NA_COLS - 1


def _na_onehot():
    q = np.arange(GRID_W)[:, None]
    kc = np.arange(GRID_W)[None, :]
    qwin = np.clip(q - NA_COLS // 2, 0, GRID_W - NA_COLS)
    mask = (kc >= qwin) & (kc < qwin + NA_COLS)
    col = np.clip(kc - q, -(NA_COLS - 1), NA_COLS - 1) + NA_COLS - 1
    onehot = np.zeros((LANES, GRID_W, 2, GRID_W), np.float32)
    qq, kk = np.nonzero(mask)
    for half in range(2):
        onehot[half * NA_NJ + col[qq, kk], qq, half, kk] = 1.0
    valid = np.broadcast_to(mask[:, None, :], (GRID_W, 2, GRID_W)).astype(np.float32)
    return onehot.reshape(LANES, -1), valid.reshape(1, -1)


def _na_pair_rows(rpb):
    n_layers = rpb.shape[0]
    pair = jnp.concatenate([rpb[:, :, :-1], rpb[:, :, 1:]], axis=-1).reshape(n_layers, NH * NA_PAIRS, 2 * NA_NJ)
    return jnp.pad(pair, ((0, 0), (0, 0), (0, LANES - 2 * NA_NJ)))


def _na_bias_table(rpb):
    n_layers = rpb.shape[0]
    onehot, valid = _na_onehot()
    width = onehot.shape[1]

    def body(r_ref, oh_ref, ok_ref, o_ref):
        t = jnp.dot(r_ref[...], oh_ref[...], precision=lax.Precision.HIGHEST, preferred_element_type=F32)
        o_ref[...] = jnp.where(ok_ref[...] > 0, t, NEG_INF)

    out = pl.pallas_call(
        body, grid=(n_layers,),
        in_specs=[pl.BlockSpec((None, NH * NA_PAIRS, LANES), lambda l: (l, 0, 0)), pl.BlockSpec((LANES, width), lambda l: (0, 0)),
                  pl.BlockSpec((1, width), lambda l: (0, 0))],
        out_specs=pl.BlockSpec((None, NH * NA_PAIRS, width), lambda l: (l, 0, 0)),
        out_shape=SDS((n_layers, NH * NA_PAIRS, width), F32), name="na_bias_table")(
            _na_pair_rows(rpb), jnp.asarray(onehot), jnp.asarray(valid))
    return out.reshape(n_layers, NH * NA_PAIRS, GRID_W, 2 * GRID_W)


def _na_bias_grad(dtab):
    n_layers = dtab.shape[0]
    onehot, _ = _na_onehot()
    width = onehot.shape[1]

    def body(x_ref, oh_ref, o_ref):
        o_ref[...] = jnp.dot(x_ref[...], oh_ref[...], precision=lax.Precision.HIGHEST, preferred_element_type=F32)

    out = pl.pallas_call(
        body, grid=(n_layers,),
        in_specs=[pl.BlockSpec((None, NH * NA_PAIRS, width), lambda l: (l, 0, 0)), pl.BlockSpec((width, LANES), lambda l: (0, 0))],
        out_specs=pl.BlockSpec((None, NH * NA_PAIRS, LANES), lambda l: (l, 0, 0)),
        out_shape=SDS((n_layers, NH * NA_PAIRS, LANES), F32), name="na_bias_grad")(
            dtab.reshape(n_layers, NH * NA_PAIRS, width), jnp.asarray(onehot.T.copy()))
    out = out.reshape(n_layers, NH, NA_PAIRS, LANES)
    zero = jnp.zeros((n_layers, NH, 1, NA_NJ), F32)
    return (jnp.concatenate([out[..., :NA_NJ], zero], axis=2) + jnp.concatenate([zero, out[..., NA_NJ:2 * NA_NJ]], axis=2))


def _place():
    x, y, c = lax.axis_index("x"), lax.axis_index("y"), lax.axis_index("c")
    chips = [(1 - x, y), (x, 1 - y), (1 - x, 1 - y)]
    return x, y, c, chips


def _remote(src, dst, ssem, rsem, dev):
    return pltpu.make_async_remote_copy(src_ref=src, dst_ref=dst, send_sem=ssem, recv_sem=rsem, device_id=dev, device_id_type=MESH)


HBM = pl.BlockSpec(memory_space=pltpu.HBM)
SEM = pl.BlockSpec(memory_space=pltpu.SEMAPHORE)
DATAFLOW = pltpu.SideEffectType.DATAFLOW_SIDE_EFFECTING


def _split_copies(name, srcs, lands, n_copies, plan):
    n_s, n_l = len(srcs), len(lands)
    hbm = lambda a: pltpu.HBM(a.shape, a.dtype)

    def start_body(*refs):
        ins = refs[:n_s + n_l]
        outs = refs[n_s + n_l:]
        sems, token = outs[:2 * n_copies], outs[-1]
        for k, (src, dst, dev) in enumerate(plan(ins[:n_s], ins[n_s:])):
            _remote(src, dst, sems[k], sems[n_copies + k], dev).start()
        token[...] = jnp.zeros_like(token)

    outs = pl.pallas_call(
        start_body, name=f"{name}_start",
        out_shape=(pltpu.SemaphoreType.DMA(()),) * (2 * n_copies) + tuple(hbm(a) for a in (*srcs, *lands)) + (SDS((8, LANES), F32),),
        in_specs=(HBM,) * (n_s + n_l), out_specs=(SEM,) * (2 * n_copies) + (HBM,) * (n_s + n_l) + (pl.BlockSpec(memory_space=pltpu.VMEM),),
        input_output_aliases={i: 2 * n_copies + i for i in range(n_s + n_l)},
        compiler_params=pltpu.CompilerParams(has_side_effects=DATAFLOW),
    )(*[pltpu.with_memory_space_constraint(a, pltpu.HBM) for a in (*srcs, *lands)])
    sems, thru, token = outs[:2 * n_copies], outs[2 * n_copies:-1], outs[-1]

    def wait(after_wait):
        def wait_body(*refs):
            ins = refs[:n_s + n_l]
            sem_refs = refs[n_s + n_l:n_s + n_l + 2 * n_copies]
            for k, (src, dst, dev) in enumerate(plan(ins[:n_s], ins[n_s:])):
                cp = _remote(src, dst, sem_refs[k], sem_refs[n_copies + k], dev)
                cp.wait_send()
                cp.wait_recv()

        res = pl.pallas_call(
            wait_body, name=f"{name}_wait", out_shape=tuple(hbm(a) for a in (*srcs, *lands)),
            in_specs=(HBM,) * (n_s + n_l) + (SEM,) * (2 * n_copies) + (ANY,), out_specs=(HBM,) * (n_s + n_l),
            input_output_aliases={i: i for i in range(n_s + n_l)},
            compiler_params=pltpu.CompilerParams(has_side_effects=DATAFLOW),
        )(*thru, *sems, after_wait)
        return list(res[n_s:])

    return token, wait


def _gather_plan(srcs, lands):
    x, y, c, chips = _place()
    me = 2 * x + y
    out = []
    for src, land in zip(srcs, lands):
        h = src.shape[0] // 2
        out.append((src, land.at[me], (x, y, 1 - c)))
        for px, py in chips:
            out.append((src.at[pl.ds(c * h, h)], land.at[me, pl.ds(c * h, h)], (px, py, c)))
    return out


def _scatter_plan(srcs, lands):
    x, y, c, chips = _place()
    out = []
    for src, land in zip(srcs, lands):
        for r, (px, py) in enumerate(chips):
            out.append((src.at[2 * px + py], land.at[r], (px, py, c)))
    return out


def _pass_plan(srcs, lands):
    x, y, c, chips = _place()
    out = []
    for land in lands:
        h = land.shape[1] // 2
        for px, py in chips:
            piece = land.at[2 * px + py, pl.ds(c * h, h)]
            out.append((piece, piece, (x, y, 1 - c)))
    return out


def _pair_plan(srcs, lands):
    x, y, c, _ = _place()
    out = []
    for src, land in zip(srcs, lands):
        h = land.shape[1]
        out.append((src.at[:, pl.ds((1 - c) * h, h)], land, (x, y, 1 - c)))
    return out


def _half_plan(srcs, lands):
    x, y, c, _ = _place()
    out = []
    for land in lands:
        h = land.shape[0] // 2
        mine = land.at[pl.ds(c * h, h)]
        out.append((mine, mine, (x, y, 1 - c)))
    return out


def _forward_halves(name, lands):
    n = len(lands)

    def body(*refs):
        ins, outs = refs[:n], refs[n:2 * n]
        ssem, rsem = refs[2 * n:]
        x, y, c, chips = _place()
        sib = (x, y, 1 - c)

        def copy(i, r, half):
            h = ins[i].shape[1] // 2
            px, py = chips[r]
            rows = pl.ds(half * h, h)
            return _remote(ins[i].at[2 * px + py, rows], outs[i].at[2 * px + py, rows], ssem.at[i, r], rsem.at[i, r], sib)

        for i in range(n):
            for r in range(3):
                copy(i, r, c).start()
        for i in range(n):
            for r in range(3):
                copy(i, r, 1 - c).wait_recv()
                copy(i, r, c).wait_send()

    return pl.pallas_call(
        body, in_specs=[ANY] * n, out_specs=[ANY] * n, out_shape=[SDS(a.shape, a.dtype) for a in lands],
        input_output_aliases={i: i for i in range(n)},
        scratch_shapes=[pltpu.SemaphoreType.DMA((n, 3)), pltpu.SemaphoreType.DMA((n, 3))], name=name)(*lands)


def _pair_exchange(name, gs):
    n = len(gs)

    def body(*refs):
        ins, outs = refs[:n], refs[n:2 * n]
        ssem, rsem = refs[2 * n:]
        x, y, c, _ = _place()
        cps = []
        for i in range(n):
            h = ins[i].shape[1] // 2
            cps.append(_remote(ins[i].at[:, pl.ds((1 - c) * h, h)], outs[i], ssem.at[i], rsem.at[i], (x, y, 1 - c)))
            cps[-1].start()
        for cp in cps:
            cp.wait()

    return pl.pallas_call(
        body, in_specs=[ANY] * n, out_specs=[ANY] * n,
        out_shape=[SDS((g.shape[0], g.shape[1] // 2, g.shape[2]), g.dtype) for g in gs],
        scratch_shapes=[pltpu.SemaphoreType.DMA((n,)), pltpu.SemaphoreType.DMA((n,))], name=name)(*gs)


def _half_exchange(name, fs):
    n = len(fs)

    def body(*refs):
        ins, outs = refs[:n], refs[n:2 * n]
        ssem, rsem = refs[2 * n:]
        x, y, c, _ = _place()

        def copy(i, half):
            h = ins[i].shape[0] // 2
            return _remote(ins[i].at[pl.ds(half * h, h)], outs[i].at[pl.ds(half * h, h)], ssem.at[i], rsem.at[i], (x, y, 1 - c))

        for i in range(n):
            copy(i, c).start()
        for i in range(n):
            copy(i, 1 - c).wait_recv()
            copy(i, c).wait_send()

    return pl.pallas_call(
        body, in_specs=[ANY] * n, out_specs=[ANY] * n, out_shape=[SDS(f.shape, f.dtype) for f in fs],
        input_output_aliases={i: i for i in range(n)},
        scratch_shapes=[pltpu.SemaphoreType.DMA((n,)), pltpu.SemaphoreType.DMA((n,))], name=name)(*fs)


def _gather_small(v):
    def body(v_ref, o_ref, lsem, ssem, rsem):
        x, y, c, _ = _place()
        me = 4 * x + 2 * y + c
        flips = [(fx, fy, fc) for fx in (0, 1) for fy in (0, 1) for fc in (0, 1)][1:]
        peer = lambda f: (1 - x if f[0] else x, 1 - y if f[1] else y, 1 - c if f[2] else c)
        loc = pltpu.make_async_copy(v_ref, o_ref.at[me], lsem)
        loc.start()
        cps = [_remote(v_ref, o_ref.at[me], ssem.at[k], rsem.at[k], peer(f)) for k, f in enumerate(flips)]
        for cp in cps:
            cp.start()
        for k, f in enumerate(flips):
            px, py, pc = peer(f)
            land = o_ref.at[4 * px + 2 * py + pc]
            _remote(land, land, ssem.at[k], rsem.at[k], peer(f)).wait_recv()
        for cp in cps:
            cp.wait_send()
        loc.wait()

    return pl.pallas_call(
        body, in_specs=[ANY], out_specs=ANY, out_shape=SDS((N_DEV,) + v.shape, v.dtype),
        scratch_shapes=[pltpu.SemaphoreType.DMA, pltpu.SemaphoreType.DMA((N_DEV - 1,)), pltpu.SemaphoreType.DMA((N_DEV - 1,))],
        name="gather_small")(v)


def _rows_tile(r):
    return _tile(r, (256, 352, 128, 64))


def _pair_sum(name, g, r1, core, me):
    n_l, s_n, h, c = r1.shape
    tr = _rows_tile(h)
    nb = h // tr

    def body(idx_ref, g_ref, r_ref, pb_ref, own_ref):
        p = g_ref[...] + r_ref[...]
        pb_ref[...] = p.astype(BF16)

        @pl.when(pl.program_id(2) == idx_ref[1])
        def _():
            own_ref[...] = p

    blk = (None, None, tr, c)
    grid_spec = pltpu.PrefetchScalarGridSpec(
        num_scalar_prefetch=1, grid=(n_l, nb, s_n),
        in_specs=[pl.BlockSpec(blk, lambda a, i, s, idx: (a, s, idx[0] * nb + i, 0)), pl.BlockSpec(blk, lambda a, i, s, idx: (a, s, i, 0))],
        out_specs=[pl.BlockSpec(blk, lambda a, i, s, idx: (a, s, i, 0)), pl.BlockSpec((None, tr, c), lambda a, i, s, idx: (a, i, 0))])
    idx = jnp.stack([core, me]).astype(jnp.int32)
    return pl.pallas_call(body, grid_spec=grid_spec, out_shape=[SDS(r1.shape, BF16), SDS((n_l, h, c), F32)], name=name)(idx, g, r1)


def _chip_sum(name, own, r2, core):
    n_l, h, c = own.shape
    tr = _rows_tile(h)
    nb = h // tr

    def body(idx_ref, o_ref, r_ref, f_ref):
        f_ref[...] = ((o_ref[...] + r_ref[0].astype(F32)) + r_ref[1].astype(F32)) + r_ref[2].astype(F32)

    grid_spec = pltpu.PrefetchScalarGridSpec(
        num_scalar_prefetch=1, grid=(n_l, nb),
        in_specs=[pl.BlockSpec((None, tr, c), lambda a, i, idx: (a, i, 0)), pl.BlockSpec((3, None, tr, c), lambda a, i, idx: (0, a, i, 0))],
        out_specs=pl.BlockSpec((None, tr, c), lambda a, i, idx: (a, idx[0] * nb + i, 0)))
    return pl.pallas_call(body, grid_spec=grid_spec, out_shape=SDS((n_l, 2 * h, c), F32), name=name)(
        jnp.reshape(core, (1,)).astype(jnp.int32), own, r2)


def _adamw_math(w, g, m, v):
    m = ADAM_B1 * m + (1.0 - ADAM_B1) * g
    v = ADAM_B2 * v + (1.0 - ADAM_B2) * jnp.square(g)
    m_hat = m / (1.0 - ADAM_B1 ** ADAM_STEP)
    v_hat = v / (1.0 - ADAM_B2 ** ADAM_STEP)
    delta = -ADAM_LR * (m_hat / (jnp.sqrt(v_hat) + ADAM_EPS) + ADAM_WD * w)
    return delta, m, v


def _adamw(name, l, w, g, m, v, bufs):
    _, r, c = w.shape
    tr = _rows_tile(r)

    def body(w_ref, g_ref, m_ref, v_ref, *rest):
        go_ref, d_ref, nm_ref, nv_ref = rest[4:]
        g = g_ref[...]
        go_ref[...] = g
        d_ref[...], nm_ref[...], nv_ref[...] = _adamw_math(w_ref[...], g, m_ref[...], v_ref[...])

    blk = pl.BlockSpec((None, tr, c), lambda i: (l, i, 0))
    return pl.pallas_call(body, grid=(r // tr,), in_specs=[blk, pl.BlockSpec((tr, c), lambda i: (i, 0)), blk, blk] + [ANY] * 4,
                          out_specs=[blk] * 4, out_shape=[SDS(w.shape, F32)] * 4, input_output_aliases={4 + k: k for k in range(4)},
                          name=name)(w, g, m, v, *bufs)


def _small_update(parts, w, m, v):
    def body(p_ref, w_ref, m_ref, v_ref, g_ref, d_ref, nm_ref, nv_ref):
        g = p_ref[0]
        for d in range(1, N_DEV):
            g = g + p_ref[d]
        g_ref[...] = g
        d_ref[...], nm_ref[...], nv_ref[...] = _adamw_math(w_ref[...], g, m_ref[...], v_ref[...])

    return pl.pallas_call(body, out_shape=[SDS(w.shape, F32)] * 4, name="small_update")(parts, w, m, v)


SMALL = ["norm_mix_g", "norm_mem_g", "ret_decay_fwd", "ret_decay_bwd", "ret_norm_g", "pool_w", "pool_scale", "na_q_norm_g",
         "na_k_norm_g", "na_rpb", "mem_q_norm_g", "mem_k_norm_g", "norm_ffn_g"]
BIG = ["w_in", "w_gate", "w_mem_kv", "w_branch", "w_out", "w_ffn_in", "w_ffn_out"]
GROUPS = (("w_in", "w_gate", "w_mem_kv", "w_branch", "w_out"), ("w_ffn_in", "w_ffn_out"))
ORDER = ["norm_mix_g", "norm_mem_g", "w_in", "w_gate", "ret_decay_fwd", "ret_decay_bwd", "ret_norm_g", "pool_w", "pool_scale",
         "na_q_norm_g", "na_k_norm_g", "na_rpb", "mem_q_norm_g", "mem_k_norm_g", "w_mem_kv", "w_branch", "w_out", "norm_ffn_g",
         "w_ffn_in", "w_ffn_out"]


def _pack(arrs):
    rows = []
    for a in arrs:
        f = a.reshape(-1).astype(F32)
        pad = (-f.shape[0]) % (8 * LANES)
        rows.append(jnp.pad(f, (0, pad)).reshape(-1, LANES))
    return jnp.concatenate(rows, axis=0)


def _unpack(p, like):
    out, at = [], 0
    for a in like:
        n = int(np.prod(a.shape))
        rows = -(-n // (8 * LANES)) * 8
        out.append(p[at:at + rows].reshape(-1)[:n].reshape(a.shape))
        at += rows
    return out


def _rope_tables(t_len):
    half = HEAD // 2
    inv = ROPE_THETA ** (-jnp.arange(half, dtype=F32) / half)
    ang = jnp.arange(t_len, dtype=F32)[:, None] * inv[None, :]
    cos, sin = jnp.cos(ang), jnp.sin(ang)
    return jnp.tile(jnp.concatenate([cos, cos], axis=1), (1, NH)), jnp.tile(jnp.concatenate([-sin, sin], axis=1), (1, NH))


def _block_diag(pw):
    out = jnp.zeros((BW, BW), pw.dtype)
    for g in range(NH):
        out = out.at[g * HEAD:(g + 1) * HEAD, g * HEAD:(g + 1) * HEAD].set(pw[g])
    return out


def _fold_heads(v):
    return v.reshape(NH, HEAD).sum(axis=0)


def _fold_lanes(v):
    return v.reshape(NH, HEAD).sum(axis=1)


def kernel(x, mem, norm_mix_g, norm_mem_g, w_in, w_gate, ret_decay_fwd, ret_decay_bwd, ret_norm_g, pool_w, pool_scale, na_q_norm_g, na_k_norm_g, na_rpb, mem_q_norm_g, mem_k_norm_g, w_mem_kv, w_branch, w_out, norm_ffn_g, w_ffn_in, w_ffn_out, loss_target, m_norm_mix_g, m_norm_mem_g, m_w_in, m_w_gate, m_ret_decay_fwd, m_ret_decay_bwd, m_ret_norm_g, m_pool_w, m_pool_scale, m_na_q_norm_g, m_na_k_norm_g, m_na_rpb, m_mem_q_norm_g, m_mem_k_norm_g, m_w_mem_kv, m_w_branch, m_w_out, m_norm_ffn_g, m_w_ffn_in, m_w_ffn_out, v_norm_mix_g, v_norm_mem_g, v_w_in, v_w_gate, v_ret_decay_fwd, v_ret_decay_bwd, v_ret_norm_g, v_pool_w, v_pool_scale, v_na_q_norm_g, v_na_k_norm_g, v_na_rpb, v_mem_q_norm_g, v_mem_k_norm_g, v_w_mem_kv, v_w_branch, v_w_out, v_norm_ffn_g, v_w_ffn_in, v_w_ffn_out):
    args = dict(locals())
    W = {n: args[n] for n in ORDER}
    M = {n: args["m_" + n] for n in ORDER}
    V = {n: args["v_" + n] for n in ORDER}
    n_layers, d_model = norm_mix_g.shape
    assert x.shape[0] == 1
    t_len = x.shape[1]
    xc, yc, cc = lax.axis_index("x"), lax.axis_index("y"), lax.axis_index("c")
    me_chip = 2 * xc + yc

    def as3(a):
        return a.reshape(a.shape[0], -1, a.shape[-1])

    gather_waits, token = {}, jnp.zeros((), F32)
    for l in range(n_layers):
        for gi, names in enumerate(GROUPS):
            srcs = [as3(W[n])[l].astype(BF16) for n in names]
            lands = [lax.empty((N_CHIPS,) + a.shape, BF16) for a in srcs]
            tok, gather_waits[l, gi] = _split_copies(f"gather_l{l}_g{gi}", srcs, lands, 4 * len(names), _gather_plan)
            token = token + tok[0, 0]

    cos, sin = _rope_tables(t_len)
    btabs = _na_bias_table(na_rpb)
    lane = lambda a: a.reshape(1, -1).astype(F32)
    x2d, mem2d, tgt = x[0], mem[0], loss_target[0]
    norm_mix_g = norm_mix_g + token

    def fetch(l, gi, after):
        lands = gather_waits[l, gi](after)
        return _split_copies(f"pass_l{l}_g{gi}", [], lands, 3 * len(GROUPS[gi]), _pass_plan)

    saved = []
    _, passing = fetch(0, 0, x2d)
    for l in range(n_layers):
        tag = f"l{l}"
        gathered = dict(zip(GROUPS[0], passing(x2d)))
        tok, passing = fetch(l, 1, x2d)
        pin_mix = tok[0, 0]
        wi = gathered["w_in"].transpose(1, 0, 2).reshape(1, 1, d_model, -1)
        wg = gathered["w_gate"][None]
        wmkv = gathered["w_mem_kv"].reshape(1, 1, d_model, 2 * BW)
        wb = gathered["w_branch"].reshape(1, N_CHIPS, NH, BW, BW)
        wo = gathered["w_out"].reshape(1, 1, d_model, d_model)
        s = {"x": x2d}
        s["decf"], s["decb"] = lane(jnp.repeat(ret_decay_fwd[l], HEAD)), lane(jnp.repeat(ret_decay_bwd[l], HEAD))
        s["gn"], s["pscale"] = lane(ret_norm_g[l]), lane(pool_scale[l])
        s["wbd"] = _block_diag(pool_w[l])
        s["nagq"], s["nagk"] = lane(jnp.tile(na_q_norm_g[l], NH)), lane(jnp.tile(na_k_norm_g[l], NH))
        s["mgq"], s["mgk"] = lane(jnp.tile(mem_q_norm_g[l], NH)), lane(jnp.tile(mem_k_norm_g[l], NH))
        s["btab"] = btabs[l]
        s["h"] = _rms_fwd(f"rms_mix_{tag}", x2d, lane(norm_mix_g[l]) + pin_mix)
        s["proj"] = _mm_nn(f"mm_in_{tag}", s["h"], wi, 0)
        s["gp3"] = _mm_nn(f"mm_gate_{tag}", s["h"], wg, 0, out3d=True)
        s["hm"] = _rms_fwd(f"rms_mem_{tag}", mem2d, lane(norm_mem_g[l]))
        s["memkv"] = _mm_nn(f"mm_memkv_{tag}", s["hm"], wmkv, 0)
        ret, s["ret_saved"] = _ret_fwd(tag, s["proj"], cos, sin, s["decf"], s["decb"], s["gn"])
        s["pvp"] = jnp.pad(s["proj"][:, 4 * BW:5 * BW], ((POOL_PAD, POOL_PAD), (0, 0)))
        pool = _pool_fwd(tag, s["pvp"], s["wbd"], s["pscale"], t_len)[POOL_PAD:POOL_PAD + t_len]
        s["kn"], s["vb"] = _na_keys(tag, s["proj"], s["nagk"])
        na = _na_fwd(tag, s["proj"], s["kn"], s["vb"], s["btab"], s["nagq"])
        mo = _mem_fwd(tag, s["proj"], s["memkv"], s["mgq"], s["mgk"])
        s["br"] = jnp.stack([ret, pool, na, mo])
        s["merged"] = _gate_fwd(f"gate_{tag}", s["gp3"], s["br"], wb, 0)
        s["x2"] = _mm_nn(f"mm_out_{tag}", s["merged"], wo, 0, add=x2d)
        gathered = dict(zip(GROUPS[1], passing(s["x2"])))
        pin_ffn = jnp.zeros((), F32)
        if l + 1 < n_layers:
            tok, passing = fetch(l + 1, 0, s["x2"])
            pin_ffn = tok[0, 0]
        wfi = gathered["w_ffn_in"][None]
        wfo = gathered["w_ffn_out"].reshape(1, 1, -1, d_model)
        s["w"] = (wi, wg, wmkv, wb, wo, wfi, wfo)
        s["h2"] = _rms_fwd(f"rms_ffn_{tag}", s["x2"], lane(norm_ffn_g[l]) + pin_ffn)
        s["ag3"] = _mm_nn(f"mm_ffn_in_{tag}", s["h2"], wfi, 0, out3d=True)
        s["u"] = _swiglu_fwd(f"swiglu_{tag}", s["ag3"])
        x2d = _mm_nn(f"mm_ffn_out_{tag}", s["u"], wfo, 0, add=s["x2"])
        saved.append(s)

    loss_part, dx = _loss_kernel("loss", x2d, tgt)

    dbtabs = [None] * n_layers
    gsmall = {n: [None] * n_layers for n in SMALL}
    opt_view = {n: (lambda a, n=n: jnp.swapaxes(as3(a), 1, 2) if n == "w_in" else as3(a)) for n in BIG}
    out_bufs = {n: tuple(lax.empty(opt_view[n](W[n]).shape, F32) for _ in range(4)) for n in BIG}

    def reduce_group(l, gi, grads):
        names = GROUPS[gi]
        gs = [grads[n] for n in names]
        lands = [lax.empty((N_CHIPS, g.shape[1] // 2, g.shape[2]), F32) for g in gs]
        tok0, wait0 = _split_copies(f"pairx_l{l}_g{gi}", gs, lands, len(names), _pair_plan)

        def sums_and_scatter(after):
            sums = [_pair_sum(f"pair_sum_{n}_l{l}", g[None], r[None], cc, me_chip) for n, g, r in zip(names, gs, wait0(after))]
            pbs = [p[0] for p, _ in sums]
            lands1 = [lax.empty((3,) + p.shape[1:], BF16) for p in pbs]
            tok1, wait1 = _split_copies(f"scatter_l{l}_g{gi}", pbs, lands1, 3 * len(names), _scatter_plan)

            def chip_sums_and_halves(after):
                halves = [_chip_sum(f"chip_sum_{n}_l{l}", own, r[:, None], cc)[0] for n, (_, own), r in zip(names, sums, wait1(after))]
                tok2, wait2 = _split_copies(f"halfx_l{l}_g{gi}", [], halves, len(names), _half_plan)

                def update(after):
                    for n, f in zip(names, wait2(after)):
                        out_bufs[n] = _adamw(f"adamw_{n}_l{l}", l, opt_view[n](W[n]), f, opt_view[n](M[n]), opt_view[n](V[n]), out_bufs[n])
                    return None, None

                return update, tok2

            return chip_sums_and_halves, tok1

        return sums_and_scatter, tok0

    def advance(stages, new):
        nxt, tok = new
        out, toks = [nxt], tok
        for stage in stages:
            nxt, tok = stage(toks)
            if nxt is not None:
                out.append(nxt)
                toks = toks + tok
        return out, toks

    stages, order_after = [], None
    for l in reversed(range(n_layers)):
        tag = f"l{l}"
        s = saved[l]
        wi, wg, wmkv, wb, wo, wfi, wfo = s["w"]
        du = _mm_nt(f"mm_ffn_out_dx_{tag}", dx, wfo, 0, after=order_after)
        g_wfo = _mm_tn(f"mm_ffn_out_dw_{tag}", s["u"], dx, 1)
        dag3 = _swiglu_bwd(f"swiglu_bwd_{tag}", s["ag3"], du)
        dh2 = _mm_nt(f"mm_ffn_in_dx_{tag}", dag3, wfi, 0)
        g_wfi = _mm_tn(f"mm_ffn_in_dw_{tag}", s["h2"], dag3, N_CHIPS)
        dx2, dg = _rms_bwd(f"rms_ffn_bwd_{tag}", s["x2"], lane(norm_ffn_g[l]), [dh2], add=dx)
        gsmall["norm_ffn_g"][l] = dg[0]
        stages, tok = advance(stages, reduce_group(l, 1, {"w_ffn_in": g_wfi, "w_ffn_out": g_wfo.reshape(N_CHIPS, -1, d_model)}))
        dmerged = _mm_nt(f"mm_out_dx_{tag}", dx2, wo, 0, after=tok)
        g_wo = _mm_tn(f"mm_out_dw_{tag}", s["merged"], dx2, 1)
        dgp3, dbr, g_wb = _gate_bwd(f"gate_bwd_{tag}", s["gp3"], s["br"], wb, 0, dmerged)
        g_wg = _mm_tn(f"mm_gate_dw_{tag}", s["h"], dgp3, N_CHIPS)
        dh_a = _mm_nt(f"mm_gate_dx_{tag}", dgp3, wg, 0)
        drq, drk, drv, drg, ddf, ddb, dgn = _ret_bwd(tag, s["proj"], cos, sin, s["decf"], s["decb"], s["gn"], s["ret_saved"], dbr[0])
        gsmall["ret_decay_fwd"][l], gsmall["ret_decay_bwd"][l], gsmall["ret_norm_g"][l] = _fold_lanes(ddf), _fold_lanes(ddb), dgn[0]
        dpool_p = jnp.pad(dbr[1], ((POOL_PAD, POOL_PAD), (0, 0)))
        dpvp, dwbd, dps = _pool_bwd(tag, s["pvp"], s["wbd"], s["pscale"], t_len, dpool_p)
        dpv = dpvp[POOL_PAD:POOL_PAD + t_len]
        gsmall["pool_w"][l] = jnp.stack([dwbd[g * HEAD:(g + 1) * HEAD, g * HEAD:(g + 1) * HEAD] for g in range(NH)])
        gsmall["pool_scale"][l] = dps[0]
        dnq, dkn, dnv, dbtabs[l], dgq = _na_bwd(tag, s["proj"], s["kn"], s["vb"], s["btab"], s["nagq"], dbr[2])
        dnk, dnv, dgk = _na_keys_bwd(tag, s["proj"], s["nagk"], dkn, dnv)
        gsmall["na_q_norm_g"][l], gsmall["na_k_norm_g"][l] = _fold_heads(dgq), _fold_heads(dgk)
        dmq, dmemkv, dgq, dgk = _mem_bwd(tag, s["proj"], s["memkv"], s["mgq"], s["mgk"], dbr[3])
        gsmall["mem_q_norm_g"][l], gsmall["mem_k_norm_g"][l] = _fold_heads(dgq), _fold_heads(dgk)
        g_wmkv = _mm_tn(f"mm_memkv_dw_{tag}", s["hm"], dmemkv, 1)
        dhm = _mm_nt(f"mm_memkv_dx_{tag}", dmemkv, wmkv, 0)
        _, dg = _rms_bwd(f"rms_mem_bwd_{tag}", mem2d, lane(norm_mem_g[l]), [dhm])
        gsmall["norm_mem_g"][l] = dg[0]
        dproj = jnp.concatenate([drq, drk, drv, drg, dpv, dnq, dnk, dnv, dmq], axis=1)
        g_wi = _mm_tn(f"mm_in_dw_{tag}", dproj, s["h"], 1)
        dh_b = _mm_nt(f"mm_in_dx_{tag}", dproj, wi, 0)
        dx, dg = _rms_bwd(f"rms_mix_bwd_{tag}", s["x"], lane(norm_mix_g[l]), [dh_a, dh_b], add=dx2)
        gsmall["norm_mix_g"][l] = dg[0]

        stages, order_after = advance(stages, reduce_group(l, 0, {
            "w_in": g_wi.reshape(N_CHIPS, -1, d_model), "w_gate": g_wg, "w_mem_kv": g_wmkv.reshape(N_CHIPS, -1, 2 * BW),
            "w_branch": g_wb.reshape(N_CHIPS, NH * BW, BW), "w_out": g_wo.reshape(N_CHIPS, -1, d_model)}))
    while stages:
        stages = [nxt for nxt, _ in (stage(dx) for stage in stages) if nxt is not None]

    out_g, out_d, out_m, out_v = {}, {}, {}, {}
    for n in BIG:
        shp = W[n].shape
        back = (lambda b: jnp.swapaxes(b, 1, 2)) if n == "w_in" else (lambda b: b)
        out_g[n], out_d[n], out_m[n], out_v[n] = [back(b).reshape(shp) for b in out_bufs[n]]

    gsmall["na_rpb"] = list(_na_bias_grad(jnp.stack(dbtabs)))
    small_g = [jnp.stack(gsmall[n]).reshape(W[n].shape) for n in SMALL] + [loss_part]
    like = [W[n] for n in SMALL] + [loss_part]
    zero = jnp.zeros((1, 1), F32)
    parts = _gather_small(_pack(small_g))
    sg, sd, sm, sv = _small_update(parts, _pack([W[n] for n in SMALL] + [zero]), _pack([M[n] for n in SMALL] + [zero]),
                                   _pack([V[n] for n in SMALL] + [zero]))
    sg, sd, sm, sv = _unpack(sg, like), _unpack(sd, like), _unpack(sm, like), _unpack(sv, like)
    for i, n in enumerate(SMALL):
        out_g[n], out_d[n], out_m[n], out_v[n] = sg[i], sd[i], sm[i], sv[i]
    loss = sg[-1].reshape(())

    return (loss, dx.reshape(x.shape), *[out_g[n] for n in ORDER], *[out_d[n] for n in ORDER],
            *[out_m[n] for n in ORDER], *[out_v[n] for n in ORDER])
```

```python
import functools

import numpy as np
import jax
import jax.numpy as jnp
from jax import lax
from jax.experimental import pallas as pl
from jax.experimental.pallas import tpu as pltpu

F32 = jnp.float32
BF16 = jnp.bfloat16
SDS = jax.ShapeDtypeStruct
MESH = pl.DeviceIdType.MESH
ANY = pl.BlockSpec(memory_space=pl.ANY)

HEAD = 64
NH = 4
BW = NH * HEAD
CH = 128
GRID_W = 64
NA_ROWS = 8
NA_COLS = 16
POOL_HALF = (1, 2, 4, 8)
POOL_PAD = 16
ROPE_THETA = 10000.0
EPS = 1e-6
NEG_INF = -1e30
N_CHIPS = 4
N_DEV = 8
LANES = 128

ADAM_LR = 0.001
ADAM_B1 = 0.9
ADAM_B2 = 0.999
ADAM_EPS = 1e-08
ADAM_WD = 0.01
ADAM_STEP = 10

VMEM_BIG = 56 << 20
VMEM_MM = 44 << 20


def _cp(vmem=None):
    return pltpu.CompilerParams(vmem_limit_bytes=vmem) if vmem else None


def _d(a, b, ca, cb):
    return lax.dot_general(a.astype(BF16), b.astype(BF16), (((ca,), (cb,)), ((), ())),
                           preferred_element_type=F32)


@jax.custom_vjp
def _nn(a, b):
    return _d(a, b, 1, 0)


def _nn_f(a, b):
    return _d(a, b, 1, 0), (a, b)


def _nn_b(r, g):
    a, b = r
    return _d(g, b, 1, 1).astype(a.dtype), _d(a, g, 0, 0).astype(b.dtype)


_nn.defvjp(_nn_f, _nn_b)


@jax.custom_vjp
def _nt(a, b):
    return _d(a, b, 1, 1)


def _nt_f(a, b):
    return _d(a, b, 1, 1), (a, b)


def _nt_b(r, g):
    a, b = r
    return _d(g, b, 1, 0).astype(a.dtype), _d(g, a, 0, 0).astype(b.dtype)


_nt.defvjp(_nt_f, _nt_b)


@jax.custom_vjp
def _tn(a, b):
    return _d(a, b, 0, 0)


def _tn_f(a, b):
    return _d(a, b, 0, 0), (a, b)


def _tn_b(r, g):
    a, b = r
    return _d(b, g, 1, 1).astype(a.dtype), _d(a, g, 1, 0).astype(b.dtype)


_tn.defvjp(_tn_f, _tn_b)


@functools.partial(jax.custom_vjp, nondiff_argnums=(1,))
def _rollr(x, s):
    return pltpu.roll(x, s % x.shape[0], 0)


def _rollr_f(x, s):
    return _rollr(x, s), None


def _rollr_b(s, _, g):
    return (_rollr(g, -s),)


_rollr.defvjp(_rollr_f, _rollr_b)


@jax.custom_vjp
def _swap32(t):
    n = t.shape[1]
    lane = lax.broadcasted_iota(jnp.int32, (1, n), 1)
    return jnp.where((lane & (HEAD // 2)) == 0, pltpu.roll(t, n - HEAD // 2, 1), pltpu.roll(t, HEAD // 2, 1))


def _swap32_f(t):
    return _swap32(t), None


def _swap32_b(_, g):
    return (_swap32(g),)


_swap32.defvjp(_swap32_f, _swap32_b)


def _head_masks():
    lane = lax.broadcasted_iota(jnp.int32, (1, BW), 1)
    return [(lane >= HEAD * h) & (lane < HEAD * (h + 1)) for h in range(NH)]


def _head_rms(x, g):
    out = jnp.zeros_like(x)
    for mh in _head_masks():
        ms = jnp.sum(jnp.where(mh, x * x, 0.0), axis=-1, keepdims=True) * (1.0 / HEAD)
        out = out + jnp.where(mh, x * lax.rsqrt(ms + EPS), 0.0)
    return out * g


def _rms(x, g):
    return x * lax.rsqrt(jnp.mean(x * x, axis=-1, keepdims=True) + EPS) * g


def _rot(t, cos, sin):
    return t * cos + _swap32(t) * sin


def _softmax(s):
    e = jnp.exp(s - lax.stop_gradient(jnp.max(s, axis=-1, keepdims=True)))
    return e / jnp.sum(e, axis=-1, keepdims=True)


def _ret_kv_f(k, v, cos, sin, decf, decb):
    lgf, lgb = jax.nn.log_sigmoid(decf), jax.nn.log_sigmoid(decb)
    kr = _rot(k, cos, sin)
    idx = lax.broadcasted_iota(jnp.int32, (CH, 1), 0).astype(F32)
    r = lax.broadcasted_iota(jnp.int32, (BW, BW), 0) // HEAD
    c = lax.broadcasted_iota(jnp.int32, (BW, BW), 1) // HEAD
    bd = r == c
    kvf = jnp.where(bd, _tn(kr * jnp.exp((CH - 1 - idx) * lgf), v), 0.0)
    kvb = jnp.where(bd, _tn(kr * jnp.exp(idx * lgb), v), 0.0)
    return kvf, kvb


def _ret_scan_f(kvf, kvb, decf, decb):
    n = len(kvf)
    cdf = jnp.exp(CH * jax.nn.log_sigmoid(decf))
    cdb = jnp.exp(CH * jax.nn.log_sigmoid(decb))
    z = jnp.zeros((BW, BW), F32)
    sf, st = [z], z
    for a in range(n - 1):
        st = cdf * st + kvf[a]
        sf.append(st)
    sb, st = [z], z
    for a in range(n - 1, 0, -1):
        st = cdb * st + kvb[a]
        sb.append(st)
    return sf, sb[::-1]


def _ret_out_f(q, k, v, g, cos, sin, stf, stb, decf, decb, gn):
    lgf, lgb = jax.nn.log_sigmoid(decf), jax.nn.log_sigmoid(decb)
    qr = _rot(q, cos, sin) * (HEAD ** -0.5)
    kr = _rot(k, cos, sin)
    diff = (lax.broadcasted_iota(jnp.int32, (CH, CH), 0) - lax.broadcasted_iota(jnp.int32, (CH, CH), 1)).astype(F32)
    o = jnp.zeros_like(q)
    for mh in _head_masks():
        lf = jnp.sum(jnp.where(mh, lgf, 0.0), axis=1, keepdims=True) * (1.0 / HEAD)
        lb = jnp.sum(jnp.where(mh, lgb, 0.0), axis=1, keepdims=True) * (1.0 / HEAD)
        s = _nt(jnp.where(mh, qr, 0.0), kr)
        p = s * jnp.exp(jnp.where(diff >= 0, diff * lf, -diff * lb))
        o = o + jnp.where(mh, _nn(p, v), 0.0)
    idx = lax.broadcasted_iota(jnp.int32, (CH, 1), 0).astype(F32)
    o = o + _nn(qr * jnp.exp((idx + 1) * lgf), stf) + _nn(qr * jnp.exp((CH - idx) * lgb), stb)
    return _head_rms(o, gn) * (g * jax.nn.sigmoid(g))


def _pool_f(pvp, wbd, scale, t_len):
    n = pvp.shape[0]
    w2 = pvp + _rollr(pvp, 1)
    w4 = _rollr(w2, 1) + _rollr(w2, -1)
    w8 = _rollr(w4, 2) + _rollr(w4, -2)
    w16 = _rollr(w8, 4) + _rollr(w8, -4)
    grp = lax.broadcasted_iota(jnp.int32, (1, BW), 1) // HEAD
    ws = jnp.where(grp == 0, w2, jnp.where(grp == 1, w4, jnp.where(grp == 2, w8, w16)))
    half = jnp.where(grp == 0, POOL_HALF[0], jnp.where(grp == 1, POOL_HALF[1], jnp.where(grp == 2, POOL_HALF[2], POOL_HALF[3])))
    t = lax.broadcasted_iota(jnp.int32, (n, 1), 0) - POOL_PAD
    cnt = jnp.minimum(t + half, t_len) - jnp.maximum(t - half, 0)
    cnt = jnp.where((t >= 0) & (t < t_len), cnt, 1).astype(F32)
    pooled = ws / cnt - pvp
    return _nn(pooled, wbd) * scale


def _na_f(q, kn, vw, bias, gq):
    qn = _head_rms(q, gq)
    o = jnp.zeros_like(q)
    for h, mh in enumerate(_head_masks()):
        s = _nt(jnp.where(mh, qn, 0.0), kn) * (HEAD ** -0.5) + jnp.concatenate(bias[h], axis=1)
        o = o + jnp.where(mh, _nn(_softmax(s), vw), 0.0)
    return o


def _mem_f(q, mk, mv, gq, gk):
    qn = _head_rms(q, gq)
    kn = _head_rms(mk, gk)
    o = jnp.zeros_like(q)
    for mh in _head_masks():
        s = _nt(jnp.where(mh, qn, 0.0), kn) * (HEAD ** -0.5)
        o = o + jnp.where(mh, _nn(_softmax(s), mv), 0.0)
    return o


def _gate_f(gp, br, wb):
    out = None
    for n in range(NH):
        t = jax.nn.sigmoid(gp[n]) * _nn(br[n], wb[n])
        out = t if out is None else out + t
    return out


def _swiglu_f(a, g):
    return a * jax.nn.sigmoid(a) * g


MM_ROWS = (1024, 512, 256)
MM_COLS = (1152, 1024, 768, 512)


def _tile(n, prefs):
    for p in prefs:
        if n % p == 0:
            return p
    return n


def _mm_call(name, a, b, extra, dn, grid, a_spec, b_spec, extra_specs, o_spec, out_shape, acc_shape, nred, after=None):
    add = len(extra) == 1
    if after is not None:
        extra, extra_specs = [*extra, after], [*extra_specs, ANY]
    n_extra = len(extra)
    n_steps = int(np.prod([grid[len(grid) - 1 - i] for i in range(nred)]))

    def body(*refs):
        a_ref, b_ref = refs[0], refs[1]
        o_ref, acc = refs[2 + n_extra], refs[-1]
        red = [len(grid) - 1 - i for i in range(nred)]
        first = functools.reduce(jnp.logical_and, [pl.program_id(ax) == 0 for ax in red])
        last = functools.reduce(jnp.logical_and, [pl.program_id(ax) == grid[ax] - 1 for ax in red])
        d = lax.dot_general(a_ref[...].astype(BF16), b_ref[...].astype(BF16), dn, preferred_element_type=F32)

        def finish(r):
            if add:
                r = r + refs[2][...]
            o_ref[...] = r.astype(o_ref.dtype)

        if n_steps == 1:
            finish(d)
        else:
            @pl.when(first)
            def _():
                acc[...] = d

            @pl.when(jnp.logical_not(first) & jnp.logical_not(last))
            def _():
                acc[...] += d

            @pl.when(last)
            def _():
                finish(acc[...] + d)

    return pl.pallas_call(
        body, grid=grid, in_specs=[a_spec, b_spec, *extra_specs], out_specs=o_spec, out_shape=out_shape,
        scratch_shapes=[pltpu.VMEM(acc_shape, F32)] if n_steps > 1 else [], name=name, compiler_params=_cp(VMEM_MM),
    )(a, b, *extra)


def _mm_nn(name, a, w, l, *, out3d=False, add=None, out_dtype=F32):
    m, k = a.shape
    _, s_n, _, ns = w.shape
    tm = _tile(m, MM_ROWS)
    tk = _tile(k, (1024, 1408, 512, 256))
    tn = ns if ns <= 1408 else _tile(ns, MM_COLS)
    grid = (m // tm, s_n, ns // tn, k // tk)
    a_spec = pl.BlockSpec((tm, tk), lambda i, s, j, kk: (i, kk))
    b_spec = pl.BlockSpec((None, None, tk, tn), lambda i, s, j, kk: (l, s, kk, j))
    if out3d:
        o_spec = pl.BlockSpec((None, tm, tn), lambda i, s, j, kk: (s, i, j))
        out_shape = SDS((s_n, m, ns), out_dtype)
    else:
        nj = ns // tn
        o_spec = pl.BlockSpec((tm, tn), lambda i, s, j, kk: (i, s * nj + j))
        out_shape = SDS((m, s_n * ns), out_dtype)
    extra, especs = [], []
    if add is not None:
        assert not out3d
        nj = ns // tn
        extra, especs = [add], [pl.BlockSpec((tm, tn), lambda i, s, j, kk: (i, s * nj + j))]
    return _mm_call(name, a, w, extra, (((1,), (0,)), ((), ())), grid, a_spec, b_spec, especs, o_spec, out_shape, (tm, tn), 1)


def _mm_nt(name, g, w, l, after=None):
    _, s_n, k, ns = w.shape
    g3d = g.ndim == 3
    m = g.shape[1] if g3d else g.shape[0]
    tm = _tile(m, MM_ROWS)
    to = _tile(k, (1024, 1408, 512, 256))
    tk = ns if (ns % LANES or ns <= 1408) else _tile(ns, MM_COLS)
    nks = ns // tk
    grid = (m // tm, k // to, s_n, nks)
    if g3d:
        a_spec = pl.BlockSpec((None, tm, tk), lambda i, j, s, r: (s, i, r))
    else:
        a_spec = pl.BlockSpec((tm, tk), lambda i, j, s, r: (i, s * nks + r))
    b_spec = pl.BlockSpec((None, None, to, tk), lambda i, j, s, r: (l, s, j, r))
    o_spec = pl.BlockSpec((tm, to), lambda i, j, s, r: (i, j))
    return _mm_call(name, g, w, [], (((1,), (1,)), ((), ())), grid, a_spec, b_spec, [], o_spec, SDS((m, k), F32), (tm, to), 2,
                    after=after)


def _mm_tn(name, a, g, s_n):
    t_len, k = a.shape
    g3d = g.ndim == 3
    ns = g.shape[2] if g3d else g.shape[1] // s_n
    tt = _tile(t_len, (1024, 512))
    to = _tile(k, (1024, 768, 1408, 512, 256, 128))
    tn = ns
    nj = ns // tn
    grid = (s_n, k // to, nj, t_len // tt)
    a_spec = pl.BlockSpec((tt, to), lambda s, i, j, t: (t, i))
    if g3d:
        b_spec = pl.BlockSpec((None, tt, tn), lambda s, i, j, t: (s, t, j))
    else:
        b_spec = pl.BlockSpec((tt, tn), lambda s, i, j, t: (t, s * nj + j))
    o_spec = pl.BlockSpec((None, to, tn), lambda s, i, j, t: (s, i, j))
    return _mm_call(name, a, g, [], (((0,), (0,)), ((), ())), grid, a_spec, b_spec, [], o_spec,
                    SDS((s_n, k, ns), F32), (to, tn), 1)


def _rms_fwd(name, x, g):
    m, d = x.shape
    tm = _tile(m, (256,))

    def body(x_ref, g_ref, o_ref):
        o_ref[...] = _rms(x_ref[...], g_ref[...]).astype(BF16)

    row = pl.BlockSpec((tm, d), lambda i: (i, 0))
    return pl.pallas_call(body, grid=(m // tm,), in_specs=[row, pl.BlockSpec((1, d), lambda i: (0, 0))], out_specs=row,
                          out_shape=SDS((m, d), BF16), name=name)(x, g)


def _rms_bwd(name, x, g, dhs, add=None):
    m, d = x.shape
    tm = _tile(m, (256,))
    nh = len(dhs)

    def body(*refs):
        x_ref, g_ref = refs[0], refs[1]
        dh = refs[2][...]
        for r in refs[3:2 + nh]:
            dh = dh + r[...]
        dx_ref, dg_ref = refs[-2], refs[-1]
        _, vjp = jax.vjp(_rms, x_ref[...], g_ref[...])
        dx, dg = vjp(dh)
        if add is not None:
            dx = dx + refs[2 + nh][...]
        dx_ref[...] = dx

        @pl.when(pl.program_id(0) == 0)
        def _():
            dg_ref[...] = jnp.zeros_like(dg_ref)

        dg_ref[...] += dg

    row = pl.BlockSpec((tm, d), lambda i: (i, 0))
    vec = pl.BlockSpec((1, d), lambda i: (0, 0))
    ins = [x, g, *dhs] + ([add] if add is not None else [])
    return pl.pallas_call(body, grid=(m // tm,), in_specs=[row, vec] + [row] * (len(ins) - 2), out_specs=[row, vec],
                          out_shape=[SDS((m, d), F32), SDS((1, d), F32)], name=name)(*ins)


def _swiglu_fwd(name, ag3):
    _, t_len, w = ag3.shape
    tm = _tile(t_len, (512, 256))

    def body(a_ref, g_ref, o_ref):
        o_ref[...] = _swiglu_f(a_ref[...], g_ref[...]).astype(BF16)

    return pl.pallas_call(
        body, grid=(t_len // tm, 2),
        in_specs=[pl.BlockSpec((None, tm, w), lambda i, j: (j, i, 0)), pl.BlockSpec((None, tm, w), lambda i, j: (j + 2, i, 0))],
        out_specs=pl.BlockSpec((tm, w), lambda i, j: (i, j)), out_shape=SDS((t_len, 2 * w), BF16), name=name)(ag3, ag3)


def _swiglu_bwd(name, ag3, du):
    _, t_len, w = ag3.shape
    tm = _tile(t_len, (512, 256))

    def body(a_ref, g_ref, du_ref, o_ref):
        _, vjp = jax.vjp(_swiglu_f, a_ref[...], g_ref[...])
        da, dg = vjp(du_ref[...])
        o_ref[0] = da.astype(BF16)
        o_ref[1] = dg.astype(BF16)

    out = pl.pallas_call(
        body, grid=(t_len // tm, 2),
        in_specs=[pl.BlockSpec((None, tm, w), lambda i, j: (j, i, 0)), pl.BlockSpec((None, tm, w), lambda i, j: (j + 2, i, 0)),
                  pl.BlockSpec((tm, w), lambda i, j: (i, j))],
        out_specs=pl.BlockSpec((2, None, tm, w), lambda i, j: (0, j, i, 0)), out_shape=SDS((2, 2, t_len, w), BF16), name=name)(ag3, ag3, du)
    return out.reshape(ag3.shape)


def _gate_fwd(name, gp3, br, wb, l):
    _, t_len, d = gp3.shape
    tm = _tile(t_len, (512, 256))

    def body(gp_ref, br_ref, wb_ref, o_ref):
        o_ref[...] = _gate_f([gp_ref[n] for n in range(NH)], [br_ref[n] for n in range(NH)],
                             [wb_ref[n] for n in range(NH)]).astype(BF16)

    return pl.pallas_call(
        body, grid=(t_len // tm, d // BW),
        in_specs=[pl.BlockSpec((NH, tm, BW), lambda i, s: (0, i, s)), pl.BlockSpec((NH, tm, BW), lambda i, s: (0, i, 0)),
                  pl.BlockSpec((None, None, NH, BW, BW), lambda i, s: (l, s, 0, 0, 0))],
        out_specs=pl.BlockSpec((tm, BW), lambda i, s: (i, s)), out_shape=SDS((t_len, d), BF16), name=name)(gp3, br, wb)


def _gate_bwd(name, gp3, br, wb, l, dmerged):
    _, t_len, d = gp3.shape
    tm = _tile(t_len, (512, 256))
    ns = d // BW

    def body(gp_ref, br_ref, wb_ref, dm_ref, dgp_ref, dbr_ref, dwb_ref):
        i, s = pl.program_id(0), pl.program_id(1)
        gp = [gp_ref[n] for n in range(NH)]
        brv = [br_ref[n].astype(F32) for n in range(NH)]
        wbv = [wb_ref[n].astype(F32) for n in range(NH)]
        _, vjp = jax.vjp(_gate_f, gp, brv, wbv)
        dgp, dbr, dwb = vjp(dm_ref[...])

        @pl.when(s == 0)
        def _():
            dbr_ref[...] = jnp.zeros_like(dbr_ref)

        @pl.when((i == 0) & (s == 0))
        def _():
            dwb_ref[...] = jnp.zeros_like(dwb_ref)

        for n in range(NH):
            dgp_ref[n] = dgp[n].astype(BF16)
            dbr_ref[n] += dbr[n]
            dwb_ref[s, n] += dwb[n]

    return pl.pallas_call(
        body, grid=(t_len // tm, ns),
        in_specs=[pl.BlockSpec((NH, tm, BW), lambda i, s: (0, i, s)), pl.BlockSpec((NH, tm, BW), lambda i, s: (0, i, 0)),
                  pl.BlockSpec((None, None, NH, BW, BW), lambda i, s: (l, s, 0, 0, 0)), pl.BlockSpec((tm, BW), lambda i, s: (i, s))],
        out_specs=[pl.BlockSpec((NH, tm, BW), lambda i, s: (0, i, s)), pl.BlockSpec((NH, tm, BW), lambda i, s: (0, i, 0)),
                   pl.BlockSpec((ns, NH, BW, BW), lambda i, s: (0, 0, 0, 0))],
        out_shape=[SDS(gp3.shape, BF16), SDS((NH, t_len, BW), F32), SDS((ns, NH, BW, BW), F32)], name=name,
        compiler_params=_cp(VMEM_MM))(gp3, br, wb, dmerged)


def _loss_kernel(name, y, tgt):
    m, d = y.shape
    tm = _tile(m, (256,))

    def body(y_ref, t_ref, l_ref, dy_ref):
        err = y_ref[...] - t_ref[...]
        dy_ref[...] = err * (1.0 / d)

        @pl.when(pl.program_id(0) == 0)
        def _():
            l_ref[...] = jnp.zeros_like(l_ref)

        l_ref[...] += 0.5 * jnp.sum(jnp.mean(err * err, axis=-1, keepdims=True), axis=0, keepdims=True)

    row = pl.BlockSpec((tm, d), lambda i: (i, 0))
    return pl.pallas_call(body, grid=(m // tm,), in_specs=[row, row], out_specs=[pl.BlockSpec((1, 1), lambda i: (0, 0)), row],
                          out_shape=[SDS((1, 1), F32), SDS((m, d), F32)], name=name)(y, tgt)


def _zero_first(refs):
    @pl.when(pl.program_id(0) == 0)
    def _():
        for r in refs:
            r[...] = jnp.zeros_like(r)


def _grp(rows, j):
    return pl.BlockSpec((rows, BW), lambda a, j=j: (a, j))


_VEC = pl.BlockSpec((1, BW), lambda a: (0, 0))


def _ret_fwd(tag, proj, cos, sin, decf, decb, gn):
    t_len = proj.shape[0]
    n = t_len // CH
    tab = pl.BlockSpec((CH, BW), lambda a: (a, 0))
    st = pl.BlockSpec((None, BW, BW), lambda a: (a, 0, 0))

    def k1(k_ref, v_ref, cos_ref, sin_ref, df_ref, db_ref, kvf_ref, kvb_ref):
        kvf_ref[...], kvb_ref[...] = _ret_kv_f(k_ref[...], v_ref[...], cos_ref[...], sin_ref[...], df_ref[...], db_ref[...])

    kvf, kvb = pl.pallas_call(
        k1, grid=(n,), in_specs=[_grp(CH, 1), _grp(CH, 2), tab, tab, _VEC, _VEC], out_specs=[st, st],
        out_shape=[SDS((n, BW, BW), F32)] * 2, name=f"ret_kv_{tag}")(proj, proj, cos, sin, decf, decb)

    def k2(kvf_ref, kvb_ref, df_ref, db_ref, sf_ref, sb_ref):
        sf, sb = _ret_scan_f([kvf_ref[a] for a in range(n)], [kvb_ref[a] for a in range(n)], df_ref[...], db_ref[...])
        for a in range(n):
            sf_ref[a] = sf[a]
            sb_ref[a] = sb[a]

    stf, stb = pl.pallas_call(k2, out_shape=[SDS((n, BW, BW), F32)] * 2, name=f"ret_scan_{tag}",
                              compiler_params=_cp(VMEM_BIG))(kvf, kvb, decf, decb)

    def k3(q_ref, k_ref, v_ref, g_ref, cos_ref, sin_ref, sf_ref, sb_ref, df_ref, db_ref, gn_ref, o_ref):
        o_ref[...] = _ret_out_f(q_ref[...], k_ref[...], v_ref[...], g_ref[...], cos_ref[...], sin_ref[...], sf_ref[...],
                                sb_ref[...], df_ref[...], db_ref[...], gn_ref[...]).astype(BF16)

    ret = pl.pallas_call(
        k3, grid=(n,), in_specs=[_grp(CH, 0), _grp(CH, 1), _grp(CH, 2), _grp(CH, 3), tab, tab, st, st, _VEC, _VEC, _VEC],
        out_specs=tab, out_shape=SDS((t_len, BW), BF16), name=f"ret_out_{tag}")(proj, proj, proj, proj, cos, sin, stf, stb, decf, decb, gn)
    return ret, (kvf, kvb, stf, stb)


def _ret_bwd(tag, proj, cos, sin, decf, decb, gn, saved, dret):
    kvf, kvb, stf, stb = saved
    t_len = proj.shape[0]
    n = t_len // CH
    tab = pl.BlockSpec((CH, BW), lambda a: (a, 0))
    st = pl.BlockSpec((None, BW, BW), lambda a: (a, 0, 0))

    def k3(q_ref, k_ref, v_ref, g_ref, cos_ref, sin_ref, sf_ref, sb_ref, df_ref, db_ref, gn_ref, do_ref,
           dq_ref, dk_ref, dv_ref, dg_ref, dsf_ref, dsb_ref, ddf_ref, ddb_ref, dgn_ref):
        cos, sin = cos_ref[...], sin_ref[...]
        f = lambda q, k, v, g, sf, sb, df, db, gnv: _ret_out_f(q, k, v, g, cos, sin, sf, sb, df, db, gnv)
        _, vjp = jax.vjp(f, q_ref[...], k_ref[...], v_ref[...], g_ref[...], sf_ref[...], sb_ref[...], df_ref[...], db_ref[...], gn_ref[...])
        dq, dk, dv, dg, dsf, dsb, ddf, ddb, dgn = vjp(do_ref[...])
        dq_ref[...] = dq.astype(BF16)
        dk_ref[...] = dk
        dv_ref[...] = dv
        dg_ref[...] = dg.astype(BF16)
        dsf_ref[...] = dsf
        dsb_ref[...] = dsb
        _zero_first([ddf_ref, ddb_ref, dgn_ref])
        ddf_ref[...] += ddf
        ddb_ref[...] += ddb
        dgn_ref[...] += dgn

    dq, dk3, dv3, dg, dstf, dstb, ddf3, ddb3, dgn = pl.pallas_call(
        k3, grid=(n,),
        in_specs=[_grp(CH, 0), _grp(CH, 1), _grp(CH, 2), _grp(CH, 3), tab, tab, st, st, _VEC, _VEC, _VEC, tab],
        out_specs=[tab, tab, tab, tab, st, st, _VEC, _VEC, _VEC],
        out_shape=[SDS((t_len, BW), BF16), SDS((t_len, BW), F32), SDS((t_len, BW), F32), SDS((t_len, BW), BF16),
                   SDS((n, BW, BW), F32), SDS((n, BW, BW), F32), SDS((1, BW), F32), SDS((1, BW), F32), SDS((1, BW), F32)],
        name=f"ret_out_bwd_{tag}")(proj, proj, proj, proj, cos, sin, stf, stb, decf, decb, gn, dret)

    def k2(kvf_ref, kvb_ref, df_ref, db_ref, dsf_ref, dsb_ref, dkvf_ref, dkvb_ref, ddf_ref, ddb_ref):
        _, vjp = jax.vjp(_ret_scan_f, [kvf_ref[a] for a in range(n)], [kvb_ref[a] for a in range(n)], df_ref[...], db_ref[...])
        dkvf, dkvb, ddf_ref[...], ddb_ref[...] = vjp(([dsf_ref[a] for a in range(n)], [dsb_ref[a] for a in range(n)]))
        for a in range(n):
            dkvf_ref[a] = dkvf[a]
            dkvb_ref[a] = dkvb[a]

    dkvf, dkvb, ddf2, ddb2 = pl.pallas_call(
        k2, out_shape=[SDS((n, BW, BW), F32), SDS((n, BW, BW), F32), SDS((1, BW), F32), SDS((1, BW), F32)],
        name=f"ret_scan_bwd_{tag}", compiler_params=_cp(VMEM_BIG))(kvf, kvb, decf, decb, dstf, dstb)

    def k1(k_ref, v_ref, cos_ref, sin_ref, df_ref, db_ref, dkvf_ref, dkvb_ref, dk3_ref, dv3_ref, dk_ref, dv_ref, ddf_ref, ddb_ref):
        cos, sin = cos_ref[...], sin_ref[...]
        f = lambda k, v, df, db: _ret_kv_f(k, v, cos, sin, df, db)
        _, vjp = jax.vjp(f, k_ref[...], v_ref[...], df_ref[...], db_ref[...])
        dk, dv, ddf, ddb = vjp((dkvf_ref[...], dkvb_ref[...]))
        dk_ref[...] = (dk + dk3_ref[...]).astype(BF16)
        dv_ref[...] = (dv + dv3_ref[...]).astype(BF16)
        _zero_first([ddf_ref, ddb_ref])
        ddf_ref[...] += ddf
        ddb_ref[...] += ddb

    dk, dv, ddf1, ddb1 = pl.pallas_call(
        k1, grid=(n,), in_specs=[_grp(CH, 1), _grp(CH, 2), tab, tab, _VEC, _VEC, st, st, tab, tab],
        out_specs=[tab, tab, _VEC, _VEC],
        out_shape=[SDS((t_len, BW), BF16), SDS((t_len, BW), BF16), SDS((1, BW), F32), SDS((1, BW), F32)],
        name=f"ret_kv_bwd_{tag}")(proj, proj, cos, sin, decf, decb, dkvf, dkvb, dk3, dv3)
    return dq, dk, dv, dg, ddf1 + ddf2 + ddf3, ddb1 + ddb2 + ddb3, dgn


def _pool_fwd(tag, pvp, wbd, scale, t_len):
    def body(p_ref, w_ref, s_ref, o_ref):
        o_ref[...] = _pool_f(p_ref[...], w_ref[...], s_ref[...], t_len).astype(BF16)

    return pl.pallas_call(body, out_shape=SDS(pvp.shape, BF16), name=f"pool_{tag}", compiler_params=_cp(VMEM_BIG))(pvp, wbd, scale)


def _pool_bwd(tag, pvp, wbd, scale, t_len, dpool_p):
    def body(p_ref, w_ref, s_ref, do_ref, dp_ref, dw_ref, ds_ref):
        f = lambda p, w, s: _pool_f(p, w, s, t_len)
        _, vjp = jax.vjp(f, p_ref[...], w_ref[...], s_ref[...])
        dp, dw, ds = vjp(do_ref[...])
        dp_ref[...] = dp.astype(BF16)
        dw_ref[...] = dw
        ds_ref[...] = ds

    return pl.pallas_call(body, out_shape=[SDS(pvp.shape, BF16), SDS((BW, BW), F32), SDS((1, BW), F32)],
                          name=f"pool_bwd_{tag}", compiler_params=_cp(VMEM_BIG))(pvp, wbd, scale, dpool_p)


def _na_window(r, rows):
    r0 = jnp.clip(r - NA_ROWS // 2, 0, rows - NA_ROWS)
    return r0, r - r0


NA_PAIRS = 2 * NA_ROWS - 2


def _na_bias_pieces(b_ref, pat):
    return [[b_ref[h * NA_PAIRS + 2 * k - pat + NA_ROWS - 1] for k in range(NA_ROWS // 2)] for h in range(NH)]


def _na_keys(tag, proj, gk):
    t_len = proj.shape[0]
    tm = _tile(t_len, (256,))

    def body(k_ref, v_ref, gk_ref, kn_ref, vb_ref):
        kn_ref[...] = _head_rms(k_ref[...], gk_ref[...]).astype(BF16)
        vb_ref[...] = v_ref[...].astype(BF16)

    row = pl.BlockSpec((tm, BW), lambda a: (a, 0))
    return pl.pallas_call(body, grid=(t_len // tm,), in_specs=[_grp(tm, 6), _grp(tm, 7), _VEC], out_specs=[row, row],
                          out_shape=[SDS((t_len, BW), BF16)] * 2, name=f"na_keys_{tag}")(proj, proj, gk)


def _na_keys_bwd(tag, proj, gk, dkn, dv):
    t_len = proj.shape[0]
    tm = _tile(t_len, (256,))

    def body(k_ref, gk_ref, dkn_ref, dv_ref, dk_ref, dvb_ref, dgk_ref):
        _, vjp = jax.vjp(_head_rms, k_ref[...], gk_ref[...])
        dk, dgk = vjp(dkn_ref[...])
        dk_ref[...] = dk.astype(BF16)
        dvb_ref[...] = dv_ref[...].astype(BF16)
        _zero_first([dgk_ref])
        dgk_ref[...] += dgk

    row = pl.BlockSpec((tm, BW), lambda a: (a, 0))
    return pl.pallas_call(body, grid=(t_len // tm,), in_specs=[_grp(tm, 6), _VEC, row, row], out_specs=[row, row, _VEC],
                          out_shape=[SDS((t_len, BW), BF16), SDS((t_len, BW), BF16), SDS((1, BW), F32)],
                          name=f"na_keys_bwd_{tag}")(proj, gk, dkn, dv)


def _na_fwd(tag, proj, kn, vb, bias_tab, gq):
    t_len = proj.shape[0]
    rows = t_len // GRID_W
    win = NA_ROWS * GRID_W

    def body(q_ref, k_ref, v_ref, b_ref, gq_ref, o_ref):
        r0, pat = _na_window(pl.program_id(0), rows)
        start = pl.multiple_of(r0 * GRID_W, GRID_W)
        o_ref[...] = _na_f(q_ref[...], k_ref[pl.ds(start, win), :], v_ref[pl.ds(start, win), :], _na_bias_pieces(b_ref, pat),
                           gq_ref[...]).astype(BF16)

    whole = pl.BlockSpec((t_len, BW), lambda a: (0, 0))
    return pl.pallas_call(
        body, grid=(rows,),
        in_specs=[_grp(GRID_W, 5), whole, whole, pl.BlockSpec(bias_tab.shape, lambda a: (0, 0, 0)), _VEC],
        out_specs=pl.BlockSpec((GRID_W, BW), lambda a: (a, 0)), out_shape=SDS((t_len, BW), BF16),
        name=f"na_{tag}", compiler_params=_cp(VMEM_MM))(proj, kn, vb, bias_tab, gq)


def _na_bwd(tag, proj, kn, vb, bias_tab, gq, dna):
    t_len = proj.shape[0]
    rows = t_len // GRID_W
    win = NA_ROWS * GRID_W

    def body(q_ref, k_ref, v_ref, b_ref, gq_ref, do_ref, dq_ref, dk_ref, dv_ref, db_ref, dgq_ref):
        r0, pat = _na_window(pl.program_id(0), rows)
        start = pl.multiple_of(r0 * GRID_W, GRID_W)
        _zero_first([dk_ref, dv_ref, db_ref, dgq_ref])
        _, vjp = jax.vjp(_na_f, q_ref[...], k_ref[pl.ds(start, win), :].astype(F32), v_ref[pl.ds(start, win), :].astype(F32),
                         _na_bias_pieces(b_ref, pat), gq_ref[...])
        dq, dk, dv, db, dgq = vjp(do_ref[...])
        dq_ref[...] = dq.astype(BF16)
        dk_ref[pl.ds(start, win), :] += dk
        dv_ref[pl.ds(start, win), :] += dv
        for h in range(NH):
            for k in range(NA_ROWS // 2):
                db_ref[h * NA_PAIRS + 2 * k - pat + NA_ROWS - 1] += db[h][k]
        dgq_ref[...] += dgq

    whole = pl.BlockSpec((t_len, BW), lambda a: (0, 0))
    tabspec = pl.BlockSpec(bias_tab.shape, lambda a: (0, 0, 0))
    row = pl.BlockSpec((GRID_W, BW), lambda a: (a, 0))
    return pl.pallas_call(
        body, grid=(rows,), in_specs=[_grp(GRID_W, 5), whole, whole, tabspec, _VEC, row],
        out_specs=[row, whole, whole, tabspec, _VEC],
        out_shape=[SDS((t_len, BW), BF16), SDS((t_len, BW), F32), SDS((t_len, BW), F32), SDS(bias_tab.shape, F32), SDS((1, BW), F32)],
        name=f"na_bwd_{tag}", compiler_params=_cp(VMEM_BIG))(proj, kn, vb, bias_tab, gq, dna)


def _mem_fwd(tag, proj, memkv, gq, gk):
    t_len = proj.shape[0]
    n_mem = memkv.shape[0]
    tm = _tile(t_len, (256,))

    def body(q_ref, mk_ref, mv_ref, gq_ref, gk_ref, o_ref):
        o_ref[...] = _mem_f(q_ref[...], mk_ref[...], mv_ref[...], gq_ref[...], gk_ref[...]).astype(BF16)

    mspec = lambda j: pl.BlockSpec((n_mem, BW), lambda a, j=j: (0, j))
    return pl.pallas_call(
        body, grid=(t_len // tm,), in_specs=[_grp(tm, 8), mspec(0), mspec(1), _VEC, _VEC],
        out_specs=pl.BlockSpec((tm, BW), lambda a: (a, 0)), out_shape=SDS((t_len, BW), BF16), name=f"mem_{tag}")(proj, memkv, memkv, gq, gk)


def _mem_bwd(tag, proj, memkv, gq, gk, dmo):
    t_len = proj.shape[0]
    n_mem = memkv.shape[0]
    tm = _tile(t_len, (256,))

    def body(q_ref, mk_ref, mv_ref, gq_ref, gk_ref, do_ref, dq_ref, dmk_ref, dmv_ref, dgq_ref, dgk_ref):
        _zero_first([dmk_ref, dmv_ref, dgq_ref, dgk_ref])
        _, vjp = jax.vjp(_mem_f, q_ref[...], mk_ref[...], mv_ref[...], gq_ref[...], gk_ref[...])
        dq, dmk, dmv, dgq, dgk = vjp(do_ref[...])
        dq_ref[...] = dq.astype(BF16)
        dmk_ref[...] += dmk
        dmv_ref[...] += dmv
        dgq_ref[...] += dgq
        dgk_ref[...] += dgk

    mspec = lambda j: pl.BlockSpec((n_mem, BW), lambda a, j=j: (0, j))
    mout = pl.BlockSpec((n_mem, BW), lambda a: (0, 0))
    row = pl.BlockSpec((tm, BW), lambda a: (a, 0))
    dq, dmk, dmv, dgq, dgk = pl.pallas_call(
        body, grid=(t_len // tm,), in_specs=[_grp(tm, 8), mspec(0), mspec(1), _VEC, _VEC, row],
        out_specs=[row, mout, mout, _VEC, _VEC],
        out_shape=[SDS((t_len, BW), BF16), SDS((n_mem, BW), F32), SDS((n_mem, BW), F32), SDS((1, BW), F32), SDS((1, BW), F32)],
        name=f"mem_bwd_{tag}")(proj, memkv, memkv, gq, gk, dmo)
    return dq, jnp.concatenate([dmk, dmv], axis=1), dgq, dgk


NA_NJ = 2 * NA_COLS - 1


def _na_onehot():
    q = np.arange(GRID_W)[:, None]
    kc = np.arange(GRID_W)[None, :]
    qwin = np.clip(q - NA_COLS // 2, 0, GRID_W - NA_COLS)
    mask = (kc >= qwin) & (kc < qwin + NA_COLS)
    col = np.clip(kc - q, -(NA_COLS - 1), NA_COLS - 1) + NA_COLS - 1
    onehot = np.zeros((LANES, GRID_W, 2, GRID_W), np.float32)
    qq, kk = np.nonzero(mask)
    for half in range(2):
        onehot[half * NA_NJ + col[qq, kk], qq, half, kk] = 1.0
    valid = np.broadcast_to(mask[:, None, :], (GRID_W, 2, GRID_W)).astype(np.float32)
    return onehot.reshape(LANES, -1), valid.reshape(1, -1)


def _na_pair_rows(rpb):
    n_layers = rpb.shape[0]
    pair = jnp.concatenate([rpb[:, :, :-1], rpb[:, :, 1:]], axis=-1).reshape(n_layers, NH * NA_PAIRS, 2 * NA_NJ)
    return jnp.pad(pair, ((0, 0), (0, 0), (0, LANES - 2 * NA_NJ)))


def _na_bias_table(rpb):
    n_layers = rpb.shape[0]
    onehot, valid = _na_onehot()
    width = onehot.shape[1]

    def body(r_ref, oh_ref, ok_ref, o_ref):
        t = jnp.dot(r_ref[...], oh_ref[...], precision=lax.Precision.HIGHEST, preferred_element_type=F32)
        o_ref[...] = jnp.where(ok_ref[...] > 0, t, NEG_INF)

    out = pl.pallas_call(
        body, grid=(n_layers,),
        in_specs=[pl.BlockSpec((None, NH * NA_PAIRS, LANES), lambda l: (l, 0, 0)), pl.BlockSpec((LANES, width), lambda l: (0, 0)),
                  pl.BlockSpec((1, width), lambda l: (0, 0))],
        out_specs=pl.BlockSpec((None, NH * NA_PAIRS, width), lambda l: (l, 0, 0)),
        out_shape=SDS((n_layers, NH * NA_PAIRS, width), F32), name="na_bias_table")(
            _na_pair_rows(rpb), jnp.asarray(onehot), jnp.asarray(valid))
    return out.reshape(n_layers, NH * NA_PAIRS, GRID_W, 2 * GRID_W)


def _na_bias_grad(dtab):
    n_layers = dtab.shape[0]
    onehot, _ = _na_onehot()
    width = onehot.shape[1]

    def body(x_ref, oh_ref, o_ref):
        o_ref[...] = jnp.dot(x_ref[...], oh_ref[...], precision=lax.Precision.HIGHEST, preferred_element_type=F32)

    out = pl.pallas_call(
        body, grid=(n_layers,),
        in_specs=[pl.BlockSpec((None, NH * NA_PAIRS, width), lambda l: (l, 0, 0)), pl.BlockSpec((width, LANES), lambda l: (0, 0))],
        out_specs=pl.BlockSpec((None, NH * NA_PAIRS, LANES), lambda l: (l, 0, 0)),
        out_shape=SDS((n_layers, NH * NA_PAIRS, LANES), F32), name="na_bias_grad")(
            dtab.reshape(n_layers, NH * NA_PAIRS, width), jnp.asarray(onehot.T.copy()))
    out = out.reshape(n_layers, NH, NA_PAIRS, LANES)
    zero = jnp.zeros((n_layers, NH, 1, NA_NJ), F32)
    return (jnp.concatenate([out[..., :NA_NJ], zero], axis=2) + jnp.concatenate([zero, out[..., NA_NJ:2 * NA_NJ]], axis=2))


def _place():
    x, y, c = lax.axis_index("x"), lax.axis_index("y"), lax.axis_index("c")
    chips = [(1 - x, y), (x, 1 - y), (1 - x, 1 - y)]
    return x, y, c, chips


def _remote(src, dst, ssem, rsem, dev):
    return pltpu.make_async_remote_copy(src_ref=src, dst_ref=dst, send_sem=ssem, recv_sem=rsem, device_id=dev, device_id_type=MESH)


HBM = pl.BlockSpec(memory_space=pltpu.HBM)
SEM = pl.BlockSpec(memory_space=pltpu.SEMAPHORE)
DATAFLOW = pltpu.SideEffectType.DATAFLOW_SIDE_EFFECTING


def _split_copies(name, srcs, lands, n_copies, plan):
    n_s, n_l = len(srcs), len(lands)
    hbm = lambda a: pltpu.HBM(a.shape, a.dtype)

    def start_body(*refs):
        ins = refs[:n_s + n_l]
        outs = refs[n_s + n_l:]
        sems, token = outs[:2 * n_copies], outs[-1]
        for k, (src, dst, dev) in enumerate(plan(ins[:n_s], ins[n_s:])):
            _remote(src, dst, sems[k], sems[n_copies + k], dev).start()
        token[...] = jnp.zeros_like(token)

    outs = pl.pallas_call(
        start_body, name=f"{name}_start",
        out_shape=(pltpu.SemaphoreType.DMA(()),) * (2 * n_copies) + tuple(hbm(a) for a in (*srcs, *lands)) + (SDS((8, LANES), F32),),
        in_specs=(HBM,) * (n_s + n_l), out_specs=(SEM,) * (2 * n_copies) + (HBM,) * (n_s + n_l) + (pl.BlockSpec(memory_space=pltpu.VMEM),),
        input_output_aliases={i: 2 * n_copies + i for i in range(n_s + n_l)},
        compiler_params=pltpu.CompilerParams(has_side_effects=DATAFLOW),
    )(*[pltpu.with_memory_space_constraint(a, pltpu.HBM) for a in (*srcs, *lands)])
    sems, thru, token = outs[:2 * n_copies], outs[2 * n_copies:-1], outs[-1]

    def wait(after_wait):
        def wait_body(*refs):
            ins = refs[:n_s + n_l]
            sem_refs = refs[n_s + n_l:n_s + n_l + 2 * n_copies]
            for k, (src, dst, dev) in enumerate(plan(ins[:n_s], ins[n_s:])):
                cp = _remote(src, dst, sem_refs[k], sem_refs[n_copies + k], dev)
                cp.wait_send()
                cp.wait_recv()

        res = pl.pallas_call(
            wait_body, name=f"{name}_wait", out_shape=tuple(hbm(a) for a in (*srcs, *lands)),
            in_specs=(HBM,) * (n_s + n_l) + (SEM,) * (2 * n_copies) + (ANY,), out_specs=(HBM,) * (n_s + n_l),
            input_output_aliases={i: i for i in range(n_s + n_l)},
            compiler_params=pltpu.CompilerParams(has_side_effects=DATAFLOW),
        )(*thru, *sems, after_wait)
        return list(res[:n_s]), list(res[n_s:])

    return token, wait


def _gather_plan(srcs, lands):
    x, y, c, chips = _place()
    me = 2 * x + y
    out = []
    for src, land in zip(srcs, lands):
        h = src.shape[0] // 2
        out.append((src, land.at[me], (x, y, 1 - c)))
        for px, py in chips:
            out.append((src.at[pl.ds(c * h, h)], land.at[me, pl.ds(c * h, h)], (px, py, c)))
    return out


def _scatter_plan(srcs, lands):
    x, y, c, chips = _place()
    out = []
    for src, land in zip(srcs, lands):
        for r, (px, py) in enumerate(chips):
            out.append((src.at[2 * px + py], land.at[r], (px, py, c)))
    return out


def _pass_plan(srcs, lands):
    x, y, c, chips = _place()
    out = []
    for land in lands:
        h = land.shape[1] // 2
        for px, py in chips:
            piece = land.at[2 * px + py, pl.ds(c * h, h)]
            out.append((piece, piece, (x, y, 1 - c)))
    return out


def _pair_plan(srcs, lands):
    x, y, c, _ = _place()
    out = []
    for src, land in zip(srcs, lands):
        h = land.shape[1]
        out.append((src.at[:, pl.ds((1 - c) * h, h)], land, (x, y, 1 - c)))
    return out


def _half_plan(srcs, lands):
    x, y, c, _ = _place()
    out = []
    for land in lands:
        h = land.shape[0] // 2
        mine = land.at[pl.ds(c * h, h)]
        out.append((mine, mine, (x, y, 1 - c)))
    return out


def _forward_halves(name, lands):
    n = len(lands)

    def body(*refs):
        ins, outs = refs[:n], refs[n:2 * n]
        ssem, rsem = refs[2 * n:]
        x, y, c, chips = _place()
        sib = (x, y, 1 - c)

        def copy(i, r, half):
            h = ins[i].shape[1] // 2
            px, py = chips[r]
            rows = pl.ds(half * h, h)
            return _remote(ins[i].at[2 * px + py, rows], outs[i].at[2 * px + py, rows], ssem.at[i, r], rsem.at[i, r], sib)

        for i in range(n):
            for r in range(3):
                copy(i, r, c).start()
        for i in range(n):
            for r in range(3):
                copy(i, r, 1 - c).wait_recv()
                copy(i, r, c).wait_send()

    return pl.pallas_call(
        body, in_specs=[ANY] * n, out_specs=[ANY] * n, out_shape=[SDS(a.shape, a.dtype) for a in lands],
        input_output_aliases={i: i for i in range(n)},
        scratch_shapes=[pltpu.SemaphoreType.DMA((n, 3)), pltpu.SemaphoreType.DMA((n, 3))], name=name)(*lands)


def _pair_exchange(name, gs):
    n = len(gs)

    def body(*refs):
        ins, outs = refs[:n], refs[n:2 * n]
        ssem, rsem = refs[2 * n:]
        x, y, c, _ = _place()
        cps = []
        for i in range(n):
            h = ins[i].shape[1] // 2
            cps.append(_remote(ins[i].at[:, pl.ds((1 - c) * h, h)], outs[i], ssem.at[i], rsem.at[i], (x, y, 1 - c)))
            cps[-1].start()
        for cp in cps:
            cp.wait()

    return pl.pallas_call(
        body, in_specs=[ANY] * n, out_specs=[ANY] * n,
        out_shape=[SDS((g.shape[0], g.shape[1] // 2, g.shape[2]), g.dtype) for g in gs],
        scratch_shapes=[pltpu.SemaphoreType.DMA((n,)), pltpu.SemaphoreType.DMA((n,))], name=name)(*gs)


def _half_exchange(name, fs):
    n = len(fs)

    def body(*refs):
        ins, outs = refs[:n], refs[n:2 * n]
        ssem, rsem = refs[2 * n:]
        x, y, c, _ = _place()

        def copy(i, half):
            h = ins[i].shape[0] // 2
            return _remote(ins[i].at[pl.ds(half * h, h)], outs[i].at[pl.ds(half * h, h)], ssem.at[i], rsem.at[i], (x, y, 1 - c))

        for i in range(n):
            copy(i, c).start()
        for i in range(n):
            copy(i, 1 - c).wait_recv()
            copy(i, c).wait_send()

    return pl.pallas_call(
        body, in_specs=[ANY] * n, out_specs=[ANY] * n, out_shape=[SDS(f.shape, f.dtype) for f in fs],
        input_output_aliases={i: i for i in range(n)},
        scratch_shapes=[pltpu.SemaphoreType.DMA((n,)), pltpu.SemaphoreType.DMA((n,))], name=name)(*fs)


def _gather_small(v):
    def body(v_ref, o_ref, lsem, ssem, rsem):
        x, y, c, _ = _place()
        me = 4 * x + 2 * y + c
        flips = [(fx, fy, fc) for fx in (0, 1) for fy in (0, 1) for fc in (0, 1)][1:]
        peer = lambda f: (1 - x if f[0] else x, 1 - y if f[1] else y, 1 - c if f[2] else c)
        loc = pltpu.make_async_copy(v_ref, o_ref.at[me], lsem)
        loc.start()
        cps = [_remote(v_ref, o_ref.at[me], ssem.at[k], rsem.at[k], peer(f)) for k, f in enumerate(flips)]
        for cp in cps:
            cp.start()
        for k, f in enumerate(flips):
            px, py, pc = peer(f)
            land = o_ref.at[4 * px + 2 * py + pc]
            _remote(land, land, ssem.at[k], rsem.at[k], peer(f)).wait_recv()
        for cp in cps:
            cp.wait_send()
        loc.wait()

    return pl.pallas_call(
        body, in_specs=[ANY], out_specs=ANY, out_shape=SDS((N_DEV,) + v.shape, v.dtype),
        scratch_shapes=[pltpu.SemaphoreType.DMA, pltpu.SemaphoreType.DMA((N_DEV - 1,)), pltpu.SemaphoreType.DMA((N_DEV - 1,))],
        name="gather_small")(v)


def _rows_tile(r):
    return _tile(r, (256, 352, 128, 64))


def _pair_sum(name, g, r1, core, me):
    n_l, s_n, h, c = r1.shape
    tr = _rows_tile(h)
    nb = h // tr

    def body(idx_ref, g_ref, r_ref, pb_ref, own_ref):
        p = g_ref[...] + r_ref[...]
        pb_ref[...] = p.astype(BF16)

        @pl.when(pl.program_id(2) == idx_ref[1])
        def _():
            own_ref[...] = p

    blk = (None, None, tr, c)
    grid_spec = pltpu.PrefetchScalarGridSpec(
        num_scalar_prefetch=1, grid=(n_l, nb, s_n),
        in_specs=[pl.BlockSpec(blk, lambda a, i, s, idx: (a, s, idx[0] * nb + i, 0)), pl.BlockSpec(blk, lambda a, i, s, idx: (a, s, i, 0))],
        out_specs=[pl.BlockSpec(blk, lambda a, i, s, idx: (a, s, i, 0)), pl.BlockSpec((None, tr, c), lambda a, i, s, idx: (a, i, 0))])
    idx = jnp.stack([core, me]).astype(jnp.int32)
    return pl.pallas_call(body, grid_spec=grid_spec, out_shape=[SDS(r1.shape, BF16), SDS((n_l, h, c), F32)], name=name)(idx, g, r1)


def _chip_sum(name, own, r2, core):
    n_l, h, c = own.shape
    tr = _rows_tile(h)
    nb = h // tr

    def body(idx_ref, o_ref, r_ref, f_ref):
        f_ref[...] = ((o_ref[...] + r_ref[0].astype(F32)) + r_ref[1].astype(F32)) + r_ref[2].astype(F32)

    grid_spec = pltpu.PrefetchScalarGridSpec(
        num_scalar_prefetch=1, grid=(n_l, nb),
        in_specs=[pl.BlockSpec((None, tr, c), lambda a, i, idx: (a, i, 0)), pl.BlockSpec((3, None, tr, c), lambda a, i, idx: (0, a, i, 0))],
        out_specs=pl.BlockSpec((None, tr, c), lambda a, i, idx: (a, idx[0] * nb + i, 0)))
    return pl.pallas_call(body, grid_spec=grid_spec, out_shape=SDS((n_l, 2 * h, c), F32), name=name)(
        jnp.reshape(core, (1,)).astype(jnp.int32), own, r2)


def _adamw_math(w, g, m, v):
    m = ADAM_B1 * m + (1.0 - ADAM_B1) * g
    v = ADAM_B2 * v + (1.0 - ADAM_B2) * jnp.square(g)
    m_hat = m / (1.0 - ADAM_B1 ** ADAM_STEP)
    v_hat = v / (1.0 - ADAM_B2 ** ADAM_STEP)
    delta = -ADAM_LR * (m_hat / (jnp.sqrt(v_hat) + ADAM_EPS) + ADAM_WD * w)
    return delta, m, v


def _adamw(name, l, w, g, m, v, bufs):
    _, r, c = w.shape
    tr = _rows_tile(r)

    def body(w_ref, g_ref, m_ref, v_ref, *rest):
        go_ref, d_ref, nm_ref, nv_ref = rest[4:]
        g = g_ref[...]
        go_ref[...] = g
        d_ref[...], nm_ref[...], nv_ref[...] = _adamw_math(w_ref[...], g, m_ref[...], v_ref[...])

    blk = pl.BlockSpec((None, tr, c), lambda i: (l, i, 0))
    return pl.pallas_call(body, grid=(r // tr,), in_specs=[blk, pl.BlockSpec((tr, c), lambda i: (i, 0)), blk, blk] + [ANY] * 4,
                          out_specs=[blk] * 4, out_shape=[SDS(w.shape, F32)] * 4, input_output_aliases={4 + k: k for k in range(4)},
                          name=name)(w, g, m, v, *bufs)


def _small_update(parts, w, m, v):
    def body(p_ref, w_ref, m_ref, v_ref, g_ref, d_ref, nm_ref, nv_ref):
        g = p_ref[0]
        for d in range(1, N_DEV):
            g = g + p_ref[d]
        g_ref[...] = g
        d_ref[...], nm_ref[...], nv_ref[...] = _adamw_math(w_ref[...], g, m_ref[...], v_ref[...])

    return pl.pallas_call(body, out_shape=[SDS(w.shape, F32)] * 4, name="small_update")(parts, w, m, v)


SMALL = ["norm_mix_g", "norm_mem_g", "ret_decay_fwd", "ret_decay_bwd", "ret_norm_g", "pool_w", "pool_scale", "na_q_norm_g",
         "na_k_norm_g", "na_rpb", "mem_q_norm_g", "mem_k_norm_g", "norm_ffn_g"]
BIG = ["w_in", "w_gate", "w_mem_kv", "w_branch", "w_out", "w_ffn_in", "w_ffn_out"]
GROUPS = (("w_in", "w_gate", "w_mem_kv", "w_branch", "w_out"), ("w_ffn_in", "w_ffn_out"))
ORDER = ["norm_mix_g", "norm_mem_g", "w_in", "w_gate", "ret_decay_fwd", "ret_decay_bwd", "ret_norm_g", "pool_w", "pool_scale",
         "na_q_norm_g", "na_k_norm_g", "na_rpb", "mem_q_norm_g", "mem_k_norm_g", "w_mem_kv", "w_branch", "w_out", "norm_ffn_g",
         "w_ffn_in", "w_ffn_out"]


def _pack(arrs):
    rows = []
    for a in arrs:
        f = a.reshape(-1).astype(F32)
        pad = (-f.shape[0]) % (8 * LANES)
        rows.append(jnp.pad(f, (0, pad)).reshape(-1, LANES))
    return jnp.concatenate(rows, axis=0)


def _unpack(p, like):
    out, at = [], 0
    for a in like:
        n = int(np.prod(a.shape))
        rows = -(-n // (8 * LANES)) * 8
        out.append(p[at:at + rows].reshape(-1)[:n].reshape(a.shape))
        at += rows
    return out


def _rope_tables(t_len):
    half = HEAD // 2
    inv = ROPE_THETA ** (-jnp.arange(half, dtype=F32) / half)
    ang = jnp.arange(t_len, dtype=F32)[:, None] * inv[None, :]
    cos, sin = jnp.cos(ang), jnp.sin(ang)
    return jnp.tile(jnp.concatenate([cos, cos], axis=1), (1, NH)), jnp.tile(jnp.concatenate([-sin, sin], axis=1), (1, NH))


def _block_diag(pw):
    out = jnp.zeros((BW, BW), pw.dtype)
    for g in range(NH):
        out = out.at[g * HEAD:(g + 1) * HEAD, g * HEAD:(g + 1) * HEAD].set(pw[g])
    return out


def _fold_heads(v):
    return v.reshape(NH, HEAD).sum(axis=0)


def _fold_lanes(v):
    return v.reshape(NH, HEAD).sum(axis=1)


def kernel(x, mem, norm_mix_g, norm_mem_g, w_in, w_gate, ret_decay_fwd, ret_decay_bwd, ret_norm_g, pool_w, pool_scale, na_q_norm_g, na_k_norm_g, na_rpb, mem_q_norm_g, mem_k_norm_g, w_mem_kv, w_branch, w_out, norm_ffn_g, w_ffn_in, w_ffn_out, loss_target, m_norm_mix_g, m_norm_mem_g, m_w_in, m_w_gate, m_ret_decay_fwd, m_ret_decay_bwd, m_ret_norm_g, m_pool_w, m_pool_scale, m_na_q_norm_g, m_na_k_norm_g, m_na_rpb, m_mem_q_norm_g, m_mem_k_norm_g, m_w_mem_kv, m_w_branch, m_w_out, m_norm_ffn_g, m_w_ffn_in, m_w_ffn_out, v_norm_mix_g, v_norm_mem_g, v_w_in, v_w_gate, v_ret_decay_fwd, v_ret_decay_bwd, v_ret_norm_g, v_pool_w, v_pool_scale, v_na_q_norm_g, v_na_k_norm_g, v_na_rpb, v_mem_q_norm_g, v_mem_k_norm_g, v_w_mem_kv, v_w_branch, v_w_out, v_norm_ffn_g, v_w_ffn_in, v_w_ffn_out):
    args = dict(locals())
    W = {n: args[n] for n in ORDER}
    M = {n: args["m_" + n] for n in ORDER}
    V = {n: args["v_" + n] for n in ORDER}
    n_layers, d_model = norm_mix_g.shape
    assert x.shape[0] == 1
    t_len = x.shape[1]
    xc, yc, cc = lax.axis_index("x"), lax.axis_index("y"), lax.axis_index("c")
    me_chip = 2 * xc + yc

    def as3(a):
        return a.reshape(a.shape[0], -1, a.shape[-1])

    gather_waits, token = {}, jnp.zeros((), F32)
    for l in range(n_layers):
        for gi, names in enumerate(GROUPS):
            srcs = [as3(W[n])[l].astype(BF16) for n in names]
            lands = [lax.empty((N_CHIPS,) + a.shape, BF16) for a in srcs]
            tok, gather_waits[l, gi] = _split_copies(f"gather_l{l}_g{gi}", srcs, lands, 4 * len(names), _gather_plan)
            token = token + tok[0, 0]

    cos, sin = _rope_tables(t_len)
    btabs = _na_bias_table(na_rpb)
    lane = lambda a: a.reshape(1, -1).astype(F32)
    x2d, mem2d, tgt = x[0], mem[0], loss_target[0]
    norm_mix_g = norm_mix_g + token

    def fetch(l, gi, after):
        lands = gather_waits[l, gi](after)[1]
        return _split_copies(f"pass_l{l}_g{gi}", [], lands, 3 * len(GROUPS[gi]), _pass_plan)

    saved = []
    _, passing = fetch(0, 0, x2d)
    for l in range(n_layers):
        tag = f"l{l}"
        gathered = dict(zip(GROUPS[0], passing(x2d)[1]))
        wi = gathered["w_in"].transpose(1, 0, 2).reshape(1, 1, d_model, -1)
        wg = gathered["w_gate"][None]
        wmkv = gathered["w_mem_kv"].reshape(1, 1, d_model, 2 * BW)
        wb = gathered["w_branch"].reshape(1, N_CHIPS, NH, BW, BW)
        wo = gathered["w_out"].reshape(1, 1, d_model, d_model)
        s = {"x": x2d}
        s["decf"], s["decb"] = lane(jnp.repeat(ret_decay_fwd[l], HEAD)), lane(jnp.repeat(ret_decay_bwd[l], HEAD))
        s["gn"], s["pscale"] = lane(ret_norm_g[l]), lane(pool_scale[l])
        s["wbd"] = _block_diag(pool_w[l])
        s["nagq"], s["nagk"] = lane(jnp.tile(na_q_norm_g[l], NH)), lane(jnp.tile(na_k_norm_g[l], NH))
        s["mgq"], s["mgk"] = lane(jnp.tile(mem_q_norm_g[l], NH)), lane(jnp.tile(mem_k_norm_g[l], NH))
        s["btab"] = btabs[l]
        s["h"] = _rms_fwd(f"rms_mix_{tag}", x2d, lane(norm_mix_g[l]))
        s["proj"] = _mm_nn(f"mm_in_{tag}", s["h"], wi, 0)
        s["gp3"] = _mm_nn(f"mm_gate_{tag}", s["h"], wg, 0, out3d=True)
        tok, passing = fetch(l, 1, s["gp3"])
        s["hm"] = _rms_fwd(f"rms_mem_{tag}", mem2d, lane(norm_mem_g[l]) + tok[0, 0])
        s["memkv"] = _mm_nn(f"mm_memkv_{tag}", s["hm"], wmkv, 0)
        ret, s["ret_saved"] = _ret_fwd(tag, s["proj"], cos, sin, s["decf"], s["decb"], s["gn"])
        s["pvp"] = jnp.pad(s["proj"][:, 4 * BW:5 * BW], ((POOL_PAD, POOL_PAD), (0, 0)))
        pool = _pool_fwd(tag, s["pvp"], s["wbd"], s["pscale"], t_len)[POOL_PAD:POOL_PAD + t_len]
        s["kn"], s["vb"] = _na_keys(tag, s["proj"], s["nagk"])
        na = _na_fwd(tag, s["proj"], s["kn"], s["vb"], s["btab"], s["nagq"])
        mo = _mem_fwd(tag, s["proj"], s["memkv"], s["mgq"], s["mgk"])
        s["br"] = jnp.stack([ret, pool, na, mo])
        s["merged"] = _gate_fwd(f"gate_{tag}", s["gp3"], s["br"], wb, 0)
        s["x2"] = _mm_nn(f"mm_out_{tag}", s["merged"], wo, 0, add=x2d)
        gathered = dict(zip(GROUPS[1], passing(s["x2"])[1]))
        pin_ffn = jnp.zeros((), F32)
        if l + 1 < n_layers:
            tok, passing = fetch(l + 1, 0, s["x2"])
            pin_ffn = tok[0, 0]
        wfi = gathered["w_ffn_in"][None]
        wfo = gathered["w_ffn_out"].reshape(1, 1, -1, d_model)
        s["w"] = (wi, wg, wmkv, wb, wo, wfi, wfo)
        s["h2"] = _rms_fwd(f"rms_ffn_{tag}", s["x2"], lane(norm_ffn_g[l]) + pin_ffn)
        s["ag3"] = _mm_nn(f"mm_ffn_in_{tag}", s["h2"], wfi, 0, out3d=True)
        s["u"] = _swiglu_fwd(f"swiglu_{tag}", s["ag3"])
        x2d = _mm_nn(f"mm_ffn_out_{tag}", s["u"], wfo, 0, add=s["x2"])
        saved.append(s)

    loss_part, dx = _loss_kernel("loss", x2d, tgt)

    dbtabs = [None] * n_layers
    gsmall = {n: [None] * n_layers for n in SMALL}
    opt_view = {n: (lambda a, n=n: jnp.swapaxes(as3(a), 1, 2) if n == "w_in" else as3(a)) for n in BIG}
    out_bufs = {n: tuple(lax.empty(opt_view[n](W[n]).shape, F32) for _ in range(4)) for n in BIG}

    def reduce_group(l, gi, grads):
        names = GROUPS[gi]
        gs = [grads[n] for n in names]
        lands = [lax.empty((N_CHIPS, g.shape[1] // 2, g.shape[2]), F32) for g in gs]
        tok0, wait0 = _split_copies(f"pairx_l{l}_g{gi}", gs, lands, len(names), _pair_plan)

        def sums_and_scatter(after):
            mine, theirs = wait0(after)
            sums = [_pair_sum(f"pair_sum_{n}_l{l}", g[None], r[None], cc, me_chip) for n, g, r in zip(names, mine, theirs)]
            pbs = [p[0] for p, _ in sums]
            lands1 = [lax.empty((3,) + p.shape[1:], BF16) for p in pbs]
            tok1, wait1 = _split_copies(f"scatter_l{l}_g{gi}", pbs, lands1, 3 * len(names), _scatter_plan)

            def chip_sums_and_halves(after):
                halves = [_chip_sum(f"chip_sum_{n}_l{l}", own, r[:, None], cc)[0] for n, (_, own), r in zip(names, sums, wait1(after)[1])]
                tok2, wait2 = _split_copies(f"halfx_l{l}_g{gi}", [], halves, len(names), _half_plan)

                def update(after):
                    for n, f in zip(names, wait2(after)[1]):
                        out_bufs[n] = _adamw(f"adamw_{n}_l{l}", l, opt_view[n](W[n]), f, opt_view[n](M[n]), opt_view[n](V[n]), out_bufs[n])
                    return None, None

                return update, tok2

            return chip_sums_and_halves, tok1

        return sums_and_scatter, tok0

    def advance(stages, new):
        nxt, tok = new
        out, toks = [nxt], tok
        for stage in stages:
            nxt, tok = stage(toks)
            if nxt is not None:
                out.append(nxt)
                toks = toks + tok
        return out, toks

    stages, order_after = [], None
    for l in reversed(range(n_layers)):
        tag = f"l{l}"
        s = saved[l]
        wi, wg, wmkv, wb, wo, wfi, wfo = s["w"]
        du = _mm_nt(f"mm_ffn_out_dx_{tag}", dx, wfo, 0, after=order_after)
        g_wfo = _mm_tn(f"mm_ffn_out_dw_{tag}", s["u"], dx, 1)
        dag3 = _swiglu_bwd(f"swiglu_bwd_{tag}", s["ag3"], du)
        dh2 = _mm_nt(f"mm_ffn_in_dx_{tag}", dag3, wfi, 0)
        g_wfi = _mm_tn(f"mm_ffn_in_dw_{tag}", s["h2"], dag3, N_CHIPS)
        dx2, dg = _rms_bwd(f"rms_ffn_bwd_{tag}", s["x2"], lane(norm_ffn_g[l]), [dh2], add=dx)
        gsmall["norm_ffn_g"][l] = dg[0]
        stages, tok = advance(stages, reduce_group(l, 1, {"w_ffn_in": g_wfi, "w_ffn_out": g_wfo.reshape(N_CHIPS, -1, d_model)}))
        dmerged = _mm_nt(f"mm_out_dx_{tag}", dx2, wo, 0, after=tok)
        g_wo = _mm_tn(f"mm_out_dw_{tag}", s["merged"], dx2, 1)
        dgp3, dbr, g_wb = _gate_bwd(f"gate_bwd_{tag}", s["gp3"], s["br"], wb, 0, dmerged)
        g_wg = _mm_tn(f"mm_gate_dw_{tag}", s["h"], dgp3, N_CHIPS)
        dh_a = _mm_nt(f"mm_gate_dx_{tag}", dgp3, wg, 0)
        drq, drk, drv, drg, ddf, ddb, dgn = _ret_bwd(tag, s["proj"], cos, sin, s["decf"], s["decb"], s["gn"], s["ret_saved"], dbr[0])
        gsmall["ret_decay_fwd"][l], gsmall["ret_decay_bwd"][l], gsmall["ret_norm_g"][l] = _fold_lanes(ddf), _fold_lanes(ddb), dgn[0]
        dpool_p = jnp.pad(dbr[1], ((POOL_PAD, POOL_PAD), (0, 0)))
        dpvp, dwbd, dps = _pool_bwd(tag, s["pvp"], s["wbd"], s["pscale"], t_len, dpool_p)
        dpv = dpvp[POOL_PAD:POOL_PAD + t_len]
        gsmall["pool_w"][l] = jnp.stack([dwbd[g * HEAD:(g + 1) * HEAD, g * HEAD:(g + 1) * HEAD] for g in range(NH)])
        gsmall["pool_scale"][l] = dps[0]
        dnq, dkn, dnv, dbtabs[l], dgq = _na_bwd(tag, s["proj"], s["kn"], s["vb"], s["btab"], s["nagq"], dbr[2])
        dnk, dnv, dgk = _na_keys_bwd(tag, s["proj"], s["nagk"], dkn, dnv)
        gsmall["na_q_norm_g"][l], gsmall["na_k_norm_g"][l] = _fold_heads(dgq), _fold_heads(dgk)
        dmq, dmemkv, dgq, dgk = _mem_bwd(tag, s["proj"], s["memkv"], s["mgq"], s["mgk"], dbr[3])
        gsmall["mem_q_norm_g"][l], gsmall["mem_k_norm_g"][l] = _fold_heads(dgq), _fold_heads(dgk)
        g_wmkv = _mm_tn(f"mm_memkv_dw_{tag}", s["hm"], dmemkv, 1)
        dhm = _mm_nt(f"mm_memkv_dx_{tag}", dmemkv, wmkv, 0)
        _, dg = _rms_bwd(f"rms_mem_bwd_{tag}", mem2d, lane(norm_mem_g[l]), [dhm])
        gsmall["norm_mem_g"][l] = dg[0]
        dproj = jnp.concatenate([drq, drk, drv, drg, dpv, dnq, dnk, dnv, dmq], axis=1)
        g_wi = _mm_tn(f"mm_in_dw_{tag}", dproj, s["h"], 1)
        dh_b = _mm_nt(f"mm_in_dx_{tag}", dproj, wi, 0)
        dx, dg = _rms_bwd(f"rms_mix_bwd_{tag}", s["x"], lane(norm_mix_g[l]), [dh_a, dh_b], add=dx2)
        gsmall["norm_mix_g"][l] = dg[0]

        stages, order_after = advance(stages, reduce_group(l, 0, {
            "w_in": g_wi.reshape(N_CHIPS, -1, d_model), "w_gate": g_wg, "w_mem_kv": g_wmkv.reshape(N_CHIPS, -1, 2 * BW),
            "w_branch": g_wb.reshape(N_CHIPS, NH * BW, BW), "w_out": g_wo.reshape(N_CHIPS, -1, d_model)}))
    while stages:
        stages = [nxt for nxt, _ in (stage(dx) for stage in stages) if nxt is not None]

    out_g, out_d, out_m, out_v = {}, {}, {}, {}
    for n in BIG:
        shp = W[n].shape
        back = (lambda b: jnp.swapaxes(b, 1, 2)) if n == "w_in" else (lambda b: b)
        out_g[n], out_d[n], out_m[n], out_v[n] = [back(b).reshape(shp) for b in out_bufs[n]]

    gsmall["na_rpb"] = list(_na_bias_grad(jnp.stack(dbtabs)))
    small_g = [jnp.stack(gsmall[n]).reshape(W[n].shape) for n in SMALL] + [loss_part]
    like = [W[n] for n in SMALL] + [loss_part]
    zero = jnp.zeros((1, 1), F32)
    parts = _gather_small(_pack(small_g))
    sg, sd, sm, sv = _small_update(parts, _pack([W[n] for n in SMALL] + [zero]), _pack([M[n] for n in SMALL] + [zero]),
                                   _pack([V[n] for n in SMALL] + [zero]))
    sg, sd, sm, sv = _unpack(sg, like), _unpack(sd, like), _unpack(sm, like), _unpack(sv, like)
    for i, n in enumerate(SMALL):
        out_g[n], out_d[n], out_m[n], out_v[n] = sg[i], sd[i], sm[i], sv[i]
    loss = sg[-1].reshape(())

    return (loss, dx.reshape(x.shape), *[out_g[n] for n in ORDER], *[out_d[n] for n in ORDER],
            *[out_m[n] for n in ORDER], *[out_v[n] for n in ORDER])
```

```python
import functools

import numpy as np
import jax
import jax.numpy as jnp
from jax import lax
from jax.experimental import pallas as pl
from jax.experimental.pallas import tpu as pltpu

F32 = jnp.float32
BF16 = jnp.bfloat16
SDS = jax.ShapeDtypeStruct
MESH = pl.DeviceIdType.MESH
ANY = pl.BlockSpec(memory_space=pl.ANY)

HEAD = 64
NH = 4
BW = NH * HEAD
CH = 128
GRID_W = 64
NA_ROWS = 8
NA_COLS = 16
POOL_HALF = (1, 2, 4, 8)
POOL_PAD = 16
ROPE_THETA = 10000.0
EPS = 1e-6
NEG_INF = -1e30
N_CHIPS = 4
N_DEV = 8
LANES = 128

ADAM_LR = 0.001
ADAM_B1 = 0.9
ADAM_B2 = 0.999
ADAM_EPS = 1e-08
ADAM_WD = 0.01
ADAM_STEP = 10

VMEM_BIG = 56 << 20
VMEM_MM = 44 << 20


def _cp(vmem=None):
    return pltpu.CompilerParams(vmem_limit_bytes=vmem) if vmem else None


def _d(a, b, ca, cb):
    return lax.dot_general(a.astype(BF16), b.astype(BF16), (((ca,), (cb,)), ((), ())),
                           preferred_element_type=F32)


@jax.custom_vjp
def _nn(a, b):
    return _d(a, b, 1, 0)


def _nn_f(a, b):
    return _d(a, b, 1, 0), (a, b)


def _nn_b(r, g):
    a, b = r
    return _d(g, b, 1, 1).astype(a.dtype), _d(a, g, 0, 0).astype(b.dtype)


_nn.defvjp(_nn_f, _nn_b)


@jax.custom_vjp
def _nt(a, b):
    return _d(a, b, 1, 1)


def _nt_f(a, b):
    return _d(a, b, 1, 1), (a, b)


def _nt_b(r, g):
    a, b = r
    return _d(g, b, 1, 0).astype(a.dtype), _d(g, a, 0, 0).astype(b.dtype)


_nt.defvjp(_nt_f, _nt_b)


@jax.custom_vjp
def _tn(a, b):
    return _d(a, b, 0, 0)


def _tn_f(a, b):
    return _d(a, b, 0, 0), (a, b)


def _tn_b(r, g):
    a, b = r
    return _d(b, g, 1, 1).astype(a.dtype), _d(a, g, 1, 0).astype(b.dtype)


_tn.defvjp(_tn_f, _tn_b)


@functools.partial(jax.custom_vjp, nondiff_argnums=(1,))
def _rollr(x, s):
    return pltpu.roll(x, s % x.shape[0], 0)


def _rollr_f(x, s):
    return _rollr(x, s), None


def _rollr_b(s, _, g):
    return (_rollr(g, -s),)


_rollr.defvjp(_rollr_f, _rollr_b)


@jax.custom_vjp
def _swap32(t):
    n = t.shape[1]
    lane = lax.broadcasted_iota(jnp.int32, (1, n), 1)
    return jnp.where((lane & (HEAD // 2)) == 0, pltpu.roll(t, n - HEAD // 2, 1), pltpu.roll(t, HEAD // 2, 1))


def _swap32_f(t):
    return _swap32(t), None


def _swap32_b(_, g):
    return (_swap32(g),)


_swap32.defvjp(_swap32_f, _swap32_b)


def _head_masks():
    lane = lax.broadcasted_iota(jnp.int32, (1, BW), 1)
    return [(lane >= HEAD * h) & (lane < HEAD * (h + 1)) for h in range(NH)]


def _head_rms(x, g):
    out = jnp.zeros_like(x)
    for mh in _head_masks():
        ms = jnp.sum(jnp.where(mh, x * x, 0.0), axis=-1, keepdims=True) * (1.0 / HEAD)
        out = out + jnp.where(mh, x * lax.rsqrt(ms + EPS), 0.0)
    return out * g


def _stack_heads(x):
    return jnp.concatenate([jnp.where(mh, x, 0.0) for mh in _head_masks()], axis=0)


def _unstack_heads(y):
    n = y.shape[0] // NH
    out = None
    for h, mh in enumerate(_head_masks()):
        t = jnp.where(mh, y[h * n:(h + 1) * n], 0.0)
        out = t if out is None else out + t
    return out


def _rms(x, g):
    return x * lax.rsqrt(jnp.mean(x * x, axis=-1, keepdims=True) + EPS) * g


def _rot(t, cos, sin):
    return t * cos + _swap32(t) * sin


def _softmax(s):
    e = jnp.exp(s - lax.stop_gradient(jnp.max(s, axis=-1, keepdims=True)))
    return e / jnp.sum(e, axis=-1, keepdims=True)


def _ret_kv_f(k, v, cos, sin, decf, decb):
    lgf, lgb = jax.nn.log_sigmoid(decf), jax.nn.log_sigmoid(decb)
    kr = _rot(k, cos, sin)
    idx = lax.broadcasted_iota(jnp.int32, (CH, 1), 0).astype(F32)
    r = lax.broadcasted_iota(jnp.int32, (BW, BW), 0) // HEAD
    c = lax.broadcasted_iota(jnp.int32, (BW, BW), 1) // HEAD
    bd = r == c
    kvf = jnp.where(bd, _tn(kr * jnp.exp((CH - 1 - idx) * lgf), v), 0.0)
    kvb = jnp.where(bd, _tn(kr * jnp.exp(idx * lgb), v), 0.0)
    return kvf, kvb


def _ret_scan_f(kvf, kvb, decf, decb):
    n = len(kvf)
    cdf = jnp.exp(CH * jax.nn.log_sigmoid(decf))
    cdb = jnp.exp(CH * jax.nn.log_sigmoid(decb))
    z = jnp.zeros((BW, BW), F32)
    sf, st = [z], z
    for a in range(n - 1):
        st = cdf * st + kvf[a]
        sf.append(st)
    sb, st = [z], z
    for a in range(n - 1, 0, -1):
        st = cdb * st + kvb[a]
        sb.append(st)
    return sf, sb[::-1]


def _ret_out_f(q, k, v, g, cos, sin, stf, stb, decf, decb, gn):
    lgf, lgb = jax.nn.log_sigmoid(decf), jax.nn.log_sigmoid(decb)
    qr = _rot(q, cos, sin) * (HEAD ** -0.5)
    kr = _rot(k, cos, sin)
    diff = (lax.broadcasted_iota(jnp.int32, (CH, CH), 0) - lax.broadcasted_iota(jnp.int32, (CH, CH), 1)).astype(F32)
    decay = []
    for mh in _head_masks():
        lf = jnp.sum(jnp.where(mh, lgf, 0.0), axis=1, keepdims=True) * (1.0 / HEAD)
        lb = jnp.sum(jnp.where(mh, lgb, 0.0), axis=1, keepdims=True) * (1.0 / HEAD)
        decay.append(jnp.exp(jnp.where(diff >= 0, diff * lf, -diff * lb)))
    p = _nt(_stack_heads(qr), kr) * jnp.concatenate(decay, axis=0)
    o = _unstack_heads(_nn(p, v))
    idx = lax.broadcasted_iota(jnp.int32, (CH, 1), 0).astype(F32)
    o = o + _nn(qr * jnp.exp((idx + 1) * lgf), stf) + _nn(qr * jnp.exp((CH - idx) * lgb), stb)
    return _head_rms(o, gn) * (g * jax.nn.sigmoid(g))


def _pool_f(pvp, wbd, scale, t_len):
    n = pvp.shape[0]
    w2 = pvp + _rollr(pvp, 1)
    w4 = _rollr(w2, 1) + _rollr(w2, -1)
    w8 = _rollr(w4, 2) + _rollr(w4, -2)
    w16 = _rollr(w8, 4) + _rollr(w8, -4)
    grp = lax.broadcasted_iota(jnp.int32, (1, BW), 1) // HEAD
    ws = jnp.where(grp == 0, w2, jnp.where(grp == 1, w4, jnp.where(grp == 2, w8, w16)))
    half = jnp.where(grp == 0, POOL_HALF[0], jnp.where(grp == 1, POOL_HALF[1], jnp.where(grp == 2, POOL_HALF[2], POOL_HALF[3])))
    t = lax.broadcasted_iota(jnp.int32, (n, 1), 0) - POOL_PAD
    cnt = jnp.minimum(t + half, t_len) - jnp.maximum(t - half, 0)
    cnt = jnp.where((t >= 0) & (t < t_len), cnt, 1).astype(F32)
    pooled = ws / cnt - pvp
    return _nn(pooled, wbd) * scale


def _na_f(q, kn, vw, bias, gq):
    qn = _head_rms(q, gq)
    bias_all = jnp.concatenate([jnp.concatenate(bias[h], axis=1) for h in range(NH)], axis=0)
    s = _nt(_stack_heads(qn), kn) * (HEAD ** -0.5) + bias_all
    return _unstack_heads(_nn(_softmax(s), vw))


def _mem_f(q, mk, mv, gq, gk):
    qn = _head_rms(q, gq)
    kn = _head_rms(mk, gk)
    s = _nt(_stack_heads(qn), kn) * (HEAD ** -0.5)
    return _unstack_heads(_nn(_softmax(s), mv))


def _gate_f(gp, br, wb):
    out = None
    for n in range(NH):
        t = jax.nn.sigmoid(gp[n]) * _nn(br[n], wb[n])
        out = t if out is None else out + t
    return out


def _swiglu_f(a, g):
    return a * jax.nn.sigmoid(a) * g


MM_ROWS = (1024, 512, 256)
MM_COLS = (1152, 1024, 768, 512)


def _tile(n, prefs):
    for p in prefs:
        if n % p == 0:
            return p
    return n


def _mm_call(name, a, b, extra, dn, grid, a_spec, b_spec, extra_specs, o_spec, out_shape, acc_shape, nred, after=None):
    add = len(extra) == 1
    if after is not None:
        extra, extra_specs = [*extra, after], [*extra_specs, ANY]
    n_extra = len(extra)
    n_steps = int(np.prod([grid[len(grid) - 1 - i] for i in range(nred)]))

    def body(*refs):
        a_ref, b_ref = refs[0], refs[1]
        o_ref, acc = refs[2 + n_extra], refs[-1]
        red = [len(grid) - 1 - i for i in range(nred)]
        first = functools.reduce(jnp.logical_and, [pl.program_id(ax) == 0 for ax in red])
        last = functools.reduce(jnp.logical_and, [pl.program_id(ax) == grid[ax] - 1 for ax in red])
        d = lax.dot_general(a_ref[...].astype(BF16), b_ref[...].astype(BF16), dn, preferred_element_type=F32)

        def finish(r):
            if add:
                r = r + refs[2][...]
            o_ref[...] = r.astype(o_ref.dtype)

        if n_steps == 1:
            finish(d)
        else:
            @pl.when(first)
            def _():
                acc[...] = d

            @pl.when(jnp.logical_not(first) & jnp.logical_not(last))
            def _():
                acc[...] += d

            @pl.when(last)
            def _():
                finish(acc[...] + d)

    return pl.pallas_call(
        body, grid=grid, in_specs=[a_spec, b_spec, *extra_specs], out_specs=o_spec, out_shape=out_shape,
        scratch_shapes=[pltpu.VMEM(acc_shape, F32)] if n_steps > 1 else [], name=name, compiler_params=_cp(VMEM_MM),
    )(a, b, *extra)


def _mm_nn(name, a, w, l, *, out3d=False, add=None, out_dtype=F32):
    m, k = a.shape
    _, s_n, _, ns = w.shape
    tm = _tile(m, MM_ROWS)
    tk = _tile(k, (1024, 1408, 512, 256))
    tn = ns if ns <= 1408 else _tile(ns, MM_COLS)
    grid = (m // tm, s_n, ns // tn, k // tk)
    a_spec = pl.BlockSpec((tm, tk), lambda i, s, j, kk: (i, kk))
    b_spec = pl.BlockSpec((None, None, tk, tn), lambda i, s, j, kk: (l, s, kk, j))
    if out3d:
        o_spec = pl.BlockSpec((None, tm, tn), lambda i, s, j, kk: (s, i, j))
        out_shape = SDS((s_n, m, ns), out_dtype)
    else:
        nj = ns // tn
        o_spec = pl.BlockSpec((tm, tn), lambda i, s, j, kk: (i, s * nj + j))
        out_shape = SDS((m, s_n * ns), out_dtype)
    extra, especs = [], []
    if add is not None:
        assert not out3d
        nj = ns // tn
        extra, especs = [add], [pl.BlockSpec((tm, tn), lambda i, s, j, kk: (i, s * nj + j))]
    return _mm_call(name, a, w, extra, (((1,), (0,)), ((), ())), grid, a_spec, b_spec, especs, o_spec, out_shape, (tm, tn), 1)


def _mm_nt(name, g, w, l, after=None):
    _, s_n, k, ns = w.shape
    g3d = g.ndim == 3
    m = g.shape[1] if g3d else g.shape[0]
    tm = _tile(m, MM_ROWS)
    to = _tile(k, (1024, 1408, 512, 256))
    tk = ns if (ns % LANES or ns <= 1408) else _tile(ns, MM_COLS)
    nks = ns // tk
    grid = (m // tm, k // to, s_n, nks)
    if g3d:
        a_spec = pl.BlockSpec((None, tm, tk), lambda i, j, s, r: (s, i, r))
    else:
        a_spec = pl.BlockSpec((tm, tk), lambda i, j, s, r: (i, s * nks + r))
    b_spec = pl.BlockSpec((None, None, to, tk), lambda i, j, s, r: (l, s, j, r))
    o_spec = pl.BlockSpec((tm, to), lambda i, j, s, r: (i, j))
    return _mm_call(name, g, w, [], (((1,), (1,)), ((), ())), grid, a_spec, b_spec, [], o_spec, SDS((m, k), F32), (tm, to), 2,
                    after=after)


def _mm_tn(name, a, g, s_n):
    t_len, k = a.shape
    g3d = g.ndim == 3
    ns = g.shape[2] if g3d else g.shape[1] // s_n
    tt = _tile(t_len, (1024, 512))
    to = _tile(k, (1024, 768, 1408, 512, 256, 128))
    tn = ns
    nj = ns // tn
    grid = (s_n, k // to, nj, t_len // tt)
    a_spec = pl.BlockSpec((tt, to), lambda s, i, j, t: (t, i))
    if g3d:
        b_spec = pl.BlockSpec((None, tt, tn), lambda s, i, j, t: (s, t, j))
    else:
        b_spec = pl.BlockSpec((tt, tn), lambda s, i, j, t: (t, s * nj + j))
    o_spec = pl.BlockSpec((None, to, tn), lambda s, i, j, t: (s, i, j))
    return _mm_call(name, a, g, [], (((0,), (0,)), ((), ())), grid, a_spec, b_spec, [], o_spec,
                    SDS((s_n, k, ns), F32), (to, tn), 1)


def _rms_fwd(name, x, g):
    m, d = x.shape
    tm = _tile(m, (256,))

    def body(x_ref, g_ref, o_ref):
        o_ref[...] = _rms(x_ref[...], g_ref[...]).astype(BF16)

    row = pl.BlockSpec((tm, d), lambda i: (i, 0))
    return pl.pallas_call(body, grid=(m // tm,), in_specs=[row, pl.BlockSpec((1, d), lambda i: (0, 0))], out_specs=row,
                          out_shape=SDS((m, d), BF16), name=name)(x, g)


def _rms_bwd(name, x, g, dhs, add=None):
    m, d = x.shape
    tm = _tile(m, (256,))
    nh = len(dhs)

    def body(*refs):
        x_ref, g_ref = refs[0], refs[1]
        dh = refs[2][...]
        for r in refs[3:2 + nh]:
            dh = dh + r[...]
        dx_ref, dg_ref = refs[-2], refs[-1]
        _, vjp = jax.vjp(_rms, x_ref[...], g_ref[...])
        dx, dg = vjp(dh)
        if add is not None:
            dx = dx + refs[2 + nh][...]
        dx_ref[...] = dx

        @pl.when(pl.program_id(0) == 0)
        def _():
            dg_ref[...] = jnp.zeros_like(dg_ref)

        dg_ref[...] += dg

    row = pl.BlockSpec((tm, d), lambda i: (i, 0))
    vec = pl.BlockSpec((1, d), lambda i: (0, 0))
    ins = [x, g, *dhs] + ([add] if add is not None else [])
    return pl.pallas_call(body, grid=(m // tm,), in_specs=[row, vec] + [row] * (len(ins) - 2), out_specs=[row, vec],
                          out_shape=[SDS((m, d), F32), SDS((1, d), F32)], name=name)(*ins)


def _swiglu_fwd(name, ag3):
    _, t_len, w = ag3.shape
    tm = _tile(t_len, (512, 256))

    def body(a_ref, g_ref, o_ref):
        o_ref[...] = _swiglu_f(a_ref[...], g_ref[...]).astype(BF16)

    return pl.pallas_call(
        body, grid=(t_len // tm, 2),
        in_specs=[pl.BlockSpec((None, tm, w), lambda i, j: (j, i, 0)), pl.BlockSpec((None, tm, w), lambda i, j: (j + 2, i, 0))],
        out_specs=pl.BlockSpec((tm, w), lambda i, j: (i, j)), out_shape=SDS((t_len, 2 * w), BF16), name=name)(ag3, ag3)


def _swiglu_bwd(name, ag3, du):
    _, t_len, w = ag3.shape
    tm = _tile(t_len, (512, 256))

    def body(a_ref, g_ref, du_ref, o_ref):
        _, vjp = jax.vjp(_swiglu_f, a_ref[...], g_ref[...])
        da, dg = vjp(du_ref[...])
        o_ref[0] = da.astype(BF16)
        o_ref[1] = dg.astype(BF16)

    out = pl.pallas_call(
        body, grid=(t_len // tm, 2),
        in_specs=[pl.BlockSpec((None, tm, w), lambda i, j: (j, i, 0)), pl.BlockSpec((None, tm, w), lambda i, j: (j + 2, i, 0)),
                  pl.BlockSpec((tm, w), lambda i, j: (i, j))],
        out_specs=pl.BlockSpec((2, None, tm, w), lambda i, j: (0, j, i, 0)), out_shape=SDS((2, 2, t_len, w), BF16), name=name)(ag3, ag3, du)
    return out.reshape(ag3.shape)


def _gate_fwd(name, gp3, br, wb, l):
    _, t_len, d = gp3.shape
    tm = _tile(t_len, (512, 256))

    def body(gp_ref, br_ref, wb_ref, o_ref):
        o_ref[...] = _gate_f([gp_ref[n] for n in range(NH)], [br_ref[n] for n in range(NH)],
                             [wb_ref[n] for n in range(NH)]).astype(BF16)

    return pl.pallas_call(
        body, grid=(t_len // tm, d // BW),
        in_specs=[pl.BlockSpec((NH, tm, BW), lambda i, s: (0, i, s)), pl.BlockSpec((NH, tm, BW), lambda i, s: (0, i, 0)),
                  pl.BlockSpec((None, None, NH, BW, BW), lambda i, s: (l, s, 0, 0, 0))],
        out_specs=pl.BlockSpec((tm, BW), lambda i, s: (i, s)), out_shape=SDS((t_len, d), BF16), name=name)(gp3, br, wb)


def _gate_bwd(name, gp3, br, wb, l, dmerged):
    _, t_len, d = gp3.shape
    tm = _tile(t_len, (512, 256))
    ns = d // BW

    def body(gp_ref, br_ref, wb_ref, dm_ref, dgp_ref, dbr_ref, dwb_ref):
        i, s = pl.program_id(0), pl.program_id(1)
        gp = [gp_ref[n] for n in range(NH)]
        brv = [br_ref[n].astype(F32) for n in range(NH)]
        wbv = [wb_ref[n].astype(F32) for n in range(NH)]
        _, vjp = jax.vjp(_gate_f, gp, brv, wbv)
        dgp, dbr, dwb = vjp(dm_ref[...])

        @pl.when(s == 0)
        def _():
            dbr_ref[...] = jnp.zeros_like(dbr_ref)

        @pl.when((i == 0) & (s == 0))
        def _():
            dwb_ref[...] = jnp.zeros_like(dwb_ref)

        for n in range(NH):
            dgp_ref[n] = dgp[n].astype(BF16)
            dbr_ref[n] += dbr[n]
            dwb_ref[s, n] += dwb[n]

    return pl.pallas_call(
        body, grid=(t_len // tm, ns),
        in_specs=[pl.BlockSpec((NH, tm, BW), lambda i, s: (0, i, s)), pl.BlockSpec((NH, tm, BW), lambda i, s: (0, i, 0)),
                  pl.BlockSpec((None, None, NH, BW, BW), lambda i, s: (l, s, 0, 0, 0)), pl.BlockSpec((tm, BW), lambda i, s: (i, s))],
        out_specs=[pl.BlockSpec((NH, tm, BW), lambda i, s: (0, i, s)), pl.BlockSpec((NH, tm, BW), lambda i, s: (0, i, 0)),
                   pl.BlockSpec((ns, NH, BW, BW), lambda i, s: (0, 0, 0, 0))],
        out_shape=[SDS(gp3.shape, BF16), SDS((NH, t_len, BW), F32), SDS((ns, NH, BW, BW), F32)], name=name,
        compiler_params=_cp(VMEM_MM))(gp3, br, wb, dmerged)


def _loss_kernel(name, y, tgt):
    m, d = y.shape
    tm = _tile(m, (256,))

    def body(y_ref, t_ref, l_ref, dy_ref):
        err = y_ref[...] - t_ref[...]
        dy_ref[...] = err * (1.0 / d)

        @pl.when(pl.program_id(0) == 0)
        def _():
            l_ref[...] = jnp.zeros_like(l_ref)

        l_ref[...] += 0.5 * jnp.sum(jnp.mean(err * err, axis=-1, keepdims=True), axis=0, keepdims=True)

    row = pl.BlockSpec((tm, d), lambda i: (i, 0))
    return pl.pallas_call(body, grid=(m // tm,), in_specs=[row, row], out_specs=[pl.BlockSpec((1, 1), lambda i: (0, 0)), row],
                          out_shape=[SDS((1, 1), F32), SDS((m, d), F32)], name=name)(y, tgt)


def _zero_first(refs):
    @pl.when(pl.program_id(0) == 0)
    def _():
        for r in refs:
            r[...] = jnp.zeros_like(r)


def _grp(rows, j):
    return pl.BlockSpec((rows, BW), lambda a, j=j: (a, j))


_VEC = pl.BlockSpec((1, BW), lambda a: (0, 0))


def _ret_fwd(tag, proj, cos, sin, decf, decb, gn):
    t_len = proj.shape[0]
    n = t_len // CH
    tab = pl.BlockSpec((CH, BW), lambda a: (a, 0))
    st = pl.BlockSpec((None, BW, BW), lambda a: (a, 0, 0))

    def k1(k_ref, v_ref, cos_ref, sin_ref, df_ref, db_ref, kvf_ref, kvb_ref):
        kvf_ref[...], kvb_ref[...] = _ret_kv_f(k_ref[...], v_ref[...], cos_ref[...], sin_ref[...], df_ref[...], db_ref[...])

    kvf, kvb = pl.pallas_call(
        k1, grid=(n,), in_specs=[_grp(CH, 1), _grp(CH, 2), tab, tab, _VEC, _VEC], out_specs=[st, st],
        out_shape=[SDS((n, BW, BW), F32)] * 2, name=f"ret_kv_{tag}")(proj, proj, cos, sin, decf, decb)

    def k2(kvf_ref, kvb_ref, df_ref, db_ref, sf_ref, sb_ref):
        sf, sb = _ret_scan_f([kvf_ref[a] for a in range(n)], [kvb_ref[a] for a in range(n)], df_ref[...], db_ref[...])
        for a in range(n):
            sf_ref[a] = sf[a]
            sb_ref[a] = sb[a]

    stf, stb = pl.pallas_call(k2, out_shape=[SDS((n, BW, BW), F32)] * 2, name=f"ret_scan_{tag}",
                              compiler_params=_cp(VMEM_BIG))(kvf, kvb, decf, decb)

    def k3(q_ref, k_ref, v_ref, g_ref, cos_ref, sin_ref, sf_ref, sb_ref, df_ref, db_ref, gn_ref, o_ref):
        o_ref[...] = _ret_out_f(q_ref[...], k_ref[...], v_ref[...], g_ref[...], cos_ref[...], sin_ref[...], sf_ref[...],
                                sb_ref[...], df_ref[...], db_ref[...], gn_ref[...]).astype(BF16)

    ret = pl.pallas_call(
        k3, grid=(n,), in_specs=[_grp(CH, 0), _grp(CH, 1), _grp(CH, 2), _grp(CH, 3), tab, tab, st, st, _VEC, _VEC, _VEC],
        out_specs=tab, out_shape=SDS((t_len, BW), BF16), name=f"ret_out_{tag}")(proj, proj, proj, proj, cos, sin, stf, stb, decf, decb, gn)
    return ret, (kvf, kvb, stf, stb)


def _ret_bwd(tag, proj, cos, sin, decf, decb, gn, saved, dret):
    kvf, kvb, stf, stb = saved
    t_len = proj.shape[0]
    n = t_len // CH
    tab = pl.BlockSpec((CH, BW), lambda a: (a, 0))
    st = pl.BlockSpec((None, BW, BW), lambda a: (a, 0, 0))

    def k3(q_ref, k_ref, v_ref, g_ref, cos_ref, sin_ref, sf_ref, sb_ref, df_ref, db_ref, gn_ref, do_ref,
           dq_ref, dk_ref, dv_ref, dg_ref, dsf_ref, dsb_ref, ddf_ref, ddb_ref, dgn_ref):
        cos, sin = cos_ref[...], sin_ref[...]
        f = lambda q, k, v, g, sf, sb, df, db, gnv: _ret_out_f(q, k, v, g, cos, sin, sf, sb, df, db, gnv)
        _, vjp = jax.vjp(f, q_ref[...], k_ref[...], v_ref[...], g_ref[...], sf_ref[...], sb_ref[...], df_ref[...], db_ref[...], gn_ref[...])
        dq, dk, dv, dg, dsf, dsb, ddf, ddb, dgn = vjp(do_ref[...])
        dq_ref[...] = dq.astype(BF16)
        dk_ref[...] = dk
        dv_ref[...] = dv
        dg_ref[...] = dg.astype(BF16)
        dsf_ref[...] = dsf
        dsb_ref[...] = dsb
        _zero_first([ddf_ref, ddb_ref, dgn_ref])
        ddf_ref[...] += ddf
        ddb_ref[...] += ddb
        dgn_ref[...] += dgn

    dq, dk3, dv3, dg, dstf, dstb, ddf3, ddb3, dgn = pl.pallas_call(
        k3, grid=(n,),
        in_specs=[_grp(CH, 0), _grp(CH, 1), _grp(CH, 2), _grp(CH, 3), tab, tab, st, st, _VEC, _VEC, _VEC, tab],
        out_specs=[tab, tab, tab, tab, st, st, _VEC, _VEC, _VEC],
        out_shape=[SDS((t_len, BW), BF16), SDS((t_len, BW), F32), SDS((t_len, BW), F32), SDS((t_len, BW), BF16),
                   SDS((n, BW, BW), F32), SDS((n, BW, BW), F32), SDS((1, BW), F32), SDS((1, BW), F32), SDS((1, BW), F32)],
        name=f"ret_out_bwd_{tag}")(proj, proj, proj, proj, cos, sin, stf, stb, decf, decb, gn, dret)

    def k2(kvf_ref, kvb_ref, df_ref, db_ref, dsf_ref, dsb_ref, dkvf_ref, dkvb_ref, ddf_ref, ddb_ref):
        _, vjp = jax.vjp(_ret_scan_f, [kvf_ref[a] for a in range(n)], [kvb_ref[a] for a in range(n)], df_ref[...], db_ref[...])
        dkvf, dkvb, ddf_ref[...], ddb_ref[...] = vjp(([dsf_ref[a] for a in range(n)], [dsb_ref[a] for a in range(n)]))
        for a in range(n):
            dkvf_ref[a] = dkvf[a]
            dkvb_ref[a] = dkvb[a]

    dkvf, dkvb, ddf2, ddb2 = pl.pallas_call(
        k2, out_shape=[SDS((n, BW, BW), F32), SDS((n, BW, BW), F32), SDS((1, BW), F32), SDS((1, BW), F32)],
        name=f"ret_scan_bwd_{tag}", compiler_params=_cp(VMEM_BIG))(kvf, kvb, decf, decb, dstf, dstb)

    def k1(k_ref, v_ref, cos_ref, sin_ref, df_ref, db_ref, dkvf_ref, dkvb_ref, dk3_ref, dv3_ref, dk_ref, dv_ref, ddf_ref, ddb_ref):
        cos, sin = cos_ref[...], sin_ref[...]
        f = lambda k, v, df, db: _ret_kv_f(k, v, cos, sin, df, db)
        _, vjp = jax.vjp(f, k_ref[...], v_ref[...], df_ref[...], db_ref[...])
        dk, dv, ddf, ddb = vjp((dkvf_ref[...], dkvb_ref[...]))
        dk_ref[...] = (dk + dk3_ref[...]).astype(BF16)
        dv_ref[...] = (dv + dv3_ref[...]).astype(BF16)
        _zero_first([ddf_ref, ddb_ref])
        ddf_ref[...] += ddf
        ddb_ref[...] += ddb

    dk, dv, ddf1, ddb1 = pl.pallas_call(
        k1, grid=(n,), in_specs=[_grp(CH, 1), _grp(CH, 2), tab, tab, _VEC, _VEC, st, st, tab, tab],
        out_specs=[tab, tab, _VEC, _VEC],
        out_shape=[SDS((t_len, BW), BF16), SDS((t_len, BW), BF16), SDS((1, BW), F32), SDS((1, BW), F32)],
        name=f"ret_kv_bwd_{tag}")(proj, proj, cos, sin, decf, decb, dkvf, dkvb, dk3, dv3)
    return dq, dk, dv, dg, ddf1 + ddf2 + ddf3, ddb1 + ddb2 + ddb3, dgn


def _pool_fwd(tag, pvp, wbd, scale, t_len):
    def body(p_ref, w_ref, s_ref, o_ref):
        o_ref[...] = _pool_f(p_ref[...], w_ref[...], s_ref[...], t_len).astype(BF16)

    return pl.pallas_call(body, out_shape=SDS(pvp.shape, BF16), name=f"pool_{tag}", compiler_params=_cp(VMEM_BIG))(pvp, wbd, scale)


def _pool_bwd(tag, pvp, wbd, scale, t_len, dpool_p):
    def body(p_ref, w_ref, s_ref, do_ref, dp_ref, dw_ref, ds_ref):
        f = lambda p, w, s: _pool_f(p, w, s, t_len)
        _, vjp = jax.vjp(f, p_ref[...], w_ref[...], s_ref[...])
        dp, dw, ds = vjp(do_ref[...])
        dp_ref[...] = dp.astype(BF16)
        dw_ref[...] = dw
        ds_ref[...] = ds

    return pl.pallas_call(body, out_shape=[SDS(pvp.shape, BF16), SDS((BW, BW), F32), SDS((1, BW), F32)],
                          name=f"pool_bwd_{tag}", compiler_params=_cp(VMEM_BIG))(pvp, wbd, scale, dpool_p)


def _na_window(r, rows):
    r0 = jnp.clip(r - NA_ROWS // 2, 0, rows - NA_ROWS)
    return r0, r - r0


NA_PAIRS = 2 * NA_ROWS - 2


def _na_bias_pieces(b_ref, pat):
    return [[b_ref[h * NA_PAIRS + 2 * k - pat + NA_ROWS - 1] for k in range(NA_ROWS // 2)] for h in range(NH)]


def _na_keys(tag, proj, gk):
    t_len = proj.shape[0]
    tm = _tile(t_len, (256,))

    def body(k_ref, v_ref, gk_ref, kn_ref, vb_ref):
        kn_ref[...] = _head_rms(k_ref[...], gk_ref[...]).astype(BF16)
        vb_ref[...] = v_ref[...].astype(BF16)

    row = pl.BlockSpec((tm, BW), lambda a: (a, 0))
    return pl.pallas_call(body, grid=(t_len // tm,), in_specs=[_grp(tm, 6), _grp(tm, 7), _VEC], out_specs=[row, row],
                          out_shape=[SDS((t_len, BW), BF16)] * 2, name=f"na_keys_{tag}")(proj, proj, gk)


def _na_keys_bwd(tag, proj, gk, dkn, dv):
    t_len = proj.shape[0]
    tm = _tile(t_len, (256,))

    def body(k_ref, gk_ref, dkn_ref, dv_ref, dk_ref, dvb_ref, dgk_ref):
        _, vjp = jax.vjp(_head_rms, k_ref[...], gk_ref[...])
        dk, dgk = vjp(dkn_ref[...])
        dk_ref[...] = dk.astype(BF16)
        dvb_ref[...] = dv_ref[...].astype(BF16)
        _zero_first([dgk_ref])
        dgk_ref[...] += dgk

    row = pl.BlockSpec((tm, BW), lambda a: (a, 0))
    return pl.pallas_call(body, grid=(t_len // tm,), in_specs=[_grp(tm, 6), _VEC, row, row], out_specs=[row, row, _VEC],
                          out_shape=[SDS((t_len, BW), BF16), SDS((t_len, BW), BF16), SDS((1, BW), F32)],
                          name=f"na_keys_bwd_{tag}")(proj, gk, dkn, dv)


def _na_fwd(tag, proj, kn, vb, bias_tab, gq):
    t_len = proj.shape[0]
    rows = t_len // GRID_W
    win = NA_ROWS * GRID_W

    def body(q_ref, k_ref, v_ref, b_ref, gq_ref, o_ref):
        r0, pat = _na_window(pl.program_id(0), rows)
        start = pl.multiple_of(r0 * GRID_W, GRID_W)
        o_ref[...] = _na_f(q_ref[...], k_ref[pl.ds(start, win), :], v_ref[pl.ds(start, win), :], _na_bias_pieces(b_ref, pat),
                           gq_ref[...]).astype(BF16)

    whole = pl.BlockSpec((t_len, BW), lambda a: (0, 0))
    return pl.pallas_call(
        body, grid=(rows,),
        in_specs=[_grp(GRID_W, 5), whole, whole, pl.BlockSpec(bias_tab.shape, lambda a: (0, 0, 0)), _VEC],
        out_specs=pl.BlockSpec((GRID_W, BW), lambda a: (a, 0)), out_shape=SDS((t_len, BW), BF16),
        name=f"na_{tag}", compiler_params=_cp(VMEM_MM))(proj, kn, vb, bias_tab, gq)


def _na_bwd(tag, proj, kn, vb, bias_tab, gq, dna):
    t_len = proj.shape[0]
    rows = t_len // GRID_W
    win = NA_ROWS * GRID_W

    def body(q_ref, k_ref, v_ref, b_ref, gq_ref, do_ref, dq_ref, dk_ref, dv_ref, db_ref, dgq_ref):
        r0, pat = _na_window(pl.program_id(0), rows)
        start = pl.multiple_of(r0 * GRID_W, GRID_W)
        _zero_first([dk_ref, dv_ref, db_ref, dgq_ref])
        _, vjp = jax.vjp(_na_f, q_ref[...], k_ref[pl.ds(start, win), :].astype(F32), v_ref[pl.ds(start, win), :].astype(F32),
                         _na_bias_pieces(b_ref, pat), gq_ref[...])
        dq, dk, dv, db, dgq = vjp(do_ref[...])
        dq_ref[...] = dq.astype(BF16)
        dk_ref[pl.ds(start, win), :] += dk
        dv_ref[pl.ds(start, win), :] += dv
        for h in range(NH):
            for k in range(NA_ROWS // 2):
                db_ref[h * NA_PAIRS + 2 * k - pat + NA_ROWS - 1] += db[h][k]
        dgq_ref[...] += dgq

    whole = pl.BlockSpec((t_len, BW), lambda a: (0, 0))
    tabspec = pl.BlockSpec(bias_tab.shape, lambda a: (0, 0, 0))
    row = pl.BlockSpec((GRID_W, BW), lambda a: (a, 0))
    return pl.pallas_call(
        body, grid=(rows,), in_specs=[_grp(GRID_W, 5), whole, whole, tabspec, _VEC, row],
        out_specs=[row, whole, whole, tabspec, _VEC],
        out_shape=[SDS((t_len, BW), BF16), SDS((t_len, BW), F32), SDS((t_len, BW), F32), SDS(bias_tab.shape, F32), SDS((1, BW), F32)],
        name=f"na_bwd_{tag}", compiler_params=_cp(VMEM_BIG))(proj, kn, vb, bias_tab, gq, dna)


def _mem_fwd(tag, proj, memkv, gq, gk):
    t_len = proj.shape[0]
    n_mem = memkv.shape[0]
    tm = _tile(t_len, (256,))

    def body(q_ref, mk_ref, mv_ref, gq_ref, gk_ref, o_ref):
        o_ref[...] = _mem_f(q_ref[...], mk_ref[...], mv_ref[...], gq_ref[...], gk_ref[...]).astype(BF16)

    mspec = lambda j: pl.BlockSpec((n_mem, BW), lambda a, j=j: (0, j))
    return pl.pallas_call(
        body, grid=(t_len // tm,), in_specs=[_grp(tm, 8), mspec(0), mspec(1), _VEC, _VEC],
        out_specs=pl.BlockSpec((tm, BW), lambda a: (a, 0)), out_shape=SDS((t_len, BW), BF16), name=f"mem_{tag}")(proj, memkv, memkv, gq, gk)


def _mem_bwd(tag, proj, memkv, gq, gk, dmo):
    t_len = proj.shape[0]
    n_mem = memkv.shape[0]
    tm = _tile(t_len, (256,))

    def body(q_ref, mk_ref, mv_ref, gq_ref, gk_ref, do_ref, dq_ref, dmk_ref, dmv_ref, dgq_ref, dgk_ref):
        _zero_first([dmk_ref, dmv_ref, dgq_ref, dgk_ref])
        _, vjp = jax.vjp(_mem_f, q_ref[...], mk_ref[...], mv_ref[...], gq_ref[...], gk_ref[...])
        dq, dmk, dmv, dgq, dgk = vjp(do_ref[...])
        dq_ref[...] = dq.astype(BF16)
        dmk_ref[...] += dmk
        dmv_ref[...] += dmv
        dgq_ref[...] += dgq
        dgk_ref[...] += dgk

    mspec = lambda j: pl.BlockSpec((n_mem, BW), lambda a, j=j: (0, j))
    mout = pl.BlockSpec((n_mem, BW), lambda a: (0, 0))
    row = pl.BlockSpec((tm, BW), lambda a: (a, 0))
    dq, dmk, dmv, dgq, dgk = pl.pallas_call(
        body, grid=(t_len // tm,), in_specs=[_grp(tm, 8), mspec(0), mspec(1), _VEC, _VEC, row],
        out_specs=[row, mout, mout, _VEC, _VEC],
        out_shape=[SDS((t_len, BW), BF16), SDS((n_mem, BW), F32), SDS((n_mem, BW), F32), SDS((1, BW), F32), SDS((1, BW), F32)],
        name=f"mem_bwd_{tag}")(proj, memkv, memkv, gq, gk, dmo)
    return dq, jnp.concatenate([dmk, dmv], axis=1), dgq, dgk


NA_NJ = 2 * NA_COLS - 1


def _na_onehot():
    q = np.arange(GRID_W)[:, None]
    kc = np.arange(GRID_W)[None, :]
    qwin = np.clip(q - NA_COLS // 2, 0, GRID_W - NA_COLS)
    mask = (kc >= qwin) & (kc < qwin + NA_COLS)
    col = np.clip(kc - q, -(NA_COLS - 1), NA_COLS - 1) + NA_COLS - 1
    onehot = np.zeros((LANES, GRID_W, 2, GRID_W), np.float32)
    qq, kk = np.nonzero(mask)
    for half in range(2):
        onehot[half * NA_NJ + col[qq, kk], qq, half, kk] = 1.0
    valid = np.broadcast_to(mask[:, None, :], (GRID_W, 2, GRID_W)).astype(np.float32)
    return onehot.reshape(LANES, -1), valid.reshape(1, -1)


def _na_pair_rows(rpb):
    n_layers = rpb.shape[0]
    pair = jnp.concatenate([rpb[:, :, :-1], rpb[:, :, 1:]], axis=-1).reshape(n_layers, NH * NA_PAIRS, 2 * NA_NJ)
    return jnp.pad(pair, ((0, 0), (0, 0), (0, LANES - 2 * NA_NJ)))


def _na_bias_table(rpb):
    n_layers = rpb.shape[0]
    onehot, valid = _na_onehot()
    width = onehot.shape[1]

    def body(r_ref, oh_ref, ok_ref, o_ref):
        t = jnp.dot(r_ref[...], oh_ref[...], precision=lax.Precision.HIGHEST, preferred_element_type=F32)
        o_ref[...] = jnp.where(ok_ref[...] > 0, t, NEG_INF)

    out = pl.pallas_call(
        body, grid=(n_layers,),
        in_specs=[pl.BlockSpec((None, NH * NA_PAIRS, LANES), lambda l: (l, 0, 0)), pl.BlockSpec((LANES, width), lambda l: (0, 0)),
                  pl.BlockSpec((1, width), lambda l: (0, 0))],
        out_specs=pl.BlockSpec((None, NH * NA_PAIRS, width), lambda l: (l, 0, 0)),
        out_shape=SDS((n_layers, NH * NA_PAIRS, width), F32), name="na_bias_table")(
            _na_pair_rows(rpb), jnp.asarray(onehot), jnp.asarray(valid))
    return out.reshape(n_layers, NH * NA_PAIRS, GRID_W, 2 * GRID_W)


def _na_bias_grad(dtab):
    n_layers = dtab.shape[0]
    onehot, _ = _na_onehot()
    width = onehot.shape[1]

    def body(x_ref, oh_ref, o_ref):
        o_ref[...] = jnp.dot(x_ref[...], oh_ref[...], precision=lax.Precision.HIGHEST, preferred_element_type=F32)

    out = pl.pallas_call(
        body, grid=(n_layers,),
        in_specs=[pl.BlockSpec((None, NH * NA_PAIRS, width), lambda l: (l, 0, 0)), pl.BlockSpec((width, LANES), lambda l: (0, 0))],
        out_specs=pl.BlockSpec((None, NH * NA_PAIRS, LANES), lambda l: (l, 0, 0)),
        out_shape=SDS((n_layers, NH * NA_PAIRS, LANES), F32), name="na_bias_grad")(
            dtab.reshape(n_layers, NH * NA_PAIRS, width), jnp.asarray(onehot.T.copy()))
    out = out.reshape(n_layers, NH, NA_PAIRS, LANES)
    zero = jnp.zeros((n_layers, NH, 1, NA_NJ), F32)
    return (jnp.concatenate([out[..., :NA_NJ], zero], axis=2) + jnp.concatenate([zero, out[..., NA_NJ:2 * NA_NJ]], axis=2))


def _place():
    x, y, c = lax.axis_index("x"), lax.axis_index("y"), lax.axis_index("c")
    chips = [(1 - x, y), (x, 1 - y), (1 - x, 1 - y)]
    return x, y, c, chips


def _remote(src, dst, ssem, rsem, dev):
    return pltpu.make_async_remote_copy(src_ref=src, dst_ref=dst, send_sem=ssem, recv_sem=rsem, device_id=dev, device_id_type=MESH)


HBM = pl.BlockSpec(memory_space=pltpu.HBM)
SEM = pl.BlockSpec(memory_space=pltpu.SEMAPHORE)
DATAFLOW = pltpu.SideEffectType.DATAFLOW_SIDE_EFFECTING


def _split_copies(name, srcs, lands, n_copies, plan):
    n_s, n_l = len(srcs), len(lands)
    hbm = lambda a: pltpu.HBM(a.shape, a.dtype)

    def start_body(*refs):
        ins = refs[:n_s + n_l]
        outs = refs[n_s + n_l:]
        sems, token = outs[:2 * n_copies], outs[-1]
        for k, (src, dst, dev) in enumerate(plan(ins[:n_s], ins[n_s:])):
            _remote(src, dst, sems[k], sems[n_copies + k], dev).start()
        token[...] = jnp.zeros_like(token)

    outs = pl.pallas_call(
        start_body, name=f"{name}_start",
        out_shape=(pltpu.SemaphoreType.DMA(()),) * (2 * n_copies) + tuple(hbm(a) for a in (*srcs, *lands)) + (SDS((8, LANES), F32),),
        in_specs=(HBM,) * (n_s + n_l), out_specs=(SEM,) * (2 * n_copies) + (HBM,) * (n_s + n_l) + (pl.BlockSpec(memory_space=pltpu.VMEM),),
        input_output_aliases={i: 2 * n_copies + i for i in range(n_s + n_l)},
        compiler_params=pltpu.CompilerParams(has_side_effects=DATAFLOW),
    )(*[pltpu.with_memory_space_constraint(a, pltpu.HBM) for a in (*srcs, *lands)])
    sems, thru, token = outs[:2 * n_copies], outs[2 * n_copies:-1], outs[-1]

    def wait(after_wait):
        def wait_body(*refs):
            ins = refs[:n_s + n_l]
            sem_refs = refs[n_s + n_l:n_s + n_l + 2 * n_copies]
            for k, (src, dst, dev) in enumerate(plan(ins[:n_s], ins[n_s:])):
                cp = _remote(src, dst, sem_refs[k], sem_refs[n_copies + k], dev)
                cp.wait_send()
                cp.wait_recv()

        res = pl.pallas_call(
            wait_body, name=f"{name}_wait", out_shape=tuple(hbm(a) for a in (*srcs, *lands)),
            in_specs=(HBM,) * (n_s + n_l) + (SEM,) * (2 * n_copies) + (ANY,), out_specs=(HBM,) * (n_s + n_l),
            input_output_aliases={i: i for i in range(n_s + n_l)},
            compiler_params=pltpu.CompilerParams(has_side_effects=DATAFLOW),
        )(*thru, *sems, after_wait)
        return list(res[:n_s]), list(res[n_s:])

    return token, wait


def _gather_plan(srcs, lands):
    x, y, c, chips = _place()
    me = 2 * x + y
    out = []
    for src, land in zip(srcs, lands):
        h = src.shape[0] // 2
        out.append((src, land.at[me], (x, y, 1 - c)))
        for px, py in chips:
            out.append((src.at[pl.ds(c * h, h)], land.at[me, pl.ds(c * h, h)], (px, py, c)))
    return out


def _scatter_plan(srcs, lands):
    x, y, c, chips = _place()
    out = []
    for src, land in zip(srcs, lands):
        for r, (px, py) in enumerate(chips):
            out.append((src.at[2 * px + py], land.at[r], (px, py, c)))
    return out


def _pass_plan(srcs, lands):
    x, y, c, chips = _place()
    out = []
    for land in lands:
        h = land.shape[1] // 2
        for px, py in chips:
            piece = land.at[2 * px + py, pl.ds(c * h, h)]
            out.append((piece, piece, (x, y, 1 - c)))
    return out


def _pair_plan(srcs, lands):
    x, y, c, _ = _place()
    out = []
    for src, land in zip(srcs, lands):
        h = land.shape[1]
        out.append((src.at[:, pl.ds((1 - c) * h, h)], land, (x, y, 1 - c)))
    return out


def _half_plan(srcs, lands):
    x, y, c, _ = _place()
    out = []
    for land in lands:
        h = land.shape[0] // 2
        mine = land.at[pl.ds(c * h, h)]
        out.append((mine, mine, (x, y, 1 - c)))
    return out


def _forward_halves(name, lands):
    n = len(lands)

    def body(*refs):
        ins, outs = refs[:n], refs[n:2 * n]
        ssem, rsem = refs[2 * n:]
        x, y, c, chips = _place()
        sib = (x, y, 1 - c)

        def copy(i, r, half):
            h = ins[i].shape[1] // 2
            px, py = chips[r]
            rows = pl.ds(half * h, h)
            return _remote(ins[i].at[2 * px + py, rows], outs[i].at[2 * px + py, rows], ssem.at[i, r], rsem.at[i, r], sib)

        for i in range(n):
            for r in range(3):
                copy(i, r, c).start()
        for i in range(n):
            for r in range(3):
                copy(i, r, 1 - c).wait_recv()
                copy(i, r, c).wait_send()

    return pl.pallas_call(
        body, in_specs=[ANY] * n, out_specs=[ANY] * n, out_shape=[SDS(a.shape, a.dtype) for a in lands],
        input_output_aliases={i: i for i in range(n)},
        scratch_shapes=[pltpu.SemaphoreType.DMA((n, 3)), pltpu.SemaphoreType.DMA((n, 3))], name=name)(*lands)


def _pair_exchange(name, gs):
    n = len(gs)

    def body(*refs):
        ins, outs = refs[:n], refs[n:2 * n]
        ssem, rsem = refs[2 * n:]
        x, y, c, _ = _place()
        cps = []
        for i in range(n):
            h = ins[i].shape[1] // 2
            cps.append(_remote(ins[i].at[:, pl.ds((1 - c) * h, h)], outs[i], ssem.at[i], rsem.at[i], (x, y, 1 - c)))
            cps[-1].start()
        for cp in cps:
            cp.wait()

    return pl.pallas_call(
        body, in_specs=[ANY] * n, out_specs=[ANY] * n,
        out_shape=[SDS((g.shape[0], g.shape[1] // 2, g.shape[2]), g.dtype) for g in gs],
        scratch_shapes=[pltpu.SemaphoreType.DMA((n,)), pltpu.SemaphoreType.DMA((n,))], name=name)(*gs)


def _half_exchange(name, fs):
    n = len(fs)

    def body(*refs):
        ins, outs = refs[:n], refs[n:2 * n]
        ssem, rsem = refs[2 * n:]
        x, y, c, _ = _place()

        def copy(i, half):
            h = ins[i].shape[0] // 2
            return _remote(ins[i].at[pl.ds(half * h, h)], outs[i].at[pl.ds(half * h, h)], ssem.at[i], rsem.at[i], (x, y, 1 - c))

        for i in range(n):
            copy(i, c).start()
        for i in range(n):
            copy(i, 1 - c).wait_recv()
            copy(i, c).wait_send()

    return pl.pallas_call(
        body, in_specs=[ANY] * n, out_specs=[ANY] * n, out_shape=[SDS(f.shape, f.dtype) for f in fs],
        input_output_aliases={i: i for i in range(n)},
        scratch_shapes=[pltpu.SemaphoreType.DMA((n,)), pltpu.SemaphoreType.DMA((n,))], name=name)(*fs)


def _gather_small(v):
    def body(v_ref, o_ref, lsem, ssem, rsem):
        x, y, c, _ = _place()
        me = 4 * x + 2 * y + c
        flips = [(fx, fy, fc) for fx in (0, 1) for fy in (0, 1) for fc in (0, 1)][1:]
        peer = lambda f: (1 - x if f[0] else x, 1 - y if f[1] else y, 1 - c if f[2] else c)
        loc = pltpu.make_async_copy(v_ref, o_ref.at[me], lsem)
        loc.start()
        cps = [_remote(v_ref, o_ref.at[me], ssem.at[k], rsem.at[k], peer(f)) for k, f in enumerate(flips)]
        for cp in cps:
            cp.start()
        for k, f in enumerate(flips):
            px, py, pc = peer(f)
            land = o_ref.at[4 * px + 2 * py + pc]
            _remote(land, land, ssem.at[k], rsem.at[k], peer(f)).wait_recv()
        for cp in cps:
            cp.wait_send()
        loc.wait()

    return pl.pallas_call(
        body, in_specs=[ANY], out_specs=ANY, out_shape=SDS((N_DEV,) + v.shape, v.dtype),
        scratch_shapes=[pltpu.SemaphoreType.DMA, pltpu.SemaphoreType.DMA((N_DEV - 1,)), pltpu.SemaphoreType.DMA((N_DEV - 1,))],
        name="gather_small")(v)


def _rows_tile(r):
    return _tile(r, (256, 352, 128, 64))


def _pair_sum(name, g, r1, core, me):
    n_l, s_n, h, c = r1.shape
    tr = _rows_tile(h)
    nb = h // tr

    def body(idx_ref, g_ref, r_ref, pb_ref, own_ref):
        p = g_ref[...] + r_ref[...]
        pb_ref[...] = p.astype(BF16)

        @pl.when(pl.program_id(2) == idx_ref[1])
        def _():
            own_ref[...] = p

    blk = (None, None, tr, c)
    grid_spec = pltpu.PrefetchScalarGridSpec(
        num_scalar_prefetch=1, grid=(n_l, nb, s_n),
        in_specs=[pl.BlockSpec(blk, lambda a, i, s, idx: (a, s, idx[0] * nb + i, 0)), pl.BlockSpec(blk, lambda a, i, s, idx: (a, s, i, 0))],
        out_specs=[pl.BlockSpec(blk, lambda a, i, s, idx: (a, s, i, 0)), pl.BlockSpec((None, tr, c), lambda a, i, s, idx: (a, i, 0))])
    idx = jnp.stack([core, me]).astype(jnp.int32)
    return pl.pallas_call(body, grid_spec=grid_spec, out_shape=[SDS(r1.shape, BF16), SDS((n_l, h, c), F32)], name=name)(idx, g, r1)


def _chip_sum(name, own, r2, core):
    n_l, h, c = own.shape
    tr = _rows_tile(h)
    nb = h // tr

    def body(idx_ref, o_ref, r_ref, f_ref):
        f_ref[...] = ((o_ref[...] + r_ref[0].astype(F32)) + r_ref[1].astype(F32)) + r_ref[2].astype(F32)

    grid_spec = pltpu.PrefetchScalarGridSpec(
        num_scalar_prefetch=1, grid=(n_l, nb),
        in_specs=[pl.BlockSpec((None, tr, c), lambda a, i, idx: (a, i, 0)), pl.BlockSpec((3, None, tr, c), lambda a, i, idx: (0, a, i, 0))],
        out_specs=pl.BlockSpec((None, tr, c), lambda a, i, idx: (a, idx[0] * nb + i, 0)))
    return pl.pallas_call(body, grid_spec=grid_spec, out_shape=SDS((n_l, 2 * h, c), F32), name=name)(
        jnp.reshape(core, (1,)).astype(jnp.int32), own, r2)


def _adamw_math(w, g, m, v):
    m = ADAM_B1 * m + (1.0 - ADAM_B1) * g
    v = ADAM_B2 * v + (1.0 - ADAM_B2) * jnp.square(g)
    m_hat = m / (1.0 - ADAM_B1 ** ADAM_STEP)
    v_hat = v / (1.0 - ADAM_B2 ** ADAM_STEP)
    delta = -ADAM_LR * (m_hat / (jnp.sqrt(v_hat) + ADAM_EPS) + ADAM_WD * w)
    return delta, m, v


def _adamw(name, l, w, g, m, v, bufs):
    _, r, c = w.shape
    tr = _rows_tile(r)

    def body(w_ref, g_ref, m_ref, v_ref, *rest):
        go_ref, d_ref, nm_ref, nv_ref = rest[4:]
        g = g_ref[...]
        go_ref[...] = g
        d_ref[...], nm_ref[...], nv_ref[...] = _adamw_math(w_ref[...], g, m_ref[...], v_ref[...])

    blk = pl.BlockSpec((None, tr, c), lambda i: (l, i, 0))
    return pl.pallas_call(body, grid=(r // tr,), in_specs=[blk, pl.BlockSpec((tr, c), lambda i: (i, 0)), blk, blk] + [ANY] * 4,
                          out_specs=[blk] * 4, out_shape=[SDS(w.shape, F32)] * 4, input_output_aliases={4 + k: k for k in range(4)},
                          name=name)(w, g, m, v, *bufs)


def _small_update(parts, w, m, v):
    def body(p_ref, w_ref, m_ref, v_ref, g_ref, d_ref, nm_ref, nv_ref):
        g = p_ref[0]
        for d in range(1, N_DEV):
            g = g + p_ref[d]
        g_ref[...] = g
        d_ref[...], nm_ref[...], nv_ref[...] = _adamw_math(w_ref[...], g, m_ref[...], v_ref[...])

    return pl.pallas_call(body, out_shape=[SDS(w.shape, F32)] * 4, name="small_update")(parts, w, m, v)


SMALL = ["norm_mix_g", "norm_mem_g", "ret_decay_fwd", "ret_decay_bwd", "ret_norm_g", "pool_w", "pool_scale", "na_q_norm_g",
         "na_k_norm_g", "na_rpb", "mem_q_norm_g", "mem_k_norm_g", "norm_ffn_g"]
BIG = ["w_in", "w_gate", "w_mem_kv", "w_branch", "w_out", "w_ffn_in", "w_ffn_out"]
GROUPS = (("w_in", "w_gate", "w_mem_kv", "w_branch", "w_out"), ("w_ffn_in", "w_ffn_out"))
ORDER = ["norm_mix_g", "norm_mem_g", "w_in", "w_gate", "ret_decay_fwd", "ret_decay_bwd", "ret_norm_g", "pool_w", "pool_scale",
         "na_q_norm_g", "na_k_norm_g", "na_rpb", "mem_q_norm_g", "mem_k_norm_g", "w_mem_kv", "w_branch", "w_out", "norm_ffn_g",
         "w_ffn_in", "w_ffn_out"]


def _pack(arrs):
    rows = []
    for a in arrs:
        f = a.reshape(-1).astype(F32)
        pad = (-f.shape[0]) % (8 * LANES)
        rows.append(jnp.pad(f, (0, pad)).reshape(-1, LANES))
    return jnp.concatenate(rows, axis=0)


def _unpack(p, like):
    out, at = [], 0
    for a in like:
        n = int(np.prod(a.shape))
        rows = -(-n // (8 * LANES)) * 8
        out.append(p[at:at + rows].reshape(-1)[:n].reshape(a.shape))
        at += rows
    return out


def _rope_tables(t_len):
    half = HEAD // 2
    inv = ROPE_THETA ** (-jnp.arange(half, dtype=F32) / half)
    ang = jnp.arange(t_len, dtype=F32)[:, None] * inv[None, :]
    cos, sin = jnp.cos(ang), jnp.sin(ang)
    return jnp.tile(jnp.concatenate([cos, cos], axis=1), (1, NH)), jnp.tile(jnp.concatenate([-sin, sin], axis=1), (1, NH))


def _block_diag(pw):
    out = jnp.zeros((BW, BW), pw.dtype)
    for g in range(NH):
        out = out.at[g * HEAD:(g + 1) * HEAD, g * HEAD:(g + 1) * HEAD].set(pw[g])
    return out


def _fold_heads(v):
    return v.reshape(NH, HEAD).sum(axis=0)


def _fold_lanes(v):
    return v.reshape(NH, HEAD).sum(axis=1)


def kernel(x, mem, norm_mix_g, norm_mem_g, w_in, w_gate, ret_decay_fwd, ret_decay_bwd, ret_norm_g, pool_w, pool_scale, na_q_norm_g, na_k_norm_g, na_rpb, mem_q_norm_g, mem_k_norm_g, w_mem_kv, w_branch, w_out, norm_ffn_g, w_ffn_in, w_ffn_out, loss_target, m_norm_mix_g, m_norm_mem_g, m_w_in, m_w_gate, m_ret_decay_fwd, m_ret_decay_bwd, m_ret_norm_g, m_pool_w, m_pool_scale, m_na_q_norm_g, m_na_k_norm_g, m_na_rpb, m_mem_q_norm_g, m_mem_k_norm_g, m_w_mem_kv, m_w_branch, m_w_out, m_norm_ffn_g, m_w_ffn_in, m_w_ffn_out, v_norm_mix_g, v_norm_mem_g, v_w_in, v_w_gate, v_ret_decay_fwd, v_ret_decay_bwd, v_ret_norm_g, v_pool_w, v_pool_scale, v_na_q_norm_g, v_na_k_norm_g, v_na_rpb, v_mem_q_norm_g, v_mem_k_norm_g, v_w_mem_kv, v_w_branch, v_w_out, v_norm_ffn_g, v_w_ffn_in, v_w_ffn_out):
    args = dict(locals())
    W = {n: args[n] for n in ORDER}
    M = {n: args["m_" + n] for n in ORDER}
    V = {n: args["v_" + n] for n in ORDER}
    n_layers, d_model = norm_mix_g.shape
    assert x.shape[0] == 1
    t_len = x.shape[1]
    xc, yc, cc = lax.axis_index("x"), lax.axis_index("y"), lax.axis_index("c")
    me_chip = 2 * xc + yc

    def as3(a):
        return a.reshape(a.shape[0], -1, a.shape[-1])

    gather_waits, token = {}, jnp.zeros((), F32)
    for l in range(n_layers):
        for gi, names in enumerate(GROUPS):
            srcs = [as3(W[n])[l].astype(BF16) for n in names]
            lands = [lax.empty((N_CHIPS,) + a.shape, BF16) for a in srcs]
            tok, gather_waits[l, gi] = _split_copies(f"gather_l{l}_g{gi}", srcs, lands, 4 * len(names), _gather_plan)
            token = token + tok[0, 0]

    cos, sin = _rope_tables(t_len)
    btabs = _na_bias_table(na_rpb)
    lane = lambda a: a.reshape(1, -1).astype(F32)
    x2d, mem2d, tgt = x[0], mem[0], loss_target[0]
    norm_mix_g = norm_mix_g + token

    def fetch(l, gi, after):
        lands = gather_waits[l, gi](after)[1]
        return _split_copies(f"pass_l{l}_g{gi}", [], lands, 3 * len(GROUPS[gi]), _pass_plan)

    saved = []
    _, passing = fetch(0, 0, x2d)
    for l in range(n_layers):
        tag = f"l{l}"
        gathered = dict(zip(GROUPS[0], passing(x2d)[1]))
        wi = gathered["w_in"].transpose(1, 0, 2).reshape(1, 1, d_model, -1)
        wg = gathered["w_gate"][None]
        wmkv = gathered["w_mem_kv"].reshape(1, 1, d_model, 2 * BW)
        wb = gathered["w_branch"].reshape(1, N_CHIPS, NH, BW, BW)
        wo = gathered["w_out"].reshape(1, 1, d_model, d_model)
        s = {"x": x2d}
        s["decf"], s["decb"] = lane(jnp.repeat(ret_decay_fwd[l], HEAD)), lane(jnp.repeat(ret_decay_bwd[l], HEAD))
        s["gn"], s["pscale"] = lane(ret_norm_g[l]), lane(pool_scale[l])
        s["wbd"] = _block_diag(pool_w[l])
        s["nagq"], s["nagk"] = lane(jnp.tile(na_q_norm_g[l], NH)), lane(jnp.tile(na_k_norm_g[l], NH))
        s["mgq"], s["mgk"] = lane(jnp.tile(mem_q_norm_g[l], NH)), lane(jnp.tile(mem_k_norm_g[l], NH))
        s["btab"] = btabs[l]
        s["h"] = _rms_fwd(f"rms_mix_{tag}", x2d, lane(norm_mix_g[l]))
        s["proj"] = _mm_nn(f"mm_in_{tag}", s["h"], wi, 0)
        s["gp3"] = _mm_nn(f"mm_gate_{tag}", s["h"], wg, 0, out3d=True)
        tok, passing = fetch(l, 1, s["gp3"])
        s["hm"] = _rms_fwd(f"rms_mem_{tag}", mem2d, lane(norm_mem_g[l]) + tok[0, 0])
        s["memkv"] = _mm_nn(f"mm_memkv_{tag}", s["hm"], wmkv, 0)
        ret, s["ret_saved"] = _ret_fwd(tag, s["proj"], cos, sin, s["decf"], s["decb"], s["gn"])
        s["pvp"] = jnp.pad(s["proj"][:, 4 * BW:5 * BW], ((POOL_PAD, POOL_PAD), (0, 0)))
        pool = _pool_fwd(tag, s["pvp"], s["wbd"], s["pscale"], t_len)[POOL_PAD:POOL_PAD + t_len]
        s["kn"], s["vb"] = _na_keys(tag, s["proj"], s["nagk"])
        na = _na_fwd(tag, s["proj"], s["kn"], s["vb"], s["btab"], s["nagq"])
        mo = _mem_fwd(tag, s["proj"], s["memkv"], s["mgq"], s["mgk"])
        s["br"] = jnp.stack([ret, pool, na, mo])
        s["merged"] = _gate_fwd(f"gate_{tag}", s["gp3"], s["br"], wb, 0)
        s["x2"] = _mm_nn(f"mm_out_{tag}", s["merged"], wo, 0, add=x2d)
        gathered = dict(zip(GROUPS[1], passing(s["x2"])[1]))
        pin_ffn = jnp.zeros((), F32)
        if l + 1 < n_layers:
            tok, passing = fetch(l + 1, 0, s["x2"])
            pin_ffn = tok[0, 0]
        wfi = gathered["w_ffn_in"][None]
        wfo = gathered["w_ffn_out"].reshape(1, 1, -1, d_model)
        s["w"] = (wi, wg, wmkv, wb, wo, wfi, wfo)
        s["h2"] = _rms_fwd(f"rms_ffn_{tag}", s["x2"], lane(norm_ffn_g[l]) + pin_ffn)
        s["ag3"] = _mm_nn(f"mm_ffn_in_{tag}", s["h2"], wfi, 0, out3d=True)
        s["u"] = _swiglu_fwd(f"swiglu_{tag}", s["ag3"])
        x2d = _mm_nn(f"mm_ffn_out_{tag}", s["u"], wfo, 0, add=s["x2"])
        saved.append(s)

    loss_part, dx = _loss_kernel("loss", x2d, tgt)

    dbtabs = [None] * n_layers
    gsmall = {n: [None] * n_layers for n in SMALL}
    opt_view = {n: (lambda a, n=n: jnp.swapaxes(as3(a), 1, 2) if n == "w_in" else as3(a)) for n in BIG}
    out_bufs = {n: tuple(lax.empty(opt_view[n](W[n]).shape, F32) for _ in range(4)) for n in BIG}

    def reduce_group(l, gi, grads):
        names = GROUPS[gi]
        gs = [grads[n] for n in names]
        lands = [lax.empty((N_CHIPS, g.shape[1] // 2, g.shape[2]), F32) for g in gs]
        tok0, wait0 = _split_copies(f"pairx_l{l}_g{gi}", gs, lands, len(names), _pair_plan)

        def sums_and_scatter(after):
            mine, theirs = wait0(after)
            sums = [_pair_sum(f"pair_sum_{n}_l{l}", g[None], r[None], cc, me_chip) for n, g, r in zip(names, mine, theirs)]
            pbs = [p[0] for p, _ in sums]
            lands1 = [lax.empty((3,) + p.shape[1:], BF16) for p in pbs]
            tok1, wait1 = _split_copies(f"scatter_l{l}_g{gi}", pbs, lands1, 3 * len(names), _scatter_plan)

            def chip_sums_and_halves(after):
                halves = [_chip_sum(f"chip_sum_{n}_l{l}", own, r[:, None], cc)[0] for n, (_, own), r in zip(names, sums, wait1(after)[1])]
                tok2, wait2 = _split_copies(f"halfx_l{l}_g{gi}", [], halves, len(names), _half_plan)

                def update(after):
                    for n, f in zip(names, wait2(after)[1]):
                        out_bufs[n] = _adamw(f"adamw_{n}_l{l}", l, opt_view[n](W[n]), f, opt_view[n](M[n]), opt_view[n](V[n]), out_bufs[n])
                    return None, None

                return update, tok2

            return chip_sums_and_halves, tok1

        return sums_and_scatter, tok0

    def advance(stages, new):
        nxt, tok = new
        out, toks = [nxt], tok
        for stage in stages:
            nxt, tok = stage(toks)
            if nxt is not None:
                out.append(nxt)
                toks = toks + tok
        return out, toks

    stages, order_after = [], None
    for l in reversed(range(n_layers)):
        tag = f"l{l}"
        s = saved[l]
        wi, wg, wmkv, wb, wo, wfi, wfo = s["w"]
        du = _mm_nt(f"mm_ffn_out_dx_{tag}", dx, wfo, 0, after=order_after)
        g_wfo = _mm_tn(f"mm_ffn_out_dw_{tag}", s["u"], dx, 1)
        dag3 = _swiglu_bwd(f"swiglu_bwd_{tag}", s["ag3"], du)
        dh2 = _mm_nt(f"mm_ffn_in_dx_{tag}", dag3, wfi, 0)
        g_wfi = _mm_tn(f"mm_ffn_in_dw_{tag}", s["h2"], dag3, N_CHIPS)
        dx2, dg = _rms_bwd(f"rms_ffn_bwd_{tag}", s["x2"], lane(norm_ffn_g[l]), [dh2], add=dx)
        gsmall["norm_ffn_g"][l] = dg[0]
        stages, tok = advance(stages, reduce_group(l, 1, {"w_ffn_in": g_wfi, "w_ffn_out": g_wfo.reshape(N_CHIPS, -1, d_model)}))
        dmerged = _mm_nt(f"mm_out_dx_{tag}", dx2, wo, 0, after=tok)
        g_wo = _mm_tn(f"mm_out_dw_{tag}", s["merged"], dx2, 1)
        dgp3, dbr, g_wb = _gate_bwd(f"gate_bwd_{tag}", s["gp3"], s["br"], wb, 0, dmerged)
        g_wg = _mm_tn(f"mm_gate_dw_{tag}", s["h"], dgp3, N_CHIPS)
        dh_a = _mm_nt(f"mm_gate_dx_{tag}", dgp3, wg, 0)
        drq, drk, drv, drg, ddf, ddb, dgn = _ret_bwd(tag, s["proj"], cos, sin, s["decf"], s["decb"], s["gn"], s["ret_saved"], dbr[0])
        gsmall["ret_decay_fwd"][l], gsmall["ret_decay_bwd"][l], gsmall["ret_norm_g"][l] = _fold_lanes(ddf), _fold_lanes(ddb), dgn[0]
        dpool_p = jnp.pad(dbr[1], ((POOL_PAD, POOL_PAD), (0, 0)))
        dpvp, dwbd, dps = _pool_bwd(tag, s["pvp"], s["wbd"], s["pscale"], t_len, dpool_p)
        dpv = dpvp[POOL_PAD:POOL_PAD + t_len]
        gsmall["pool_w"][l] = jnp.stack([dwbd[g * HEAD:(g + 1) * HEAD, g * HEAD:(g + 1) * HEAD] for g in range(NH)])
        gsmall["pool_scale"][l] = dps[0]
        dnq, dkn, dnv, dbtabs[l], dgq = _na_bwd(tag, s["proj"], s["kn"], s["vb"], s["btab"], s["nagq"], dbr[2])
        dnk, dnv, dgk = _na_keys_bwd(tag, s["proj"], s["nagk"], dkn, dnv)
        gsmall["na_q_norm_g"][l], gsmall["na_k_norm_g"][l] = _fold_heads(dgq), _fold_heads(dgk)
        dmq, dmemkv, dgq, dgk = _mem_bwd(tag, s["proj"], s["memkv"], s["mgq"], s["mgk"], dbr[3])
        gsmall["mem_q_norm_g"][l], gsmall["mem_k_norm_g"][l] = _fold_heads(dgq), _fold_heads(dgk)
        g_wmkv = _mm_tn(f"mm_memkv_dw_{tag}", s["hm"], dmemkv, 1)
        dhm = _mm_nt(f"mm_memkv_dx_{tag}", dmemkv, wmkv, 0)
        _, dg = _rms_bwd(f"rms_mem_bwd_{tag}", mem2d, lane(norm_mem_g[l]), [dhm])
        gsmall["norm_mem_g"][l] = dg[0]
        dproj = jnp.concatenate([drq, drk, drv, drg, dpv, dnq, dnk, dnv, dmq], axis=1)
        g_wi = _mm_tn(f"mm_in_dw_{tag}", dproj, s["h"], 1)
        dh_b = _mm_nt(f"mm_in_dx_{tag}", dproj, wi, 0)
        dx, dg = _rms_bwd(f"rms_mix_bwd_{tag}", s["x"], lane(norm_mix_g[l]), [dh_a, dh_b], add=dx2)
        gsmall["norm_mix_g"][l] = dg[0]

        stages, order_after = advance(stages, reduce_group(l, 0, {
            "w_in": g_wi.reshape(N_CHIPS, -1, d_model), "w_gate": g_wg, "w_mem_kv": g_wmkv.reshape(N_CHIPS, -1, 2 * BW),
            "w_branch": g_wb.reshape(N_CHIPS, NH * BW, BW), "w_out": g_wo.reshape(N_CHIPS, -1, d_model)}))
    while stages:
        stages = [nxt for nxt, _ in (stage(dx) for stage in stages) if nxt is not None]

    out_g, out_d, out_m, out_v = {}, {}, {}, {}
    for n in BIG:
        shp = W[n].shape
        back = (lambda b: jnp.swapaxes(b, 1, 2)) if n == "w_in" else (lambda b: b)
        out_g[n], out_d[n], out_m[n], out_v[n] = [back(b).reshape(shp) for b in out_bufs[n]]

    gsmall["na_rpb"] = list(_na_bias_grad(jnp.stack(dbtabs)))
    small_g = [jnp.stack(gsmall[n]).reshape(W[n].shape) for n in SMALL] + [loss_part]
    like = [W[n] for n in SMALL] + [loss_part]
    zero = jnp.zeros((1, 1), F32)
    parts = _gather_small(_pack(small_g))
    sg, sd, sm, sv = _small_update(parts, _pack([W[n] for n in SMALL] + [zero]), _pack([M[n] for n in SMALL] + [zero]),
                                   _pack([V[n] for n in SMALL] + [zero]))
    sg, sd, sm, sv = _unpack(sg, like), _unpack(sd, like), _unpack(sm, like), _unpack(sv, like)
    for i, n in enumerate(SMALL):
        out_g[n], out_d[n], out_m[n], out_v[n] = sg[i], sd[i], sm[i], sv[i]
    loss = sg[-1].reshape(())

    return (loss, dx.reshape(x.shape), *[out_g[n] for n in ORDER], *[out_d[n] for n in ORDER],
            *[out_m[n] for n in ORDER], *[out_v[n] for n in ORDER])
```

```python
import functools

import numpy as np
import jax
import jax.numpy as jnp
from jax import lax
from jax.experimental import pallas as pl
from jax.experimental.pallas import tpu as pltpu

F32 = jnp.float32
BF16 = jnp.bfloat16
SDS = jax.ShapeDtypeStruct
MESH = pl.DeviceIdType.MESH
ANY = pl.BlockSpec(memory_space=pl.ANY)

HEAD = 64
NH = 4
BW = NH * HEAD
CH = 128
GRID_W = 64
NA_ROWS = 8
NA_COLS = 16
POOL_HALF = (1, 2, 4, 8)
POOL_PAD = 16
ROPE_THETA = 10000.0
EPS = 1e-6
NEG_INF = -1e30
N_CHIPS = 4
N_DEV = 8
LANES = 128

ADAM_LR = 0.001
ADAM_B1 = 0.9
ADAM_B2 = 0.999
ADAM_EPS = 1e-08
ADAM_WD = 0.01
ADAM_STEP = 10

VMEM_BIG = 56 << 20
VMEM_MM = 44 << 20


def _cp(vmem=None):
    return pltpu.CompilerParams(vmem_limit_bytes=vmem) if vmem else None


def _d(a, b, ca, cb):
    return lax.dot_general(a.astype(BF16), b.astype(BF16), (((ca,), (cb,)), ((), ())),
                           preferred_element_type=F32)


@jax.custom_vjp
def _nn(a, b):
    return _d(a, b, 1, 0)


def _nn_f(a, b):
    return _d(a, b, 1, 0), (a, b)


def _nn_b(r, g):
    a, b = r
    return _d(g, b, 1, 1).astype(a.dtype), _d(a, g, 0, 0).astype(b.dtype)


_nn.defvjp(_nn_f, _nn_b)


@jax.custom_vjp
def _nt(a, b):
    return _d(a, b, 1, 1)


def _nt_f(a, b):
    return _d(a, b, 1, 1), (a, b)


def _nt_b(r, g):
    a, b = r
    return _d(g, b, 1, 0).astype(a.dtype), _d(g, a, 0, 0).astype(b.dtype)


_nt.defvjp(_nt_f, _nt_b)


@jax.custom_vjp
def _tn(a, b):
    return _d(a, b, 0, 0)


def _tn_f(a, b):
    return _d(a, b, 0, 0), (a, b)


def _tn_b(r, g):
    a, b = r
    return _d(b, g, 1, 1).astype(a.dtype), _d(a, g, 1, 0).astype(b.dtype)


_tn.defvjp(_tn_f, _tn_b)


@functools.partial(jax.custom_vjp, nondiff_argnums=(1,))
def _rollr(x, s):
    return pltpu.roll(x, s % x.shape[0], 0)


def _rollr_f(x, s):
    return _rollr(x, s), None


def _rollr_b(s, _, g):
    return (_rollr(g, -s),)


_rollr.defvjp(_rollr_f, _rollr_b)


@jax.custom_vjp
def _swap32(t):
    n = t.shape[1]
    lane = lax.broadcasted_iota(jnp.int32, (1, n), 1)
    return jnp.where((lane & (HEAD // 2)) == 0, pltpu.roll(t, n - HEAD // 2, 1), pltpu.roll(t, HEAD // 2, 1))


def _swap32_f(t):
    return _swap32(t), None


def _swap32_b(_, g):
    return (_swap32(g),)


_swap32.defvjp(_swap32_f, _swap32_b)


def _head_masks():
    lane = lax.broadcasted_iota(jnp.int32, (1, BW), 1)
    return [(lane >= HEAD * h) & (lane < HEAD * (h + 1)) for h in range(NH)]


def _head_rms(x, g):
    out = jnp.zeros_like(x)
    for mh in _head_masks():
        ms = jnp.sum(jnp.where(mh, x * x, 0.0), axis=-1, keepdims=True) * (1.0 / HEAD)
        out = out + jnp.where(mh, x * lax.rsqrt(ms + EPS), 0.0)
    return out * g


def _stack_heads(x):
    return jnp.concatenate([jnp.where(mh, x, 0.0) for mh in _head_masks()], axis=0)


def _unstack_heads(y):
    n = y.shape[0] // NH
    out = None
    for h, mh in enumerate(_head_masks()):
        t = jnp.where(mh, y[h * n:(h + 1) * n], 0.0)
        out = t if out is None else out + t
    return out


def _rms(x, g):
    return x * lax.rsqrt(jnp.mean(x * x, axis=-1, keepdims=True) + EPS) * g


def _rot(t, cos, sin):
    return t * cos + _swap32(t) * sin


def _softmax(s):
    e = jnp.exp(s - lax.stop_gradient(jnp.max(s, axis=-1, keepdims=True)))
    return e / jnp.sum(e, axis=-1, keepdims=True)


def _ret_kv_f(k, v, cos, sin, decf, decb):
    lgf, lgb = jax.nn.log_sigmoid(decf), jax.nn.log_sigmoid(decb)
    kr = _rot(k, cos, sin)
    idx = lax.broadcasted_iota(jnp.int32, (CH, 1), 0).astype(F32)
    r = lax.broadcasted_iota(jnp.int32, (BW, BW), 0) // HEAD
    c = lax.broadcasted_iota(jnp.int32, (BW, BW), 1) // HEAD
    bd = r == c
    kvf = jnp.where(bd, _tn(kr * jnp.exp((CH - 1 - idx) * lgf), v), 0.0)
    kvb = jnp.where(bd, _tn(kr * jnp.exp(idx * lgb), v), 0.0)
    return kvf, kvb


def _ret_scan_f(kvf, kvb, decf, decb):
    n = len(kvf)
    cdf = jnp.exp(CH * jax.nn.log_sigmoid(decf))
    cdb = jnp.exp(CH * jax.nn.log_sigmoid(decb))
    z = jnp.zeros((BW, BW), F32)
    sf, st = [z], z
    for a in range(n - 1):
        st = cdf * st + kvf[a]
        sf.append(st)
    sb, st = [z], z
    for a in range(n - 1, 0, -1):
        st = cdb * st + kvb[a]
        sb.append(st)
    return sf, sb[::-1]


def _ret_out_f(q, k, v, g, cos, sin, stf, stb, decf, decb, gn):
    lgf, lgb = jax.nn.log_sigmoid(decf), jax.nn.log_sigmoid(decb)
    qr = _rot(q, cos, sin) * (HEAD ** -0.5)
    kr = _rot(k, cos, sin)
    diff = (lax.broadcasted_iota(jnp.int32, (CH, CH), 0) - lax.broadcasted_iota(jnp.int32, (CH, CH), 1)).astype(F32)
    decay = []
    for mh in _head_masks():
        lf = jnp.sum(jnp.where(mh, lgf, 0.0), axis=1, keepdims=True) * (1.0 / HEAD)
        lb = jnp.sum(jnp.where(mh, lgb, 0.0), axis=1, keepdims=True) * (1.0 / HEAD)
        decay.append(jnp.exp(jnp.where(diff >= 0, diff * lf, -diff * lb)))
    p = _nt(_stack_heads(qr), kr) * jnp.concatenate(decay, axis=0)
    o = _unstack_heads(_nn(p, v))
    idx = lax.broadcasted_iota(jnp.int32, (CH, 1), 0).astype(F32)
    o = o + _nn(qr * jnp.exp((idx + 1) * lgf), stf) + _nn(qr * jnp.exp((CH - idx) * lgb), stb)
    return _head_rms(o, gn) * (g * jax.nn.sigmoid(g))


def _pool_f(pvp, wbd, scale, t_len):
    n = pvp.shape[0]
    w2 = pvp + _rollr(pvp, 1)
    w4 = _rollr(w2, 1) + _rollr(w2, -1)
    w8 = _rollr(w4, 2) + _rollr(w4, -2)
    w16 = _rollr(w8, 4) + _rollr(w8, -4)
    grp = lax.broadcasted_iota(jnp.int32, (1, BW), 1) // HEAD
    ws = jnp.where(grp == 0, w2, jnp.where(grp == 1, w4, jnp.where(grp == 2, w8, w16)))
    half = jnp.where(grp == 0, POOL_HALF[0], jnp.where(grp == 1, POOL_HALF[1], jnp.where(grp == 2, POOL_HALF[2], POOL_HALF[3])))
    t = lax.broadcasted_iota(jnp.int32, (n, 1), 0) - POOL_PAD
    cnt = jnp.minimum(t + half, t_len) - jnp.maximum(t - half, 0)
    cnt = jnp.where((t >= 0) & (t < t_len), cnt, 1).astype(F32)
    pooled = ws / cnt - pvp
    return _nn(pooled, wbd) * scale


def _na_f(q, kn, vw, bias, gq):
    qn = _head_rms(q, gq)
    bias_all = jnp.concatenate([jnp.concatenate(bias[h], axis=1) for h in range(NH)], axis=0)
    s = _nt(_stack_heads(qn), kn) * (HEAD ** -0.5) + bias_all
    return _unstack_heads(_nn(_softmax(s), vw))


def _mem_f(q, mk, mv, gq, gk):
    qn = _head_rms(q, gq)
    kn = _head_rms(mk, gk)
    s = _nt(_stack_heads(qn), kn) * (HEAD ** -0.5)
    return _unstack_heads(_nn(_softmax(s), mv))


def _gate_f(gp, br, wb):
    out = None
    for n in range(NH):
        t = jax.nn.sigmoid(gp[n]) * _nn(br[n], wb[n])
        out = t if out is None else out + t
    return out


def _swiglu_f(a, g):
    return a * jax.nn.sigmoid(a) * g


MM_ROWS = (1024, 512, 256)
MM_COLS = (1152, 1024, 768, 512)


def _tile(n, prefs):
    for p in prefs:
        if n % p == 0:
            return p
    return n


def _mm_call(name, a, b, extra, dn, grid, a_spec, b_spec, extra_specs, o_spec, out_shape, acc_shape, nred, after=None):
    add = len(extra) == 1
    if after is not None:
        extra, extra_specs = [*extra, after], [*extra_specs, ANY]
    n_extra = len(extra)
    n_steps = int(np.prod([grid[len(grid) - 1 - i] for i in range(nred)]))

    def body(*refs):
        a_ref, b_ref = refs[0], refs[1]
        o_ref, acc = refs[2 + n_extra], refs[-1]
        red = [len(grid) - 1 - i for i in range(nred)]
        first = functools.reduce(jnp.logical_and, [pl.program_id(ax) == 0 for ax in red])
        last = functools.reduce(jnp.logical_and, [pl.program_id(ax) == grid[ax] - 1 for ax in red])
        d = lax.dot_general(a_ref[...].astype(BF16), b_ref[...].astype(BF16), dn, preferred_element_type=F32)

        def finish(r):
            if add:
                r = r + refs[2][...]
            o_ref[...] = r.astype(o_ref.dtype)

        if n_steps == 1:
            finish(d)
        else:
            @pl.when(first)
            def _():
                acc[...] = d

            @pl.when(jnp.logical_not(first) & jnp.logical_not(last))
            def _():
                acc[...] += d

            @pl.when(last)
            def _():
                finish(acc[...] + d)

    return pl.pallas_call(
        body, grid=grid, in_specs=[a_spec, b_spec, *extra_specs], out_specs=o_spec, out_shape=out_shape,
        scratch_shapes=[pltpu.VMEM(acc_shape, F32)] if n_steps > 1 else [], name=name, compiler_params=_cp(VMEM_MM),
    )(a, b, *extra)


def _mm_nn(name, a, w, l, *, out3d=False, add=None, out_dtype=F32):
    m, k = a.shape
    _, s_n, _, ns = w.shape
    tm = _tile(m, MM_ROWS)
    tk = _tile(k, (1024, 1408, 512, 256))
    tn = ns if ns <= 1408 else _tile(ns, MM_COLS)
    grid = (m // tm, s_n, ns // tn, k // tk)
    a_spec = pl.BlockSpec((tm, tk), lambda i, s, j, kk: (i, kk))
    b_spec = pl.BlockSpec((None, None, tk, tn), lambda i, s, j, kk: (l, s, kk, j))
    if out3d:
        o_spec = pl.BlockSpec((None, tm, tn), lambda i, s, j, kk: (s, i, j))
        out_shape = SDS((s_n, m, ns), out_dtype)
    else:
        nj = ns // tn
        o_spec = pl.BlockSpec((tm, tn), lambda i, s, j, kk: (i, s * nj + j))
        out_shape = SDS((m, s_n * ns), out_dtype)
    extra, especs = [], []
    if add is not None:
        assert not out3d
        nj = ns // tn
        extra, especs = [add], [pl.BlockSpec((tm, tn), lambda i, s, j, kk: (i, s * nj + j))]
    return _mm_call(name, a, w, extra, (((1,), (0,)), ((), ())), grid, a_spec, b_spec, especs, o_spec, out_shape, (tm, tn), 1)


def _mm_nt(name, g, w, l, after=None):
    _, s_n, k, ns = w.shape
    g3d = g.ndim == 3
    m = g.shape[1] if g3d else g.shape[0]
    tm = _tile(m, MM_ROWS)
    to = _tile(k, (1024, 1408, 512, 256))
    tk = ns if (ns % LANES or ns <= 1408) else _tile(ns, MM_COLS)
    nks = ns // tk
    grid = (m // tm, k // to, s_n, nks)
    if g3d:
        a_spec = pl.BlockSpec((None, tm, tk), lambda i, j, s, r: (s, i, r))
    else:
        a_spec = pl.BlockSpec((tm, tk), lambda i, j, s, r: (i, s * nks + r))
    b_spec = pl.BlockSpec((None, None, to, tk), lambda i, j, s, r: (l, s, j, r))
    o_spec = pl.BlockSpec((tm, to), lambda i, j, s, r: (i, j))
    return _mm_call(name, g, w, [], (((1,), (1,)), ((), ())), grid, a_spec, b_spec, [], o_spec, SDS((m, k), F32), (tm, to), 2,
                    after=after)


def _mm_tn(name, a, g, s_n):
    t_len, k = a.shape
    g3d = g.ndim == 3
    ns = g.shape[2] if g3d else g.shape[1] // s_n
    tt = _tile(t_len, (1024, 512))
    to = _tile(k, (1024, 768, 1408, 512, 256, 128))
    tn = ns
    nj = ns // tn
    grid = (s_n, k // to, nj, t_len // tt)
    a_spec = pl.BlockSpec((tt, to), lambda s, i, j, t: (t, i))
    if g3d:
        b_spec = pl.BlockSpec((None, tt, tn), lambda s, i, j, t: (s, t, j))
    else:
        b_spec = pl.BlockSpec((tt, tn), lambda s, i, j, t: (t, s * nj + j))
    o_spec = pl.BlockSpec((None, to, tn), lambda s, i, j, t: (s, i, j))
    return _mm_call(name, a, g, [], (((0,), (0,)), ((), ())), grid, a_spec, b_spec, [], o_spec,
                    SDS((s_n, k, ns), F32), (to, tn), 1)


def _rms_fwd(name, x, g):
    m, d = x.shape
    tm = _tile(m, (256,))

    def body(x_ref, g_ref, o_ref):
        o_ref[...] = _rms(x_ref[...], g_ref[...]).astype(BF16)

    row = pl.BlockSpec((tm, d), lambda i: (i, 0))
    return pl.pallas_call(body, grid=(m // tm,), in_specs=[row, pl.BlockSpec((1, d), lambda i: (0, 0))], out_specs=row,
                          out_shape=SDS((m, d), BF16), name=name)(x, g)


def _rms_bwd(name, x, g, dhs, add=None):
    m, d = x.shape
    tm = _tile(m, (256,))
    nh = len(dhs)

    def body(*refs):
        x_ref, g_ref = refs[0], refs[1]
        dh = refs[2][...]
        for r in refs[3:2 + nh]:
            dh = dh + r[...]
        dx_ref, dg_ref = refs[-2], refs[-1]
        _, vjp = jax.vjp(_rms, x_ref[...], g_ref[...])
        dx, dg = vjp(dh)
        if add is not None:
            dx = dx + refs[2 + nh][...]
        dx_ref[...] = dx

        @pl.when(pl.program_id(0) == 0)
        def _():
            dg_ref[...] = jnp.zeros_like(dg_ref)

        dg_ref[...] += dg

    row = pl.BlockSpec((tm, d), lambda i: (i, 0))
    vec = pl.BlockSpec((1, d), lambda i: (0, 0))
    ins = [x, g, *dhs] + ([add] if add is not None else [])
    return pl.pallas_call(body, grid=(m // tm,), in_specs=[row, vec] + [row] * (len(ins) - 2), out_specs=[row, vec],
                          out_shape=[SDS((m, d), F32), SDS((1, d), F32)], name=name)(*ins)


def _swiglu_fwd(name, ag3):
    _, t_len, w = ag3.shape
    tm = _tile(t_len, (512, 256))

    def body(a_ref, g_ref, o_ref):
        o_ref[...] = _swiglu_f(a_ref[...], g_ref[...]).astype(BF16)

    return pl.pallas_call(
        body, grid=(t_len // tm, 2),
        in_specs=[pl.BlockSpec((None, tm, w), lambda i, j: (j, i, 0)), pl.BlockSpec((None, tm, w), lambda i, j: (j + 2, i, 0))],
        out_specs=pl.BlockSpec((tm, w), lambda i, j: (i, j)), out_shape=SDS((t_len, 2 * w), BF16), name=name)(ag3, ag3)


def _swiglu_bwd(name, ag3, du):
    _, t_len, w = ag3.shape
    tm = _tile(t_len, (512, 256))

    def body(a_ref, g_ref, du_ref, o_ref):
        _, vjp = jax.vjp(_swiglu_f, a_ref[...], g_ref[...])
        da, dg = vjp(du_ref[...])
        o_ref[0] = da.astype(BF16)
        o_ref[1] = dg.astype(BF16)

    out = pl.pallas_call(
        body, grid=(t_len // tm, 2),
        in_specs=[pl.BlockSpec((None, tm, w), lambda i, j: (j, i, 0)), pl.BlockSpec((None, tm, w), lambda i, j: (j + 2, i, 0)),
                  pl.BlockSpec((tm, w), lambda i, j: (i, j))],
        out_specs=pl.BlockSpec((2, None, tm, w), lambda i, j: (0, j, i, 0)), out_shape=SDS((2, 2, t_len, w), BF16), name=name)(ag3, ag3, du)
    return out.reshape(ag3.shape)


def _gate_fwd(name, gp3, br, wb, l):
    _, t_len, d = gp3.shape
    tm = _tile(t_len, (512, 256))

    def body(gp_ref, br_ref, wb_ref, o_ref):
        o_ref[...] = _gate_f([gp_ref[n] for n in range(NH)], [br_ref[n] for n in range(NH)],
                             [wb_ref[n] for n in range(NH)]).astype(BF16)

    return pl.pallas_call(
        body, grid=(t_len // tm, d // BW),
        in_specs=[pl.BlockSpec((NH, tm, BW), lambda i, s: (0, i, s)), pl.BlockSpec((NH, tm, BW), lambda i, s: (0, i, 0)),
                  pl.BlockSpec((None, None, NH, BW, BW), lambda i, s: (l, s, 0, 0, 0))],
        out_specs=pl.BlockSpec((tm, BW), lambda i, s: (i, s)), out_shape=SDS((t_len, d), BF16), name=name)(gp3, br, wb)


def _gate_bwd(name, gp3, br, wb, l, dmerged):
    _, t_len, d = gp3.shape
    tm = _tile(t_len, (512, 256))
    ns = d // BW

    def body(gp_ref, br_ref, wb_ref, dm_ref, dgp_ref, dbr_ref, dwb_ref):
        i, s = pl.program_id(0), pl.program_id(1)
        gp = [gp_ref[n] for n in range(NH)]
        brv = [br_ref[n].astype(F32) for n in range(NH)]
        wbv = [wb_ref[n].astype(F32) for n in range(NH)]
        _, vjp = jax.vjp(_gate_f, gp, brv, wbv)
        dgp, dbr, dwb = vjp(dm_ref[...])

        @pl.when(s == 0)
        def _():
            dbr_ref[...] = jnp.zeros_like(dbr_ref)

        @pl.when((i == 0) & (s == 0))
        def _():
            dwb_ref[...] = jnp.zeros_like(dwb_ref)

        for n in range(NH):
            dgp_ref[n] = dgp[n].astype(BF16)
            dbr_ref[n] += dbr[n]
            dwb_ref[s, n] += dwb[n]

    return pl.pallas_call(
        body, grid=(t_len // tm, ns),
        in_specs=[pl.BlockSpec((NH, tm, BW), lambda i, s: (0, i, s)), pl.BlockSpec((NH, tm, BW), lambda i, s: (0, i, 0)),
                  pl.BlockSpec((None, None, NH, BW, BW), lambda i, s: (l, s, 0, 0, 0)), pl.BlockSpec((tm, BW), lambda i, s: (i, s))],
        out_specs=[pl.BlockSpec((NH, tm, BW), lambda i, s: (0, i, s)), pl.BlockSpec((NH, tm, BW), lambda i, s: (0, i, 0)),
                   pl.BlockSpec((ns, NH, BW, BW), lambda i, s: (0, 0, 0, 0))],
        out_shape=[SDS(gp3.shape, BF16), SDS((NH, t_len, BW), F32), SDS((ns, NH, BW, BW), F32)], name=name,
        compiler_params=_cp(VMEM_MM))(gp3, br, wb, dmerged)


def _loss_kernel(name, y, tgt):
    m, d = y.shape
    tm = _tile(m, (256,))

    def body(y_ref, t_ref, l_ref, dy_ref):
        err = y_ref[...] - t_ref[...]
        dy_ref[...] = err * (1.0 / d)

        @pl.when(pl.program_id(0) == 0)
        def _():
            l_ref[...] = jnp.zeros_like(l_ref)

        l_ref[...] += 0.5 * jnp.sum(jnp.mean(err * err, axis=-1, keepdims=True), axis=0, keepdims=True)

    row = pl.BlockSpec((tm, d), lambda i: (i, 0))
    return pl.pallas_call(body, grid=(m // tm,), in_specs=[row, row], out_specs=[pl.BlockSpec((1, 1), lambda i: (0, 0)), row],
                          out_shape=[SDS((1, 1), F32), SDS((m, d), F32)], name=name)(y, tgt)


def _zero_first(refs):
    @pl.when(pl.program_id(0) == 0)
    def _():
        for r in refs:
            r[...] = jnp.zeros_like(r)


def _grp(rows, j):
    return pl.BlockSpec((rows, BW), lambda a, j=j: (a, j))


_VEC = pl.BlockSpec((1, BW), lambda a: (0, 0))


def _ret_fwd(tag, proj, cos, sin, decf, decb, gn):
    t_len = proj.shape[0]
    n = t_len // CH
    tab = pl.BlockSpec((CH, BW), lambda a: (a, 0))
    st = pl.BlockSpec((None, BW, BW), lambda a: (a, 0, 0))

    def k1(k_ref, v_ref, cos_ref, sin_ref, df_ref, db_ref, kvf_ref, kvb_ref):
        kvf_ref[...], kvb_ref[...] = _ret_kv_f(k_ref[...], v_ref[...], cos_ref[...], sin_ref[...], df_ref[...], db_ref[...])

    kvf, kvb = pl.pallas_call(
        k1, grid=(n,), in_specs=[_grp(CH, 1), _grp(CH, 2), tab, tab, _VEC, _VEC], out_specs=[st, st],
        out_shape=[SDS((n, BW, BW), F32)] * 2, name=f"ret_kv_{tag}")(proj, proj, cos, sin, decf, decb)

    def k2(kvf_ref, kvb_ref, df_ref, db_ref, sf_ref, sb_ref):
        sf, sb = _ret_scan_f([kvf_ref[a] for a in range(n)], [kvb_ref[a] for a in range(n)], df_ref[...], db_ref[...])
        for a in range(n):
            sf_ref[a] = sf[a]
            sb_ref[a] = sb[a]

    stf, stb = pl.pallas_call(k2, out_shape=[SDS((n, BW, BW), F32)] * 2, name=f"ret_scan_{tag}",
                              compiler_params=_cp(VMEM_BIG))(kvf, kvb, decf, decb)

    def k3(q_ref, k_ref, v_ref, g_ref, cos_ref, sin_ref, sf_ref, sb_ref, df_ref, db_ref, gn_ref, o_ref):
        o_ref[...] = _ret_out_f(q_ref[...], k_ref[...], v_ref[...], g_ref[...], cos_ref[...], sin_ref[...], sf_ref[...],
                                sb_ref[...], df_ref[...], db_ref[...], gn_ref[...]).astype(BF16)

    ret = pl.pallas_call(
        k3, grid=(n,), in_specs=[_grp(CH, 0), _grp(CH, 1), _grp(CH, 2), _grp(CH, 3), tab, tab, st, st, _VEC, _VEC, _VEC],
        out_specs=tab, out_shape=SDS((t_len, BW), BF16), name=f"ret_out_{tag}")(proj, proj, proj, proj, cos, sin, stf, stb, decf, decb, gn)
    return ret, (kvf, kvb, stf, stb)


def _ret_bwd(tag, proj, cos, sin, decf, decb, gn, saved, dret):
    kvf, kvb, stf, stb = saved
    t_len = proj.shape[0]
    n = t_len // CH
    tab = pl.BlockSpec((CH, BW), lambda a: (a, 0))
    st = pl.BlockSpec((None, BW, BW), lambda a: (a, 0, 0))

    def k3(q_ref, k_ref, v_ref, g_ref, cos_ref, sin_ref, sf_ref, sb_ref, df_ref, db_ref, gn_ref, do_ref,
           dq_ref, dk_ref, dv_ref, dg_ref, dsf_ref, dsb_ref, ddf_ref, ddb_ref, dgn_ref):
        cos, sin = cos_ref[...], sin_ref[...]
        f = lambda q, k, v, g, sf, sb, df, db, gnv: _ret_out_f(q, k, v, g, cos, sin, sf, sb, df, db, gnv)
        _, vjp = jax.vjp(f, q_ref[...], k_ref[...], v_ref[...], g_ref[...], sf_ref[...], sb_ref[...], df_ref[...], db_ref[...], gn_ref[...])
        dq, dk, dv, dg, dsf, dsb, ddf, ddb, dgn = vjp(do_ref[...])
        dq_ref[...] = dq.astype(BF16)
        dk_ref[...] = dk
        dv_ref[...] = dv
        dg_ref[...] = dg.astype(BF16)
        dsf_ref[...] = dsf
        dsb_ref[...] = dsb
        _zero_first([ddf_ref, ddb_ref, dgn_ref])
        ddf_ref[...] += ddf
        ddb_ref[...] += ddb
        dgn_ref[...] += dgn

    dq, dk3, dv3, dg, dstf, dstb, ddf3, ddb3, dgn = pl.pallas_call(
        k3, grid=(n,),
        in_specs=[_grp(CH, 0), _grp(CH, 1), _grp(CH, 2), _grp(CH, 3), tab, tab, st, st, _VEC, _VEC, _VEC, tab],
        out_specs=[tab, tab, tab, tab, st, st, _VEC, _VEC, _VEC],
        out_shape=[SDS((t_len, BW), BF16), SDS((t_len, BW), F32), SDS((t_len, BW), F32), SDS((t_len, BW), BF16),
                   SDS((n, BW, BW), F32), SDS((n, BW, BW), F32), SDS((1, BW), F32), SDS((1, BW), F32), SDS((1, BW), F32)],
        name=f"ret_out_bwd_{tag}")(proj, proj, proj, proj, cos, sin, stf, stb, decf, decb, gn, dret)

    def k2(kvf_ref, kvb_ref, df_ref, db_ref, dsf_ref, dsb_ref, dkvf_ref, dkvb_ref, ddf_ref, ddb_ref):
        _, vjp = jax.vjp(_ret_scan_f, [kvf_ref[a] for a in range(n)], [kvb_ref[a] for a in range(n)], df_ref[...], db_ref[...])
        dkvf, dkvb, ddf_ref[...], ddb_ref[...] = vjp(([dsf_ref[a] for a in range(n)], [dsb_ref[a] for a in range(n)]))
        for a in range(n):
            dkvf_ref[a] = dkvf[a]
            dkvb_ref[a] = dkvb[a]

    dkvf, dkvb, ddf2, ddb2 = pl.pallas_call(
        k2, out_shape=[SDS((n, BW, BW), F32), SDS((n, BW, BW), F32), SDS((1, BW), F32), SDS((1, BW), F32)],
        name=f"ret_scan_bwd_{tag}", compiler_params=_cp(VMEM_BIG))(kvf, kvb, decf, decb, dstf, dstb)

    def k1(k_ref, v_ref, cos_ref, sin_ref, df_ref, db_ref, dkvf_ref, dkvb_ref, dk3_ref, dv3_ref, dk_ref, dv_ref, ddf_ref, ddb_ref):
        cos, sin = cos_ref[...], sin_ref[...]
        f = lambda k, v, df, db: _ret_kv_f(k, v, cos, sin, df, db)
        _, vjp = jax.vjp(f, k_ref[...], v_ref[...], df_ref[...], db_ref[...])
        dk, dv, ddf, ddb = vjp((dkvf_ref[...], dkvb_ref[...]))
        dk_ref[...] = (dk + dk3_ref[...]).astype(BF16)
        dv_ref[...] = (dv + dv3_ref[...]).astype(BF16)
        _zero_first([ddf_ref, ddb_ref])
        ddf_ref[...] += ddf
        ddb_ref[...] += ddb

    dk, dv, ddf1, ddb1 = pl.pallas_call(
        k1, grid=(n,), in_specs=[_grp(CH, 1), _grp(CH, 2), tab, tab, _VEC, _VEC, st, st, tab, tab],
        out_specs=[tab, tab, _VEC, _VEC],
        out_shape=[SDS((t_len, BW), BF16), SDS((t_len, BW), BF16), SDS((1, BW), F32), SDS((1, BW), F32)],
        name=f"ret_kv_bwd_{tag}")(proj, proj, cos, sin, decf, decb, dkvf, dkvb, dk3, dv3)
    return dq, dk, dv, dg, ddf1 + ddf2 + ddf3, ddb1 + ddb2 + ddb3, dgn


def _pool_fwd(tag, pvp, wbd, scale, t_len):
    def body(p_ref, w_ref, s_ref, o_ref):
        o_ref[...] = _pool_f(p_ref[...], w_ref[...], s_ref[...], t_len).astype(BF16)

    return pl.pallas_call(body, out_shape=SDS(pvp.shape, BF16), name=f"pool_{tag}", compiler_params=_cp(VMEM_BIG))(pvp, wbd, scale)


def _pool_bwd(tag, pvp, wbd, scale, t_len, dpool_p):
    def body(p_ref, w_ref, s_ref, do_ref, dp_ref, dw_ref, ds_ref):
        f = lambda p, w, s: _pool_f(p, w, s, t_len)
        _, vjp = jax.vjp(f, p_ref[...], w_ref[...], s_ref[...])
        dp, dw, ds = vjp(do_ref[...])
        dp_ref[...] = dp.astype(BF16)
        dw_ref[...] = dw
        ds_ref[...] = ds

    return pl.pallas_call(body, out_shape=[SDS(pvp.shape, BF16), SDS((BW, BW), F32), SDS((1, BW), F32)],
                          name=f"pool_bwd_{tag}", compiler_params=_cp(VMEM_BIG))(pvp, wbd, scale, dpool_p)


def _na_window(r, rows):
    r0 = jnp.clip(r - NA_ROWS // 2, 0, rows - NA_ROWS)
    return r0, r - r0


NA_PAIRS = 2 * NA_ROWS - 2


def _na_bias_pieces(b_ref, pat):
    return [[b_ref[h * NA_PAIRS + 2 * k - pat + NA_ROWS - 1] for k in range(NA_ROWS // 2)] for h in range(NH)]


def _na_keys(tag, proj, gk):
    t_len = proj.shape[0]
    tm = _tile(t_len, (256,))

    def body(k_ref, v_ref, gk_ref, kn_ref, vb_ref):
        kn_ref[...] = _head_rms(k_ref[...], gk_ref[...]).astype(BF16)
        vb_ref[...] = v_ref[...].astype(BF16)

    row = pl.BlockSpec((tm, BW), lambda a: (a, 0))
    return pl.pallas_call(body, grid=(t_len // tm,), in_specs=[_grp(tm, 6), _grp(tm, 7), _VEC], out_specs=[row, row],
                          out_shape=[SDS((t_len, BW), BF16)] * 2, name=f"na_keys_{tag}")(proj, proj, gk)


def _na_keys_bwd(tag, proj, gk, dkn, dv):
    t_len = proj.shape[0]
    tm = _tile(t_len, (256,))

    def body(k_ref, gk_ref, dkn_ref, dv_ref, dk_ref, dvb_ref, dgk_ref):
        _, vjp = jax.vjp(_head_rms, k_ref[...], gk_ref[...])
        dk, dgk = vjp(dkn_ref[...])
        dk_ref[...] = dk.astype(BF16)
        dvb_ref[...] = dv_ref[...].astype(BF16)
        _zero_first([dgk_ref])
        dgk_ref[...] += dgk

    row = pl.BlockSpec((tm, BW), lambda a: (a, 0))
    return pl.pallas_call(body, grid=(t_len // tm,), in_specs=[_grp(tm, 6), _VEC, row, row], out_specs=[row, row, _VEC],
                          out_shape=[SDS((t_len, BW), BF16), SDS((t_len, BW), BF16), SDS((1, BW), F32)],
                          name=f"na_keys_bwd_{tag}")(proj, gk, dkn, dv)


def _na_fwd(tag, proj, kn, vb, bias_tab, gq):
    t_len = proj.shape[0]
    rows = t_len // GRID_W
    win = NA_ROWS * GRID_W

    def body(q_ref, k_ref, v_ref, b_ref, gq_ref, o_ref):
        r0, pat = _na_window(pl.program_id(0), rows)
        start = pl.multiple_of(r0 * GRID_W, GRID_W)
        o_ref[...] = _na_f(q_ref[...], k_ref[pl.ds(start, win), :], v_ref[pl.ds(start, win), :], _na_bias_pieces(b_ref, pat),
                           gq_ref[...]).astype(BF16)

    whole = pl.BlockSpec((t_len, BW), lambda a: (0, 0))
    return pl.pallas_call(
        body, grid=(rows,),
        in_specs=[_grp(GRID_W, 5), whole, whole, pl.BlockSpec(bias_tab.shape, lambda a: (0, 0, 0)), _VEC],
        out_specs=pl.BlockSpec((GRID_W, BW), lambda a: (a, 0)), out_shape=SDS((t_len, BW), BF16),
        name=f"na_{tag}", compiler_params=_cp(VMEM_MM))(proj, kn, vb, bias_tab, gq)


def _na_bwd(tag, proj, kn, vb, bias_tab, gq, dna):
    t_len = proj.shape[0]
    rows = t_len // GRID_W
    win = NA_ROWS * GRID_W

    def body(q_ref, k_ref, v_ref, b_ref, gq_ref, do_ref, dq_ref, dk_ref, dv_ref, db_ref, dgq_ref):
        r0, pat = _na_window(pl.program_id(0), rows)
        start = pl.multiple_of(r0 * GRID_W, GRID_W)
        _zero_first([dk_ref, dv_ref, db_ref, dgq_ref])
        _, vjp = jax.vjp(_na_f, q_ref[...], k_ref[pl.ds(start, win), :].astype(F32), v_ref[pl.ds(start, win), :].astype(F32),
                         _na_bias_pieces(b_ref, pat), gq_ref[...])
        dq, dk, dv, db, dgq = vjp(do_ref[...])
        dq_ref[...] = dq.astype(BF16)
        dk_ref[pl.ds(start, win), :] += dk
        dv_ref[pl.ds(start, win), :] += dv
        for h in range(NH):
            for k in range(NA_ROWS // 2):
                db_ref[h * NA_PAIRS + 2 * k - pat + NA_ROWS - 1] += db[h][k]
        dgq_ref[...] += dgq

    whole = pl.BlockSpec((t_len, BW), lambda a: (0, 0))
    tabspec = pl.BlockSpec(bias_tab.shape, lambda a: (0, 0, 0))
    row = pl.BlockSpec((GRID_W, BW), lambda a: (a, 0))
    return pl.pallas_call(
        body, grid=(rows,), in_specs=[_grp(GRID_W, 5), whole, whole, tabspec, _VEC, row],
        out_specs=[row, whole, whole, tabspec, _VEC],
        out_shape=[SDS((t_len, BW), BF16), SDS((t_len, BW), F32), SDS((t_len, BW), F32), SDS(bias_tab.shape, F32), SDS((1, BW), F32)],
        name=f"na_bwd_{tag}", compiler_params=_cp(VMEM_BIG))(proj, kn, vb, bias_tab, gq, dna)


def _mem_fwd(tag, proj, memkv, gq, gk):
    t_len = proj.shape[0]
    n_mem = memkv.shape[0]
    tm = _tile(t_len, (256,))

    def body(q_ref, mk_ref, mv_ref, gq_ref, gk_ref, o_ref):
        o_ref[...] = _mem_f(q_ref[...], mk_ref[...], mv_ref[...], gq_ref[...], gk_ref[...]).astype(BF16)

    mspec = lambda j: pl.BlockSpec((n_mem, BW), lambda a, j=j: (0, j))
    return pl.pallas_call(
        body, grid=(t_len // tm,), in_specs=[_grp(tm, 8), mspec(0), mspec(1), _VEC, _VEC],
        out_specs=pl.BlockSpec((tm, BW), lambda a: (a, 0)), out_shape=SDS((t_len, BW), BF16), name=f"mem_{tag}")(proj, memkv, memkv, gq, gk)


def _mem_bwd(tag, proj, memkv, gq, gk, dmo):
    t_len = proj.shape[0]
    n_mem = memkv.shape[0]
    tm = _tile(t_len, (256,))

    def body(q_ref, mk_ref, mv_ref, gq_ref, gk_ref, do_ref, dq_ref, dmk_ref, dmv_ref, dgq_ref, dgk_ref):
        _zero_first([dmk_ref, dmv_ref, dgq_ref, dgk_ref])
        _, vjp = jax.vjp(_mem_f, q_ref[...], mk_ref[...], mv_ref[...], gq_ref[...], gk_ref[...])
        dq, dmk, dmv, dgq, dgk = vjp(do_ref[...])
        dq_ref[...] = dq.astype(BF16)
        dmk_ref[...] += dmk
        dmv_ref[...] += dmv
        dgq_ref[...] += dgq
        dgk_ref[...] += dgk

    mspec = lambda j: pl.BlockSpec((n_mem, BW), lambda a, j=j: (0, j))
    mout = pl.BlockSpec((n_mem, BW), lambda a: (0, 0))
    row = pl.BlockSpec((tm, BW), lambda a: (a, 0))
    dq, dmk, dmv, dgq, dgk = pl.pallas_call(
        body, grid=(t_len // tm,), in_specs=[_grp(tm, 8), mspec(0), mspec(1), _VEC, _VEC, row],
        out_specs=[row, mout, mout, _VEC, _VEC],
        out_shape=[SDS((t_len, BW), BF16), SDS((n_mem, BW), F32), SDS((n_mem, BW), F32), SDS((1, BW), F32), SDS((1, BW), F32)],
        name=f"mem_bwd_{tag}")(proj, memkv, memkv, gq, gk, dmo)
    return dq, jnp.concatenate([dmk, dmv], axis=1), dgq, dgk


NA_NJ = 2 * NA_COLS - 1


def _na_onehot():
    q = np.arange(GRID_W)[:, None]
    kc = np.arange(GRID_W)[None, :]
    qwin = np.clip(q - NA_COLS // 2, 0, GRID_W - NA_COLS)
    mask = (kc >= qwin) & (kc < qwin + NA_COLS)
    col = np.clip(kc - q, -(NA_COLS - 1), NA_COLS - 1) + NA_COLS - 1
    onehot = np.zeros((LANES, GRID_W, 2, GRID_W), np.float32)
    qq, kk = np.nonzero(mask)
    for half in range(2):
        onehot[half * NA_NJ + col[qq, kk], qq, half, kk] = 1.0
    valid = np.broadcast_to(mask[:, None, :], (GRID_W, 2, GRID_W)).astype(np.float32)
    return onehot.reshape(LANES, -1), valid.reshape(1, -1)


def _na_pair_rows(rpb):
    n_layers = rpb.shape[0]
    pair = jnp.concatenate([rpb[:, :, :-1], rpb[:, :, 1:]], axis=-1).reshape(n_layers, NH * NA_PAIRS, 2 * NA_NJ)
    return jnp.pad(pair, ((0, 0), (0, 0), (0, LANES - 2 * NA_NJ)))


def _na_bias_table(rpb):
    n_layers = rpb.shape[0]
    onehot, valid = _na_onehot()
    width = onehot.shape[1]

    def body(r_ref, oh_ref, ok_ref, o_ref):
        t = jnp.dot(r_ref[...], oh_ref[...], precision=lax.Precision.HIGHEST, preferred_element_type=F32)
        o_ref[...] = jnp.where(ok_ref[...] > 0, t, NEG_INF)

    out = pl.pallas_call(
        body, grid=(n_layers,),
        in_specs=[pl.BlockSpec((None, NH * NA_PAIRS, LANES), lambda l: (l, 0, 0)), pl.BlockSpec((LANES, width), lambda l: (0, 0)),
                  pl.BlockSpec((1, width), lambda l: (0, 0))],
        out_specs=pl.BlockSpec((None, NH * NA_PAIRS, width), lambda l: (l, 0, 0)),
        out_shape=SDS((n_layers, NH * NA_PAIRS, width), F32), name="na_bias_table")(
            _na_pair_rows(rpb), jnp.asarray(onehot), jnp.asarray(valid))
    return out.reshape(n_layers, NH * NA_PAIRS, GRID_W, 2 * GRID_W)


def _na_bias_grad(dtab):
    n_layers = dtab.shape[0]
    onehot, _ = _na_onehot()
    width = onehot.shape[1]

    def body(x_ref, oh_ref, o_ref):
        o_ref[...] = jnp.dot(x_ref[...], oh_ref[...], precision=lax.Precision.HIGHEST, preferred_element_type=F32)

    out = pl.pallas_call(
        body, grid=(n_layers,),
        in_specs=[pl.BlockSpec((None, NH * NA_PAIRS, width), lambda l: (l, 0, 0)), pl.BlockSpec((width, LANES), lambda l: (0, 0))],
        out_specs=pl.BlockSpec((None, NH * NA_PAIRS, LANES), lambda l: (l, 0, 0)),
        out_shape=SDS((n_layers, NH * NA_PAIRS, LANES), F32), name="na_bias_grad")(
            dtab.reshape(n_layers, NH * NA_PAIRS, width), jnp.asarray(onehot.T.copy()))
    out = out.reshape(n_layers, NH, NA_PAIRS, LANES)
    zero = jnp.zeros((n_layers, NH, 1, NA_NJ), F32)
    return (jnp.concatenate([out[..., :NA_NJ], zero], axis=2) + jnp.concatenate([zero, out[..., NA_NJ:2 * NA_NJ]], axis=2))


def _place():
    x, y, c = lax.axis_index("x"), lax.axis_index("y"), lax.axis_index("c")
    chips = [(1 - x, y), (x, 1 - y), (1 - x, 1 - y)]
    return x, y, c, chips


def _remote(src, dst, ssem, rsem, dev):
    return pltpu.make_async_remote_copy(src_ref=src, dst_ref=dst, send_sem=ssem, recv_sem=rsem, device_id=dev, device_id_type=MESH)


HBM = pl.BlockSpec(memory_space=pltpu.HBM)
SEM = pl.BlockSpec(memory_space=pltpu.SEMAPHORE)
DATAFLOW = pltpu.SideEffectType.DATAFLOW_SIDE_EFFECTING


def _split_copies(name, srcs, lands, n_copies, plan):
    n_s, n_l = len(srcs), len(lands)
    hbm = lambda a: pltpu.HBM(a.shape, a.dtype)

    def start_body(*refs):
        ins = refs[:n_s + n_l]
        outs = refs[n_s + n_l:]
        sems, token = outs[:2 * n_copies], outs[-1]
        for k, (src, dst, dev) in enumerate(plan(ins[:n_s], ins[n_s:])):
            _remote(src, dst, sems[k], sems[n_copies + k], dev).start()
        token[...] = jnp.zeros_like(token)

    outs = pl.pallas_call(
        start_body, name=f"{name}_start",
        out_shape=(pltpu.SemaphoreType.DMA(()),) * (2 * n_copies) + tuple(hbm(a) for a in (*srcs, *lands)) + (SDS((8, LANES), F32),),
        in_specs=(HBM,) * (n_s + n_l), out_specs=(SEM,) * (2 * n_copies) + (HBM,) * (n_s + n_l) + (pl.BlockSpec(memory_space=pltpu.VMEM),),
        input_output_aliases={i: 2 * n_copies + i for i in range(n_s + n_l)},
        compiler_params=pltpu.CompilerParams(has_side_effects=DATAFLOW),
    )(*[pltpu.with_memory_space_constraint(a, pltpu.HBM) for a in (*srcs, *lands)])
    sems, thru, token = outs[:2 * n_copies], outs[2 * n_copies:-1], outs[-1]

    def wait(after_wait):
        def wait_body(*refs):
            ins = refs[:n_s + n_l]
            sem_refs = refs[n_s + n_l:n_s + n_l + 2 * n_copies]
            for k, (src, dst, dev) in enumerate(plan(ins[:n_s], ins[n_s:])):
                cp = _remote(src, dst, sem_refs[k], sem_refs[n_copies + k], dev)
                cp.wait_send()
                cp.wait_recv()

        res = pl.pallas_call(
            wait_body, name=f"{name}_wait", out_shape=tuple(hbm(a) for a in (*srcs, *lands)),
            in_specs=(HBM,) * (n_s + n_l) + (SEM,) * (2 * n_copies) + (ANY,), out_specs=(HBM,) * (n_s + n_l),
            input_output_aliases={i: i for i in range(n_s + n_l)},
            compiler_params=pltpu.CompilerParams(has_side_effects=DATAFLOW),
        )(*thru, *sems, after_wait)
        return list(res[:n_s]), list(res[n_s:])

    return token, wait


def _gather_plan(srcs, lands):
    x, y, c, chips = _place()
    me = 2 * x + y
    out = []
    for src, land in zip(srcs, lands):
        h = src.shape[0] // 2
        out.append((src, land.at[me], (x, y, 1 - c)))
        for px, py in chips:
            out.append((src.at[pl.ds(c * h, h)], land.at[me, pl.ds(c * h, h)], (px, py, c)))
    return out


def _scatter_plan(srcs, lands):
    x, y, c, chips = _place()
    out = []
    for src, land in zip(srcs, lands):
        for r, (px, py) in enumerate(chips):
            out.append((src.at[2 * px + py], land.at[r], (px, py, c)))
    return out


def _pass_plan(srcs, lands):
    x, y, c, chips = _place()
    out = []
    for land in lands:
        h = land.shape[1] // 2
        for px, py in chips:
            piece = land.at[2 * px + py, pl.ds(c * h, h)]
            out.append((piece, piece, (x, y, 1 - c)))
    return out


def _pair_plan(srcs, lands):
    x, y, c, _ = _place()
    out = []
    for src, land in zip(srcs, lands):
        h = land.shape[1]
        out.append((src.at[:, pl.ds((1 - c) * h, h)], land, (x, y, 1 - c)))
    return out


def _half_plan(srcs, lands):
    x, y, c, _ = _place()
    out = []
    for land in lands:
        h = land.shape[0] // 2
        mine = land.at[pl.ds(c * h, h)]
        out.append((mine, mine, (x, y, 1 - c)))
    return out


def _forward_halves(name, lands):
    n = len(lands)

    def body(*refs):
        ins, outs = refs[:n], refs[n:2 * n]
        ssem, rsem = refs[2 * n:]
        x, y, c, chips = _place()
        sib = (x, y, 1 - c)

        def copy(i, r, half):
            h = ins[i].shape[1] // 2
            px, py = chips[r]
            rows = pl.ds(half * h, h)
            return _remote(ins[i].at[2 * px + py, rows], outs[i].at[2 * px + py, rows], ssem.at[i, r], rsem.at[i, r], sib)

        for i in range(n):
            for r in range(3):
                copy(i, r, c).start()
        for i in range(n):
            for r in range(3):
                copy(i, r, 1 - c).wait_recv()
                copy(i, r, c).wait_send()

    return pl.pallas_call(
        body, in_specs=[ANY] * n, out_specs=[ANY] * n, out_shape=[SDS(a.shape, a.dtype) for a in lands],
        input_output_aliases={i: i for i in range(n)},
        scratch_shapes=[pltpu.SemaphoreType.DMA((n, 3)), pltpu.SemaphoreType.DMA((n, 3))], name=name)(*lands)


def _pair_exchange(name, gs):
    n = len(gs)

    def body(*refs):
        ins, outs = refs[:n], refs[n:2 * n]
        ssem, rsem = refs[2 * n:]
        x, y, c, _ = _place()
        cps = []
        for i in range(n):
            h = ins[i].shape[1] // 2
            cps.append(_remote(ins[i].at[:, pl.ds((1 - c) * h, h)], outs[i], ssem.at[i], rsem.at[i], (x, y, 1 - c)))
            cps[-1].start()
        for cp in cps:
            cp.wait()

    return pl.pallas_call(
        body, in_specs=[ANY] * n, out_specs=[ANY] * n,
        out_shape=[SDS((g.shape[0], g.shape[1] // 2, g.shape[2]), g.dtype) for g in gs],
        scratch_shapes=[pltpu.SemaphoreType.DMA((n,)), pltpu.SemaphoreType.DMA((n,))], name=name)(*gs)


def _half_exchange(name, fs):
    n = len(fs)

    def body(*refs):
        ins, outs = refs[:n], refs[n:2 * n]
        ssem, rsem = refs[2 * n:]
        x, y, c, _ = _place()

        def copy(i, half):
            h = ins[i].shape[0] // 2
            return _remote(ins[i].at[pl.ds(half * h, h)], outs[i].at[pl.ds(half * h, h)], ssem.at[i], rsem.at[i], (x, y, 1 - c))

        for i in range(n):
            copy(i, c).start()
        for i in range(n):
            copy(i, 1 - c).wait_recv()
            copy(i, c).wait_send()

    return pl.pallas_call(
        body, in_specs=[ANY] * n, out_specs=[ANY] * n, out_shape=[SDS(f.shape, f.dtype) for f in fs],
        input_output_aliases={i: i for i in range(n)},
        scratch_shapes=[pltpu.SemaphoreType.DMA((n,)), pltpu.SemaphoreType.DMA((n,))], name=name)(*fs)


def _gather_small(v):
    def body(v_ref, o_ref, lsem, ssem, rsem):
        x, y, c, _ = _place()
        me = 4 * x + 2 * y + c
        flips = [(fx, fy, fc) for fx in (0, 1) for fy in (0, 1) for fc in (0, 1)][1:]
        peer = lambda f: (1 - x if f[0] else x, 1 - y if f[1] else y, 1 - c if f[2] else c)
        loc = pltpu.make_async_copy(v_ref, o_ref.at[me], lsem)
        loc.start()
        cps = [_remote(v_ref, o_ref.at[me], ssem.at[k], rsem.at[k], peer(f)) for k, f in enumerate(flips)]
        for cp in cps:
            cp.start()
        for k, f in enumerate(flips):
            px, py, pc = peer(f)
            land = o_ref.at[4 * px + 2 * py + pc]
            _remote(land, land, ssem.at[k], rsem.at[k], peer(f)).wait_recv()
        for cp in cps:
            cp.wait_send()
        loc.wait()

    return pl.pallas_call(
        body, in_specs=[ANY], out_specs=ANY, out_shape=SDS((N_DEV,) + v.shape, v.dtype),
        scratch_shapes=[pltpu.SemaphoreType.DMA, pltpu.SemaphoreType.DMA((N_DEV - 1,)), pltpu.SemaphoreType.DMA((N_DEV - 1,))],
        name="gather_small")(v)


def _rows_tile(r):
    return _tile(r, (256, 352, 128, 64))


def _pair_sum(name, g, r1, core, me):
    n_l, s_n, h, c = r1.shape
    tr = _rows_tile(h)
    nb = h // tr

    def body(idx_ref, g_ref, r_ref, pb_ref, own_ref):
        p = g_ref[...] + r_ref[...]
        pb_ref[...] = p.astype(BF16)

        @pl.when(pl.program_id(2) == idx_ref[1])
        def _():
            own_ref[...] = p

    blk = (None, None, tr, c)
    grid_spec = pltpu.PrefetchScalarGridSpec(
        num_scalar_prefetch=1, grid=(n_l, nb, s_n),
        in_specs=[pl.BlockSpec(blk, lambda a, i, s, idx: (a, s, idx[0] * nb + i, 0)), pl.BlockSpec(blk, lambda a, i, s, idx: (a, s, i, 0))],
        out_specs=[pl.BlockSpec(blk, lambda a, i, s, idx: (a, s, i, 0)), pl.BlockSpec((None, tr, c), lambda a, i, s, idx: (a, i, 0))])
    idx = jnp.stack([core, me]).astype(jnp.int32)
    return pl.pallas_call(body, grid_spec=grid_spec, out_shape=[SDS(r1.shape, BF16), SDS((n_l, h, c), F32)], name=name)(idx, g, r1)


def _chip_sum(name, own, r2, core):
    n_l, h, c = own.shape
    tr = _rows_tile(h)
    nb = h // tr

    def body(idx_ref, o_ref, r_ref, f_ref):
        f_ref[...] = ((o_ref[...] + r_ref[0].astype(F32)) + r_ref[1].astype(F32)) + r_ref[2].astype(F32)

    grid_spec = pltpu.PrefetchScalarGridSpec(
        num_scalar_prefetch=1, grid=(n_l, nb),
        in_specs=[pl.BlockSpec((None, tr, c), lambda a, i, idx: (a, i, 0)), pl.BlockSpec((3, None, tr, c), lambda a, i, idx: (0, a, i, 0))],
        out_specs=pl.BlockSpec((None, tr, c), lambda a, i, idx: (a, idx[0] * nb + i, 0)))
    return pl.pallas_call(body, grid_spec=grid_spec, out_shape=SDS((n_l, 2 * h, c), F32), name=name)(
        jnp.reshape(core, (1,)).astype(jnp.int32), own, r2)


def _adamw_math(w, g, m, v):
    m = ADAM_B1 * m + (1.0 - ADAM_B1) * g
    v = ADAM_B2 * v + (1.0 - ADAM_B2) * jnp.square(g)
    m_hat = m / (1.0 - ADAM_B1 ** ADAM_STEP)
    v_hat = v / (1.0 - ADAM_B2 ** ADAM_STEP)
    delta = -ADAM_LR * (m_hat / (jnp.sqrt(v_hat) + ADAM_EPS) + ADAM_WD * w)
    return delta, m, v


def _adamw(name, l, w, g, m, v, bufs):
    _, r, c = w.shape
    tr = _rows_tile(r)

    def body(w_ref, g_ref, m_ref, v_ref, *rest):
        go_ref, d_ref, nm_ref, nv_ref = rest[4:]
        g = g_ref[...]
        go_ref[...] = g
        d_ref[...], nm_ref[...], nv_ref[...] = _adamw_math(w_ref[...], g, m_ref[...], v_ref[...])

    blk = pl.BlockSpec((None, tr, c), lambda i: (l, i, 0))
    return pl.pallas_call(body, grid=(r // tr,), in_specs=[blk, pl.BlockSpec((tr, c), lambda i: (i, 0)), blk, blk] + [ANY] * 4,
                          out_specs=[blk] * 4, out_shape=[SDS(w.shape, F32)] * 4, input_output_aliases={4 + k: k for k in range(4)},
                          name=name)(w, g, m, v, *bufs)


def _small_update(parts, w, m, v):
    def body(p_ref, w_ref, m_ref, v_ref, g_ref, d_ref, nm_ref, nv_ref):
        g = p_ref[0]
        for d in range(1, N_DEV):
            g = g + p_ref[d]
        g_ref[...] = g
        d_ref[...], nm_ref[...], nv_ref[...] = _adamw_math(w_ref[...], g, m_ref[...], v_ref[...])

    return pl.pallas_call(body, out_shape=[SDS(w.shape, F32)] * 4, name="small_update")(parts, w, m, v)


SMALL = ["norm_mix_g", "norm_mem_g", "ret_decay_fwd", "ret_decay_bwd", "ret_norm_g", "pool_w", "pool_scale", "na_q_norm_g",
         "na_k_norm_g", "na_rpb", "mem_q_norm_g", "mem_k_norm_g", "norm_ffn_g"]
BIG = ["w_in", "w_gate", "w_mem_kv", "w_branch", "w_out", "w_ffn_in", "w_ffn_out"]
GROUPS = (("w_in", "w_gate", "w_mem_kv", "w_branch", "w_out"), ("w_ffn_in", "w_ffn_out"))
REDUCE_GROUPS = (("w_ffn_in", "w_ffn_out"), ("w_gate", "w_branch", "w_out"), ("w_in", "w_mem_kv"))
ORDER = ["norm_mix_g", "norm_mem_g", "w_in", "w_gate", "ret_decay_fwd", "ret_decay_bwd", "ret_norm_g", "pool_w", "pool_scale",
         "na_q_norm_g", "na_k_norm_g", "na_rpb", "mem_q_norm_g", "mem_k_norm_g", "w_mem_kv", "w_branch", "w_out", "norm_ffn_g",
         "w_ffn_in", "w_ffn_out"]


def _pack(arrs):
    rows = []
    for a in arrs:
        f = a.reshape(-1).astype(F32)
        pad = (-f.shape[0]) % (8 * LANES)
        rows.append(jnp.pad(f, (0, pad)).reshape(-1, LANES))
    return jnp.concatenate(rows, axis=0)


def _unpack(p, like):
    out, at = [], 0
    for a in like:
        n = int(np.prod(a.shape))
        rows = -(-n // (8 * LANES)) * 8
        out.append(p[at:at + rows].reshape(-1)[:n].reshape(a.shape))
        at += rows
    return out


def _rope_tables(t_len):
    half = HEAD // 2
    inv = ROPE_THETA ** (-jnp.arange(half, dtype=F32) / half)
    ang = jnp.arange(t_len, dtype=F32)[:, None] * inv[None, :]
    cos, sin = jnp.cos(ang), jnp.sin(ang)
    return jnp.tile(jnp.concatenate([cos, cos], axis=1), (1, NH)), jnp.tile(jnp.concatenate([-sin, sin], axis=1), (1, NH))


def _block_diag(pw):
    out = jnp.zeros((BW, BW), pw.dtype)
    for g in range(NH):
        out = out.at[g * HEAD:(g + 1) * HEAD, g * HEAD:(g + 1) * HEAD].set(pw[g])
    return out


def _fold_heads(v):
    return v.reshape(NH, HEAD).sum(axis=0)


def _fold_lanes(v):
    return v.reshape(NH, HEAD).sum(axis=1)


def kernel(x, mem, norm_mix_g, norm_mem_g, w_in, w_gate, ret_decay_fwd, ret_decay_bwd, ret_norm_g, pool_w, pool_scale, na_q_norm_g, na_k_norm_g, na_rpb, mem_q_norm_g, mem_k_norm_g, w_mem_kv, w_branch, w_out, norm_ffn_g, w_ffn_in, w_ffn_out, loss_target, m_norm_mix_g, m_norm_mem_g, m_w_in, m_w_gate, m_ret_decay_fwd, m_ret_decay_bwd, m_ret_norm_g, m_pool_w, m_pool_scale, m_na_q_norm_g, m_na_k_norm_g, m_na_rpb, m_mem_q_norm_g, m_mem_k_norm_g, m_w_mem_kv, m_w_branch, m_w_out, m_norm_ffn_g, m_w_ffn_in, m_w_ffn_out, v_norm_mix_g, v_norm_mem_g, v_w_in, v_w_gate, v_ret_decay_fwd, v_ret_decay_bwd, v_ret_norm_g, v_pool_w, v_pool_scale, v_na_q_norm_g, v_na_k_norm_g, v_na_rpb, v_mem_q_norm_g, v_mem_k_norm_g, v_w_mem_kv, v_w_branch, v_w_out, v_norm_ffn_g, v_w_ffn_in, v_w_ffn_out):
    args = dict(locals())
    W = {n: args[n] for n in ORDER}
    M = {n: args["m_" + n] for n in ORDER}
    V = {n: args["v_" + n] for n in ORDER}
    n_layers, d_model = norm_mix_g.shape
    assert x.shape[0] == 1
    t_len = x.shape[1]
    xc, yc, cc = lax.axis_index("x"), lax.axis_index("y"), lax.axis_index("c")
    me_chip = 2 * xc + yc

    def as3(a):
        return a.reshape(a.shape[0], -1, a.shape[-1])

    gather_waits, token = {}, jnp.zeros((), F32)
    for l in range(n_layers):
        for gi, names in enumerate(GROUPS):
            srcs = [as3(W[n])[l].astype(BF16) for n in names]
            lands = [lax.empty((N_CHIPS,) + a.shape, BF16) for a in srcs]
            tok, gather_waits[l, gi] = _split_copies(f"gather_l{l}_g{gi}", srcs, lands, 4 * len(names), _gather_plan)
            token = token + tok[0, 0]

    cos, sin = _rope_tables(t_len)
    btabs = _na_bias_table(na_rpb)
    lane = lambda a: a.reshape(1, -1).astype(F32)
    x2d, mem2d, tgt = x[0], mem[0], loss_target[0]
    norm_mix_g = norm_mix_g + token

    def fetch(l, gi, after):
        lands = gather_waits[l, gi](after)[1]
        return _split_copies(f"pass_l{l}_g{gi}", [], lands, 3 * len(GROUPS[gi]), _pass_plan)

    saved = []
    _, passing = fetch(0, 0, x2d)
    for l in range(n_layers):
        tag = f"l{l}"
        gathered = dict(zip(GROUPS[0], passing(x2d)[1]))
        wi = gathered["w_in"].transpose(1, 0, 2).reshape(1, 1, d_model, -1)
        wg = gathered["w_gate"][None]
        wmkv = gathered["w_mem_kv"].reshape(1, 1, d_model, 2 * BW)
        wb = gathered["w_branch"].reshape(1, N_CHIPS, NH, BW, BW)
        wo = gathered["w_out"].reshape(1, 1, d_model, d_model)
        s = {"x": x2d}
        s["decf"], s["decb"] = lane(jnp.repeat(ret_decay_fwd[l], HEAD)), lane(jnp.repeat(ret_decay_bwd[l], HEAD))
        s["gn"], s["pscale"] = lane(ret_norm_g[l]), lane(pool_scale[l])
        s["wbd"] = _block_diag(pool_w[l])
        s["nagq"], s["nagk"] = lane(jnp.tile(na_q_norm_g[l], NH)), lane(jnp.tile(na_k_norm_g[l], NH))
        s["mgq"], s["mgk"] = lane(jnp.tile(mem_q_norm_g[l], NH)), lane(jnp.tile(mem_k_norm_g[l], NH))
        s["btab"] = btabs[l]
        s["h"] = _rms_fwd(f"rms_mix_{tag}", x2d, lane(norm_mix_g[l]))
        s["proj"] = _mm_nn(f"mm_in_{tag}", s["h"], wi, 0)
        s["gp3"] = _mm_nn(f"mm_gate_{tag}", s["h"], wg, 0, out3d=True)
        tok, passing = fetch(l, 1, s["gp3"])
        s["hm"] = _rms_fwd(f"rms_mem_{tag}", mem2d, lane(norm_mem_g[l]) + tok[0, 0])
        s["memkv"] = _mm_nn(f"mm_memkv_{tag}", s["hm"], wmkv, 0)
        ret, s["ret_saved"] = _ret_fwd(tag, s["proj"], cos, sin, s["decf"], s["decb"], s["gn"])
        s["pvp"] = jnp.pad(s["proj"][:, 4 * BW:5 * BW], ((POOL_PAD, POOL_PAD), (0, 0)))
        pool = _pool_fwd(tag, s["pvp"], s["wbd"], s["pscale"], t_len)[POOL_PAD:POOL_PAD + t_len]
        s["kn"], s["vb"] = _na_keys(tag, s["proj"], s["nagk"])
        na = _na_fwd(tag, s["proj"], s["kn"], s["vb"], s["btab"], s["nagq"])
        mo = _mem_fwd(tag, s["proj"], s["memkv"], s["mgq"], s["mgk"])
        s["br"] = jnp.stack([ret, pool, na, mo])
        s["merged"] = _gate_fwd(f"gate_{tag}", s["gp3"], s["br"], wb, 0)
        s["x2"] = _mm_nn(f"mm_out_{tag}", s["merged"], wo, 0, add=x2d)
        gathered = dict(zip(GROUPS[1], passing(s["x2"])[1]))
        pin_ffn = jnp.zeros((), F32)
        if l + 1 < n_layers:
            tok, passing = fetch(l + 1, 0, s["x2"])
            pin_ffn = tok[0, 0]
        wfi = gathered["w_ffn_in"][None]
        wfo = gathered["w_ffn_out"].reshape(1, 1, -1, d_model)
        s["w"] = (wi, wg, wmkv, wb, wo, wfi, wfo)
        s["h2"] = _rms_fwd(f"rms_ffn_{tag}", s["x2"], lane(norm_ffn_g[l]) + pin_ffn)
        s["ag3"] = _mm_nn(f"mm_ffn_in_{tag}", s["h2"], wfi, 0, out3d=True)
        s["u"] = _swiglu_fwd(f"swiglu_{tag}", s["ag3"])
        x2d = _mm_nn(f"mm_ffn_out_{tag}", s["u"], wfo, 0, add=s["x2"])
        saved.append(s)

    loss_part, dx = _loss_kernel("loss", x2d, tgt)

    dbtabs = [None] * n_layers
    gsmall = {n: [None] * n_layers for n in SMALL}
    opt_view = {n: (lambda a, n=n: jnp.swapaxes(as3(a), 1, 2) if n == "w_in" else as3(a)) for n in BIG}
    out_bufs = {n: tuple(lax.empty(opt_view[n](W[n]).shape, F32) for _ in range(4)) for n in BIG}

    def reduce_group(l, gi, grads):
        names = REDUCE_GROUPS[gi]
        gs = [grads[n] for n in names]
        lands = [lax.empty((N_CHIPS, g.shape[1] // 2, g.shape[2]), F32) for g in gs]
        tok0, wait0 = _split_copies(f"pairx_l{l}_g{gi}", gs, lands, len(names), _pair_plan)

        def sums_and_scatter(after):
            mine, theirs = wait0(after)
            sums = [_pair_sum(f"pair_sum_{n}_l{l}", g[None], r[None], cc, me_chip) for n, g, r in zip(names, mine, theirs)]
            pbs = [p[0] for p, _ in sums]
            lands1 = [lax.empty((3,) + p.shape[1:], BF16) for p in pbs]
            tok1, wait1 = _split_copies(f"scatter_l{l}_g{gi}", pbs, lands1, 3 * len(names), _scatter_plan)

            def chip_sums_and_halves(after):
                halves = [_chip_sum(f"chip_sum_{n}_l{l}", own, r[:, None], cc)[0] for n, (_, own), r in zip(names, sums, wait1(after)[1])]
                tok2, wait2 = _split_copies(f"halfx_l{l}_g{gi}", [], halves, len(names), _half_plan)

                def update(after):
                    for n, f in zip(names, wait2(after)[1]):
                        out_bufs[n] = _adamw(f"adamw_{n}_l{l}", l, opt_view[n](W[n]), f, opt_view[n](M[n]), opt_view[n](V[n]), out_bufs[n])
                    return None, None

                return update, tok2

            return chip_sums_and_halves, tok1

        return sums_and_scatter, tok0

    def advance(stages, new):
        nxt, tok = new
        out, toks = [nxt], tok
        for stage in stages:
            nxt, tok = stage(toks)
            if nxt is not None:
                out.append(nxt)
                toks = toks + tok
        return out, toks

    stages, order_after = [], None
    for l in reversed(range(n_layers)):
        tag = f"l{l}"
        s = saved[l]
        wi, wg, wmkv, wb, wo, wfi, wfo = s["w"]
        du = _mm_nt(f"mm_ffn_out_dx_{tag}", dx, wfo, 0, after=order_after)
        g_wfo = _mm_tn(f"mm_ffn_out_dw_{tag}", s["u"], dx, 1)
        dag3 = _swiglu_bwd(f"swiglu_bwd_{tag}", s["ag3"], du)
        dh2 = _mm_nt(f"mm_ffn_in_dx_{tag}", dag3, wfi, 0)
        g_wfi = _mm_tn(f"mm_ffn_in_dw_{tag}", s["h2"], dag3, N_CHIPS)
        dx2, dg = _rms_bwd(f"rms_ffn_bwd_{tag}", s["x2"], lane(norm_ffn_g[l]), [dh2], add=dx)
        gsmall["norm_ffn_g"][l] = dg[0]
        stages, tok = advance(stages, reduce_group(l, 0, {"w_ffn_in": g_wfi, "w_ffn_out": g_wfo.reshape(N_CHIPS, -1, d_model)}))
        dmerged = _mm_nt(f"mm_out_dx_{tag}", dx2, wo, 0, after=tok)
        g_wo = _mm_tn(f"mm_out_dw_{tag}", s["merged"], dx2, 1)
        dgp3, dbr, g_wb = _gate_bwd(f"gate_bwd_{tag}", s["gp3"], s["br"], wb, 0, dmerged)
        g_wg = _mm_tn(f"mm_gate_dw_{tag}", s["h"], dgp3, N_CHIPS)
        stages, tok = advance(stages, reduce_group(l, 1, {
            "w_gate": g_wg, "w_branch": g_wb.reshape(N_CHIPS, NH * BW, BW), "w_out": g_wo.reshape(N_CHIPS, -1, d_model)}))
        dh_a = _mm_nt(f"mm_gate_dx_{tag}", dgp3, wg, 0, after=tok)
        drq, drk, drv, drg, ddf, ddb, dgn = _ret_bwd(tag, s["proj"], cos, sin, s["decf"], s["decb"], s["gn"], s["ret_saved"], dbr[0])
        gsmall["ret_decay_fwd"][l], gsmall["ret_decay_bwd"][l], gsmall["ret_norm_g"][l] = _fold_lanes(ddf), _fold_lanes(ddb), dgn[0]
        dpool_p = jnp.pad(dbr[1], ((POOL_PAD, POOL_PAD), (0, 0)))
        dpvp, dwbd, dps = _pool_bwd(tag, s["pvp"], s["wbd"], s["pscale"], t_len, dpool_p)
        dpv = dpvp[POOL_PAD:POOL_PAD + t_len]
        gsmall["pool_w"][l] = jnp.stack([dwbd[g * HEAD:(g + 1) * HEAD, g * HEAD:(g + 1) * HEAD] for g in range(NH)])
        gsmall["pool_scale"][l] = dps[0]
        dnq, dkn, dnv, dbtabs[l], dgq = _na_bwd(tag, s["proj"], s["kn"], s["vb"], s["btab"], s["nagq"], dbr[2])
        dnk, dnv, dgk = _na_keys_bwd(tag, s["proj"], s["nagk"], dkn, dnv)
        gsmall["na_q_norm_g"][l], gsmall["na_k_norm_g"][l] = _fold_heads(dgq), _fold_heads(dgk)
        dmq, dmemkv, dgq, dgk = _mem_bwd(tag, s["proj"], s["memkv"], s["mgq"], s["mgk"], dbr[3])
        gsmall["mem_q_norm_g"][l], gsmall["mem_k_norm_g"][l] = _fold_heads(dgq), _fold_heads(dgk)
        g_wmkv = _mm_tn(f"mm_memkv_dw_{tag}", s["hm"], dmemkv, 1)
        dhm = _mm_nt(f"mm_memkv_dx_{tag}", dmemkv, wmkv, 0)
        _, dg = _rms_bwd(f"rms_mem_bwd_{tag}", mem2d, lane(norm_mem_g[l]), [dhm])
        gsmall["norm_mem_g"][l] = dg[0]
        dproj = jnp.concatenate([drq, drk, drv, drg, dpv, dnq, dnk, dnv, dmq], axis=1)
        g_wi = _mm_tn(f"mm_in_dw_{tag}", dproj, s["h"], 1)
        dh_b = _mm_nt(f"mm_in_dx_{tag}", dproj, wi, 0)
        dx, dg = _rms_bwd(f"rms_mix_bwd_{tag}", s["x"], lane(norm_mix_g[l]), [dh_a, dh_b], add=dx2)
        gsmall["norm_mix_g"][l] = dg[0]

        stages, order_after = advance(stages, reduce_group(l, 2, {
            "w_in": g_wi.reshape(N_CHIPS, -1, d_model), "w_mem_kv": g_wmkv.reshape(N_CHIPS, -1, 2 * BW)}))
    out_g, out_d, out_m, out_v = {}, {}, {}, {}
    gsmall["na_rpb"] = list(_na_bias_grad(jnp.stack(dbtabs)))
    small_g = [jnp.stack(gsmall[n]).reshape(W[n].shape) for n in SMALL] + [loss_part]
    like = [W[n] for n in SMALL] + [loss_part]
    zero = jnp.zeros((1, 1), F32)
    parts = _gather_small(_pack(small_g))
    packed = _small_update(parts, _pack([W[n] for n in SMALL] + [zero]), _pack([M[n] for n in SMALL] + [zero]),
                           _pack([V[n] for n in SMALL] + [zero]))
    sg, sd, sm, sv = [_unpack(p, like) for p in packed]
    for i, n in enumerate(SMALL):
        out_g[n], out_d[n], out_m[n], out_v[n] = sg[i], sd[i], sm[i], sv[i]
    loss = sg[-1].reshape(())

    while stages:
        stages = [nxt for nxt, _ in (stage(packed[0]) for stage in stages) if nxt is not None]

    for n in BIG:
        shp = W[n].shape
        back = (lambda b: jnp.swapaxes(b, 1, 2)) if n == "w_in" else (lambda b: b)
        out_g[n], out_d[n], out_m[n], out_v[n] = [back(b).reshape(shp) for b in out_bufs[n]]

    return (loss, dx.reshape(x.shape), *[out_g[n] for n in ORDER], *[out_d[n] for n in ORDER],
            *[out_m[n] for n in ORDER], *[out_v[n] for n in ORDER])
```

```python
import functools

import numpy as np
import jax
import jax.numpy as jnp
from jax import lax
from jax.experimental import pallas as pl
from jax.experimental.pallas import tpu as pltpu

F32 = jnp.float32
BF16 = jnp.bfloat16
SDS = jax.ShapeDtypeStruct
MESH = pl.DeviceIdType.MESH
ANY = pl.BlockSpec(memory_space=pl.ANY)

HEAD = 64
NH = 4
BW = NH * HEAD
CH = 128
GRID_W = 64
NA_ROWS = 8
NA_COLS = 16
POOL_HALF = (1, 2, 4, 8)
POOL_PAD = 16
ROPE_THETA = 10000.0
EPS = 1e-6
NEG_INF = -1e30
N_CHIPS = 4
N_DEV = 8
LANES = 128

ADAM_LR = 0.001
ADAM_B1 = 0.9
ADAM_B2 = 0.999
ADAM_EPS = 1e-08
ADAM_WD = 0.01
ADAM_STEP = 10

VMEM_BIG = 56 << 20
VMEM_MM = 44 << 20


def _cp(vmem=None):
    return pltpu.CompilerParams(vmem_limit_bytes=vmem) if vmem else None


def _d(a, b, ca, cb):
    return lax.dot_general(a.astype(BF16), b.astype(BF16), (((ca,), (cb,)), ((), ())),
                           preferred_element_type=F32)


@jax.custom_vjp
def _nn(a, b):
    return _d(a, b, 1, 0)


def _nn_f(a, b):
    return _d(a, b, 1, 0), (a, b)


def _nn_b(r, g):
    a, b = r
    return _d(g, b, 1, 1).astype(a.dtype), _d(a, g, 0, 0).astype(b.dtype)


_nn.defvjp(_nn_f, _nn_b)


@jax.custom_vjp
def _nt(a, b):
    return _d(a, b, 1, 1)


def _nt_f(a, b):
    return _d(a, b, 1, 1), (a, b)


def _nt_b(r, g):
    a, b = r
    return _d(g, b, 1, 0).astype(a.dtype), _d(g, a, 0, 0).astype(b.dtype)


_nt.defvjp(_nt_f, _nt_b)


@jax.custom_vjp
def _tn(a, b):
    return _d(a, b, 0, 0)


def _tn_f(a, b):
    return _d(a, b, 0, 0), (a, b)


def _tn_b(r, g):
    a, b = r
    return _d(b, g, 1, 1).astype(a.dtype), _d(a, g, 1, 0).astype(b.dtype)


_tn.defvjp(_tn_f, _tn_b)


@functools.partial(jax.custom_vjp, nondiff_argnums=(1,))
def _rollr(x, s):
    return pltpu.roll(x, s % x.shape[0], 0)


def _rollr_f(x, s):
    return _rollr(x, s), None


def _rollr_b(s, _, g):
    return (_rollr(g, -s),)


_rollr.defvjp(_rollr_f, _rollr_b)


@jax.custom_vjp
def _swap32(t):
    n = t.shape[1]
    lane = lax.broadcasted_iota(jnp.int32, (1, n), 1)
    return jnp.where((lane & (HEAD // 2)) == 0, pltpu.roll(t, n - HEAD // 2, 1), pltpu.roll(t, HEAD // 2, 1))


def _swap32_f(t):
    return _swap32(t), None


def _swap32_b(_, g):
    return (_swap32(g),)


_swap32.defvjp(_swap32_f, _swap32_b)


def _head_masks():
    lane = lax.broadcasted_iota(jnp.int32, (1, BW), 1)
    return [(lane >= HEAD * h) & (lane < HEAD * (h + 1)) for h in range(NH)]


def _head_rms(x, g):
    out = jnp.zeros_like(x)
    for mh in _head_masks():
        ms = jnp.sum(jnp.where(mh, x * x, 0.0), axis=-1, keepdims=True) * (1.0 / HEAD)
        out = out + jnp.where(mh, x * lax.rsqrt(ms + EPS), 0.0)
    return out * g


def _stack_heads(x):
    return jnp.concatenate([jnp.where(mh, x, 0.0) for mh in _head_masks()], axis=0)


def _unstack_heads(y):
    n = y.shape[0] // NH
    out = None
    for h, mh in enumerate(_head_masks()):
        t = jnp.where(mh, y[h * n:(h + 1) * n], 0.0)
        out = t if out is None else out + t
    return out


def _rms(x, g):
    return x * lax.rsqrt(jnp.mean(x * x, axis=-1, keepdims=True) + EPS) * g


def _rot(t, cos, sin):
    return t * cos + _swap32(t) * sin


def _softmax(s):
    e = jnp.exp(s - lax.stop_gradient(jnp.max(s, axis=-1, keepdims=True)))
    return e / jnp.sum(e, axis=-1, keepdims=True)


def _ret_kv_f(k, v, cos, sin, decf, decb):
    lgf, lgb = jax.nn.log_sigmoid(decf), jax.nn.log_sigmoid(decb)
    kr = _rot(k, cos, sin)
    idx = lax.broadcasted_iota(jnp.int32, (CH, 1), 0).astype(F32)
    r = lax.broadcasted_iota(jnp.int32, (BW, BW), 0) // HEAD
    c = lax.broadcasted_iota(jnp.int32, (BW, BW), 1) // HEAD
    bd = r == c
    kvf = jnp.where(bd, _tn(kr * jnp.exp((CH - 1 - idx) * lgf), v), 0.0)
    kvb = jnp.where(bd, _tn(kr * jnp.exp(idx * lgb), v), 0.0)
    return kvf, kvb


def _ret_scan_f(kvf, kvb, decf, decb):
    n = len(kvf)
    cdf = jnp.exp(CH * jax.nn.log_sigmoid(decf))
    cdb = jnp.exp(CH * jax.nn.log_sigmoid(decb))
    z = jnp.zeros((BW, BW), F32)
    sf, st = [z], z
    for a in range(n - 1):
        st = cdf * st + kvf[a]
        sf.append(st)
    sb, st = [z], z
    for a in range(n - 1, 0, -1):
        st = cdb * st + kvb[a]
        sb.append(st)
    return sf, sb[::-1]


def _ret_out_f(q, k, v, g, cos, sin, stf, stb, decf, decb, gn):
    lgf, lgb = jax.nn.log_sigmoid(decf), jax.nn.log_sigmoid(decb)
    qr = _rot(q, cos, sin) * (HEAD ** -0.5)
    kr = _rot(k, cos, sin)
    diff = (lax.broadcasted_iota(jnp.int32, (CH, CH), 0) - lax.broadcasted_iota(jnp.int32, (CH, CH), 1)).astype(F32)
    decay = []
    for mh in _head_masks():
        lf = jnp.sum(jnp.where(mh, lgf, 0.0), axis=1, keepdims=True) * (1.0 / HEAD)
        lb = jnp.sum(jnp.where(mh, lgb, 0.0), axis=1, keepdims=True) * (1.0 / HEAD)
        decay.append(jnp.exp(jnp.where(diff >= 0, diff * lf, -diff * lb)))
    p = _nt(_stack_heads(qr), kr) * jnp.concatenate(decay, axis=0)
    o = _unstack_heads(_nn(p, v))
    idx = lax.broadcasted_iota(jnp.int32, (CH, 1), 0).astype(F32)
    o = o + _nn(qr * jnp.exp((idx + 1) * lgf), stf) + _nn(qr * jnp.exp((CH - idx) * lgb), stb)
    return _head_rms(o, gn) * (g * jax.nn.sigmoid(g))


def _pool_f(pvp, wbd, scale, t_len):
    n = pvp.shape[0]
    w2 = pvp + _rollr(pvp, 1)
    w4 = _rollr(w2, 1) + _rollr(w2, -1)
    w8 = _rollr(w4, 2) + _rollr(w4, -2)
    w16 = _rollr(w8, 4) + _rollr(w8, -4)
    grp = lax.broadcasted_iota(jnp.int32, (1, BW), 1) // HEAD
    ws = jnp.where(grp == 0, w2, jnp.where(grp == 1, w4, jnp.where(grp == 2, w8, w16)))
    half = jnp.where(grp == 0, POOL_HALF[0], jnp.where(grp == 1, POOL_HALF[1], jnp.where(grp == 2, POOL_HALF[2], POOL_HALF[3])))
    t = lax.broadcasted_iota(jnp.int32, (n, 1), 0) - POOL_PAD
    cnt = jnp.minimum(t + half, t_len) - jnp.maximum(t - half, 0)
    cnt = jnp.where((t >= 0) & (t < t_len), cnt, 1).astype(F32)
    pooled = ws / cnt - pvp
    return _nn(pooled, wbd) * scale


def _na_f(q, kn, vw, bias, gq):
    qn = _head_rms(q, gq)
    bias_all = jnp.concatenate([jnp.concatenate(bias[h], axis=1) for h in range(NH)], axis=0)
    s = _nt(_stack_heads(qn), kn) * (HEAD ** -0.5) + bias_all
    return _unstack_heads(_nn(_softmax(s), vw))


def _mem_f(q, mk, mv, gq, gk):
    qn = _head_rms(q, gq)
    kn = _head_rms(mk, gk)
    s = _nt(_stack_heads(qn), kn) * (HEAD ** -0.5)
    return _unstack_heads(_nn(_softmax(s), mv))


def _gate_f(gp, br, wb):
    out = None
    for n in range(NH):
        t = jax.nn.sigmoid(gp[n]) * _nn(br[n], wb[n])
        out = t if out is None else out + t
    return out


def _swiglu_f(a, g):
    return a * jax.nn.sigmoid(a) * g


MM_ROWS = (1024, 512, 256)
MM_COLS = (1152, 1024, 768, 512)


def _tile(n, prefs):
    for p in prefs:
        if n % p == 0:
            return p
    return n


def _mm_call(name, a, b, extra, dn, grid, a_spec, b_spec, extra_specs, o_spec, out_shape, acc_shape, nred, after=None):
    add = len(extra) == 1
    if after is not None:
        extra, extra_specs = [*extra, after], [*extra_specs, ANY]
    n_extra = len(extra)
    n_steps = int(np.prod([grid[len(grid) - 1 - i] for i in range(nred)]))

    def body(*refs):
        a_ref, b_ref = refs[0], refs[1]
        o_ref, acc = refs[2 + n_extra], refs[-1]
        red = [len(grid) - 1 - i for i in range(nred)]
        first = functools.reduce(jnp.logical_and, [pl.program_id(ax) == 0 for ax in red])
        last = functools.reduce(jnp.logical_and, [pl.program_id(ax) == grid[ax] - 1 for ax in red])
        d = lax.dot_general(a_ref[...].astype(BF16), b_ref[...].astype(BF16), dn, preferred_element_type=F32)

        def finish(r):
            if add:
                r = r + refs[2][...]
            o_ref[...] = r.astype(o_ref.dtype)

        if n_steps == 1:
            finish(d)
        else:
            @pl.when(first)
            def _():
                acc[...] = d

            @pl.when(jnp.logical_not(first) & jnp.logical_not(last))
            def _():
                acc[...] += d

            @pl.when(last)
            def _():
                finish(acc[...] + d)

    return pl.pallas_call(
        body, grid=grid, in_specs=[a_spec, b_spec, *extra_specs], out_specs=o_spec, out_shape=out_shape,
        scratch_shapes=[pltpu.VMEM(acc_shape, F32)] if n_steps > 1 else [], name=name, compiler_params=_cp(VMEM_MM),
    )(a, b, *extra)


def _mm_nn(name, a, w, l, *, out3d=False, add=None, out_dtype=F32):
    m, k = a.shape
    _, s_n, _, ns = w.shape
    tm = _tile(m, MM_ROWS)
    tk = _tile(k, (1024, 1408, 512, 256))
    tn = ns if ns <= 1408 else _tile(ns, MM_COLS)
    grid = (m // tm, s_n, ns // tn, k // tk)
    a_spec = pl.BlockSpec((tm, tk), lambda i, s, j, kk: (i, kk))
    b_spec = pl.BlockSpec((None, None, tk, tn), lambda i, s, j, kk: (l, s, kk, j))
    if out3d:
        o_spec = pl.BlockSpec((None, tm, tn), lambda i, s, j, kk: (s, i, j))
        out_shape = SDS((s_n, m, ns), out_dtype)
    else:
        nj = ns // tn
        o_spec = pl.BlockSpec((tm, tn), lambda i, s, j, kk: (i, s * nj + j))
        out_shape = SDS((m, s_n * ns), out_dtype)
    extra, especs = [], []
    if add is not None:
        assert not out3d
        nj = ns // tn
        extra, especs = [add], [pl.BlockSpec((tm, tn), lambda i, s, j, kk: (i, s * nj + j))]
    return _mm_call(name, a, w, extra, (((1,), (0,)), ((), ())), grid, a_spec, b_spec, especs, o_spec, out_shape, (tm, tn), 1)


def _mm_nt(name, g, w, l, after=None):
    _, s_n, k, ns = w.shape
    g3d = g.ndim == 3
    m = g.shape[1] if g3d else g.shape[0]
    tm = _tile(m, MM_ROWS)
    to = _tile(k, (1024, 1408, 512, 256))
    tk = ns if (ns % LANES or ns <= 1408) else _tile(ns, MM_COLS)
    nks = ns // tk
    grid = (m // tm, k // to, s_n, nks)
    if g3d:
        a_spec = pl.BlockSpec((None, tm, tk), lambda i, j, s, r: (s, i, r))
    else:
        a_spec = pl.BlockSpec((tm, tk), lambda i, j, s, r: (i, s * nks + r))
    b_spec = pl.BlockSpec((None, None, to, tk), lambda i, j, s, r: (l, s, j, r))
    o_spec = pl.BlockSpec((tm, to), lambda i, j, s, r: (i, j))
    return _mm_call(name, g, w, [], (((1,), (1,)), ((), ())), grid, a_spec, b_spec, [], o_spec, SDS((m, k), F32), (tm, to), 2,
                    after=after)


def _mm_tn(name, a, g, s_n):
    t_len, k = a.shape
    g3d = g.ndim == 3
    ns = g.shape[2] if g3d else g.shape[1] // s_n
    tt = _tile(t_len, (2048, 1024, 512))
    to = _tile(k, (1024, 768, 1408, 512, 256, 128))
    tn = ns
    nj = ns // tn
    grid = (s_n, k // to, nj, t_len // tt)
    a_spec = pl.BlockSpec((tt, to), lambda s, i, j, t: (t, i))
    if g3d:
        b_spec = pl.BlockSpec((None, tt, tn), lambda s, i, j, t: (s, t, j))
    else:
        b_spec = pl.BlockSpec((tt, tn), lambda s, i, j, t: (t, s * nj + j))
    o_spec = pl.BlockSpec((None, to, tn), lambda s, i, j, t: (s, i, j))
    return _mm_call(name, a, g, [], (((0,), (0,)), ((), ())), grid, a_spec, b_spec, [], o_spec,
                    SDS((s_n, k, ns), F32), (to, tn), 1)


def _rms_fwd(name, x, g):
    m, d = x.shape
    tm = _tile(m, (256,))

    def body(x_ref, g_ref, o_ref):
        o_ref[...] = _rms(x_ref[...], g_ref[...]).astype(BF16)

    row = pl.BlockSpec((tm, d), lambda i: (i, 0))
    return pl.pallas_call(body, grid=(m // tm,), in_specs=[row, pl.BlockSpec((1, d), lambda i: (0, 0))], out_specs=row,
                          out_shape=SDS((m, d), BF16), name=name)(x, g)


def _rms_bwd(name, x, g, dhs, add=None):
    m, d = x.shape
    tm = _tile(m, (256,))
    nh = len(dhs)

    def body(*refs):
        x_ref, g_ref = refs[0], refs[1]
        dh = refs[2][...]
        for r in refs[3:2 + nh]:
            dh = dh + r[...]
        dx_ref, dg_ref = refs[-2], refs[-1]
        _, vjp = jax.vjp(_rms, x_ref[...], g_ref[...])
        dx, dg = vjp(dh)
        if add is not None:
            dx = dx + refs[2 + nh][...]
        dx_ref[...] = dx

        @pl.when(pl.program_id(0) == 0)
        def _():
            dg_ref[...] = jnp.zeros_like(dg_ref)

        dg_ref[...] += dg

    row = pl.BlockSpec((tm, d), lambda i: (i, 0))
    vec = pl.BlockSpec((1, d), lambda i: (0, 0))
    ins = [x, g, *dhs] + ([add] if add is not None else [])
    return pl.pallas_call(body, grid=(m // tm,), in_specs=[row, vec] + [row] * (len(ins) - 2), out_specs=[row, vec],
                          out_shape=[SDS((m, d), F32), SDS((1, d), F32)], name=name)(*ins)


def _swiglu_fwd(name, ag3):
    _, t_len, w = ag3.shape
    tm = _tile(t_len, (512, 256))

    def body(a_ref, g_ref, o_ref):
        o_ref[...] = _swiglu_f(a_ref[...], g_ref[...]).astype(BF16)

    return pl.pallas_call(
        body, grid=(t_len // tm, 2),
        in_specs=[pl.BlockSpec((None, tm, w), lambda i, j: (j, i, 0)), pl.BlockSpec((None, tm, w), lambda i, j: (j + 2, i, 0))],
        out_specs=pl.BlockSpec((tm, w), lambda i, j: (i, j)), out_shape=SDS((t_len, 2 * w), BF16), name=name)(ag3, ag3)


def _swiglu_bwd(name, ag3, du):
    _, t_len, w = ag3.shape
    tm = _tile(t_len, (512, 256))

    def body(a_ref, g_ref, du_ref, o_ref):
        _, vjp = jax.vjp(_swiglu_f, a_ref[...], g_ref[...])
        da, dg = vjp(du_ref[...])
        o_ref[0] = da.astype(BF16)
        o_ref[1] = dg.astype(BF16)

    out = pl.pallas_call(
        body, grid=(t_len // tm, 2),
        in_specs=[pl.BlockSpec((None, tm, w), lambda i, j: (j, i, 0)), pl.BlockSpec((None, tm, w), lambda i, j: (j + 2, i, 0)),
                  pl.BlockSpec((tm, w), lambda i, j: (i, j))],
        out_specs=pl.BlockSpec((2, None, tm, w), lambda i, j: (0, j, i, 0)), out_shape=SDS((2, 2, t_len, w), BF16), name=name)(ag3, ag3, du)
    return out.reshape(ag3.shape)


def _gate_fwd(name, gp3, br, wb, l):
    _, t_len, d = gp3.shape
    tm = _tile(t_len, MM_ROWS)

    def body(gp_ref, br_ref, wb_ref, o_ref):
        o_ref[...] = _gate_f([gp_ref[n] for n in range(NH)], [br_ref[n] for n in range(NH)],
                             [wb_ref[n] for n in range(NH)]).astype(BF16)

    return pl.pallas_call(
        body, grid=(t_len // tm, d // BW),
        in_specs=[pl.BlockSpec((NH, tm, BW), lambda i, s: (0, i, s)), pl.BlockSpec((NH, tm, BW), lambda i, s: (0, i, 0)),
                  pl.BlockSpec((None, None, NH, BW, BW), lambda i, s: (l, s, 0, 0, 0))],
        out_specs=pl.BlockSpec((tm, BW), lambda i, s: (i, s)), out_shape=SDS((t_len, d), BF16), name=name,
        compiler_params=_cp(VMEM_MM))(gp3, br, wb)


def _gate_bwd(name, gp3, br, wb, l, dmerged):
    _, t_len, d = gp3.shape
    tm = _tile(t_len, (512, 256))
    ns = d // BW

    def body(gp_ref, br_ref, wb_ref, dm_ref, dgp_ref, dbr_ref, dwb_ref):
        i, s = pl.program_id(0), pl.program_id(1)
        gp = [gp_ref[n] for n in range(NH)]
        brv = [br_ref[n].astype(F32) for n in range(NH)]
        wbv = [wb_ref[n].astype(F32) for n in range(NH)]
        _, vjp = jax.vjp(_gate_f, gp, brv, wbv)
        dgp, dbr, dwb = vjp(dm_ref[...])

        @pl.when(s == 0)
        def _():
            dbr_ref[...] = jnp.zeros_like(dbr_ref)

        @pl.when((i == 0) & (s == 0))
        def _():
            dwb_ref[...] = jnp.zeros_like(dwb_ref)

        for n in range(NH):
            dgp_ref[n] = dgp[n].astype(BF16)
            dbr_ref[n] += dbr[n]
            dwb_ref[s, n] += dwb[n]

    return pl.pallas_call(
        body, grid=(t_len // tm, ns),
        in_specs=[pl.BlockSpec((NH, tm, BW), lambda i, s: (0, i, s)), pl.BlockSpec((NH, tm, BW), lambda i, s: (0, i, 0)),
                  pl.BlockSpec((None, None, NH, BW, BW), lambda i, s: (l, s, 0, 0, 0)), pl.BlockSpec((tm, BW), lambda i, s: (i, s))],
        out_specs=[pl.BlockSpec((NH, tm, BW), lambda i, s: (0, i, s)), pl.BlockSpec((NH, tm, BW), lambda i, s: (0, i, 0)),
                   pl.BlockSpec((ns, NH, BW, BW), lambda i, s: (0, 0, 0, 0))],
        out_shape=[SDS(gp3.shape, BF16), SDS((NH, t_len, BW), F32), SDS((ns, NH, BW, BW), F32)], name=name,
        compiler_params=_cp(VMEM_MM))(gp3, br, wb, dmerged)


def _loss_kernel(name, y, tgt):
    m, d = y.shape
    tm = _tile(m, (256,))

    def body(y_ref, t_ref, l_ref, dy_ref):
        err = y_ref[...] - t_ref[...]
        dy_ref[...] = err * (1.0 / d)

        @pl.when(pl.program_id(0) == 0)
        def _():
            l_ref[...] = jnp.zeros_like(l_ref)

        l_ref[...] += 0.5 * jnp.sum(jnp.mean(err * err, axis=-1, keepdims=True), axis=0, keepdims=True)

    row = pl.BlockSpec((tm, d), lambda i: (i, 0))
    return pl.pallas_call(body, grid=(m // tm,), in_specs=[row, row], out_specs=[pl.BlockSpec((1, 1), lambda i: (0, 0)), row],
                          out_shape=[SDS((1, 1), F32), SDS((m, d), F32)], name=name)(y, tgt)


def _zero_first(refs):
    @pl.when(pl.program_id(0) == 0)
    def _():
        for r in refs:
            r[...] = jnp.zeros_like(r)


def _grp(rows, j):
    return pl.BlockSpec((rows, BW), lambda a, j=j: (a, j))


_VEC = pl.BlockSpec((1, BW), lambda a: (0, 0))


def _ret_fwd(tag, proj, cos, sin, decf, decb, gn):
    t_len = proj.shape[0]
    n = t_len // CH
    tab = pl.BlockSpec((CH, BW), lambda a: (a, 0))
    st = pl.BlockSpec((None, BW, BW), lambda a: (a, 0, 0))

    def k1(k_ref, v_ref, cos_ref, sin_ref, df_ref, db_ref, kvf_ref, kvb_ref):
        kvf_ref[...], kvb_ref[...] = _ret_kv_f(k_ref[...], v_ref[...], cos_ref[...], sin_ref[...], df_ref[...], db_ref[...])

    kvf, kvb = pl.pallas_call(
        k1, grid=(n,), in_specs=[_grp(CH, 1), _grp(CH, 2), tab, tab, _VEC, _VEC], out_specs=[st, st],
        out_shape=[SDS((n, BW, BW), F32)] * 2, name=f"ret_kv_{tag}")(proj, proj, cos, sin, decf, decb)

    def k2(kvf_ref, kvb_ref, df_ref, db_ref, sf_ref, sb_ref):
        sf, sb = _ret_scan_f([kvf_ref[a] for a in range(n)], [kvb_ref[a] for a in range(n)], df_ref[...], db_ref[...])
        for a in range(n):
            sf_ref[a] = sf[a]
            sb_ref[a] = sb[a]

    stf, stb = pl.pallas_call(k2, out_shape=[SDS((n, BW, BW), F32)] * 2, name=f"ret_scan_{tag}",
                              compiler_params=_cp(VMEM_BIG))(kvf, kvb, decf, decb)

    def k3(q_ref, k_ref, v_ref, g_ref, cos_ref, sin_ref, sf_ref, sb_ref, df_ref, db_ref, gn_ref, o_ref):
        o_ref[...] = _ret_out_f(q_ref[...], k_ref[...], v_ref[...], g_ref[...], cos_ref[...], sin_ref[...], sf_ref[...],
                                sb_ref[...], df_ref[...], db_ref[...], gn_ref[...]).astype(BF16)

    ret = pl.pallas_call(
        k3, grid=(n,), in_specs=[_grp(CH, 0), _grp(CH, 1), _grp(CH, 2), _grp(CH, 3), tab, tab, st, st, _VEC, _VEC, _VEC],
        out_specs=tab, out_shape=SDS((t_len, BW), BF16), name=f"ret_out_{tag}")(proj, proj, proj, proj, cos, sin, stf, stb, decf, decb, gn)
    return ret, (kvf, kvb, stf, stb)


def _ret_bwd(tag, proj, cos, sin, decf, decb, gn, saved, dret):
    kvf, kvb, stf, stb = saved
    t_len = proj.shape[0]
    n = t_len // CH
    tab = pl.BlockSpec((CH, BW), lambda a: (a, 0))
    st = pl.BlockSpec((None, BW, BW), lambda a: (a, 0, 0))

    def k3(q_ref, k_ref, v_ref, g_ref, cos_ref, sin_ref, sf_ref, sb_ref, df_ref, db_ref, gn_ref, do_ref,
           dq_ref, dk_ref, dv_ref, dg_ref, dsf_ref, dsb_ref, ddf_ref, ddb_ref, dgn_ref):
        cos, sin = cos_ref[...], sin_ref[...]
        f = lambda q, k, v, g, sf, sb, df, db, gnv: _ret_out_f(q, k, v, g, cos, sin, sf, sb, df, db, gnv)
        _, vjp = jax.vjp(f, q_ref[...], k_ref[...], v_ref[...], g_ref[...], sf_ref[...], sb_ref[...], df_ref[...], db_ref[...], gn_ref[...])
        dq, dk, dv, dg, dsf, dsb, ddf, ddb, dgn = vjp(do_ref[...])
        dq_ref[...] = dq.astype(BF16)
        dk_ref[...] = dk
        dv_ref[...] = dv
        dg_ref[...] = dg.astype(BF16)
        dsf_ref[...] = dsf
        dsb_ref[...] = dsb
        _zero_first([ddf_ref, ddb_ref, dgn_ref])
        ddf_ref[...] += ddf
        ddb_ref[...] += ddb
        dgn_ref[...] += dgn

    dq, dk3, dv3, dg, dstf, dstb, ddf3, ddb3, dgn = pl.pallas_call(
        k3, grid=(n,),
        in_specs=[_grp(CH, 0), _grp(CH, 1), _grp(CH, 2), _grp(CH, 3), tab, tab, st, st, _VEC, _VEC, _VEC, tab],
        out_specs=[tab, tab, tab, tab, st, st, _VEC, _VEC, _VEC],
        out_shape=[SDS((t_len, BW), BF16), SDS((t_len, BW), F32), SDS((t_len, BW), F32), SDS((t_len, BW), BF16),
                   SDS((n, BW, BW), F32), SDS((n, BW, BW), F32), SDS((1, BW), F32), SDS((1, BW), F32), SDS((1, BW), F32)],
        name=f"ret_out_bwd_{tag}")(proj, proj, proj, proj, cos, sin, stf, stb, decf, decb, gn, dret)

    def k2(kvf_ref, kvb_ref, df_ref, db_ref, dsf_ref, dsb_ref, dkvf_ref, dkvb_ref, ddf_ref, ddb_ref):
        _, vjp = jax.vjp(_ret_scan_f, [kvf_ref[a] for a in range(n)], [kvb_ref[a] for a in range(n)], df_ref[...], db_ref[...])
        dkvf, dkvb, ddf_ref[...], ddb_ref[...] = vjp(([dsf_ref[a] for a in range(n)], [dsb_ref[a] for a in range(n)]))
        for a in range(n):
            dkvf_ref[a] = dkvf[a]
            dkvb_ref[a] = dkvb[a]

    dkvf, dkvb, ddf2, ddb2 = pl.pallas_call(
        k2, out_shape=[SDS((n, BW, BW), F32), SDS((n, BW, BW), F32), SDS((1, BW), F32), SDS((1, BW), F32)],
        name=f"ret_scan_bwd_{tag}", compiler_params=_cp(VMEM_BIG))(kvf, kvb, decf, decb, dstf, dstb)

    def k1(k_ref, v_ref, cos_ref, sin_ref, df_ref, db_ref, dkvf_ref, dkvb_ref, dk3_ref, dv3_ref, dk_ref, dv_ref, ddf_ref, ddb_ref):
        cos, sin = cos_ref[...], sin_ref[...]
        f = lambda k, v, df, db: _ret_kv_f(k, v, cos, sin, df, db)
        _, vjp = jax.vjp(f, k_ref[...], v_ref[...], df_ref[...], db_ref[...])
        dk, dv, ddf, ddb = vjp((dkvf_ref[...], dkvb_ref[...]))
        dk_ref[...] = (dk + dk3_ref[...]).astype(BF16)
        dv_ref[...] = (dv + dv3_ref[...]).astype(BF16)
        _zero_first([ddf_ref, ddb_ref])
        ddf_ref[...] += ddf
        ddb_ref[...] += ddb

    dk, dv, ddf1, ddb1 = pl.pallas_call(
        k1, grid=(n,), in_specs=[_grp(CH, 1), _grp(CH, 2), tab, tab, _VEC, _VEC, st, st, tab, tab],
        out_specs=[tab, tab, _VEC, _VEC],
        out_shape=[SDS((t_len, BW), BF16), SDS((t_len, BW), BF16), SDS((1, BW), F32), SDS((1, BW), F32)],
        name=f"ret_kv_bwd_{tag}")(proj, proj, cos, sin, decf, decb, dkvf, dkvb, dk3, dv3)
    return dq, dk, dv, dg, ddf1 + ddf2 + ddf3, ddb1 + ddb2 + ddb3, dgn


def _pool_fwd(tag, pvp, wbd, scale, t_len):
    def body(p_ref, w_ref, s_ref, o_ref):
        o_ref[...] = _pool_f(p_ref[...], w_ref[...], s_ref[...], t_len).astype(BF16)

    return pl.pallas_call(body, out_shape=SDS(pvp.shape, BF16), name=f"pool_{tag}", compiler_params=_cp(VMEM_BIG))(pvp, wbd, scale)


def _pool_bwd(tag, pvp, wbd, scale, t_len, dpool_p):
    def body(p_ref, w_ref, s_ref, do_ref, dp_ref, dw_ref, ds_ref):
        f = lambda p, w, s: _pool_f(p, w, s, t_len)
        _, vjp = jax.vjp(f, p_ref[...], w_ref[...], s_ref[...])
        dp, dw, ds = vjp(do_ref[...])
        dp_ref[...] = dp.astype(BF16)
        dw_ref[...] = dw
        ds_ref[...] = ds

    return pl.pallas_call(body, out_shape=[SDS(pvp.shape, BF16), SDS((BW, BW), F32), SDS((1, BW), F32)],
                          name=f"pool_bwd_{tag}", compiler_params=_cp(VMEM_BIG))(pvp, wbd, scale, dpool_p)


def _na_window(r, rows):
    r0 = jnp.clip(r - NA_ROWS // 2, 0, rows - NA_ROWS)
    return r0, r - r0


NA_PAIRS = 2 * NA_ROWS - 2


def _na_bias_pieces(b_ref, pat):
    return [[b_ref[h * NA_PAIRS + 2 * k - pat + NA_ROWS - 1] for k in range(NA_ROWS // 2)] for h in range(NH)]


def _na_keys(tag, proj, gk):
    t_len = proj.shape[0]
    tm = _tile(t_len, (256,))

    def body(k_ref, v_ref, gk_ref, kn_ref, vb_ref):
        kn_ref[...] = _head_rms(k_ref[...], gk_ref[...]).astype(BF16)
        vb_ref[...] = v_ref[...].astype(BF16)

    row = pl.BlockSpec((tm, BW), lambda a: (a, 0))
    return pl.pallas_call(body, grid=(t_len // tm,), in_specs=[_grp(tm, 6), _grp(tm, 7), _VEC], out_specs=[row, row],
                          out_shape=[SDS((t_len, BW), BF16)] * 2, name=f"na_keys_{tag}")(proj, proj, gk)


def _na_keys_bwd(tag, proj, gk, dkn, dv):
    t_len = proj.shape[0]
    tm = _tile(t_len, (256,))

    def body(k_ref, gk_ref, dkn_ref, dv_ref, dk_ref, dvb_ref, dgk_ref):
        _, vjp = jax.vjp(_head_rms, k_ref[...], gk_ref[...])
        dk, dgk = vjp(dkn_ref[...])
        dk_ref[...] = dk.astype(BF16)
        dvb_ref[...] = dv_ref[...].astype(BF16)
        _zero_first([dgk_ref])
        dgk_ref[...] += dgk

    row = pl.BlockSpec((tm, BW), lambda a: (a, 0))
    return pl.pallas_call(body, grid=(t_len // tm,), in_specs=[_grp(tm, 6), _VEC, row, row], out_specs=[row, row, _VEC],
                          out_shape=[SDS((t_len, BW), BF16), SDS((t_len, BW), BF16), SDS((1, BW), F32)],
                          name=f"na_keys_bwd_{tag}")(proj, gk, dkn, dv)


def _na_fwd(tag, proj, kn, vb, bias_tab, gq):
    t_len = proj.shape[0]
    rows = t_len // GRID_W
    win = NA_ROWS * GRID_W

    def body(q_ref, k_ref, v_ref, b_ref, gq_ref, o_ref):
        r0, pat = _na_window(pl.program_id(0), rows)
        start = pl.multiple_of(r0 * GRID_W, GRID_W)
        o_ref[...] = _na_f(q_ref[...], k_ref[pl.ds(start, win), :], v_ref[pl.ds(start, win), :], _na_bias_pieces(b_ref, pat),
                           gq_ref[...]).astype(BF16)

    whole = pl.BlockSpec((t_len, BW), lambda a: (0, 0))
    return pl.pallas_call(
        body, grid=(rows,),
        in_specs=[_grp(GRID_W, 5), whole, whole, pl.BlockSpec(bias_tab.shape, lambda a: (0, 0, 0)), _VEC],
        out_specs=pl.BlockSpec((GRID_W, BW), lambda a: (a, 0)), out_shape=SDS((t_len, BW), BF16),
        name=f"na_{tag}", compiler_params=_cp(VMEM_MM))(proj, kn, vb, bias_tab, gq)


def _na_bwd(tag, proj, kn, vb, bias_tab, gq, dna):
    t_len = proj.shape[0]
    rows = t_len // GRID_W
    win = NA_ROWS * GRID_W

    def body(q_ref, k_ref, v_ref, b_ref, gq_ref, do_ref, dq_ref, dk_ref, dv_ref, db_ref, dgq_ref):
        r0, pat = _na_window(pl.program_id(0), rows)
        start = pl.multiple_of(r0 * GRID_W, GRID_W)
        _zero_first([dk_ref, dv_ref, db_ref, dgq_ref])
        _, vjp = jax.vjp(_na_f, q_ref[...], k_ref[pl.ds(start, win), :].astype(F32), v_ref[pl.ds(start, win), :].astype(F32),
                         _na_bias_pieces(b_ref, pat), gq_ref[...])
        dq, dk, dv, db, dgq = vjp(do_ref[...])
        dq_ref[...] = dq.astype(BF16)
        dk_ref[pl.ds(start, win), :] += dk
        dv_ref[pl.ds(start, win), :] += dv
        for h in range(NH):
            for k in range(NA_ROWS // 2):
                db_ref[h * NA_PAIRS + 2 * k - pat + NA_ROWS - 1] += db[h][k]
        dgq_ref[...] += dgq

    whole = pl.BlockSpec((t_len, BW), lambda a: (0, 0))
    tabspec = pl.BlockSpec(bias_tab.shape, lambda a: (0, 0, 0))
    row = pl.BlockSpec((GRID_W, BW), lambda a: (a, 0))
    return pl.pallas_call(
        body, grid=(rows,), in_specs=[_grp(GRID_W, 5), whole, whole, tabspec, _VEC, row],
        out_specs=[row, whole, whole, tabspec, _VEC],
        out_shape=[SDS((t_len, BW), BF16), SDS((t_len, BW), F32), SDS((t_len, BW), F32), SDS(bias_tab.shape, F32), SDS((1, BW), F32)],
        name=f"na_bwd_{tag}", compiler_params=_cp(VMEM_BIG))(proj, kn, vb, bias_tab, gq, dna)


def _mem_fwd(tag, proj, memkv, gq, gk):
    t_len = proj.shape[0]
    n_mem = memkv.shape[0]
    tm = _tile(t_len, (256,))

    def body(q_ref, mk_ref, mv_ref, gq_ref, gk_ref, o_ref):
        o_ref[...] = _mem_f(q_ref[...], mk_ref[...], mv_ref[...], gq_ref[...], gk_ref[...]).astype(BF16)

    mspec = lambda j: pl.BlockSpec((n_mem, BW), lambda a, j=j: (0, j))
    return pl.pallas_call(
        body, grid=(t_len // tm,), in_specs=[_grp(tm, 8), mspec(0), mspec(1), _VEC, _VEC],
        out_specs=pl.BlockSpec((tm, BW), lambda a: (a, 0)), out_shape=SDS((t_len, BW), BF16), name=f"mem_{tag}")(proj, memkv, memkv, gq, gk)


def _mem_bwd(tag, proj, memkv, gq, gk, dmo):
    t_len = proj.shape[0]
    n_mem = memkv.shape[0]
    tm = _tile(t_len, (256,))

    def body(q_ref, mk_ref, mv_ref, gq_ref, gk_ref, do_ref, dq_ref, dmk_ref, dmv_ref, dgq_ref, dgk_ref):
        _zero_first([dmk_ref, dmv_ref, dgq_ref, dgk_ref])
        _, vjp = jax.vjp(_mem_f, q_ref[...], mk_ref[...], mv_ref[...], gq_ref[...], gk_ref[...])
        dq, dmk, dmv, dgq, dgk = vjp(do_ref[...])
        dq_ref[...] = dq.astype(BF16)
        dmk_ref[...] += dmk
        dmv_ref[...] += dmv
        dgq_ref[...] += dgq
        dgk_ref[...] += dgk

    mspec = lambda j: pl.BlockSpec((n_mem, BW), lambda a, j=j: (0, j))
    mout = pl.BlockSpec((n_mem, BW), lambda a: (0, 0))
    row = pl.BlockSpec((tm, BW), lambda a: (a, 0))
    dq, dmk, dmv, dgq, dgk = pl.pallas_call(
        body, grid=(t_len // tm,), in_specs=[_grp(tm, 8), mspec(0), mspec(1), _VEC, _VEC, row],
        out_specs=[row, mout, mout, _VEC, _VEC],
        out_shape=[SDS((t_len, BW), BF16), SDS((n_mem, BW), F32), SDS((n_mem, BW), F32), SDS((1, BW), F32), SDS((1, BW), F32)],
        name=f"mem_bwd_{tag}")(proj, memkv, memkv, gq, gk, dmo)
    return dq, jnp.concatenate([dmk, dmv], axis=1), dgq, dgk


NA_NJ = 2 * NA_COLS - 1


def _na_onehot():
    q = np.arange(GRID_W)[:, None]
    kc = np.arange(GRID_W)[None, :]
    qwin = np.clip(q - NA_COLS // 2, 0, GRID_W - NA_COLS)
    mask = (kc >= qwin) & (kc < qwin + NA_COLS)
    col = np.clip(kc - q, -(NA_COLS - 1), NA_COLS - 1) + NA_COLS - 1
    onehot = np.zeros((LANES, GRID_W, 2, GRID_W), np.float32)
    qq, kk = np.nonzero(mask)
    for half in range(2):
        onehot[half * NA_NJ + col[qq, kk], qq, half, kk] = 1.0
    valid = np.broadcast_to(mask[:, None, :], (GRID_W, 2, GRID_W)).astype(np.float32)
    return onehot.reshape(LANES, -1), valid.reshape(1, -1)


def _na_pair_rows(rpb):
    n_layers = rpb.shape[0]
    pair = jnp.concatenate([rpb[:, :, :-1], rpb[:, :, 1:]], axis=-1).reshape(n_layers, NH * NA_PAIRS, 2 * NA_NJ)
    return jnp.pad(pair, ((0, 0), (0, 0), (0, LANES - 2 * NA_NJ)))


def _na_bias_table(rpb):
    n_layers = rpb.shape[0]
    onehot, valid = _na_onehot()
    width = onehot.shape[1]

    def body(r_ref, oh_ref, ok_ref, o_ref):
        t = jnp.dot(r_ref[...], oh_ref[...], precision=lax.Precision.HIGHEST, preferred_element_type=F32)
        o_ref[...] = jnp.where(ok_ref[...] > 0, t, NEG_INF)

    out = pl.pallas_call(
        body, grid=(n_layers,),
        in_specs=[pl.BlockSpec((None, NH * NA_PAIRS, LANES), lambda l: (l, 0, 0)), pl.BlockSpec((LANES, width), lambda l: (0, 0)),
                  pl.BlockSpec((1, width), lambda l: (0, 0))],
        out_specs=pl.BlockSpec((None, NH * NA_PAIRS, width), lambda l: (l, 0, 0)),
        out_shape=SDS((n_layers, NH * NA_PAIRS, width), F32), name="na_bias_table")(
            _na_pair_rows(rpb), jnp.asarray(onehot), jnp.asarray(valid))
    return out.reshape(n_layers, NH * NA_PAIRS, GRID_W, 2 * GRID_W)


def _na_bias_grad(dtab):
    n_layers = dtab.shape[0]
    onehot, _ = _na_onehot()
    width = onehot.shape[1]

    def body(x_ref, oh_ref, o_ref):
        o_ref[...] = jnp.dot(x_ref[...], oh_ref[...], precision=lax.Precision.HIGHEST, preferred_element_type=F32)

    out = pl.pallas_call(
        body, grid=(n_layers,),
        in_specs=[pl.BlockSpec((None, NH * NA_PAIRS, width), lambda l: (l, 0, 0)), pl.BlockSpec((width, LANES), lambda l: (0, 0))],
        out_specs=pl.BlockSpec((None, NH * NA_PAIRS, LANES), lambda l: (l, 0, 0)),
        out_shape=SDS((n_layers, NH * NA_PAIRS, LANES), F32), name="na_bias_grad")(
            dtab.reshape(n_layers, NH * NA_PAIRS, width), jnp.asarray(onehot.T.copy()))
    out = out.reshape(n_layers, NH, NA_PAIRS, LANES)
    zero = jnp.zeros((n_layers, NH, 1, NA_NJ), F32)
    return (jnp.concatenate([out[..., :NA_NJ], zero], axis=2) + jnp.concatenate([zero, out[..., NA_NJ:2 * NA_NJ]], axis=2))


def _place():
    x, y, c = lax.axis_index("x"), lax.axis_index("y"), lax.axis_index("c")
    chips = [(1 - x, y), (x, 1 - y), (1 - x, 1 - y)]
    return x, y, c, chips


def _remote(src, dst, ssem, rsem, dev):
    return pltpu.make_async_remote_copy(src_ref=src, dst_ref=dst, send_sem=ssem, recv_sem=rsem, device_id=dev, device_id_type=MESH)


HBM = pl.BlockSpec(memory_space=pltpu.HBM)
SEM = pl.BlockSpec(memory_space=pltpu.SEMAPHORE)
DATAFLOW = pltpu.SideEffectType.DATAFLOW_SIDE_EFFECTING


def _split_copies(name, srcs, lands, n_copies, plan):
    n_s, n_l = len(srcs), len(lands)
    hbm = lambda a: pltpu.HBM(a.shape, a.dtype)

    def start_body(*refs):
        ins = refs[:n_s + n_l]
        outs = refs[n_s + n_l:]
        sems, token = outs[:2 * n_copies], outs[-1]
        for k, (src, dst, dev) in enumerate(plan(ins[:n_s], ins[n_s:])):
            _remote(src, dst, sems[k], sems[n_copies + k], dev).start()
        token[...] = jnp.zeros_like(token)

    outs = pl.pallas_call(
        start_body, name=f"{name}_start",
        out_shape=(pltpu.SemaphoreType.DMA(()),) * (2 * n_copies) + tuple(hbm(a) for a in (*srcs, *lands)) + (SDS((8, LANES), F32),),
        in_specs=(HBM,) * (n_s + n_l), out_specs=(SEM,) * (2 * n_copies) + (HBM,) * (n_s + n_l) + (pl.BlockSpec(memory_space=pltpu.VMEM),),
        input_output_aliases={i: 2 * n_copies + i for i in range(n_s + n_l)},
        compiler_params=pltpu.CompilerParams(has_side_effects=DATAFLOW),
    )(*[pltpu.with_memory_space_constraint(a, pltpu.HBM) for a in (*srcs, *lands)])
    sems, thru, token = outs[:2 * n_copies], outs[2 * n_copies:-1], outs[-1]

    def wait(after_wait):
        def wait_body(*refs):
            ins = refs[:n_s + n_l]
            sem_refs = refs[n_s + n_l:n_s + n_l + 2 * n_copies]
            for k, (src, dst, dev) in enumerate(plan(ins[:n_s], ins[n_s:])):
                cp = _remote(src, dst, sem_refs[k], sem_refs[n_copies + k], dev)
                cp.wait_send()
                cp.wait_recv()

        res = pl.pallas_call(
            wait_body, name=f"{name}_wait", out_shape=tuple(hbm(a) for a in (*srcs, *lands)),
            in_specs=(HBM,) * (n_s + n_l) + (SEM,) * (2 * n_copies) + (ANY,), out_specs=(HBM,) * (n_s + n_l),
            input_output_aliases={i: i for i in range(n_s + n_l)},
            compiler_params=pltpu.CompilerParams(has_side_effects=DATAFLOW),
        )(*thru, *sems, after_wait)
        return list(res[:n_s]), list(res[n_s:])

    return token, wait


def _gather_plan(srcs, lands):
    x, y, c, chips = _place()
    me = 2 * x + y
    out = []
    for src, land in zip(srcs, lands):
        h = src.shape[0] // 2
        out.append((src, land.at[me], (x, y, 1 - c)))
        for px, py in chips:
            out.append((src.at[pl.ds(c * h, h)], land.at[me, pl.ds(c * h, h)], (px, py, c)))
    return out


def _scatter_plan(srcs, lands):
    x, y, c, chips = _place()
    out = []
    for src, land in zip(srcs, lands):
        for r, (px, py) in enumerate(chips):
            out.append((src.at[2 * px + py], land.at[r], (px, py, c)))
    return out


def _pass_plan(srcs, lands):
    x, y, c, chips = _place()
    out = []
    for land in lands:
        h = land.shape[1] // 2
        for px, py in chips:
            piece = land.at[2 * px + py, pl.ds(c * h, h)]
            out.append((piece, piece, (x, y, 1 - c)))
    return out


def _pair_plan(srcs, lands):
    x, y, c, _ = _place()
    out = []
    for src, land in zip(srcs, lands):
        h = land.shape[1]
        out.append((src.at[:, pl.ds((1 - c) * h, h)], land, (x, y, 1 - c)))
    return out


def _half_plan(srcs, lands):
    x, y, c, _ = _place()
    out = []
    for land in lands:
        h = land.shape[0] // 2
        mine = land.at[pl.ds(c * h, h)]
        out.append((mine, mine, (x, y, 1 - c)))
    return out


def _forward_halves(name, lands):
    n = len(lands)

    def body(*refs):
        ins, outs = refs[:n], refs[n:2 * n]
        ssem, rsem = refs[2 * n:]
        x, y, c, chips = _place()
        sib = (x, y, 1 - c)

        def copy(i, r, half):
            h = ins[i].shape[1] // 2
            px, py = chips[r]
            rows = pl.ds(half * h, h)
            return _remote(ins[i].at[2 * px + py, rows], outs[i].at[2 * px + py, rows], ssem.at[i, r], rsem.at[i, r], sib)

        for i in range(n):
            for r in range(3):
                copy(i, r, c).start()
        for i in range(n):
            for r in range(3):
                copy(i, r, 1 - c).wait_recv()
                copy(i, r, c).wait_send()

    return pl.pallas_call(
        body, in_specs=[ANY] * n, out_specs=[ANY] * n, out_shape=[SDS(a.shape, a.dtype) for a in lands],
        input_output_aliases={i: i for i in range(n)},
        scratch_shapes=[pltpu.SemaphoreType.DMA((n, 3)), pltpu.SemaphoreType.DMA((n, 3))], name=name)(*lands)


def _pair_exchange(name, gs):
    n = len(gs)

    def body(*refs):
        ins, outs = refs[:n], refs[n:2 * n]
        ssem, rsem = refs[2 * n:]
        x, y, c, _ = _place()
        cps = []
        for i in range(n):
            h = ins[i].shape[1] // 2
            cps.append(_remote(ins[i].at[:, pl.ds((1 - c) * h, h)], outs[i], ssem.at[i], rsem.at[i], (x, y, 1 - c)))
            cps[-1].start()
        for cp in cps:
            cp.wait()

    return pl.pallas_call(
        body, in_specs=[ANY] * n, out_specs=[ANY] * n,
        out_shape=[SDS((g.shape[0], g.shape[1] // 2, g.shape[2]), g.dtype) for g in gs],
        scratch_shapes=[pltpu.SemaphoreType.DMA((n,)), pltpu.SemaphoreType.DMA((n,))], name=name)(*gs)


def _half_exchange(name, fs):
    n = len(fs)

    def body(*refs):
        ins, outs = refs[:n], refs[n:2 * n]
        ssem, rsem = refs[2 * n:]
        x, y, c, _ = _place()

        def copy(i, half):
            h = ins[i].shape[0] // 2
            return _remote(ins[i].at[pl.ds(half * h, h)], outs[i].at[pl.ds(half * h, h)], ssem.at[i], rsem.at[i], (x, y, 1 - c))

        for i in range(n):
            copy(i, c).start()
        for i in range(n):
            copy(i, 1 - c).wait_recv()
            copy(i, c).wait_send()

    return pl.pallas_call(
        body, in_specs=[ANY] * n, out_specs=[ANY] * n, out_shape=[SDS(f.shape, f.dtype) for f in fs],
        input_output_aliases={i: i for i in range(n)},
        scratch_shapes=[pltpu.SemaphoreType.DMA((n,)), pltpu.SemaphoreType.DMA((n,))], name=name)(*fs)


def _gather_small(v):
    def body(v_ref, o_ref, lsem, ssem, rsem):
        x, y, c, _ = _place()
        me = 4 * x + 2 * y + c
        flips = [(fx, fy, fc) for fx in (0, 1) for fy in (0, 1) for fc in (0, 1)][1:]
        peer = lambda f: (1 - x if f[0] else x, 1 - y if f[1] else y, 1 - c if f[2] else c)
        loc = pltpu.make_async_copy(v_ref, o_ref.at[me], lsem)
        loc.start()
        cps = [_remote(v_ref, o_ref.at[me], ssem.at[k], rsem.at[k], peer(f)) for k, f in enumerate(flips)]
        for cp in cps:
            cp.start()
        for k, f in enumerate(flips):
            px, py, pc = peer(f)
            land = o_ref.at[4 * px + 2 * py + pc]
            _remote(land, land, ssem.at[k], rsem.at[k], peer(f)).wait_recv()
        for cp in cps:
            cp.wait_send()
        loc.wait()

    return pl.pallas_call(
        body, in_specs=[ANY], out_specs=ANY, out_shape=SDS((N_DEV,) + v.shape, v.dtype),
        scratch_shapes=[pltpu.SemaphoreType.DMA, pltpu.SemaphoreType.DMA((N_DEV - 1,)), pltpu.SemaphoreType.DMA((N_DEV - 1,))],
        name="gather_small")(v)


def _rows_tile(r, big=False):
    return _tile(r, ((512,) if big else ()) + (256, 352, 128, 64))


def _pair_sum(name, g, r1, core, me):
    n_l, s_n, h, c = r1.shape
    tr = _rows_tile(h, big=True)
    nb = h // tr

    def body(idx_ref, g_ref, r_ref, pb_ref, own_ref):
        p = g_ref[...] + r_ref[...]
        pb_ref[...] = p.astype(BF16)

        @pl.when(pl.program_id(2) == idx_ref[1])
        def _():
            own_ref[...] = p

    blk = (None, None, tr, c)
    grid_spec = pltpu.PrefetchScalarGridSpec(
        num_scalar_prefetch=1, grid=(n_l, nb, s_n),
        in_specs=[pl.BlockSpec(blk, lambda a, i, s, idx: (a, s, idx[0] * nb + i, 0)), pl.BlockSpec(blk, lambda a, i, s, idx: (a, s, i, 0))],
        out_specs=[pl.BlockSpec(blk, lambda a, i, s, idx: (a, s, i, 0)), pl.BlockSpec((None, tr, c), lambda a, i, s, idx: (a, i, 0))])
    idx = jnp.stack([core, me]).astype(jnp.int32)
    return pl.pallas_call(body, grid_spec=grid_spec, out_shape=[SDS(r1.shape, BF16), SDS((n_l, h, c), F32)], name=name,
                          compiler_params=_cp(VMEM_MM))(idx, g, r1)


def _chip_sum(name, own, r2, core):
    n_l, h, c = own.shape
    tr = _rows_tile(h, big=True)
    nb = h // tr

    def body(idx_ref, o_ref, r_ref, f_ref):
        f_ref[...] = ((o_ref[...] + r_ref[0].astype(F32)) + r_ref[1].astype(F32)) + r_ref[2].astype(F32)

    grid_spec = pltpu.PrefetchScalarGridSpec(
        num_scalar_prefetch=1, grid=(n_l, nb),
        in_specs=[pl.BlockSpec((None, tr, c), lambda a, i, idx: (a, i, 0)), pl.BlockSpec((3, None, tr, c), lambda a, i, idx: (0, a, i, 0))],
        out_specs=pl.BlockSpec((None, tr, c), lambda a, i, idx: (a, idx[0] * nb + i, 0)))
    return pl.pallas_call(body, grid_spec=grid_spec, out_shape=SDS((n_l, 2 * h, c), F32), name=name,
                          compiler_params=_cp(VMEM_MM))(jnp.reshape(core, (1,)).astype(jnp.int32), own, r2)


def _adamw_math(w, g, m, v):
    m = ADAM_B1 * m + (1.0 - ADAM_B1) * g
    v = ADAM_B2 * v + (1.0 - ADAM_B2) * jnp.square(g)
    m_hat = m / (1.0 - ADAM_B1 ** ADAM_STEP)
    v_hat = v / (1.0 - ADAM_B2 ** ADAM_STEP)
    delta = -ADAM_LR * (m_hat / (jnp.sqrt(v_hat) + ADAM_EPS) + ADAM_WD * w)
    return delta, m, v


def _adamw(name, l, w, g, m, v, bufs):
    _, r, c = w.shape
    tr = _rows_tile(r)

    def body(w_ref, g_ref, m_ref, v_ref, *rest):
        go_ref, d_ref, nm_ref, nv_ref = rest[4:]
        g = g_ref[...]
        go_ref[...] = g
        d_ref[...], nm_ref[...], nv_ref[...] = _adamw_math(w_ref[...], g, m_ref[...], v_ref[...])

    blk = pl.BlockSpec((None, tr, c), lambda i: (l, i, 0))
    return pl.pallas_call(body, grid=(r // tr,), in_specs=[blk, pl.BlockSpec((tr, c), lambda i: (i, 0)), blk, blk] + [ANY] * 4,
                          out_specs=[blk] * 4, out_shape=[SDS(w.shape, F32)] * 4, input_output_aliases={4 + k: k for k in range(4)},
                          name=name)(w, g, m, v, *bufs)


def _small_update(parts, w, m, v):
    def body(p_ref, w_ref, m_ref, v_ref, g_ref, d_ref, nm_ref, nv_ref):
        g = p_ref[0]
        for d in range(1, N_DEV):
            g = g + p_ref[d]
        g_ref[...] = g
        d_ref[...], nm_ref[...], nv_ref[...] = _adamw_math(w_ref[...], g, m_ref[...], v_ref[...])

    return pl.pallas_call(body, out_shape=[SDS(w.shape, F32)] * 4, name="small_update")(parts, w, m, v)


SMALL = ["norm_mix_g", "norm_mem_g", "ret_decay_fwd", "ret_decay_bwd", "ret_norm_g", "pool_w", "pool_scale", "na_q_norm_g",
         "na_k_norm_g", "na_rpb", "mem_q_norm_g", "mem_k_norm_g", "norm_ffn_g"]
BIG = ["w_in", "w_gate", "w_mem_kv", "w_branch", "w_out", "w_ffn_in", "w_ffn_out"]
GROUPS = (("w_in", "w_gate", "w_mem_kv", "w_branch", "w_out"), ("w_ffn_in", "w_ffn_out"))
REDUCE_GROUPS = (("w_ffn_in", "w_ffn_out"), ("w_gate", "w_branch", "w_out"), ("w_in", "w_mem_kv"))
ORDER = ["norm_mix_g", "norm_mem_g", "w_in", "w_gate", "ret_decay_fwd", "ret_decay_bwd", "ret_norm_g", "pool_w", "pool_scale",
         "na_q_norm_g", "na_k_norm_g", "na_rpb", "mem_q_norm_g", "mem_k_norm_g", "w_mem_kv", "w_branch", "w_out", "norm_ffn_g",
         "w_ffn_in", "w_ffn_out"]


def _pack(arrs):
    rows = []
    for a in arrs:
        f = a.reshape(-1).astype(F32)
        pad = (-f.shape[0]) % (8 * LANES)
        rows.append(jnp.pad(f, (0, pad)).reshape(-1, LANES))
    return jnp.concatenate(rows, axis=0)


def _unpack(p, like):
    out, at = [], 0
    for a in like:
        n = int(np.prod(a.shape))
        rows = -(-n // (8 * LANES)) * 8
        out.append(p[at:at + rows].reshape(-1)[:n].reshape(a.shape))
        at += rows
    return out


def _rope_tables(t_len):
    half = HEAD // 2
    inv = ROPE_THETA ** (-jnp.arange(half, dtype=F32) / half)
    ang = jnp.arange(t_len, dtype=F32)[:, None] * inv[None, :]
    cos, sin = jnp.cos(ang), jnp.sin(ang)
    return jnp.tile(jnp.concatenate([cos, cos], axis=1), (1, NH)), jnp.tile(jnp.concatenate([-sin, sin], axis=1), (1, NH))


def _block_diag(pw):
    out = jnp.zeros((BW, BW), pw.dtype)
    for g in range(NH):
        out = out.at[g * HEAD:(g + 1) * HEAD, g * HEAD:(g + 1) * HEAD].set(pw[g])
    return out


def _fold_heads(v):
    return v.reshape(NH, HEAD).sum(axis=0)


def _fold_lanes(v):
    return v.reshape(NH, HEAD).sum(axis=1)


def kernel(x, mem, norm_mix_g, norm_mem_g, w_in, w_gate, ret_decay_fwd, ret_decay_bwd, ret_norm_g, pool_w, pool_scale, na_q_norm_g, na_k_norm_g, na_rpb, mem_q_norm_g, mem_k_norm_g, w_mem_kv, w_branch, w_out, norm_ffn_g, w_ffn_in, w_ffn_out, loss_target, m_norm_mix_g, m_norm_mem_g, m_w_in, m_w_gate, m_ret_decay_fwd, m_ret_decay_bwd, m_ret_norm_g, m_pool_w, m_pool_scale, m_na_q_norm_g, m_na_k_norm_g, m_na_rpb, m_mem_q_norm_g, m_mem_k_norm_g, m_w_mem_kv, m_w_branch, m_w_out, m_norm_ffn_g, m_w_ffn_in, m_w_ffn_out, v_norm_mix_g, v_norm_mem_g, v_w_in, v_w_gate, v_ret_decay_fwd, v_ret_decay_bwd, v_ret_norm_g, v_pool_w, v_pool_scale, v_na_q_norm_g, v_na_k_norm_g, v_na_rpb, v_mem_q_norm_g, v_mem_k_norm_g, v_w_mem_kv, v_w_branch, v_w_out, v_norm_ffn_g, v_w_ffn_in, v_w_ffn_out):
    args = dict(locals())
    W = {n: args[n] for n in ORDER}
    M = {n: args["m_" + n] for n in ORDER}
    V = {n: args["v_" + n] for n in ORDER}
    n_layers, d_model = norm_mix_g.shape
    assert x.shape[0] == 1
    t_len = x.shape[1]
    xc, yc, cc = lax.axis_index("x"), lax.axis_index("y"), lax.axis_index("c")
    me_chip = 2 * xc + yc

    def as3(a):
        return a.reshape(a.shape[0], -1, a.shape[-1])

    gather_waits, token = {}, jnp.zeros((), F32)
    for l in range(n_layers):
        for gi, names in enumerate(GROUPS):
            srcs = [as3(W[n])[l].astype(BF16) for n in names]
            lands = [lax.empty((N_CHIPS,) + a.shape, BF16) for a in srcs]
            tok, gather_waits[l, gi] = _split_copies(f"gather_l{l}_g{gi}", srcs, lands, 4 * len(names), _gather_plan)
            token = token + tok[0, 0]

    cos, sin = _rope_tables(t_len)
    btabs = _na_bias_table(na_rpb)
    lane = lambda a: a.reshape(1, -1).astype(F32)
    x2d, mem2d, tgt = x[0], mem[0], loss_target[0]
    norm_mix_g = norm_mix_g + token

    def fetch(l, gi, after):
        lands = gather_waits[l, gi](after)[1]
        return _split_copies(f"pass_l{l}_g{gi}", [], lands, 3 * len(GROUPS[gi]), _pass_plan)

    saved = []
    _, passing = fetch(0, 0, x2d)
    for l in range(n_layers):
        tag = f"l{l}"
        gathered = dict(zip(GROUPS[0], passing(x2d)[1]))
        wi = gathered["w_in"].transpose(1, 0, 2).reshape(1, 1, d_model, -1)
        wg = gathered["w_gate"][None]
        wmkv = gathered["w_mem_kv"].reshape(1, 1, d_model, 2 * BW)
        wb = gathered["w_branch"].reshape(1, N_CHIPS, NH, BW, BW)
        wo = gathered["w_out"].reshape(1, 1, d_model, d_model)
        s = {"x": x2d}
        s["decf"], s["decb"] = lane(jnp.repeat(ret_decay_fwd[l], HEAD)), lane(jnp.repeat(ret_decay_bwd[l], HEAD))
        s["gn"], s["pscale"] = lane(ret_norm_g[l]), lane(pool_scale[l])
        s["wbd"] = _block_diag(pool_w[l])
        s["nagq"], s["nagk"] = lane(jnp.tile(na_q_norm_g[l], NH)), lane(jnp.tile(na_k_norm_g[l], NH))
        s["mgq"], s["mgk"] = lane(jnp.tile(mem_q_norm_g[l], NH)), lane(jnp.tile(mem_k_norm_g[l], NH))
        s["btab"] = btabs[l]
        s["h"] = _rms_fwd(f"rms_mix_{tag}", x2d, lane(norm_mix_g[l]))
        s["proj"] = _mm_nn(f"mm_in_{tag}", s["h"], wi, 0)
        s["gp3"] = _mm_nn(f"mm_gate_{tag}", s["h"], wg, 0, out3d=True)
        tok, passing = fetch(l, 1, s["gp3"])
        s["hm"] = _rms_fwd(f"rms_mem_{tag}", mem2d, lane(norm_mem_g[l]) + tok[0, 0])
        s["memkv"] = _mm_nn(f"mm_memkv_{tag}", s["hm"], wmkv, 0)
        ret, s["ret_saved"] = _ret_fwd(tag, s["proj"], cos, sin, s["decf"], s["decb"], s["gn"])
        s["pvp"] = jnp.pad(s["proj"][:, 4 * BW:5 * BW], ((POOL_PAD, POOL_PAD), (0, 0)))
        pool = _pool_fwd(tag, s["pvp"], s["wbd"], s["pscale"], t_len)[POOL_PAD:POOL_PAD + t_len]
        s["kn"], s["vb"] = _na_keys(tag, s["proj"], s["nagk"])
        na = _na_fwd(tag, s["proj"], s["kn"], s["vb"], s["btab"], s["nagq"])
        mo = _mem_fwd(tag, s["proj"], s["memkv"], s["mgq"], s["mgk"])
        s["br"] = jnp.stack([ret, pool, na, mo])
        s["merged"] = _gate_fwd(f"gate_{tag}", s["gp3"], s["br"], wb, 0)
        s["x2"] = _mm_nn(f"mm_out_{tag}", s["merged"], wo, 0, add=x2d)
        gathered = dict(zip(GROUPS[1], passing(s["x2"])[1]))
        pin_ffn = jnp.zeros((), F32)
        if l + 1 < n_layers:
            tok, passing = fetch(l + 1, 0, s["x2"])
            pin_ffn = tok[0, 0]
        wfi = gathered["w_ffn_in"][None]
        wfo = gathered["w_ffn_out"].reshape(1, 1, -1, d_model)
        s["w"] = (wi, wg, wmkv, wb, wo, wfi, wfo)
        s["h2"] = _rms_fwd(f"rms_ffn_{tag}", s["x2"], lane(norm_ffn_g[l]) + pin_ffn)
        s["ag3"] = _mm_nn(f"mm_ffn_in_{tag}", s["h2"], wfi, 0, out3d=True)
        s["u"] = _swiglu_fwd(f"swiglu_{tag}", s["ag3"])
        x2d = _mm_nn(f"mm_ffn_out_{tag}", s["u"], wfo, 0, add=s["x2"])
        saved.append(s)

    loss_part, dx = _loss_kernel("loss", x2d, tgt)

    dbtabs = [None] * n_layers
    gsmall = {n: [None] * n_layers for n in SMALL}
    opt_view = {n: (lambda a, n=n: jnp.swapaxes(as3(a), 1, 2) if n == "w_in" else as3(a)) for n in BIG}
    out_bufs = {n: tuple(lax.empty(opt_view[n](W[n]).shape, F32) for _ in range(4)) for n in BIG}

    def reduce_group(l, gi, grads):
        names = REDUCE_GROUPS[gi]
        gs = [grads[n] for n in names]
        lands = [lax.empty((N_CHIPS, g.shape[1] // 2, g.shape[2]), F32) for g in gs]
        tok0, wait0 = _split_copies(f"pairx_l{l}_g{gi}", gs, lands, len(names), _pair_plan)

        def sums_and_scatter(after):
            mine, theirs = wait0(after)
            sums = [_pair_sum(f"pair_sum_{n}_l{l}", g[None], r[None], cc, me_chip) for n, g, r in zip(names, mine, theirs)]
            pbs = [p[0] for p, _ in sums]
            lands1 = [lax.empty((3,) + p.shape[1:], BF16) for p in pbs]
            tok1, wait1 = _split_copies(f"scatter_l{l}_g{gi}", pbs, lands1, 3 * len(names), _scatter_plan)

            def chip_sums_and_halves(after):
                halves = [_chip_sum(f"chip_sum_{n}_l{l}", own, r[:, None], cc)[0] for n, (_, own), r in zip(names, sums, wait1(after)[1])]
                tok2, wait2 = _split_copies(f"halfx_l{l}_g{gi}", [], halves, len(names), _half_plan)

                def update(after):
                    for n, f in zip(names, wait2(after)[1]):
                        out_bufs[n] = _adamw(f"adamw_{n}_l{l}", l, opt_view[n](W[n]), f, opt_view[n](M[n]), opt_view[n](V[n]), out_bufs[n])
                    return None, None

                return update, tok2

            return chip_sums_and_halves, tok1

        return sums_and_scatter, tok0

    def advance(stages, new):
        nxt, tok = new
        out, toks = [nxt], tok
        for stage in stages:
            nxt, tok = stage(toks)
            if nxt is not None:
                out.append(nxt)
                toks = toks + tok
        return out, toks

    stages, order_after = [], None
    for l in reversed(range(n_layers)):
        tag = f"l{l}"
        s = saved[l]
        wi, wg, wmkv, wb, wo, wfi, wfo = s["w"]
        du = _mm_nt(f"mm_ffn_out_dx_{tag}", dx, wfo, 0, after=order_after)
        g_wfo = _mm_tn(f"mm_ffn_out_dw_{tag}", s["u"], dx, 1)
        dag3 = _swiglu_bwd(f"swiglu_bwd_{tag}", s["ag3"], du)
        dh2 = _mm_nt(f"mm_ffn_in_dx_{tag}", dag3, wfi, 0)
        g_wfi = _mm_tn(f"mm_ffn_in_dw_{tag}", s["h2"], dag3, N_CHIPS)
        dx2, dg = _rms_bwd(f"rms_ffn_bwd_{tag}", s["x2"], lane(norm_ffn_g[l]), [dh2], add=dx)
        gsmall["norm_ffn_g"][l] = dg[0]
        stages, tok = advance(stages, reduce_group(l, 0, {"w_ffn_in": g_wfi, "w_ffn_out": g_wfo.reshape(N_CHIPS, -1, d_model)}))
        dmerged = _mm_nt(f"mm_out_dx_{tag}", dx2, wo, 0, after=tok)
        g_wo = _mm_tn(f"mm_out_dw_{tag}", s["merged"], dx2, 1)
        dgp3, dbr, g_wb = _gate_bwd(f"gate_bwd_{tag}", s["gp3"], s["br"], wb, 0, dmerged)
        g_wg = _mm_tn(f"mm_gate_dw_{tag}", s["h"], dgp3, N_CHIPS)
        stages, tok = advance(stages, reduce_group(l, 1, {
            "w_gate": g_wg, "w_branch": g_wb.reshape(N_CHIPS, NH * BW, BW), "w_out": g_wo.reshape(N_CHIPS, -1, d_model)}))
        dh_a = _mm_nt(f"mm_gate_dx_{tag}", dgp3, wg, 0, after=tok)
        drq, drk, drv, drg, ddf, ddb, dgn = _ret_bwd(tag, s["proj"], cos, sin, s["decf"], s["decb"], s["gn"], s["ret_saved"], dbr[0])
        gsmall["ret_decay_fwd"][l], gsmall["ret_decay_bwd"][l], gsmall["ret_norm_g"][l] = _fold_lanes(ddf), _fold_lanes(ddb), dgn[0]
        dpool_p = jnp.pad(dbr[1], ((POOL_PAD, POOL_PAD), (0, 0)))
        dpvp, dwbd, dps = _pool_bwd(tag, s["pvp"], s["wbd"], s["pscale"], t_len, dpool_p)
        dpv = dpvp[POOL_PAD:POOL_PAD + t_len]
        gsmall["pool_w"][l] = jnp.stack([dwbd[g * HEAD:(g + 1) * HEAD, g * HEAD:(g + 1) * HEAD] for g in range(NH)])
        gsmall["pool_scale"][l] = dps[0]
        dnq, dkn, dnv, dbtabs[l], dgq = _na_bwd(tag, s["proj"], s["kn"], s["vb"], s["btab"], s["nagq"], dbr[2])
        dnk, dnv, dgk = _na_keys_bwd(tag, s["proj"], s["nagk"], dkn, dnv)
        gsmall["na_q_norm_g"][l], gsmall["na_k_norm_g"][l] = _fold_heads(dgq), _fold_heads(dgk)
        dmq, dmemkv, dgq, dgk = _mem_bwd(tag, s["proj"], s["memkv"], s["mgq"], s["mgk"], dbr[3])
        gsmall["mem_q_norm_g"][l], gsmall["mem_k_norm_g"][l] = _fold_heads(dgq), _fold_heads(dgk)
        g_wmkv = _mm_tn(f"mm_memkv_dw_{tag}", s["hm"], dmemkv, 1)
        dhm = _mm_nt(f"mm_memkv_dx_{tag}", dmemkv, wmkv, 0)
        _, dg = _rms_bwd(f"rms_mem_bwd_{tag}", mem2d, lane(norm_mem_g[l]), [dhm])
        gsmall["norm_mem_g"][l] = dg[0]
        dproj = jnp.concatenate([drq, drk, drv, drg, dpv, dnq, dnk, dnv, dmq], axis=1)
        g_wi = _mm_tn(f"mm_in_dw_{tag}", dproj, s["h"], 1)
        dh_b = _mm_nt(f"mm_in_dx_{tag}", dproj, wi, 0)
        dx, dg = _rms_bwd(f"rms_mix_bwd_{tag}", s["x"], lane(norm_mix_g[l]), [dh_a, dh_b], add=dx2)
        gsmall["norm_mix_g"][l] = dg[0]

        stages, order_after = advance(stages, reduce_group(l, 2, {
            "w_in": g_wi.reshape(N_CHIPS, -1, d_model), "w_mem_kv": g_wmkv.reshape(N_CHIPS, -1, 2 * BW)}))
    out_g, out_d, out_m, out_v = {}, {}, {}, {}
    gsmall["na_rpb"] = list(_na_bias_grad(jnp.stack(dbtabs)))
    small_g = [jnp.stack(gsmall[n]).reshape(W[n].shape) for n in SMALL] + [loss_part]
    like = [W[n] for n in SMALL] + [loss_part]
    zero = jnp.zeros((1, 1), F32)
    parts = _gather_small(_pack(small_g))
    packed = _small_update(parts, _pack([W[n] for n in SMALL] + [zero]), _pack([M[n] for n in SMALL] + [zero]),
                           _pack([V[n] for n in SMALL] + [zero]))
    sg, sd, sm, sv = [_unpack(p, like) for p in packed]
    for i, n in enumerate(SMALL):
        out_g[n], out_d[n], out_m[n], out_v[n] = sg[i], sd[i], sm[i], sv[i]
    loss = sg[-1].reshape(())

    while stages:
        stages = [nxt for nxt, _ in (stage(packed[0]) for stage in stages) if nxt is not None]

    for n in BIG:
        shp = W[n].shape
        back = (lambda b: jnp.swapaxes(b, 1, 2)) if n == "w_in" else (lambda b: b)
        out_g[n], out_d[n], out_m[n], out_v[n] = [back(b).reshape(shp) for b in out_bufs[n]]

    return (loss, dx.reshape(x.shape), *[out_g[n] for n in ORDER], *[out_d[n] for n in ORDER],
            *[out_m[n] for n in ORDER], *[out_v[n] for n in ORDER])
```

```python
import functools

import numpy as np
import jax
import jax.numpy as jnp
from jax import lax
from jax.experimental import pallas as pl
from jax.experimental.pallas import tpu as pltpu

F32 = jnp.float32
BF16 = jnp.bfloat16
SDS = jax.ShapeDtypeStruct
MESH = pl.DeviceIdType.MESH
ANY = pl.BlockSpec(memory_space=pl.ANY)

HEAD = 64
NH = 4
BW = NH * HEAD
CH = 128
GRID_W = 64
NA_ROWS = 8
NA_COLS = 16
POOL_HALF = (1, 2, 4, 8)
POOL_PAD = 16
ROPE_THETA = 10000.0
EPS = 1e-6
NEG_INF = -1e30
N_CHIPS = 4
N_DEV = 8
LANES = 128

ADAM_LR = 0.001
ADAM_B1 = 0.9
ADAM_B2 = 0.999
ADAM_EPS = 1e-08
ADAM_WD = 0.01
ADAM_STEP = 10

VMEM_BIG = 56 << 20
VMEM_MM = 44 << 20


def _cp(vmem=None):
    return pltpu.CompilerParams(vmem_limit_bytes=vmem) if vmem else None


def _d(a, b, ca, cb):
    return lax.dot_general(a.astype(BF16), b.astype(BF16), (((ca,), (cb,)), ((), ())),
                           preferred_element_type=F32)


@jax.custom_vjp
def _nn(a, b):
    return _d(a, b, 1, 0)


def _nn_f(a, b):
    return _d(a, b, 1, 0), (a, b)


def _nn_b(r, g):
    a, b = r
    return _d(g, b, 1, 1).astype(a.dtype), _d(a, g, 0, 0).astype(b.dtype)


_nn.defvjp(_nn_f, _nn_b)


@jax.custom_vjp
def _nt(a, b):
    return _d(a, b, 1, 1)


def _nt_f(a, b):
    return _d(a, b, 1, 1), (a, b)


def _nt_b(r, g):
    a, b = r
    return _d(g, b, 1, 0).astype(a.dtype), _d(g, a, 0, 0).astype(b.dtype)


_nt.defvjp(_nt_f, _nt_b)


@jax.custom_vjp
def _tn(a, b):
    return _d(a, b, 0, 0)


def _tn_f(a, b):
    return _d(a, b, 0, 0), (a, b)


def _tn_b(r, g):
    a, b = r
    return _d(b, g, 1, 1).astype(a.dtype), _d(a, g, 1, 0).astype(b.dtype)


_tn.defvjp(_tn_f, _tn_b)


@functools.partial(jax.custom_vjp, nondiff_argnums=(1,))
def _rollr(x, s):
    return pltpu.roll(x, s % x.shape[0], 0)


def _rollr_f(x, s):
    return _rollr(x, s), None


def _rollr_b(s, _, g):
    return (_rollr(g, -s),)


_rollr.defvjp(_rollr_f, _rollr_b)


@jax.custom_vjp
def _swap32(t):
    n = t.shape[1]
    lane = lax.broadcasted_iota(jnp.int32, (1, n), 1)
    return jnp.where((lane & (HEAD // 2)) == 0, pltpu.roll(t, n - HEAD // 2, 1), pltpu.roll(t, HEAD // 2, 1))


def _swap32_f(t):
    return _swap32(t), None


def _swap32_b(_, g):
    return (_swap32(g),)


_swap32.defvjp(_swap32_f, _swap32_b)


def _head_masks():
    lane = lax.broadcasted_iota(jnp.int32, (1, BW), 1)
    return [(lane >= HEAD * h) & (lane < HEAD * (h + 1)) for h in range(NH)]


def _head_rms(x, g):
    out = jnp.zeros_like(x)
    for mh in _head_masks():
        ms = jnp.sum(jnp.where(mh, x * x, 0.0), axis=-1, keepdims=True) * (1.0 / HEAD)
        out = out + jnp.where(mh, x * lax.rsqrt(ms + EPS), 0.0)
    return out * g


def _stack_heads(x):
    return jnp.concatenate([jnp.where(mh, x, 0.0) for mh in _head_masks()], axis=0)


def _unstack_heads(y):
    n = y.shape[0] // NH
    out = None
    for h, mh in enumerate(_head_masks()):
        t = jnp.where(mh, y[h * n:(h + 1) * n], 0.0)
        out = t if out is None else out + t
    return out


def _rms(x, g):
    return x * lax.rsqrt(jnp.mean(x * x, axis=-1, keepdims=True) + EPS) * g


def _rot(t, cos, sin):
    return t * cos + _swap32(t) * sin


def _softmax(s):
    e = jnp.exp(s - lax.stop_gradient(jnp.max(s, axis=-1, keepdims=True)))
    return e / jnp.sum(e, axis=-1, keepdims=True)


def _ret_kv_f(k, v, cos, sin, decf, decb):
    lgf, lgb = jax.nn.log_sigmoid(decf), jax.nn.log_sigmoid(decb)
    kr = _rot(k, cos, sin)
    idx = lax.broadcasted_iota(jnp.int32, (CH, 1), 0).astype(F32)
    r = lax.broadcasted_iota(jnp.int32, (BW, BW), 0) // HEAD
    c = lax.broadcasted_iota(jnp.int32, (BW, BW), 1) // HEAD
    bd = r == c
    kvf = jnp.where(bd, _tn(kr * jnp.exp((CH - 1 - idx) * lgf), v), 0.0)
    kvb = jnp.where(bd, _tn(kr * jnp.exp(idx * lgb), v), 0.0)
    return kvf, kvb


def _ret_scan_f(kvf, kvb, decf, decb):
    n = len(kvf)
    cdf = jnp.exp(CH * jax.nn.log_sigmoid(decf))
    cdb = jnp.exp(CH * jax.nn.log_sigmoid(decb))
    z = jnp.zeros((BW, BW), F32)
    sf, st = [z], z
    for a in range(n - 1):
        st = cdf * st + kvf[a]
        sf.append(st)
    sb, st = [z], z
    for a in range(n - 1, 0, -1):
        st = cdb * st + kvb[a]
        sb.append(st)
    return sf, sb[::-1]


def _ret_out_f(q, k, v, g, cos, sin, stf, stb, decf, decb, gn):
    lgf, lgb = jax.nn.log_sigmoid(decf), jax.nn.log_sigmoid(decb)
    qr = _rot(q, cos, sin) * (HEAD ** -0.5)
    kr = _rot(k, cos, sin)
    diff = (lax.broadcasted_iota(jnp.int32, (CH, CH), 0) - lax.broadcasted_iota(jnp.int32, (CH, CH), 1)).astype(F32)
    decay = []
    for mh in _head_masks():
        lf = jnp.sum(jnp.where(mh, lgf, 0.0), axis=1, keepdims=True) * (1.0 / HEAD)
        lb = jnp.sum(jnp.where(mh, lgb, 0.0), axis=1, keepdims=True) * (1.0 / HEAD)
        decay.append(jnp.exp(jnp.where(diff >= 0, diff * lf, -diff * lb)))
    p = _nt(_stack_heads(qr), kr) * jnp.concatenate(decay, axis=0)
    o = _unstack_heads(_nn(p, v))
    idx = lax.broadcasted_iota(jnp.int32, (CH, 1), 0).astype(F32)
    o = o + _nn(qr * jnp.exp((idx + 1) * lgf), stf) + _nn(qr * jnp.exp((CH - idx) * lgb), stb)
    return _head_rms(o, gn) * (g * jax.nn.sigmoid(g))


def _pool_f(pvp, wbd, scale, t_len):
    n = pvp.shape[0]
    w2 = pvp + _rollr(pvp, 1)
    w4 = _rollr(w2, 1) + _rollr(w2, -1)
    w8 = _rollr(w4, 2) + _rollr(w4, -2)
    w16 = _rollr(w8, 4) + _rollr(w8, -4)
    grp = lax.broadcasted_iota(jnp.int32, (1, BW), 1) // HEAD
    ws = jnp.where(grp == 0, w2, jnp.where(grp == 1, w4, jnp.where(grp == 2, w8, w16)))
    half = jnp.where(grp == 0, POOL_HALF[0], jnp.where(grp == 1, POOL_HALF[1], jnp.where(grp == 2, POOL_HALF[2], POOL_HALF[3])))
    t = lax.broadcasted_iota(jnp.int32, (n, 1), 0) - POOL_PAD
    cnt = jnp.minimum(t + half, t_len) - jnp.maximum(t - half, 0)
    cnt = jnp.where((t >= 0) & (t < t_len), cnt, 1).astype(F32)
    pooled = ws / cnt - pvp
    return _nn(pooled, wbd) * scale


def _na_f(q, kn, vw, bias, gq):
    qn = _head_rms(q, gq)
    bias_all = jnp.concatenate([jnp.concatenate(bias[h], axis=1) for h in range(NH)], axis=0)
    s = _nt(_stack_heads(qn), kn) * (HEAD ** -0.5) + bias_all
    return _unstack_heads(_nn(_softmax(s), vw))


def _mem_f(q, mk, mv, gq, gk):
    qn = _head_rms(q, gq)
    kn = _head_rms(mk, gk)
    s = _nt(_stack_heads(qn), kn) * (HEAD ** -0.5)
    return _unstack_heads(_nn(_softmax(s), mv))


def _gate_f(gp, br, wb):
    out = None
    for n in range(NH):
        t = jax.nn.sigmoid(gp[n]) * _nn(br[n], wb[n])
        out = t if out is None else out + t
    return out


def _swiglu_f(a, g):
    return a * jax.nn.sigmoid(a) * g


MM_ROWS = (1024, 512, 256)
MM_COLS = (1152, 1024, 768, 512)


def _tile(n, prefs):
    for p in prefs:
        if n % p == 0:
            return p
    return n


def _mm_call(name, a, b, extra, dn, grid, a_spec, b_spec, extra_specs, o_spec, out_shape, acc_shape, nred, after=None):
    add = len(extra) == 1
    if after is not None:
        extra, extra_specs = [*extra, after], [*extra_specs, ANY]
    n_extra = len(extra)
    n_steps = int(np.prod([grid[len(grid) - 1 - i] for i in range(nred)]))

    def body(*refs):
        a_ref, b_ref = refs[0], refs[1]
        o_ref, acc = refs[2 + n_extra], refs[-1]
        red = [len(grid) - 1 - i for i in range(nred)]
        first = functools.reduce(jnp.logical_and, [pl.program_id(ax) == 0 for ax in red])
        last = functools.reduce(jnp.logical_and, [pl.program_id(ax) == grid[ax] - 1 for ax in red])
        d = lax.dot_general(a_ref[...].astype(BF16), b_ref[...].astype(BF16), dn, preferred_element_type=F32)

        def finish(r):
            if add:
                r = r + refs[2][...]
            o_ref[...] = r.astype(o_ref.dtype)

        if n_steps == 1:
            finish(d)
        else:
            @pl.when(first)
            def _():
                acc[...] = d

            @pl.when(jnp.logical_not(first) & jnp.logical_not(last))
            def _():
                acc[...] += d

            @pl.when(last)
            def _():
                finish(acc[...] + d)

    return pl.pallas_call(
        body, grid=grid, in_specs=[a_spec, b_spec, *extra_specs], out_specs=o_spec, out_shape=out_shape,
        scratch_shapes=[pltpu.VMEM(acc_shape, F32)] if n_steps > 1 else [], name=name, compiler_params=_cp(VMEM_MM),
    )(a, b, *extra)


def _mm_nn(name, a, w, l, *, out3d=False, add=None, out_dtype=F32):
    m, k = a.shape
    _, s_n, _, ns = w.shape
    tm = _tile(m, MM_ROWS)
    tk = _tile(k, (1024, 1408, 512, 256))
    tn = ns if ns <= 1408 else _tile(ns, MM_COLS)
    grid = (m // tm, s_n, ns // tn, k // tk)
    a_spec = pl.BlockSpec((tm, tk), lambda i, s, j, kk: (i, kk))
    b_spec = pl.BlockSpec((None, None, tk, tn), lambda i, s, j, kk: (l, s, kk, j))
    if out3d:
        o_spec = pl.BlockSpec((None, tm, tn), lambda i, s, j, kk: (s, i, j))
        out_shape = SDS((s_n, m, ns), out_dtype)
    else:
        nj = ns // tn
        o_spec = pl.BlockSpec((tm, tn), lambda i, s, j, kk: (i, s * nj + j))
        out_shape = SDS((m, s_n * ns), out_dtype)
    extra, especs = [], []
    if add is not None:
        assert not out3d
        nj = ns // tn
        extra, especs = [add], [pl.BlockSpec((tm, tn), lambda i, s, j, kk: (i, s * nj + j))]
    return _mm_call(name, a, w, extra, (((1,), (0,)), ((), ())), grid, a_spec, b_spec, especs, o_spec, out_shape, (tm, tn), 1)


def _mm_nt(name, g, w, l, after=None):
    _, s_n, k, ns = w.shape
    g3d = g.ndim == 3
    m = g.shape[1] if g3d else g.shape[0]
    tm = _tile(m, MM_ROWS)
    to = _tile(k, (1024, 1408, 512, 256))
    tk = ns if (ns % LANES or ns <= 1408) else _tile(ns, MM_COLS)
    nks = ns // tk
    grid = (m // tm, k // to, s_n, nks)
    if g3d:
        a_spec = pl.BlockSpec((None, tm, tk), lambda i, j, s, r: (s, i, r))
    else:
        a_spec = pl.BlockSpec((tm, tk), lambda i, j, s, r: (i, s * nks + r))
    b_spec = pl.BlockSpec((None, None, to, tk), lambda i, j, s, r: (l, s, j, r))
    o_spec = pl.BlockSpec((tm, to), lambda i, j, s, r: (i, j))
    return _mm_call(name, g, w, [], (((1,), (1,)), ((), ())), grid, a_spec, b_spec, [], o_spec, SDS((m, k), F32), (tm, to), 2,
                    after=after)


def _mm_tn(name, a, g, s_n):
    t_len, k = a.shape
    g3d = g.ndim == 3
    ns = g.shape[2] if g3d else g.shape[1] // s_n
    tt = _tile(t_len, (2048, 1024, 512))
    to = _tile(k, (1024, 768, 1408, 512, 256, 128))
    tn = ns
    nj = ns // tn
    grid = (s_n, k // to, nj, t_len // tt)
    a_spec = pl.BlockSpec((tt, to), lambda s, i, j, t: (t, i))
    if g3d:
        b_spec = pl.BlockSpec((None, tt, tn), lambda s, i, j, t: (s, t, j))
    else:
        b_spec = pl.BlockSpec((tt, tn), lambda s, i, j, t: (t, s * nj + j))
    o_spec = pl.BlockSpec((None, to, tn), lambda s, i, j, t: (s, i, j))
    return _mm_call(name, a, g, [], (((0,), (0,)), ((), ())), grid, a_spec, b_spec, [], o_spec,
                    SDS((s_n, k, ns), F32), (to, tn), 1)


def _rms_fwd(name, x, g):
    m, d = x.shape
    tm = _tile(m, (512, 256))

    def body(x_ref, g_ref, o_ref):
        o_ref[...] = _rms(x_ref[...], g_ref[...]).astype(BF16)

    row = pl.BlockSpec((tm, d), lambda i: (i, 0))
    return pl.pallas_call(body, grid=(m // tm,), in_specs=[row, pl.BlockSpec((1, d), lambda i: (0, 0))], out_specs=row,
                          out_shape=SDS((m, d), BF16), name=name, compiler_params=_cp(VMEM_MM))(x, g)


def _rms_bwd(name, x, g, dhs, add=None):
    m, d = x.shape
    tm = _tile(m, (512, 256))
    nh = len(dhs)

    def body(*refs):
        x_ref, g_ref = refs[0], refs[1]
        dh = refs[2][...]
        for r in refs[3:2 + nh]:
            dh = dh + r[...]
        dx_ref, dg_ref = refs[-2], refs[-1]
        _, vjp = jax.vjp(_rms, x_ref[...], g_ref[...])
        dx, dg = vjp(dh)
        if add is not None:
            dx = dx + refs[2 + nh][...]
        dx_ref[...] = dx

        @pl.when(pl.program_id(0) == 0)
        def _():
            dg_ref[...] = jnp.zeros_like(dg_ref)

        dg_ref[...] += dg

    row = pl.BlockSpec((tm, d), lambda i: (i, 0))
    vec = pl.BlockSpec((1, d), lambda i: (0, 0))
    ins = [x, g, *dhs] + ([add] if add is not None else [])
    return pl.pallas_call(body, grid=(m // tm,), in_specs=[row, vec] + [row] * (len(ins) - 2), out_specs=[row, vec],
                          out_shape=[SDS((m, d), F32), SDS((1, d), F32)], name=name, compiler_params=_cp(VMEM_MM))(*ins)


def _swiglu_fwd(name, ag3):
    _, t_len, w = ag3.shape
    tm = _tile(t_len, (512, 256))

    def body(a_ref, g_ref, o_ref):
        o_ref[...] = _swiglu_f(a_ref[...], g_ref[...]).astype(BF16)

    return pl.pallas_call(
        body, grid=(t_len // tm, 2),
        in_specs=[pl.BlockSpec((None, tm, w), lambda i, j: (j, i, 0)), pl.BlockSpec((None, tm, w), lambda i, j: (j + 2, i, 0))],
        out_specs=pl.BlockSpec((tm, w), lambda i, j: (i, j)), out_shape=SDS((t_len, 2 * w), BF16), name=name)(ag3, ag3)


def _swiglu_bwd(name, ag3, du):
    _, t_len, w = ag3.shape
    tm = _tile(t_len, (512, 256))

    def body(a_ref, g_ref, du_ref, o_ref):
        _, vjp = jax.vjp(_swiglu_f, a_ref[...], g_ref[...])
        da, dg = vjp(du_ref[...])
        o_ref[0] = da.astype(BF16)
        o_ref[1] = dg.astype(BF16)

    out = pl.pallas_call(
        body, grid=(t_len // tm, 2),
        in_specs=[pl.BlockSpec((None, tm, w), lambda i, j: (j, i, 0)), pl.BlockSpec((None, tm, w), lambda i, j: (j + 2, i, 0)),
                  pl.BlockSpec((tm, w), lambda i, j: (i, j))],
        out_specs=pl.BlockSpec((2, None, tm, w), lambda i, j: (0, j, i, 0)), out_shape=SDS((2, 2, t_len, w), BF16), name=name)(ag3, ag3, du)
    return out.reshape(ag3.shape)


def _gate_fwd(name, gp3, br, wb, l):
    _, t_len, d = gp3.shape
    tm = _tile(t_len, MM_ROWS)

    def body(gp_ref, br_ref, wb_ref, o_ref):
        o_ref[...] = _gate_f([gp_ref[n] for n in range(NH)], [br_ref[n] for n in range(NH)],
                             [wb_ref[n] for n in range(NH)]).astype(BF16)

    return pl.pallas_call(
        body, grid=(t_len // tm, d // BW),
        in_specs=[pl.BlockSpec((NH, tm, BW), lambda i, s: (0, i, s)), pl.BlockSpec((NH, tm, BW), lambda i, s: (0, i, 0)),
                  pl.BlockSpec((None, None, NH, BW, BW), lambda i, s: (l, s, 0, 0, 0))],
        out_specs=pl.BlockSpec((tm, BW), lambda i, s: (i, s)), out_shape=SDS((t_len, d), BF16), name=name,
        compiler_params=_cp(VMEM_MM))(gp3, br, wb)


def _gate_bwd(name, gp3, br, wb, l, dmerged):
    _, t_len, d = gp3.shape
    tm = _tile(t_len, (512, 256))
    ns = d // BW

    def body(gp_ref, br_ref, wb_ref, dm_ref, dgp_ref, dbr_ref, dwb_ref):
        i, s = pl.program_id(0), pl.program_id(1)
        gp = [gp_ref[n] for n in range(NH)]
        brv = [br_ref[n].astype(F32) for n in range(NH)]
        wbv = [wb_ref[n].astype(F32) for n in range(NH)]
        _, vjp = jax.vjp(_gate_f, gp, brv, wbv)
        dgp, dbr, dwb = vjp(dm_ref[...])

        @pl.when(s == 0)
        def _():
            dbr_ref[...] = jnp.zeros_like(dbr_ref)

        @pl.when((i == 0) & (s == 0))
        def _():
            dwb_ref[...] = jnp.zeros_like(dwb_ref)

        for n in range(NH):
            dgp_ref[n] = dgp[n].astype(BF16)
            dbr_ref[n] += dbr[n]
            dwb_ref[s, n] += dwb[n]

    return pl.pallas_call(
        body, grid=(t_len // tm, ns),
        in_specs=[pl.BlockSpec((NH, tm, BW), lambda i, s: (0, i, s)), pl.BlockSpec((NH, tm, BW), lambda i, s: (0, i, 0)),
                  pl.BlockSpec((None, None, NH, BW, BW), lambda i, s: (l, s, 0, 0, 0)), pl.BlockSpec((tm, BW), lambda i, s: (i, s))],
        out_specs=[pl.BlockSpec((NH, tm, BW), lambda i, s: (0, i, s)), pl.BlockSpec((NH, tm, BW), lambda i, s: (0, i, 0)),
                   pl.BlockSpec((ns, NH, BW, BW), lambda i, s: (0, 0, 0, 0))],
        out_shape=[SDS(gp3.shape, BF16), SDS((NH, t_len, BW), F32), SDS((ns, NH, BW, BW), F32)], name=name,
        compiler_params=_cp(VMEM_MM))(gp3, br, wb, dmerged)


def _loss_kernel(name, y, tgt):
    m, d = y.shape
    tm = _tile(m, (256,))

    def body(y_ref, t_ref, l_ref, dy_ref):
        err = y_ref[...] - t_ref[...]
        dy_ref[...] = err * (1.0 / d)

        @pl.when(pl.program_id(0) == 0)
        def _():
            l_ref[...] = jnp.zeros_like(l_ref)

        l_ref[...] += 0.5 * jnp.sum(jnp.mean(err * err, axis=-1, keepdims=True), axis=0, keepdims=True)

    row = pl.BlockSpec((tm, d), lambda i: (i, 0))
    return pl.pallas_call(body, grid=(m // tm,), in_specs=[row, row], out_specs=[pl.BlockSpec((1, 1), lambda i: (0, 0)), row],
                          out_shape=[SDS((1, 1), F32), SDS((m, d), F32)], name=name)(y, tgt)


def _zero_first(refs):
    @pl.when(pl.program_id(0) == 0)
    def _():
        for r in refs:
            r[...] = jnp.zeros_like(r)


def _grp(rows, j):
    return pl.BlockSpec((rows, BW), lambda a, j=j: (a, j))


_VEC = pl.BlockSpec((1, BW), lambda a: (0, 0))


def _ret_fwd(tag, proj, cos, sin, decf, decb, gn):
    t_len = proj.shape[0]
    n = t_len // CH
    tab = pl.BlockSpec((CH, BW), lambda a: (a, 0))
    st = pl.BlockSpec((None, BW, BW), lambda a: (a, 0, 0))

    def k1(k_ref, v_ref, cos_ref, sin_ref, df_ref, db_ref, kvf_ref, kvb_ref):
        kvf_ref[...], kvb_ref[...] = _ret_kv_f(k_ref[...], v_ref[...], cos_ref[...], sin_ref[...], df_ref[...], db_ref[...])

    kvf, kvb = pl.pallas_call(
        k1, grid=(n,), in_specs=[_grp(CH, 1), _grp(CH, 2), tab, tab, _VEC, _VEC], out_specs=[st, st],
        out_shape=[SDS((n, BW, BW), F32)] * 2, name=f"ret_kv_{tag}")(proj, proj, cos, sin, decf, decb)

    def k2(kvf_ref, kvb_ref, df_ref, db_ref, sf_ref, sb_ref):
        sf, sb = _ret_scan_f([kvf_ref[a] for a in range(n)], [kvb_ref[a] for a in range(n)], df_ref[...], db_ref[...])
        for a in range(n):
            sf_ref[a] = sf[a]
            sb_ref[a] = sb[a]

    stf, stb = pl.pallas_call(k2, out_shape=[SDS((n, BW, BW), F32)] * 2, name=f"ret_scan_{tag}",
                              compiler_params=_cp(VMEM_BIG))(kvf, kvb, decf, decb)

    def k3(q_ref, k_ref, v_ref, g_ref, cos_ref, sin_ref, sf_ref, sb_ref, df_ref, db_ref, gn_ref, o_ref):
        o_ref[...] = _ret_out_f(q_ref[...], k_ref[...], v_ref[...], g_ref[...], cos_ref[...], sin_ref[...], sf_ref[...],
                                sb_ref[...], df_ref[...], db_ref[...], gn_ref[...]).astype(BF16)

    ret = pl.pallas_call(
        k3, grid=(n,), in_specs=[_grp(CH, 0), _grp(CH, 1), _grp(CH, 2), _grp(CH, 3), tab, tab, st, st, _VEC, _VEC, _VEC],
        out_specs=tab, out_shape=SDS((t_len, BW), BF16), name=f"ret_out_{tag}")(proj, proj, proj, proj, cos, sin, stf, stb, decf, decb, gn)
    return ret, (kvf, kvb, stf, stb)


def _ret_bwd(tag, proj, cos, sin, decf, decb, gn, saved, dret):
    kvf, kvb, stf, stb = saved
    t_len = proj.shape[0]
    n = t_len // CH
    tab = pl.BlockSpec((CH, BW), lambda a: (a, 0))
    st = pl.BlockSpec((None, BW, BW), lambda a: (a, 0, 0))

    def k3(q_ref, k_ref, v_ref, g_ref, cos_ref, sin_ref, sf_ref, sb_ref, df_ref, db_ref, gn_ref, do_ref,
           dq_ref, dk_ref, dv_ref, dg_ref, dsf_ref, dsb_ref, ddf_ref, ddb_ref, dgn_ref):
        cos, sin = cos_ref[...], sin_ref[...]
        f = lambda q, k, v, g, sf, sb, df, db, gnv: _ret_out_f(q, k, v, g, cos, sin, sf, sb, df, db, gnv)
        _, vjp = jax.vjp(f, q_ref[...], k_ref[...], v_ref[...], g_ref[...], sf_ref[...], sb_ref[...], df_ref[...], db_ref[...], gn_ref[...])
        dq, dk, dv, dg, dsf, dsb, ddf, ddb, dgn = vjp(do_ref[...])
        dq_ref[...] = dq.astype(BF16)
        dk_ref[...] = dk
        dv_ref[...] = dv
        dg_ref[...] = dg.astype(BF16)
        dsf_ref[...] = dsf
        dsb_ref[...] = dsb
        _zero_first([ddf_ref, ddb_ref, dgn_ref])
        ddf_ref[...] += ddf
        ddb_ref[...] += ddb
        dgn_ref[...] += dgn

    dq, dk3, dv3, dg, dstf, dstb, ddf3, ddb3, dgn = pl.pallas_call(
        k3, grid=(n,),
        in_specs=[_grp(CH, 0), _grp(CH, 1), _grp(CH, 2), _grp(CH, 3), tab, tab, st, st, _VEC, _VEC, _VEC, tab],
        out_specs=[tab, tab, tab, tab, st, st, _VEC, _VEC, _VEC],
        out_shape=[SDS((t_len, BW), BF16), SDS((t_len, BW), F32), SDS((t_len, BW), F32), SDS((t_len, BW), BF16),
                   SDS((n, BW, BW), F32), SDS((n, BW, BW), F32), SDS((1, BW), F32), SDS((1, BW), F32), SDS((1, BW), F32)],
        name=f"ret_out_bwd_{tag}")(proj, proj, proj, proj, cos, sin, stf, stb, decf, decb, gn, dret)

    def k2(kvf_ref, kvb_ref, df_ref, db_ref, dsf_ref, dsb_ref, dkvf_ref, dkvb_ref, ddf_ref, ddb_ref):
        _, vjp = jax.vjp(_ret_scan_f, [kvf_ref[a] for a in range(n)], [kvb_ref[a] for a in range(n)], df_ref[...], db_ref[...])
        dkvf, dkvb, ddf_ref[...], ddb_ref[...] = vjp(([dsf_ref[a] for a in range(n)], [dsb_ref[a] for a in range(n)]))
        for a in range(n):
            dkvf_ref[a] = dkvf[a]
            dkvb_ref[a] = dkvb[a]

    dkvf, dkvb, ddf2, ddb2 = pl.pallas_call(
        k2, out_shape=[SDS((n, BW, BW), F32), SDS((n, BW, BW), F32), SDS((1, BW), F32), SDS((1, BW), F32)],
        name=f"ret_scan_bwd_{tag}", compiler_params=_cp(VMEM_BIG))(kvf, kvb, decf, decb, dstf, dstb)

    def k1(k_ref, v_ref, cos_ref, sin_ref, df_ref, db_ref, dkvf_ref, dkvb_ref, dk3_ref, dv3_ref, dk_ref, dv_ref, ddf_ref, ddb_ref):
        cos, sin = cos_ref[...], sin_ref[...]
        f = lambda k, v, df, db: _ret_kv_f(k, v, cos, sin, df, db)
        _, vjp = jax.vjp(f, k_ref[...], v_ref[...], df_ref[...], db_ref[...])
        dk, dv, ddf, ddb = vjp((dkvf_ref[...], dkvb_ref[...]))
        dk_ref[...] = (dk + dk3_ref[...]).astype(BF16)
        dv_ref[...] = (dv + dv3_ref[...]).astype(BF16)
        _zero_first([ddf_ref, ddb_ref])
        ddf_ref[...] += ddf
        ddb_ref[...] += ddb

    dk, dv, ddf1, ddb1 = pl.pallas_call(
        k1, grid=(n,), in_specs=[_grp(CH, 1), _grp(CH, 2), tab, tab, _VEC, _VEC, st, st, tab, tab],
        out_specs=[tab, tab, _VEC, _VEC],
        out_shape=[SDS((t_len, BW), BF16), SDS((t_len, BW), BF16), SDS((1, BW), F32), SDS((1, BW), F32)],
        name=f"ret_kv_bwd_{tag}")(proj, proj, cos, sin, decf, decb, dkvf, dkvb, dk3, dv3)
    return dq, dk, dv, dg, ddf1 + ddf2 + ddf3, ddb1 + ddb2 + ddb3, dgn


def _pool_fwd(tag, pvp, wbd, scale, t_len):
    def body(p_ref, w_ref, s_ref, o_ref):
        o_ref[...] = _pool_f(p_ref[...], w_ref[...], s_ref[...], t_len).astype(BF16)

    return pl.pallas_call(body, out_shape=SDS(pvp.shape, BF16), name=f"pool_{tag}", compiler_params=_cp(VMEM_BIG))(pvp, wbd, scale)


def _pool_bwd(tag, pvp, wbd, scale, t_len, dpool_p):
    def body(p_ref, w_ref, s_ref, do_ref, dp_ref, dw_ref, ds_ref):
        f = lambda p, w, s: _pool_f(p, w, s, t_len)
        _, vjp = jax.vjp(f, p_ref[...], w_ref[...], s_ref[...])
        dp, dw, ds = vjp(do_ref[...])
        dp_ref[...] = dp.astype(BF16)
        dw_ref[...] = dw
        ds_ref[...] = ds

    return pl.pallas_call(body, out_shape=[SDS(pvp.shape, BF16), SDS((BW, BW), F32), SDS((1, BW), F32)],
                          name=f"pool_bwd_{tag}", compiler_params=_cp(VMEM_BIG))(pvp, wbd, scale, dpool_p)


def _na_window(r, rows):
    r0 = jnp.clip(r - NA_ROWS // 2, 0, rows - NA_ROWS)
    return r0, r - r0


NA_PAIRS = 2 * NA_ROWS - 2


def _na_bias_pieces(b_ref, pat):
    return [[b_ref[h * NA_PAIRS + 2 * k - pat + NA_ROWS - 1] for k in range(NA_ROWS // 2)] for h in range(NH)]


def _na_keys(tag, proj, gk):
    t_len = proj.shape[0]
    tm = _tile(t_len, (256,))

    def body(k_ref, v_ref, gk_ref, kn_ref, vb_ref):
        kn_ref[...] = _head_rms(k_ref[...], gk_ref[...]).astype(BF16)
        vb_ref[...] = v_ref[...].astype(BF16)

    row = pl.BlockSpec((tm, BW), lambda a: (a, 0))
    return pl.pallas_call(body, grid=(t_len // tm,), in_specs=[_grp(tm, 6), _grp(tm, 7), _VEC], out_specs=[row, row],
                          out_shape=[SDS((t_len, BW), BF16)] * 2, name=f"na_keys_{tag}")(proj, proj, gk)


def _na_keys_bwd(tag, proj, gk, dkn, dv):
    t_len = proj.shape[0]
    tm = _tile(t_len, (256,))

    def body(k_ref, gk_ref, dkn_ref, dv_ref, dk_ref, dvb_ref, dgk_ref):
        _, vjp = jax.vjp(_head_rms, k_ref[...], gk_ref[...])
        dk, dgk = vjp(dkn_ref[...])
        dk_ref[...] = dk.astype(BF16)
        dvb_ref[...] = dv_ref[...].astype(BF16)
        _zero_first([dgk_ref])
        dgk_ref[...] += dgk

    row = pl.BlockSpec((tm, BW), lambda a: (a, 0))
    return pl.pallas_call(body, grid=(t_len // tm,), in_specs=[_grp(tm, 6), _VEC, row, row], out_specs=[row, row, _VEC],
                          out_shape=[SDS((t_len, BW), BF16), SDS((t_len, BW), BF16), SDS((1, BW), F32)],
                          name=f"na_keys_bwd_{tag}")(proj, gk, dkn, dv)


def _na_fwd(tag, proj, kn, vb, bias_tab, gq):
    t_len = proj.shape[0]
    rows = t_len // GRID_W
    win = NA_ROWS * GRID_W

    def body(q_ref, k_ref, v_ref, b_ref, gq_ref, o_ref):
        r0, pat = _na_window(pl.program_id(0), rows)
        start = pl.multiple_of(r0 * GRID_W, GRID_W)
        o_ref[...] = _na_f(q_ref[...], k_ref[pl.ds(start, win), :], v_ref[pl.ds(start, win), :], _na_bias_pieces(b_ref, pat),
                           gq_ref[...]).astype(BF16)

    whole = pl.BlockSpec((t_len, BW), lambda a: (0, 0))
    return pl.pallas_call(
        body, grid=(rows,),
        in_specs=[_grp(GRID_W, 5), whole, whole, pl.BlockSpec(bias_tab.shape, lambda a: (0, 0, 0)), _VEC],
        out_specs=pl.BlockSpec((GRID_W, BW), lambda a: (a, 0)), out_shape=SDS((t_len, BW), BF16),
        name=f"na_{tag}", compiler_params=_cp(VMEM_MM))(proj, kn, vb, bias_tab, gq)


def _na_bwd(tag, proj, kn, vb, bias_tab, gq, dna):
    t_len = proj.shape[0]
    rows = t_len // GRID_W
    win = NA_ROWS * GRID_W

    def body(q_ref, k_ref, v_ref, b_ref, gq_ref, do_ref, dq_ref, dk_ref, dv_ref, db_ref, dgq_ref):
        r0, pat = _na_window(pl.program_id(0), rows)
        start = pl.multiple_of(r0 * GRID_W, GRID_W)
        _zero_first([dk_ref, dv_ref, db_ref, dgq_ref])
        _, vjp = jax.vjp(_na_f, q_ref[...], k_ref[pl.ds(start, win), :].astype(F32), v_ref[pl.ds(start, win), :].astype(F32),
                         _na_bias_pieces(b_ref, pat), gq_ref[...])
        dq, dk, dv, db, dgq = vjp(do_ref[...])
        dq_ref[...] = dq.astype(BF16)
        dk_ref[pl.ds(start, win), :] += dk
        dv_ref[pl.ds(start, win), :] += dv
        for h in range(NH):
            for k in range(NA_ROWS // 2):
                db_ref[h * NA_PAIRS + 2 * k - pat + NA_ROWS - 1] += db[h][k]
        dgq_ref[...] += dgq

    whole = pl.BlockSpec((t_len, BW), lambda a: (0, 0))
    tabspec = pl.BlockSpec(bias_tab.shape, lambda a: (0, 0, 0))
    row = pl.BlockSpec((GRID_W, BW), lambda a: (a, 0))
    return pl.pallas_call(
        body, grid=(rows,), in_specs=[_grp(GRID_W, 5), whole, whole, tabspec, _VEC, row],
        out_specs=[row, whole, whole, tabspec, _VEC],
        out_shape=[SDS((t_len, BW), BF16), SDS((t_len, BW), F32), SDS((t_len, BW), F32), SDS(bias_tab.shape, F32), SDS((1, BW), F32)],
        name=f"na_bwd_{tag}", compiler_params=_cp(VMEM_BIG))(proj, kn, vb, bias_tab, gq, dna)


def _mem_fwd(tag, proj, memkv, gq, gk):
    t_len = proj.shape[0]
    n_mem = memkv.shape[0]
    tm = _tile(t_len, (256,))

    def body(q_ref, mk_ref, mv_ref, gq_ref, gk_ref, o_ref):
        o_ref[...] = _mem_f(q_ref[...], mk_ref[...], mv_ref[...], gq_ref[...], gk_ref[...]).astype(BF16)

    mspec = lambda j: pl.BlockSpec((n_mem, BW), lambda a, j=j: (0, j))
    return pl.pallas_call(
        body, grid=(t_len // tm,), in_specs=[_grp(tm, 8), mspec(0), mspec(1), _VEC, _VEC],
        out_specs=pl.BlockSpec((tm, BW), lambda a: (a, 0)), out_shape=SDS((t_len, BW), BF16), name=f"mem_{tag}")(proj, memkv, memkv, gq, gk)


def _mem_bwd(tag, proj, memkv, gq, gk, dmo):
    t_len = proj.shape[0]
    n_mem = memkv.shape[0]
    tm = _tile(t_len, (256,))

    def body(q_ref, mk_ref, mv_ref, gq_ref, gk_ref, do_ref, dq_ref, dmk_ref, dmv_ref, dgq_ref, dgk_ref):
        _zero_first([dmk_ref, dmv_ref, dgq_ref, dgk_ref])
        _, vjp = jax.vjp(_mem_f, q_ref[...], mk_ref[...], mv_ref[...], gq_ref[...], gk_ref[...])
        dq, dmk, dmv, dgq, dgk = vjp(do_ref[...])
        dq_ref[...] = dq.astype(BF16)
        dmk_ref[...] += dmk
        dmv_ref[...] += dmv
        dgq_ref[...] += dgq
        dgk_ref[...] += dgk

    mspec = lambda j: pl.BlockSpec((n_mem, BW), lambda a, j=j: (0, j))
    mout = pl.BlockSpec((n_mem, BW), lambda a: (0, 0))
    row = pl.BlockSpec((tm, BW), lambda a: (a, 0))
    dq, dmk, dmv, dgq, dgk = pl.pallas_call(
        body, grid=(t_len // tm,), in_specs=[_grp(tm, 8), mspec(0), mspec(1), _VEC, _VEC, row],
        out_specs=[row, mout, mout, _VEC, _VEC],
        out_shape=[SDS((t_len, BW), BF16), SDS((n_mem, BW), F32), SDS((n_mem, BW), F32), SDS((1, BW), F32), SDS((1, BW), F32)],
        name=f"mem_bwd_{tag}")(proj, memkv, memkv, gq, gk, dmo)
    return dq, jnp.concatenate([dmk, dmv], axis=1), dgq, dgk


NA_NJ = 2 * NA_COLS - 1


def _na_onehot():
    q = np.arange(GRID_W)[:, None]
    kc = np.arange(GRID_W)[None, :]
    qwin = np.clip(q - NA_COLS // 2, 0, GRID_W - NA_COLS)
    mask = (kc >= qwin) & (kc < qwin + NA_COLS)
    col = np.clip(kc - q, -(NA_COLS - 1), NA_COLS - 1) + NA_COLS - 1
    onehot = np.zeros((LANES, GRID_W, 2, GRID_W), np.float32)
    qq, kk = np.nonzero(mask)
    for half in range(2):
        onehot[half * NA_NJ + col[qq, kk], qq, half, kk] = 1.0
    valid = np.broadcast_to(mask[:, None, :], (GRID_W, 2, GRID_W)).astype(np.float32)
    return onehot.reshape(LANES, -1), valid.reshape(1, -1)


def _na_pair_rows(rpb):
    n_layers = rpb.shape[0]
    pair = jnp.concatenate([rpb[:, :, :-1], rpb[:, :, 1:]], axis=-1).reshape(n_layers, NH * NA_PAIRS, 2 * NA_NJ)
    return jnp.pad(pair, ((0, 0), (0, 0), (0, LANES - 2 * NA_NJ)))


def _na_bias_table(rpb):
    n_layers = rpb.shape[0]
    onehot, valid = _na_onehot()
    width = onehot.shape[1]

    def body(r_ref, oh_ref, ok_ref, o_ref):
        t = jnp.dot(r_ref[...], oh_ref[...], precision=lax.Precision.HIGHEST, preferred_element_type=F32)
        o_ref[...] = jnp.where(ok_ref[...] > 0, t, NEG_INF)

    out = pl.pallas_call(
        body, grid=(n_layers,),
        in_specs=[pl.BlockSpec((None, NH * NA_PAIRS, LANES), lambda l: (l, 0, 0)), pl.BlockSpec((LANES, width), lambda l: (0, 0)),
                  pl.BlockSpec((1, width), lambda l: (0, 0))],
        out_specs=pl.BlockSpec((None, NH * NA_PAIRS, width), lambda l: (l, 0, 0)),
        out_shape=SDS((n_layers, NH * NA_PAIRS, width), F32), name="na_bias_table")(
            _na_pair_rows(rpb), jnp.asarray(onehot), jnp.asarray(valid))
    return out.reshape(n_layers, NH * NA_PAIRS, GRID_W, 2 * GRID_W)


def _na_bias_grad(dtab):
    n_layers = dtab.shape[0]
    onehot, _ = _na_onehot()
    width = onehot.shape[1]

    def body(x_ref, oh_ref, o_ref):
        o_ref[...] = jnp.dot(x_ref[...], oh_ref[...], precision=lax.Precision.HIGHEST, preferred_element_type=F32)

    out = pl.pallas_call(
        body, grid=(n_layers,),
        in_specs=[pl.BlockSpec((None, NH * NA_PAIRS, width), lambda l: (l, 0, 0)), pl.BlockSpec((width, LANES), lambda l: (0, 0))],
        out_specs=pl.BlockSpec((None, NH * NA_PAIRS, LANES), lambda l: (l, 0, 0)),
        out_shape=SDS((n_layers, NH * NA_PAIRS, LANES), F32), name="na_bias_grad")(
            dtab.reshape(n_layers, NH * NA_PAIRS, width), jnp.asarray(onehot.T.copy()))
    out = out.reshape(n_layers, NH, NA_PAIRS, LANES)
    zero = jnp.zeros((n_layers, NH, 1, NA_NJ), F32)
    return (jnp.concatenate([out[..., :NA_NJ], zero], axis=2) + jnp.concatenate([zero, out[..., NA_NJ:2 * NA_NJ]], axis=2))


def _place():
    x, y, c = lax.axis_index("x"), lax.axis_index("y"), lax.axis_index("c")
    chips = [(1 - x, y), (x, 1 - y), (1 - x, 1 - y)]
    return x, y, c, chips


def _remote(src, dst, ssem, rsem, dev):
    return pltpu.make_async_remote_copy(src_ref=src, dst_ref=dst, send_sem=ssem, recv_sem=rsem, device_id=dev, device_id_type=MESH)


HBM = pl.BlockSpec(memory_space=pltpu.HBM)
SEM = pl.BlockSpec(memory_space=pltpu.SEMAPHORE)
DATAFLOW = pltpu.SideEffectType.DATAFLOW_SIDE_EFFECTING


def _split_copies(name, srcs, lands, n_copies, plan):
    n_s, n_l = len(srcs), len(lands)
    hbm = lambda a: pltpu.HBM(a.shape, a.dtype)

    def start_body(*refs):
        ins = refs[:n_s + n_l]
        outs = refs[n_s + n_l:]
        sems, token = outs[:2 * n_copies], outs[-1]
        for k, (src, dst, dev) in enumerate(plan(ins[:n_s], ins[n_s:])):
            _remote(src, dst, sems[k], sems[n_copies + k], dev).start()
        token[...] = jnp.zeros_like(token)

    outs = pl.pallas_call(
        start_body, name=f"{name}_start",
        out_shape=(pltpu.SemaphoreType.DMA(()),) * (2 * n_copies) + tuple(hbm(a) for a in (*srcs, *lands)) + (SDS((8, LANES), F32),),
        in_specs=(HBM,) * (n_s + n_l), out_specs=(SEM,) * (2 * n_copies) + (HBM,) * (n_s + n_l) + (pl.BlockSpec(memory_space=pltpu.VMEM),),
        input_output_aliases={i: 2 * n_copies + i for i in range(n_s + n_l)},
        compiler_params=pltpu.CompilerParams(has_side_effects=DATAFLOW),
    )(*[pltpu.with_memory_space_constraint(a, pltpu.HBM) for a in (*srcs, *lands)])
    sems, thru, token = outs[:2 * n_copies], outs[2 * n_copies:-1], outs[-1]

    def wait(after_wait):
        def wait_body(*refs):
            ins = refs[:n_s + n_l]
            sem_refs = refs[n_s + n_l:n_s + n_l + 2 * n_copies]
            for k, (src, dst, dev) in enumerate(plan(ins[:n_s], ins[n_s:])):
                cp = _remote(src, dst, sem_refs[k], sem_refs[n_copies + k], dev)
                cp.wait_send()
                cp.wait_recv()

        res = pl.pallas_call(
            wait_body, name=f"{name}_wait", out_shape=tuple(hbm(a) for a in (*srcs, *lands)),
            in_specs=(HBM,) * (n_s + n_l) + (SEM,) * (2 * n_copies) + (ANY,), out_specs=(HBM,) * (n_s + n_l),
            input_output_aliases={i: i for i in range(n_s + n_l)},
            compiler_params=pltpu.CompilerParams(has_side_effects=DATAFLOW),
        )(*thru, *sems, after_wait)
        return list(res[:n_s]), list(res[n_s:])

    return token, wait


def _gather_plan(srcs, lands):
    x, y, c, chips = _place()
    me = 2 * x + y
    out = []
    for src, land in zip(srcs, lands):
        h = src.shape[0] // 2
        out.append((src, land.at[me], (x, y, 1 - c)))
        for px, py in chips:
            out.append((src.at[pl.ds(c * h, h)], land.at[me, pl.ds(c * h, h)], (px, py, c)))
    return out


def _scatter_plan(srcs, lands):
    x, y, c, chips = _place()
    out = []
    for src, land in zip(srcs, lands):
        for r, (px, py) in enumerate(chips):
            out.append((src.at[2 * px + py], land.at[r], (px, py, c)))
    return out


def _pass_plan(srcs, lands):
    x, y, c, chips = _place()
    out = []
    for land in lands:
        h = land.shape[1] // 2
        for px, py in chips:
            piece = land.at[2 * px + py, pl.ds(c * h, h)]
            out.append((piece, piece, (x, y, 1 - c)))
    return out


def _pair_plan(srcs, lands):
    x, y, c, _ = _place()
    out = []
    for src, land in zip(srcs, lands):
        h = land.shape[1]
        out.append((src.at[:, pl.ds((1 - c) * h, h)], land, (x, y, 1 - c)))
    return out


def _half_plan(srcs, lands):
    x, y, c, _ = _place()
    out = []
    for land in lands:
        h = land.shape[0] // 2
        mine = land.at[pl.ds(c * h, h)]
        out.append((mine, mine, (x, y, 1 - c)))
    return out


def _forward_halves(name, lands):
    n = len(lands)

    def body(*refs):
        ins, outs = refs[:n], refs[n:2 * n]
        ssem, rsem = refs[2 * n:]
        x, y, c, chips = _place()
        sib = (x, y, 1 - c)

        def copy(i, r, half):
            h = ins[i].shape[1] // 2
            px, py = chips[r]
            rows = pl.ds(half * h, h)
            return _remote(ins[i].at[2 * px + py, rows], outs[i].at[2 * px + py, rows], ssem.at[i, r], rsem.at[i, r], sib)

        for i in range(n):
            for r in range(3):
                copy(i, r, c).start()
        for i in range(n):
            for r in range(3):
                copy(i, r, 1 - c).wait_recv()
                copy(i, r, c).wait_send()

    return pl.pallas_call(
        body, in_specs=[ANY] * n, out_specs=[ANY] * n, out_shape=[SDS(a.shape, a.dtype) for a in lands],
        input_output_aliases={i: i for i in range(n)},
        scratch_shapes=[pltpu.SemaphoreType.DMA((n, 3)), pltpu.SemaphoreType.DMA((n, 3))], name=name)(*lands)


def _pair_exchange(name, gs):
    n = len(gs)

    def body(*refs):
        ins, outs = refs[:n], refs[n:2 * n]
        ssem, rsem = refs[2 * n:]
        x, y, c, _ = _place()
        cps = []
        for i in range(n):
            h = ins[i].shape[1] // 2
            cps.append(_remote(ins[i].at[:, pl.ds((1 - c) * h, h)], outs[i], ssem.at[i], rsem.at[i], (x, y, 1 - c)))
            cps[-1].start()
        for cp in cps:
            cp.wait()

    return pl.pallas_call(
        body, in_specs=[ANY] * n, out_specs=[ANY] * n,
        out_shape=[SDS((g.shape[0], g.shape[1] // 2, g.shape[2]), g.dtype) for g in gs],
        scratch_shapes=[pltpu.SemaphoreType.DMA((n,)), pltpu.SemaphoreType.DMA((n,))], name=name)(*gs)


def _half_exchange(name, fs):
    n = len(fs)

    def body(*refs):
        ins, outs = refs[:n], refs[n:2 * n]
        ssem, rsem = refs[2 * n:]
        x, y, c, _ = _place()

        def copy(i, half):
            h = ins[i].shape[0] // 2
            return _remote(ins[i].at[pl.ds(half * h, h)], outs[i].at[pl.ds(half * h, h)], ssem.at[i], rsem.at[i], (x, y, 1 - c))

        for i in range(n):
            copy(i, c).start()
        for i in range(n):
            copy(i, 1 - c).wait_recv()
            copy(i, c).wait_send()

    return pl.pallas_call(
        body, in_specs=[ANY] * n, out_specs=[ANY] * n, out_shape=[SDS(f.shape, f.dtype) for f in fs],
        input_output_aliases={i: i for i in range(n)},
        scratch_shapes=[pltpu.SemaphoreType.DMA((n,)), pltpu.SemaphoreType.DMA((n,))], name=name)(*fs)


def _gather_small(v):
    def body(v_ref, o_ref, lsem, ssem, rsem):
        x, y, c, _ = _place()
        me = 4 * x + 2 * y + c
        flips = [(fx, fy, fc) for fx in (0, 1) for fy in (0, 1) for fc in (0, 1)][1:]
        peer = lambda f: (1 - x if f[0] else x, 1 - y if f[1] else y, 1 - c if f[2] else c)
        loc = pltpu.make_async_copy(v_ref, o_ref.at[me], lsem)
        loc.start()
        cps = [_remote(v_ref, o_ref.at[me], ssem.at[k], rsem.at[k], peer(f)) for k, f in enumerate(flips)]
        for cp in cps:
            cp.start()
        for k, f in enumerate(flips):
            px, py, pc = peer(f)
            land = o_ref.at[4 * px + 2 * py + pc]
            _remote(land, land, ssem.at[k], rsem.at[k], peer(f)).wait_recv()
        for cp in cps:
            cp.wait_send()
        loc.wait()

    return pl.pallas_call(
        body, in_specs=[ANY], out_specs=ANY, out_shape=SDS((N_DEV,) + v.shape, v.dtype),
        scratch_shapes=[pltpu.SemaphoreType.DMA, pltpu.SemaphoreType.DMA((N_DEV - 1,)), pltpu.SemaphoreType.DMA((N_DEV - 1,))],
        name="gather_small")(v)


def _rows_tile(r, big=False):
    return _tile(r, ((512,) if big else ()) + (256, 352, 128, 64))


def _pair_sum(name, g, r1, core, me):
    n_l, s_n, h, c = r1.shape
    tr = _rows_tile(h, big=True)
    nb = h // tr

    def body(idx_ref, g_ref, r_ref, pb_ref, own_ref):
        p = g_ref[...] + r_ref[...]
        pb_ref[...] = p.astype(BF16)

        @pl.when(pl.program_id(2) == idx_ref[1])
        def _():
            own_ref[...] = p

    blk = (None, None, tr, c)
    grid_spec = pltpu.PrefetchScalarGridSpec(
        num_scalar_prefetch=1, grid=(n_l, nb, s_n),
        in_specs=[pl.BlockSpec(blk, lambda a, i, s, idx: (a, s, idx[0] * nb + i, 0)), pl.BlockSpec(blk, lambda a, i, s, idx: (a, s, i, 0))],
        out_specs=[pl.BlockSpec(blk, lambda a, i, s, idx: (a, s, i, 0)), pl.BlockSpec((None, tr, c), lambda a, i, s, idx: (a, i, 0))])
    idx = jnp.stack([core, me]).astype(jnp.int32)
    return pl.pallas_call(body, grid_spec=grid_spec, out_shape=[SDS(r1.shape, BF16), SDS((n_l, h, c), F32)], name=name,
                          compiler_params=_cp(VMEM_MM))(idx, g, r1)


def _chip_sum(name, own, r2, core):
    n_l, h, c = own.shape
    tr = _rows_tile(h, big=True)
    nb = h // tr

    def body(idx_ref, o_ref, r_ref, f_ref):
        f_ref[...] = ((o_ref[...] + r_ref[0].astype(F32)) + r_ref[1].astype(F32)) + r_ref[2].astype(F32)

    grid_spec = pltpu.PrefetchScalarGridSpec(
        num_scalar_prefetch=1, grid=(n_l, nb),
        in_specs=[pl.BlockSpec((None, tr, c), lambda a, i, idx: (a, i, 0)), pl.BlockSpec((3, None, tr, c), lambda a, i, idx: (0, a, i, 0))],
        out_specs=pl.BlockSpec((None, tr, c), lambda a, i, idx: (a, idx[0] * nb + i, 0)))
    return pl.pallas_call(body, grid_spec=grid_spec, out_shape=SDS((n_l, 2 * h, c), F32), name=name,
                          compiler_params=_cp(VMEM_MM))(jnp.reshape(core, (1,)).astype(jnp.int32), own, r2)


def _adamw_math(w, g, m, v):
    m = ADAM_B1 * m + (1.0 - ADAM_B1) * g
    v = ADAM_B2 * v + (1.0 - ADAM_B2) * jnp.square(g)
    m_hat = m / (1.0 - ADAM_B1 ** ADAM_STEP)
    v_hat = v / (1.0 - ADAM_B2 ** ADAM_STEP)
    delta = -ADAM_LR * (m_hat / (jnp.sqrt(v_hat) + ADAM_EPS) + ADAM_WD * w)
    return delta, m, v


def _adamw(name, l, w, g, m, v, bufs):
    _, r, c = w.shape
    tr = _rows_tile(r, big=c <= 1024)

    def body(w_ref, g_ref, m_ref, v_ref, *rest):
        go_ref, d_ref, nm_ref, nv_ref = rest[4:]
        g = g_ref[...]
        go_ref[...] = g
        d_ref[...], nm_ref[...], nv_ref[...] = _adamw_math(w_ref[...], g, m_ref[...], v_ref[...])

    blk = pl.BlockSpec((None, tr, c), lambda i: (l, i, 0))
    return pl.pallas_call(body, grid=(r // tr,), in_specs=[blk, pl.BlockSpec((tr, c), lambda i: (i, 0)), blk, blk] + [ANY] * 4,
                          out_specs=[blk] * 4, out_shape=[SDS(w.shape, F32)] * 4, input_output_aliases={4 + k: k for k in range(4)},
                          name=name, compiler_params=_cp(VMEM_MM))(w, g, m, v, *bufs)


def _small_update(parts, w, m, v):
    def body(p_ref, w_ref, m_ref, v_ref, g_ref, d_ref, nm_ref, nv_ref):
        g = p_ref[0]
        for d in range(1, N_DEV):
            g = g + p_ref[d]
        g_ref[...] = g
        d_ref[...], nm_ref[...], nv_ref[...] = _adamw_math(w_ref[...], g, m_ref[...], v_ref[...])

    return pl.pallas_call(body, out_shape=[SDS(w.shape, F32)] * 4, name="small_update")(parts, w, m, v)


SMALL = ["norm_mix_g", "norm_mem_g", "ret_decay_fwd", "ret_decay_bwd", "ret_norm_g", "pool_w", "pool_scale", "na_q_norm_g",
         "na_k_norm_g", "na_rpb", "mem_q_norm_g", "mem_k_norm_g", "norm_ffn_g"]
BIG = ["w_in", "w_gate", "w_mem_kv", "w_branch", "w_out", "w_ffn_in", "w_ffn_out"]
GROUPS = (("w_in", "w_gate", "w_mem_kv", "w_branch", "w_out"), ("w_ffn_in", "w_ffn_out"))
REDUCE_GROUPS = (("w_ffn_in", "w_ffn_out"), ("w_gate", "w_branch", "w_out"), ("w_in", "w_mem_kv"))
ORDER = ["norm_mix_g", "norm_mem_g", "w_in", "w_gate", "ret_decay_fwd", "ret_decay_bwd", "ret_norm_g", "pool_w", "pool_scale",
         "na_q_norm_g", "na_k_norm_g", "na_rpb", "mem_q_norm_g", "mem_k_norm_g", "w_mem_kv", "w_branch", "w_out", "norm_ffn_g",
         "w_ffn_in", "w_ffn_out"]


def _pack(arrs):
    rows = []
    for a in arrs:
        f = a.reshape(-1).astype(F32)
        pad = (-f.shape[0]) % (8 * LANES)
        rows.append(jnp.pad(f, (0, pad)).reshape(-1, LANES))
    return jnp.concatenate(rows, axis=0)


def _unpack(p, like):
    out, at = [], 0
    for a in like:
        n = int(np.prod(a.shape))
        rows = -(-n // (8 * LANES)) * 8
        out.append(p[at:at + rows].reshape(-1)[:n].reshape(a.shape))
        at += rows
    return out


def _rope_tables(t_len):
    half = HEAD // 2
    inv = ROPE_THETA ** (-jnp.arange(half, dtype=F32) / half)
    ang = jnp.arange(t_len, dtype=F32)[:, None] * inv[None, :]
    cos, sin = jnp.cos(ang), jnp.sin(ang)
    return jnp.tile(jnp.concatenate([cos, cos], axis=1), (1, NH)), jnp.tile(jnp.concatenate([-sin, sin], axis=1), (1, NH))


def _block_diag(pw):
    out = jnp.zeros((BW, BW), pw.dtype)
    for g in range(NH):
        out = out.at[g * HEAD:(g + 1) * HEAD, g * HEAD:(g + 1) * HEAD].set(pw[g])
    return out


def _fold_heads(v):
    return v.reshape(NH, HEAD).sum(axis=0)


def _fold_lanes(v):
    return v.reshape(NH, HEAD).sum(axis=1)


def kernel(x, mem, norm_mix_g, norm_mem_g, w_in, w_gate, ret_decay_fwd, ret_decay_bwd, ret_norm_g, pool_w, pool_scale, na_q_norm_g, na_k_norm_g, na_rpb, mem_q_norm_g, mem_k_norm_g, w_mem_kv, w_branch, w_out, norm_ffn_g, w_ffn_in, w_ffn_out, loss_target, m_norm_mix_g, m_norm_mem_g, m_w_in, m_w_gate, m_ret_decay_fwd, m_ret_decay_bwd, m_ret_norm_g, m_pool_w, m_pool_scale, m_na_q_norm_g, m_na_k_norm_g, m_na_rpb, m_mem_q_norm_g, m_mem_k_norm_g, m_w_mem_kv, m_w_branch, m_w_out, m_norm_ffn_g, m_w_ffn_in, m_w_ffn_out, v_norm_mix_g, v_norm_mem_g, v_w_in, v_w_gate, v_ret_decay_fwd, v_ret_decay_bwd, v_ret_norm_g, v_pool_w, v_pool_scale, v_na_q_norm_g, v_na_k_norm_g, v_na_rpb, v_mem_q_norm_g, v_mem_k_norm_g, v_w_mem_kv, v_w_branch, v_w_out, v_norm_ffn_g, v_w_ffn_in, v_w_ffn_out):
    args = dict(locals())
    W = {n: args[n] for n in ORDER}
    M = {n: args["m_" + n] for n in ORDER}
    V = {n: args["v_" + n] for n in ORDER}
    n_layers, d_model = norm_mix_g.shape
    assert x.shape[0] == 1
    t_len = x.shape[1]
    xc, yc, cc = lax.axis_index("x"), lax.axis_index("y"), lax.axis_index("c")
    me_chip = 2 * xc + yc

    def as3(a):
        return a.reshape(a.shape[0], -1, a.shape[-1])

    gather_waits, token = {}, jnp.zeros((), F32)
    for l in range(n_layers):
        for gi, names in enumerate(GROUPS):
            srcs = [as3(W[n])[l].astype(BF16) for n in names]
            lands = [lax.empty((N_CHIPS,) + a.shape, BF16) for a in srcs]
            tok, gather_waits[l, gi] = _split_copies(f"gather_l{l}_g{gi}", srcs, lands, 4 * len(names), _gather_plan)
            token = token + tok[0, 0]

    cos, sin = _rope_tables(t_len)
    btabs = _na_bias_table(na_rpb)
    lane = lambda a: a.reshape(1, -1).astype(F32)
    x2d, mem2d, tgt = x[0], mem[0], loss_target[0]
    norm_mix_g = norm_mix_g + token

    def fetch(l, gi, after):
        lands = gather_waits[l, gi](after)[1]
        return _split_copies(f"pass_l{l}_g{gi}", [], lands, 3 * len(GROUPS[gi]), _pass_plan)

    saved = []
    _, passing = fetch(0, 0, x2d)
    for l in range(n_layers):
        tag = f"l{l}"
        gathered = dict(zip(GROUPS[0], passing(x2d)[1]))
        wi = gathered["w_in"].transpose(1, 0, 2).reshape(1, 1, d_model, -1)
        wg = gathered["w_gate"][None]
        wmkv = gathered["w_mem_kv"].reshape(1, 1, d_model, 2 * BW)
        wb = gathered["w_branch"].reshape(1, N_CHIPS, NH, BW, BW)
        wo = gathered["w_out"].reshape(1, 1, d_model, d_model)
        s = {"x": x2d}
        s["decf"], s["decb"] = lane(jnp.repeat(ret_decay_fwd[l], HEAD)), lane(jnp.repeat(ret_decay_bwd[l], HEAD))
        s["gn"], s["pscale"] = lane(ret_norm_g[l]), lane(pool_scale[l])
        s["wbd"] = _block_diag(pool_w[l])
        s["nagq"], s["nagk"] = lane(jnp.tile(na_q_norm_g[l], NH)), lane(jnp.tile(na_k_norm_g[l], NH))
        s["mgq"], s["mgk"] = lane(jnp.tile(mem_q_norm_g[l], NH)), lane(jnp.tile(mem_k_norm_g[l], NH))
        s["btab"] = btabs[l]
        s["h"] = _rms_fwd(f"rms_mix_{tag}", x2d, lane(norm_mix_g[l]))
        s["proj"] = _mm_nn(f"mm_in_{tag}", s["h"], wi, 0)
        s["gp3"] = _mm_nn(f"mm_gate_{tag}", s["h"], wg, 0, out3d=True)
        tok, passing = fetch(l, 1, s["gp3"])
        s["hm"] = _rms_fwd(f"rms_mem_{tag}", mem2d, lane(norm_mem_g[l]) + tok[0, 0])
        s["memkv"] = _mm_nn(f"mm_memkv_{tag}", s["hm"], wmkv, 0)
        ret, s["ret_saved"] = _ret_fwd(tag, s["proj"], cos, sin, s["decf"], s["decb"], s["gn"])
        s["pvp"] = jnp.pad(s["proj"][:, 4 * BW:5 * BW], ((POOL_PAD, POOL_PAD), (0, 0)))
        pool = _pool_fwd(tag, s["pvp"], s["wbd"], s["pscale"], t_len)[POOL_PAD:POOL_PAD + t_len]
        s["kn"], s["vb"] = _na_keys(tag, s["proj"], s["nagk"])
        na = _na_fwd(tag, s["proj"], s["kn"], s["vb"], s["btab"], s["nagq"])
        mo = _mem_fwd(tag, s["proj"], s["memkv"], s["mgq"], s["mgk"])
        s["br"] = jnp.stack([ret, pool, na, mo])
        s["merged"] = _gate_fwd(f"gate_{tag}", s["gp3"], s["br"], wb, 0)
        s["x2"] = _mm_nn(f"mm_out_{tag}", s["merged"], wo, 0, add=x2d)
        gathered = dict(zip(GROUPS[1], passing(s["x2"])[1]))
        pin_ffn = jnp.zeros((), F32)
        if l + 1 < n_layers:
            tok, passing = fetch(l + 1, 0, s["x2"])
            pin_ffn = tok[0, 0]
        wfi = gathered["w_ffn_in"][None]
        wfo = gathered["w_ffn_out"].reshape(1, 1, -1, d_model)
        s["w"] = (wi, wg, wmkv, wb, wo, wfi, wfo)
        s["h2"] = _rms_fwd(f"rms_ffn_{tag}", s["x2"], lane(norm_ffn_g[l]) + pin_ffn)
        s["ag3"] = _mm_nn(f"mm_ffn_in_{tag}", s["h2"], wfi, 0, out3d=True)
        s["u"] = _swiglu_fwd(f"swiglu_{tag}", s["ag3"])
        x2d = _mm_nn(f"mm_ffn_out_{tag}", s["u"], wfo, 0, add=s["x2"])
        saved.append(s)

    loss_part, dx = _loss_kernel("loss", x2d, tgt)

    dbtabs = [None] * n_layers
    gsmall = {n: [None] * n_layers for n in SMALL}
    opt_view = {n: (lambda a, n=n: jnp.swapaxes(as3(a), 1, 2) if n == "w_in" else as3(a)) for n in BIG}
    out_bufs = {n: tuple(lax.empty(opt_view[n](W[n]).shape, F32) for _ in range(4)) for n in BIG}

    def reduce_group(l, gi, grads):
        names = REDUCE_GROUPS[gi]
        gs = [grads[n] for n in names]
        lands = [lax.empty((N_CHIPS, g.shape[1] // 2, g.shape[2]), F32) for g in gs]
        tok0, wait0 = _split_copies(f"pairx_l{l}_g{gi}", gs, lands, len(names), _pair_plan)

        def sums_and_scatter(after):
            mine, theirs = wait0(after)
            sums = [_pair_sum(f"pair_sum_{n}_l{l}", g[None], r[None], cc, me_chip) for n, g, r in zip(names, mine, theirs)]
            pbs = [p[0] for p, _ in sums]
            lands1 = [lax.empty((3,) + p.shape[1:], BF16) for p in pbs]
            tok1, wait1 = _split_copies(f"scatter_l{l}_g{gi}", pbs, lands1, 3 * len(names), _scatter_plan)

            def chip_sums_and_halves(after):
                halves = [_chip_sum(f"chip_sum_{n}_l{l}", own, r[:, None], cc)[0] for n, (_, own), r in zip(names, sums, wait1(after)[1])]
                tok2, wait2 = _split_copies(f"halfx_l{l}_g{gi}", [], halves, len(names), _half_plan)

                def update(after):
                    for n, f in zip(names, wait2(after)[1]):
                        out_bufs[n] = _adamw(f"adamw_{n}_l{l}", l, opt_view[n](W[n]), f, opt_view[n](M[n]), opt_view[n](V[n]), out_bufs[n])
                    return None, None

                return update, tok2

            return chip_sums_and_halves, tok1

        return sums_and_scatter, tok0

    def advance(stages, new):
        nxt, tok = new
        out, toks = [nxt], tok
        for stage in stages:
            nxt, tok = stage(toks)
            if nxt is not None:
                out.append(nxt)
                toks = toks + tok
        return out, toks

    stages, order_after = [], None
    for l in reversed(range(n_layers)):
        tag = f"l{l}"
        s = saved[l]
        wi, wg, wmkv, wb, wo, wfi, wfo = s["w"]
        du = _mm_nt(f"mm_ffn_out_dx_{tag}", dx, wfo, 0, after=order_after)
        g_wfo = _mm_tn(f"mm_ffn_out_dw_{tag}", s["u"], dx, 1)
        dag3 = _swiglu_bwd(f"swiglu_bwd_{tag}", s["ag3"], du)
        dh2 = _mm_nt(f"mm_ffn_in_dx_{tag}", dag3, wfi, 0)
        g_wfi = _mm_tn(f"mm_ffn_in_dw_{tag}", s["h2"], dag3, N_CHIPS)
        dx2, dg = _rms_bwd(f"rms_ffn_bwd_{tag}", s["x2"], lane(norm_ffn_g[l]), [dh2], add=dx)
        gsmall["norm_ffn_g"][l] = dg[0]
        stages, tok = advance(stages, reduce_group(l, 0, {"w_ffn_in": g_wfi, "w_ffn_out": g_wfo.reshape(N_CHIPS, -1, d_model)}))
        dmerged = _mm_nt(f"mm_out_dx_{tag}", dx2, wo, 0, after=tok)
        g_wo = _mm_tn(f"mm_out_dw_{tag}", s["merged"], dx2, 1)
        dgp3, dbr, g_wb = _gate_bwd(f"gate_bwd_{tag}", s["gp3"], s["br"], wb, 0, dmerged)
        g_wg = _mm_tn(f"mm_gate_dw_{tag}", s["h"], dgp3, N_CHIPS)
        stages, tok = advance(stages, reduce_group(l, 1, {
            "w_gate": g_wg, "w_branch": g_wb.reshape(N_CHIPS, NH * BW, BW), "w_out": g_wo.reshape(N_CHIPS, -1, d_model)}))
        dh_a = _mm_nt(f"mm_gate_dx_{tag}", dgp3, wg, 0, after=tok)
        drq, drk, drv, drg, ddf, ddb, dgn = _ret_bwd(tag, s["proj"], cos, sin, s["decf"], s["decb"], s["gn"], s["ret_saved"], dbr[0])
        gsmall["ret_decay_fwd"][l], gsmall["ret_decay_bwd"][l], gsmall["ret_norm_g"][l] = _fold_lanes(ddf), _fold_lanes(ddb), dgn[0]
        dpool_p = jnp.pad(dbr[1], ((POOL_PAD, POOL_PAD), (0, 0)))
        dpvp, dwbd, dps = _pool_bwd(tag, s["pvp"], s["wbd"], s["pscale"], t_len, dpool_p)
        dpv = dpvp[POOL_PAD:POOL_PAD + t_len]
        gsmall["pool_w"][l] = jnp.stack([dwbd[g * HEAD:(g + 1) * HEAD, g * HEAD:(g + 1) * HEAD] for g in range(NH)])
        gsmall["pool_scale"][l] = dps[0]
        dnq, dkn, dnv, dbtabs[l], dgq = _na_bwd(tag, s["proj"], s["kn"], s["vb"], s["btab"], s["nagq"], dbr[2])
        dnk, dnv, dgk = _na_keys_bwd(tag, s["proj"], s["nagk"], dkn, dnv)
        gsmall["na_q_norm_g"][l], gsmall["na_k_norm_g"][l] = _fold_heads(dgq), _fold_heads(dgk)
        dmq, dmemkv, dgq, dgk = _mem_bwd(tag, s["proj"], s["memkv"], s["mgq"], s["mgk"], dbr[3])
        gsmall["mem_q_norm_g"][l], gsmall["mem_k_norm_g"][l] = _fold_heads(dgq), _fold_heads(dgk)
        g_wmkv = _mm_tn(f"mm_memkv_dw_{tag}", s["hm"], dmemkv, 1)
        dhm = _mm_nt(f"mm_memkv_dx_{tag}", dmemkv, wmkv, 0)
        _, dg = _rms_bwd(f"rms_mem_bwd_{tag}", mem2d, lane(norm_mem_g[l]), [dhm])
        gsmall["norm_mem_g"][l] = dg[0]
        dproj = jnp.concatenate([drq, drk, drv, drg, dpv, dnq, dnk, dnv, dmq], axis=1)
        g_wi = _mm_tn(f"mm_in_dw_{tag}", dproj, s["h"], 1)
        dh_b = _mm_nt(f"mm_in_dx_{tag}", dproj, wi, 0)
        dx, dg = _rms_bwd(f"rms_mix_bwd_{tag}", s["x"], lane(norm_mix_g[l]), [dh_a, dh_b], add=dx2)
        gsmall["norm_mix_g"][l] = dg[0]

        stages, order_after = advance(stages, reduce_group(l, 2, {
            "w_in": g_wi.reshape(N_CHIPS, -1, d_model), "w_mem_kv": g_wmkv.reshape(N_CHIPS, -1, 2 * BW)}))
    out_g, out_d, out_m, out_v = {}, {}, {}, {}
    gsmall["na_rpb"] = list(_na_bias_grad(jnp.stack(dbtabs)))
    small_g = [jnp.stack(gsmall[n]).reshape(W[n].shape) for n in SMALL] + [loss_part]
    like = [W[n] for n in SMALL] + [loss_part]
    zero = jnp.zeros((1, 1), F32)
    parts = _gather_small(_pack(small_g))
    packed = _small_update(parts, _pack([W[n] for n in SMALL] + [zero]), _pack([M[n] for n in SMALL] + [zero]),
                           _pack([V[n] for n in SMALL] + [zero]))
    sg, sd, sm, sv = [_unpack(p, like) for p in packed]
    for i, n in enumerate(SMALL):
        out_g[n], out_d[n], out_m[n], out_v[n] = sg[i], sd[i], sm[i], sv[i]
    loss = sg[-1].reshape(())

    while stages:
        stages = [nxt for nxt, _ in (stage(packed[0]) for stage in stages) if nxt is not None]

    for n in BIG:
        shp = W[n].shape
        back = (lambda b: jnp.swapaxes(b, 1, 2)) if n == "w_in" else (lambda b: b)
        out_g[n], out_d[n], out_m[n], out_v[n] = [back(b).reshape(shp) for b in out_bufs[n]]

    return (loss, dx.reshape(x.shape), *[out_g[n] for n in ORDER], *[out_d[n] for n in ORDER],
            *[out_m[n] for n in ORDER], *[out_v[n] for n in ORDER])
```

```python
import functools

import numpy as np
import jax
import jax.numpy as jnp
from jax import lax
from jax.experimental import pallas as pl
from jax.experimental.pallas import tpu as pltpu

F32 = jnp.float32
BF16 = jnp.bfloat16
SDS = jax.ShapeDtypeStruct
MESH = pl.DeviceIdType.MESH
ANY = pl.BlockSpec(memory_space=pl.ANY)

HEAD = 64
NH = 4
BW = NH * HEAD
CH = 128
GRID_W = 64
NA_ROWS = 8
NA_COLS = 16
POOL_HALF = (1, 2, 4, 8)
POOL_PAD = 16
ROPE_THETA = 10000.0
EPS = 1e-6
NEG_INF = -1e30
N_CHIPS = 4
N_DEV = 8
LANES = 128

ADAM_LR = 0.001
ADAM_B1 = 0.9
ADAM_B2 = 0.999
ADAM_EPS = 1e-08
ADAM_WD = 0.01
ADAM_STEP = 10

VMEM_BIG = 56 << 20
VMEM_MM = 44 << 20


def _cp(vmem=None):
    return pltpu.CompilerParams(vmem_limit_bytes=vmem) if vmem else None


def _d(a, b, ca, cb):
    return lax.dot_general(a.astype(BF16), b.astype(BF16), (((ca,), (cb,)), ((), ())),
                           preferred_element_type=F32)


@jax.custom_vjp
def _nn(a, b):
    return _d(a, b, 1, 0)


def _nn_f(a, b):
    return _d(a, b, 1, 0), (a, b)


def _nn_b(r, g):
    a, b = r
    return _d(g, b, 1, 1).astype(a.dtype), _d(a, g, 0, 0).astype(b.dtype)


_nn.defvjp(_nn_f, _nn_b)


@jax.custom_vjp
def _nt(a, b):
    return _d(a, b, 1, 1)


def _nt_f(a, b):
    return _d(a, b, 1, 1), (a, b)


def _nt_b(r, g):
    a, b = r
    return _d(g, b, 1, 0).astype(a.dtype), _d(g, a, 0, 0).astype(b.dtype)


_nt.defvjp(_nt_f, _nt_b)


@jax.custom_vjp
def _tn(a, b):
    return _d(a, b, 0, 0)


def _tn_f(a, b):
    return _d(a, b, 0, 0), (a, b)


def _tn_b(r, g):
    a, b = r
    return _d(b, g, 1, 1).astype(a.dtype), _d(a, g, 1, 0).astype(b.dtype)


_tn.defvjp(_tn_f, _tn_b)


@functools.partial(jax.custom_vjp, nondiff_argnums=(1,))
def _rollr(x, s):
    return pltpu.roll(x, s % x.shape[0], 0)


def _rollr_f(x, s):
    return _rollr(x, s), None


def _rollr_b(s, _, g):
    return (_rollr(g, -s),)


_rollr.defvjp(_rollr_f, _rollr_b)


@jax.custom_vjp
def _swap32(t):
    n = t.shape[1]
    lane = lax.broadcasted_iota(jnp.int32, (1, n), 1)
    return jnp.where((lane & (HEAD // 2)) == 0, pltpu.roll(t, n - HEAD // 2, 1), pltpu.roll(t, HEAD // 2, 1))


def _swap32_f(t):
    return _swap32(t), None


def _swap32_b(_, g):
    return (_swap32(g),)


_swap32.defvjp(_swap32_f, _swap32_b)


def _head_masks():
    lane = lax.broadcasted_iota(jnp.int32, (1, BW), 1)
    return [(lane >= HEAD * h) & (lane < HEAD * (h + 1)) for h in range(NH)]


def _head_rms(x, g):
    out = jnp.zeros_like(x)
    for mh in _head_masks():
        ms = jnp.sum(jnp.where(mh, x * x, 0.0), axis=-1, keepdims=True) * (1.0 / HEAD)
        out = out + jnp.where(mh, x * lax.rsqrt(ms + EPS), 0.0)
    return out * g


def _stack_heads(x):
    return jnp.concatenate([jnp.where(mh, x, 0.0) for mh in _head_masks()], axis=0)


def _unstack_heads(y):
    n = y.shape[0] // NH
    out = None
    for h, mh in enumerate(_head_masks()):
        t = jnp.where(mh, y[h * n:(h + 1) * n], 0.0)
        out = t if out is None else out + t
    return out


def _rms(x, g):
    return x * lax.rsqrt(jnp.mean(x * x, axis=-1, keepdims=True) + EPS) * g


def _rot(t, cos, sin):
    return t * cos + _swap32(t) * sin


def _softmax(s):
    e = jnp.exp(s - lax.stop_gradient(jnp.max(s, axis=-1, keepdims=True)))
    return e / jnp.sum(e, axis=-1, keepdims=True)


def _ret_kv_f(k, v, cos, sin, decf, decb):
    lgf, lgb = jax.nn.log_sigmoid(decf), jax.nn.log_sigmoid(decb)
    kr = _rot(k, cos, sin)
    idx = lax.broadcasted_iota(jnp.int32, (CH, 1), 0).astype(F32)
    r = lax.broadcasted_iota(jnp.int32, (BW, BW), 0) // HEAD
    c = lax.broadcasted_iota(jnp.int32, (BW, BW), 1) // HEAD
    bd = r == c
    kvf = jnp.where(bd, _tn(kr * jnp.exp((CH - 1 - idx) * lgf), v), 0.0)
    kvb = jnp.where(bd, _tn(kr * jnp.exp(idx * lgb), v), 0.0)
    return kvf, kvb


def _ret_scan_f(kvf, kvb, decf, decb):
    n = len(kvf)
    cdf = jnp.exp(CH * jax.nn.log_sigmoid(decf))
    cdb = jnp.exp(CH * jax.nn.log_sigmoid(decb))
    z = jnp.zeros((BW, BW), F32)
    sf, st = [z], z
    for a in range(n - 1):
        st = cdf * st + kvf[a]
        sf.append(st)
    sb, st = [z], z
    for a in range(n - 1, 0, -1):
        st = cdb * st + kvb[a]
        sb.append(st)
    return sf, sb[::-1]


def _ret_out_f(q, k, v, g, cos, sin, stf, stb, decf, decb, gn):
    lgf, lgb = jax.nn.log_sigmoid(decf), jax.nn.log_sigmoid(decb)
    qr = _rot(q, cos, sin) * (HEAD ** -0.5)
    kr = _rot(k, cos, sin)
    diff = (lax.broadcasted_iota(jnp.int32, (CH, CH), 0) - lax.broadcasted_iota(jnp.int32, (CH, CH), 1)).astype(F32)
    decay = []
    for mh in _head_masks():
        lf = jnp.sum(jnp.where(mh, lgf, 0.0), axis=1, keepdims=True) * (1.0 / HEAD)
        lb = jnp.sum(jnp.where(mh, lgb, 0.0), axis=1, keepdims=True) * (1.0 / HEAD)
        decay.append(jnp.exp(jnp.where(diff >= 0, diff * lf, -diff * lb)))
    p = _nt(_stack_heads(qr), kr) * jnp.concatenate(decay, axis=0)
    o = _unstack_heads(_nn(p, v))
    idx = lax.broadcasted_iota(jnp.int32, (CH, 1), 0).astype(F32)
    o = o + _nn(qr * jnp.exp((idx + 1) * lgf), stf) + _nn(qr * jnp.exp((CH - idx) * lgb), stb)
    return _head_rms(o, gn) * (g * jax.nn.sigmoid(g))


def _pool_f(pvp, wbd, scale, t_len):
    n = pvp.shape[0]
    w2 = pvp + _rollr(pvp, 1)
    w4 = _rollr(w2, 1) + _rollr(w2, -1)
    w8 = _rollr(w4, 2) + _rollr(w4, -2)
    w16 = _rollr(w8, 4) + _rollr(w8, -4)
    grp = lax.broadcasted_iota(jnp.int32, (1, BW), 1) // HEAD
    ws = jnp.where(grp == 0, w2, jnp.where(grp == 1, w4, jnp.where(grp == 2, w8, w16)))
    half = jnp.where(grp == 0, POOL_HALF[0], jnp.where(grp == 1, POOL_HALF[1], jnp.where(grp == 2, POOL_HALF[2], POOL_HALF[3])))
    t = lax.broadcasted_iota(jnp.int32, (n, 1), 0) - POOL_PAD
    cnt = jnp.minimum(t + half, t_len) - jnp.maximum(t - half, 0)
    cnt = jnp.where((t >= 0) & (t < t_len), cnt, 1).astype(F32)
    pooled = ws / cnt - pvp
    return _nn(pooled, wbd) * scale


def _na_f(q, kn, vw, bias, gq):
    qn = _head_rms(q, gq)
    bias_all = jnp.concatenate([jnp.concatenate(bias[h], axis=1) for h in range(NH)], axis=0)
    s = _nt(_stack_heads(qn), kn) * (HEAD ** -0.5) + bias_all
    return _unstack_heads(_nn(_softmax(s), vw))


def _mem_f(q, mk, mv, gq, gk):
    qn = _head_rms(q, gq)
    kn = _head_rms(mk, gk)
    s = _nt(_stack_heads(qn), kn) * (HEAD ** -0.5)
    return _unstack_heads(_nn(_softmax(s), mv))


def _gate_f(gp, br, wb):
    out = None
    for n in range(NH):
        t = jax.nn.sigmoid(gp[n]) * _nn(br[n], wb[n])
        out = t if out is None else out + t
    return out


def _swiglu_f(a, g):
    return a * jax.nn.sigmoid(a) * g


MM_ROWS = (1024, 512, 256)
MM_COLS = (1152, 1024, 768, 512)


def _tile(n, prefs):
    for p in prefs:
        if n % p == 0:
            return p
    return n


def _mm_call(name, a, b, extra, dn, grid, a_spec, b_spec, extra_specs, o_spec, out_shape, acc_shape, nred, after=None):
    add = len(extra) == 1
    if after is not None:
        extra, extra_specs = [*extra, after], [*extra_specs, ANY]
    n_extra = len(extra)
    n_steps = int(np.prod([grid[len(grid) - 1 - i] for i in range(nred)]))

    def body(*refs):
        a_ref, b_ref = refs[0], refs[1]
        o_ref, acc = refs[2 + n_extra], refs[-1]
        red = [len(grid) - 1 - i for i in range(nred)]
        first = functools.reduce(jnp.logical_and, [pl.program_id(ax) == 0 for ax in red])
        last = functools.reduce(jnp.logical_and, [pl.program_id(ax) == grid[ax] - 1 for ax in red])
        d = lax.dot_general(a_ref[...].astype(BF16), b_ref[...].astype(BF16), dn, preferred_element_type=F32)

        def finish(r):
            if add:
                r = r + refs[2][...]
            o_ref[...] = r.astype(o_ref.dtype)

        if n_steps == 1:
            finish(d)
        else:
            @pl.when(first)
            def _():
                acc[...] = d

            @pl.when(jnp.logical_not(first) & jnp.logical_not(last))
            def _():
                acc[...] += d

            @pl.when(last)
            def _():
                finish(acc[...] + d)

    return pl.pallas_call(
        body, grid=grid, in_specs=[a_spec, b_spec, *extra_specs], out_specs=o_spec, out_shape=out_shape,
        scratch_shapes=[pltpu.VMEM(acc_shape, F32)] if n_steps > 1 else [], name=name, compiler_params=_cp(VMEM_MM),
    )(a, b, *extra)


def _mm_nn(name, a, w, l, *, out3d=False, add=None, out_dtype=F32):
    m, k = a.shape
    _, s_n, _, ns = w.shape
    tm = _tile(m, MM_ROWS)
    tk = _tile(k, (1024, 1408, 512, 256))
    tn = ns if ns <= 1408 else _tile(ns, MM_COLS)
    grid = (m // tm, s_n, ns // tn, k // tk)
    a_spec = pl.BlockSpec((tm, tk), lambda i, s, j, kk: (i, kk))
    b_spec = pl.BlockSpec((None, None, tk, tn), lambda i, s, j, kk: (l, s, kk, j))
    if out3d:
        o_spec = pl.BlockSpec((None, tm, tn), lambda i, s, j, kk: (s, i, j))
        out_shape = SDS((s_n, m, ns), out_dtype)
    else:
        nj = ns // tn
        o_spec = pl.BlockSpec((tm, tn), lambda i, s, j, kk: (i, s * nj + j))
        out_shape = SDS((m, s_n * ns), out_dtype)
    extra, especs = [], []
    if add is not None:
        assert not out3d
        nj = ns // tn
        extra, especs = [add], [pl.BlockSpec((tm, tn), lambda i, s, j, kk: (i, s * nj + j))]
    return _mm_call(name, a, w, extra, (((1,), (0,)), ((), ())), grid, a_spec, b_spec, especs, o_spec, out_shape, (tm, tn), 1)


def _mm_nt(name, g, w, l, after=None):
    _, s_n, k, ns = w.shape
    g3d = g.ndim == 3
    m = g.shape[1] if g3d else g.shape[0]
    tm = _tile(m, MM_ROWS)
    to = _tile(k, (1024, 1408, 512, 256))
    tk = ns if (ns % LANES or ns <= 1408) else _tile(ns, MM_COLS)
    nks = ns // tk
    grid = (m // tm, k // to, s_n, nks)
    if g3d:
        a_spec = pl.BlockSpec((None, tm, tk), lambda i, j, s, r: (s, i, r))
    else:
        a_spec = pl.BlockSpec((tm, tk), lambda i, j, s, r: (i, s * nks + r))
    b_spec = pl.BlockSpec((None, None, to, tk), lambda i, j, s, r: (l, s, j, r))
    o_spec = pl.BlockSpec((tm, to), lambda i, j, s, r: (i, j))
    return _mm_call(name, g, w, [], (((1,), (1,)), ((), ())), grid, a_spec, b_spec, [], o_spec, SDS((m, k), F32), (tm, to), 2,
                    after=after)


def _mm_tn(name, a, g, s_n):
    t_len, k = a.shape
    g3d = g.ndim == 3
    ns = g.shape[2] if g3d else g.shape[1] // s_n
    tt = _tile(t_len, (2048, 1024, 512))
    to = _tile(k, (1024, 768, 1408, 512, 256, 128))
    tn = ns
    nj = ns // tn
    grid = (s_n, k // to, nj, t_len // tt)
    a_spec = pl.BlockSpec((tt, to), lambda s, i, j, t: (t, i))
    if g3d:
        b_spec = pl.BlockSpec((None, tt, tn), lambda s, i, j, t: (s, t, j))
    else:
        b_spec = pl.BlockSpec((tt, tn), lambda s, i, j, t: (t, s * nj + j))
    o_spec = pl.BlockSpec((None, to, tn), lambda s, i, j, t: (s, i, j))
    return _mm_call(name, a, g, [], (((0,), (0,)), ((), ())), grid, a_spec, b_spec, [], o_spec,
                    SDS((s_n, k, ns), F32), (to, tn), 1)


def _rms_fwd(name, x, g):
    m, d = x.shape
    tm = _tile(m, (512, 256))

    def body(x_ref, g_ref, o_ref):
        o_ref[...] = _rms(x_ref[...], g_ref[...]).astype(BF16)

    row = pl.BlockSpec((tm, d), lambda i: (i, 0))
    return pl.pallas_call(body, grid=(m // tm,), in_specs=[row, pl.BlockSpec((1, d), lambda i: (0, 0))], out_specs=row,
                          out_shape=SDS((m, d), BF16), name=name, compiler_params=_cp(VMEM_MM))(x, g)


def _rms_bwd(name, x, g, dhs, add=None):
    m, d = x.shape
    tm = _tile(m, (512, 256))
    nh = len(dhs)

    def body(*refs):
        x_ref, g_ref = refs[0], refs[1]
        dh = refs[2][...]
        for r in refs[3:2 + nh]:
            dh = dh + r[...]
        dx_ref, dg_ref = refs[-2], refs[-1]
        _, vjp = jax.vjp(_rms, x_ref[...], g_ref[...])
        dx, dg = vjp(dh)
        if add is not None:
            dx = dx + refs[2 + nh][...]
        dx_ref[...] = dx

        @pl.when(pl.program_id(0) == 0)
        def _():
            dg_ref[...] = jnp.zeros_like(dg_ref)

        dg_ref[...] += dg

    row = pl.BlockSpec((tm, d), lambda i: (i, 0))
    vec = pl.BlockSpec((1, d), lambda i: (0, 0))
    ins = [x, g, *dhs] + ([add] if add is not None else [])
    return pl.pallas_call(body, grid=(m // tm,), in_specs=[row, vec] + [row] * (len(ins) - 2), out_specs=[row, vec],
                          out_shape=[SDS((m, d), F32), SDS((1, d), F32)], name=name, compiler_params=_cp(VMEM_MM))(*ins)


def _swiglu_fwd(name, ag3):
    _, t_len, w = ag3.shape
    tm = _tile(t_len, (512, 256))

    def body(a_ref, g_ref, o_ref):
        o_ref[...] = _swiglu_f(a_ref[...], g_ref[...]).astype(BF16)

    return pl.pallas_call(
        body, grid=(t_len // tm, 2),
        in_specs=[pl.BlockSpec((None, tm, w), lambda i, j: (j, i, 0)), pl.BlockSpec((None, tm, w), lambda i, j: (j + 2, i, 0))],
        out_specs=pl.BlockSpec((tm, w), lambda i, j: (i, j)), out_shape=SDS((t_len, 2 * w), BF16), name=name,
        compiler_params=_cp(VMEM_MM))(ag3, ag3)


def _swiglu_bwd(name, ag3, du):
    _, t_len, w = ag3.shape
    tm = _tile(t_len, (512, 256))

    def body(a_ref, g_ref, du_ref, o_ref):
        _, vjp = jax.vjp(_swiglu_f, a_ref[...], g_ref[...])
        da, dg = vjp(du_ref[...])
        o_ref[0] = da.astype(BF16)
        o_ref[1] = dg.astype(BF16)

    out = pl.pallas_call(
        body, grid=(t_len // tm, 2),
        in_specs=[pl.BlockSpec((None, tm, w), lambda i, j: (j, i, 0)), pl.BlockSpec((None, tm, w), lambda i, j: (j + 2, i, 0)),
                  pl.BlockSpec((tm, w), lambda i, j: (i, j))],
        out_specs=pl.BlockSpec((2, None, tm, w), lambda i, j: (0, j, i, 0)), out_shape=SDS((2, 2, t_len, w), BF16), name=name,
        compiler_params=_cp(VMEM_MM))(ag3, ag3, du)
    return out.reshape(ag3.shape)


def _gate_fwd(name, gp3, br, wb, l):
    _, t_len, d = gp3.shape
    tm = _tile(t_len, MM_ROWS)

    def body(gp_ref, br_ref, wb_ref, o_ref):
        o_ref[...] = _gate_f([gp_ref[n] for n in range(NH)], [br_ref[n] for n in range(NH)],
                             [wb_ref[n] for n in range(NH)]).astype(BF16)

    return pl.pallas_call(
        body, grid=(t_len // tm, d // BW),
        in_specs=[pl.BlockSpec((NH, tm, BW), lambda i, s: (0, i, s)), pl.BlockSpec((NH, tm, BW), lambda i, s: (0, i, 0)),
                  pl.BlockSpec((None, None, NH, BW, BW), lambda i, s: (l, s, 0, 0, 0))],
        out_specs=pl.BlockSpec((tm, BW), lambda i, s: (i, s)), out_shape=SDS((t_len, d), BF16), name=name,
        compiler_params=_cp(VMEM_MM))(gp3, br, wb)


def _gate_bwd(name, gp3, br, wb, l, dmerged):
    _, t_len, d = gp3.shape
    tm = _tile(t_len, (512, 256))
    ns = d // BW

    def body(gp_ref, br_ref, wb_ref, dm_ref, dgp_ref, dbr_ref, dwb_ref):
        i, s = pl.program_id(0), pl.program_id(1)
        gp = [gp_ref[n] for n in range(NH)]
        brv = [br_ref[n].astype(F32) for n in range(NH)]
        wbv = [wb_ref[n].astype(F32) for n in range(NH)]
        _, vjp = jax.vjp(_gate_f, gp, brv, wbv)
        dgp, dbr, dwb = vjp(dm_ref[...])

        @pl.when(s == 0)
        def _():
            dbr_ref[...] = jnp.zeros_like(dbr_ref)

        @pl.when((i == 0) & (s == 0))
        def _():
            dwb_ref[...] = jnp.zeros_like(dwb_ref)

        for n in range(NH):
            dgp_ref[n] = dgp[n].astype(BF16)
            dbr_ref[n] += dbr[n]
            dwb_ref[s, n] += dwb[n]

    return pl.pallas_call(
        body, grid=(t_len // tm, ns),
        in_specs=[pl.BlockSpec((NH, tm, BW), lambda i, s: (0, i, s)), pl.BlockSpec((NH, tm, BW), lambda i, s: (0, i, 0)),
                  pl.BlockSpec((None, None, NH, BW, BW), lambda i, s: (l, s, 0, 0, 0)), pl.BlockSpec((tm, BW), lambda i, s: (i, s))],
        out_specs=[pl.BlockSpec((NH, tm, BW), lambda i, s: (0, i, s)), pl.BlockSpec((NH, tm, BW), lambda i, s: (0, i, 0)),
                   pl.BlockSpec((ns, NH, BW, BW), lambda i, s: (0, 0, 0, 0))],
        out_shape=[SDS(gp3.shape, BF16), SDS((NH, t_len, BW), F32), SDS((ns, NH, BW, BW), F32)], name=name,
        compiler_params=_cp(VMEM_MM))(gp3, br, wb, dmerged)


def _loss_kernel(name, y, tgt):
    m, d = y.shape
    tm = _tile(m, (512, 256))

    def body(y_ref, t_ref, l_ref, dy_ref):
        err = y_ref[...] - t_ref[...]
        dy_ref[...] = err * (1.0 / d)

        @pl.when(pl.program_id(0) == 0)
        def _():
            l_ref[...] = jnp.zeros_like(l_ref)

        l_ref[...] += 0.5 * jnp.sum(jnp.mean(err * err, axis=-1, keepdims=True), axis=0, keepdims=True)

    row = pl.BlockSpec((tm, d), lambda i: (i, 0))
    return pl.pallas_call(body, grid=(m // tm,), in_specs=[row, row], out_specs=[pl.BlockSpec((1, 1), lambda i: (0, 0)), row],
                          out_shape=[SDS((1, 1), F32), SDS((m, d), F32)], name=name)(y, tgt)


def _zero_first(refs):
    @pl.when(pl.program_id(0) == 0)
    def _():
        for r in refs:
            r[...] = jnp.zeros_like(r)


def _grp(rows, j):
    return pl.BlockSpec((rows, BW), lambda a, j=j: (a, j))


_VEC = pl.BlockSpec((1, BW), lambda a: (0, 0))


def _ret_fwd(tag, proj, cos, sin, decf, decb, gn):
    t_len = proj.shape[0]
    n = t_len // CH
    tab = pl.BlockSpec((CH, BW), lambda a: (a, 0))
    st = pl.BlockSpec((None, BW, BW), lambda a: (a, 0, 0))

    def k1(k_ref, v_ref, cos_ref, sin_ref, df_ref, db_ref, kvf_ref, kvb_ref):
        kvf_ref[...], kvb_ref[...] = _ret_kv_f(k_ref[...], v_ref[...], cos_ref[...], sin_ref[...], df_ref[...], db_ref[...])

    kvf, kvb = pl.pallas_call(
        k1, grid=(n,), in_specs=[_grp(CH, 1), _grp(CH, 2), tab, tab, _VEC, _VEC], out_specs=[st, st],
        out_shape=[SDS((n, BW, BW), F32)] * 2, name=f"ret_kv_{tag}")(proj, proj, cos, sin, decf, decb)

    def k2(kvf_ref, kvb_ref, df_ref, db_ref, sf_ref, sb_ref):
        sf, sb = _ret_scan_f([kvf_ref[a] for a in range(n)], [kvb_ref[a] for a in range(n)], df_ref[...], db_ref[...])
        for a in range(n):
            sf_ref[a] = sf[a]
            sb_ref[a] = sb[a]

    stf, stb = pl.pallas_call(k2, out_shape=[SDS((n, BW, BW), F32)] * 2, name=f"ret_scan_{tag}",
                              compiler_params=_cp(VMEM_BIG))(kvf, kvb, decf, decb)

    def k3(q_ref, k_ref, v_ref, g_ref, cos_ref, sin_ref, sf_ref, sb_ref, df_ref, db_ref, gn_ref, o_ref):
        o_ref[...] = _ret_out_f(q_ref[...], k_ref[...], v_ref[...], g_ref[...], cos_ref[...], sin_ref[...], sf_ref[...],
                                sb_ref[...], df_ref[...], db_ref[...], gn_ref[...]).astype(BF16)

    ret = pl.pallas_call(
        k3, grid=(n,), in_specs=[_grp(CH, 0), _grp(CH, 1), _grp(CH, 2), _grp(CH, 3), tab, tab, st, st, _VEC, _VEC, _VEC],
        out_specs=tab, out_shape=SDS((t_len, BW), BF16), name=f"ret_out_{tag}")(proj, proj, proj, proj, cos, sin, stf, stb, decf, decb, gn)
    return ret, (kvf, kvb, stf, stb)


def _ret_bwd(tag, proj, cos, sin, decf, decb, gn, saved, dret):
    kvf, kvb, stf, stb = saved
    t_len = proj.shape[0]
    n = t_len // CH
    tab = pl.BlockSpec((CH, BW), lambda a: (a, 0))
    st = pl.BlockSpec((None, BW, BW), lambda a: (a, 0, 0))

    def k3(q_ref, k_ref, v_ref, g_ref, cos_ref, sin_ref, sf_ref, sb_ref, df_ref, db_ref, gn_ref, do_ref,
           dq_ref, dk_ref, dv_ref, dg_ref, dsf_ref, dsb_ref, ddf_ref, ddb_ref, dgn_ref):
        cos, sin = cos_ref[...], sin_ref[...]
        f = lambda q, k, v, g, sf, sb, df, db, gnv: _ret_out_f(q, k, v, g, cos, sin, sf, sb, df, db, gnv)
        _, vjp = jax.vjp(f, q_ref[...], k_ref[...], v_ref[...], g_ref[...], sf_ref[...], sb_ref[...], df_ref[...], db_ref[...], gn_ref[...])
        dq, dk, dv, dg, dsf, dsb, ddf, ddb, dgn = vjp(do_ref[...])
        dq_ref[...] = dq.astype(BF16)
        dk_ref[...] = dk
        dv_ref[...] = dv
        dg_ref[...] = dg.astype(BF16)
        dsf_ref[...] = dsf
        dsb_ref[...] = dsb
        _zero_first([ddf_ref, ddb_ref, dgn_ref])
        ddf_ref[...] += ddf
        ddb_ref[...] += ddb
        dgn_ref[...] += dgn

    dq, dk3, dv3, dg, dstf, dstb, ddf3, ddb3, dgn = pl.pallas_call(
        k3, grid=(n,),
        in_specs=[_grp(CH, 0), _grp(CH, 1), _grp(CH, 2), _grp(CH, 3), tab, tab, st, st, _VEC, _VEC, _VEC, tab],
        out_specs=[tab, tab, tab, tab, st, st, _VEC, _VEC, _VEC],
        out_shape=[SDS((t_len, BW), BF16), SDS((t_len, BW), F32), SDS((t_len, BW), F32), SDS((t_len, BW), BF16),
                   SDS((n, BW, BW), F32), SDS((n, BW, BW), F32), SDS((1, BW), F32), SDS((1, BW), F32), SDS((1, BW), F32)],
        name=f"ret_out_bwd_{tag}")(proj, proj, proj, proj, cos, sin, stf, stb, decf, decb, gn, dret)

    def k2(kvf_ref, kvb_ref, df_ref, db_ref, dsf_ref, dsb_ref, dkvf_ref, dkvb_ref, ddf_ref, ddb_ref):
        _, vjp = jax.vjp(_ret_scan_f, [kvf_ref[a] for a in range(n)], [kvb_ref[a] for a in range(n)], df_ref[...], db_ref[...])
        dkvf, dkvb, ddf_ref[...], ddb_ref[...] = vjp(([dsf_ref[a] for a in range(n)], [dsb_ref[a] for a in range(n)]))
        for a in range(n):
            dkvf_ref[a] = dkvf[a]
            dkvb_ref[a] = dkvb[a]

    dkvf, dkvb, ddf2, ddb2 = pl.pallas_call(
        k2, out_shape=[SDS((n, BW, BW), F32), SDS((n, BW, BW), F32), SDS((1, BW), F32), SDS((1, BW), F32)],
        name=f"ret_scan_bwd_{tag}", compiler_params=_cp(VMEM_BIG))(kvf, kvb, decf, decb, dstf, dstb)

    def k1(k_ref, v_ref, cos_ref, sin_ref, df_ref, db_ref, dkvf_ref, dkvb_ref, dk3_ref, dv3_ref, dk_ref, dv_ref, ddf_ref, ddb_ref):
        cos, sin = cos_ref[...], sin_ref[...]
        f = lambda k, v, df, db: _ret_kv_f(k, v, cos, sin, df, db)
        _, vjp = jax.vjp(f, k_ref[...], v_ref[...], df_ref[...], db_ref[...])
        dk, dv, ddf, ddb = vjp((dkvf_ref[...], dkvb_ref[...]))
        dk_ref[...] = (dk + dk3_ref[...]).astype(BF16)
        dv_ref[...] = (dv + dv3_ref[...]).astype(BF16)
        _zero_first([ddf_ref, ddb_ref])
        ddf_ref[...] += ddf
        ddb_ref[...] += ddb

    dk, dv, ddf1, ddb1 = pl.pallas_call(
        k1, grid=(n,), in_specs=[_grp(CH, 1), _grp(CH, 2), tab, tab, _VEC, _VEC, st, st, tab, tab],
        out_specs=[tab, tab, _VEC, _VEC],
        out_shape=[SDS((t_len, BW), BF16), SDS((t_len, BW), BF16), SDS((1, BW), F32), SDS((1, BW), F32)],
        name=f"ret_kv_bwd_{tag}")(proj, proj, cos, sin, decf, decb, dkvf, dkvb, dk3, dv3)
    return dq, dk, dv, dg, ddf1 + ddf2 + ddf3, ddb1 + ddb2 + ddb3, dgn


def _pool_fwd(tag, pvp, wbd, scale, t_len):
    def body(p_ref, w_ref, s_ref, o_ref):
        o_ref[...] = _pool_f(p_ref[...], w_ref[...], s_ref[...], t_len).astype(BF16)

    return pl.pallas_call(body, out_shape=SDS(pvp.shape, BF16), name=f"pool_{tag}", compiler_params=_cp(VMEM_BIG))(pvp, wbd, scale)


def _pool_bwd(tag, pvp, wbd, scale, t_len, dpool_p):
    def body(p_ref, w_ref, s_ref, do_ref, dp_ref, dw_ref, ds_ref):
        f = lambda p, w, s: _pool_f(p, w, s, t_len)
        _, vjp = jax.vjp(f, p_ref[...], w_ref[...], s_ref[...])
        dp, dw, ds = vjp(do_ref[...])
        dp_ref[...] = dp.astype(BF16)
        dw_ref[...] = dw
        ds_ref[...] = ds

    return pl.pallas_call(body, out_shape=[SDS(pvp.shape, BF16), SDS((BW, BW), F32), SDS((1, BW), F32)],
                          name=f"pool_bwd_{tag}", compiler_params=_cp(VMEM_BIG))(pvp, wbd, scale, dpool_p)


def _na_window(r, rows):
    r0 = jnp.clip(r - NA_ROWS // 2, 0, rows - NA_ROWS)
    return r0, r - r0


NA_PAIRS = 2 * NA_ROWS - 2


def _na_bias_pieces(b_ref, pat):
    return [[b_ref[h * NA_PAIRS + 2 * k - pat + NA_ROWS - 1] for k in range(NA_ROWS // 2)] for h in range(NH)]


def _na_keys(tag, proj, gk):
    t_len = proj.shape[0]
    tm = _tile(t_len, (256,))

    def body(k_ref, v_ref, gk_ref, kn_ref, vb_ref):
        kn_ref[...] = _head_rms(k_ref[...], gk_ref[...]).astype(BF16)
        vb_ref[...] = v_ref[...].astype(BF16)

    row = pl.BlockSpec((tm, BW), lambda a: (a, 0))
    return pl.pallas_call(body, grid=(t_len // tm,), in_specs=[_grp(tm, 6), _grp(tm, 7), _VEC], out_specs=[row, row],
                          out_shape=[SDS((t_len, BW), BF16)] * 2, name=f"na_keys_{tag}", compiler_params=_cp(VMEM_MM))(proj, proj, gk)


def _na_keys_bwd(tag, proj, gk, dkn, dv):
    t_len = proj.shape[0]
    tm = _tile(t_len, (256,))

    def body(k_ref, gk_ref, dkn_ref, dv_ref, dk_ref, dvb_ref, dgk_ref):
        _, vjp = jax.vjp(_head_rms, k_ref[...], gk_ref[...])
        dk, dgk = vjp(dkn_ref[...])
        dk_ref[...] = dk.astype(BF16)
        dvb_ref[...] = dv_ref[...].astype(BF16)
        _zero_first([dgk_ref])
        dgk_ref[...] += dgk

    row = pl.BlockSpec((tm, BW), lambda a: (a, 0))
    return pl.pallas_call(body, grid=(t_len // tm,), in_specs=[_grp(tm, 6), _VEC, row, row], out_specs=[row, row, _VEC],
                          out_shape=[SDS((t_len, BW), BF16), SDS((t_len, BW), BF16), SDS((1, BW), F32)],
                          name=f"na_keys_bwd_{tag}", compiler_params=_cp(VMEM_MM))(proj, gk, dkn, dv)


def _na_fwd(tag, proj, kn, vb, bias_tab, gq):
    t_len = proj.shape[0]
    rows = t_len // GRID_W
    win = NA_ROWS * GRID_W

    def body(q_ref, k_ref, v_ref, b_ref, gq_ref, o_ref):
        r0, pat = _na_window(pl.program_id(0), rows)
        start = pl.multiple_of(r0 * GRID_W, GRID_W)
        o_ref[...] = _na_f(q_ref[...], k_ref[pl.ds(start, win), :], v_ref[pl.ds(start, win), :], _na_bias_pieces(b_ref, pat),
                           gq_ref[...]).astype(BF16)

    whole = pl.BlockSpec((t_len, BW), lambda a: (0, 0))
    return pl.pallas_call(
        body, grid=(rows,),
        in_specs=[_grp(GRID_W, 5), whole, whole, pl.BlockSpec(bias_tab.shape, lambda a: (0, 0, 0)), _VEC],
        out_specs=pl.BlockSpec((GRID_W, BW), lambda a: (a, 0)), out_shape=SDS((t_len, BW), BF16),
        name=f"na_{tag}", compiler_params=_cp(VMEM_MM))(proj, kn, vb, bias_tab, gq)


def _na_bwd(tag, proj, kn, vb, bias_tab, gq, dna):
    t_len = proj.shape[0]
    rows = t_len // GRID_W
    win = NA_ROWS * GRID_W

    def body(q_ref, k_ref, v_ref, b_ref, gq_ref, do_ref, dq_ref, dk_ref, dv_ref, db_ref, dgq_ref):
        r0, pat = _na_window(pl.program_id(0), rows)
        start = pl.multiple_of(r0 * GRID_W, GRID_W)
        _zero_first([dk_ref, dv_ref, db_ref, dgq_ref])
        _, vjp = jax.vjp(_na_f, q_ref[...], k_ref[pl.ds(start, win), :].astype(F32), v_ref[pl.ds(start, win), :].astype(F32),
                         _na_bias_pieces(b_ref, pat), gq_ref[...])
        dq, dk, dv, db, dgq = vjp(do_ref[...])
        dq_ref[...] = dq.astype(BF16)
        dk_ref[pl.ds(start, win), :] += dk
        dv_ref[pl.ds(start, win), :] += dv
        for h in range(NH):
            for k in range(NA_ROWS // 2):
                db_ref[h * NA_PAIRS + 2 * k - pat + NA_ROWS - 1] += db[h][k]
        dgq_ref[...] += dgq

    whole = pl.BlockSpec((t_len, BW), lambda a: (0, 0))
    tabspec = pl.BlockSpec(bias_tab.shape, lambda a: (0, 0, 0))
    row = pl.BlockSpec((GRID_W, BW), lambda a: (a, 0))
    return pl.pallas_call(
        body, grid=(rows,), in_specs=[_grp(GRID_W, 5), whole, whole, tabspec, _VEC, row],
        out_specs=[row, whole, whole, tabspec, _VEC],
        out_shape=[SDS((t_len, BW), BF16), SDS((t_len, BW), F32), SDS((t_len, BW), F32), SDS(bias_tab.shape, F32), SDS((1, BW), F32)],
        name=f"na_bwd_{tag}", compiler_params=_cp(VMEM_BIG))(proj, kn, vb, bias_tab, gq, dna)


def _mem_fwd(tag, proj, memkv, gq, gk):
    t_len = proj.shape[0]
    n_mem = memkv.shape[0]
    tm = _tile(t_len, (256,))

    def body(q_ref, mk_ref, mv_ref, gq_ref, gk_ref, o_ref):
        o_ref[...] = _mem_f(q_ref[...], mk_ref[...], mv_ref[...], gq_ref[...], gk_ref[...]).astype(BF16)

    mspec = lambda j: pl.BlockSpec((n_mem, BW), lambda a, j=j: (0, j))
    return pl.pallas_call(
        body, grid=(t_len // tm,), in_specs=[_grp(tm, 8), mspec(0), mspec(1), _VEC, _VEC],
        out_specs=pl.BlockSpec((tm, BW), lambda a: (a, 0)), out_shape=SDS((t_len, BW), BF16), name=f"mem_{tag}")(proj, memkv, memkv, gq, gk)


def _mem_bwd(tag, proj, memkv, gq, gk, dmo):
    t_len = proj.shape[0]
    n_mem = memkv.shape[0]
    tm = _tile(t_len, (256,))

    def body(q_ref, mk_ref, mv_ref, gq_ref, gk_ref, do_ref, dq_ref, dmk_ref, dmv_ref, dgq_ref, dgk_ref):
        _zero_first([dmk_ref, dmv_ref, dgq_ref, dgk_ref])
        _, vjp = jax.vjp(_mem_f, q_ref[...], mk_ref[...], mv_ref[...], gq_ref[...], gk_ref[...])
        dq, dmk, dmv, dgq, dgk = vjp(do_ref[...])
        dq_ref[...] = dq.astype(BF16)
        dmk_ref[...] += dmk
        dmv_ref[...] += dmv
        dgq_ref[...] += dgq
        dgk_ref[...] += dgk

    mspec = lambda j: pl.BlockSpec((n_mem, BW), lambda a, j=j: (0, j))
    mout = pl.BlockSpec((n_mem, BW), lambda a: (0, 0))
    row = pl.BlockSpec((tm, BW), lambda a: (a, 0))
    dq, dmk, dmv, dgq, dgk = pl.pallas_call(
        body, grid=(t_len // tm,), in_specs=[_grp(tm, 8), mspec(0), mspec(1), _VEC, _VEC, row],
        out_specs=[row, mout, mout, _VEC, _VEC],
        out_shape=[SDS((t_len, BW), BF16), SDS((n_mem, BW), F32), SDS((n_mem, BW), F32), SDS((1, BW), F32), SDS((1, BW), F32)],
        name=f"mem_bwd_{tag}")(proj, memkv, memkv, gq, gk, dmo)
    return dq, jnp.concatenate([dmk, dmv], axis=1), dgq, dgk


NA_NJ = 2 * NA_COLS - 1


def _na_onehot():
    q = np.arange(GRID_W)[:, None]
    kc = np.arange(GRID_W)[None, :]
    qwin = np.clip(q - NA_COLS // 2, 0, GRID_W - NA_COLS)
    mask = (kc >= qwin) & (kc < qwin + NA_COLS)
    col = np.clip(kc - q, -(NA_COLS - 1), NA_COLS - 1) + NA_COLS - 1
    onehot = np.zeros((LANES, GRID_W, 2, GRID_W), np.float32)
    qq, kk = np.nonzero(mask)
    for half in range(2):
        onehot[half * NA_NJ + col[qq, kk], qq, half, kk] = 1.0
    valid = np.broadcast_to(mask[:, None, :], (GRID_W, 2, GRID_W)).astype(np.float32)
    return onehot.reshape(LANES, -1), valid.reshape(1, -1)


def _na_pair_rows(rpb):
    n_layers = rpb.shape[0]
    pair = jnp.concatenate([rpb[:, :, :-1], rpb[:, :, 1:]], axis=-1).reshape(n_layers, NH * NA_PAIRS, 2 * NA_NJ)
    return jnp.pad(pair, ((0, 0), (0, 0), (0, LANES - 2 * NA_NJ)))


def _na_bias_table(rpb):
    n_layers = rpb.shape[0]
    onehot, valid = _na_onehot()
    width = onehot.shape[1]

    def body(r_ref, oh_ref, ok_ref, o_ref):
        t = jnp.dot(r_ref[...], oh_ref[...], precision=lax.Precision.HIGHEST, preferred_element_type=F32)
        o_ref[...] = jnp.where(ok_ref[...] > 0, t, NEG_INF)

    out = pl.pallas_call(
        body, grid=(n_layers,),
        in_specs=[pl.BlockSpec((None, NH * NA_PAIRS, LANES), lambda l: (l, 0, 0)), pl.BlockSpec((LANES, width), lambda l: (0, 0)),
                  pl.BlockSpec((1, width), lambda l: (0, 0))],
        out_specs=pl.BlockSpec((None, NH * NA_PAIRS, width), lambda l: (l, 0, 0)),
        out_shape=SDS((n_layers, NH * NA_PAIRS, width), F32), name="na_bias_table")(
            _na_pair_rows(rpb), jnp.asarray(onehot), jnp.asarray(valid))
    return out.reshape(n_layers, NH * NA_PAIRS, GRID_W, 2 * GRID_W)


def _na_bias_grad(dtab):
    n_layers = dtab.shape[0]
    onehot, _ = _na_onehot()
    width = onehot.shape[1]

    def body(x_ref, oh_ref, o_ref):
        o_ref[...] = jnp.dot(x_ref[...], oh_ref[...], precision=lax.Precision.HIGHEST, preferred_element_type=F32)

    out = pl.pallas_call(
        body, grid=(n_layers,),
        in_specs=[pl.BlockSpec((None, NH * NA_PAIRS, width), lambda l: (l, 0, 0)), pl.BlockSpec((width, LANES), lambda l: (0, 0))],
        out_specs=pl.BlockSpec((None, NH * NA_PAIRS, LANES), lambda l: (l, 0, 0)),
        out_shape=SDS((n_layers, NH * NA_PAIRS, LANES), F32), name="na_bias_grad")(
            dtab.reshape(n_layers, NH * NA_PAIRS, width), jnp.asarray(onehot.T.copy()))
    out = out.reshape(n_layers, NH, NA_PAIRS, LANES)
    zero = jnp.zeros((n_layers, NH, 1, NA_NJ), F32)
    return (jnp.concatenate([out[..., :NA_NJ], zero], axis=2) + jnp.concatenate([zero, out[..., NA_NJ:2 * NA_NJ]], axis=2))


def _place():
    x, y, c = lax.axis_index("x"), lax.axis_index("y"), lax.axis_index("c")
    chips = [(1 - x, y), (x, 1 - y), (1 - x, 1 - y)]
    return x, y, c, chips


def _remote(src, dst, ssem, rsem, dev):
    return pltpu.make_async_remote_copy(src_ref=src, dst_ref=dst, send_sem=ssem, recv_sem=rsem, device_id=dev, device_id_type=MESH)


HBM = pl.BlockSpec(memory_space=pltpu.HBM)
SEM = pl.BlockSpec(memory_space=pltpu.SEMAPHORE)
DATAFLOW = pltpu.SideEffectType.DATAFLOW_SIDE_EFFECTING


def _split_copies(name, srcs, lands, n_copies, plan):
    n_s, n_l = len(srcs), len(lands)
    hbm = lambda a: pltpu.HBM(a.shape, a.dtype)

    def start_body(*refs):
        ins = refs[:n_s + n_l]
        outs = refs[n_s + n_l:]
        sems, token = outs[:2 * n_copies], outs[-1]
        for k, (src, dst, dev) in enumerate(plan(ins[:n_s], ins[n_s:])):
            _remote(src, dst, sems[k], sems[n_copies + k], dev).start()
        token[...] = jnp.zeros_like(token)

    outs = pl.pallas_call(
        start_body, name=f"{name}_start",
        out_shape=(pltpu.SemaphoreType.DMA(()),) * (2 * n_copies) + tuple(hbm(a) for a in (*srcs, *lands)) + (SDS((8, LANES), F32),),
        in_specs=(HBM,) * (n_s + n_l), out_specs=(SEM,) * (2 * n_copies) + (HBM,) * (n_s + n_l) + (pl.BlockSpec(memory_space=pltpu.VMEM),),
        input_output_aliases={i: 2 * n_copies + i for i in range(n_s + n_l)},
        compiler_params=pltpu.CompilerParams(has_side_effects=DATAFLOW),
    )(*[pltpu.with_memory_space_constraint(a, pltpu.HBM) for a in (*srcs, *lands)])
    sems, thru, token = outs[:2 * n_copies], outs[2 * n_copies:-1], outs[-1]

    def wait(after_wait):
        def wait_body(*refs):
            ins = refs[:n_s + n_l]
            sem_refs = refs[n_s + n_l:n_s + n_l + 2 * n_copies]
            for k, (src, dst, dev) in enumerate(plan(ins[:n_s], ins[n_s:])):
                cp = _remote(src, dst, sem_refs[k], sem_refs[n_copies + k], dev)
                cp.wait_send()
                cp.wait_recv()

        res = pl.pallas_call(
            wait_body, name=f"{name}_wait", out_shape=tuple(hbm(a) for a in (*srcs, *lands)),
            in_specs=(HBM,) * (n_s + n_l) + (SEM,) * (2 * n_copies) + (ANY,), out_specs=(HBM,) * (n_s + n_l),
            input_output_aliases={i: i for i in range(n_s + n_l)},
            compiler_params=pltpu.CompilerParams(has_side_effects=DATAFLOW),
        )(*thru, *sems, after_wait)
        return list(res[:n_s]), list(res[n_s:])

    return token, wait


def _gather_plan(srcs, lands):
    x, y, c, chips = _place()
    me = 2 * x + y
    out = []
    for src, land in zip(srcs, lands):
        h = src.shape[0] // 2
        out.append((src, land.at[me], (x, y, 1 - c)))
        for px, py in chips:
            out.append((src.at[pl.ds(c * h, h)], land.at[me, pl.ds(c * h, h)], (px, py, c)))
    return out


def _scatter_plan(srcs, lands):
    x, y, c, chips = _place()
    out = []
    for src, land in zip(srcs, lands):
        for r, (px, py) in enumerate(chips):
            out.append((src.at[2 * px + py], land.at[r], (px, py, c)))
    return out


def _pass_plan(srcs, lands):
    x, y, c, chips = _place()
    out = []
    for land in lands:
        h = land.shape[1] // 2
        for px, py in chips:
            piece = land.at[2 * px + py, pl.ds(c * h, h)]
            out.append((piece, piece, (x, y, 1 - c)))
    return out


def _pair_plan(srcs, lands):
    x, y, c, _ = _place()
    out = []
    for src, land in zip(srcs, lands):
        h = land.shape[1]
        out.append((src.at[:, pl.ds((1 - c) * h, h)], land, (x, y, 1 - c)))
    return out


def _half_plan(srcs, lands):
    x, y, c, _ = _place()
    out = []
    for land in lands:
        h = land.shape[0] // 2
        mine = land.at[pl.ds(c * h, h)]
        out.append((mine, mine, (x, y, 1 - c)))
    return out


def _forward_halves(name, lands):
    n = len(lands)

    def body(*refs):
        ins, outs = refs[:n], refs[n:2 * n]
        ssem, rsem = refs[2 * n:]
        x, y, c, chips = _place()
        sib = (x, y, 1 - c)

        def copy(i, r, half):
            h = ins[i].shape[1] // 2
            px, py = chips[r]
            rows = pl.ds(half * h, h)
            return _remote(ins[i].at[2 * px + py, rows], outs[i].at[2 * px + py, rows], ssem.at[i, r], rsem.at[i, r], sib)

        for i in range(n):
            for r in range(3):
                copy(i, r, c).start()
        for i in range(n):
            for r in range(3):
                copy(i, r, 1 - c).wait_recv()
                copy(i, r, c).wait_send()

    return pl.pallas_call(
        body, in_specs=[ANY] * n, out_specs=[ANY] * n, out_shape=[SDS(a.shape, a.dtype) for a in lands],
        input_output_aliases={i: i for i in range(n)},
        scratch_shapes=[pltpu.SemaphoreType.DMA((n, 3)), pltpu.SemaphoreType.DMA((n, 3))], name=name)(*lands)


def _pair_exchange(name, gs):
    n = len(gs)

    def body(*refs):
        ins, outs = refs[:n], refs[n:2 * n]
        ssem, rsem = refs[2 * n:]
        x, y, c, _ = _place()
        cps = []
        for i in range(n):
            h = ins[i].shape[1] // 2
            cps.append(_remote(ins[i].at[:, pl.ds((1 - c) * h, h)], outs[i], ssem.at[i], rsem.at[i], (x, y, 1 - c)))
            cps[-1].start()
        for cp in cps:
            cp.wait()

    return pl.pallas_call(
        body, in_specs=[ANY] * n, out_specs=[ANY] * n,
        out_shape=[SDS((g.shape[0], g.shape[1] // 2, g.shape[2]), g.dtype) for g in gs],
        scratch_shapes=[pltpu.SemaphoreType.DMA((n,)), pltpu.SemaphoreType.DMA((n,))], name=name)(*gs)


def _half_exchange(name, fs):
    n = len(fs)

    def body(*refs):
        ins, outs = refs[:n], refs[n:2 * n]
        ssem, rsem = refs[2 * n:]
        x, y, c, _ = _place()

        def copy(i, half):
            h = ins[i].shape[0] // 2
            return _remote(ins[i].at[pl.ds(half * h, h)], outs[i].at[pl.ds(half * h, h)], ssem.at[i], rsem.at[i], (x, y, 1 - c))

        for i in range(n):
            copy(i, c).start()
        for i in range(n):
            copy(i, 1 - c).wait_recv()
            copy(i, c).wait_send()

    return pl.pallas_call(
        body, in_specs=[ANY] * n, out_specs=[ANY] * n, out_shape=[SDS(f.shape, f.dtype) for f in fs],
        input_output_aliases={i: i for i in range(n)},
        scratch_shapes=[pltpu.SemaphoreType.DMA((n,)), pltpu.SemaphoreType.DMA((n,))], name=name)(*fs)


def _gather_small(v):
    def body(v_ref, o_ref, lsem, ssem, rsem):
        x, y, c, _ = _place()
        me = 4 * x + 2 * y + c
        flips = [(fx, fy, fc) for fx in (0, 1) for fy in (0, 1) for fc in (0, 1)][1:]
        peer = lambda f: (1 - x if f[0] else x, 1 - y if f[1] else y, 1 - c if f[2] else c)
        loc = pltpu.make_async_copy(v_ref, o_ref.at[me], lsem)
        loc.start()
        cps = [_remote(v_ref, o_ref.at[me], ssem.at[k], rsem.at[k], peer(f)) for k, f in enumerate(flips)]
        for cp in cps:
            cp.start()
        for k, f in enumerate(flips):
            px, py, pc = peer(f)
            land = o_ref.at[4 * px + 2 * py + pc]
            _remote(land, land, ssem.at[k], rsem.at[k], peer(f)).wait_recv()
        for cp in cps:
            cp.wait_send()
        loc.wait()

    return pl.pallas_call(
        body, in_specs=[ANY], out_specs=ANY, out_shape=SDS((N_DEV,) + v.shape, v.dtype),
        scratch_shapes=[pltpu.SemaphoreType.DMA, pltpu.SemaphoreType.DMA((N_DEV - 1,)), pltpu.SemaphoreType.DMA((N_DEV - 1,))],
        name="gather_small")(v)


def _rows_tile(r, big=False):
    return _tile(r, ((512,) if big else ()) + (256, 352, 128, 64))


def _pair_sum(name, g, r1, core, me):
    n_l, s_n, h, c = r1.shape
    tr = _rows_tile(h, big=True)
    nb = h // tr

    def body(idx_ref, g_ref, r_ref, pb_ref, own_ref):
        p = g_ref[...] + r_ref[...]
        pb_ref[...] = p.astype(BF16)

        @pl.when(pl.program_id(2) == idx_ref[1])
        def _():
            own_ref[...] = p

    blk = (None, None, tr, c)
    grid_spec = pltpu.PrefetchScalarGridSpec(
        num_scalar_prefetch=1, grid=(n_l, nb, s_n),
        in_specs=[pl.BlockSpec(blk, lambda a, i, s, idx: (a, s, idx[0] * nb + i, 0)), pl.BlockSpec(blk, lambda a, i, s, idx: (a, s, i, 0))],
        out_specs=[pl.BlockSpec(blk, lambda a, i, s, idx: (a, s, i, 0)), pl.BlockSpec((None, tr, c), lambda a, i, s, idx: (a, i, 0))])
    idx = jnp.stack([core, me]).astype(jnp.int32)
    return pl.pallas_call(body, grid_spec=grid_spec, out_shape=[SDS(r1.shape, BF16), SDS((n_l, h, c), F32)], name=name,
                          compiler_params=_cp(VMEM_MM))(idx, g, r1)


def _chip_sum(name, own, r2, core):
    n_l, h, c = own.shape
    tr = _rows_tile(h, big=True)
    nb = h // tr

    def body(idx_ref, o_ref, r_ref, f_ref):
        f_ref[...] = ((o_ref[...] + r_ref[0].astype(F32)) + r_ref[1].astype(F32)) + r_ref[2].astype(F32)

    grid_spec = pltpu.PrefetchScalarGridSpec(
        num_scalar_prefetch=1, grid=(n_l, nb),
        in_specs=[pl.BlockSpec((None, tr, c), lambda a, i, idx: (a, i, 0)), pl.BlockSpec((3, None, tr, c), lambda a, i, idx: (0, a, i, 0))],
        out_specs=pl.BlockSpec((None, tr, c), lambda a, i, idx: (a, idx[0] * nb + i, 0)))
    return pl.pallas_call(body, grid_spec=grid_spec, out_shape=SDS((n_l, 2 * h, c), F32), name=name,
                          compiler_params=_cp(VMEM_MM))(jnp.reshape(core, (1,)).astype(jnp.int32), own, r2)


def _adamw_math(w, g, m, v):
    m = ADAM_B1 * m + (1.0 - ADAM_B1) * g
    v = ADAM_B2 * v + (1.0 - ADAM_B2) * jnp.square(g)
    m_hat = m / (1.0 - ADAM_B1 ** ADAM_STEP)
    v_hat = v / (1.0 - ADAM_B2 ** ADAM_STEP)
    delta = -ADAM_LR * (m_hat / (jnp.sqrt(v_hat) + ADAM_EPS) + ADAM_WD * w)
    return delta, m, v


def _adamw(name, l, w, g, m, v, bufs):
    _, r, c = w.shape
    tr = _rows_tile(r, big=c <= 1024)

    def body(w_ref, g_ref, m_ref, v_ref, *rest):
        go_ref, d_ref, nm_ref, nv_ref = rest[4:]
        g = g_ref[...]
        go_ref[...] = g
        d_ref[...], nm_ref[...], nv_ref[...] = _adamw_math(w_ref[...], g, m_ref[...], v_ref[...])

    blk = pl.BlockSpec((None, tr, c), lambda i: (l, i, 0))
    return pl.pallas_call(body, grid=(r // tr,), in_specs=[blk, pl.BlockSpec((tr, c), lambda i: (i, 0)), blk, blk] + [ANY] * 4,
                          out_specs=[blk] * 4, out_shape=[SDS(w.shape, F32)] * 4, input_output_aliases={4 + k: k for k in range(4)},
                          name=name, compiler_params=_cp(VMEM_MM))(w, g, m, v, *bufs)


def _small_update(parts, w, m, v):
    def body(p_ref, w_ref, m_ref, v_ref, g_ref, d_ref, nm_ref, nv_ref):
        g = p_ref[0]
        for d in range(1, N_DEV):
            g = g + p_ref[d]
        g_ref[...] = g
        d_ref[...], nm_ref[...], nv_ref[...] = _adamw_math(w_ref[...], g, m_ref[...], v_ref[...])

    return pl.pallas_call(body, out_shape=[SDS(w.shape, F32)] * 4, name="small_update")(parts, w, m, v)


SMALL = ["norm_mix_g", "norm_mem_g", "ret_decay_fwd", "ret_decay_bwd", "ret_norm_g", "pool_w", "pool_scale", "na_q_norm_g",
         "na_k_norm_g", "na_rpb", "mem_q_norm_g", "mem_k_norm_g", "norm_ffn_g"]
BIG = ["w_in", "w_gate", "w_mem_kv", "w_branch", "w_out", "w_ffn_in", "w_ffn_out"]
GROUPS = (("w_in", "w_gate", "w_mem_kv", "w_branch", "w_out"), ("w_ffn_in", "w_ffn_out"))
REDUCE_GROUPS = (("w_ffn_in", "w_ffn_out"), ("w_gate", "w_branch", "w_out"), ("w_in", "w_mem_kv"))
ORDER = ["norm_mix_g", "norm_mem_g", "w_in", "w_gate", "ret_decay_fwd", "ret_decay_bwd", "ret_norm_g", "pool_w", "pool_scale",
         "na_q_norm_g", "na_k_norm_g", "na_rpb", "mem_q_norm_g", "mem_k_norm_g", "w_mem_kv", "w_branch", "w_out", "norm_ffn_g",
         "w_ffn_in", "w_ffn_out"]


def _pack(arrs):
    rows = []
    for a in arrs:
        f = a.reshape(-1).astype(F32)
        pad = (-f.shape[0]) % (8 * LANES)
        rows.append(jnp.pad(f, (0, pad)).reshape(-1, LANES))
    return jnp.concatenate(rows, axis=0)


def _unpack(p, like):
    out, at = [], 0
    for a in like:
        n = int(np.prod(a.shape))
        rows = -(-n // (8 * LANES)) * 8
        out.append(p[at:at + rows].reshape(-1)[:n].reshape(a.shape))
        at += rows
    return out


def _rope_tables(t_len):
    half = HEAD // 2
    inv = ROPE_THETA ** (-jnp.arange(half, dtype=F32) / half)
    ang = jnp.arange(t_len, dtype=F32)[:, None] * inv[None, :]
    cos, sin = jnp.cos(ang), jnp.sin(ang)
    return jnp.tile(jnp.concatenate([cos, cos], axis=1), (1, NH)), jnp.tile(jnp.concatenate([-sin, sin], axis=1), (1, NH))


def _block_diag(pw):
    out = jnp.zeros((BW, BW), pw.dtype)
    for g in range(NH):
        out = out.at[g * HEAD:(g + 1) * HEAD, g * HEAD:(g + 1) * HEAD].set(pw[g])
    return out


def _fold_heads(v):
    return v.reshape(NH, HEAD).sum(axis=0)


def _fold_lanes(v):
    return v.reshape(NH, HEAD).sum(axis=1)


def kernel(x, mem, norm_mix_g, norm_mem_g, w_in, w_gate, ret_decay_fwd, ret_decay_bwd, ret_norm_g, pool_w, pool_scale, na_q_norm_g, na_k_norm_g, na_rpb, mem_q_norm_g, mem_k_norm_g, w_mem_kv, w_branch, w_out, norm_ffn_g, w_ffn_in, w_ffn_out, loss_target, m_norm_mix_g, m_norm_mem_g, m_w_in, m_w_gate, m_ret_decay_fwd, m_ret_decay_bwd, m_ret_norm_g, m_pool_w, m_pool_scale, m_na_q_norm_g, m_na_k_norm_g, m_na_rpb, m_mem_q_norm_g, m_mem_k_norm_g, m_w_mem_kv, m_w_branch, m_w_out, m_norm_ffn_g, m_w_ffn_in, m_w_ffn_out, v_norm_mix_g, v_norm_mem_g, v_w_in, v_w_gate, v_ret_decay_fwd, v_ret_decay_bwd, v_ret_norm_g, v_pool_w, v_pool_scale, v_na_q_norm_g, v_na_k_norm_g, v_na_rpb, v_mem_q_norm_g, v_mem_k_norm_g, v_w_mem_kv, v_w_branch, v_w_out, v_norm_ffn_g, v_w_ffn_in, v_w_ffn_out):
    args = dict(locals())
    W = {n: args[n] for n in ORDER}
    M = {n: args["m_" + n] for n in ORDER}
    V = {n: args["v_" + n] for n in ORDER}
    n_layers, d_model = norm_mix_g.shape
    assert x.shape[0] == 1
    t_len = x.shape[1]
    xc, yc, cc = lax.axis_index("x"), lax.axis_index("y"), lax.axis_index("c")
    me_chip = 2 * xc + yc

    def as3(a):
        return a.reshape(a.shape[0], -1, a.shape[-1])

    gather_waits, token = {}, jnp.zeros((), F32)
    for l in range(n_layers):
        for gi, names in enumerate(GROUPS):
            srcs = [as3(W[n])[l].astype(BF16) for n in names]
            lands = [lax.empty((N_CHIPS,) + a.shape, BF16) for a in srcs]
            tok, gather_waits[l, gi] = _split_copies(f"gather_l{l}_g{gi}", srcs, lands, 4 * len(names), _gather_plan)
            token = token + tok[0, 0]

    cos, sin = _rope_tables(t_len)
    btabs = _na_bias_table(na_rpb)
    lane = lambda a: a.reshape(1, -1).astype(F32)
    x2d, mem2d, tgt = x[0], mem[0], loss_target[0]
    norm_mix_g = norm_mix_g + token

    def fetch(l, gi, after):
        lands = gather_waits[l, gi](after)[1]
        return _split_copies(f"pass_l{l}_g{gi}", [], lands, 3 * len(GROUPS[gi]), _pass_plan)

    saved = []
    _, passing = fetch(0, 0, x2d)
    for l in range(n_layers):
        tag = f"l{l}"
        gathered = dict(zip(GROUPS[0], passing(x2d)[1]))
        wi = gathered["w_in"].transpose(1, 0, 2).reshape(1, 1, d_model, -1)
        wg = gathered["w_gate"][None]
        wmkv = gathered["w_mem_kv"].reshape(1, 1, d_model, 2 * BW)
        wb = gathered["w_branch"].reshape(1, N_CHIPS, NH, BW, BW)
        wo = gathered["w_out"].reshape(1, 1, d_model, d_model)
        s = {"x": x2d}
        s["decf"], s["decb"] = lane(jnp.repeat(ret_decay_fwd[l], HEAD)), lane(jnp.repeat(ret_decay_bwd[l], HEAD))
        s["gn"], s["pscale"] = lane(ret_norm_g[l]), lane(pool_scale[l])
        s["wbd"] = _block_diag(pool_w[l])
        s["nagq"], s["nagk"] = lane(jnp.tile(na_q_norm_g[l], NH)), lane(jnp.tile(na_k_norm_g[l], NH))
        s["mgq"], s["mgk"] = lane(jnp.tile(mem_q_norm_g[l], NH)), lane(jnp.tile(mem_k_norm_g[l], NH))
        s["btab"] = btabs[l]
        s["h"] = _rms_fwd(f"rms_mix_{tag}", x2d, lane(norm_mix_g[l]))
        s["proj"] = _mm_nn(f"mm_in_{tag}", s["h"], wi, 0)
        s["gp3"] = _mm_nn(f"mm_gate_{tag}", s["h"], wg, 0, out3d=True)
        tok, passing = fetch(l, 1, s["gp3"])
        s["hm"] = _rms_fwd(f"rms_mem_{tag}", mem2d, lane(norm_mem_g[l]) + tok[0, 0])
        s["memkv"] = _mm_nn(f"mm_memkv_{tag}", s["hm"], wmkv, 0)
        ret, s["ret_saved"] = _ret_fwd(tag, s["proj"], cos, sin, s["decf"], s["decb"], s["gn"])
        s["pvp"] = jnp.pad(s["proj"][:, 4 * BW:5 * BW], ((POOL_PAD, POOL_PAD), (0, 0)))
        pool = _pool_fwd(tag, s["pvp"], s["wbd"], s["pscale"], t_len)[POOL_PAD:POOL_PAD + t_len]
        s["kn"], s["vb"] = _na_keys(tag, s["proj"], s["nagk"])
        na = _na_fwd(tag, s["proj"], s["kn"], s["vb"], s["btab"], s["nagq"])
        mo = _mem_fwd(tag, s["proj"], s["memkv"], s["mgq"], s["mgk"])
        s["br"] = jnp.stack([ret, pool, na, mo])
        s["merged"] = _gate_fwd(f"gate_{tag}", s["gp3"], s["br"], wb, 0)
        s["x2"] = _mm_nn(f"mm_out_{tag}", s["merged"], wo, 0, add=x2d)
        gathered = dict(zip(GROUPS[1], passing(s["x2"])[1]))
        pin_ffn = jnp.zeros((), F32)
        if l + 1 < n_layers:
            tok, passing = fetch(l + 1, 0, s["x2"])
            pin_ffn = tok[0, 0]
        wfi = gathered["w_ffn_in"][None]
        wfo = gathered["w_ffn_out"].reshape(1, 1, -1, d_model)
        s["w"] = (wi, wg, wmkv, wb, wo, wfi, wfo)
        s["h2"] = _rms_fwd(f"rms_ffn_{tag}", s["x2"], lane(norm_ffn_g[l]) + pin_ffn)
        s["ag3"] = _mm_nn(f"mm_ffn_in_{tag}", s["h2"], wfi, 0, out3d=True)
        s["u"] = _swiglu_fwd(f"swiglu_{tag}", s["ag3"])
        x2d = _mm_nn(f"mm_ffn_out_{tag}", s["u"], wfo, 0, add=s["x2"])
        saved.append(s)

    loss_part, dx = _loss_kernel("loss", x2d, tgt)

    dbtabs = [None] * n_layers
    gsmall = {n: [None] * n_layers for n in SMALL}
    opt_view = {n: (lambda a, n=n: jnp.swapaxes(as3(a), 1, 2) if n == "w_in" else as3(a)) for n in BIG}
    out_bufs = {n: tuple(lax.empty(opt_view[n](W[n]).shape, F32) for _ in range(4)) for n in BIG}

    def reduce_group(l, gi, grads):
        names = REDUCE_GROUPS[gi]
        gs = [grads[n] for n in names]
        lands = [lax.empty((N_CHIPS, g.shape[1] // 2, g.shape[2]), F32) for g in gs]
        tok0, wait0 = _split_copies(f"pairx_l{l}_g{gi}", gs, lands, len(names), _pair_plan)

        def sums_and_scatter(after):
            mine, theirs = wait0(after)
            sums = [_pair_sum(f"pair_sum_{n}_l{l}", g[None], r[None], cc, me_chip) for n, g, r in zip(names, mine, theirs)]
            pbs = [p[0] for p, _ in sums]
            lands1 = [lax.empty((3,) + p.shape[1:], BF16) for p in pbs]
            tok1, wait1 = _split_copies(f"scatter_l{l}_g{gi}", pbs, lands1, 3 * len(names), _scatter_plan)

            def chip_sums_and_halves(after):
                halves = [_chip_sum(f"chip_sum_{n}_l{l}", own, r[:, None], cc)[0] for n, (_, own), r in zip(names, sums, wait1(after)[1])]
                tok2, wait2 = _split_copies(f"halfx_l{l}_g{gi}", [], halves, len(names), _half_plan)

                def update(after):
                    for n, f in zip(names, wait2(after)[1]):
                        out_bufs[n] = _adamw(f"adamw_{n}_l{l}", l, opt_view[n](W[n]), f, opt_view[n](M[n]), opt_view[n](V[n]), out_bufs[n])
                    return None, None

                return update, tok2

            return chip_sums_and_halves, tok1

        return sums_and_scatter, tok0

    def advance(stages, new):
        nxt, tok = new
        out, toks = [nxt], tok
        for stage in stages:
            nxt, tok = stage(toks)
            if nxt is not None:
                out.append(nxt)
                toks = toks + tok
        return out, toks

    stages, order_after = [], None
    for l in reversed(range(n_layers)):
        tag = f"l{l}"
        s = saved[l]
        wi, wg, wmkv, wb, wo, wfi, wfo = s["w"]
        du = _mm_nt(f"mm_ffn_out_dx_{tag}", dx, wfo, 0, after=order_after)
        g_wfo = _mm_tn(f"mm_ffn_out_dw_{tag}", s["u"], dx, 1)
        dag3 = _swiglu_bwd(f"swiglu_bwd_{tag}", s["ag3"], du)
        dh2 = _mm_nt(f"mm_ffn_in_dx_{tag}", dag3, wfi, 0)
        g_wfi = _mm_tn(f"mm_ffn_in_dw_{tag}", s["h2"], dag3, N_CHIPS)
        dx2, dg = _rms_bwd(f"rms_ffn_bwd_{tag}", s["x2"], lane(norm_ffn_g[l]), [dh2], add=dx)
        gsmall["norm_ffn_g"][l] = dg[0]
        stages, tok = advance(stages, reduce_group(l, 0, {"w_ffn_in": g_wfi, "w_ffn_out": g_wfo.reshape(N_CHIPS, -1, d_model)}))
        dmerged = _mm_nt(f"mm_out_dx_{tag}", dx2, wo, 0, after=tok)
        g_wo = _mm_tn(f"mm_out_dw_{tag}", s["merged"], dx2, 1)
        dgp3, dbr, g_wb = _gate_bwd(f"gate_bwd_{tag}", s["gp3"], s["br"], wb, 0, dmerged)
        g_wg = _mm_tn(f"mm_gate_dw_{tag}", s["h"], dgp3, N_CHIPS)
        stages, tok = advance(stages, reduce_group(l, 1, {
            "w_gate": g_wg, "w_branch": g_wb.reshape(N_CHIPS, NH * BW, BW), "w_out": g_wo.reshape(N_CHIPS, -1, d_model)}))
        dh_a = _mm_nt(f"mm_gate_dx_{tag}", dgp3, wg, 0, after=tok)
        drq, drk, drv, drg, ddf, ddb, dgn = _ret_bwd(tag, s["proj"], cos, sin, s["decf"], s["decb"], s["gn"], s["ret_saved"], dbr[0])
        gsmall["ret_decay_fwd"][l], gsmall["ret_decay_bwd"][l], gsmall["ret_norm_g"][l] = _fold_lanes(ddf), _fold_lanes(ddb), dgn[0]
        dpool_p = jnp.pad(dbr[1], ((POOL_PAD, POOL_PAD), (0, 0)))
        dpvp, dwbd, dps = _pool_bwd(tag, s["pvp"], s["wbd"], s["pscale"], t_len, dpool_p)
        dpv = dpvp[POOL_PAD:POOL_PAD + t_len]
        gsmall["pool_w"][l] = jnp.stack([dwbd[g * HEAD:(g + 1) * HEAD, g * HEAD:(g + 1) * HEAD] for g in range(NH)])
        gsmall["pool_scale"][l] = dps[0]
        dnq, dkn, dnv, dbtabs[l], dgq = _na_bwd(tag, s["proj"], s["kn"], s["vb"], s["btab"], s["nagq"], dbr[2])
        dnk, dnv, dgk = _na_keys_bwd(tag, s["proj"], s["nagk"], dkn, dnv)
        gsmall["na_q_norm_g"][l], gsmall["na_k_norm_g"][l] = _fold_heads(dgq), _fold_heads(dgk)
        dmq, dmemkv, dgq, dgk = _mem_bwd(tag, s["proj"], s["memkv"], s["mgq"], s["mgk"], dbr[3])
        gsmall["mem_q_norm_g"][l], gsmall["mem_k_norm_g"][l] = _fold_heads(dgq), _fold_heads(dgk)
        g_wmkv = _mm_tn(f"mm_memkv_dw_{tag}", s["hm"], dmemkv, 1)
        dhm = _mm_nt(f"mm_memkv_dx_{tag}", dmemkv, wmkv, 0)
        _, dg = _rms_bwd(f"rms_mem_bwd_{tag}", mem2d, lane(norm_mem_g[l]), [dhm])
        gsmall["norm_mem_g"][l] = dg[0]
        dproj = jnp.concatenate([drq, drk, drv, drg, dpv, dnq, dnk, dnv, dmq], axis=1)
        g_wi = _mm_tn(f"mm_in_dw_{tag}", dproj, s["h"], 1)
        dh_b = _mm_nt(f"mm_in_dx_{tag}", dproj, wi, 0)
        dx, dg = _rms_bwd(f"rms_mix_bwd_{tag}", s["x"], lane(norm_mix_g[l]), [dh_a, dh_b], add=dx2)
        gsmall["norm_mix_g"][l] = dg[0]

        stages, order_after = advance(stages, reduce_group(l, 2, {
            "w_in": g_wi.reshape(N_CHIPS, -1, d_model), "w_mem_kv": g_wmkv.reshape(N_CHIPS, -1, 2 * BW)}))
    out_g, out_d, out_m, out_v = {}, {}, {}, {}
    gsmall["na_rpb"] = list(_na_bias_grad(jnp.stack(dbtabs)))
    small_g = [jnp.stack(gsmall[n]).reshape(W[n].shape) for n in SMALL] + [loss_part]
    like = [W[n] for n in SMALL] + [loss_part]
    zero = jnp.zeros((1, 1), F32)
    parts = _gather_small(_pack(small_g))
    packed = _small_update(parts, _pack([W[n] for n in SMALL] + [zero]), _pack([M[n] for n in SMALL] + [zero]),
                           _pack([V[n] for n in SMALL] + [zero]))
    sg, sd, sm, sv = [_unpack(p, like) for p in packed]
    for i, n in enumerate(SMALL):
        out_g[n], out_d[n], out_m[n], out_v[n] = sg[i], sd[i], sm[i], sv[i]
    loss = sg[-1].reshape(())

    while stages:
        stages = [nxt for nxt, _ in (stage(packed[0]) for stage in stages) if nxt is not None]

    for n in BIG:
        shp = W[n].shape
        back = (lambda b: jnp.swapaxes(b, 1, 2)) if n == "w_in" else (lambda b: b)
        out_g[n], out_d[n], out_m[n], out_v[n] = [back(b).reshape(shp) for b in out_bufs[n]]

    return (loss, dx.reshape(x.shape), *[out_g[n] for n in ORDER], *[out_d[n] for n in ORDER],
            *[out_m[n] for n in ORDER], *[out_v[n] for n in ORDER])
```

```python
import functools

import numpy as np
import jax
import jax.numpy as jnp
from jax import lax
from jax.experimental import pallas as pl
from jax.experimental.pallas import tpu as pltpu

F32 = jnp.float32
BF16 = jnp.bfloat16
SDS = jax.ShapeDtypeStruct
MESH = pl.DeviceIdType.MESH
ANY = pl.BlockSpec(memory_space=pl.ANY)

HEAD = 64
NH = 4
BW = NH * HEAD
CH = 128
GRID_W = 64
NA_ROWS = 8
NA_COLS = 16
POOL_HALF = (1, 2, 4, 8)
POOL_PAD = 16
ROPE_THETA = 10000.0
EPS = 1e-6
NEG_INF = -1e30
N_CHIPS = 4
N_DEV = 8
LANES = 128

ADAM_LR = 0.001
ADAM_B1 = 0.9
ADAM_B2 = 0.999
ADAM_EPS = 1e-08
ADAM_WD = 0.01
ADAM_STEP = 10

VMEM_BIG = 56 << 20
VMEM_MM = 44 << 20


def _cp(vmem=None):
    return pltpu.CompilerParams(vmem_limit_bytes=vmem) if vmem else None


def _d(a, b, ca, cb):
    return lax.dot_general(a.astype(BF16), b.astype(BF16), (((ca,), (cb,)), ((), ())),
                           preferred_element_type=F32)


@jax.custom_vjp
def _nn(a, b):
    return _d(a, b, 1, 0)


def _nn_f(a, b):
    return _d(a, b, 1, 0), (a, b)


def _nn_b(r, g):
    a, b = r
    return _d(g, b, 1, 1).astype(a.dtype), _d(a, g, 0, 0).astype(b.dtype)


_nn.defvjp(_nn_f, _nn_b)


@jax.custom_vjp
def _nt(a, b):
    return _d(a, b, 1, 1)


def _nt_f(a, b):
    return _d(a, b, 1, 1), (a, b)


def _nt_b(r, g):
    a, b = r
    return _d(g, b, 1, 0).astype(a.dtype), _d(g, a, 0, 0).astype(b.dtype)


_nt.defvjp(_nt_f, _nt_b)


@jax.custom_vjp
def _tn(a, b):
    return _d(a, b, 0, 0)


def _tn_f(a, b):
    return _d(a, b, 0, 0), (a, b)


def _tn_b(r, g):
    a, b = r
    return _d(b, g, 1, 1).astype(a.dtype), _d(a, g, 1, 0).astype(b.dtype)


_tn.defvjp(_tn_f, _tn_b)


@functools.partial(jax.custom_vjp, nondiff_argnums=(1,))
def _rollr(x, s):
    return pltpu.roll(x, s % x.shape[0], 0)


def _rollr_f(x, s):
    return _rollr(x, s), None


def _rollr_b(s, _, g):
    return (_rollr(g, -s),)


_rollr.defvjp(_rollr_f, _rollr_b)


@jax.custom_vjp
def _swap32(t):
    n = t.shape[1]
    lane = lax.broadcasted_iota(jnp.int32, (1, n), 1)
    return jnp.where((lane & (HEAD // 2)) == 0, pltpu.roll(t, n - HEAD // 2, 1), pltpu.roll(t, HEAD // 2, 1))


def _swap32_f(t):
    return _swap32(t), None


def _swap32_b(_, g):
    return (_swap32(g),)


_swap32.defvjp(_swap32_f, _swap32_b)


def _head_masks():
    lane = lax.broadcasted_iota(jnp.int32, (1, BW), 1)
    return [(lane >= HEAD * h) & (lane < HEAD * (h + 1)) for h in range(NH)]


def _head_rms(x, g):
    out = jnp.zeros_like(x)
    for mh in _head_masks():
        ms = jnp.sum(jnp.where(mh, x * x, 0.0), axis=-1, keepdims=True) * (1.0 / HEAD)
        out = out + jnp.where(mh, x * lax.rsqrt(ms + EPS), 0.0)
    return out * g


def _stack_heads(x):
    return jnp.concatenate([jnp.where(mh, x, 0.0) for mh in _head_masks()], axis=0)


def _unstack_heads(y):
    n = y.shape[0] // NH
    out = None
    for h, mh in enumerate(_head_masks()):
        t = jnp.where(mh, y[h * n:(h + 1) * n], 0.0)
        out = t if out is None else out + t
    return out


def _rms(x, g):
    return x * lax.rsqrt(jnp.mean(x * x, axis=-1, keepdims=True) + EPS) * g


def _rot(t, cos, sin):
    return t * cos + _swap32(t) * sin


def _softmax(s):
    e = jnp.exp(s - lax.stop_gradient(jnp.max(s, axis=-1, keepdims=True)))
    return e / jnp.sum(e, axis=-1, keepdims=True)


def _ret_kv_f(k, v, cos, sin, decf, decb):
    lgf, lgb = jax.nn.log_sigmoid(decf), jax.nn.log_sigmoid(decb)
    kr = _rot(k, cos, sin)
    idx = lax.broadcasted_iota(jnp.int32, (CH, 1), 0).astype(F32)
    r = lax.broadcasted_iota(jnp.int32, (BW, BW), 0) // HEAD
    c = lax.broadcasted_iota(jnp.int32, (BW, BW), 1) // HEAD
    bd = r == c
    kvf = jnp.where(bd, _tn(kr * jnp.exp((CH - 1 - idx) * lgf), v), 0.0)
    kvb = jnp.where(bd, _tn(kr * jnp.exp(idx * lgb), v), 0.0)
    return kvf, kvb


def _ret_scan_f(kvf, kvb, decf, decb):
    n = len(kvf)
    cdf = jnp.exp(CH * jax.nn.log_sigmoid(decf))
    cdb = jnp.exp(CH * jax.nn.log_sigmoid(decb))
    z = jnp.zeros((BW, BW), F32)
    sf, st = [z], z
    for a in range(n - 1):
        st = cdf * st + kvf[a]
        sf.append(st)
    sb, st = [z], z
    for a in range(n - 1, 0, -1):
        st = cdb * st + kvb[a]
        sb.append(st)
    return sf, sb[::-1]


def _ret_out_f(q, k, v, g, cos, sin, stf, stb, decf, decb, gn):
    lgf, lgb = jax.nn.log_sigmoid(decf), jax.nn.log_sigmoid(decb)
    qr = _rot(q, cos, sin) * (HEAD ** -0.5)
    kr = _rot(k, cos, sin)
    diff = (lax.broadcasted_iota(jnp.int32, (CH, CH), 0) - lax.broadcasted_iota(jnp.int32, (CH, CH), 1)).astype(F32)
    decay = []
    for mh in _head_masks():
        lf = jnp.sum(jnp.where(mh, lgf, 0.0), axis=1, keepdims=True) * (1.0 / HEAD)
        lb = jnp.sum(jnp.where(mh, lgb, 0.0), axis=1, keepdims=True) * (1.0 / HEAD)
        decay.append(jnp.exp(jnp.where(diff >= 0, diff * lf, -diff * lb)))
    p = _nt(_stack_heads(qr), kr) * jnp.concatenate(decay, axis=0)
    o = _unstack_heads(_nn(p, v))
    idx = lax.broadcasted_iota(jnp.int32, (CH, 1), 0).astype(F32)
    o = o + _nn(qr * jnp.exp((idx + 1) * lgf), stf) + _nn(qr * jnp.exp((CH - idx) * lgb), stb)
    return _head_rms(o, gn) * (g * jax.nn.sigmoid(g))


def _pool_f(pvp, wbd, scale, t_len):
    n = pvp.shape[0]
    w2 = pvp + _rollr(pvp, 1)
    w4 = _rollr(w2, 1) + _rollr(w2, -1)
    w8 = _rollr(w4, 2) + _rollr(w4, -2)
    w16 = _rollr(w8, 4) + _rollr(w8, -4)
    grp = lax.broadcasted_iota(jnp.int32, (1, BW), 1) // HEAD
    ws = jnp.where(grp == 0, w2, jnp.where(grp == 1, w4, jnp.where(grp == 2, w8, w16)))
    half = jnp.where(grp == 0, POOL_HALF[0], jnp.where(grp == 1, POOL_HALF[1], jnp.where(grp == 2, POOL_HALF[2], POOL_HALF[3])))
    t = lax.broadcasted_iota(jnp.int32, (n, 1), 0) - POOL_PAD
    cnt = jnp.minimum(t + half, t_len) - jnp.maximum(t - half, 0)
    cnt = jnp.where((t >= 0) & (t < t_len), cnt, 1).astype(F32)
    pooled = ws / cnt - pvp
    return _nn(pooled, wbd) * scale


def _na_f(q, kn, vw, bias, gq):
    qn = _head_rms(q, gq)
    bias_all = jnp.concatenate([jnp.concatenate(bias[h], axis=1) for h in range(NH)], axis=0)
    s = _nt(_stack_heads(qn), kn) * (HEAD ** -0.5) + bias_all
    return _unstack_heads(_nn(_softmax(s), vw))


def _mem_f(q, mk, mv, gq, gk):
    qn = _head_rms(q, gq)
    kn = _head_rms(mk, gk)
    s = _nt(_stack_heads(qn), kn) * (HEAD ** -0.5)
    return _unstack_heads(_nn(_softmax(s), mv))


def _gate_f(gp, br, wb):
    out = None
    for n in range(NH):
        t = jax.nn.sigmoid(gp[n]) * _nn(br[n], wb[n])
        out = t if out is None else out + t
    return out


def _swiglu_f(a, g):
    return a * jax.nn.sigmoid(a) * g


MM_ROWS = (1024, 512, 256)
MM_COLS = (1152, 1024, 768, 512)


def _tile(n, prefs):
    for p in prefs:
        if n % p == 0:
            return p
    return n


def _mm_call(name, a, b, extra, dn, grid, a_spec, b_spec, extra_specs, o_spec, out_shape, acc_shape, nred, after=None):
    add = len(extra) == 1
    if after is not None:
        extra, extra_specs = [*extra, after], [*extra_specs, ANY]
    n_extra = len(extra)
    n_steps = int(np.prod([grid[len(grid) - 1 - i] for i in range(nred)]))

    def body(*refs):
        a_ref, b_ref = refs[0], refs[1]
        o_ref, acc = refs[2 + n_extra], refs[-1]
        red = [len(grid) - 1 - i for i in range(nred)]
        first = functools.reduce(jnp.logical_and, [pl.program_id(ax) == 0 for ax in red])
        last = functools.reduce(jnp.logical_and, [pl.program_id(ax) == grid[ax] - 1 for ax in red])
        d = lax.dot_general(a_ref[...].astype(BF16), b_ref[...].astype(BF16), dn, preferred_element_type=F32)

        def finish(r):
            if add:
                r = r + refs[2][...]
            o_ref[...] = r.astype(o_ref.dtype)

        if n_steps == 1:
            finish(d)
        else:
            @pl.when(first)
            def _():
                acc[...] = d

            @pl.when(jnp.logical_not(first) & jnp.logical_not(last))
            def _():
                acc[...] += d

            @pl.when(last)
            def _():
                finish(acc[...] + d)

    return pl.pallas_call(
        body, grid=grid, in_specs=[a_spec, b_spec, *extra_specs], out_specs=o_spec, out_shape=out_shape,
        scratch_shapes=[pltpu.VMEM(acc_shape, F32)] if n_steps > 1 else [], name=name, compiler_params=_cp(VMEM_MM),
    )(a, b, *extra)


def _mm_nn(name, a, w, l, *, out3d=False, add=None, out_dtype=F32):
    m, k = a.shape
    _, s_n, _, ns = w.shape
    tm = _tile(m, MM_ROWS)
    tk = _tile(k, (1024, 1408, 512, 256))
    tn = ns if ns <= 1408 else _tile(ns, MM_COLS)
    grid = (m // tm, s_n, ns // tn, k // tk)
    a_spec = pl.BlockSpec((tm, tk), lambda i, s, j, kk: (i, kk))
    b_spec = pl.BlockSpec((None, None, tk, tn), lambda i, s, j, kk: (l, s, kk, j))
    if out3d:
        o_spec = pl.BlockSpec((None, tm, tn), lambda i, s, j, kk: (s, i, j))
        out_shape = SDS((s_n, m, ns), out_dtype)
    else:
        nj = ns // tn
        o_spec = pl.BlockSpec((tm, tn), lambda i, s, j, kk: (i, s * nj + j))
        out_shape = SDS((m, s_n * ns), out_dtype)
    extra, especs = [], []
    if add is not None:
        assert not out3d
        nj = ns // tn
        extra, especs = [add], [pl.BlockSpec((tm, tn), lambda i, s, j, kk: (i, s * nj + j))]
    return _mm_call(name, a, w, extra, (((1,), (0,)), ((), ())), grid, a_spec, b_spec, especs, o_spec, out_shape, (tm, tn), 1)


def _mm_nt(name, g, w, l, after=None):
    _, s_n, k, ns = w.shape
    g3d = g.ndim == 3
    m = g.shape[1] if g3d else g.shape[0]
    tm = _tile(m, MM_ROWS)
    to = _tile(k, (1024, 1408, 512, 256))
    tk = ns if (ns % LANES or ns <= 1408) else _tile(ns, MM_COLS)
    nks = ns // tk
    grid = (m // tm, k // to, s_n, nks)
    if g3d:
        a_spec = pl.BlockSpec((None, tm, tk), lambda i, j, s, r: (s, i, r))
    else:
        a_spec = pl.BlockSpec((tm, tk), lambda i, j, s, r: (i, s * nks + r))
    b_spec = pl.BlockSpec((None, None, to, tk), lambda i, j, s, r: (l, s, j, r))
    o_spec = pl.BlockSpec((tm, to), lambda i, j, s, r: (i, j))
    return _mm_call(name, g, w, [], (((1,), (1,)), ((), ())), grid, a_spec, b_spec, [], o_spec, SDS((m, k), F32), (tm, to), 2,
                    after=after)


def _mm_tn(name, a, g, s_n):
    t_len, k = a.shape
    g3d = g.ndim == 3
    ns = g.shape[2] if g3d else g.shape[1] // s_n
    tt = _tile(t_len, (2048, 1024, 512))
    to = _tile(k, (1024, 768, 1408, 512, 256, 128))
    tn = ns
    nj = ns // tn
    grid = (s_n, k // to, nj, t_len // tt)
    a_spec = pl.BlockSpec((tt, to), lambda s, i, j, t: (t, i))
    if g3d:
        b_spec = pl.BlockSpec((None, tt, tn), lambda s, i, j, t: (s, t, j))
    else:
        b_spec = pl.BlockSpec((tt, tn), lambda s, i, j, t: (t, s * nj + j))
    o_spec = pl.BlockSpec((None, to, tn), lambda s, i, j, t: (s, i, j))
    return _mm_call(name, a, g, [], (((0,), (0,)), ((), ())), grid, a_spec, b_spec, [], o_spec,
                    SDS((s_n, k, ns), F32), (to, tn), 1)


def _rms_fwd(name, x, g):
    m, d = x.shape
    tm = _tile(m, (512, 256))

    def body(x_ref, g_ref, o_ref):
        o_ref[...] = _rms(x_ref[...], g_ref[...]).astype(BF16)

    row = pl.BlockSpec((tm, d), lambda i: (i, 0))
    return pl.pallas_call(body, grid=(m // tm,), in_specs=[row, pl.BlockSpec((1, d), lambda i: (0, 0))], out_specs=row,
                          out_shape=SDS((m, d), BF16), name=name, compiler_params=_cp(VMEM_MM))(x, g)


def _rms_bwd(name, x, g, dhs, add=None):
    m, d = x.shape
    tm = _tile(m, (512, 256))
    nh = len(dhs)

    def body(*refs):
        x_ref, g_ref = refs[0], refs[1]
        dh = refs[2][...]
        for r in refs[3:2 + nh]:
            dh = dh + r[...]
        dx_ref, dg_ref = refs[-2], refs[-1]
        _, vjp = jax.vjp(_rms, x_ref[...], g_ref[...])
        dx, dg = vjp(dh)
        if add is not None:
            dx = dx + refs[2 + nh][...]
        dx_ref[...] = dx

        @pl.when(pl.program_id(0) == 0)
        def _():
            dg_ref[...] = jnp.zeros_like(dg_ref)

        dg_ref[...] += dg

    row = pl.BlockSpec((tm, d), lambda i: (i, 0))
    vec = pl.BlockSpec((1, d), lambda i: (0, 0))
    ins = [x, g, *dhs] + ([add] if add is not None else [])
    return pl.pallas_call(body, grid=(m // tm,), in_specs=[row, vec] + [row] * (len(ins) - 2), out_specs=[row, vec],
                          out_shape=[SDS((m, d), F32), SDS((1, d), F32)], name=name, compiler_params=_cp(VMEM_MM))(*ins)


def _swiglu_fwd(name, ag3):
    _, t_len, w = ag3.shape
    tm = _tile(t_len, (512, 256))

    def body(a_ref, g_ref, o_ref):
        o_ref[...] = _swiglu_f(a_ref[...], g_ref[...]).astype(BF16)

    return pl.pallas_call(
        body, grid=(t_len // tm, 2),
        in_specs=[pl.BlockSpec((None, tm, w), lambda i, j: (j, i, 0)), pl.BlockSpec((None, tm, w), lambda i, j: (j + 2, i, 0))],
        out_specs=pl.BlockSpec((tm, w), lambda i, j: (i, j)), out_shape=SDS((t_len, 2 * w), BF16), name=name,
        compiler_params=_cp(VMEM_MM))(ag3, ag3)


def _swiglu_bwd(name, ag3, du):
    _, t_len, w = ag3.shape
    tm = _tile(t_len, (512, 256))

    def body(a_ref, g_ref, du_ref, o_ref):
        _, vjp = jax.vjp(_swiglu_f, a_ref[...], g_ref[...])
        da, dg = vjp(du_ref[...])
        o_ref[0] = da.astype(BF16)
        o_ref[1] = dg.astype(BF16)

    out = pl.pallas_call(
        body, grid=(t_len // tm, 2),
        in_specs=[pl.BlockSpec((None, tm, w), lambda i, j: (j, i, 0)), pl.BlockSpec((None, tm, w), lambda i, j: (j + 2, i, 0)),
                  pl.BlockSpec((tm, w), lambda i, j: (i, j))],
        out_specs=pl.BlockSpec((2, None, tm, w), lambda i, j: (0, j, i, 0)), out_shape=SDS((2, 2, t_len, w), BF16), name=name,
        compiler_params=_cp(VMEM_MM))(ag3, ag3, du)
    return out.reshape(ag3.shape)


def _gate_fwd(name, gp3, br, wb, l):
    _, t_len, d = gp3.shape
    tm = _tile(t_len, MM_ROWS)

    def body(gp_ref, br_ref, wb_ref, o_ref):
        o_ref[...] = _gate_f([gp_ref[n] for n in range(NH)], [br_ref[n] for n in range(NH)],
                             [wb_ref[n] for n in range(NH)]).astype(BF16)

    return pl.pallas_call(
        body, grid=(t_len // tm, d // BW),
        in_specs=[pl.BlockSpec((NH, tm, BW), lambda i, s: (0, i, s)), pl.BlockSpec((NH, tm, BW), lambda i, s: (0, i, 0)),
                  pl.BlockSpec((None, None, NH, BW, BW), lambda i, s: (l, s, 0, 0, 0))],
        out_specs=pl.BlockSpec((tm, BW), lambda i, s: (i, s)), out_shape=SDS((t_len, d), BF16), name=name,
        compiler_params=_cp(VMEM_MM))(gp3, br, wb)


def _gate_bwd(name, gp3, br, wb, l, dmerged):
    _, t_len, d = gp3.shape
    tm = _tile(t_len, (512, 256))
    ns = d // BW

    def body(gp_ref, br_ref, wb_ref, dm_ref, dgp_ref, dbr_ref, dwb_ref):
        i, s = pl.program_id(0), pl.program_id(1)
        gp = [gp_ref[n] for n in range(NH)]
        brv = [br_ref[n].astype(F32) for n in range(NH)]
        wbv = [wb_ref[n].astype(F32) for n in range(NH)]
        _, vjp = jax.vjp(_gate_f, gp, brv, wbv)
        dgp, dbr, dwb = vjp(dm_ref[...])

        @pl.when(s == 0)
        def _():
            dbr_ref[...] = jnp.zeros_like(dbr_ref)

        @pl.when((i == 0) & (s == 0))
        def _():
            dwb_ref[...] = jnp.zeros_like(dwb_ref)

        for n in range(NH):
            dgp_ref[n] = dgp[n].astype(BF16)
            dbr_ref[n] += dbr[n]
            dwb_ref[s, n] += dwb[n]

    return pl.pallas_call(
        body, grid=(t_len // tm, ns),
        in_specs=[pl.BlockSpec((NH, tm, BW), lambda i, s: (0, i, s)), pl.BlockSpec((NH, tm, BW), lambda i, s: (0, i, 0)),
                  pl.BlockSpec((None, None, NH, BW, BW), lambda i, s: (l, s, 0, 0, 0)), pl.BlockSpec((tm, BW), lambda i, s: (i, s))],
        out_specs=[pl.BlockSpec((NH, tm, BW), lambda i, s: (0, i, s)), pl.BlockSpec((NH, tm, BW), lambda i, s: (0, i, 0)),
                   pl.BlockSpec((ns, NH, BW, BW), lambda i, s: (0, 0, 0, 0))],
        out_shape=[SDS(gp3.shape, BF16), SDS((NH, t_len, BW), F32), SDS((ns, NH, BW, BW), F32)], name=name,
        compiler_params=_cp(VMEM_MM))(gp3, br, wb, dmerged)


def _loss_kernel(name, y, tgt):
    m, d = y.shape
    tm = _tile(m, (512, 256))

    def body(y_ref, t_ref, l_ref, dy_ref):
        err = y_ref[...] - t_ref[...]
        dy_ref[...] = err * (1.0 / d)

        @pl.when(pl.program_id(0) == 0)
        def _():
            l_ref[...] = jnp.zeros_like(l_ref)

        l_ref[...] += 0.5 * jnp.sum(jnp.mean(err * err, axis=-1, keepdims=True), axis=0, keepdims=True)

    row = pl.BlockSpec((tm, d), lambda i: (i, 0))
    return pl.pallas_call(body, grid=(m // tm,), in_specs=[row, row], out_specs=[pl.BlockSpec((1, 1), lambda i: (0, 0)), row],
                          out_shape=[SDS((1, 1), F32), SDS((m, d), F32)], name=name)(y, tgt)


def _zero_first(refs):
    @pl.when(pl.program_id(0) == 0)
    def _():
        for r in refs:
            r[...] = jnp.zeros_like(r)


def _grp(rows, j):
    return pl.BlockSpec((rows, BW), lambda a, j=j: (a, j))


_VEC = pl.BlockSpec((1, BW), lambda a: (0, 0))


def _ret_fwd(tag, proj, cos, sin, decf, decb, gn):
    t_len = proj.shape[0]
    n = t_len // CH
    tab = pl.BlockSpec((CH, BW), lambda a: (a, 0))
    st = pl.BlockSpec((None, BW, BW), lambda a: (a, 0, 0))

    def k1(k_ref, v_ref, cos_ref, sin_ref, df_ref, db_ref, kvf_ref, kvb_ref):
        kvf_ref[...], kvb_ref[...] = _ret_kv_f(k_ref[...], v_ref[...], cos_ref[...], sin_ref[...], df_ref[...], db_ref[...])

    kvf, kvb = pl.pallas_call(
        k1, grid=(n,), in_specs=[_grp(CH, 1), _grp(CH, 2), tab, tab, _VEC, _VEC], out_specs=[st, st],
        out_shape=[SDS((n, BW, BW), F32)] * 2, name=f"ret_kv_{tag}")(proj, proj, cos, sin, decf, decb)

    def k2(kvf_ref, kvb_ref, df_ref, db_ref, sf_ref, sb_ref):
        sf, sb = _ret_scan_f([kvf_ref[a] for a in range(n)], [kvb_ref[a] for a in range(n)], df_ref[...], db_ref[...])
        for a in range(n):
            sf_ref[a] = sf[a]
            sb_ref[a] = sb[a]

    stf, stb = pl.pallas_call(k2, out_shape=[SDS((n, BW, BW), F32)] * 2, name=f"ret_scan_{tag}",
                              compiler_params=_cp(VMEM_BIG))(kvf, kvb, decf, decb)

    def k3(q_ref, k_ref, v_ref, g_ref, cos_ref, sin_ref, sf_ref, sb_ref, df_ref, db_ref, gn_ref, o_ref):
        o_ref[...] = _ret_out_f(q_ref[...], k_ref[...], v_ref[...], g_ref[...], cos_ref[...], sin_ref[...], sf_ref[...],
                                sb_ref[...], df_ref[...], db_ref[...], gn_ref[...]).astype(BF16)

    ret = pl.pallas_call(
        k3, grid=(n,), in_specs=[_grp(CH, 0), _grp(CH, 1), _grp(CH, 2), _grp(CH, 3), tab, tab, st, st, _VEC, _VEC, _VEC],
        out_specs=tab, out_shape=SDS((t_len, BW), BF16), name=f"ret_out_{tag}")(proj, proj, proj, proj, cos, sin, stf, stb, decf, decb, gn)
    return ret, (kvf, kvb, stf, stb)


def _ret_bwd(tag, proj, cos, sin, decf, decb, gn, saved, dret):
    kvf, kvb, stf, stb = saved
    t_len = proj.shape[0]
    n = t_len // CH
    tab = pl.BlockSpec((CH, BW), lambda a: (a, 0))
    st = pl.BlockSpec((None, BW, BW), lambda a: (a, 0, 0))

    def k3(q_ref, k_ref, v_ref, g_ref, cos_ref, sin_ref, sf_ref, sb_ref, df_ref, db_ref, gn_ref, do_ref,
           dq_ref, dk_ref, dv_ref, dg_ref, dsf_ref, dsb_ref, ddf_ref, ddb_ref, dgn_ref):
        cos, sin = cos_ref[...], sin_ref[...]
        f = lambda q, k, v, g, sf, sb, df, db, gnv: _ret_out_f(q, k, v, g, cos, sin, sf, sb, df, db, gnv)
        _, vjp = jax.vjp(f, q_ref[...], k_ref[...], v_ref[...], g_ref[...], sf_ref[...], sb_ref[...], df_ref[...], db_ref[...], gn_ref[...])
        dq, dk, dv, dg, dsf, dsb, ddf, ddb, dgn = vjp(do_ref[...])
        dq_ref[...] = dq.astype(BF16)
        dk_ref[...] = dk
        dv_ref[...] = dv
        dg_ref[...] = dg.astype(BF16)
        dsf_ref[...] = dsf
        dsb_ref[...] = dsb
        _zero_first([ddf_ref, ddb_ref, dgn_ref])
        ddf_ref[...] += ddf
        ddb_ref[...] += ddb
        dgn_ref[...] += dgn

    dq, dk3, dv3, dg, dstf, dstb, ddf3, ddb3, dgn = pl.pallas_call(
        k3, grid=(n,),
        in_specs=[_grp(CH, 0), _grp(CH, 1), _grp(CH, 2), _grp(CH, 3), tab, tab, st, st, _VEC, _VEC, _VEC, tab],
        out_specs=[tab, tab, tab, tab, st, st, _VEC, _VEC, _VEC],
        out_shape=[SDS((t_len, BW), BF16), SDS((t_len, BW), F32), SDS((t_len, BW), F32), SDS((t_len, BW), BF16),
                   SDS((n, BW, BW), F32), SDS((n, BW, BW), F32), SDS((1, BW), F32), SDS((1, BW), F32), SDS((1, BW), F32)],
        name=f"ret_out_bwd_{tag}")(proj, proj, proj, proj, cos, sin, stf, stb, decf, decb, gn, dret)

    def k2(kvf_ref, kvb_ref, df_ref, db_ref, dsf_ref, dsb_ref, dkvf_ref, dkvb_ref, ddf_ref, ddb_ref):
        _, vjp = jax.vjp(_ret_scan_f, [kvf_ref[a] for a in range(n)], [kvb_ref[a] for a in range(n)], df_ref[...], db_ref[...])
        dkvf, dkvb, ddf_ref[...], ddb_ref[...] = vjp(([dsf_ref[a] for a in range(n)], [dsb_ref[a] for a in range(n)]))
        for a in range(n):
            dkvf_ref[a] = dkvf[a]
            dkvb_ref[a] = dkvb[a]

    dkvf, dkvb, ddf2, ddb2 = pl.pallas_call(
        k2, out_shape=[SDS((n, BW, BW), F32), SDS((n, BW, BW), F32), SDS((1, BW), F32), SDS((1, BW), F32)],
        name=f"ret_scan_bwd_{tag}", compiler_params=_cp(VMEM_BIG))(kvf, kvb, decf, decb, dstf, dstb)

    def k1(k_ref, v_ref, cos_ref, sin_ref, df_ref, db_ref, dkvf_ref, dkvb_ref, dk3_ref, dv3_ref, dk_ref, dv_ref, ddf_ref, ddb_ref):
        cos, sin = cos_ref[...], sin_ref[...]
        f = lambda k, v, df, db: _ret_kv_f(k, v, cos, sin, df, db)
        _, vjp = jax.vjp(f, k_ref[...], v_ref[...], df_ref[...], db_ref[...])
        dk, dv, ddf, ddb = vjp((dkvf_ref[...], dkvb_ref[...]))
        dk_ref[...] = (dk + dk3_ref[...]).astype(BF16)
        dv_ref[...] = (dv + dv3_ref[...]).astype(BF16)
        _zero_first([ddf_ref, ddb_ref])
        ddf_ref[...] += ddf
        ddb_ref[...] += ddb

    dk, dv, ddf1, ddb1 = pl.pallas_call(
        k1, grid=(n,), in_specs=[_grp(CH, 1), _grp(CH, 2), tab, tab, _VEC, _VEC, st, st, tab, tab],
        out_specs=[tab, tab, _VEC, _VEC],
        out_shape=[SDS((t_len, BW), BF16), SDS((t_len, BW), BF16), SDS((1, BW), F32), SDS((1, BW), F32)],
        name=f"ret_kv_bwd_{tag}")(proj, proj, cos, sin, decf, decb, dkvf, dkvb, dk3, dv3)
    return dq, dk, dv, dg, ddf1 + ddf2 + ddf3, ddb1 + ddb2 + ddb3, dgn


def _pool_fwd(tag, pvp, wbd, scale, t_len):
    def body(p_ref, w_ref, s_ref, o_ref):
        o_ref[...] = _pool_f(p_ref[...], w_ref[...], s_ref[...], t_len).astype(BF16)

    return pl.pallas_call(body, out_shape=SDS(pvp.shape, BF16), name=f"pool_{tag}", compiler_params=_cp(VMEM_BIG))(pvp, wbd, scale)


def _pool_bwd(tag, pvp, wbd, scale, t_len, dpool_p):
    def body(p_ref, w_ref, s_ref, do_ref, dp_ref, dw_ref, ds_ref):
        f = lambda p, w, s: _pool_f(p, w, s, t_len)
        _, vjp = jax.vjp(f, p_ref[...], w_ref[...], s_ref[...])
        dp, dw, ds = vjp(do_ref[...])
        dp_ref[...] = dp.astype(BF16)
        dw_ref[...] = dw
        ds_ref[...] = ds

    return pl.pallas_call(body, out_shape=[SDS(pvp.shape, BF16), SDS((BW, BW), F32), SDS((1, BW), F32)],
                          name=f"pool_bwd_{tag}", compiler_params=_cp(VMEM_BIG))(pvp, wbd, scale, dpool_p)


def _na_window(r, rows):
    r0 = jnp.clip(r - NA_ROWS // 2, 0, rows - NA_ROWS)
    return r0, r - r0


NA_PAIRS = 2 * NA_ROWS - 2


def _na_bias_pieces(b_ref, pat):
    return [[b_ref[h * NA_PAIRS + 2 * k - pat + NA_ROWS - 1] for k in range(NA_ROWS // 2)] for h in range(NH)]


def _na_keys(tag, proj, gk):
    t_len = proj.shape[0]
    tm = _tile(t_len, (256,))

    def body(k_ref, v_ref, gk_ref, kn_ref, vb_ref):
        kn_ref[...] = _head_rms(k_ref[...], gk_ref[...]).astype(BF16)
        vb_ref[...] = v_ref[...].astype(BF16)

    row = pl.BlockSpec((tm, BW), lambda a: (a, 0))
    return pl.pallas_call(body, grid=(t_len // tm,), in_specs=[_grp(tm, 6), _grp(tm, 7), _VEC], out_specs=[row, row],
                          out_shape=[SDS((t_len, BW), BF16)] * 2, name=f"na_keys_{tag}", compiler_params=_cp(VMEM_MM))(proj, proj, gk)


def _na_keys_bwd(tag, proj, gk, dkn, dv):
    t_len = proj.shape[0]
    tm = _tile(t_len, (256,))

    def body(k_ref, gk_ref, dkn_ref, dv_ref, dk_ref, dvb_ref, dgk_ref):
        _, vjp = jax.vjp(_head_rms, k_ref[...], gk_ref[...])
        dk, dgk = vjp(dkn_ref[...])
        dk_ref[...] = dk.astype(BF16)
        dvb_ref[...] = dv_ref[...].astype(BF16)
        _zero_first([dgk_ref])
        dgk_ref[...] += dgk

    row = pl.BlockSpec((tm, BW), lambda a: (a, 0))
    return pl.pallas_call(body, grid=(t_len // tm,), in_specs=[_grp(tm, 6), _VEC, row, row], out_specs=[row, row, _VEC],
                          out_shape=[SDS((t_len, BW), BF16), SDS((t_len, BW), BF16), SDS((1, BW), F32)],
                          name=f"na_keys_bwd_{tag}", compiler_params=_cp(VMEM_MM))(proj, gk, dkn, dv)


def _na_fwd(tag, proj, kn, vb, bias_tab, gq):
    t_len = proj.shape[0]
    rows = t_len // GRID_W
    win = NA_ROWS * GRID_W

    def body(q_ref, k_ref, v_ref, b_ref, gq_ref, o_ref):
        r0, pat = _na_window(pl.program_id(0), rows)
        start = pl.multiple_of(r0 * GRID_W, GRID_W)
        o_ref[...] = _na_f(q_ref[...], k_ref[pl.ds(start, win), :], v_ref[pl.ds(start, win), :], _na_bias_pieces(b_ref, pat),
                           gq_ref[...]).astype(BF16)

    whole = pl.BlockSpec((t_len, BW), lambda a: (0, 0))
    return pl.pallas_call(
        body, grid=(rows,),
        in_specs=[_grp(GRID_W, 5), whole, whole, pl.BlockSpec(bias_tab.shape, lambda a: (0, 0, 0)), _VEC],
        out_specs=pl.BlockSpec((GRID_W, BW), lambda a: (a, 0)), out_shape=SDS((t_len, BW), BF16),
        name=f"na_{tag}", compiler_params=_cp(VMEM_MM))(proj, kn, vb, bias_tab, gq)


def _na_bwd(tag, proj, kn, vb, bias_tab, gq, dna):
    t_len = proj.shape[0]
    rows = t_len // GRID_W
    win = NA_ROWS * GRID_W

    def body(q_ref, k_ref, v_ref, b_ref, gq_ref, do_ref, dq_ref, dk_ref, dv_ref, db_ref, dgq_ref):
        r0, pat = _na_window(pl.program_id(0), rows)
        start = pl.multiple_of(r0 * GRID_W, GRID_W)
        _zero_first([dk_ref, dv_ref, db_ref, dgq_ref])
        _, vjp = jax.vjp(_na_f, q_ref[...], k_ref[pl.ds(start, win), :].astype(F32), v_ref[pl.ds(start, win), :].astype(F32),
                         _na_bias_pieces(b_ref, pat), gq_ref[...])
        dq, dk, dv, db, dgq = vjp(do_ref[...])
        dq_ref[...] = dq.astype(BF16)
        dk_ref[pl.ds(start, win), :] += dk
        dv_ref[pl.ds(start, win), :] += dv
        for h in range(NH):
            for k in range(NA_ROWS // 2):
                db_ref[h * NA_PAIRS + 2 * k - pat + NA_ROWS - 1] += db[h][k]
        dgq_ref[...] += dgq

    whole = pl.BlockSpec((t_len, BW), lambda a: (0, 0))
    tabspec = pl.BlockSpec(bias_tab.shape, lambda a: (0, 0, 0))
    row = pl.BlockSpec((GRID_W, BW), lambda a: (a, 0))
    return pl.pallas_call(
        body, grid=(rows,), in_specs=[_grp(GRID_W, 5), whole, whole, tabspec, _VEC, row],
        out_specs=[row, whole, whole, tabspec, _VEC],
        out_shape=[SDS((t_len, BW), BF16), SDS((t_len, BW), F32), SDS((t_len, BW), F32), SDS(bias_tab.shape, F32), SDS((1, BW), F32)],
        name=f"na_bwd_{tag}", compiler_params=_cp(VMEM_BIG))(proj, kn, vb, bias_tab, gq, dna)


def _mem_fwd(tag, proj, memkv, gq, gk):
    t_len = proj.shape[0]
    n_mem = memkv.shape[0]
    tm = _tile(t_len, (256,))

    def body(q_ref, mk_ref, mv_ref, gq_ref, gk_ref, o_ref):
        o_ref[...] = _mem_f(q_ref[...], mk_ref[...], mv_ref[...], gq_ref[...], gk_ref[...]).astype(BF16)

    mspec = lambda j: pl.BlockSpec((n_mem, BW), lambda a, j=j: (0, j))
    return pl.pallas_call(
        body, grid=(t_len // tm,), in_specs=[_grp(tm, 8), mspec(0), mspec(1), _VEC, _VEC],
        out_specs=pl.BlockSpec((tm, BW), lambda a: (a, 0)), out_shape=SDS((t_len, BW), BF16), name=f"mem_{tag}")(proj, memkv, memkv, gq, gk)


def _mem_bwd(tag, proj, memkv, gq, gk, dmo):
    t_len = proj.shape[0]
    n_mem = memkv.shape[0]
    tm = _tile(t_len, (256,))

    def body(q_ref, mk_ref, mv_ref, gq_ref, gk_ref, do_ref, dq_ref, dmk_ref, dmv_ref, dgq_ref, dgk_ref):
        _zero_first([dmk_ref, dmv_ref, dgq_ref, dgk_ref])
        _, vjp = jax.vjp(_mem_f, q_ref[...], mk_ref[...], mv_ref[...], gq_ref[...], gk_ref[...])
        dq, dmk, dmv, dgq, dgk = vjp(do_ref[...])
        dq_ref[...] = dq.astype(BF16)
        dmk_ref[...] += dmk
        dmv_ref[...] += dmv
        dgq_ref[...] += dgq
        dgk_ref[...] += dgk

    mspec = lambda j: pl.BlockSpec((n_mem, BW), lambda a, j=j: (0, j))
    mout = pl.BlockSpec((n_mem, BW), lambda a: (0, 0))
    row = pl.BlockSpec((tm, BW), lambda a: (a, 0))
    dq, dmk, dmv, dgq, dgk = pl.pallas_call(
        body, grid=(t_len // tm,), in_specs=[_grp(tm, 8), mspec(0), mspec(1), _VEC, _VEC, row],
        out_specs=[row, mout, mout, _VEC, _VEC],
        out_shape=[SDS((t_len, BW), BF16), SDS((n_mem, BW), F32), SDS((n_mem, BW), F32), SDS((1, BW), F32), SDS((1, BW), F32)],
        name=f"mem_bwd_{tag}")(proj, memkv, memkv, gq, gk, dmo)
    return dq, jnp.concatenate([dmk, dmv], axis=1), dgq, dgk


NA_NJ = 2 * NA_COLS - 1


def _na_onehot():
    q = np.arange(GRID_W)[:, None]
    kc = np.arange(GRID_W)[None, :]
    qwin = np.clip(q - NA_COLS // 2, 0, GRID_W - NA_COLS)
    mask = (kc >= qwin) & (kc < qwin + NA_COLS)
    col = np.clip(kc - q, -(NA_COLS - 1), NA_COLS - 1) + NA_COLS - 1
    onehot = np.zeros((LANES, GRID_W, 2, GRID_W), np.float32)
    qq, kk = np.nonzero(mask)
    for half in range(2):
        onehot[half * NA_NJ + col[qq, kk], qq, half, kk] = 1.0
    valid = np.broadcast_to(mask[:, None, :], (GRID_W, 2, GRID_W)).astype(np.float32)
    return onehot.reshape(LANES, -1), valid.reshape(1, -1)


def _na_pair_rows(rpb):
    n_layers = rpb.shape[0]
    pair = jnp.concatenate([rpb[:, :, :-1], rpb[:, :, 1:]], axis=-1).reshape(n_layers, NH * NA_PAIRS, 2 * NA_NJ)
    return jnp.pad(pair, ((0, 0), (0, 0), (0, LANES - 2 * NA_NJ)))


def _na_bias_table(rpb):
    n_layers = rpb.shape[0]
    onehot, valid = _na_onehot()
    width = onehot.shape[1]

    def body(r_ref, oh_ref, ok_ref, o_ref):
        t = jnp.dot(r_ref[...], oh_ref[...], precision=lax.Precision.HIGHEST, preferred_element_type=F32)
        o_ref[...] = jnp.where(ok_ref[...] > 0, t, NEG_INF)

    out = pl.pallas_call(
        body, grid=(n_layers,),
        in_specs=[pl.BlockSpec((None, NH * NA_PAIRS, LANES), lambda l: (l, 0, 0)), pl.BlockSpec((LANES, width), lambda l: (0, 0)),
                  pl.BlockSpec((1, width), lambda l: (0, 0))],
        out_specs=pl.BlockSpec((None, NH * NA_PAIRS, width), lambda l: (l, 0, 0)),
        out_shape=SDS((n_layers, NH * NA_PAIRS, width), F32), name="na_bias_table")(
            _na_pair_rows(rpb), jnp.asarray(onehot), jnp.asarray(valid))
    return out.reshape(n_layers, NH * NA_PAIRS, GRID_W, 2 * GRID_W)


def _na_bias_grad(dtab):
    n_layers = dtab.shape[0]
    onehot, _ = _na_onehot()
    width = onehot.shape[1]

    def body(x_ref, oh_ref, o_ref):
        o_ref[...] = jnp.dot(x_ref[...], oh_ref[...], precision=lax.Precision.HIGHEST, preferred_element_type=F32)

    out = pl.pallas_call(
        body, grid=(n_layers,),
        in_specs=[pl.BlockSpec((None, NH * NA_PAIRS, width), lambda l: (l, 0, 0)), pl.BlockSpec((width, LANES), lambda l: (0, 0))],
        out_specs=pl.BlockSpec((None, NH * NA_PAIRS, LANES), lambda l: (l, 0, 0)),
        out_shape=SDS((n_layers, NH * NA_PAIRS, LANES), F32), name="na_bias_grad")(
            dtab.reshape(n_layers, NH * NA_PAIRS, width), jnp.asarray(onehot.T.copy()))
    out = out.reshape(n_layers, NH, NA_PAIRS, LANES)
    zero = jnp.zeros((n_layers, NH, 1, NA_NJ), F32)
    return (jnp.concatenate([out[..., :NA_NJ], zero], axis=2) + jnp.concatenate([zero, out[..., NA_NJ:2 * NA_NJ]], axis=2))


def _place():
    x, y, c = lax.axis_index("x"), lax.axis_index("y"), lax.axis_index("c")
    chips = [(1 - x, y), (x, 1 - y), (1 - x, 1 - y)]
    return x, y, c, chips


def _remote(src, dst, ssem, rsem, dev):
    return pltpu.make_async_remote_copy(src_ref=src, dst_ref=dst, send_sem=ssem, recv_sem=rsem, device_id=dev, device_id_type=MESH)


HBM = pl.BlockSpec(memory_space=pltpu.HBM)
SEM = pl.BlockSpec(memory_space=pltpu.SEMAPHORE)
DATAFLOW = pltpu.SideEffectType.DATAFLOW_SIDE_EFFECTING


def _split_copies(name, srcs, lands, n_copies, plan):
    n_s, n_l = len(srcs), len(lands)
    hbm = lambda a: pltpu.HBM(a.shape, a.dtype)

    def start_body(*refs):
        ins = refs[:n_s + n_l]
        outs = refs[n_s + n_l:]
        sems, token = outs[:2 * n_copies], outs[-1]
        for k, (src, dst, dev) in enumerate(plan(ins[:n_s], ins[n_s:])):
            _remote(src, dst, sems[k], sems[n_copies + k], dev).start()
        token[...] = jnp.zeros_like(token)

    outs = pl.pallas_call(
        start_body, name=f"{name}_start",
        out_shape=(pltpu.SemaphoreType.DMA(()),) * (2 * n_copies) + tuple(hbm(a) for a in (*srcs, *lands)) + (SDS((8, LANES), F32),),
        in_specs=(HBM,) * (n_s + n_l), out_specs=(SEM,) * (2 * n_copies) + (HBM,) * (n_s + n_l) + (pl.BlockSpec(memory_space=pltpu.VMEM),),
        input_output_aliases={i: 2 * n_copies + i for i in range(n_s + n_l)},
        compiler_params=pltpu.CompilerParams(has_side_effects=DATAFLOW),
    )(*[pltpu.with_memory_space_constraint(a, pltpu.HBM) for a in (*srcs, *lands)])
    sems, thru, token = outs[:2 * n_copies], outs[2 * n_copies:-1], outs[-1]

    def wait(after_wait):
        def wait_body(*refs):
            ins = refs[:n_s + n_l]
            sem_refs = refs[n_s + n_l:n_s + n_l + 2 * n_copies]
            for k, (src, dst, dev) in enumerate(plan(ins[:n_s], ins[n_s:])):
                cp = _remote(src, dst, sem_refs[k], sem_refs[n_copies + k], dev)
                cp.wait_send()
                cp.wait_recv()

        res = pl.pallas_call(
            wait_body, name=f"{name}_wait", out_shape=tuple(hbm(a) for a in (*srcs, *lands)),
            in_specs=(HBM,) * (n_s + n_l) + (SEM,) * (2 * n_copies) + (ANY,), out_specs=(HBM,) * (n_s + n_l),
            input_output_aliases={i: i for i in range(n_s + n_l)},
            compiler_params=pltpu.CompilerParams(has_side_effects=DATAFLOW),
        )(*thru, *sems, after_wait)
        return list(res[:n_s]), list(res[n_s:])

    return token, wait


def _gather_plan(srcs, lands):
    x, y, c, chips = _place()
    me = 2 * x + y
    out = []
    for src, land in zip(srcs, lands):
        h = src.shape[0] // 2
        out.append((src, land.at[me], (x, y, 1 - c)))
        for px, py in chips:
            out.append((src.at[pl.ds(c * h, h)], land.at[me, pl.ds(c * h, h)], (px, py, c)))
    return out


def _scatter_plan(srcs, lands):
    x, y, c, chips = _place()
    out = []
    for src, land in zip(srcs, lands):
        for r, (px, py) in enumerate(chips):
            out.append((src.at[2 * px + py], land.at[r], (px, py, c)))
    return out


def _pass_plan(srcs, lands):
    x, y, c, chips = _place()
    out = []
    for land in lands:
        h = land.shape[1] // 2
        for px, py in chips:
            piece = land.at[2 * px + py, pl.ds(c * h, h)]
            out.append((piece, piece, (x, y, 1 - c)))
    return out


def _pair_plan(srcs, lands):
    x, y, c, _ = _place()
    out = []
    for src, land in zip(srcs, lands):
        h = land.shape[1]
        out.append((src.at[:, pl.ds((1 - c) * h, h)], land, (x, y, 1 - c)))
    return out


def _half_plan(srcs, lands):
    x, y, c, _ = _place()
    out = []
    for land in lands:
        h = land.shape[0] // 2
        mine = land.at[pl.ds(c * h, h)]
        out.append((mine, mine, (x, y, 1 - c)))
    return out


def _forward_halves(name, lands):
    n = len(lands)

    def body(*refs):
        ins, outs = refs[:n], refs[n:2 * n]
        ssem, rsem = refs[2 * n:]
        x, y, c, chips = _place()
        sib = (x, y, 1 - c)

        def copy(i, r, half):
            h = ins[i].shape[1] // 2
            px, py = chips[r]
            rows = pl.ds(half * h, h)
            return _remote(ins[i].at[2 * px + py, rows], outs[i].at[2 * px + py, rows], ssem.at[i, r], rsem.at[i, r], sib)

        for i in range(n):
            for r in range(3):
                copy(i, r, c).start()
        for i in range(n):
            for r in range(3):
                copy(i, r, 1 - c).wait_recv()
                copy(i, r, c).wait_send()

    return pl.pallas_call(
        body, in_specs=[ANY] * n, out_specs=[ANY] * n, out_shape=[SDS(a.shape, a.dtype) for a in lands],
        input_output_aliases={i: i for i in range(n)},
        scratch_shapes=[pltpu.SemaphoreType.DMA((n, 3)), pltpu.SemaphoreType.DMA((n, 3))], name=name)(*lands)


def _pair_exchange(name, gs):
    n = len(gs)

    def body(*refs):
        ins, outs = refs[:n], refs[n:2 * n]
        ssem, rsem = refs[2 * n:]
        x, y, c, _ = _place()
        cps = []
        for i in range(n):
            h = ins[i].shape[1] // 2
            cps.append(_remote(ins[i].at[:, pl.ds((1 - c) * h, h)], outs[i], ssem.at[i], rsem.at[i], (x, y, 1 - c)))
            cps[-1].start()
        for cp in cps:
            cp.wait()

    return pl.pallas_call(
        body, in_specs=[ANY] * n, out_specs=[ANY] * n,
        out_shape=[SDS((g.shape[0], g.shape[1] // 2, g.shape[2]), g.dtype) for g in gs],
        scratch_shapes=[pltpu.SemaphoreType.DMA((n,)), pltpu.SemaphoreType.DMA((n,))], name=name)(*gs)


def _half_exchange(name, fs):
    n = len(fs)

    def body(*refs):
        ins, outs = refs[:n], refs[n:2 * n]
        ssem, rsem = refs[2 * n:]
        x, y, c, _ = _place()

        def copy(i, half):
            h = ins[i].shape[0] // 2
            return _remote(ins[i].at[pl.ds(half * h, h)], outs[i].at[pl.ds(half * h, h)], ssem.at[i], rsem.at[i], (x, y, 1 - c))

        for i in range(n):
            copy(i, c).start()
        for i in range(n):
            copy(i, 1 - c).wait_recv()
            copy(i, c).wait_send()

    return pl.pallas_call(
        body, in_specs=[ANY] * n, out_specs=[ANY] * n, out_shape=[SDS(f.shape, f.dtype) for f in fs],
        input_output_aliases={i: i for i in range(n)},
        scratch_shapes=[pltpu.SemaphoreType.DMA((n,)), pltpu.SemaphoreType.DMA((n,))], name=name)(*fs)


def _gather_small(v):
    def body(v_ref, o_ref, lsem, ssem, rsem):
        x, y, c, _ = _place()
        me = 4 * x + 2 * y + c
        flips = [(fx, fy, fc) for fx in (0, 1) for fy in (0, 1) for fc in (0, 1)][1:]
        peer = lambda f: (1 - x if f[0] else x, 1 - y if f[1] else y, 1 - c if f[2] else c)
        loc = pltpu.make_async_copy(v_ref, o_ref.at[me], lsem)
        loc.start()
        cps = [_remote(v_ref, o_ref.at[me], ssem.at[k], rsem.at[k], peer(f)) for k, f in enumerate(flips)]
        for cp in cps:
            cp.start()
        for k, f in enumerate(flips):
            px, py, pc = peer(f)
            land = o_ref.at[4 * px + 2 * py + pc]
            _remote(land, land, ssem.at[k], rsem.at[k], peer(f)).wait_recv()
        for cp in cps:
            cp.wait_send()
        loc.wait()

    return pl.pallas_call(
        body, in_specs=[ANY], out_specs=ANY, out_shape=SDS((N_DEV,) + v.shape, v.dtype),
        scratch_shapes=[pltpu.SemaphoreType.DMA, pltpu.SemaphoreType.DMA((N_DEV - 1,)), pltpu.SemaphoreType.DMA((N_DEV - 1,))],
        name="gather_small")(v)


def _rows_tile(r, big=False):
    return _tile(r, ((512,) if big else ()) + (256, 352, 128, 64))


def _pair_sum(name, g, r1, core, me):
    n_l, s_n, h, c = r1.shape
    tr = _rows_tile(h, big=True)
    nb = h // tr

    def body(idx_ref, g_ref, r_ref, pb_ref, own_ref):
        p = g_ref[...] + r_ref[...]
        pb_ref[...] = p.astype(BF16)

        @pl.when(pl.program_id(2) == idx_ref[1])
        def _():
            own_ref[...] = p

    blk = (None, None, tr, c)
    grid_spec = pltpu.PrefetchScalarGridSpec(
        num_scalar_prefetch=1, grid=(n_l, nb, s_n),
        in_specs=[pl.BlockSpec(blk, lambda a, i, s, idx: (a, s, idx[0] * nb + i, 0)), pl.BlockSpec(blk, lambda a, i, s, idx: (a, s, i, 0))],
        out_specs=[pl.BlockSpec(blk, lambda a, i, s, idx: (a, s, i, 0)), pl.BlockSpec((None, tr, c), lambda a, i, s, idx: (a, i, 0))])
    idx = jnp.stack([core, me]).astype(jnp.int32)
    return pl.pallas_call(body, grid_spec=grid_spec, out_shape=[SDS(r1.shape, BF16), SDS((n_l, h, c), F32)], name=name,
                          compiler_params=_cp(VMEM_MM))(idx, g, r1)


def _chip_sum(name, own, r2, core):
    n_l, h, c = own.shape
    tr = _rows_tile(h, big=True)
    nb = h // tr

    def body(idx_ref, o_ref, r_ref, f_ref):
        f_ref[...] = ((o_ref[...] + r_ref[0].astype(F32)) + r_ref[1].astype(F32)) + r_ref[2].astype(F32)

    grid_spec = pltpu.PrefetchScalarGridSpec(
        num_scalar_prefetch=1, grid=(n_l, nb),
        in_specs=[pl.BlockSpec((None, tr, c), lambda a, i, idx: (a, i, 0)), pl.BlockSpec((3, None, tr, c), lambda a, i, idx: (0, a, i, 0))],
        out_specs=pl.BlockSpec((None, tr, c), lambda a, i, idx: (a, idx[0] * nb + i, 0)))
    return pl.pallas_call(body, grid_spec=grid_spec, out_shape=SDS((n_l, 2 * h, c), F32), name=name,
                          compiler_params=_cp(VMEM_MM))(jnp.reshape(core, (1,)).astype(jnp.int32), own, r2)


def _adamw_math(w, g, m, v):
    m = ADAM_B1 * m + (1.0 - ADAM_B1) * g
    v = ADAM_B2 * v + (1.0 - ADAM_B2) * jnp.square(g)
    m_hat = m / (1.0 - ADAM_B1 ** ADAM_STEP)
    v_hat = v / (1.0 - ADAM_B2 ** ADAM_STEP)
    delta = -ADAM_LR * (m_hat / (jnp.sqrt(v_hat) + ADAM_EPS) + ADAM_WD * w)
    return delta, m, v


def _adamw(name, l, w, g, m, v, bufs):
    _, r, c = w.shape
    tr = _rows_tile(r, big=True)

    def body(w_ref, g_ref, m_ref, v_ref, *rest):
        go_ref, d_ref, nm_ref, nv_ref = rest[4:]
        g = g_ref[...]
        go_ref[...] = g
        d_ref[...], nm_ref[...], nv_ref[...] = _adamw_math(w_ref[...], g, m_ref[...], v_ref[...])

    blk = pl.BlockSpec((None, tr, c), lambda i: (l, i, 0))
    return pl.pallas_call(body, grid=(r // tr,), in_specs=[blk, pl.BlockSpec((tr, c), lambda i: (i, 0)), blk, blk] + [ANY] * 4,
                          out_specs=[blk] * 4, out_shape=[SDS(w.shape, F32)] * 4, input_output_aliases={4 + k: k for k in range(4)},
                          name=name, compiler_params=_cp(VMEM_BIG))(w, g, m, v, *bufs)


def _small_update(parts, w, m, v):
    def body(p_ref, w_ref, m_ref, v_ref, g_ref, d_ref, nm_ref, nv_ref):
        g = p_ref[0]
        for d in range(1, N_DEV):
            g = g + p_ref[d]
        g_ref[...] = g
        d_ref[...], nm_ref[...], nv_ref[...] = _adamw_math(w_ref[...], g, m_ref[...], v_ref[...])

    return pl.pallas_call(body, out_shape=[SDS(w.shape, F32)] * 4, name="small_update")(parts, w, m, v)


SMALL = ["norm_mix_g", "norm_mem_g", "ret_decay_fwd", "ret_decay_bwd", "ret_norm_g", "pool_w", "pool_scale", "na_q_norm_g",
         "na_k_norm_g", "na_rpb", "mem_q_norm_g", "mem_k_norm_g", "norm_ffn_g"]
BIG = ["w_in", "w_gate", "w_mem_kv", "w_branch", "w_out", "w_ffn_in", "w_ffn_out"]
GROUPS = (("w_in", "w_gate", "w_mem_kv", "w_branch", "w_out"), ("w_ffn_in", "w_ffn_out"))
REDUCE_GROUPS = (("w_ffn_in", "w_ffn_out"), ("w_gate", "w_branch", "w_out"), ("w_in", "w_mem_kv"))
ORDER = ["norm_mix_g", "norm_mem_g", "w_in", "w_gate", "ret_decay_fwd", "ret_decay_bwd", "ret_norm_g", "pool_w", "pool_scale",
         "na_q_norm_g", "na_k_norm_g", "na_rpb", "mem_q_norm_g", "mem_k_norm_g", "w_mem_kv", "w_branch", "w_out", "norm_ffn_g",
         "w_ffn_in", "w_ffn_out"]


def _pack(arrs):
    rows = []
    for a in arrs:
        f = a.reshape(-1).astype(F32)
        pad = (-f.shape[0]) % (8 * LANES)
        rows.append(jnp.pad(f, (0, pad)).reshape(-1, LANES))
    return jnp.concatenate(rows, axis=0)


def _unpack(p, like):
    out, at = [], 0
    for a in like:
        n = int(np.prod(a.shape))
        rows = -(-n // (8 * LANES)) * 8
        out.append(p[at:at + rows].reshape(-1)[:n].reshape(a.shape))
        at += rows
    return out


def _rope_tables(t_len):
    half = HEAD // 2
    inv = ROPE_THETA ** (-jnp.arange(half, dtype=F32) / half)
    ang = jnp.arange(t_len, dtype=F32)[:, None] * inv[None, :]
    cos, sin = jnp.cos(ang), jnp.sin(ang)
    return jnp.tile(jnp.concatenate([cos, cos], axis=1), (1, NH)), jnp.tile(jnp.concatenate([-sin, sin], axis=1), (1, NH))


def _block_diag(pw):
    out = jnp.zeros((BW, BW), pw.dtype)
    for g in range(NH):
        out = out.at[g * HEAD:(g + 1) * HEAD, g * HEAD:(g + 1) * HEAD].set(pw[g])
    return out


def _fold_heads(v):
    return v.reshape(NH, HEAD).sum(axis=0)


def _fold_lanes(v):
    return v.reshape(NH, HEAD).sum(axis=1)


def kernel(x, mem, norm_mix_g, norm_mem_g, w_in, w_gate, ret_decay_fwd, ret_decay_bwd, ret_norm_g, pool_w, pool_scale, na_q_norm_g, na_k_norm_g, na_rpb, mem_q_norm_g, mem_k_norm_g, w_mem_kv, w_branch, w_out, norm_ffn_g, w_ffn_in, w_ffn_out, loss_target, m_norm_mix_g, m_norm_mem_g, m_w_in, m_w_gate, m_ret_decay_fwd, m_ret_decay_bwd, m_ret_norm_g, m_pool_w, m_pool_scale, m_na_q_norm_g, m_na_k_norm_g, m_na_rpb, m_mem_q_norm_g, m_mem_k_norm_g, m_w_mem_kv, m_w_branch, m_w_out, m_norm_ffn_g, m_w_ffn_in, m_w_ffn_out, v_norm_mix_g, v_norm_mem_g, v_w_in, v_w_gate, v_ret_decay_fwd, v_ret_decay_bwd, v_ret_norm_g, v_pool_w, v_pool_scale, v_na_q_norm_g, v_na_k_norm_g, v_na_rpb, v_mem_q_norm_g, v_mem_k_norm_g, v_w_mem_kv, v_w_branch, v_w_out, v_norm_ffn_g, v_w_ffn_in, v_w_ffn_out):
    args = dict(locals())
    W = {n: args[n] for n in ORDER}
    M = {n: args["m_" + n] for n in ORDER}
    V = {n: args["v_" + n] for n in ORDER}
    n_layers, d_model = norm_mix_g.shape
    assert x.shape[0] == 1
    t_len = x.shape[1]
    xc, yc, cc = lax.axis_index("x"), lax.axis_index("y"), lax.axis_index("c")
    me_chip = 2 * xc + yc

    def as3(a):
        return a.reshape(a.shape[0], -1, a.shape[-1])

    gather_waits, token = {}, jnp.zeros((), F32)
    for l in range(n_layers):
        for gi, names in enumerate(GROUPS):
            srcs = [as3(W[n])[l].astype(BF16) for n in names]
            lands = [lax.empty((N_CHIPS,) + a.shape, BF16) for a in srcs]
            tok, gather_waits[l, gi] = _split_copies(f"gather_l{l}_g{gi}", srcs, lands, 4 * len(names), _gather_plan)
            token = token + tok[0, 0]

    cos, sin = _rope_tables(t_len)
    btabs = _na_bias_table(na_rpb)
    lane = lambda a: a.reshape(1, -1).astype(F32)
    x2d, mem2d, tgt = x[0], mem[0], loss_target[0]
    norm_mix_g = norm_mix_g + token

    def fetch(l, gi, after):
        lands = gather_waits[l, gi](after)[1]
        return _split_copies(f"pass_l{l}_g{gi}", [], lands, 3 * len(GROUPS[gi]), _pass_plan)

    saved = []
    _, passing = fetch(0, 0, x2d)
    for l in range(n_layers):
        tag = f"l{l}"
        gathered = dict(zip(GROUPS[0], passing(x2d)[1]))
        wi = gathered["w_in"].transpose(1, 0, 2).reshape(1, 1, d_model, -1)
        wg = gathered["w_gate"][None]
        wmkv = gathered["w_mem_kv"].reshape(1, 1, d_model, 2 * BW)
        wb = gathered["w_branch"].reshape(1, N_CHIPS, NH, BW, BW)
        wo = gathered["w_out"].reshape(1, 1, d_model, d_model)
        s = {"x": x2d}
        s["decf"], s["decb"] = lane(jnp.repeat(ret_decay_fwd[l], HEAD)), lane(jnp.repeat(ret_decay_bwd[l], HEAD))
        s["gn"], s["pscale"] = lane(ret_norm_g[l]), lane(pool_scale[l])
        s["wbd"] = _block_diag(pool_w[l])
        s["nagq"], s["nagk"] = lane(jnp.tile(na_q_norm_g[l], NH)), lane(jnp.tile(na_k_norm_g[l], NH))
        s["mgq"], s["mgk"] = lane(jnp.tile(mem_q_norm_g[l], NH)), lane(jnp.tile(mem_k_norm_g[l], NH))
        s["btab"] = btabs[l]
        s["h"] = _rms_fwd(f"rms_mix_{tag}", x2d, lane(norm_mix_g[l]))
        s["proj"] = _mm_nn(f"mm_in_{tag}", s["h"], wi, 0)
        s["gp3"] = _mm_nn(f"mm_gate_{tag}", s["h"], wg, 0, out3d=True)
        tok, passing = fetch(l, 1, s["gp3"])
        s["hm"] = _rms_fwd(f"rms_mem_{tag}", mem2d, lane(norm_mem_g[l]) + tok[0, 0])
        s["memkv"] = _mm_nn(f"mm_memkv_{tag}", s["hm"], wmkv, 0)
        ret, s["ret_saved"] = _ret_fwd(tag, s["proj"], cos, sin, s["decf"], s["decb"], s["gn"])
        s["pvp"] = jnp.pad(s["proj"][:, 4 * BW:5 * BW], ((POOL_PAD, POOL_PAD), (0, 0)))
        pool = _pool_fwd(tag, s["pvp"], s["wbd"], s["pscale"], t_len)[POOL_PAD:POOL_PAD + t_len]
        s["kn"], s["vb"] = _na_keys(tag, s["proj"], s["nagk"])
        na = _na_fwd(tag, s["proj"], s["kn"], s["vb"], s["btab"], s["nagq"])
        mo = _mem_fwd(tag, s["proj"], s["memkv"], s["mgq"], s["mgk"])
        s["br"] = jnp.stack([ret, pool, na, mo])
        s["merged"] = _gate_fwd(f"gate_{tag}", s["gp3"], s["br"], wb, 0)
        s["x2"] = _mm_nn(f"mm_out_{tag}", s["merged"], wo, 0, add=x2d)
        gathered = dict(zip(GROUPS[1], passing(s["x2"])[1]))
        pin_ffn = jnp.zeros((), F32)
        if l + 1 < n_layers:
            tok, passing = fetch(l + 1, 0, s["x2"])
            pin_ffn = tok[0, 0]
        wfi = gathered["w_ffn_in"][None]
        wfo = gathered["w_ffn_out"].reshape(1, 1, -1, d_model)
        s["w"] = (wi, wg, wmkv, wb, wo, wfi, wfo)
        s["h2"] = _rms_fwd(f"rms_ffn_{tag}", s["x2"], lane(norm_ffn_g[l]) + pin_ffn)
        s["ag3"] = _mm_nn(f"mm_ffn_in_{tag}", s["h2"], wfi, 0, out3d=True)
        s["u"] = _swiglu_fwd(f"swiglu_{tag}", s["ag3"])
        x2d = _mm_nn(f"mm_ffn_out_{tag}", s["u"], wfo, 0, add=s["x2"])
        saved.append(s)

    loss_part, dx = _loss_kernel("loss", x2d, tgt)

    dbtabs = [None] * n_layers
    gsmall = {n: [None] * n_layers for n in SMALL}
    opt_view = {n: (lambda a, n=n: jnp.swapaxes(as3(a), 1, 2) if n == "w_in" else as3(a)) for n in BIG}
    out_bufs = {n: tuple(lax.empty(opt_view[n](W[n]).shape, F32) for _ in range(4)) for n in BIG}

    def reduce_group(l, gi, grads):
        names = REDUCE_GROUPS[gi]
        gs = [grads[n] for n in names]
        lands = [lax.empty((N_CHIPS, g.shape[1] // 2, g.shape[2]), F32) for g in gs]
        tok0, wait0 = _split_copies(f"pairx_l{l}_g{gi}", gs, lands, len(names), _pair_plan)

        def sums_and_scatter(after):
            mine, theirs = wait0(after)
            sums = [_pair_sum(f"pair_sum_{n}_l{l}", g[None], r[None], cc, me_chip) for n, g, r in zip(names, mine, theirs)]
            pbs = [p[0] for p, _ in sums]
            lands1 = [lax.empty((3,) + p.shape[1:], BF16) for p in pbs]
            tok1, wait1 = _split_copies(f"scatter_l{l}_g{gi}", pbs, lands1, 3 * len(names), _scatter_plan)

            def chip_sums_and_halves(after):
                halves = [_chip_sum(f"chip_sum_{n}_l{l}", own, r[:, None], cc)[0] for n, (_, own), r in zip(names, sums, wait1(after)[1])]
                tok2, wait2 = _split_copies(f"halfx_l{l}_g{gi}", [], halves, len(names), _half_plan)

                def update(after):
                    for n, f in zip(names, wait2(after)[1]):
                        out_bufs[n] = _adamw(f"adamw_{n}_l{l}", l, opt_view[n](W[n]), f, opt_view[n](M[n]), opt_view[n](V[n]), out_bufs[n])
                    return None, None

                return update, tok2

            return chip_sums_and_halves, tok1

        return sums_and_scatter, tok0

    def advance(stages, new):
        nxt, tok = new
        out, toks = [nxt], tok
        for stage in stages:
            nxt, tok = stage(toks)
            if nxt is not None:
                out.append(nxt)
                toks = toks + tok
        return out, toks

    stages, order_after = [], None
    for l in reversed(range(n_layers)):
        tag = f"l{l}"
        s = saved[l]
        wi, wg, wmkv, wb, wo, wfi, wfo = s["w"]
        du = _mm_nt(f"mm_ffn_out_dx_{tag}", dx, wfo, 0, after=order_after)
        g_wfo = _mm_tn(f"mm_ffn_out_dw_{tag}", s["u"], dx, 1)
        dag3 = _swiglu_bwd(f"swiglu_bwd_{tag}", s["ag3"], du)
        dh2 = _mm_nt(f"mm_ffn_in_dx_{tag}", dag3, wfi, 0)
        g_wfi = _mm_tn(f"mm_ffn_in_dw_{tag}", s["h2"], dag3, N_CHIPS)
        dx2, dg = _rms_bwd(f"rms_ffn_bwd_{tag}", s["x2"], lane(norm_ffn_g[l]), [dh2], add=dx)
        gsmall["norm_ffn_g"][l] = dg[0]
        stages, tok = advance(stages, reduce_group(l, 0, {"w_ffn_in": g_wfi, "w_ffn_out": g_wfo.reshape(N_CHIPS, -1, d_model)}))
        dmerged = _mm_nt(f"mm_out_dx_{tag}", dx2, wo, 0, after=tok)
        g_wo = _mm_tn(f"mm_out_dw_{tag}", s["merged"], dx2, 1)
        dgp3, dbr, g_wb = _gate_bwd(f"gate_bwd_{tag}", s["gp3"], s["br"], wb, 0, dmerged)
        g_wg = _mm_tn(f"mm_gate_dw_{tag}", s["h"], dgp3, N_CHIPS)
        stages, tok = advance(stages, reduce_group(l, 1, {
            "w_gate": g_wg, "w_branch": g_wb.reshape(N_CHIPS, NH * BW, BW), "w_out": g_wo.reshape(N_CHIPS, -1, d_model)}))
        dh_a = _mm_nt(f"mm_gate_dx_{tag}", dgp3, wg, 0, after=tok)
        drq, drk, drv, drg, ddf, ddb, dgn = _ret_bwd(tag, s["proj"], cos, sin, s["decf"], s["decb"], s["gn"], s["ret_saved"], dbr[0])
        gsmall["ret_decay_fwd"][l], gsmall["ret_decay_bwd"][l], gsmall["ret_norm_g"][l] = _fold_lanes(ddf), _fold_lanes(ddb), dgn[0]
        dpool_p = jnp.pad(dbr[1], ((POOL_PAD, POOL_PAD), (0, 0)))
        dpvp, dwbd, dps = _pool_bwd(tag, s["pvp"], s["wbd"], s["pscale"], t_len, dpool_p)
        dpv = dpvp[POOL_PAD:POOL_PAD + t_len]
        gsmall["pool_w"][l] = jnp.stack([dwbd[g * HEAD:(g + 1) * HEAD, g * HEAD:(g + 1) * HEAD] for g in range(NH)])
        gsmall["pool_scale"][l] = dps[0]
        dnq, dkn, dnv, dbtabs[l], dgq = _na_bwd(tag, s["proj"], s["kn"], s["vb"], s["btab"], s["nagq"], dbr[2])
        dnk, dnv, dgk = _na_keys_bwd(tag, s["proj"], s["nagk"], dkn, dnv)
        gsmall["na_q_norm_g"][l], gsmall["na_k_norm_g"][l] = _fold_heads(dgq), _fold_heads(dgk)
        dmq, dmemkv, dgq, dgk = _mem_bwd(tag, s["proj"], s["memkv"], s["mgq"], s["mgk"], dbr[3])
        gsmall["mem_q_norm_g"][l], gsmall["mem_k_norm_g"][l] = _fold_heads(dgq), _fold_heads(dgk)
        g_wmkv = _mm_tn(f"mm_memkv_dw_{tag}", s["hm"], dmemkv, 1)
        dhm = _mm_nt(f"mm_memkv_dx_{tag}", dmemkv, wmkv, 0)
        _, dg = _rms_bwd(f"rms_mem_bwd_{tag}", mem2d, lane(norm_mem_g[l]), [dhm])
        gsmall["norm_mem_g"][l] = dg[0]
        dproj = jnp.concatenate([drq, drk, drv, drg, dpv, dnq, dnk, dnv, dmq], axis=1)
        g_wi = _mm_tn(f"mm_in_dw_{tag}", dproj, s["h"], 1)
        dh_b = _mm_nt(f"mm_in_dx_{tag}", dproj, wi, 0)
        dx, dg = _rms_bwd(f"rms_mix_bwd_{tag}", s["x"], lane(norm_mix_g[l]), [dh_a, dh_b], add=dx2)
        gsmall["norm_mix_g"][l] = dg[0]

        stages, order_after = advance(stages, reduce_group(l, 2, {
            "w_in": g_wi.reshape(N_CHIPS, -1, d_model), "w_mem_kv": g_wmkv.reshape(N_CHIPS, -1, 2 * BW)}))
    out_g, out_d, out_m, out_v = {}, {}, {}, {}
    gsmall["na_rpb"] = list(_na_bias_grad(jnp.stack(dbtabs)))
    small_g = [jnp.stack(gsmall[n]).reshape(W[n].shape) for n in SMALL] + [loss_part]
    like = [W[n] for n in SMALL] + [loss_part]
    zero = jnp.zeros((1, 1), F32)
    parts = _gather_small(_pack(small_g))
    packed = _small_update(parts, _pack([W[n] for n in SMALL] + [zero]), _pack([M[n] for n in SMALL] + [zero]),
                           _pack([V[n] for n in SMALL] + [zero]))
    sg, sd, sm, sv = [_unpack(p, like) for p in packed]
    for i, n in enumerate(SMALL):
        out_g[n], out_d[n], out_m[n], out_v[n] = sg[i], sd[i], sm[i], sv[i]
    loss = sg[-1].reshape(())

    while stages:
        stages = [nxt for nxt, _ in (stage(packed[0]) for stage in stages) if nxt is not None]

    for n in BIG:
        shp = W[n].shape
        back = (lambda b: jnp.swapaxes(b, 1, 2)) if n == "w_in" else (lambda b: b)
        out_g[n], out_d[n], out_m[n], out_v[n] = [back(b).reshape(shp) for b in out_bufs[n]]

    return (loss, dx.reshape(x.shape), *[out_g[n] for n in ORDER], *[out_d[n] for n in ORDER],
            *[out_m[n] for n in ORDER], *[out_v[n] for n in ORDER])
```
